```python
import jax, jax.numpy as jnp
from jax import lax
import numpy as np

D_MODEL = 1024
BATCH = 8
SEQ = 2048
DEPTH = 4

CTX_LEN = 256
GRID_W = 64
N_MIXERS = 3
EPS = 1e-6
CONV_WIDTH = 31
HEAD_DIM = 64
N_Q_HEADS = D_MODEL // HEAD_DIM
N_KV_HEADS = N_Q_HEADS // 4
Q_PER_KV = N_Q_HEADS // N_KV_HEADS
WINDOW = 128
ATTN_BLOCK = 128
ROPE_BASE = 10000.0
GMLP_CHUNK = 128
GMLP_WIDTH = 2 * D_MODEL
GMLP_GROUP_DIM = 128
GMLP_GROUPS = GMLP_WIDTH // GMLP_GROUP_DIM
FFN_DIM = 2816
FFN_CONV_WIDTH = 3

kernel_name = "hybrid_interleaved_conv_swa_gmlp_dit"


def rms_norm(x, g):
    xf = x.astype(jnp.float32)
    y = xf * lax.rsqrt(jnp.mean(xf * xf, axis=-1, keepdims=True) + EPS)
    return (y * g.astype(jnp.float32)).astype(x.dtype)


def layer_norm(x, g, b):
    xf = x.astype(jnp.float32)
    mu = jnp.mean(xf, axis=-1, keepdims=True)
    var = jnp.mean(jnp.square(xf - mu), axis=-1, keepdims=True)
    y = (xf - mu) * lax.rsqrt(var + EPS)
    return (y * g.astype(jnp.float32) + b.astype(jnp.float32)).astype(x.dtype)


def depthwise_conv(x, w, b):
    k = w.shape[0]
    half = (k - 1) // 2
    y = lax.conv_general_dilated(
        x, w[:, None, :].astype(x.dtype), window_strides=(1,), padding=[(half, half)],
        dimension_numbers=("NWC", "WIO", "NWC"), feature_group_count=x.shape[-1])
    return y + b


def axial_rope_tables(rows):
    row = jnp.repeat(jnp.arange(rows), GRID_W).astype(jnp.float32)
    col = jnp.tile(jnp.arange(GRID_W), rows).astype(jnp.float32)
    axis_dim = HEAD_DIM // 2
    inv_freq = ROPE_BASE ** (-jnp.arange(0, axis_dim, 2, dtype=jnp.float32) / axis_dim)
    ang_r = row[:, None] * inv_freq[None, :]
    ang_c = col[:, None] * inv_freq[None, :]
    ang = jnp.concatenate([ang_r, ang_r, ang_c, ang_c], axis=-1)
    return jnp.cos(ang), jnp.sin(ang)


def rotate_half(x):
    x1, x2 = jnp.split(x, 2, axis=-1)
    return jnp.concatenate([-x2, x1], axis=-1)


def apply_axial_rope(x, cos, sin):
    xr, xc = jnp.split(x, 2, axis=-1)
    rot = jnp.concatenate([rotate_half(xr), rotate_half(xc)], axis=-1)
    out = x * cos[None, :, None, :] + rot * sin[None, :, None, :]
    return out.astype(x.dtype)


def adaln_params(cond, w, b):
    mod = (jax.nn.silu(cond) @ w + b)[:, None, :]
    return jnp.split(mod, 6, axis=-1)


def conformer_conv(h, w_in, b_in, dw_w, dw_b, ln_g, ln_b, w_out, b_out):
    a, gt = jnp.split(h @ w_in + b_in, 2, axis=-1)
    z = a * jax.nn.sigmoid(gt)
    z = depthwise_conv(z, dw_w, dw_b)
    z = jax.nn.silu(layer_norm(z, ln_g, ln_b))
    return z @ w_out + b_out


def chunk_gmlp(h, w_in, b_in, ln_g, ln_b, w_s, b_s, w_out):
    bsz, length, _ = h.shape
    u, v = jnp.split(jax.nn.gelu(h @ w_in + b_in), 2, axis=-1)
    v = layer_norm(v, ln_g, ln_b)
    v = v.reshape(bsz, length // GMLP_CHUNK, GMLP_CHUNK, GMLP_GROUPS, GMLP_GROUP_DIM)
    s = jnp.einsum("gpq,bnqgc->bnpgc", w_s, v) + b_s.T[None, None, :, :, None]
    s = s.reshape(bsz, length, GMLP_WIDTH)
    return (u * s) @ w_out


def windowed_gqa(h, hc, w_qkv, sink, w_o, cos, sin, ctx_out):
    bsz, length, _ = h.shape
    n_ctx = hc.shape[1]
    nb = length // ATTN_BLOCK
    q_dim = N_Q_HEADS * HEAD_DIM
    kv_dim = N_KV_HEADS * HEAD_DIM
    scale = HEAD_DIM ** -0.5
    neg = jnp.float32(-1e30)

    q, k, v = jnp.split(h @ w_qkv, [q_dim, q_dim + kv_dim], axis=-1)
    q = apply_axial_rope(q.reshape(bsz, length, N_Q_HEADS, HEAD_DIM), cos, sin)
    k = apply_axial_rope(k.reshape(bsz, length, N_KV_HEADS, HEAD_DIM), cos, sin)
    v = v.reshape(bsz, length, N_KV_HEADS, HEAD_DIM)
    if ctx_out:
        qc, kc, vc = jnp.split(hc @ w_qkv, [q_dim, q_dim + kv_dim], axis=-1)
    else:
        kc, vc = jnp.split(hc @ w_qkv[:, q_dim:], 2, axis=-1)
    kc = kc.reshape(bsz, n_ctx, N_KV_HEADS, HEAD_DIM)
    vc = vc.reshape(bsz, n_ctx, N_KV_HEADS, HEAD_DIM)
    sink_f = sink.astype(jnp.float32).reshape(N_KV_HEADS, Q_PER_KV)

    pad = ((0, 0), (ATTN_BLOCK, ATTN_BLOCK), (0, 0), (0, 0))
    kp = jnp.pad(k, pad).reshape(bsz, nb + 2, ATTN_BLOCK, N_KV_HEADS, HEAD_DIM)
    vp = jnp.pad(v, pad).reshape(bsz, nb + 2, ATTN_BLOCK, N_KV_HEADS, HEAD_DIM)
    band = lambda t: jnp.concatenate([t[:, :-2], t[:, 1:-1], t[:, 2:]], axis=2)
    kw, vw = band(kp), band(vp)
    qb = q.reshape(bsz, nb, ATTN_BLOCK, N_KV_HEADS, Q_PER_KV, HEAD_DIM)

    s_win = jnp.einsum("bnqhgd,bnkhd->bnhgqk", qb, kw).astype(jnp.float32) * scale
    s_ctx = jnp.einsum("bnqhgd,bkhd->bnhgqk", qb, kc).astype(jnp.float32) * scale
    q_off = jnp.arange(ATTN_BLOCK)[:, None] + ATTN_BLOCK
    k_off = jnp.arange(3 * ATTN_BLOCK)[None, :]
    in_window = jnp.abs(q_off - k_off) <= WINDOW
    key_abs = (jnp.arange(nb)[:, None] - 1) * ATTN_BLOCK + jnp.arange(3 * ATTN_BLOCK)[None, :]
    valid = (key_abs >= 0) & (key_abs < length)
    mask = in_window[None] & valid[:, None, :]
    s_win = jnp.where(mask[None, :, None, None], s_win, neg)

    sk = sink_f[None, None, :, :, None, None]
    m = jnp.maximum(jnp.maximum(s_win.max(-1, keepdims=True), s_ctx.max(-1, keepdims=True)), sk)
    p_win = jnp.exp(s_win - m)
    p_ctx = jnp.exp(s_ctx - m)
    inv = 1.0 / (p_win.sum(-1, keepdims=True) + p_ctx.sum(-1, keepdims=True) + jnp.exp(sk - m))
    o = (jnp.einsum("bnhgqk,bnkhd->bnqhgd", p_win * inv, vw)
         + jnp.einsum("bnhgqk,bkhd->bnqhgd", p_ctx * inv, vc))
    y = o.astype(h.dtype).reshape(bsz, length, q_dim) @ w_o

    if not ctx_out:
        return y, None
    qc = qc.reshape(bsz, n_ctx, N_KV_HEADS, Q_PER_KV, HEAD_DIM)
    sc = jnp.einsum("bqhgd,bkhd->bhgqk", qc, kc).astype(jnp.float32) * scale
    skc = sink_f[None, :, :, None, None]
    mc = jnp.maximum(sc.max(-1, keepdims=True), skc)
    pc = jnp.exp(sc - mc)
    pc = pc / (pc.sum(-1, keepdims=True) + jnp.exp(skc - mc))
    oc = jnp.einsum("bhgqk,bkhd->bqhgd", pc, vc).astype(hc.dtype)
    yc = oc.reshape(bsz, n_ctx, q_dim) @ w_o
    return y, yc


def conv_ffn(h, w_up, conv_w, conv_b, w_down):
    z = depthwise_conv(h @ w_up, conv_w, conv_b)
    gate, val = jnp.split(z, 2, axis=-1)
    return (jax.nn.silu(gate) * val) @ w_down


def _fwd_setup_inputs(seed: int = 0) -> dict:
    key = jax.random.key(seed)
    keys = jax.random.split(key, 32)
    ks = [keys[i] for i in range(32)]
    counter = [0]

    def nrm(shape, s):
        k = ks[counter[0]]
        counter[0] += 1
        return jax.random.normal(k, shape, jnp.float32) * s

    D = D_MODEL
    n_a = len(range(0, DEPTH, N_MIXERS))
    n_b = len(range(1, DEPTH, N_MIXERS))
    n_c = len(range(2, DEPTH, N_MIXERS))
    qkv_dim = (N_Q_HEADS + 2 * N_KV_HEADS) * HEAD_DIM
    return {
        "x": nrm((BATCH, SEQ, D), 1.0),
        "c": nrm((BATCH, D), 1.0),
        "ctx": nrm((BATCH, CTX_LEN, D), 1.0),
        "c_ctx": nrm((D,), 1.0),
        "ada_w": nrm((DEPTH, D, 6 * D), 0.5 * D ** -0.5),
        "ada_b": nrm((DEPTH, 6 * D), 0.02),
        "norm_g": 1.0 + nrm((DEPTH, 4, D), 0.02),
        "ffn_w_up": nrm((DEPTH, D, 2 * FFN_DIM), D ** -0.5),
        "ffn_conv_w": nrm((DEPTH, FFN_CONV_WIDTH, 2 * FFN_DIM), FFN_CONV_WIDTH ** -0.5),
        "ffn_conv_b": nrm((DEPTH, 2 * FFN_DIM), 0.02),
        "ffn_w_down": nrm((DEPTH, FFN_DIM, D), FFN_DIM ** -0.5),
        "cm_w_in": nrm((n_a, D, 2 * D), D ** -0.5),
        "cm_b_in": nrm((n_a, 2 * D), 0.02),
        "cm_dw_w": nrm((n_a, CONV_WIDTH, D), CONV_WIDTH ** -0.5),
        "cm_dw_b": nrm((n_a, D), 0.02),
        "cm_ln_g": 1.0 + nrm((n_a, D), 0.02),
        "cm_ln_b": nrm((n_a, D), 0.02),
        "cm_w_out": nrm((n_a, D, D), D ** -0.5),
        "cm_b_out": nrm((n_a, D), 0.02),
        "attn_w_qkv": nrm((n_b, D, qkv_dim), D ** -0.5),
        "attn_sink": nrm((n_b, N_Q_HEADS), 0.5),
        "attn_w_o": nrm((n_b, N_Q_HEADS * HEAD_DIM, D), D ** -0.5),
        "gm_w_in": nrm((n_c, D, 2 * GMLP_WIDTH), D ** -0.5),
        "gm_b_in": nrm((n_c, 2 * GMLP_WIDTH), 0.02),
        "gm_ln_g": 1.0 + nrm((n_c, GMLP_WIDTH), 0.02),
        "gm_ln_b": nrm((n_c, GMLP_WIDTH), 0.02),
        "gm_w_s": nrm((n_c, GMLP_GROUPS, GMLP_CHUNK, GMLP_CHUNK), GMLP_CHUNK ** -0.5),
        "gm_b_s": 1.0 + nrm((n_c, GMLP_GROUPS, GMLP_CHUNK), 0.02),
        "gm_w_out": nrm((n_c, GMLP_WIDTH, D), GMLP_WIDTH ** -0.5),
    }


def _fwd_reference(x, c, ctx, c_ctx, ada_w, ada_b, norm_g, ffn_w_up, ffn_conv_w, ffn_conv_b, ffn_w_down,
              cm_w_in, cm_b_in, cm_dw_w, cm_dw_b, cm_ln_g, cm_ln_b, cm_w_out, cm_b_out,
              attn_w_qkv, attn_sink, attn_w_o,
              gm_w_in, gm_b_in, gm_ln_g, gm_ln_b, gm_w_s, gm_b_s, gm_w_out):
    length = x.shape[1]
    ROWS = length // GRID_W
    cos, sin = axial_rope_tables(ROWS)
    ctx_readers = [i for i in range(DEPTH) if i % N_MIXERS == 1]
    last_reader = max(ctx_readers) if ctx_readers else -1

    h, hc = x, ctx
    for i in range(DEPTH):
        kind, j = i % N_MIXERS, i // N_MIXERS
        use_ctx = i <= last_reader
        ctx_full = i < last_reader

        sh1, sc1, g1, sh2, sc2, g2 = adaln_params(c, ada_w[i], ada_b[i])
        a = rms_norm(h, norm_g[i, 0]) * (1.0 + sc1) + sh1
        if use_ctx:
            csh1, csc1, cg1, csh2, csc2, cg2 = adaln_params(c_ctx[None, :], ada_w[i], ada_b[i])
            ac = rms_norm(hc, norm_g[i, 0]) * (1.0 + csc1) + csh1

        if kind == 0:
            cm = (cm_w_in[j], cm_b_in[j], cm_dw_w[j], cm_dw_b[j], cm_ln_g[j], cm_ln_b[j], cm_w_out[j], cm_b_out[j])
            y = conformer_conv(a, *cm)
            yc = conformer_conv(ac, *cm) if ctx_full else None
        elif kind == 1:
            y, yc = windowed_gqa(a, ac, attn_w_qkv[j], attn_sink[j], attn_w_o[j], cos, sin, ctx_full)
        else:
            gm = (gm_w_in[j], gm_b_in[j], gm_ln_g[j], gm_ln_b[j], gm_w_s[j], gm_b_s[j], gm_w_out[j])
            y = chunk_gmlp(a, *gm)
            yc = chunk_gmlp(ac, *gm) if ctx_full else None

        h = h + g1 * rms_norm(y, norm_g[i, 1])
        f = conv_ffn(rms_norm(h, norm_g[i, 2]) * (1.0 + sc2) + sh2,
                     ffn_w_up[i], ffn_conv_w[i], ffn_conv_b[i], ffn_w_down[i])
        h = h + g2 * rms_norm(f, norm_g[i, 3])

        if ctx_full:
            hc = hc + cg1 * rms_norm(yc, norm_g[i, 1])
            fc = conv_ffn(rms_norm(hc, norm_g[i, 2]) * (1.0 + csc2) + csh2,
                          ffn_w_up[i], ffn_conv_w[i], ffn_conv_b[i], ffn_w_down[i])
            hc = hc + cg2 * rms_norm(fc, norm_g[i, 3])
    return h


import jax as _jax
import jax.numpy as _jnp

TWIN_FORMAT = 'train_step'
FWD_PARAMS = ['x', 'c', 'ctx', 'c_ctx', 'ada_w', 'ada_b', 'norm_g', 'ffn_w_up', 'ffn_conv_w', 'ffn_conv_b', 'ffn_w_down', 'cm_w_in', 'cm_b_in', 'cm_dw_w', 'cm_dw_b', 'cm_ln_g', 'cm_ln_b', 'cm_w_out', 'cm_b_out', 'attn_w_qkv', 'attn_sink', 'attn_w_o', 'gm_w_in', 'gm_b_in', 'gm_ln_g', 'gm_ln_b', 'gm_w_s', 'gm_b_s', 'gm_w_out']
TWIN_WEIGHTS = ['c_ctx', 'ada_w', 'ada_b', 'norm_g', 'ffn_w_up', 'ffn_conv_w', 'ffn_conv_b', 'ffn_w_down', 'cm_w_in', 'cm_b_in', 'cm_dw_w', 'cm_dw_b', 'cm_ln_g', 'cm_ln_b', 'cm_w_out', 'cm_b_out', 'attn_w_qkv', 'attn_sink', 'attn_w_o', 'gm_w_in', 'gm_b_in', 'gm_ln_g', 'gm_ln_b', 'gm_w_s', 'gm_b_s', 'gm_w_out']
TWIN_DIFF_INPUT = 'x'
TWIN_INPUTS = ['x', 'c', 'ctx', 'c_ctx', 'ada_w', 'ada_b', 'norm_g', 'ffn_w_up', 'ffn_conv_w', 'ffn_conv_b', 'ffn_w_down', 'cm_w_in', 'cm_b_in', 'cm_dw_w', 'cm_dw_b', 'cm_ln_g', 'cm_ln_b', 'cm_w_out', 'cm_b_out', 'attn_w_qkv', 'attn_sink', 'attn_w_o', 'gm_w_in', 'gm_b_in', 'gm_ln_g', 'gm_ln_b', 'gm_w_s', 'gm_b_s', 'gm_w_out', 'loss_target', 'm_c_ctx', 'm_ada_w', 'm_ada_b', 'm_norm_g', 'm_ffn_w_up', 'm_ffn_conv_w', 'm_ffn_conv_b', 'm_ffn_w_down', 'm_cm_w_in', 'm_cm_b_in', 'm_cm_dw_w', 'm_cm_dw_b', 'm_cm_ln_g', 'm_cm_ln_b', 'm_cm_w_out', 'm_cm_b_out', 'm_attn_w_qkv', 'm_attn_sink', 'm_attn_w_o', 'm_gm_w_in', 'm_gm_b_in', 'm_gm_ln_g', 'm_gm_ln_b', 'm_gm_w_s', 'm_gm_b_s', 'm_gm_w_out', 'v_c_ctx', 'v_ada_w', 'v_ada_b', 'v_norm_g', 'v_ffn_w_up', 'v_ffn_conv_w', 'v_ffn_conv_b', 'v_ffn_w_down', 'v_cm_w_in', 'v_cm_b_in', 'v_cm_dw_w', 'v_cm_dw_b', 'v_cm_ln_g', 'v_cm_ln_b', 'v_cm_w_out', 'v_cm_b_out', 'v_attn_w_qkv', 'v_attn_sink', 'v_attn_w_o', 'v_gm_w_in', 'v_gm_b_in', 'v_gm_ln_g', 'v_gm_ln_b', 'v_gm_w_s', 'v_gm_b_s', 'v_gm_w_out']
TWIN_OUTPUTS = ['loss', 'grad_x', 'grad_c_ctx', 'grad_ada_w', 'grad_ada_b', 'grad_norm_g', 'grad_ffn_w_up', 'grad_ffn_conv_w', 'grad_ffn_conv_b', 'grad_ffn_w_down', 'grad_cm_w_in', 'grad_cm_b_in', 'grad_cm_dw_w', 'grad_cm_dw_b', 'grad_cm_ln_g', 'grad_cm_ln_b', 'grad_cm_w_out', 'grad_cm_b_out', 'grad_attn_w_qkv', 'grad_attn_sink', 'grad_attn_w_o', 'grad_gm_w_in', 'grad_gm_b_in', 'grad_gm_ln_g', 'grad_gm_ln_b', 'grad_gm_w_s', 'grad_gm_b_s', 'grad_gm_w_out', 'delta_c_ctx', 'delta_ada_w', 'delta_ada_b', 'delta_norm_g', 'delta_ffn_w_up', 'delta_ffn_conv_w', 'delta_ffn_conv_b', 'delta_ffn_w_down', 'delta_cm_w_in', 'delta_cm_b_in', 'delta_cm_dw_w', 'delta_cm_dw_b', 'delta_cm_ln_g', 'delta_cm_ln_b', 'delta_cm_w_out', 'delta_cm_b_out', 'delta_attn_w_qkv', 'delta_attn_sink', 'delta_attn_w_o', 'delta_gm_w_in', 'delta_gm_b_in', 'delta_gm_ln_g', 'delta_gm_ln_b', 'delta_gm_w_s', 'delta_gm_b_s', 'delta_gm_w_out', 'new_m_c_ctx', 'new_m_ada_w', 'new_m_ada_b', 'new_m_norm_g', 'new_m_ffn_w_up', 'new_m_ffn_conv_w', 'new_m_ffn_conv_b', 'new_m_ffn_w_down', 'new_m_cm_w_in', 'new_m_cm_b_in', 'new_m_cm_dw_w', 'new_m_cm_dw_b', 'new_m_cm_ln_g', 'new_m_cm_ln_b', 'new_m_cm_w_out', 'new_m_cm_b_out', 'new_m_attn_w_qkv', 'new_m_attn_sink', 'new_m_attn_w_o', 'new_m_gm_w_in', 'new_m_gm_b_in', 'new_m_gm_ln_g', 'new_m_gm_ln_b', 'new_m_gm_w_s', 'new_m_gm_b_s', 'new_m_gm_w_out', 'new_v_c_ctx', 'new_v_ada_w', 'new_v_ada_b', 'new_v_norm_g', 'new_v_ffn_w_up', 'new_v_ffn_conv_w', 'new_v_ffn_conv_b', 'new_v_ffn_w_down', 'new_v_cm_w_in', 'new_v_cm_b_in', 'new_v_cm_dw_w', 'new_v_cm_dw_b', 'new_v_cm_ln_g', 'new_v_cm_ln_b', 'new_v_cm_w_out', 'new_v_cm_b_out', 'new_v_attn_w_qkv', 'new_v_attn_sink', 'new_v_attn_w_o', 'new_v_gm_w_in', 'new_v_gm_b_in', 'new_v_gm_ln_g', 'new_v_gm_ln_b', 'new_v_gm_w_s', 'new_v_gm_b_s', 'new_v_gm_w_out']
TWIN_LEAF_KINDS = {'loss': 'loss', 'grad_x': 'grad_x', 'grad_c_ctx': 'grad_w', 'grad_ada_w': 'grad_w', 'grad_ada_b': 'grad_w', 'grad_norm_g': 'grad_w', 'grad_ffn_w_up': 'grad_w', 'grad_ffn_conv_w': 'grad_w', 'grad_ffn_conv_b': 'grad_w', 'grad_ffn_w_down': 'grad_w', 'grad_cm_w_in': 'grad_w', 'grad_cm_b_in': 'grad_w', 'grad_cm_dw_w': 'grad_w', 'grad_cm_dw_b': 'grad_w', 'grad_cm_ln_g': 'grad_w', 'grad_cm_ln_b': 'grad_w', 'grad_cm_w_out': 'grad_w', 'grad_cm_b_out': 'grad_w', 'grad_attn_w_qkv': 'grad_w', 'grad_attn_sink': 'grad_w', 'grad_attn_w_o': 'grad_w', 'grad_gm_w_in': 'grad_w', 'grad_gm_b_in': 'grad_w', 'grad_gm_ln_g': 'grad_w', 'grad_gm_ln_b': 'grad_w', 'grad_gm_w_s': 'grad_w', 'grad_gm_b_s': 'grad_w', 'grad_gm_w_out': 'grad_w', 'delta_c_ctx': 'delta_w', 'delta_ada_w': 'delta_w', 'delta_ada_b': 'delta_w', 'delta_norm_g': 'delta_w', 'delta_ffn_w_up': 'delta_w', 'delta_ffn_conv_w': 'delta_w', 'delta_ffn_conv_b': 'delta_w', 'delta_ffn_w_down': 'delta_w', 'delta_cm_w_in': 'delta_w', 'delta_cm_b_in': 'delta_w', 'delta_cm_dw_w': 'delta_w', 'delta_cm_dw_b': 'delta_w', 'delta_cm_ln_g': 'delta_w', 'delta_cm_ln_b': 'delta_w', 'delta_cm_w_out': 'delta_w', 'delta_cm_b_out': 'delta_w', 'delta_attn_w_qkv': 'delta_w', 'delta_attn_sink': 'delta_w', 'delta_attn_w_o': 'delta_w', 'delta_gm_w_in': 'delta_w', 'delta_gm_b_in': 'delta_w', 'delta_gm_ln_g': 'delta_w', 'delta_gm_ln_b': 'delta_w', 'delta_gm_w_s': 'delta_w', 'delta_gm_b_s': 'delta_w', 'delta_gm_w_out': 'delta_w', 'new_m_c_ctx': 'new_m', 'new_m_ada_w': 'new_m', 'new_m_ada_b': 'new_m', 'new_m_norm_g': 'new_m', 'new_m_ffn_w_up': 'new_m', 'new_m_ffn_conv_w': 'new_m', 'new_m_ffn_conv_b': 'new_m', 'new_m_ffn_w_down': 'new_m', 'new_m_cm_w_in': 'new_m', 'new_m_cm_b_in': 'new_m', 'new_m_cm_dw_w': 'new_m', 'new_m_cm_dw_b': 'new_m', 'new_m_cm_ln_g': 'new_m', 'new_m_cm_ln_b': 'new_m', 'new_m_cm_w_out': 'new_m', 'new_m_cm_b_out': 'new_m', 'new_m_attn_w_qkv': 'new_m', 'new_m_attn_sink': 'new_m', 'new_m_attn_w_o': 'new_m', 'new_m_gm_w_in': 'new_m', 'new_m_gm_b_in': 'new_m', 'new_m_gm_ln_g': 'new_m', 'new_m_gm_ln_b': 'new_m', 'new_m_gm_w_s': 'new_m', 'new_m_gm_b_s': 'new_m', 'new_m_gm_w_out': 'new_m', 'new_v_c_ctx': 'new_v', 'new_v_ada_w': 'new_v', 'new_v_ada_b': 'new_v', 'new_v_norm_g': 'new_v', 'new_v_ffn_w_up': 'new_v', 'new_v_ffn_conv_w': 'new_v', 'new_v_ffn_conv_b': 'new_v', 'new_v_ffn_w_down': 'new_v', 'new_v_cm_w_in': 'new_v', 'new_v_cm_b_in': 'new_v', 'new_v_cm_dw_w': 'new_v', 'new_v_cm_dw_b': 'new_v', 'new_v_cm_ln_g': 'new_v', 'new_v_cm_ln_b': 'new_v', 'new_v_cm_w_out': 'new_v', 'new_v_cm_b_out': 'new_v', 'new_v_attn_w_qkv': 'new_v', 'new_v_attn_sink': 'new_v', 'new_v_attn_w_o': 'new_v', 'new_v_gm_w_in': 'new_v', 'new_v_gm_b_in': 'new_v', 'new_v_gm_ln_g': 'new_v', 'new_v_gm_ln_b': 'new_v', 'new_v_gm_w_s': 'new_v', 'new_v_gm_b_s': 'new_v', 'new_v_gm_w_out': 'new_v'}


def _forward(args):
    return _fwd_reference(*[args[k] for k in FWD_PARAMS])


def _output_shape():
    out = _jax.eval_shape(lambda: _forward(_fwd_setup_inputs(0)))
    return out.shape, out.dtype

N_MICROBATCH = 1
ADAM_LR = 0.001
ADAM_B1 = 0.9
ADAM_B2 = 0.999
ADAM_EPS = 1e-08
ADAM_WD = 0.01
ADAM_STEP = 10
PER_EXAMPLE_BATCH_AXIS = {'x': 0, 'c': 0, 'ctx': 0, 'loss_target': 0}
SHARED_INPUTS = []
_WEIGHT_DTYPES = {'c_ctx': _jnp.float32, 'ada_w': _jnp.float32, 'ada_b': _jnp.float32, 'norm_g': _jnp.float32, 'ffn_w_up': _jnp.float32, 'ffn_conv_w': _jnp.float32, 'ffn_conv_b': _jnp.float32, 'ffn_w_down': _jnp.float32, 'cm_w_in': _jnp.float32, 'cm_b_in': _jnp.float32, 'cm_dw_w': _jnp.float32, 'cm_dw_b': _jnp.float32, 'cm_ln_g': _jnp.float32, 'cm_ln_b': _jnp.float32, 'cm_w_out': _jnp.float32, 'cm_b_out': _jnp.float32, 'attn_w_qkv': _jnp.float32, 'attn_sink': _jnp.float32, 'attn_w_o': _jnp.float32, 'gm_w_in': _jnp.float32, 'gm_b_in': _jnp.float32, 'gm_ln_g': _jnp.float32, 'gm_ln_b': _jnp.float32, 'gm_w_s': _jnp.float32, 'gm_b_s': _jnp.float32, 'gm_w_out': _jnp.float32}
MOMENT_SCALE = {'c_ctx': 7.035781e-01, 'ada_w': 1.201120e+00, 'ada_b': 2.235392e+00, 'norm_g': 1.471001e+00, 'ffn_w_up': 9.772756e-02, 'ffn_conv_w': 1.060258e-01, 'ffn_conv_b': 2.507294e-01, 'ffn_w_down': 1.938973e-01, 'cm_w_in': 2.222868e-01, 'cm_b_in': 7.130201e-01, 'cm_dw_w': 3.202506e-01, 'cm_dw_b': 1.749244e+00, 'cm_ln_g': 8.016887e-01, 'cm_ln_b': 1.056733e+00, 'cm_w_out': 5.305440e-01, 'cm_b_out': 2.081688e+00, 'attn_w_qkv': 9.229239e-01, 'attn_sink': 1.357059e-02, 'attn_w_o': 1.174578e+00, 'gm_w_in': 1.072976e-01, 'gm_b_in': 2.733246e-01, 'gm_ln_g': 4.124972e-02, 'gm_ln_b': 4.382100e-02, 'gm_w_s': 3.925090e-02, 'gm_b_s': 3.929086e-02, 'gm_w_out': 3.867566e-01}


def _to_microbatches(a, axis):
    t = _jnp.moveaxis(a, axis, 0)
    t = t.reshape((N_MICROBATCH, t.shape[0] // N_MICROBATCH) + t.shape[1:])
    return _jnp.moveaxis(t, 1, axis + 1)


def setup_inputs(seed: int = 0) -> dict:
    inp = _fwd_setup_inputs(seed)
    key = _jax.random.fold_in(_jax.random.key(seed), 7919)
    shape, _ = _output_shape()
    out = dict(inp)
    out["loss_target"] = _jax.random.normal(_jax.random.fold_in(key, 0), shape, _jnp.float32)
    for i, name in enumerate(TWIN_WEIGHTS):
        w = inp[name].astype(_jnp.float32)
        if MOMENT_SCALE is None:
            s = _jnp.sqrt(_jnp.mean(_jnp.square(w)) + 1e-30)
        else:
            s = MOMENT_SCALE[name]
        km, kv = _jax.random.split(_jax.random.fold_in(key, i + 1))
        out[name] = w
        out["m_" + name] = s * _jax.random.normal(km, w.shape, _jnp.float32)
        out["v_" + name] = (s * s) * _jax.random.uniform(kv, w.shape, _jnp.float32, 0.5, 1.5)
    if N_MICROBATCH > 1:
        for name, axis in PER_EXAMPLE_BATCH_AXIS.items():
            out[name] = _to_microbatches(out[name], axis)
    return {'x': out['x'], 'c': out['c'], 'ctx': out['ctx'], 'c_ctx': out['c_ctx'], 'ada_w': out['ada_w'], 'ada_b': out['ada_b'], 'norm_g': out['norm_g'], 'ffn_w_up': out['ffn_w_up'], 'ffn_conv_w': out['ffn_conv_w'], 'ffn_conv_b': out['ffn_conv_b'], 'ffn_w_down': out['ffn_w_down'], 'cm_w_in': out['cm_w_in'], 'cm_b_in': out['cm_b_in'], 'cm_dw_w': out['cm_dw_w'], 'cm_dw_b': out['cm_dw_b'], 'cm_ln_g': out['cm_ln_g'], 'cm_ln_b': out['cm_ln_b'], 'cm_w_out': out['cm_w_out'], 'cm_b_out': out['cm_b_out'], 'attn_w_qkv': out['attn_w_qkv'], 'attn_sink': out['attn_sink'], 'attn_w_o': out['attn_w_o'], 'gm_w_in': out['gm_w_in'], 'gm_b_in': out['gm_b_in'], 'gm_ln_g': out['gm_ln_g'], 'gm_ln_b': out['gm_ln_b'], 'gm_w_s': out['gm_w_s'], 'gm_b_s': out['gm_b_s'], 'gm_w_out': out['gm_w_out'], 'loss_target': out['loss_target'], 'm_c_ctx': out['m_c_ctx'], 'm_ada_w': out['m_ada_w'], 'm_ada_b': out['m_ada_b'], 'm_norm_g': out['m_norm_g'], 'm_ffn_w_up': out['m_ffn_w_up'], 'm_ffn_conv_w': out['m_ffn_conv_w'], 'm_ffn_conv_b': out['m_ffn_conv_b'], 'm_ffn_w_down': out['m_ffn_w_down'], 'm_cm_w_in': out['m_cm_w_in'], 'm_cm_b_in': out['m_cm_b_in'], 'm_cm_dw_w': out['m_cm_dw_w'], 'm_cm_dw_b': out['m_cm_dw_b'], 'm_cm_ln_g': out['m_cm_ln_g'], 'm_cm_ln_b': out['m_cm_ln_b'], 'm_cm_w_out': out['m_cm_w_out'], 'm_cm_b_out': out['m_cm_b_out'], 'm_attn_w_qkv': out['m_attn_w_qkv'], 'm_attn_sink': out['m_attn_sink'], 'm_attn_w_o': out['m_attn_w_o'], 'm_gm_w_in': out['m_gm_w_in'], 'm_gm_b_in': out['m_gm_b_in'], 'm_gm_ln_g': out['m_gm_ln_g'], 'm_gm_ln_b': out['m_gm_ln_b'], 'm_gm_w_s': out['m_gm_w_s'], 'm_gm_b_s': out['m_gm_b_s'], 'm_gm_w_out': out['m_gm_w_out'], 'v_c_ctx': out['v_c_ctx'], 'v_ada_w': out['v_ada_w'], 'v_ada_b': out['v_ada_b'], 'v_norm_g': out['v_norm_g'], 'v_ffn_w_up': out['v_ffn_w_up'], 'v_ffn_conv_w': out['v_ffn_conv_w'], 'v_ffn_conv_b': out['v_ffn_conv_b'], 'v_ffn_w_down': out['v_ffn_w_down'], 'v_cm_w_in': out['v_cm_w_in'], 'v_cm_b_in': out['v_cm_b_in'], 'v_cm_dw_w': out['v_cm_dw_w'], 'v_cm_dw_b': out['v_cm_dw_b'], 'v_cm_ln_g': out['v_cm_ln_g'], 'v_cm_ln_b': out['v_cm_ln_b'], 'v_cm_w_out': out['v_cm_w_out'], 'v_cm_b_out': out['v_cm_b_out'], 'v_attn_w_qkv': out['v_attn_w_qkv'], 'v_attn_sink': out['v_attn_sink'], 'v_attn_w_o': out['v_attn_w_o'], 'v_gm_w_in': out['v_gm_w_in'], 'v_gm_b_in': out['v_gm_b_in'], 'v_gm_ln_g': out['v_gm_ln_g'], 'v_gm_ln_b': out['v_gm_ln_b'], 'v_gm_w_s': out['v_gm_w_s'], 'v_gm_b_s': out['v_gm_b_s'], 'v_gm_w_out': out['v_gm_w_out']}


def _loss(weights, diff, rest, loss_target):
    with _jax.named_scope("forward"):
        args = {**rest, TWIN_DIFF_INPUT: diff, **{k: w.astype(_WEIGHT_DTYPES[k]) for k, w in weights.items()}}
        y = _forward(args)
    with _jax.named_scope("loss_head"):
        err = _jnp.square(y.astype(_jnp.float32) - loss_target)
        return 0.5 * _jnp.sum(_jnp.mean(err, axis=-1)) if err.ndim else 0.5 * err


def _adamw(w, g, m, v):
    m = ADAM_B1 * m + (1.0 - ADAM_B1) * g
    v = ADAM_B2 * v + (1.0 - ADAM_B2) * _jnp.square(g)
    m_hat = m / (1.0 - ADAM_B1 ** ADAM_STEP)
    v_hat = v / (1.0 - ADAM_B2 ** ADAM_STEP)
    delta = -ADAM_LR * (m_hat / (_jnp.sqrt(v_hat) + ADAM_EPS) + ADAM_WD * w)
    return delta, m, v


def reference(x, c, ctx, c_ctx, ada_w, ada_b, norm_g, ffn_w_up, ffn_conv_w, ffn_conv_b, ffn_w_down, cm_w_in, cm_b_in, cm_dw_w, cm_dw_b, cm_ln_g, cm_ln_b, cm_w_out, cm_b_out, attn_w_qkv, attn_sink, attn_w_o, gm_w_in, gm_b_in, gm_ln_g, gm_ln_b, gm_w_s, gm_b_s, gm_w_out, loss_target, m_c_ctx, m_ada_w, m_ada_b, m_norm_g, m_ffn_w_up, m_ffn_conv_w, m_ffn_conv_b, m_ffn_w_down, m_cm_w_in, m_cm_b_in, m_cm_dw_w, m_cm_dw_b, m_cm_ln_g, m_cm_ln_b, m_cm_w_out, m_cm_b_out, m_attn_w_qkv, m_attn_sink, m_attn_w_o, m_gm_w_in, m_gm_b_in, m_gm_ln_g, m_gm_ln_b, m_gm_w_s, m_gm_b_s, m_gm_w_out, v_c_ctx, v_ada_w, v_ada_b, v_norm_g, v_ffn_w_up, v_ffn_conv_w, v_ffn_conv_b, v_ffn_w_down, v_cm_w_in, v_cm_b_in, v_cm_dw_w, v_cm_dw_b, v_cm_ln_g, v_cm_ln_b, v_cm_w_out, v_cm_b_out, v_attn_w_qkv, v_attn_sink, v_attn_w_o, v_gm_w_in, v_gm_b_in, v_gm_ln_g, v_gm_ln_b, v_gm_w_s, v_gm_b_s, v_gm_w_out):
    given = dict(x=x, c=c, ctx=ctx, c_ctx=c_ctx, ada_w=ada_w, ada_b=ada_b, norm_g=norm_g, ffn_w_up=ffn_w_up, ffn_conv_w=ffn_conv_w, ffn_conv_b=ffn_conv_b, ffn_w_down=ffn_w_down, cm_w_in=cm_w_in, cm_b_in=cm_b_in, cm_dw_w=cm_dw_w, cm_dw_b=cm_dw_b, cm_ln_g=cm_ln_g, cm_ln_b=cm_ln_b, cm_w_out=cm_w_out, cm_b_out=cm_b_out, attn_w_qkv=attn_w_qkv, attn_sink=attn_sink, attn_w_o=attn_w_o, gm_w_in=gm_w_in, gm_b_in=gm_b_in, gm_ln_g=gm_ln_g, gm_ln_b=gm_ln_b, gm_w_s=gm_w_s, gm_b_s=gm_b_s, gm_w_out=gm_w_out, loss_target=loss_target, m_c_ctx=m_c_ctx, m_ada_w=m_ada_w, m_ada_b=m_ada_b, m_norm_g=m_norm_g, m_ffn_w_up=m_ffn_w_up, m_ffn_conv_w=m_ffn_conv_w, m_ffn_conv_b=m_ffn_conv_b, m_ffn_w_down=m_ffn_w_down, m_cm_w_in=m_cm_w_in, m_cm_b_in=m_cm_b_in, m_cm_dw_w=m_cm_dw_w, m_cm_dw_b=m_cm_dw_b, m_cm_ln_g=m_cm_ln_g, m_cm_ln_b=m_cm_ln_b, m_cm_w_out=m_cm_w_out, m_cm_b_out=m_cm_b_out, m_attn_w_qkv=m_attn_w_qkv, m_attn_sink=m_attn_sink, m_attn_w_o=m_attn_w_o, m_gm_w_in=m_gm_w_in, m_gm_b_in=m_gm_b_in, m_gm_ln_g=m_gm_ln_g, m_gm_ln_b=m_gm_ln_b, m_gm_w_s=m_gm_w_s, m_gm_b_s=m_gm_b_s, m_gm_w_out=m_gm_w_out, v_c_ctx=v_c_ctx, v_ada_w=v_ada_w, v_ada_b=v_ada_b, v_norm_g=v_norm_g, v_ffn_w_up=v_ffn_w_up, v_ffn_conv_w=v_ffn_conv_w, v_ffn_conv_b=v_ffn_conv_b, v_ffn_w_down=v_ffn_w_down, v_cm_w_in=v_cm_w_in, v_cm_b_in=v_cm_b_in, v_cm_dw_w=v_cm_dw_w, v_cm_dw_b=v_cm_dw_b, v_cm_ln_g=v_cm_ln_g, v_cm_ln_b=v_cm_ln_b, v_cm_w_out=v_cm_w_out, v_cm_b_out=v_cm_b_out, v_attn_w_qkv=v_attn_w_qkv, v_attn_sink=v_attn_sink, v_attn_w_o=v_attn_w_o, v_gm_w_in=v_gm_w_in, v_gm_b_in=v_gm_b_in, v_gm_ln_g=v_gm_ln_g, v_gm_ln_b=v_gm_ln_b, v_gm_w_s=v_gm_w_s, v_gm_b_s=v_gm_b_s, v_gm_w_out=v_gm_w_out)
    weights = {n: given[n] for n in TWIN_WEIGHTS}
    shared = {n: given[n] for n in SHARED_INPUTS}
    per_example = {n: given[n] for n in ['x', 'c', 'ctx']}
    grad_fn = _jax.value_and_grad(_loss, argnums=(0, 1))

    def one_microbatch(ex, loss_target):
        ex = dict(ex)
        diff = ex.pop(TWIN_DIFF_INPUT)
        return grad_fn(weights, diff, {**shared, **ex}, loss_target)

    if N_MICROBATCH == 1:
        loss, (grad_w, grad_x) = one_microbatch(per_example, given["loss_target"])
    else:
        def body(carry, xs):
            loss_sum, grad_sum = carry
            l_k, (gw_k, gx_k) = one_microbatch(xs[0], xs[1])
            with _jax.named_scope("update"):
                return (loss_sum + l_k, _jax.tree.map(_jnp.add, grad_sum, gw_k)), gx_k

        init = (_jnp.zeros((), _jnp.float32), _jax.tree.map(_jnp.zeros_like, weights))
        (loss, grad_w), grad_x = _jax.lax.scan(body, init, (per_example, given["loss_target"]))
    with _jax.named_scope("update"):
        delta_w, new_m, new_v = {}, {}, {}
        for n in TWIN_WEIGHTS:
            delta_w[n], new_m[n], new_v[n] = _adamw(weights[n], grad_w[n], given["m_" + n], given["v_" + n])
    return (loss, grad_x, *[grad_w[n] for n in TWIN_WEIGHTS], *[delta_w[n] for n in TWIN_WEIGHTS],
            *[new_m[n] for n in TWIN_WEIGHTS], *[new_v[n] for n in TWIN_WEIGHTS])
```

```python
import functools
import math

import jax
import jax.numpy as jnp
from jax import lax
from jax.experimental import pallas as pl
from jax.experimental.pallas import tpu as pltpu

F32 = jnp.float32
MMT = jnp.bfloat16
SDS = jax.ShapeDtypeStruct
MESH = pl.DeviceIdType.MESH

EPS = 1e-6
HEAD_DIM = 64
Q_PER_KV = 4
ATTN_BLOCK = 128
GRID_W = 64
ROPE_BASE = 10000.0
GMLP_CHUNK = 128
GMLP_GROUP_DIM = 128
CONV_WIDTH = 31
FFN_CONV_WIDTH = 3
NEG = -1e30

ADAM_LR, ADAM_B1, ADAM_B2, ADAM_EPS, ADAM_WD, ADAM_STEP = 0.001, 0.9, 0.999, 1e-08, 0.01, 10

LANES = 128
SUBLANES = 8
VMEM_LIMIT = 52 * 1024 * 1024
CONV_ROWS = 128
N_CHIPS = 4
N_DEV = 8


def _cparams(*sem):
    return pltpu.CompilerParams(dimension_semantics=sem if sem else None, vmem_limit_bytes=VMEM_LIMIT)


def _tile(n, cap, mult=LANES):
    best = None
    for d in range(mult, min(n, cap) + 1, mult):
        if n % d == 0:
            best = d
    return best if best is not None else n


def _sum0(v):
    return jnp.sum(v, axis=0, keepdims=True)


def _rms(v):
    r = lax.rsqrt(jnp.mean(v * v, axis=-1, keepdims=True) + EPS)
    return v * r, r


def _sig(v):
    return jax.nn.sigmoid(v)


def _dot(a, b, ca, cb):
    return lax.dot_general(a.astype(MMT), b.astype(MMT), (((ca,), (cb,)), ((), ())), preferred_element_type=F32)


def _mm(a, b, mode, out_dtype, name):
    if mode == "nn":
        (M, K), N = a.shape, b.shape[1]
    elif mode == "nt":
        (M, K), N = a.shape, b.shape[0]
    else:
        (K, M), N = a.shape, b.shape[1]
    tm, tn, tk = _tile(M, 512), _tile(N, 1408), _tile(K, 1536)
    nk = K // tk
    ca, cb = {"nn": (1, 0), "nt": (1, 1), "tn": (0, 0)}[mode]

    def body(a_ref, b_ref, o_ref, acc):
        k = pl.program_id(2)

        @pl.when(k == 0)
        def _():
            acc[...] = jnp.zeros_like(acc)

        acc[...] += _dot(a_ref[...], b_ref[...], ca, cb)

        @pl.when(k == nk - 1)
        def _():
            o_ref[...] = acc[...].astype(o_ref.dtype)

    a_spec = pl.BlockSpec((tk, tm), lambda i, j, k: (k, i)) if mode == "tn" else pl.BlockSpec((tm, tk), lambda i, j, k: (i, k))
    b_spec = pl.BlockSpec((tn, tk), lambda i, j, k: (j, k)) if mode == "nt" else pl.BlockSpec((tk, tn), lambda i, j, k: (k, j))
    return pl.pallas_call(
        body, name=name, grid=(M // tm, N // tn, nk), in_specs=[a_spec, b_spec],
        out_specs=pl.BlockSpec((tm, tn), lambda i, j, k: (i, j)), out_shape=SDS((M, N), out_dtype),
        scratch_shapes=[pltpu.VMEM((tm, tn), F32)], compiler_params=_cparams("parallel", "parallel", "arbitrary"))(a, b)


def _seg_of(nl, nseg):
    return (lambda i: jnp.where(i >= nl, 1, 0)) if nseg == 2 else (lambda i: 0)


def _prenorm(h, mod, gn, which, rows, nl, tm, name):
    D = h.shape[1]
    nseg = mod.shape[0]
    seg = _seg_of(nl, nseg)
    sh_i, sc_i = (0, 1) if which == 0 else (3, 4)

    def body(h_ref, mod_ref, gn_ref, a_ref):
        n, _ = _rms(h_ref[...])
        a_ref[...] = (n * gn_ref[...] * (1.0 + mod_ref[pl.ds(sc_i, 1), :]) + mod_ref[pl.ds(sh_i, 1), :]).astype(a_ref.dtype)

    return pl.pallas_call(
        body, name=name, grid=(rows // tm,),
        in_specs=[pl.BlockSpec((tm, D), lambda i: (i, 0)), pl.BlockSpec((None, 6, D), lambda i: (seg(i), 0, 0)),
                  pl.BlockSpec((1, D), lambda i: (0, 0))],
        out_specs=pl.BlockSpec((tm, D), lambda i: (i, 0)), out_shape=SDS((rows, D), MMT),
        compiler_params=_cparams("parallel"))(h, mod, gn)


def _acc_spec(D, seg):
    return pl.BlockSpec((None, 1, D), lambda i: (seg(i), 0, 0))


def _prenorm_bwd(h, da, dh_in, mod, gn, which, rows, nl, tm, name):
    D = h.shape[1]
    nseg = mod.shape[0]
    seg = _seg_of(nl, nseg)
    sc_i = 1 if which == 0 else 4

    def body(h_ref, da_ref, dhin_ref, mod_ref, gn_ref, dh_ref, dsh_ref, dsc_ref, dgn_ref):
        i = pl.program_id(0)
        first = (i == 0) | (i == nl) if nseg == 2 else (i == 0)

        @pl.when(first)
        def _():
            dsh_ref[...] = jnp.zeros_like(dsh_ref)
            dsc_ref[...] = jnp.zeros_like(dsc_ref)
            dgn_ref[...] = jnp.zeros_like(dgn_ref)

        n, r = _rms(h_ref[...])
        da_v = da_ref[...].astype(F32)
        gn_v = gn_ref[...]
        sc1 = 1.0 + mod_ref[pl.ds(sc_i, 1), :]
        dsh_ref[...] += _sum0(da_v)
        dsc_ref[...] += _sum0(da_v * (n * gn_v))
        dgn_ref[...] += _sum0(da_v * n * sc1)
        dn = da_v * (gn_v * sc1)
        dh_ref[...] = dhin_ref[...] + r * (dn - n * jnp.mean(dn * n, axis=-1, keepdims=True))

    row = pl.BlockSpec((tm, D), lambda i: (i, 0))
    acc = SDS((nseg, 1, D), F32)
    return pl.pallas_call(
        body, name=name, grid=(rows // tm,),
        in_specs=[row, row, row, pl.BlockSpec((None, 6, D), lambda i: (seg(i), 0, 0)), pl.BlockSpec((1, D), lambda i: (0, 0))],
        out_specs=[row, _acc_spec(D, seg), _acc_spec(D, seg), _acc_spec(D, seg)],
        out_shape=[SDS((rows, D), F32), acc, acc, acc], compiler_params=_cparams("arbitrary"))(h, da, dh_in, mod, gn)


def _postnorm(h, y, bias, mod, gn, gate_i, rows, nl, tm, name):
    D = h.shape[1]
    nseg = mod.shape[0]
    seg = _seg_of(nl, nseg)

    def body(h_ref, y_ref, b_ref, mod_ref, gn_ref, o_ref):
        ny, _ = _rms(y_ref[...] + b_ref[...])
        o_ref[...] = h_ref[...] + mod_ref[pl.ds(gate_i, 1), :] * (ny * gn_ref[...])

    row = pl.BlockSpec((tm, D), lambda i: (i, 0))
    vec = pl.BlockSpec((1, D), lambda i: (0, 0))
    return pl.pallas_call(
        body, name=name, grid=(rows // tm,),
        in_specs=[row, row, vec, pl.BlockSpec((None, 6, D), lambda i: (seg(i), 0, 0)), vec],
        out_specs=row, out_shape=SDS((rows, D), F32), compiler_params=_cparams("parallel"))(h, y, bias, mod, gn)


def _postnorm_bwd(dh, y, bias, mod, gn, gate_i, rows, nl, tm, name):
    D = y.shape[1]
    nseg = mod.shape[0]
    seg = _seg_of(nl, nseg)

    def body(dh_ref, y_ref, b_ref, mod_ref, gn_ref, dy_ref, dg_ref, dgn_ref, db_ref):
        i = pl.program_id(0)
        first = (i == 0) | (i == nl) if nseg == 2 else (i == 0)

        @pl.when(first)
        def _():
            dg_ref[...] = jnp.zeros_like(dg_ref)
            dgn_ref[...] = jnp.zeros_like(dgn_ref)
            db_ref[...] = jnp.zeros_like(db_ref)

        ny, ry = _rms(y_ref[...] + b_ref[...])
        g = mod_ref[pl.ds(gate_i, 1), :]
        gn_v = gn_ref[...]
        dh_v = dh_ref[...]
        dg_ref[...] += _sum0(dh_v * (ny * gn_v))
        dgn_ref[...] += _sum0(dh_v * ny * g)
        dny = dh_v * (g * gn_v)
        dy = ry * (dny - ny * jnp.mean(dny * ny, axis=-1, keepdims=True))
        db_ref[...] += _sum0(dy)
        dy_ref[...] = dy.astype(dy_ref.dtype)

    row = pl.BlockSpec((tm, D), lambda i: (i, 0))
    vec = pl.BlockSpec((1, D), lambda i: (0, 0))
    acc = SDS((nseg, 1, D), F32)
    return pl.pallas_call(
        body, name=name, grid=(rows // tm,),
        in_specs=[row, row, vec, pl.BlockSpec((None, 6, D), lambda i: (seg(i), 0, 0)), vec],
        out_specs=[row, _acc_spec(D, seg), _acc_spec(D, seg), _acc_spec(D, seg)],
        out_shape=[SDS((rows, D), MMT), acc, acc, acc], compiler_params=_cparams("arbitrary"))(dh, y, bias, mod, gn)


def _seg_layout(segs, H):
    out, base = [], H
    for s0, n in segs:
        out.append((s0, n, base))
        base += n + H
    return out, base


def _zero_pads(ref, lay, H):
    width = ref.shape[1]
    ref[pl.ds(0, H), :] = jnp.zeros((H, width), ref.dtype)
    for _, n, base in lay:
        ref[pl.ds(base + n, H), :] = jnp.zeros((H, width), ref.dtype)


def _window(ref, base, off, H):
    return ref[pl.ds(base - H + off, CONV_ROWS + 2 * H), :]


def _taps(win, H, offs):
    W = CONV_ROWS + 2 * H
    rolled, out = {}, {}
    for o in offs:
        s = H + o
        b = s % SUBLANES
        if b not in rolled:
            rolled[b] = win if b == 0 else pltpu.roll(win, shift=W - b, axis=0)
        out[o] = rolled[b][s - b:s - b + CONV_ROWS, :]
    return out


def _chunks(lay, fn):
    for s0, n, base in lay:
        def step(r, carry, s0=s0, base=base):
            fn(s0, base, pl.multiple_of(r * CONV_ROWS, CONV_ROWS))
            return carry
        lax.fori_loop(0, n // CONV_ROWS, step, 0)


def _ffn_gate(z0, conv_w, conv_b, segs, name):
    T, F2 = z0.shape
    F = F2 // 2
    tc = _tile(F, 256)
    nF = F // tc
    H = SUBLANES
    lay, srows = _seg_layout(segs, H)
    offs = [-1, 0, 1]

    def body(zg_ref, zv_ref, wg_ref, wv_ref, bg_ref, bv_ref, u_ref, xg, xv):
        _zero_pads(xg, lay, H)
        _zero_pads(xv, lay, H)
        for s0, n, base in lay:
            xg[pl.ds(base, n), :] = zg_ref[pl.ds(s0, n), :]
            xv[pl.ds(base, n), :] = zv_ref[pl.ds(s0, n), :]

        def chunk(s0, base, off):
            tg = _taps(_window(xg, base, off, H), H, offs)
            tv = _taps(_window(xv, base, off, H), H, offs)
            zg = bg_ref[...] + sum(tg[k - 1] * wg_ref[pl.ds(k, 1), :] for k in range(3))
            zv = bv_ref[...] + sum(tv[k - 1] * wv_ref[pl.ds(k, 1), :] for k in range(3))
            u_ref[pl.ds(s0 + off, CONV_ROWS), :] = (zg * _sig(zg) * zv).astype(u_ref.dtype)

        _chunks(lay, chunk)

    colg = lambda r: pl.BlockSpec((r, tc), lambda j: (0, j))
    colv = lambda r: pl.BlockSpec((r, tc), lambda j: (0, j + nF))
    return pl.pallas_call(
        body, name=name, grid=(nF,),
        in_specs=[colg(T), colv(T), colg(3), colv(3), colg(1), colv(1)],
        out_specs=colg(T), out_shape=SDS((T, F), MMT),
        scratch_shapes=[pltpu.VMEM((srows, tc), F32), pltpu.VMEM((srows, tc), F32)],
        compiler_params=_cparams("parallel"))(z0, z0, conv_w, conv_w, conv_b, conv_b)


def _ffn_gate_bwd(z0, du, conv_w, conv_b, segs, name):
    T, F2 = z0.shape
    F = F2 // 2
    tc = _tile(F, 256)
    nF = F // tc
    H = SUBLANES
    lay, srows = _seg_layout(segs, H)
    offs = [-1, 0, 1]

    def body(zo_ref, zt_ref, du_ref, wo_ref, wt_ref, bo_ref, bt_ref, u_ref, dz0_ref, dw_ref, db_ref, xo, xt, dzp):
        own_is_gate = pl.program_id(1) == 0
        for ref in (xo, xt, dzp):
            _zero_pads(ref, lay, H)
        for s0, n, base in lay:
            xo[pl.ds(base, n), :] = zo_ref[pl.ds(s0, n), :]
            xt[pl.ds(base, n), :] = zt_ref[pl.ds(s0, n), :]

        def grads(s0, base, off):
            to = _taps(_window(xo, base, off, H), H, offs)
            tt = _taps(_window(xt, base, off, H), H, offs)
            zo = bo_ref[...] + sum(to[k - 1] * wo_ref[pl.ds(k, 1), :] for k in range(3))
            zt = bt_ref[...] + sum(tt[k - 1] * wt_ref[pl.ds(k, 1), :] for k in range(3))
            so, st = _sig(zo), _sig(zt)
            du_v = du_ref[pl.ds(s0 + off, CONV_ROWS), :]
            d_gate = du_v * zt * (so * (1.0 + zo * (1.0 - so)))
            d_val = du_v * (zt * st)
            dzp[pl.ds(base + off, CONV_ROWS), :] = jnp.where(own_is_gate, d_gate, d_val)

            @pl.when(own_is_gate)
            def _():
                u_ref[pl.ds(s0 + off, CONV_ROWS), :] = (zo * so * zt).astype(u_ref.dtype)

        _chunks(lay, grads)
        dw_ref[...] = jnp.zeros_like(dw_ref)
        db_ref[...] = jnp.zeros_like(db_ref)

        def back(s0, base, off):
            td = _taps(_window(dzp, base, off, H), H, offs)
            tx = _taps(_window(xo, base, off, H), H, offs)
            dz0 = sum(td[1 - k] * wo_ref[pl.ds(k, 1), :] for k in range(3))
            dz0_ref[pl.ds(s0 + off, CONV_ROWS), :] = dz0.astype(dz0_ref.dtype)
            db_ref[...] += _sum0(td[0])
            for k in range(3):
                dw_ref[pl.ds(k, 1), :] += _sum0(td[0] * tx[k - 1])

        _chunks(lay, back)

    own = lambda r: pl.BlockSpec((r, tc), lambda j, hf: (0, hf * nF + j))
    oth = lambda r: pl.BlockSpec((r, tc), lambda j, hf: (0, (1 - hf) * nF + j))
    ucol = pl.BlockSpec((T, tc), lambda j, hf: (0, j))
    return pl.pallas_call(
        body, name=name, grid=(nF, 2),
        in_specs=[own(T), oth(T), ucol, own(3), oth(3), own(1), oth(1)],
        out_specs=[ucol, own(T), own(3), own(1)],
        out_shape=[SDS((T, F), MMT), SDS((T, F2), MMT), SDS((3, F2), F32), SDS((1, F2), F32)],
        scratch_shapes=[pltpu.VMEM((srows, tc), F32)] * 3,
        compiler_params=_cparams("parallel", "arbitrary"))(z0, z0, du, conv_w, conv_w, conv_b, conv_b)


def _glu_conv(p0, b_in, dw_w, dw_b, segs, name):
    T, D2 = p0.shape
    D = D2 // 2
    tc = _tile(D, 256)
    nD = D // tc
    H = 2 * SUBLANES
    half = (CONV_WIDTH - 1) // 2
    lay, srows = _seg_layout(segs, H)
    offs = list(range(-half, half + 1))

    def body(pa_ref, pg_ref, ba_ref, bg_ref, w_ref, b_ref, z2_ref, z1p):
        _zero_pads(z1p, lay, H)

        def glu(s0, base, off):
            rows = pl.ds(s0 + off, CONV_ROWS)
            z1p[pl.ds(base + off, CONV_ROWS), :] = (pa_ref[rows, :] + ba_ref[...]) * _sig(pg_ref[rows, :] + bg_ref[...])

        _chunks(lay, glu)

        def conv(s0, base, off):
            t = _taps(_window(z1p, base, off, H), H, offs)
            acc = b_ref[...] + t[-half] * w_ref[pl.ds(0, 1), :]
            for k in range(1, CONV_WIDTH):
                acc = acc + t[k - half] * w_ref[pl.ds(k, 1), :]
            z2_ref[pl.ds(s0 + off, CONV_ROWS), :] = acc

        _chunks(lay, conv)

    cola = lambda r: pl.BlockSpec((r, tc), lambda j: (0, j))
    colg = lambda r: pl.BlockSpec((r, tc), lambda j: (0, j + nD))
    return pl.pallas_call(
        body, name=name, grid=(nD,),
        in_specs=[cola(T), colg(T), cola(1), colg(1), cola(CONV_WIDTH), cola(1)],
        out_specs=cola(T), out_shape=SDS((T, D), F32), scratch_shapes=[pltpu.VMEM((srows, tc), F32)],
        compiler_params=_cparams("parallel"))(p0, p0, b_in, b_in, dw_w, dw_b)


def _glu_conv_bwd(p0, b_in, dw_w, dz2, segs, name):
    T, D2 = p0.shape
    D = D2 // 2
    tc = _tile(D, 256)
    nD = D // tc
    H = 2 * SUBLANES
    half = (CONV_WIDTH - 1) // 2
    lay, srows = _seg_layout(segs, H)
    offs = list(range(-half, half + 1))

    def body(pa_ref, pg_ref, ba_ref, bg_ref, w_ref, dz2_ref, dpa_ref, dpg_ref, dw_ref, db_ref, dba_ref, dbg_ref, z1p, dzp):
        _zero_pads(z1p, lay, H)
        _zero_pads(dzp, lay, H)
        for s0, n, base in lay:
            dzp[pl.ds(base, n), :] = dz2_ref[pl.ds(s0, n), :]

        def glu(s0, base, off):
            rows = pl.ds(s0 + off, CONV_ROWS)
            z1p[pl.ds(base + off, CONV_ROWS), :] = (pa_ref[rows, :] + ba_ref[...]) * _sig(pg_ref[rows, :] + bg_ref[...])

        _chunks(lay, glu)
        for ref in (dw_ref, db_ref, dba_ref, dbg_ref):
            ref[...] = jnp.zeros_like(ref)

        def back(s0, base, off):
            td = _taps(_window(dzp, base, off, H), H, offs)
            tz = _taps(_window(z1p, base, off, H), H, offs)
            dz1 = td[half] * w_ref[pl.ds(0, 1), :]
            for k in range(1, CONV_WIDTH):
                dz1 = dz1 + td[half - k] * w_ref[pl.ds(k, 1), :]
            db_ref[...] += _sum0(td[0])
            for k in range(CONV_WIDTH):
                dw_ref[pl.ds(k, 1), :] += _sum0(td[0] * tz[k - half])
            rows = pl.ds(s0 + off, CONV_ROWS)
            pa = pa_ref[rows, :] + ba_ref[...]
            sg = _sig(pg_ref[rows, :] + bg_ref[...])
            dpa = dz1 * sg
            dpg = dz1 * pa * (sg * (1.0 - sg))
            dba_ref[...] += _sum0(dpa)
            dbg_ref[...] += _sum0(dpg)
            dpa_ref[rows, :] = dpa.astype(dpa_ref.dtype)
            dpg_ref[rows, :] = dpg.astype(dpg_ref.dtype)

        _chunks(lay, back)

    cola = lambda r: pl.BlockSpec((r, tc), lambda j: (0, j))
    colg = lambda r: pl.BlockSpec((r, tc), lambda j: (0, j + nD))
    return pl.pallas_call(
        body, name=name, grid=(nD,),
        in_specs=[cola(T), colg(T), cola(1), colg(1), cola(CONV_WIDTH), cola(T)],
        out_specs=[cola(T), cola(T), cola(CONV_WIDTH), cola(1), cola(1), cola(1)],
        out_shape=[SDS((T, D), MMT), SDS((T, D), MMT), SDS((CONV_WIDTH, D), F32), SDS((1, D), F32), SDS((1, D), F32),
                   SDS((1, D), F32)],
        scratch_shapes=[pltpu.VMEM((srows, tc), F32)] * 2, compiler_params=_cparams("parallel"))(p0, p0, b_in, b_in, dw_w, dz2)


def _layer_norm_stats(v):
    mu = jnp.mean(v, axis=-1, keepdims=True)
    var = jnp.mean(jnp.square(v - mu), axis=-1, keepdims=True)
    rstd = lax.rsqrt(var + EPS)
    return (v - mu) * rstd, rstd


def _ln_silu(z2, ln_g, ln_b, rows, tm, name):
    D = z2.shape[1]

    def body(z_ref, g_ref, b_ref, o_ref):
        xh, _ = _layer_norm_stats(z_ref[...])
        z3 = xh * g_ref[...] + b_ref[...]
        o_ref[...] = (z3 * _sig(z3)).astype(o_ref.dtype)

    row = pl.BlockSpec((tm, D), lambda i: (i, 0))
    vec = pl.BlockSpec((1, D), lambda i: (0, 0))
    return pl.pallas_call(body, name=name, grid=(rows // tm,), in_specs=[row, vec, vec], out_specs=row,
                          out_shape=SDS((rows, D), MMT), compiler_params=_cparams("parallel"))(z2, ln_g, ln_b)


def _ln_silu_bwd(z2, dz4, ln_g, ln_b, rows, tm, name):
    D = z2.shape[1]

    def body(z_ref, d_ref, g_ref, b_ref, dz_ref, dg_ref, db_ref):
        @pl.when(pl.program_id(0) == 0)
        def _():
            dg_ref[...] = jnp.zeros_like(dg_ref)
            db_ref[...] = jnp.zeros_like(db_ref)

        xh, rstd = _layer_norm_stats(z_ref[...])
        z3 = xh * g_ref[...] + b_ref[...]
        s = _sig(z3)
        dz3 = d_ref[...] * (s * (1.0 + z3 * (1.0 - s)))
        dg_ref[...] += _sum0(dz3 * xh)
        db_ref[...] += _sum0(dz3)
        dxh = dz3 * g_ref[...]
        dz_ref[...] = rstd * (dxh - jnp.mean(dxh, axis=-1, keepdims=True) - xh * jnp.mean(dxh * xh, axis=-1, keepdims=True))

    row = pl.BlockSpec((tm, D), lambda i: (i, 0))
    vec = pl.BlockSpec((1, D), lambda i: (0, 0))
    return pl.pallas_call(body, name=name, grid=(rows // tm,), in_specs=[row, row, vec, vec], out_specs=[row, vec, vec],
                          out_shape=[SDS((rows, D), F32), SDS((1, D), F32), SDS((1, D), F32)],
                          compiler_params=_cparams("arbitrary"))(z2, dz4, ln_g, ln_b)


def _rot_half_pairs(v):
    width = v.shape[1]
    lane = lax.broadcasted_iota(jnp.int32, v.shape, 1)
    return jnp.where((lane % 32) < 16, -pltpu.roll(v, shift=width - 16, axis=1), pltpu.roll(v, shift=16, axis=1))


def _rope(qkv, cos, sin, L, qk, tm, name):
    width = qkv.shape[1]
    kv = width - qk

    def body(x_ref, c_ref, s_ref, qk_ref, v_ref):
        xv = x_ref[:, pl.ds(0, qk)]
        c = jnp.tile(c_ref[...], (1, qk // LANES))
        s = jnp.tile(s_ref[...], (1, qk // LANES))
        qk_ref[...] = (xv * c + _rot_half_pairs(xv) * s).astype(qk_ref.dtype)
        v_ref[...] = x_ref[:, pl.ds(qk, kv)].astype(v_ref.dtype)

    tab = pl.BlockSpec((tm, LANES), lambda i: (i, 0))
    return pl.pallas_call(
        body, name=name, grid=(L // tm,), in_specs=[pl.BlockSpec((tm, width), lambda i: (i, 0)), tab, tab],
        out_specs=[pl.BlockSpec((tm, qk), lambda i: (i, 0)), pl.BlockSpec((tm, kv), lambda i: (i, 0))],
        out_shape=[SDS((L, qk), MMT), SDS((L, kv), MMT)], compiler_params=_cparams("parallel"))(qkv, cos, sin)


def _rope_bwd(dqk, dv, cos, sin, tm, name):
    L, qk = dqk.shape
    kv = dv.shape[1]

    def body(d_ref, dv_ref, c_ref, s_ref, o_ref):
        dv_ = d_ref[...]
        c = jnp.tile(c_ref[...], (1, qk // LANES))
        s = jnp.tile(s_ref[...], (1, qk // LANES))
        o_ref[:, pl.ds(0, qk)] = (dv_ * c - _rot_half_pairs(dv_ * s)).astype(o_ref.dtype)
        o_ref[:, pl.ds(qk, kv)] = dv_ref[...].astype(o_ref.dtype)

    tab = pl.BlockSpec((tm, LANES), lambda i: (i, 0))
    return pl.pallas_call(
        body, name=name, grid=(L // tm,),
        in_specs=[pl.BlockSpec((tm, qk), lambda i: (i, 0)), pl.BlockSpec((tm, kv), lambda i: (i, 0)), tab, tab],
        out_specs=pl.BlockSpec((tm, qk + kv), lambda i: (i, 0)), out_shape=SDS((L, qk + kv), MMT),
        compiler_params=_cparams("parallel"))(dqk, dv, cos, sin)


def _band_specs(nb, width):
    blk = lambda f: pl.BlockSpec((None, ATTN_BLOCK, width), f)
    return [blk(lambda h, n: (h, jnp.maximum(n - 1, 0), 0)), blk(lambda h, n: (h, n, 0)),
            blk(lambda h, n: (h, jnp.minimum(n + 1, nb - 1), 0))]


def _window_mask(n, L):
    qi = lax.broadcasted_iota(jnp.int32, (ATTN_BLOCK, 3 * ATTN_BLOCK), 0)
    kk = lax.broadcasted_iota(jnp.int32, (ATTN_BLOCK, 3 * ATTN_BLOCK), 1)
    key_abs = (n - 1) * ATTN_BLOCK + kk
    return (jnp.abs(qi + ATTN_BLOCK - kk) <= ATTN_BLOCK) & (key_abs >= 0) & (key_abs < L)


def _attn_fwd(q, k, v, kc, vc, sink, name):
    nkv, _, L, hd = q.shape
    C = kc.shape[1]
    nb = L // ATTN_BLOCK
    scale = HEAD_DIM ** -0.5

    def body(sink_ref, q_ref, k0, k1, k2, v0, v1, v2, kc_ref, vc_ref, o_ref, lse_ref):
        hh, n = pl.program_id(0), pl.program_id(1)
        kw = jnp.concatenate([k0[...], k1[...], k2[...]], axis=0)
        vw = jnp.concatenate([v0[...], v1[...], v2[...]], axis=0)
        mask = _window_mask(n, L)
        for g in range(Q_PER_KV):
            qg = q_ref[g]
            sw = jnp.where(mask, _dot(qg, kw, 1, 1) * scale, NEG)
            sc = _dot(qg, kc_ref[...], 1, 1) * scale
            sk = sink_ref[hh * Q_PER_KV + g]
            m = jnp.maximum(jnp.maximum(jnp.max(sw, axis=-1, keepdims=True), jnp.max(sc, axis=-1, keepdims=True)), sk)
            pw, pc = jnp.exp(sw - m), jnp.exp(sc - m)
            den = jnp.sum(pw, axis=-1, keepdims=True) + jnp.sum(pc, axis=-1, keepdims=True) + jnp.exp(sk - m)
            inv = 1.0 / den
            o_ref[g] = _dot(pw * inv, vw, 1, 0) + _dot(pc * inv, vc_ref[...], 1, 0)
            lse_ref[g] = m + jnp.log(den)

    qspec = pl.BlockSpec((None, Q_PER_KV, ATTN_BLOCK, hd), lambda h, n: (h, 0, n, 0))
    cspec = pl.BlockSpec((None, C, hd), lambda h, n: (h, 0, 0))
    return pl.pallas_call(
        body, name=name, grid=(nkv, nb),
        in_specs=[pl.BlockSpec(memory_space=pltpu.SMEM), qspec] + _band_specs(nb, hd) + _band_specs(nb, hd) + [cspec, cspec],
        out_specs=[qspec, pl.BlockSpec((None, Q_PER_KV, ATTN_BLOCK, 1), lambda h, n: (h, 0, n, 0))],
        out_shape=[SDS((nkv, Q_PER_KV, L, hd), F32), SDS((nkv, Q_PER_KV, L, 1), F32)],
        compiler_params=_cparams("parallel", "parallel"))(sink, q, k, k, k, v, v, v, kc, vc)


def _attn_bwd_q(q, k, v, kc, vc, sink, o, do, lse, name):
    nkv, _, L, hd = q.shape
    C = kc.shape[1]
    nb = L // ATTN_BLOCK
    scale = HEAD_DIM ** -0.5

    def body(sink_ref, q_ref, k0, k1, k2, v0, v1, v2, kc_ref, vc_ref, o_ref, do_ref, lse_ref, dq_ref, dkc_ref, dvc_ref, dsk_ref):
        hh, n = pl.program_id(0), pl.program_id(1)

        @pl.when(n == 0)
        def _():
            dkc_ref[...] = jnp.zeros_like(dkc_ref)
            dvc_ref[...] = jnp.zeros_like(dvc_ref)
            dsk_ref[...] = jnp.zeros_like(dsk_ref)

        kw = jnp.concatenate([k0[...], k1[...], k2[...]], axis=0)
        vw = jnp.concatenate([v0[...], v1[...], v2[...]], axis=0)
        mask = _window_mask(n, L)
        for g in range(Q_PER_KV):
            qg, dog, lse_g = q_ref[g], do_ref[g], lse_ref[g]
            delta = jnp.sum(dog.astype(F32) * o_ref[g], axis=-1, keepdims=True)
            pw = jnp.exp(jnp.where(mask, _dot(qg, kw, 1, 1) * scale, NEG) - lse_g)
            pc = jnp.exp(_dot(qg, kc_ref[...], 1, 1) * scale - lse_g)
            dsw = pw * (_dot(dog, vw, 1, 1) - delta)
            dsc = pc * (_dot(dog, vc_ref[...], 1, 1) - delta)
            dq_ref[g] = (_dot(dsw, kw, 1, 0) + _dot(dsc, kc_ref[...], 1, 0)) * scale
            dkc_ref[...] += _dot(dsc, qg, 0, 0) * scale
            dvc_ref[...] += _dot(pc, dog, 0, 0)
            psk = jnp.exp(sink_ref[hh * Q_PER_KV + g] - lse_g)
            dsk_ref[pl.ds(g, 1), :] += jnp.broadcast_to(jnp.sum(-psk * delta, axis=0, keepdims=True), (1, LANES))

    qspec = pl.BlockSpec((None, Q_PER_KV, ATTN_BLOCK, hd), lambda h, n: (h, 0, n, 0))
    lspec = pl.BlockSpec((None, Q_PER_KV, ATTN_BLOCK, 1), lambda h, n: (h, 0, n, 0))
    cspec = pl.BlockSpec((None, C, hd), lambda h, n: (h, 0, 0))
    return pl.pallas_call(
        body, name=name, grid=(nkv, nb),
        in_specs=[pl.BlockSpec(memory_space=pltpu.SMEM), qspec] + _band_specs(nb, hd) + _band_specs(nb, hd)
        + [cspec, cspec, qspec, qspec, lspec],
        out_specs=[qspec, cspec, cspec, pl.BlockSpec((None, SUBLANES, LANES), lambda h, n: (h, 0, 0))],
        out_shape=[SDS((nkv, Q_PER_KV, L, hd), F32), SDS((nkv, C, hd), F32), SDS((nkv, C, hd), F32),
                   SDS((nkv, SUBLANES, LANES), F32)],
        compiler_params=_cparams("parallel", "arbitrary"))(sink, q, k, k, k, v, v, v, kc, vc, o, do, lse)


def _attn_bwd_kv(q, k, v, o, do, lse, name):
    nkv, _, L, hd = q.shape
    nb = L // ATTN_BLOCK
    scale = HEAD_DIM ** -0.5

    def body(q0, q1, q2, do0, do1, do2, o0, o1, o2, l0, l1, l2, k_ref, v_ref, dk_ref, dv_ref):
        j = pl.program_id(1)
        qi = lax.broadcasted_iota(jnp.int32, (ATTN_BLOCK, ATTN_BLOCK), 0)
        kk = lax.broadcasted_iota(jnp.int32, (ATTN_BLOCK, ATTN_BLOCK), 1)
        kj, vj = k_ref[...], v_ref[...]
        dk = jnp.zeros((ATTN_BLOCK, hd), F32)
        dv = jnp.zeros((ATTN_BLOCK, hd), F32)
        for slot, (q_r, do_r, o_r, l_r) in enumerate(((q0, do0, o0, l0), (q1, do1, o1, l1), (q2, do2, o2, l2))):
            n = j - 1 + slot
            ok = (n >= 0) & (n < nb) & (jnp.abs(qi + ATTN_BLOCK - ((2 - slot) * ATTN_BLOCK + kk)) <= ATTN_BLOCK)
            for g in range(Q_PER_KV):
                qg, dog = q_r[g], do_r[g]
                delta = jnp.sum(dog.astype(F32) * o_r[g], axis=-1, keepdims=True)
                p = jnp.exp(jnp.where(ok, _dot(qg, kj, 1, 1) * scale - l_r[g], NEG))
                ds = p * (_dot(dog, vj, 1, 1) - delta)
                dk = dk + _dot(ds, qg, 0, 0) * scale
                dv = dv + _dot(p, dog, 0, 0)
        dk_ref[...] = dk
        dv_ref[...] = dv

    def band(width):
        blk = lambda f: pl.BlockSpec((None, Q_PER_KV, ATTN_BLOCK, width), f)
        return [blk(lambda h, j: (h, 0, jnp.maximum(j - 1, 0), 0)), blk(lambda h, j: (h, 0, j, 0)),
                blk(lambda h, j: (h, 0, jnp.minimum(j + 1, nb - 1), 0))]

    kspec = pl.BlockSpec((None, ATTN_BLOCK, hd), lambda h, j: (h, j, 0))
    return pl.pallas_call(
        body, name=name, grid=(nkv, nb), in_specs=band(hd) + band(hd) + band(hd) + band(1) + [kspec, kspec],
        out_specs=[kspec, kspec], out_shape=[SDS((nkv, L, hd), F32), SDS((nkv, L, hd), F32)],
        compiler_params=_cparams("parallel", "parallel"))(q, q, q, do, do, do, o, o, o, lse, lse, lse, k, v)


_GELU_K = math.sqrt(2.0 / math.pi)


def _gelu(v):
    return 0.5 * v * (1.0 + jnp.tanh(_GELU_K * (v + 0.044715 * (v * v * v))))


def _gelu_grad(v):
    t = jnp.tanh(_GELU_K * (v + 0.044715 * (v * v * v)))
    return 0.5 * (1.0 + t) + 0.5 * v * (1.0 - t * t) * (_GELU_K * (1.0 + 3.0 * 0.044715 * (v * v)))


def _gmlp_fwd(p0, b_in, ln_g, ln_b, w_s, b_s, name):
    L, W2 = p0.shape
    W = W2 // 2
    G = W // GMLP_GROUP_DIM

    def body(p_ref, bi_ref, g_ref, b_ref, ws_ref, bs_ref, o_ref):
        ge = _gelu(p_ref[...] + bi_ref[...])
        xh, _ = _layer_norm_stats(ge[:, W:])
        vln = xh * g_ref[...] + b_ref[...]
        for gi in range(G):
            cols = slice(gi * GMLP_GROUP_DIM, (gi + 1) * GMLP_GROUP_DIM)
            s = _dot(ws_ref[gi], vln[:, cols], 1, 0) + bs_ref[gi]
            o_ref[:, cols] = (ge[:, cols] * s).astype(o_ref.dtype)

    full = lambda shape: pl.BlockSpec(shape, lambda i: (0,) * len(shape))
    return pl.pallas_call(
        body, name=name, grid=(L // GMLP_CHUNK,),
        in_specs=[pl.BlockSpec((GMLP_CHUNK, W2), lambda i: (i, 0)), full((1, W2)), full((1, W)), full((1, W)),
                  full((G, GMLP_CHUNK, GMLP_CHUNK)), full((G, GMLP_CHUNK, 1))],
        out_specs=pl.BlockSpec((GMLP_CHUNK, W), lambda i: (i, 0)), out_shape=SDS((L, W), MMT),
        compiler_params=_cparams("parallel"))(p0, b_in, ln_g, ln_b, w_s, b_s)


def _gmlp_bwd(p0, dus, b_in, ln_g, ln_b, w_s, w_st, b_s, name):
    L, W2 = p0.shape
    W = W2 // 2
    G = W // GMLP_GROUP_DIM

    def body(p_ref, d_ref, bi_ref, g_ref, b_ref, ws_ref, wst_ref, bs_ref, dpre_ref, dbi_ref, dg_ref, db_ref, dws_ref, dbs_ref, dvln):
        @pl.when(pl.program_id(0) == 0)
        def _():
            for ref in (dbi_ref, dg_ref, db_ref, dws_ref, dbs_ref):
                ref[...] = jnp.zeros_like(ref)

        pre = p_ref[...] + bi_ref[...]
        ge = _gelu(pre)
        xh, rstd = _layer_norm_stats(ge[:, W:])
        vln = xh * g_ref[...] + b_ref[...]
        dge_u = []
        for gi in range(G):
            cols = slice(gi * GMLP_GROUP_DIM, (gi + 1) * GMLP_GROUP_DIM)
            vg = vln[:, cols]
            s = _dot(ws_ref[gi], vg, 1, 0) + bs_ref[gi]
            dus_g = d_ref[:, cols]
            dge_u.append(dus_g * s)
            ds = dus_g * ge[:, cols]
            dbs_ref[gi] += jnp.sum(ds, axis=1, keepdims=True)
            dws_ref[gi] += _dot(ds, vg, 1, 1)
            dvln[:, cols] = _dot(wst_ref[gi], ds, 1, 0)
        dv = dvln[...]
        dg_ref[...] += _sum0(dv * xh)
        db_ref[...] += _sum0(dv)
        dxh = dv * g_ref[...]
        dv0 = rstd * (dxh - jnp.mean(dxh, axis=-1, keepdims=True) - xh * jnp.mean(dxh * xh, axis=-1, keepdims=True))
        dpre = jnp.concatenate(dge_u + [dv0], axis=1) * _gelu_grad(pre)
        dbi_ref[...] += _sum0(dpre)
        dpre_ref[...] = dpre.astype(dpre_ref.dtype)

    full = lambda shape: pl.BlockSpec(shape, lambda i: (0,) * len(shape))
    mats = (G, GMLP_CHUNK, GMLP_CHUNK)
    return pl.pallas_call(
        body, name=name, grid=(L // GMLP_CHUNK,),
        in_specs=[pl.BlockSpec((GMLP_CHUNK, W2), lambda i: (i, 0)), pl.BlockSpec((GMLP_CHUNK, W), lambda i: (i, 0)),
                  full((1, W2)), full((1, W)), full((1, W)), full(mats), full(mats), full((G, GMLP_CHUNK, 1))],
        out_specs=[pl.BlockSpec((GMLP_CHUNK, W2), lambda i: (i, 0)), full((1, W2)), full((1, W)), full((1, W)), full(mats),
                   full((G, GMLP_CHUNK, 1))],
        out_shape=[SDS((L, W2), MMT), SDS((1, W2), F32), SDS((1, W), F32), SDS((1, W), F32), SDS(mats, F32),
                   SDS((G, GMLP_CHUNK, 1), F32)],
        scratch_shapes=[pltpu.VMEM((GMLP_CHUNK, W), F32)], compiler_params=_cparams("arbitrary"))(
            p0, dus, b_in, ln_g, ln_b, w_s, w_st, b_s)


def _loss_head(h, target, tm, name):
    L, D = h.shape

    def body(h_ref, t_ref, l_ref, d_ref):
        @pl.when(pl.program_id(0) == 0)
        def _():
            l_ref[...] = jnp.zeros_like(l_ref)

        e = h_ref[...] - t_ref[...]
        l_ref[...] += 0.5 * jnp.sum(jnp.mean(e * e, axis=-1, keepdims=True), axis=0, keepdims=True)
        d_ref[...] = e * (1.0 / D)

    row = pl.BlockSpec((tm, D), lambda i: (i, 0))
    return pl.pallas_call(body, name=name, grid=(L // tm,), in_specs=[row, row],
                          out_specs=[pl.BlockSpec((1, 1), lambda i: (0, 0)), row],
                          out_shape=[SDS((1, 1), F32), SDS((L, D), F32)], compiler_params=_cparams("arbitrary"))(h, target)


def _ada_fwd(cond, ada_w, ada_b, name):
    NL, D, n = ada_w.shape
    tn = _tile(n, 768)

    def body(c_ref, w_ref, b_ref, o_ref):
        cv = c_ref[...]
        o_ref[...] = _dot(cv * _sig(cv), w_ref[...], 1, 0) + b_ref[...]

    return pl.pallas_call(
        body, name=name, grid=(NL, n // tn),
        in_specs=[pl.BlockSpec((2 * SUBLANES, D), lambda i, j: (0, 0)), pl.BlockSpec((None, D, tn), lambda i, j: (i, 0, j)),
                  pl.BlockSpec((None, 1, tn), lambda i, j: (i, 0, j))],
        out_specs=pl.BlockSpec((None, 2 * SUBLANES, tn), lambda i, j: (i, 0, j)), out_shape=SDS((NL, 2 * SUBLANES, n), F32),
        compiler_params=_cparams("parallel", "parallel"))(cond, ada_w, ada_b)


def _ada_bwd(cond, ada_w, dm_lat, dm_ctx, name):
    NL, D, n = ada_w.shape
    tn = _tile(n, 768)

    def body(c_ref, w_ref, dl_ref, dc_ref, dw_ref, ds_ref):
        @pl.when((pl.program_id(0) == 0) & (pl.program_id(1) == 0))
        def _():
            ds_ref[...] = jnp.zeros_like(ds_ref)

        cv = c_ref[...]
        row = lax.broadcasted_iota(jnp.int32, (SUBLANES, tn), 0)
        ctx_rows = jnp.where(row == 0, _sum0(dc_ref[...]), 0.0)
        dm = jnp.concatenate([dl_ref[...], ctx_rows], axis=0)
        dw_ref[...] = _dot(cv * _sig(cv), dm, 0, 0)
        ds_ref[...] += _dot(dm, w_ref[...], 1, 1)

    dspec = pl.BlockSpec((None, SUBLANES, tn), lambda i, j: (i, 0, j))
    return pl.pallas_call(
        body, name=name, grid=(NL, n // tn),
        in_specs=[pl.BlockSpec((2 * SUBLANES, D), lambda i, j: (0, 0)), pl.BlockSpec((None, D, tn), lambda i, j: (i, 0, j)),
                  dspec, dspec],
        out_specs=[pl.BlockSpec((None, D, tn), lambda i, j: (i, 0, j)), pl.BlockSpec((2 * SUBLANES, D), lambda i, j: (0, 0))],
        out_shape=[SDS((NL, D, n), F32), SDS((2 * SUBLANES, D), F32)],
        compiler_params=_cparams("arbitrary", "arbitrary"))(cond, ada_w, dm_lat, dm_ctx)


def _adam_math(w, g, m, v):
    m = ADAM_B1 * m + (1.0 - ADAM_B1) * g
    v = ADAM_B2 * v + (1.0 - ADAM_B2) * jnp.square(g)
    m_hat = m / (1.0 - ADAM_B1 ** ADAM_STEP)
    v_hat = v / (1.0 - ADAM_B2 ** ADAM_STEP)
    return -ADAM_LR * (m_hat / (jnp.sqrt(v_hat) + ADAM_EPS) + ADAM_WD * w), m, v


def _row_tile(rows, cols, elems):
    want = max(SUBLANES, elems // cols)
    best = SUBLANES if rows % SUBLANES == 0 else rows
    for d in range(SUBLANES, min(rows, want) + 1, SUBLANES):
        if rows % d == 0:
            best = d
    return best


def _adamw(w, m, v, parts, name):
    R, C = w.shape
    tr = _row_tile(R, C, 128 * 1024)
    npart = len(parts)

    def body(*refs):
        w_ref, m_ref, v_ref = refs[:3]
        g_ref, d_ref, nm_ref, nv_ref = refs[3 + npart:]
        g = refs[3][...]
        for p_ref in refs[4:3 + npart]:
            g = g + p_ref[...]
        d, nm, nv = _adam_math(w_ref[...], g, m_ref[...], v_ref[...])
        g_ref[...], d_ref[...], nm_ref[...], nv_ref[...] = g, d, nm, nv

    blk = pl.BlockSpec((tr, C), lambda i: (i, 0))
    return pl.pallas_call(body, name=name, grid=(R // tr,), in_specs=[blk] * (3 + npart), out_specs=[blk] * 4,
                          out_shape=[SDS((R, C), F32)] * 4, compiler_params=_cparams("parallel"))(w, m, v, *parts)


def _sum_slots(x, name, out_dtype=F32):
    S, R, C = x.shape
    tr = _row_tile(R, C, 128 * 1024)

    def body(x_ref, o_ref):
        acc = x_ref[0].astype(F32)
        for s in range(1, S):
            acc = acc + x_ref[s].astype(F32)
        o_ref[...] = acc.astype(o_ref.dtype)

    return pl.pallas_call(body, name=name, grid=(R // tr,), in_specs=[pl.BlockSpec((S, tr, C), lambda i: (0, i, 0))],
                          out_specs=pl.BlockSpec((tr, C), lambda i: (i, 0)), out_shape=SDS((R, C), out_dtype),
                          compiler_params=_cparams("parallel"))(x)


def _my_place():
    return lax.axis_index("x"), lax.axis_index("y"), lax.axis_index("c")


def _other_chips(x, y):
    return [(1 - x, y), (x, 1 - y), (1 - x, 1 - y)]


def _all_gather(v, name):
    R, C = v.shape

    def body(v_ref, o_ref, send_sems, recv_sems, local_sem):
        x, y, c = _my_place()
        me = 4 * x + 2 * y + c
        mine = pltpu.make_async_copy(v_ref, o_ref.at[me], local_sem)
        mine.start()
        copies = []
        for flip in range(1, N_DEV):
            fx, fy, fc = (flip >> 2) & 1, (flip >> 1) & 1, flip & 1
            peer = ((x + fx) % 2, (y + fy) % 2, (c + fc) % 2)
            cp = pltpu.make_async_remote_copy(src_ref=v_ref, dst_ref=o_ref.at[me], send_sem=send_sems.at[flip - 1],
                                              recv_sem=recv_sems.at[flip - 1], device_id=peer, device_id_type=MESH)
            cp.start()
            copies.append(cp)
        for cp in copies:
            cp.wait()
        mine.wait()

    return pl.pallas_call(
        body, name=name, in_specs=[pl.BlockSpec(memory_space=pl.ANY)], out_specs=pl.BlockSpec(memory_space=pl.ANY),
        out_shape=SDS((N_DEV, R, C), v.dtype),
        scratch_shapes=[pltpu.SemaphoreType.DMA((N_DEV - 1,)), pltpu.SemaphoreType.DMA((N_DEV - 1,)), pltpu.SemaphoreType.DMA],
        )(v)


def _shard_window(ref, axis, j, size):
    idx = [slice(None)] * len(ref.shape)
    idx[axis] = pl.ds(pl.multiple_of(j * size, SUBLANES), size)
    return ref.at[tuple(idx)]


def _gather_weights(shards, axes, name):
    nt = len(shards)
    out_shapes = []
    for s, ax in zip(shards, axes):
        shape = list(s.shape)
        shape[ax] *= N_CHIPS
        out_shapes.append(SDS(tuple(shape), s.dtype))

    def body(*refs):
        ins, outs = refs[:nt], refs[nt:2 * nt]
        send_sems, recv_sems, local_sems = refs[2 * nt:]
        x, y, c = _my_place()
        j = 2 * x + y
        copies = []
        for t in range(nt):
            size = ins[t].shape[axes[t]]
            dst = _shard_window(outs[t], axes[t], j, size)
            mine = pltpu.make_async_copy(ins[t], dst, local_sems.at[t])
            mine.start()
            copies.append(mine)
            for k, (px, py) in enumerate(_other_chips(x, y)):
                cp = pltpu.make_async_remote_copy(src_ref=ins[t], dst_ref=dst, send_sem=send_sems.at[t, k],
                                                  recv_sem=recv_sems.at[t, k], device_id=(px, py, c), device_id_type=MESH)
                cp.start()
                copies.append(cp)
        for cp in copies:
            cp.wait()

    any_spec = pl.BlockSpec(memory_space=pl.ANY)
    return pl.pallas_call(
        body, name=name, in_specs=[any_spec] * nt, out_specs=[any_spec] * nt, out_shape=out_shapes,
        scratch_shapes=[pltpu.SemaphoreType.DMA((nt, 3)), pltpu.SemaphoreType.DMA((nt, 3)), pltpu.SemaphoreType.DMA((nt,))],
        )(*shards)


def _scatter_grads(groups, axes, name):
    nt = len(groups)
    flat = [g for grp in groups for g in grp]
    starts = [sum(len(grp) for grp in groups[:t]) for t in range(nt)]
    out_shapes, sizes = [], []
    for grp, ax in zip(groups, axes):
        shape = list(grp[0].shape)
        shape[ax] //= N_CHIPS
        sizes.append(shape[ax])
        out_shapes.append(SDS((N_CHIPS, len(grp)) + tuple(shape), grp[0].dtype))
    ncopy = len(flat)

    def body(*refs):
        ins, outs = refs[:ncopy], refs[ncopy:ncopy + nt]
        send_sems, recv_sems, local_sems = refs[ncopy + nt:]
        x, y, c = _my_place()
        j = 2 * x + y
        copies = []
        for t in range(nt):
            for l in range(len(groups[t])):
                src_full = ins[starts[t] + l]
                n = starts[t] + l
                mine = pltpu.make_async_copy(_shard_window(src_full, axes[t], j, sizes[t]), outs[t].at[j, l], local_sems.at[n])
                mine.start()
                copies.append(mine)
                for k, (px, py) in enumerate(_other_chips(x, y)):
                    cp = pltpu.make_async_remote_copy(
                        src_ref=_shard_window(src_full, axes[t], 2 * px + py, sizes[t]), dst_ref=outs[t].at[j, l],
                        send_sem=send_sems.at[n, k], recv_sem=recv_sems.at[n, k], device_id=(px, py, c), device_id_type=MESH)
                    cp.start()
                    copies.append(cp)
        for cp in copies:
            cp.wait()

    any_spec = pl.BlockSpec(memory_space=pl.ANY)
    return pl.pallas_call(
        body, name=name, in_specs=[any_spec] * ncopy, out_specs=[any_spec] * nt, out_shape=out_shapes,
        scratch_shapes=[pltpu.SemaphoreType.DMA((ncopy, 3)), pltpu.SemaphoreType.DMA((ncopy, 3)),
                        pltpu.SemaphoreType.DMA((ncopy,))],
        )(*flat)


def _swap_with_sibling(parts, name):
    nt = len(parts)

    def body(*refs):
        ins, outs = refs[:nt], refs[nt:2 * nt]
        send_sems, recv_sems = refs[2 * nt:]
        x, y, c = _my_place()
        copies = []
        for t in range(nt):
            cp = pltpu.make_async_remote_copy(src_ref=ins[t], dst_ref=outs[t], send_sem=send_sems.at[t], recv_sem=recv_sems.at[t],
                                              device_id=(x, y, 1 - c), device_id_type=MESH)
            cp.start()
            copies.append(cp)
        for cp in copies:
            cp.wait()

    any_spec = pl.BlockSpec(memory_space=pl.ANY)
    return pl.pallas_call(
        body, name=name, in_specs=[any_spec] * nt, out_specs=[any_spec] * nt, out_shape=[SDS(p.shape, p.dtype) for p in parts],
        scratch_shapes=[pltpu.SemaphoreType.DMA((nt,)), pltpu.SemaphoreType.DMA((nt,))],
        )(*parts)


PACK_COLS = 1024


def _pack(arrays):
    flat = jnp.concatenate([a.reshape(-1) for a in arrays])
    pad = (-flat.shape[0]) % (SUBLANES * PACK_COLS)
    return jnp.pad(flat, (0, pad)).reshape(-1, PACK_COLS)


def _unpack(packed, shapes):
    flat, out, pos = packed.reshape(-1), [], 0
    for shape in shapes:
        n = math.prod(shape)
        out.append(flat[pos:pos + n].reshape(shape))
        pos += n
    return out


def _unshard_last(stacked):
    moved = jnp.moveaxis(stacked, 0, -2)
    return moved.reshape(moved.shape[:-2] + (moved.shape[-2] * moved.shape[-1],))


def _my_block_last(full, j):
    s = full.shape[-1] // N_CHIPS
    return lax.dynamic_index_in_dim(full.reshape(full.shape[:-1] + (N_CHIPS, s)), j, axis=full.ndim - 1, keepdims=False)


def _rope_tables(L):
    rows = L // GRID_W
    row = jnp.repeat(jnp.arange(rows), GRID_W).astype(F32)
    col = jnp.tile(jnp.arange(GRID_W), rows).astype(F32)
    axis_dim = HEAD_DIM // 2
    inv_freq = ROPE_BASE ** (-jnp.arange(0, axis_dim, 2, dtype=F32) / axis_dim)
    ang_r, ang_c = row[:, None] * inv_freq[None, :], col[:, None] * inv_freq[None, :]
    ang = jnp.concatenate([ang_r, ang_r, ang_c, ang_c] * 2, axis=-1)
    return jnp.cos(ang), jnp.sin(ang)


SMALL_SHARDED = ("norm_g", "ffn_conv_w", "cm_b_in", "cm_dw_w", "cm_dw_b", "cm_ln_g", "cm_ln_b", "cm_b_out", "gm_b_in", "gm_ln_g",
                 "gm_ln_b")
SMALL_REPLICATED = ("c_ctx", "ada_b", "ffn_conv_b", "attn_sink", "gm_w_s", "gm_b_s")
BIG = ("ffn_w_up", "ffn_w_down", "cm_w_in", "cm_w_out", "attn_w_qkv", "attn_w_o", "gm_w_in", "gm_w_out")
BIG_AXIS = {"ffn_w_up": 2, "ffn_w_down": 1, "cm_w_in": 2, "cm_w_out": 1, "attn_w_qkv": 2, "attn_w_o": 1, "gm_w_in": 2, "gm_w_out": 1}
WEIGHTS = ("c_ctx", "ada_w", "ada_b", "norm_g", "ffn_w_up", "ffn_conv_w", "ffn_conv_b", "ffn_w_down", "cm_w_in", "cm_b_in",
           "cm_dw_w", "cm_dw_b", "cm_ln_g", "cm_ln_b", "cm_w_out", "cm_b_out", "attn_w_qkv", "attn_sink", "attn_w_o", "gm_w_in",
           "gm_b_in", "gm_ln_g", "gm_ln_b", "gm_w_s", "gm_b_s", "gm_w_out")


def _step(x, c, ctx, target, W, M, V):
    L, D = x.shape[1], x.shape[2]
    C = ctx.shape[1]
    T = L + C
    NL = W["ada_w"].shape[0]
    tm = 256 if C % 256 == 0 else 128
    nl = L // tm
    xi, yi, ci = _my_place()
    chip = 2 * xi + yi
    dev = 4 * xi + 2 * yi + ci
    segs2, segs1 = [(0, L), (L, C)], [(0, L)]
    vec = lambda a: a.reshape(1, -1)

    small_shapes = [W[n].shape for n in SMALL_SHARDED]
    ag1 = _all_gather(_pack([c.reshape(-1)] + [W[n] for n in SMALL_SHARDED]), "gather_small")
    parts = [_unpack(ag1[2 * s], [(D,)] + small_shapes) for s in range(N_CHIPS)]
    c_rows = jnp.stack([_unpack(ag1[d], [(D,)])[0] for d in range(N_DEV)])
    P = {n: _unshard_last(jnp.stack([parts[s][1 + i] for s in range(N_CHIPS)])) for i, n in enumerate(SMALL_SHARDED)}
    for n in SMALL_REPLICATED:
        P[n] = W[n]
    big_order = list(BIG)
    gathered = _gather_weights([W[n].astype(MMT) for n in big_order], [BIG_AXIS[n] for n in big_order], "gather_weights")
    for n, g in zip(big_order, gathered):
        P[n] = g

    cond = jnp.concatenate([c_rows, W["c_ctx"][None, :], jnp.zeros((2 * SUBLANES - N_DEV - 1, D), F32)], axis=0)
    ncol = W["ada_w"].shape[2]
    ada_b_mine = lax.dynamic_slice_in_dim(W["ada_b"], chip * ncol, ncol, axis=1)[:, None, :]
    mods_mine = _ada_fwd(cond, W["ada_w"], ada_b_mine, "ada_fwd")
    ag2 = _all_gather(mods_mine.reshape(NL * 2 * SUBLANES, ncol), "gather_mods").reshape(N_DEV, NL, 2 * SUBLANES, ncol)
    mods_all = _unshard_last(jnp.stack([ag2[2 * s] for s in range(N_CHIPS)]))
    mod_lat = lax.dynamic_index_in_dim(mods_all, dev, axis=1, keepdims=False).reshape(NL, 6, D)
    mod_ctx = mods_all[:, N_DEV].reshape(NL, 6, D)
    mod2 = jnp.stack([mod_lat, mod_ctx], axis=1)
    mod1 = mod_lat[:, None]

    zero_d = jnp.zeros((1, D), F32)
    cos, sin = _rope_tables(L)
    nkv = D // HEAD_DIM // Q_PER_KV
    qdim, kvdim = D, nkv * HEAD_DIM

    def ffn_fwd(i, h, mod, rows, segs, tag):
        a2 = _prenorm(h, mod, vec(P["norm_g"][i, 2]), 1, rows, nl, tm, f"pre_ffn_{tag}")
        z0 = _mm(a2, P["ffn_w_up"][i], "nn", F32, f"ffn_up_{tag}")
        u = _ffn_gate(z0, P["ffn_conv_w"][i], vec(P["ffn_conv_b"][i]), segs, f"ffn_gate_{tag}")
        f = _mm(u, P["ffn_w_down"][i], "nn", F32, f"ffn_down_{tag}")
        h_out = _postnorm(h, f, zero_d, mod, vec(P["norm_g"][i, 3]), 5, rows, nl, tm, f"post_ffn_{tag}")
        return h_out, dict(h=h, a2=a2, z0=z0, f=f)

    def ffn_bwd(i, dh, sv, mod, rows, segs, tag, G):
        df, dg2, dgn3, _ = _postnorm_bwd(dh, sv["f"], zero_d, mod, vec(P["norm_g"][i, 3]), 5, rows, nl, tm, f"post_ffn_bwd_{tag}")
        du = _mm(df, P["ffn_w_down"][i], "nt", F32, f"ffn_down_dx_{tag}")
        u, dz0, dcw, dcb = _ffn_gate_bwd(sv["z0"], du, P["ffn_conv_w"][i], vec(P["ffn_conv_b"][i]), segs, f"ffn_gate_bwd_{tag}")
        G["ffn_w_down"][i] = _mm(u, df, "tn", MMT, f"ffn_down_dw_{tag}")
        G["ffn_w_up"][i] = _mm(sv["a2"], dz0, "tn", MMT, f"ffn_up_dw_{tag}")
        da2 = _mm(dz0, P["ffn_w_up"][i], "nt", F32, f"ffn_up_dx_{tag}")
        dh, dsh2, dsc2, dgn2 = _prenorm_bwd(sv["h"], da2, dh, mod, vec(P["norm_g"][i, 2]), 1, rows, nl, tm, f"pre_ffn_bwd_{tag}")
        G["ffn_conv_w"][i], G["ffn_conv_b"][i] = dcw, dcb[0]
        return dh, (dsh2, dsc2, dg2), (dgn2, dgn3)

    def conformer_fwd(i, j, h, mod, rows, segs, tag):
        a = _prenorm(h, mod, vec(P["norm_g"][i, 0]), 0, rows, nl, tm, f"pre_mix_{tag}")
        p0 = _mm(a, P["cm_w_in"][j], "nn", F32, f"cm_in_{tag}")
        z2 = _glu_conv(p0, vec(P["cm_b_in"][j]), P["cm_dw_w"][j], vec(P["cm_dw_b"][j]), segs, f"cm_conv_{tag}")
        z4 = _ln_silu(z2, vec(P["cm_ln_g"][j]), vec(P["cm_ln_b"][j]), rows, tm, f"cm_ln_{tag}")
        y = _mm(z4, P["cm_w_out"][j], "nn", F32, f"cm_out_{tag}")
        h_out = _postnorm(h, y, vec(P["cm_b_out"][j]), mod, vec(P["norm_g"][i, 1]), 2, rows, nl, tm, f"post_mix_{tag}")
        return h_out, dict(h=h, a=a, p0=p0, z2=z2, z4=z4, y=y)

    def conformer_bwd(i, j, dh, sv, mod, rows, segs, tag, G):
        dy, dg1, dgn1, dbo = _postnorm_bwd(dh, sv["y"], vec(P["cm_b_out"][j]), mod, vec(P["norm_g"][i, 1]), 2, rows, nl, tm,
                                           f"post_mix_bwd_{tag}")
        G["cm_w_out"][j] = _mm(sv["z4"], dy, "tn", MMT, f"cm_out_dw_{tag}")
        dz4 = _mm(dy, P["cm_w_out"][j], "nt", F32, f"cm_out_dx_{tag}")
        dz2, dlg, dlb = _ln_silu_bwd(sv["z2"], dz4, vec(P["cm_ln_g"][j]), vec(P["cm_ln_b"][j]), rows, tm, f"cm_ln_bwd_{tag}")
        dpa, dpg, ddw, ddb, dba, dbg = _glu_conv_bwd(sv["p0"], vec(P["cm_b_in"][j]), P["cm_dw_w"][j], dz2, segs, f"cm_conv_bwd_{tag}")
        dp = jnp.concatenate([dpa, dpg], axis=1)
        G["cm_w_in"][j] = _mm(sv["a"], dp, "tn", MMT, f"cm_in_dw_{tag}")
        da = _mm(dp, P["cm_w_in"][j], "nt", F32, f"cm_in_dx_{tag}")
        dh, dsh1, dsc1, dgn0 = _prenorm_bwd(sv["h"], da, dh, mod, vec(P["norm_g"][i, 0]), 0, rows, nl, tm, f"pre_mix_bwd_{tag}")
        G["cm_b_out"][j] = jnp.sum(dbo, axis=0)[0]
        G["cm_ln_g"][j], G["cm_ln_b"][j], G["cm_dw_w"][j], G["cm_dw_b"][j] = dlg[0], dlb[0], ddw, ddb[0]
        G["cm_b_in"][j] = jnp.concatenate([dba[0], dbg[0]])
        return dh, (dsh1, dsc1, dg1), (dgn0, dgn1)

    def heads(a, n):
        return a.reshape(a.shape[0], n, HEAD_DIM).transpose(1, 0, 2)

    def unheads(a):
        return a.transpose(1, 0, 2).reshape(a.shape[1], -1)

    G = {n: [None] * W[n].shape[0] for n in WEIGHTS if n not in ("c_ctx", "ada_w", "ada_b", "norm_g")}
    saved = []
    h = jnp.concatenate([x[0], ctx[0]], axis=0)
    h, s_mix = conformer_fwd(0, 0, h, mod2[0], T, segs2, "l0")
    h, s_ffn = ffn_fwd(0, h, mod2[0], T, segs2, "l0")
    saved.append((s_mix, s_ffn))
    a_all = _prenorm(h, mod2[1], vec(P["norm_g"][1, 0]), 0, T, nl, tm, "pre_mix_l1")
    qkv = _mm(a_all, P["attn_w_qkv"][0], "nn", F32, "attn_qkv")
    qk_rot, v_lat = _rope(qkv, cos, sin, L, qdim + kvdim, tm, "rope")
    q_h = heads(qk_rot[:, :qdim], nkv * Q_PER_KV).reshape(nkv, Q_PER_KV, L, HEAD_DIM)
    k_h, v_h = heads(qk_rot[:, qdim:], nkv), heads(v_lat, nkv)
    kc_h = heads(qkv[L:, qdim:qdim + kvdim].astype(MMT), nkv)
    vc_h = heads(qkv[L:, qdim + kvdim:].astype(MMT), nkv)
    sink = P["attn_sink"][0]
    o_h, lse = _attn_fwd(q_h, k_h, v_h, kc_h, vc_h, sink, "attn")
    o_nat = unheads(o_h.reshape(nkv * Q_PER_KV, L, HEAD_DIM)).astype(MMT)
    y1 = _mm(o_nat, P["attn_w_o"][0], "nn", F32, "attn_out")
    h_in1 = h
    h = _postnorm(h, y1, zero_d, mod1[1], vec(P["norm_g"][1, 1]), 2, L, nl, tm, "post_mix_l1")
    h, s_ffn1 = ffn_fwd(1, h, mod1[1], L, segs1, "lat")
    h_in2 = h
    a_2 = _prenorm(h, mod1[2], vec(P["norm_g"][2, 0]), 0, L, nl, tm, "pre_mix_l2")
    p0_2 = _mm(a_2, P["gm_w_in"][0], "nn", F32, "gm_in")
    ws_bf = P["gm_w_s"][0].astype(MMT)
    bs_col = P["gm_b_s"][0][:, :, None]
    us = _gmlp_fwd(p0_2, vec(P["gm_b_in"][0]), vec(P["gm_ln_g"][0]), vec(P["gm_ln_b"][0]), ws_bf, bs_col, "gmlp")
    y2 = _mm(us, P["gm_w_out"][0], "nn", F32, "gm_out")
    h = _postnorm(h, y2, zero_d, mod1[2], vec(P["norm_g"][2, 1]), 2, L, nl, tm, "post_mix_l2")
    h, s_ffn2 = ffn_fwd(2, h, mod1[2], L, segs1, "lat")
    h, s_mix3 = conformer_fwd(3, 1, h, mod1[3], L, segs1, "l3")
    h, s_ffn3 = ffn_fwd(3, h, mod1[3], L, segs1, "lat")

    loss_mine, dh = _loss_head(h, target[0], tm, "loss_head")

    dmod = [None] * NL
    dgn = [None] * NL

    def finish(i, mix, ffn, gns_mix, gns_ffn):
        dmod[i] = jnp.concatenate(list(mix) + list(ffn), axis=1)
        dgn[i] = jnp.stack([jnp.sum(g, axis=0)[0] for g in (gns_mix[0], gns_mix[1], gns_ffn[0], gns_ffn[1])])

    dh, m_ffn, n_ffn = ffn_bwd(3, dh, s_ffn3, mod1[3], L, segs1, "lat", G)
    dh, m_mix, n_mix = conformer_bwd(3, 1, dh, s_mix3, mod1[3], L, segs1, "l3", G)
    finish(3, m_mix, m_ffn, n_mix, n_ffn)

    dh, m_ffn, n_ffn = ffn_bwd(2, dh, s_ffn2, mod1[2], L, segs1, "lat", G)
    dy2, dg1, dgn1, _ = _postnorm_bwd(dh, y2, zero_d, mod1[2], vec(P["norm_g"][2, 1]), 2, L, nl, tm, "post_mix_bwd_l2")
    G["gm_w_out"][0] = _mm(us, dy2, "tn", MMT, "gm_out_dw")
    dus = _mm(dy2, P["gm_w_out"][0], "nt", F32, "gm_out_dx")
    ws_t = jnp.swapaxes(P["gm_w_s"][0], 1, 2).astype(MMT)
    dpre, dbi, dlg, dlb, dws, dbs = _gmlp_bwd(p0_2, dus, vec(P["gm_b_in"][0]), vec(P["gm_ln_g"][0]), vec(P["gm_ln_b"][0]), ws_bf,
                                              ws_t, bs_col, "gmlp_bwd")
    G["gm_w_in"][0] = _mm(a_2, dpre, "tn", MMT, "gm_in_dw")
    da = _mm(dpre, P["gm_w_in"][0], "nt", F32, "gm_in_dx")
    dh, dsh1, dsc1, dgn0 = _prenorm_bwd(h_in2, da, dh, mod1[2], vec(P["norm_g"][2, 0]), 0, L, nl, tm, "pre_mix_bwd_l2")
    G["gm_b_in"][0], G["gm_ln_g"][0], G["gm_ln_b"][0], G["gm_w_s"][0], G["gm_b_s"][0] = dbi[0], dlg[0], dlb[0], dws, dbs[:, :, 0]
    finish(2, (dsh1, dsc1, dg1), m_ffn, (dgn0, dgn1), n_ffn)

    dh, m_ffn, n_ffn = ffn_bwd(1, dh, s_ffn1, mod1[1], L, segs1, "lat", G)
    dy1, dg1, dgn1, _ = _postnorm_bwd(dh, y1, zero_d, mod1[1], vec(P["norm_g"][1, 1]), 2, L, nl, tm, "post_mix_bwd_l1")
    G["attn_w_o"][0] = _mm(o_nat, dy1, "tn", MMT, "attn_out_dw")
    do_nat = _mm(dy1, P["attn_w_o"][0], "nt", MMT, "attn_out_dx")
    do_h = heads(do_nat, nkv * Q_PER_KV).reshape(nkv, Q_PER_KV, L, HEAD_DIM)
    dq_h, dkc_h, dvc_h, dsk = _attn_bwd_q(q_h, k_h, v_h, kc_h, vc_h, sink, o_h, do_h, lse, "attn_bwd_q")
    dk_h, dv_h = _attn_bwd_kv(q_h, k_h, v_h, o_h, do_h, lse, "attn_bwd_kv")
    dqk = jnp.concatenate([unheads(dq_h.reshape(nkv * Q_PER_KV, L, HEAD_DIM)), unheads(dk_h)], axis=1)
    dqkv_lat = _rope_bwd(dqk, unheads(dv_h), cos, sin, tm, "rope_bwd")
    dqkv_ctx = jnp.concatenate([jnp.zeros((C, qdim), MMT), unheads(dkc_h).astype(MMT), unheads(dvc_h).astype(MMT)], axis=1)
    dqkv = jnp.concatenate([dqkv_lat, dqkv_ctx], axis=0)
    G["attn_w_qkv"][0] = _mm(a_all, dqkv, "tn", MMT, "attn_qkv_dw")
    da_all = _mm(dqkv, P["attn_w_qkv"][0], "nt", F32, "attn_qkv_dx")
    dh_all = jnp.concatenate([dh, jnp.zeros((C, D), F32)], axis=0)
    dh, dsh1, dsc1, dgn0 = _prenorm_bwd(h_in1, da_all, dh_all, mod2[1], vec(P["norm_g"][1, 0]), 0, T, nl, tm, "pre_mix_bwd_l1")
    G["attn_sink"][0] = dsk[:, :Q_PER_KV, 0].reshape(-1)
    pad_ctx = lambda a: jnp.concatenate([a, jnp.zeros_like(a)], axis=0)
    finish(1, (dsh1, dsc1, pad_ctx(dg1)), [pad_ctx(a) for a in m_ffn], (dgn0, dgn1), n_ffn)

    s_mix0, s_ffn0 = saved[0]
    dh, m_ffn, n_ffn = ffn_bwd(0, dh, s_ffn0, mod2[0], T, segs2, "l0", G)
    dh, m_mix, n_mix = conformer_bwd(0, 0, dh, s_mix0, mod2[0], T, segs2, "l0", G)
    finish(0, m_mix, m_ffn, n_mix, n_ffn)
    grad_x = dh[:L][None]

    for i in range(2, NL):
        dmod[i] = pad_ctx(dmod[i])
    dmod_all = jnp.stack(dmod).reshape(NL, 2, 6 * D)

    ag3 = _all_gather(dmod_all.reshape(NL * 2, 6 * D), "gather_dmods").reshape(N_DEV, NL, 2, N_CHIPS, ncol)
    dm_cols = lax.dynamic_index_in_dim(ag3, chip, axis=3, keepdims=False)
    dm_lat, dm_ctx = jnp.moveaxis(dm_cols[:, :, 0], 0, 1), jnp.moveaxis(dm_cols[:, :, 1], 0, 1)
    g_ada_w, dsilu = _ada_bwd(cond, W["ada_w"], dm_lat, dm_ctx, "ada_bwd")
    cc = W["c_ctx"]
    sg = jax.nn.sigmoid(cc)
    dcctx_part = jnp.where(ci == 0, 1.0, 0.0) * dsilu[N_DEV] * (sg * (1.0 + cc * (1.0 - sg)))

    Gs = {n: jnp.stack(G[n]) for n in G if n not in BIG}
    Gs["norm_g"] = jnp.stack(dgn)
    Gs["ada_b"] = jnp.sum(dmod_all, axis=1)
    Gs["c_ctx"] = dcctx_part
    small_names = list(SMALL_SHARDED) + list(SMALL_REPLICATED)
    small_full_shapes = [P[n].shape for n in small_names]
    ag4 = _all_gather(_pack([Gs[n] for n in small_names]), "gather_small_grads")
    small_sum = _unpack(_sum_slots(ag4, "sum_small_grads"), small_full_shapes)
    g_small = {}
    for n, g in zip(small_names, small_sum):
        g_small[n] = _my_block_last(g, chip) if n in SMALL_SHARDED else g
    packed = [_pack([d[n] for n in small_names]) for d in (W, M, V)]
    outs_small = _adamw(packed[0], packed[1], packed[2], [_pack([g_small[n] for n in small_names])], "adamw_small")
    shard_shapes = [W[n].shape for n in small_names]
    res = {n: tuple(_unpack(o, shard_shapes)[k] for o in outs_small) for k, n in enumerate(small_names)}

    groups = [[G[n][l] for l in range(len(G[n]))] for n in big_order]
    contrib = _scatter_grads(groups, [BIG_AXIS[n] - 1 for n in big_order], "scatter_grads")
    partial = []
    for n, cb in zip(big_order, contrib):
        rows, cols = math.prod(cb.shape[1:-1]), cb.shape[-1]
        partial.append(_sum_slots(cb.reshape(N_CHIPS, rows, cols), f"sum_chips_{n}"))
    theirs = _swap_with_sibling(partial, "swap_cores")
    flat2 = lambda a: a.reshape(-1, a.shape[-1])
    for n, mine, other in zip(big_order, partial, theirs):
        outs = _adamw(flat2(W[n]), flat2(M[n]), flat2(V[n]), [mine, other], f"adamw_{n}")
        res[n] = tuple(o.reshape(W[n].shape) for o in outs)
    outs = _adamw(flat2(W["ada_w"]), flat2(M["ada_w"]), flat2(V["ada_w"]), [flat2(g_ada_w)], "adamw_ada_w")
    res["ada_w"] = tuple(o.reshape(W["ada_w"].shape) for o in outs)

    loss = lax.psum(loss_mine[0, 0], ("x", "y", "c"))
    return (loss, grad_x) + tuple(res[n][k] for k in range(4) for n in WEIGHTS)


def kernel(x, c, ctx, c_ctx, ada_w, ada_b, norm_g, ffn_w_up, ffn_conv_w, ffn_conv_b, ffn_w_down, cm_w_in, cm_b_in, cm_dw_w, cm_dw_b, cm_ln_g, cm_ln_b, cm_w_out, cm_b_out, attn_w_qkv, attn_sink, attn_w_o, gm_w_in, gm_b_in, gm_ln_g, gm_ln_b, gm_w_s, gm_b_s, gm_w_out, loss_target, m_c_ctx, m_ada_w, m_ada_b, m_norm_g, m_ffn_w_up, m_ffn_conv_w, m_ffn_conv_b, m_ffn_w_down, m_cm_w_in, m_cm_b_in, m_cm_dw_w, m_cm_dw_b, m_cm_ln_g, m_cm_ln_b, m_cm_w_out, m_cm_b_out, m_attn_w_qkv, m_attn_sink, m_attn_w_o, m_gm_w_in, m_gm_b_in, m_gm_ln_g, m_gm_ln_b, m_gm_w_s, m_gm_b_s, m_gm_w_out, v_c_ctx, v_ada_w, v_ada_b, v_norm_g, v_ffn_w_up, v_ffn_conv_w, v_ffn_conv_b, v_ffn_w_down, v_cm_w_in, v_cm_b_in, v_cm_dw_w, v_cm_dw_b, v_cm_ln_g, v_cm_ln_b, v_cm_w_out, v_cm_b_out, v_attn_w_qkv, v_attn_sink, v_attn_w_o, v_gm_w_in, v_gm_b_in, v_gm_ln_g, v_gm_ln_b, v_gm_w_s, v_gm_b_s, v_gm_w_out):
    args = locals()
    W = {n: args[n] for n in WEIGHTS}
    M = {n: args["m_" + n] for n in WEIGHTS}
    V = {n: args["v_" + n] for n in WEIGHTS}
    return _step(x, c, ctx, loss_target, W, M, V)
```

```python
import functools
import math

import jax
import jax.numpy as jnp
from jax import lax
from jax.experimental import pallas as pl
from jax.experimental.pallas import tpu as pltpu

F32 = jnp.float32
MMT = jnp.bfloat16
SDS = jax.ShapeDtypeStruct
MESH = pl.DeviceIdType.MESH

EPS = 1e-6
HEAD_DIM = 64
Q_PER_KV = 4
ATTN_BLOCK = 128
GRID_W = 64
ROPE_BASE = 10000.0
GMLP_CHUNK = 128
GMLP_GROUP_DIM = 128
CONV_WIDTH = 31
FFN_CONV_WIDTH = 3
NEG = -1e30

ADAM_LR, ADAM_B1, ADAM_B2, ADAM_EPS, ADAM_WD, ADAM_STEP = 0.001, 0.9, 0.999, 1e-08, 0.01, 10

LANES = 128
SUBLANES = 8
VMEM_LIMIT = 52 * 1024 * 1024
CONV_ROWS = 128
N_CHIPS = 4
N_DEV = 8


def _cparams(*sem):
    return pltpu.CompilerParams(dimension_semantics=sem if sem else None, vmem_limit_bytes=VMEM_LIMIT)


def _tile(n, cap, mult=LANES):
    best = None
    for d in range(mult, min(n, cap) + 1, mult):
        if n % d == 0:
            best = d
    return best if best is not None else n


def _sum0(v):
    return jnp.sum(v, axis=0, keepdims=True)


def _rms(v):
    r = lax.rsqrt(jnp.mean(v * v, axis=-1, keepdims=True) + EPS)
    return v * r, r


def _sig(v):
    return jax.nn.sigmoid(v)


def _dot(a, b, ca, cb):
    return lax.dot_general(a.astype(MMT), b.astype(MMT), (((ca,), (cb,)), ((), ())), preferred_element_type=F32)


def _mm(a, b, mode, out_dtype, name):
    if mode == "nn":
        (M, K), N = a.shape, b.shape[1]
    elif mode == "nt":
        (M, K), N = a.shape, b.shape[0]
    else:
        (K, M), N = a.shape, b.shape[1]
    tm, tn, tk = _tile(M, 512), _tile(N, 1408), _tile(K, 1536)
    nk = K // tk
    ca, cb = {"nn": (1, 0), "nt": (1, 1), "tn": (0, 0)}[mode]

    def body(a_ref, b_ref, o_ref, acc):
        k = pl.program_id(2)

        @pl.when(k == 0)
        def _():
            acc[...] = jnp.zeros_like(acc)

        acc[...] += _dot(a_ref[...], b_ref[...], ca, cb)

        @pl.when(k == nk - 1)
        def _():
            o_ref[...] = acc[...].astype(o_ref.dtype)

    a_spec = pl.BlockSpec((tk, tm), lambda i, j, k: (k, i)) if mode == "tn" else pl.BlockSpec((tm, tk), lambda i, j, k: (i, k))
    b_spec = pl.BlockSpec((tn, tk), lambda i, j, k: (j, k)) if mode == "nt" else pl.BlockSpec((tk, tn), lambda i, j, k: (k, j))
    return pl.pallas_call(
        body, name=name, grid=(M // tm, N // tn, nk), in_specs=[a_spec, b_spec],
        out_specs=pl.BlockSpec((tm, tn), lambda i, j, k: (i, j)), out_shape=SDS((M, N), out_dtype),
        scratch_shapes=[pltpu.VMEM((tm, tn), F32)], compiler_params=_cparams("parallel", "parallel", "arbitrary"))(a, b)


def _seg_of(nl, nseg):
    return (lambda i: jnp.where(i >= nl, 1, 0)) if nseg == 2 else (lambda i: 0)


def _prenorm(h, mod, gn, which, rows, nl, tm, name):
    D = h.shape[1]
    nseg = mod.shape[0]
    seg = _seg_of(nl, nseg)
    sh_i, sc_i = (0, 1) if which == 0 else (3, 4)

    def body(h_ref, mod_ref, gn_ref, a_ref):
        n, _ = _rms(h_ref[...])
        a_ref[...] = (n * gn_ref[...] * (1.0 + mod_ref[pl.ds(sc_i, 1), :]) + mod_ref[pl.ds(sh_i, 1), :]).astype(a_ref.dtype)

    return pl.pallas_call(
        body, name=name, grid=(rows // tm,),
        in_specs=[pl.BlockSpec((tm, D), lambda i: (i, 0)), pl.BlockSpec((None, 6, D), lambda i: (seg(i), 0, 0)),
                  pl.BlockSpec((1, D), lambda i: (0, 0))],
        out_specs=pl.BlockSpec((tm, D), lambda i: (i, 0)), out_shape=SDS((rows, D), MMT),
        compiler_params=_cparams("parallel"))(h, mod, gn)


def _acc_spec(D, seg):
    return pl.BlockSpec((None, 1, D), lambda i: (seg(i), 0, 0))


def _prenorm_bwd(h, da, dh_in, mod, gn, which, rows, nl, tm, name):
    D = h.shape[1]
    nseg = mod.shape[0]
    seg = _seg_of(nl, nseg)
    sc_i = 1 if which == 0 else 4

    def body(h_ref, da_ref, dhin_ref, mod_ref, gn_ref, dh_ref, dsh_ref, dsc_ref, dgn_ref):
        i = pl.program_id(0)
        first = (i == 0) | (i == nl) if nseg == 2 else (i == 0)

        @pl.when(first)
        def _():
            dsh_ref[...] = jnp.zeros_like(dsh_ref)
            dsc_ref[...] = jnp.zeros_like(dsc_ref)
            dgn_ref[...] = jnp.zeros_like(dgn_ref)

        n, r = _rms(h_ref[...])
        da_v = da_ref[...].astype(F32)
        gn_v = gn_ref[...]
        sc1 = 1.0 + mod_ref[pl.ds(sc_i, 1), :]
        dsh_ref[...] += _sum0(da_v)
        dsc_ref[...] += _sum0(da_v * (n * gn_v))
        dgn_ref[...] += _sum0(da_v * n * sc1)
        dn = da_v * (gn_v * sc1)
        dh_ref[...] = dhin_ref[...] + r * (dn - n * jnp.mean(dn * n, axis=-1, keepdims=True))

    row = pl.BlockSpec((tm, D), lambda i: (i, 0))
    acc = SDS((nseg, 1, D), F32)
    return pl.pallas_call(
        body, name=name, grid=(rows // tm,),
        in_specs=[row, row, row, pl.BlockSpec((None, 6, D), lambda i: (seg(i), 0, 0)), pl.BlockSpec((1, D), lambda i: (0, 0))],
        out_specs=[row, _acc_spec(D, seg), _acc_spec(D, seg), _acc_spec(D, seg)],
        out_shape=[SDS((rows, D), F32), acc, acc, acc], compiler_params=_cparams("arbitrary"))(h, da, dh_in, mod, gn)


def _postnorm(h, y, bias, mod, gn, gate_i, rows, nl, tm, name):
    D = h.shape[1]
    nseg = mod.shape[0]
    seg = _seg_of(nl, nseg)

    def body(h_ref, y_ref, b_ref, mod_ref, gn_ref, o_ref):
        ny, _ = _rms(y_ref[...] + b_ref[...])
        o_ref[...] = h_ref[...] + mod_ref[pl.ds(gate_i, 1), :] * (ny * gn_ref[...])

    row = pl.BlockSpec((tm, D), lambda i: (i, 0))
    vec = pl.BlockSpec((1, D), lambda i: (0, 0))
    return pl.pallas_call(
        body, name=name, grid=(rows // tm,),
        in_specs=[row, row, vec, pl.BlockSpec((None, 6, D), lambda i: (seg(i), 0, 0)), vec],
        out_specs=row, out_shape=SDS((rows, D), F32), compiler_params=_cparams("parallel"))(h, y, bias, mod, gn)


def _postnorm_bwd(dh, y, bias, mod, gn, gate_i, rows, nl, tm, name):
    D = y.shape[1]
    nseg = mod.shape[0]
    seg = _seg_of(nl, nseg)

    def body(dh_ref, y_ref, b_ref, mod_ref, gn_ref, dy_ref, dg_ref, dgn_ref, db_ref):
        i = pl.program_id(0)
        first = (i == 0) | (i == nl) if nseg == 2 else (i == 0)

        @pl.when(first)
        def _():
            dg_ref[...] = jnp.zeros_like(dg_ref)
            dgn_ref[...] = jnp.zeros_like(dgn_ref)
            db_ref[...] = jnp.zeros_like(db_ref)

        ny, ry = _rms(y_ref[...] + b_ref[...])
        g = mod_ref[pl.ds(gate_i, 1), :]
        gn_v = gn_ref[...]
        dh_v = dh_ref[...]
        dg_ref[...] += _sum0(dh_v * (ny * gn_v))
        dgn_ref[...] += _sum0(dh_v * ny * g)
        dny = dh_v * (g * gn_v)
        dy = ry * (dny - ny * jnp.mean(dny * ny, axis=-1, keepdims=True))
        db_ref[...] += _sum0(dy)
        dy_ref[...] = dy.astype(dy_ref.dtype)

    row = pl.BlockSpec((tm, D), lambda i: (i, 0))
    vec = pl.BlockSpec((1, D), lambda i: (0, 0))
    acc = SDS((nseg, 1, D), F32)
    return pl.pallas_call(
        body, name=name, grid=(rows // tm,),
        in_specs=[row, row, vec, pl.BlockSpec((None, 6, D), lambda i: (seg(i), 0, 0)), vec],
        out_specs=[row, _acc_spec(D, seg), _acc_spec(D, seg), _acc_spec(D, seg)],
        out_shape=[SDS((rows, D), MMT), acc, acc, acc], compiler_params=_cparams("arbitrary"))(dh, y, bias, mod, gn)


def _seg_layout(segs, H):
    out, base = [], H
    for s0, n in segs:
        out.append((s0, n, base))
        base += n + H
    return out, base


def _zero_pads(ref, lay, H):
    width = ref.shape[1]
    ref[pl.ds(0, H), :] = jnp.zeros((H, width), ref.dtype)
    for _, n, base in lay:
        ref[pl.ds(base + n, H), :] = jnp.zeros((H, width), ref.dtype)


def _window(ref, base, off, H):
    return ref[pl.ds(base - H + off, CONV_ROWS + 2 * H), :]


def _taps(win, H, offs):
    W = CONV_ROWS + 2 * H
    rolled, out = {}, {}
    for o in offs:
        s = H + o
        b = s % SUBLANES
        if b not in rolled:
            rolled[b] = win if b == 0 else pltpu.roll(win, shift=W - b, axis=0)
        out[o] = rolled[b][s - b:s - b + CONV_ROWS, :]
    return out


def _chunks(lay, fn):
    for s0, n, base in lay:
        def step(r, carry, s0=s0, base=base):
            fn(s0, base, pl.multiple_of(r * CONV_ROWS, CONV_ROWS))
            return carry
        lax.fori_loop(0, n // CONV_ROWS, step, 0)


def _ffn_gate(z0, conv_w, conv_b, segs, name):
    T, F2 = z0.shape
    F = F2 // 2
    tc = _tile(F, 256)
    nF = F // tc
    H = SUBLANES
    lay, srows = _seg_layout(segs, H)
    offs = [-1, 0, 1]

    def body(zg_ref, zv_ref, wg_ref, wv_ref, bg_ref, bv_ref, u_ref, xg, xv):
        _zero_pads(xg, lay, H)
        _zero_pads(xv, lay, H)
        for s0, n, base in lay:
            xg[pl.ds(base, n), :] = zg_ref[pl.ds(s0, n), :]
            xv[pl.ds(base, n), :] = zv_ref[pl.ds(s0, n), :]

        def chunk(s0, base, off):
            tg = _taps(_window(xg, base, off, H), H, offs)
            tv = _taps(_window(xv, base, off, H), H, offs)
            zg = bg_ref[...] + sum(tg[k - 1] * wg_ref[pl.ds(k, 1), :] for k in range(3))
            zv = bv_ref[...] + sum(tv[k - 1] * wv_ref[pl.ds(k, 1), :] for k in range(3))
            u_ref[pl.ds(s0 + off, CONV_ROWS), :] = (zg * _sig(zg) * zv).astype(u_ref.dtype)

        _chunks(lay, chunk)

    colg = lambda r: pl.BlockSpec((r, tc), lambda j: (0, j))
    colv = lambda r: pl.BlockSpec((r, tc), lambda j: (0, j + nF))
    return pl.pallas_call(
        body, name=name, grid=(nF,),
        in_specs=[colg(T), colv(T), colg(3), colv(3), colg(1), colv(1)],
        out_specs=colg(T), out_shape=SDS((T, F), MMT),
        scratch_shapes=[pltpu.VMEM((srows, tc), F32), pltpu.VMEM((srows, tc), F32)],
        compiler_params=_cparams("parallel"))(z0, z0, conv_w, conv_w, conv_b, conv_b)


def _ffn_gate_bwd(z0, du, conv_w, conv_b, segs, name):
    T, F2 = z0.shape
    F = F2 // 2
    tc = _tile(F, 256)
    nF = F // tc
    H = SUBLANES
    lay, srows = _seg_layout(segs, H)
    offs = [-1, 0, 1]

    def body(zo_ref, zt_ref, du_ref, wo_ref, wt_ref, bo_ref, bt_ref, u_ref, dz0_ref, dw_ref, db_ref, xo, xt, dzp):
        own_is_gate = pl.program_id(1) == 0
        for ref in (xo, xt, dzp):
            _zero_pads(ref, lay, H)
        for s0, n, base in lay:
            xo[pl.ds(base, n), :] = zo_ref[pl.ds(s0, n), :]
            xt[pl.ds(base, n), :] = zt_ref[pl.ds(s0, n), :]

        def grads(s0, base, off):
            to = _taps(_window(xo, base, off, H), H, offs)
            tt = _taps(_window(xt, base, off, H), H, offs)
            zo = bo_ref[...] + sum(to[k - 1] * wo_ref[pl.ds(k, 1), :] for k in range(3))
            zt = bt_ref[...] + sum(tt[k - 1] * wt_ref[pl.ds(k, 1), :] for k in range(3))
            so, st = _sig(zo), _sig(zt)
            du_v = du_ref[pl.ds(s0 + off, CONV_ROWS), :]
            d_gate = du_v * zt * (so * (1.0 + zo * (1.0 - so)))
            d_val = du_v * (zt * st)
            dzp[pl.ds(base + off, CONV_ROWS), :] = jnp.where(own_is_gate, d_gate, d_val)

            @pl.when(own_is_gate)
            def _():
                u_ref[pl.ds(s0 + off, CONV_ROWS), :] = (zo * so * zt).astype(u_ref.dtype)

        _chunks(lay, grads)
        dw_ref[...] = jnp.zeros_like(dw_ref)
        db_ref[...] = jnp.zeros_like(db_ref)

        def back(s0, base, off):
            td = _taps(_window(dzp, base, off, H), H, offs)
            tx = _taps(_window(xo, base, off, H), H, offs)
            dz0 = sum(td[1 - k] * wo_ref[pl.ds(k, 1), :] for k in range(3))
            dz0_ref[pl.ds(s0 + off, CONV_ROWS), :] = dz0.astype(dz0_ref.dtype)
            db_ref[...] += _sum0(td[0])
            for k in range(3):
                dw_ref[pl.ds(k, 1), :] += _sum0(td[0] * tx[k - 1])

        _chunks(lay, back)

    own = lambda r: pl.BlockSpec((r, tc), lambda j, hf: (0, hf * nF + j))
    oth = lambda r: pl.BlockSpec((r, tc), lambda j, hf: (0, (1 - hf) * nF + j))
    ucol = pl.BlockSpec((T, tc), lambda j, hf: (0, j))
    return pl.pallas_call(
        body, name=name, grid=(nF, 2),
        in_specs=[own(T), oth(T), ucol, own(3), oth(3), own(1), oth(1)],
        out_specs=[ucol, own(T), own(3), own(1)],
        out_shape=[SDS((T, F), MMT), SDS((T, F2), MMT), SDS((3, F2), F32), SDS((1, F2), F32)],
        scratch_shapes=[pltpu.VMEM((srows, tc), F32)] * 3,
        compiler_params=_cparams("parallel", "arbitrary"))(z0, z0, du, conv_w, conv_w, conv_b, conv_b)


def _glu_conv(p0, b_in, dw_w, dw_b, segs, name):
    T, D2 = p0.shape
    D = D2 // 2
    tc = _tile(D, 256)
    nD = D // tc
    H = 2 * SUBLANES
    half = (CONV_WIDTH - 1) // 2
    lay, srows = _seg_layout(segs, H)
    offs = list(range(-half, half + 1))

    def body(pa_ref, pg_ref, ba_ref, bg_ref, w_ref, b_ref, z2_ref, z1p):
        _zero_pads(z1p, lay, H)

        def glu(s0, base, off):
            rows = pl.ds(s0 + off, CONV_ROWS)
            z1p[pl.ds(base + off, CONV_ROWS), :] = (pa_ref[rows, :] + ba_ref[...]) * _sig(pg_ref[rows, :] + bg_ref[...])

        _chunks(lay, glu)

        def conv(s0, base, off):
            t = _taps(_window(z1p, base, off, H), H, offs)
            acc = b_ref[...] + t[-half] * w_ref[pl.ds(0, 1), :]
            for k in range(1, CONV_WIDTH):
                acc = acc + t[k - half] * w_ref[pl.ds(k, 1), :]
            z2_ref[pl.ds(s0 + off, CONV_ROWS), :] = acc

        _chunks(lay, conv)

    cola = lambda r: pl.BlockSpec((r, tc), lambda j: (0, j))
    colg = lambda r: pl.BlockSpec((r, tc), lambda j: (0, j + nD))
    return pl.pallas_call(
        body, name=name, grid=(nD,),
        in_specs=[cola(T), colg(T), cola(1), colg(1), cola(CONV_WIDTH), cola(1)],
        out_specs=cola(T), out_shape=SDS((T, D), F32), scratch_shapes=[pltpu.VMEM((srows, tc), F32)],
        compiler_params=_cparams("parallel"))(p0, p0, b_in, b_in, dw_w, dw_b)


def _glu_conv_bwd(p0, b_in, dw_w, dz2, segs, name):
    T, D2 = p0.shape
    D = D2 // 2
    tc = _tile(D, 256)
    nD = D // tc
    H = 2 * SUBLANES
    half = (CONV_WIDTH - 1) // 2
    lay, srows = _seg_layout(segs, H)
    offs = list(range(-half, half + 1))

    def body(pa_ref, pg_ref, ba_ref, bg_ref, w_ref, dz2_ref, dpa_ref, dpg_ref, dw_ref, db_ref, dba_ref, dbg_ref, z1p, dzp):
        _zero_pads(z1p, lay, H)
        _zero_pads(dzp, lay, H)
        for s0, n, base in lay:
            dzp[pl.ds(base, n), :] = dz2_ref[pl.ds(s0, n), :]

        def glu(s0, base, off):
            rows = pl.ds(s0 + off, CONV_ROWS)
            z1p[pl.ds(base + off, CONV_ROWS), :] = (pa_ref[rows, :] + ba_ref[...]) * _sig(pg_ref[rows, :] + bg_ref[...])

        _chunks(lay, glu)
        for ref in (dw_ref, db_ref, dba_ref, dbg_ref):
            ref[...] = jnp.zeros_like(ref)

        def back(s0, base, off):
            td = _taps(_window(dzp, base, off, H), H, offs)
            tz = _taps(_window(z1p, base, off, H), H, offs)
            dz1 = td[half] * w_ref[pl.ds(0, 1), :]
            for k in range(1, CONV_WIDTH):
                dz1 = dz1 + td[half - k] * w_ref[pl.ds(k, 1), :]
            db_ref[...] += _sum0(td[0])
            for k in range(CONV_WIDTH):
                dw_ref[pl.ds(k, 1), :] += _sum0(td[0] * tz[k - half])
            rows = pl.ds(s0 + off, CONV_ROWS)
            pa = pa_ref[rows, :] + ba_ref[...]
            sg = _sig(pg_ref[rows, :] + bg_ref[...])
            dpa = dz1 * sg
            dpg = dz1 * pa * (sg * (1.0 - sg))
            dba_ref[...] += _sum0(dpa)
            dbg_ref[...] += _sum0(dpg)
            dpa_ref[rows, :] = dpa.astype(dpa_ref.dtype)
            dpg_ref[rows, :] = dpg.astype(dpg_ref.dtype)

        _chunks(lay, back)

    cola = lambda r: pl.BlockSpec((r, tc), lambda j: (0, j))
    colg = lambda r: pl.BlockSpec((r, tc), lambda j: (0, j + nD))
    return pl.pallas_call(
        body, name=name, grid=(nD,),
        in_specs=[cola(T), colg(T), cola(1), colg(1), cola(CONV_WIDTH), cola(T)],
        out_specs=[cola(T), cola(T), cola(CONV_WIDTH), cola(1), cola(1), cola(1)],
        out_shape=[SDS((T, D), MMT), SDS((T, D), MMT), SDS((CONV_WIDTH, D), F32), SDS((1, D), F32), SDS((1, D), F32),
                   SDS((1, D), F32)],
        scratch_shapes=[pltpu.VMEM((srows, tc), F32)] * 2, compiler_params=_cparams("parallel"))(p0, p0, b_in, b_in, dw_w, dz2)


def _layer_norm_stats(v):
    mu = jnp.mean(v, axis=-1, keepdims=True)
    var = jnp.mean(jnp.square(v - mu), axis=-1, keepdims=True)
    rstd = lax.rsqrt(var + EPS)
    return (v - mu) * rstd, rstd


def _ln_silu(z2, ln_g, ln_b, rows, tm, name):
    D = z2.shape[1]

    def body(z_ref, g_ref, b_ref, o_ref):
        xh, _ = _layer_norm_stats(z_ref[...])
        z3 = xh * g_ref[...] + b_ref[...]
        o_ref[...] = (z3 * _sig(z3)).astype(o_ref.dtype)

    row = pl.BlockSpec((tm, D), lambda i: (i, 0))
    vec = pl.BlockSpec((1, D), lambda i: (0, 0))
    return pl.pallas_call(body, name=name, grid=(rows // tm,), in_specs=[row, vec, vec], out_specs=row,
                          out_shape=SDS((rows, D), MMT), compiler_params=_cparams("parallel"))(z2, ln_g, ln_b)


def _ln_silu_bwd(z2, dz4, ln_g, ln_b, rows, tm, name):
    D = z2.shape[1]

    def body(z_ref, d_ref, g_ref, b_ref, dz_ref, dg_ref, db_ref):
        @pl.when(pl.program_id(0) == 0)
        def _():
            dg_ref[...] = jnp.zeros_like(dg_ref)
            db_ref[...] = jnp.zeros_like(db_ref)

        xh, rstd = _layer_norm_stats(z_ref[...])
        z3 = xh * g_ref[...] + b_ref[...]
        s = _sig(z3)
        dz3 = d_ref[...] * (s * (1.0 + z3 * (1.0 - s)))
        dg_ref[...] += _sum0(dz3 * xh)
        db_ref[...] += _sum0(dz3)
        dxh = dz3 * g_ref[...]
        dz_ref[...] = rstd * (dxh - jnp.mean(dxh, axis=-1, keepdims=True) - xh * jnp.mean(dxh * xh, axis=-1, keepdims=True))

    row = pl.BlockSpec((tm, D), lambda i: (i, 0))
    vec = pl.BlockSpec((1, D), lambda i: (0, 0))
    return pl.pallas_call(body, name=name, grid=(rows // tm,), in_specs=[row, row, vec, vec], out_specs=[row, vec, vec],
                          out_shape=[SDS((rows, D), F32), SDS((1, D), F32), SDS((1, D), F32)],
                          compiler_params=_cparams("arbitrary"))(z2, dz4, ln_g, ln_b)


def _rot_half_pairs(v):
    width = v.shape[1]
    lane = lax.broadcasted_iota(jnp.int32, v.shape, 1)
    return jnp.where((lane % 32) < 16, -pltpu.roll(v, shift=width - 16, axis=1), pltpu.roll(v, shift=16, axis=1))


def _rope(qkv, cos, sin, L, qk, tm, name):
    width = qkv.shape[1]
    kv = width - qk

    def body(x_ref, c_ref, s_ref, qk_ref, v_ref):
        xv = x_ref[:, pl.ds(0, qk)]
        c = jnp.tile(c_ref[...], (1, qk // LANES))
        s = jnp.tile(s_ref[...], (1, qk // LANES))
        qk_ref[...] = (xv * c + _rot_half_pairs(xv) * s).astype(qk_ref.dtype)
        v_ref[...] = x_ref[:, pl.ds(qk, kv)].astype(v_ref.dtype)

    tab = pl.BlockSpec((tm, LANES), lambda i: (i, 0))
    return pl.pallas_call(
        body, name=name, grid=(L // tm,), in_specs=[pl.BlockSpec((tm, width), lambda i: (i, 0)), tab, tab],
        out_specs=[pl.BlockSpec((tm, qk), lambda i: (i, 0)), pl.BlockSpec((tm, kv), lambda i: (i, 0))],
        out_shape=[SDS((L, qk), MMT), SDS((L, kv), MMT)], compiler_params=_cparams("parallel"))(qkv, cos, sin)


def _rope_bwd(dqk, dv, cos, sin, tm, name):
    L, qk = dqk.shape
    kv = dv.shape[1]

    def body(d_ref, dv_ref, c_ref, s_ref, o_ref):
        dv_ = d_ref[...]
        c = jnp.tile(c_ref[...], (1, qk // LANES))
        s = jnp.tile(s_ref[...], (1, qk // LANES))
        o_ref[:, pl.ds(0, qk)] = (dv_ * c - _rot_half_pairs(dv_ * s)).astype(o_ref.dtype)
        o_ref[:, pl.ds(qk, kv)] = dv_ref[...].astype(o_ref.dtype)

    tab = pl.BlockSpec((tm, LANES), lambda i: (i, 0))
    return pl.pallas_call(
        body, name=name, grid=(L // tm,),
        in_specs=[pl.BlockSpec((tm, qk), lambda i: (i, 0)), pl.BlockSpec((tm, kv), lambda i: (i, 0)), tab, tab],
        out_specs=pl.BlockSpec((tm, qk + kv), lambda i: (i, 0)), out_shape=SDS((L, qk + kv), MMT),
        compiler_params=_cparams("parallel"))(dqk, dv, cos, sin)


def _band_specs(nb, width):
    blk = lambda f: pl.BlockSpec((None, ATTN_BLOCK, width), f)
    return [blk(lambda h, n: (h, jnp.maximum(n - 1, 0), 0)), blk(lambda h, n: (h, n, 0)),
            blk(lambda h, n: (h, jnp.minimum(n + 1, nb - 1), 0))]


def _window_mask(n, L):
    qi = lax.broadcasted_iota(jnp.int32, (ATTN_BLOCK, 3 * ATTN_BLOCK), 0)
    kk = lax.broadcasted_iota(jnp.int32, (ATTN_BLOCK, 3 * ATTN_BLOCK), 1)
    key_abs = (n - 1) * ATTN_BLOCK + kk
    return (jnp.abs(qi + ATTN_BLOCK - kk) <= ATTN_BLOCK) & (key_abs >= 0) & (key_abs < L)


def _attn_fwd(q, k, v, kc, vc, sink, name):
    nkv, _, L, hd = q.shape
    C = kc.shape[1]
    nb = L // ATTN_BLOCK
    scale = HEAD_DIM ** -0.5

    def body(sink_ref, q_ref, k0, k1, k2, v0, v1, v2, kc_ref, vc_ref, o_ref, lse_ref):
        hh, n = pl.program_id(0), pl.program_id(1)
        kw = jnp.concatenate([k0[...], k1[...], k2[...]], axis=0)
        vw = jnp.concatenate([v0[...], v1[...], v2[...]], axis=0)
        mask = _window_mask(n, L)
        for g in range(Q_PER_KV):
            qg = q_ref[g]
            sw = jnp.where(mask, _dot(qg, kw, 1, 1) * scale, NEG)
            sc = _dot(qg, kc_ref[...], 1, 1) * scale
            sk = sink_ref[hh * Q_PER_KV + g]
            m = jnp.maximum(jnp.maximum(jnp.max(sw, axis=-1, keepdims=True), jnp.max(sc, axis=-1, keepdims=True)), sk)
            pw, pc = jnp.exp(sw - m), jnp.exp(sc - m)
            den = jnp.sum(pw, axis=-1, keepdims=True) + jnp.sum(pc, axis=-1, keepdims=True) + jnp.exp(sk - m)
            inv = 1.0 / den
            o_ref[g] = _dot(pw * inv, vw, 1, 0) + _dot(pc * inv, vc_ref[...], 1, 0)
            lse_ref[g] = m + jnp.log(den)

    qspec = pl.BlockSpec((None, Q_PER_KV, ATTN_BLOCK, hd), lambda h, n: (h, 0, n, 0))
    cspec = pl.BlockSpec((None, C, hd), lambda h, n: (h, 0, 0))
    return pl.pallas_call(
        body, name=name, grid=(nkv, nb),
        in_specs=[pl.BlockSpec(memory_space=pltpu.SMEM), qspec] + _band_specs(nb, hd) + _band_specs(nb, hd) + [cspec, cspec],
        out_specs=[qspec, pl.BlockSpec((None, Q_PER_KV, ATTN_BLOCK, 1), lambda h, n: (h, 0, n, 0))],
        out_shape=[SDS((nkv, Q_PER_KV, L, hd), F32), SDS((nkv, Q_PER_KV, L, 1), F32)],
        compiler_params=_cparams("parallel", "parallel"))(sink, q, k, k, k, v, v, v, kc, vc)


def _attn_bwd_q(q, k, v, kc, vc, sink, o, do, lse, name):
    nkv, _, L, hd = q.shape
    C = kc.shape[1]
    nb = L // ATTN_BLOCK
    scale = HEAD_DIM ** -0.5

    def body(sink_ref, q_ref, k0, k1, k2, v0, v1, v2, kc_ref, vc_ref, o_ref, do_ref, lse_ref, dq_ref, dkc_ref, dvc_ref, dsk_ref):
        hh, n = pl.program_id(0), pl.program_id(1)

        @pl.when(n == 0)
        def _():
            dkc_ref[...] = jnp.zeros_like(dkc_ref)
            dvc_ref[...] = jnp.zeros_like(dvc_ref)
            dsk_ref[...] = jnp.zeros_like(dsk_ref)

        kw = jnp.concatenate([k0[...], k1[...], k2[...]], axis=0)
        vw = jnp.concatenate([v0[...], v1[...], v2[...]], axis=0)
        mask = _window_mask(n, L)
        for g in range(Q_PER_KV):
            qg, dog, lse_g = q_ref[g], do_ref[g], lse_ref[g]
            delta = jnp.sum(dog.astype(F32) * o_ref[g], axis=-1, keepdims=True)
            pw = jnp.exp(jnp.where(mask, _dot(qg, kw, 1, 1) * scale, NEG) - lse_g)
            pc = jnp.exp(_dot(qg, kc_ref[...], 1, 1) * scale - lse_g)
            dsw = pw * (_dot(dog, vw, 1, 1) - delta)
            dsc = pc * (_dot(dog, vc_ref[...], 1, 1) - delta)
            dq_ref[g] = (_dot(dsw, kw, 1, 0) + _dot(dsc, kc_ref[...], 1, 0)) * scale
            dkc_ref[...] += _dot(dsc, qg, 0, 0) * scale
            dvc_ref[...] += _dot(pc, dog, 0, 0)
            psk = jnp.exp(sink_ref[hh * Q_PER_KV + g] - lse_g)
            dsk_ref[pl.ds(g, 1), :] += jnp.broadcast_to(jnp.sum(-psk * delta, axis=0, keepdims=True), (1, LANES))

    qspec = pl.BlockSpec((None, Q_PER_KV, ATTN_BLOCK, hd), lambda h, n: (h, 0, n, 0))
    lspec = pl.BlockSpec((None, Q_PER_KV, ATTN_BLOCK, 1), lambda h, n: (h, 0, n, 0))
    cspec = pl.BlockSpec((None, C, hd), lambda h, n: (h, 0, 0))
    return pl.pallas_call(
        body, name=name, grid=(nkv, nb),
        in_specs=[pl.BlockSpec(memory_space=pltpu.SMEM), qspec] + _band_specs(nb, hd) + _band_specs(nb, hd)
        + [cspec, cspec, qspec, qspec, lspec],
        out_specs=[qspec, cspec, cspec, pl.BlockSpec((None, SUBLANES, LANES), lambda h, n: (h, 0, 0))],
        out_shape=[SDS((nkv, Q_PER_KV, L, hd), F32), SDS((nkv, C, hd), F32), SDS((nkv, C, hd), F32),
                   SDS((nkv, SUBLANES, LANES), F32)],
        compiler_params=_cparams("parallel", "arbitrary"))(sink, q, k, k, k, v, v, v, kc, vc, o, do, lse)


def _attn_bwd_kv(q, k, v, o, do, lse, name):
    nkv, _, L, hd = q.shape
    nb = L // ATTN_BLOCK
    scale = HEAD_DIM ** -0.5

    def body(q0, q1, q2, do0, do1, do2, o0, o1, o2, l0, l1, l2, k_ref, v_ref, dk_ref, dv_ref):
        j = pl.program_id(1)
        qi = lax.broadcasted_iota(jnp.int32, (ATTN_BLOCK, ATTN_BLOCK), 0)
        kk = lax.broadcasted_iota(jnp.int32, (ATTN_BLOCK, ATTN_BLOCK), 1)
        kj, vj = k_ref[...], v_ref[...]
        dk = jnp.zeros((ATTN_BLOCK, hd), F32)
        dv = jnp.zeros((ATTN_BLOCK, hd), F32)
        for slot, (q_r, do_r, o_r, l_r) in enumerate(((q0, do0, o0, l0), (q1, do1, o1, l1), (q2, do2, o2, l2))):
            n = j - 1 + slot
            ok = (n >= 0) & (n < nb) & (jnp.abs(qi + ATTN_BLOCK - ((2 - slot) * ATTN_BLOCK + kk)) <= ATTN_BLOCK)
            for g in range(Q_PER_KV):
                qg, dog = q_r[g], do_r[g]
                delta = jnp.sum(dog.astype(F32) * o_r[g], axis=-1, keepdims=True)
                p = jnp.exp(jnp.where(ok, _dot(qg, kj, 1, 1) * scale - l_r[g], NEG))
                ds = p * (_dot(dog, vj, 1, 1) - delta)
                dk = dk + _dot(ds, qg, 0, 0) * scale
                dv = dv + _dot(p, dog, 0, 0)
        dk_ref[...] = dk
        dv_ref[...] = dv

    def band(width):
        blk = lambda f: pl.BlockSpec((None, Q_PER_KV, ATTN_BLOCK, width), f)
        return [blk(lambda h, j: (h, 0, jnp.maximum(j - 1, 0), 0)), blk(lambda h, j: (h, 0, j, 0)),
                blk(lambda h, j: (h, 0, jnp.minimum(j + 1, nb - 1), 0))]

    kspec = pl.BlockSpec((None, ATTN_BLOCK, hd), lambda h, j: (h, j, 0))
    return pl.pallas_call(
        body, name=name, grid=(nkv, nb), in_specs=band(hd) + band(hd) + band(hd) + band(1) + [kspec, kspec],
        out_specs=[kspec, kspec], out_shape=[SDS((nkv, L, hd), F32), SDS((nkv, L, hd), F32)],
        compiler_params=_cparams("parallel", "parallel"))(q, q, q, do, do, do, o, o, o, lse, lse, lse, k, v)


_GELU_K = math.sqrt(2.0 / math.pi)


def _gelu(v):
    return 0.5 * v * (1.0 + jnp.tanh(_GELU_K * (v + 0.044715 * (v * v * v))))


def _gelu_grad(v):
    t = jnp.tanh(_GELU_K * (v + 0.044715 * (v * v * v)))
    return 0.5 * (1.0 + t) + 0.5 * v * (1.0 - t * t) * (_GELU_K * (1.0 + 3.0 * 0.044715 * (v * v)))


def _gmlp_fwd(p0, b_in, ln_g, ln_b, w_s, b_s, name):
    L, W2 = p0.shape
    W = W2 // 2
    G = W // GMLP_GROUP_DIM

    def body(p_ref, bi_ref, g_ref, b_ref, ws_ref, bs_ref, o_ref):
        ge = _gelu(p_ref[...] + bi_ref[...])
        xh, _ = _layer_norm_stats(ge[:, W:])
        vln = xh * g_ref[...] + b_ref[...]
        for gi in range(G):
            cols = slice(gi * GMLP_GROUP_DIM, (gi + 1) * GMLP_GROUP_DIM)
            s = _dot(ws_ref[gi], vln[:, cols], 1, 0) + bs_ref[gi]
            o_ref[:, cols] = (ge[:, cols] * s).astype(o_ref.dtype)

    full = lambda shape: pl.BlockSpec(shape, lambda i: (0,) * len(shape))
    return pl.pallas_call(
        body, name=name, grid=(L // GMLP_CHUNK,),
        in_specs=[pl.BlockSpec((GMLP_CHUNK, W2), lambda i: (i, 0)), full((1, W2)), full((1, W)), full((1, W)),
                  full((G, GMLP_CHUNK, GMLP_CHUNK)), full((G, GMLP_CHUNK, 1))],
        out_specs=pl.BlockSpec((GMLP_CHUNK, W), lambda i: (i, 0)), out_shape=SDS((L, W), MMT),
        compiler_params=_cparams("parallel"))(p0, b_in, ln_g, ln_b, w_s, b_s)


def _gmlp_bwd(p0, dus, b_in, ln_g, ln_b, w_s, w_st, b_s, name):
    L, W2 = p0.shape
    W = W2 // 2
    G = W // GMLP_GROUP_DIM

    def body(p_ref, d_ref, bi_ref, g_ref, b_ref, ws_ref, wst_ref, bs_ref, dpre_ref, dbi_ref, dg_ref, db_ref, dws_ref, dbs_ref, dvln):
        @pl.when(pl.program_id(0) == 0)
        def _():
            for ref in (dbi_ref, dg_ref, db_ref, dws_ref, dbs_ref):
                ref[...] = jnp.zeros_like(ref)

        pre = p_ref[...] + bi_ref[...]
        ge = _gelu(pre)
        xh, rstd = _layer_norm_stats(ge[:, W:])
        vln = xh * g_ref[...] + b_ref[...]
        dge_u = []
        for gi in range(G):
            cols = slice(gi * GMLP_GROUP_DIM, (gi + 1) * GMLP_GROUP_DIM)
            vg = vln[:, cols]
            s = _dot(ws_ref[gi], vg, 1, 0) + bs_ref[gi]
            dus_g = d_ref[:, cols]
            dge_u.append(dus_g * s)
            ds = dus_g * ge[:, cols]
            dbs_ref[gi] += jnp.sum(ds, axis=1, keepdims=True)
            dws_ref[gi] += _dot(ds, vg, 1, 1)
            dvln[:, cols] = _dot(wst_ref[gi], ds, 1, 0)
        dv = dvln[...]
        dg_ref[...] += _sum0(dv * xh)
        db_ref[...] += _sum0(dv)
        dxh = dv * g_ref[...]
        dv0 = rstd * (dxh - jnp.mean(dxh, axis=-1, keepdims=True) - xh * jnp.mean(dxh * xh, axis=-1, keepdims=True))
        dpre = jnp.concatenate(dge_u + [dv0], axis=1) * _gelu_grad(pre)
        dbi_ref[...] += _sum0(dpre)
        dpre_ref[...] = dpre.astype(dpre_ref.dtype)

    full = lambda shape: pl.BlockSpec(shape, lambda i: (0,) * len(shape))
    mats = (G, GMLP_CHUNK, GMLP_CHUNK)
    return pl.pallas_call(
        body, name=name, grid=(L // GMLP_CHUNK,),
        in_specs=[pl.BlockSpec((GMLP_CHUNK, W2), lambda i: (i, 0)), pl.BlockSpec((GMLP_CHUNK, W), lambda i: (i, 0)),
                  full((1, W2)), full((1, W)), full((1, W)), full(mats), full(mats), full((G, GMLP_CHUNK, 1))],
        out_specs=[pl.BlockSpec((GMLP_CHUNK, W2), lambda i: (i, 0)), full((1, W2)), full((1, W)), full((1, W)), full(mats),
                   full((G, GMLP_CHUNK, 1))],
        out_shape=[SDS((L, W2), MMT), SDS((1, W2), F32), SDS((1, W), F32), SDS((1, W), F32), SDS(mats, F32),
                   SDS((G, GMLP_CHUNK, 1), F32)],
        scratch_shapes=[pltpu.VMEM((GMLP_CHUNK, W), F32)], compiler_params=_cparams("arbitrary"))(
            p0, dus, b_in, ln_g, ln_b, w_s, w_st, b_s)


def _loss_head(h, target, tm, name):
    L, D = h.shape

    def body(h_ref, t_ref, l_ref, d_ref):
        @pl.when(pl.program_id(0) == 0)
        def _():
            l_ref[...] = jnp.zeros_like(l_ref)

        e = h_ref[...] - t_ref[...]
        l_ref[...] += 0.5 * jnp.sum(jnp.mean(e * e, axis=-1, keepdims=True), axis=0, keepdims=True)
        d_ref[...] = e * (1.0 / D)

    row = pl.BlockSpec((tm, D), lambda i: (i, 0))
    return pl.pallas_call(body, name=name, grid=(L // tm,), in_specs=[row, row],
                          out_specs=[pl.BlockSpec((1, 1), lambda i: (0, 0)), row],
                          out_shape=[SDS((1, 1), F32), SDS((L, D), F32)], compiler_params=_cparams("arbitrary"))(h, target)


def _ada_fwd(cond, ada_w, ada_b, name):
    NL, D, n = ada_w.shape
    tn = _tile(n, 768)

    def body(c_ref, w_ref, b_ref, o_ref):
        cv = c_ref[...]
        o_ref[...] = _dot(cv * _sig(cv), w_ref[...], 1, 0) + b_ref[...]

    return pl.pallas_call(
        body, name=name, grid=(NL, n // tn),
        in_specs=[pl.BlockSpec((2 * SUBLANES, D), lambda i, j: (0, 0)), pl.BlockSpec((None, D, tn), lambda i, j: (i, 0, j)),
                  pl.BlockSpec((None, 1, tn), lambda i, j: (i, 0, j))],
        out_specs=pl.BlockSpec((None, 2 * SUBLANES, tn), lambda i, j: (i, 0, j)), out_shape=SDS((NL, 2 * SUBLANES, n), F32),
        compiler_params=_cparams("parallel", "parallel"))(cond, ada_w, ada_b)


def _ada_bwd(cond, ada_w, dm_lat, dm_ctx, name):
    NL, D, n = ada_w.shape
    tn = _tile(n, 768)

    def body(c_ref, w_ref, dl_ref, dc_ref, dw_ref, ds_ref):
        @pl.when((pl.program_id(0) == 0) & (pl.program_id(1) == 0))
        def _():
            ds_ref[...] = jnp.zeros_like(ds_ref)

        cv = c_ref[...]
        row = lax.broadcasted_iota(jnp.int32, (SUBLANES, tn), 0)
        ctx_rows = jnp.where(row == 0, _sum0(dc_ref[...]), 0.0)
        dm = jnp.concatenate([dl_ref[...], ctx_rows], axis=0)
        dw_ref[...] = _dot(cv * _sig(cv), dm, 0, 0)
        ds_ref[...] += _dot(dm, w_ref[...], 1, 1)

    dspec = pl.BlockSpec((None, SUBLANES, tn), lambda i, j: (i, 0, j))
    return pl.pallas_call(
        body, name=name, grid=(NL, n // tn),
        in_specs=[pl.BlockSpec((2 * SUBLANES, D), lambda i, j: (0, 0)), pl.BlockSpec((None, D, tn), lambda i, j: (i, 0, j)),
                  dspec, dspec],
        out_specs=[pl.BlockSpec((None, D, tn), lambda i, j: (i, 0, j)), pl.BlockSpec((2 * SUBLANES, D), lambda i, j: (0, 0))],
        out_shape=[SDS((NL, D, n), F32), SDS((2 * SUBLANES, D), F32)],
        compiler_params=_cparams("arbitrary", "arbitrary"))(cond, ada_w, dm_lat, dm_ctx)


def _adam_math(w, g, m, v):
    m = ADAM_B1 * m + (1.0 - ADAM_B1) * g
    v = ADAM_B2 * v + (1.0 - ADAM_B2) * jnp.square(g)
    m_hat = m / (1.0 - ADAM_B1 ** ADAM_STEP)
    v_hat = v / (1.0 - ADAM_B2 ** ADAM_STEP)
    return -ADAM_LR * (m_hat / (jnp.sqrt(v_hat) + ADAM_EPS) + ADAM_WD * w), m, v


def _row_tile(rows, cols, elems):
    want = max(SUBLANES, elems // cols)
    best = SUBLANES if rows % SUBLANES == 0 else rows
    for d in range(SUBLANES, min(rows, want) + 1, SUBLANES):
        if rows % d == 0:
            best = d
    return best


def _adamw(w, m, v, parts, name):
    R, C = w.shape
    tr = _row_tile(R, C, 128 * 1024)
    npart = len(parts)

    def body(*refs):
        w_ref, m_ref, v_ref = refs[:3]
        g_ref, d_ref, nm_ref, nv_ref = refs[3 + npart:]
        g = refs[3][...]
        for p_ref in refs[4:3 + npart]:
            g = g + p_ref[...]
        d, nm, nv = _adam_math(w_ref[...], g, m_ref[...], v_ref[...])
        g_ref[...], d_ref[...], nm_ref[...], nv_ref[...] = g, d, nm, nv

    blk = pl.BlockSpec((tr, C), lambda i: (i, 0))
    return pl.pallas_call(body, name=name, grid=(R // tr,), in_specs=[blk] * (3 + npart), out_specs=[blk] * 4,
                          out_shape=[SDS((R, C), F32)] * 4, compiler_params=_cparams("parallel"))(w, m, v, *parts)


def _adamw_layer(w, m, v, layer, parts, prev, name):
    _, R, C = w.shape
    tr = _row_tile(R, C, 128 * 1024)
    npart = len(parts)
    nprev = 0 if prev is None else 4

    def body(*refs):
        w_ref, m_ref, v_ref = refs[:3]
        g_ref, d_ref, nm_ref, nv_ref = refs[3 + npart + nprev:]
        g = refs[3][...]
        for p_ref in refs[4:3 + npart]:
            g = g + p_ref[...]
        d, nm, nv = _adam_math(w_ref[...], g, m_ref[...], v_ref[...])
        g_ref[...], d_ref[...], nm_ref[...], nv_ref[...] = g, d, nm, nv

    stacked = pl.BlockSpec((None, tr, C), lambda i: (layer, i, 0))
    flat = pl.BlockSpec((tr, C), lambda i: (i, 0))
    return pl.pallas_call(
        body, name=name, grid=(R // tr,),
        in_specs=[stacked] * 3 + [flat] * npart + [pl.BlockSpec(memory_space=pl.ANY)] * nprev, out_specs=[stacked] * 4,
        out_shape=[SDS(w.shape, F32)] * 4, input_output_aliases={3 + npart + k: k for k in range(nprev)},
        compiler_params=_cparams("parallel"))(w, m, v, *parts, *(prev or ()))


def _sum_slots(x, name, out_dtype=F32):
    S, R, C = x.shape
    tr = _row_tile(R, C, 128 * 1024)

    def body(x_ref, o_ref):
        acc = x_ref[0].astype(F32)
        for s in range(1, S):
            acc = acc + x_ref[s].astype(F32)
        o_ref[...] = acc.astype(o_ref.dtype)

    return pl.pallas_call(body, name=name, grid=(R // tr,), in_specs=[pl.BlockSpec((S, tr, C), lambda i: (0, i, 0))],
                          out_specs=pl.BlockSpec((tr, C), lambda i: (i, 0)), out_shape=SDS((R, C), out_dtype),
                          compiler_params=_cparams("parallel"))(x)


def _my_place():
    return lax.axis_index("x"), lax.axis_index("y"), lax.axis_index("c")


def _other_chips(x, y):
    return [(1 - x, y), (x, 1 - y), (1 - x, 1 - y)]


def _all_gather(v, name):
    R, C = v.shape

    def body(v_ref, o_ref, send_sems, recv_sems, local_sem):
        x, y, c = _my_place()
        me = 4 * x + 2 * y + c
        mine = pltpu.make_async_copy(v_ref, o_ref.at[me], local_sem)
        mine.start()
        copies = []
        for flip in range(1, N_DEV):
            fx, fy, fc = (flip >> 2) & 1, (flip >> 1) & 1, flip & 1
            peer = ((x + fx) % 2, (y + fy) % 2, (c + fc) % 2)
            cp = pltpu.make_async_remote_copy(src_ref=v_ref, dst_ref=o_ref.at[me], send_sem=send_sems.at[flip - 1],
                                              recv_sem=recv_sems.at[flip - 1], device_id=peer, device_id_type=MESH)
            cp.start()
            copies.append(cp)
        for cp in copies:
            cp.wait()
        mine.wait()

    return pl.pallas_call(
        body, name=name, in_specs=[pl.BlockSpec(memory_space=pl.ANY)], out_specs=pl.BlockSpec(memory_space=pl.ANY),
        out_shape=SDS((N_DEV, R, C), v.dtype),
        scratch_shapes=[pltpu.SemaphoreType.DMA((N_DEV - 1,)), pltpu.SemaphoreType.DMA((N_DEV - 1,)), pltpu.SemaphoreType.DMA],
        )(v)


def _shard_window(ref, axis, j, size):
    idx = [slice(None)] * len(ref.shape)
    idx[axis] = pl.ds(pl.multiple_of(j * size, SUBLANES), size)
    return ref.at[tuple(idx)]


def _gather_plan(axis):
    def plan(srcs, lands):
        x, y, c = _my_place()
        dst = _shard_window(lands[0], axis, 2 * x + y, srcs[0].shape[axis])
        return [(srcs[0], dst)], [(srcs[0], dst, (px, py, c)) for px, py in _other_chips(x, y)]
    return plan


def _scatter_plan(axis):
    def plan(srcs, lands):
        x, y, c = _my_place()
        j = 2 * x + y
        size = srcs[0].shape[axis] // N_CHIPS
        local = [(_shard_window(srcs[0], axis, j, size), lands[0].at[j])]
        remote = [(_shard_window(srcs[0], axis, 2 * px + py, size), lands[0].at[j], (px, py, c)) for px, py in _other_chips(x, y)]
        return local, remote
    return plan


def _all_gather_plan(srcs, lands):
    x, y, c = _my_place()
    dst = lands[0].at[4 * x + 2 * y + c]
    remote = []
    for flip in range(1, N_DEV):
        fx, fy, fc = (flip >> 2) & 1, (flip >> 1) & 1, flip & 1
        remote.append((srcs[0], dst, ((x + fx) % 2, (y + fy) % 2, (c + fc) % 2)))
    return [(srcs[0], dst)], remote


HBM_SPEC = pl.BlockSpec(memory_space=pltpu.HBM)
SEM_SPEC = pl.BlockSpec(memory_space=pltpu.SEMAPHORE)


def _in_hbm(a):
    return pltpu.with_memory_space_constraint(a, pltpu.HBM)


def _split_start(name, exchanges):
    srcs = [s for e in exchanges for s in e[0]]
    lands = [lax.empty(s.shape, s.dtype) for e in exchanges for s in e[1]]
    ns, nl, ne = len(srcs), len(lands), len(exchanges)

    def body(*refs):
        src_refs, land_refs = refs[:ns], refs[ns:ns + nl]
        sem_refs = refs[ns + nl:ns + nl + 2 * ne]
        token, local_sem = refs[-2], refs[-1]
        si = li = 0
        locals_ = []
        for e, (e_srcs, e_lands, plan, n) in enumerate(exchanges):
            local, remote = plan(src_refs[si:si + len(e_srcs)], land_refs[li:li + len(e_lands)])
            si, li = si + len(e_srcs), li + len(e_lands)
            assert len(remote) == n
            for k, (src, dst, peer) in enumerate(remote):
                pltpu.make_async_remote_copy(src_ref=src, dst_ref=dst, send_sem=sem_refs[2 * e].at[k], recv_sem=sem_refs[2 * e + 1].at[k],
                                             device_id=peer, device_id_type=MESH).start()
            locals_ += local
        for src, dst in locals_:
            cp = pltpu.make_async_copy(src, dst, local_sem)
            cp.start()
            cp.wait()
        token[...] = jnp.zeros_like(token)

    sems = [pltpu.SemaphoreType.DMA((e[3],)) for e in exchanges for _ in range(2)]
    res = pl.pallas_call(
        body, name=name, in_specs=[HBM_SPEC] * (ns + nl),
        out_shape=sems + [pltpu.HBM(a.shape, a.dtype) for a in srcs + lands] + [SDS((SUBLANES, LANES), F32)],
        out_specs=[SEM_SPEC] * (2 * ne) + [HBM_SPEC] * (ns + nl) + [pl.BlockSpec(memory_space=pltpu.VMEM)],
        input_output_aliases={i: 2 * ne + i for i in range(ns + nl)}, scratch_shapes=[pltpu.SemaphoreType.DMA],
        compiler_params=pltpu.CompilerParams(has_side_effects=pltpu.SideEffectType.DATAFLOW_SIDE_EFFECTING))(
            *[_in_hbm(a) for a in srcs + lands])
    sem_out, thru, token = res[:2 * ne], res[2 * ne:2 * ne + ns + nl], res[-1]
    handles, si, li = [], 0, 0
    for e, (e_srcs, e_lands, plan, n) in enumerate(exchanges):
        handles.append(dict(srcs=list(thru[si:si + len(e_srcs)]), lands=list(thru[ns + li:ns + li + len(e_lands)]),
                            sems=(sem_out[2 * e], sem_out[2 * e + 1]), plan=plan, n=n))
        si, li = si + len(e_srcs), li + len(e_lands)
    return handles, token


def _split_wait(name, handles, after):
    srcs = [s for h in handles for s in h["srcs"]]
    lands = [s for h in handles for s in h["lands"]]
    ns, nl, nh = len(srcs), len(lands), len(handles)

    def body(*refs):
        src_refs, land_refs = refs[:ns], refs[ns:ns + nl]
        sem_refs = refs[ns + nl:ns + nl + 2 * nh]
        si = li = 0
        for e, h in enumerate(handles):
            _, remote = h["plan"](src_refs[si:si + len(h["srcs"])], land_refs[li:li + len(h["lands"])])
            si, li = si + len(h["srcs"]), li + len(h["lands"])
            for k, (src, dst, peer) in enumerate(remote):
                cp = pltpu.make_async_remote_copy(src_ref=src, dst_ref=dst, send_sem=sem_refs[2 * e].at[k],
                                                  recv_sem=sem_refs[2 * e + 1].at[k], device_id=peer, device_id_type=MESH)
                cp.wait_send()
                cp.wait_recv()

    res = pl.pallas_call(
        body, name=name, in_specs=[HBM_SPEC] * (ns + nl) + [SEM_SPEC] * (2 * nh) + [pl.BlockSpec(memory_space=pl.ANY)],
        out_shape=[pltpu.HBM(a.shape, a.dtype) for a in srcs + lands], out_specs=[HBM_SPEC] * (ns + nl),
        input_output_aliases={i: i for i in range(ns + nl)},
        compiler_params=pltpu.CompilerParams(has_side_effects=pltpu.SideEffectType.DATAFLOW_SIDE_EFFECTING))(
            *srcs, *lands, *[s for h in handles for s in h["sems"]], after)
    out, li = [], 0
    for h in handles:
        out.append(list(res[ns + li:ns + li + len(h["lands"])]))
        li += len(h["lands"])
    return out


def _swap_with_sibling(parts, name):
    nt = len(parts)

    def body(*refs):
        ins, outs = refs[:nt], refs[nt:2 * nt]
        send_sems, recv_sems = refs[2 * nt:]
        x, y, c = _my_place()
        copies = []
        for t in range(nt):
            cp = pltpu.make_async_remote_copy(src_ref=ins[t], dst_ref=outs[t], send_sem=send_sems.at[t], recv_sem=recv_sems.at[t],
                                              device_id=(x, y, 1 - c), device_id_type=MESH)
            cp.start()
            copies.append(cp)
        for cp in copies:
            cp.wait()

    any_spec = pl.BlockSpec(memory_space=pl.ANY)
    return pl.pallas_call(
        body, name=name, in_specs=[any_spec] * nt, out_specs=[any_spec] * nt, out_shape=[SDS(p.shape, p.dtype) for p in parts],
        scratch_shapes=[pltpu.SemaphoreType.DMA((nt,)), pltpu.SemaphoreType.DMA((nt,))],
        )(*parts)


PACK_COLS = 1024


def _pack(arrays):
    flat = jnp.concatenate([a.reshape(-1) for a in arrays])
    pad = (-flat.shape[0]) % (SUBLANES * PACK_COLS)
    return jnp.pad(flat, (0, pad)).reshape(-1, PACK_COLS)


def _unpack(packed, shapes):
    flat, out, pos = packed.reshape(-1), [], 0
    for shape in shapes:
        n = math.prod(shape)
        out.append(flat[pos:pos + n].reshape(shape))
        pos += n
    return out


def _unshard_last(stacked):
    moved = jnp.moveaxis(stacked, 0, -2)
    return moved.reshape(moved.shape[:-2] + (moved.shape[-2] * moved.shape[-1],))


def _my_block_last(full, j):
    s = full.shape[-1] // N_CHIPS
    return lax.dynamic_index_in_dim(full.reshape(full.shape[:-1] + (N_CHIPS, s)), j, axis=full.ndim - 1, keepdims=False)


def _rope_tables(L):
    rows = L // GRID_W
    row = jnp.repeat(jnp.arange(rows), GRID_W).astype(F32)
    col = jnp.tile(jnp.arange(GRID_W), rows).astype(F32)
    axis_dim = HEAD_DIM // 2
    inv_freq = ROPE_BASE ** (-jnp.arange(0, axis_dim, 2, dtype=F32) / axis_dim)
    ang_r, ang_c = row[:, None] * inv_freq[None, :], col[:, None] * inv_freq[None, :]
    ang = jnp.concatenate([ang_r, ang_r, ang_c, ang_c] * 2, axis=-1)
    return jnp.cos(ang), jnp.sin(ang)


SMALL_SHARDED = ("norm_g", "ffn_conv_w", "cm_b_in", "cm_dw_w", "cm_dw_b", "cm_ln_g", "cm_ln_b", "cm_b_out", "gm_b_in", "gm_ln_g",
                 "gm_ln_b")
SMALL_REPLICATED = ("c_ctx", "ada_b", "ffn_conv_b", "attn_sink", "gm_w_s", "gm_b_s")
BIG = ("ffn_w_up", "ffn_w_down", "cm_w_in", "cm_w_out", "attn_w_qkv", "attn_w_o", "gm_w_in", "gm_w_out")
BIG_AXIS = {"ffn_w_up": 2, "ffn_w_down": 1, "cm_w_in": 2, "cm_w_out": 1, "attn_w_qkv": 2, "attn_w_o": 1, "gm_w_in": 2, "gm_w_out": 1}
WEIGHTS = ("c_ctx", "ada_w", "ada_b", "norm_g", "ffn_w_up", "ffn_conv_w", "ffn_conv_b", "ffn_w_down", "cm_w_in", "cm_b_in",
           "cm_dw_w", "cm_dw_b", "cm_ln_g", "cm_ln_b", "cm_w_out", "cm_b_out", "attn_w_qkv", "attn_sink", "attn_w_o", "gm_w_in",
           "gm_b_in", "gm_ln_g", "gm_ln_b", "gm_w_s", "gm_b_s", "gm_w_out")


def _step(x, c, ctx, target, W, M, V):
    L, D = x.shape[1], x.shape[2]
    C = ctx.shape[1]
    T = L + C
    NL = W["ada_w"].shape[0]
    tm = 256 if C % 256 == 0 else 128
    nl = L // tm
    xi, yi, ci = _my_place()
    chip = 2 * xi + yi
    dev = 4 * xi + 2 * yi + ci
    segs2, segs1 = [(0, L), (L, C)], [(0, L)]
    vec = lambda a: a.reshape(1, -1)

    use_order = [("cm_w_in", 0), ("cm_w_out", 0), ("ffn_w_up", 0), ("ffn_w_down", 0), ("attn_w_qkv", 0), ("attn_w_o", 0),
                 ("ffn_w_up", 1), ("ffn_w_down", 1), ("gm_w_in", 0), ("gm_w_out", 0), ("ffn_w_up", 2), ("ffn_w_down", 2),
                 ("cm_w_in", 1), ("cm_w_out", 1), ("ffn_w_up", 3), ("ffn_w_down", 3)]
    exchanges = []
    for n, i in use_order:
        shard = W[n][i].astype(MMT)
        whole = list(shard.shape)
        whole[BIG_AXIS[n] - 1] *= N_CHIPS
        exchanges.append(([shard], [SDS(tuple(whole), MMT)], _gather_plan(BIG_AXIS[n] - 1), N_CHIPS - 1))
    handles, started = _split_start("gather_weights_start", exchanges)
    in_flight = dict(zip(use_order, handles))
    arrived = {}

    def big(n, i, after=None):
        if (n, i) not in arrived:
            arrived[(n, i)] = _split_wait(f"wait_{n}_{i}", [in_flight[(n, i)]], after)[0][0]
        return arrived[(n, i)]

    small_shapes = [W[n].shape for n in SMALL_SHARDED]
    c = c + started[0, 0]
    ag1 = _all_gather(_pack([c.reshape(-1)] + [W[n] for n in SMALL_SHARDED]), "gather_small")
    parts = [_unpack(ag1[2 * s], [(D,)] + small_shapes) for s in range(N_CHIPS)]
    c_rows = jnp.stack([_unpack(ag1[d], [(D,)])[0] for d in range(N_DEV)])
    P = {n: _unshard_last(jnp.stack([parts[s][1 + i] for s in range(N_CHIPS)])) for i, n in enumerate(SMALL_SHARDED)}
    for n in SMALL_REPLICATED:
        P[n] = W[n]

    cond = jnp.concatenate([c_rows, W["c_ctx"][None, :], jnp.zeros((2 * SUBLANES - N_DEV - 1, D), F32)], axis=0)
    ncol = W["ada_w"].shape[2]
    ada_b_mine = lax.dynamic_slice_in_dim(W["ada_b"], chip * ncol, ncol, axis=1)[:, None, :]
    mods_mine = _ada_fwd(cond, W["ada_w"], ada_b_mine, "ada_fwd")
    ag2 = _all_gather(mods_mine.reshape(NL * 2 * SUBLANES, ncol), "gather_mods").reshape(N_DEV, NL, 2 * SUBLANES, ncol)
    mods_all = _unshard_last(jnp.stack([ag2[2 * s] for s in range(N_CHIPS)]))
    mod_lat = lax.dynamic_index_in_dim(mods_all, dev, axis=1, keepdims=False).reshape(NL, 6, D)
    mod_ctx = mods_all[:, N_DEV].reshape(NL, 6, D)
    mod2 = jnp.stack([mod_lat, mod_ctx], axis=1)
    mod1 = mod_lat[:, None]

    zero_d = jnp.zeros((1, D), F32)
    cos, sin = _rope_tables(L)
    nkv = D // HEAD_DIM // Q_PER_KV
    qdim, kvdim = D, nkv * HEAD_DIM

    def ffn_fwd(i, h, mod, rows, segs, tag):
        a2 = _prenorm(h, mod, vec(P["norm_g"][i, 2]), 1, rows, nl, tm, f"pre_ffn_{tag}")
        z0 = _mm(a2, big("ffn_w_up", i, a2), "nn", F32, f"ffn_up_{tag}")
        u = _ffn_gate(z0, P["ffn_conv_w"][i], vec(P["ffn_conv_b"][i]), segs, f"ffn_gate_{tag}")
        f = _mm(u, big("ffn_w_down", i, u), "nn", F32, f"ffn_down_{tag}")
        h_out = _postnorm(h, f, zero_d, mod, vec(P["norm_g"][i, 3]), 5, rows, nl, tm, f"post_ffn_{tag}")
        return h_out, dict(h=h, a2=a2, z0=z0, f=f)

    def ffn_bwd(i, dh, sv, mod, rows, segs, tag, G):
        df, dg2, dgn3, _ = _postnorm_bwd(dh, sv["f"], zero_d, mod, vec(P["norm_g"][i, 3]), 5, rows, nl, tm, f"post_ffn_bwd_{tag}")
        du = _mm(df, big("ffn_w_down", i), "nt", F32, f"ffn_down_dx_{tag}")
        u, dz0, dcw, dcb = _ffn_gate_bwd(sv["z0"], du, P["ffn_conv_w"][i], vec(P["ffn_conv_b"][i]), segs, f"ffn_gate_bwd_{tag}")
        G["ffn_w_down"][i] = _mm(u, df, "tn", MMT, f"ffn_down_dw_{tag}")
        G["ffn_w_up"][i] = _mm(sv["a2"], dz0, "tn", MMT, f"ffn_up_dw_{tag}")
        da2 = _mm(dz0, big("ffn_w_up", i), "nt", F32, f"ffn_up_dx_{tag}")
        dh, dsh2, dsc2, dgn2 = _prenorm_bwd(sv["h"], da2, dh, mod, vec(P["norm_g"][i, 2]), 1, rows, nl, tm, f"pre_ffn_bwd_{tag}")
        G["ffn_conv_w"][i], G["ffn_conv_b"][i] = dcw, dcb[0]
        return dh, (dsh2, dsc2, dg2), (dgn2, dgn3)

    def conformer_fwd(i, j, h, mod, rows, segs, tag):
        a = _prenorm(h, mod, vec(P["norm_g"][i, 0]), 0, rows, nl, tm, f"pre_mix_{tag}")
        p0 = _mm(a, big("cm_w_in", j, a), "nn", F32, f"cm_in_{tag}")
        z2 = _glu_conv(p0, vec(P["cm_b_in"][j]), P["cm_dw_w"][j], vec(P["cm_dw_b"][j]), segs, f"cm_conv_{tag}")
        z4 = _ln_silu(z2, vec(P["cm_ln_g"][j]), vec(P["cm_ln_b"][j]), rows, tm, f"cm_ln_{tag}")
        y = _mm(z4, big("cm_w_out", j, z4), "nn", F32, f"cm_out_{tag}")
        h_out = _postnorm(h, y, vec(P["cm_b_out"][j]), mod, vec(P["norm_g"][i, 1]), 2, rows, nl, tm, f"post_mix_{tag}")
        return h_out, dict(h=h, a=a, p0=p0, z2=z2, z4=z4, y=y)

    def conformer_bwd(i, j, dh, sv, mod, rows, segs, tag, G):
        dy, dg1, dgn1, dbo = _postnorm_bwd(dh, sv["y"], vec(P["cm_b_out"][j]), mod, vec(P["norm_g"][i, 1]), 2, rows, nl, tm,
                                           f"post_mix_bwd_{tag}")
        G["cm_w_out"][j] = _mm(sv["z4"], dy, "tn", MMT, f"cm_out_dw_{tag}")
        dz4 = _mm(dy, big("cm_w_out", j), "nt", F32, f"cm_out_dx_{tag}")
        dz2, dlg, dlb = _ln_silu_bwd(sv["z2"], dz4, vec(P["cm_ln_g"][j]), vec(P["cm_ln_b"][j]), rows, tm, f"cm_ln_bwd_{tag}")
        dpa, dpg, ddw, ddb, dba, dbg = _glu_conv_bwd(sv["p0"], vec(P["cm_b_in"][j]), P["cm_dw_w"][j], dz2, segs, f"cm_conv_bwd_{tag}")
        dp = jnp.concatenate([dpa, dpg], axis=1)
        G["cm_w_in"][j] = _mm(sv["a"], dp, "tn", MMT, f"cm_in_dw_{tag}")
        da = _mm(dp, big("cm_w_in", j), "nt", F32, f"cm_in_dx_{tag}")
        dh, dsh1, dsc1, dgn0 = _prenorm_bwd(sv["h"], da, dh, mod, vec(P["norm_g"][i, 0]), 0, rows, nl, tm, f"pre_mix_bwd_{tag}")
        G["cm_b_out"][j] = jnp.sum(dbo, axis=0)[0]
        G["cm_ln_g"][j], G["cm_ln_b"][j], G["cm_dw_w"][j], G["cm_dw_b"][j] = dlg[0], dlb[0], ddw, ddb[0]
        G["cm_b_in"][j] = jnp.concatenate([dba[0], dbg[0]])
        return dh, (dsh1, dsc1, dg1), (dgn0, dgn1)

    def heads(a, n):
        return a.reshape(a.shape[0], n, HEAD_DIM).transpose(1, 0, 2)

    def unheads(a):
        return a.transpose(1, 0, 2).reshape(a.shape[1], -1)

    G = {n: [None] * W[n].shape[0] for n in WEIGHTS if n not in ("c_ctx", "ada_w", "ada_b", "norm_g")}
    saved = []
    h = jnp.concatenate([x[0], ctx[0]], axis=0)
    h, s_mix = conformer_fwd(0, 0, h, mod2[0], T, segs2, "l0")
    h, s_ffn = ffn_fwd(0, h, mod2[0], T, segs2, "l0")
    saved.append((s_mix, s_ffn))
    a_all = _prenorm(h, mod2[1], vec(P["norm_g"][1, 0]), 0, T, nl, tm, "pre_mix_l1")
    qkv = _mm(a_all, big("attn_w_qkv", 0, a_all), "nn", F32, "attn_qkv")
    qk_rot, v_lat = _rope(qkv, cos, sin, L, qdim + kvdim, tm, "rope")
    q_h = heads(qk_rot[:, :qdim], nkv * Q_PER_KV).reshape(nkv, Q_PER_KV, L, HEAD_DIM)
    k_h, v_h = heads(qk_rot[:, qdim:], nkv), heads(v_lat, nkv)
    kc_h = heads(qkv[L:, qdim:qdim + kvdim].astype(MMT), nkv)
    vc_h = heads(qkv[L:, qdim + kvdim:].astype(MMT), nkv)
    sink = P["attn_sink"][0]
    o_h, lse = _attn_fwd(q_h, k_h, v_h, kc_h, vc_h, sink, "attn")
    o_nat = unheads(o_h.reshape(nkv * Q_PER_KV, L, HEAD_DIM)).astype(MMT)
    y1 = _mm(o_nat, big("attn_w_o", 0, o_nat), "nn", F32, "attn_out")
    h_in1 = h
    h = _postnorm(h, y1, zero_d, mod1[1], vec(P["norm_g"][1, 1]), 2, L, nl, tm, "post_mix_l1")
    h, s_ffn1 = ffn_fwd(1, h, mod1[1], L, segs1, "lat")
    h_in2 = h
    a_2 = _prenorm(h, mod1[2], vec(P["norm_g"][2, 0]), 0, L, nl, tm, "pre_mix_l2")
    p0_2 = _mm(a_2, big("gm_w_in", 0, a_2), "nn", F32, "gm_in")
    ws_bf = P["gm_w_s"][0].astype(MMT)
    bs_col = P["gm_b_s"][0][:, :, None]
    us = _gmlp_fwd(p0_2, vec(P["gm_b_in"][0]), vec(P["gm_ln_g"][0]), vec(P["gm_ln_b"][0]), ws_bf, bs_col, "gmlp")
    y2 = _mm(us, big("gm_w_out", 0, us), "nn", F32, "gm_out")
    h = _postnorm(h, y2, zero_d, mod1[2], vec(P["norm_g"][2, 1]), 2, L, nl, tm, "post_mix_l2")
    h, s_ffn2 = ffn_fwd(2, h, mod1[2], L, segs1, "lat")
    h, s_mix3 = conformer_fwd(3, 1, h, mod1[3], L, segs1, "l3")
    h, s_ffn3 = ffn_fwd(3, h, mod1[3], L, segs1, "lat")

    loss_mine, dh = _loss_head(h, target[0], tm, "loss_head")

    dmod = [None] * NL
    dgn = [None] * NL

    def finish(i, mix, ffn, gns_mix, gns_ffn):
        dmod[i] = jnp.concatenate(list(mix) + list(ffn), axis=1)
        dgn[i] = jnp.stack([jnp.sum(g, axis=0)[0] for g in (gns_mix[0], gns_mix[1], gns_ffn[0], gns_ffn[1])])

    sent = {}

    def send(tag, tensors):
        exchanges = []
        for n, l in tensors:
            g = G[n][l]
            shard = list(g.shape)
            shard[BIG_AXIS[n] - 1] //= N_CHIPS
            exchanges.append(([g], [SDS((N_CHIPS,) + tuple(shard), g.dtype)], _scatter_plan(BIG_AXIS[n] - 1), N_CHIPS - 1))
        handles, token = _split_start(f"scatter_start_{tag}", exchanges)
        sent[tag] = (tensors, handles)
        return token[0:1, 0:1]

    dh, m_ffn, n_ffn = ffn_bwd(3, dh, s_ffn3, mod1[3], L, segs1, "lat", G)
    dh, m_mix, n_mix = conformer_bwd(3, 1, dh, s_mix3, mod1[3], L, segs1, "l3", G)
    finish(3, m_mix, m_ffn, n_mix, n_ffn)
    zero_d = zero_d + send("l3", [("ffn_w_up", 3), ("ffn_w_down", 3), ("cm_w_in", 1), ("cm_w_out", 1)])

    dh, m_ffn, n_ffn = ffn_bwd(2, dh, s_ffn2, mod1[2], L, segs1, "lat", G)
    dy2, dg1, dgn1, _ = _postnorm_bwd(dh, y2, zero_d, mod1[2], vec(P["norm_g"][2, 1]), 2, L, nl, tm, "post_mix_bwd_l2")
    G["gm_w_out"][0] = _mm(us, dy2, "tn", MMT, "gm_out_dw")
    dus = _mm(dy2, big("gm_w_out", 0), "nt", F32, "gm_out_dx")
    ws_t = jnp.swapaxes(P["gm_w_s"][0], 1, 2).astype(MMT)
    dpre, dbi, dlg, dlb, dws, dbs = _gmlp_bwd(p0_2, dus, vec(P["gm_b_in"][0]), vec(P["gm_ln_g"][0]), vec(P["gm_ln_b"][0]), ws_bf,
                                              ws_t, bs_col, "gmlp_bwd")
    G["gm_w_in"][0] = _mm(a_2, dpre, "tn", MMT, "gm_in_dw")
    da = _mm(dpre, big("gm_w_in", 0), "nt", F32, "gm_in_dx")
    dh, dsh1, dsc1, dgn0 = _prenorm_bwd(h_in2, da, dh, mod1[2], vec(P["norm_g"][2, 0]), 0, L, nl, tm, "pre_mix_bwd_l2")
    G["gm_b_in"][0], G["gm_ln_g"][0], G["gm_ln_b"][0], G["gm_w_s"][0], G["gm_b_s"][0] = dbi[0], dlg[0], dlb[0], dws, dbs[:, :, 0]
    finish(2, (dsh1, dsc1, dg1), m_ffn, (dgn0, dgn1), n_ffn)
    zero_d = zero_d + send("l2", [("ffn_w_up", 2), ("ffn_w_down", 2), ("gm_w_in", 0), ("gm_w_out", 0)])

    dh, m_ffn, n_ffn = ffn_bwd(1, dh, s_ffn1, mod1[1], L, segs1, "lat", G)
    dy1, dg1, dgn1, _ = _postnorm_bwd(dh, y1, zero_d, mod1[1], vec(P["norm_g"][1, 1]), 2, L, nl, tm, "post_mix_bwd_l1")
    G["attn_w_o"][0] = _mm(o_nat, dy1, "tn", MMT, "attn_out_dw")
    do_nat = _mm(dy1, big("attn_w_o", 0), "nt", MMT, "attn_out_dx")
    do_h = heads(do_nat, nkv * Q_PER_KV).reshape(nkv, Q_PER_KV, L, HEAD_DIM)
    dq_h, dkc_h, dvc_h, dsk = _attn_bwd_q(q_h, k_h, v_h, kc_h, vc_h, sink, o_h, do_h, lse, "attn_bwd_q")
    dk_h, dv_h = _attn_bwd_kv(q_h, k_h, v_h, o_h, do_h, lse, "attn_bwd_kv")
    dqk = jnp.concatenate([unheads(dq_h.reshape(nkv * Q_PER_KV, L, HEAD_DIM)), unheads(dk_h)], axis=1)
    dqkv_lat = _rope_bwd(dqk, unheads(dv_h), cos, sin, tm, "rope_bwd")
    dqkv_ctx = jnp.concatenate([jnp.zeros((C, qdim), MMT), unheads(dkc_h).astype(MMT), unheads(dvc_h).astype(MMT)], axis=1)
    dqkv = jnp.concatenate([dqkv_lat, dqkv_ctx], axis=0)
    G["attn_w_qkv"][0] = _mm(a_all, dqkv, "tn", MMT, "attn_qkv_dw")
    da_all = _mm(dqkv, big("attn_w_qkv", 0), "nt", F32, "attn_qkv_dx")
    dh_all = jnp.concatenate([dh, jnp.zeros((C, D), F32)], axis=0)
    dh, dsh1, dsc1, dgn0 = _prenorm_bwd(h_in1, da_all, dh_all, mod2[1], vec(P["norm_g"][1, 0]), 0, T, nl, tm, "pre_mix_bwd_l1")
    G["attn_sink"][0] = dsk[:, :Q_PER_KV, 0].reshape(-1)
    pad_ctx = lambda a: jnp.concatenate([a, jnp.zeros_like(a)], axis=0)
    finish(1, (dsh1, dsc1, pad_ctx(dg1)), [pad_ctx(a) for a in m_ffn], (dgn0, dgn1), n_ffn)
    zero_d = zero_d + send("l1", [("ffn_w_up", 1), ("ffn_w_down", 1), ("attn_w_qkv", 0), ("attn_w_o", 0)])

    s_mix0, s_ffn0 = saved[0]
    dh, m_ffn, n_ffn = ffn_bwd(0, dh, s_ffn0, mod2[0], T, segs2, "l0", G)
    dh, m_mix, n_mix = conformer_bwd(0, 0, dh, s_mix0, mod2[0], T, segs2, "l0", G)
    finish(0, m_mix, m_ffn, n_mix, n_ffn)
    grad_x = dh[:L][None]
    started_l0 = send("l0", [("ffn_w_up", 0), ("ffn_w_down", 0), ("cm_w_in", 0), ("cm_w_out", 0)])

    for i in range(2, NL):
        dmod[i] = pad_ctx(dmod[i])
    dmod_all = jnp.stack(dmod).reshape(NL, 2, 6 * D) + started_l0

    ag3 = _all_gather(dmod_all.reshape(NL * 2, 6 * D), "gather_dmods").reshape(N_DEV, NL, 2, N_CHIPS, ncol)
    dm_cols = lax.dynamic_index_in_dim(ag3, chip, axis=3, keepdims=False)
    dm_lat, dm_ctx = jnp.moveaxis(dm_cols[:, :, 0], 0, 1), jnp.moveaxis(dm_cols[:, :, 1], 0, 1)
    g_ada_w, dsilu = _ada_bwd(cond, W["ada_w"], dm_lat, dm_ctx, "ada_bwd")
    cc = W["c_ctx"]
    sg = jax.nn.sigmoid(cc)
    dcctx_part = jnp.where(ci == 0, 1.0, 0.0) * dsilu[N_DEV] * (sg * (1.0 + cc * (1.0 - sg)))

    Gs = {n: jnp.stack(G[n]) for n in G if n not in BIG}
    Gs["norm_g"] = jnp.stack(dgn)
    Gs["ada_b"] = jnp.sum(dmod_all, axis=1)
    Gs["c_ctx"] = dcctx_part
    small_names = list(SMALL_SHARDED) + list(SMALL_REPLICATED)
    small_full_shapes = [P[n].shape for n in small_names]
    small_pack = _pack([Gs[n] for n in small_names])
    (ag4_handle,), _ = _split_start("gather_small_grads_start", [
        ([small_pack], [SDS((N_DEV,) + small_pack.shape, F32)], _all_gather_plan, N_DEV - 1)])

    flat2 = lambda a: a.reshape(-1, a.shape[-1])
    res = {}
    outs = _adamw(flat2(W["ada_w"]), flat2(M["ada_w"]), flat2(V["ada_w"]), [flat2(g_ada_w)], "adamw_ada_w")
    res["ada_w"] = tuple(o.reshape(W["ada_w"].shape) for o in outs)

    after = outs[0]
    so_far = {}
    for tag in ("l3", "l2", "l1", "l0"):
        tensors, handles = sent[tag]
        landed = _split_wait(f"scatter_wait_{tag}", handles, after)
        mine = [_sum_slots(lands[0], f"sum_chips_{n}_{l}") for (n, l), lands in zip(tensors, landed)]
        theirs = _swap_with_sibling(mine, f"swap_cores_{tag}")
        for (n, l), a, b in zip(tensors, mine, theirs):
            so_far[n] = _adamw_layer(W[n], M[n], V[n], l, [a, b], so_far.get(n), f"adamw_{n}_{l}")
        after = so_far["ffn_w_up"][0]
    for n in BIG:
        res[n] = tuple(so_far[n])

    ag4 = _split_wait("gather_small_grads_wait", [ag4_handle], after)[0][0]
    small_sum = _unpack(_sum_slots(ag4, "sum_small_grads"), small_full_shapes)
    g_small = {}
    for n, g in zip(small_names, small_sum):
        g_small[n] = _my_block_last(g, chip) if n in SMALL_SHARDED else g
    packed = [_pack([d[n] for n in small_names]) for d in (W, M, V)]
    outs_small = _adamw(packed[0], packed[1], packed[2], [_pack([g_small[n] for n in small_names])], "adamw_small")
    shard_shapes = [W[n].shape for n in small_names]
    for k, n in enumerate(small_names):
        res[n] = tuple(_unpack(o, shard_shapes)[k] for o in outs_small)

    loss = lax.psum(loss_mine[0, 0], ("x", "y", "c"))
    return (loss, grad_x) + tuple(res[n][k] for k in range(4) for n in WEIGHTS)


def kernel(x, c, ctx, c_ctx, ada_w, ada_b, norm_g, ffn_w_up, ffn_conv_w, ffn_conv_b, ffn_w_down, cm_w_in, cm_b_in, cm_dw_w, cm_dw_b, cm_ln_g, cm_ln_b, cm_w_out, cm_b_out, attn_w_qkv, attn_sink, attn_w_o, gm_w_in, gm_b_in, gm_ln_g, gm_ln_b, gm_w_s, gm_b_s, gm_w_out, loss_target, m_c_ctx, m_ada_w, m_ada_b, m_norm_g, m_ffn_w_up, m_ffn_conv_w, m_ffn_conv_b, m_ffn_w_down, m_cm_w_in, m_cm_b_in, m_cm_dw_w, m_cm_dw_b, m_cm_ln_g, m_cm_ln_b, m_cm_w_out, m_cm_b_out, m_attn_w_qkv, m_attn_sink, m_attn_w_o, m_gm_w_in, m_gm_b_in, m_gm_ln_g, m_gm_ln_b, m_gm_w_s, m_gm_b_s, m_gm_w_out, v_c_ctx, v_ada_w, v_ada_b, v_norm_g, v_ffn_w_up, v_ffn_conv_w, v_ffn_conv_b, v_ffn_w_down, v_cm_w_in, v_cm_b_in, v_cm_dw_w, v_cm_dw_b, v_cm_ln_g, v_cm_ln_b, v_cm_w_out, v_cm_b_out, v_attn_w_qkv, v_attn_sink, v_attn_w_o, v_gm_w_in, v_gm_b_in, v_gm_ln_g, v_gm_ln_b, v_gm_w_s, v_gm_b_s, v_gm_w_out):
    args = locals()
    W = {n: args[n] for n in WEIGHTS}
    M = {n: args["m_" + n] for n in WEIGHTS}
    V = {n: args["v_" + n] for n in WEIGHTS}
    return _step(x, c, ctx, loss_target, W, M, V)
```

```python
import functools
import math

import jax
import jax.numpy as jnp
from jax import lax
from jax.experimental import pallas as pl
from jax.experimental.pallas import tpu as pltpu

F32 = jnp.float32
MMT = jnp.bfloat16
SDS = jax.ShapeDtypeStruct
MESH = pl.DeviceIdType.MESH

EPS = 1e-6
HEAD_DIM = 64
Q_PER_KV = 4
ATTN_BLOCK = 128
GRID_W = 64
ROPE_BASE = 10000.0
GMLP_CHUNK = 128
GMLP_GROUP_DIM = 128
CONV_WIDTH = 31
FFN_CONV_WIDTH = 3
NEG = -1e30

ADAM_LR, ADAM_B1, ADAM_B2, ADAM_EPS, ADAM_WD, ADAM_STEP = 0.001, 0.9, 0.999, 1e-08, 0.01, 10

LANES = 128
SUBLANES = 8
VMEM_LIMIT = 52 * 1024 * 1024
CONV_ROWS = 128
N_CHIPS = 4
N_DEV = 8


def _cparams(*sem):
    return pltpu.CompilerParams(dimension_semantics=sem if sem else None, vmem_limit_bytes=VMEM_LIMIT)


def _tile(n, cap, mult=LANES):
    best = None
    for d in range(mult, min(n, cap) + 1, mult):
        if n % d == 0:
            best = d
    return best if best is not None else n


def _sum0(v):
    return jnp.sum(v, axis=0, keepdims=True)


def _rms(v):
    r = lax.rsqrt(jnp.mean(v * v, axis=-1, keepdims=True) + EPS)
    return v * r, r


def _sig(v):
    return jax.nn.sigmoid(v)


def _dot(a, b, ca, cb):
    return lax.dot_general(a.astype(MMT), b.astype(MMT), (((ca,), (cb,)), ((), ())), preferred_element_type=F32)


def _mm(a, b, mode, out_dtype, name):
    if mode == "nn":
        (M, K), N = a.shape, b.shape[1]
    elif mode == "nt":
        (M, K), N = a.shape, b.shape[0]
    else:
        (K, M), N = a.shape, b.shape[1]
    tm, tn, tk = _tile(M, 512), _tile(N, 1408), _tile(K, 1536)
    nk = K // tk
    ca, cb = {"nn": (1, 0), "nt": (1, 1), "tn": (0, 0)}[mode]

    def body(a_ref, b_ref, o_ref, acc):
        k = pl.program_id(2)

        @pl.when(k == 0)
        def _():
            acc[...] = jnp.zeros_like(acc)

        acc[...] += _dot(a_ref[...], b_ref[...], ca, cb)

        @pl.when(k == nk - 1)
        def _():
            o_ref[...] = acc[...].astype(o_ref.dtype)

    a_spec = pl.BlockSpec((tk, tm), lambda i, j, k: (k, i)) if mode == "tn" else pl.BlockSpec((tm, tk), lambda i, j, k: (i, k))
    b_spec = pl.BlockSpec((tn, tk), lambda i, j, k: (j, k)) if mode == "nt" else pl.BlockSpec((tk, tn), lambda i, j, k: (k, j))
    return pl.pallas_call(
        body, name=name, grid=(M // tm, N // tn, nk), in_specs=[a_spec, b_spec],
        out_specs=pl.BlockSpec((tm, tn), lambda i, j, k: (i, j)), out_shape=SDS((M, N), out_dtype),
        scratch_shapes=[pltpu.VMEM((tm, tn), F32)], compiler_params=_cparams("parallel", "parallel", "arbitrary"))(a, b)


def _seg_of(nl, nseg):
    return (lambda i: jnp.where(i >= nl, 1, 0)) if nseg == 2 else (lambda i: 0)


def _prenorm(h, mod, gn, which, rows, nl, tm, name):
    D = h.shape[1]
    nseg = mod.shape[0]
    seg = _seg_of(nl, nseg)
    sh_i, sc_i = (0, 1) if which == 0 else (3, 4)

    def body(h_ref, mod_ref, gn_ref, a_ref):
        n, _ = _rms(h_ref[...])
        a_ref[...] = (n * gn_ref[...] * (1.0 + mod_ref[pl.ds(sc_i, 1), :]) + mod_ref[pl.ds(sh_i, 1), :]).astype(a_ref.dtype)

    return pl.pallas_call(
        body, name=name, grid=(rows // tm,),
        in_specs=[pl.BlockSpec((tm, D), lambda i: (i, 0)), pl.BlockSpec((None, 6, D), lambda i: (seg(i), 0, 0)),
                  pl.BlockSpec((1, D), lambda i: (0, 0))],
        out_specs=pl.BlockSpec((tm, D), lambda i: (i, 0)), out_shape=SDS((rows, D), MMT),
        compiler_params=_cparams("parallel"))(h, mod, gn)


def _acc_spec(D, seg):
    return pl.BlockSpec((None, 1, D), lambda i: (seg(i), 0, 0))


def _prenorm_bwd(h, da, dh_in, mod, gn, which, rows, nl, tm, name):
    D = h.shape[1]
    nseg = mod.shape[0]
    seg = _seg_of(nl, nseg)
    sc_i = 1 if which == 0 else 4

    def body(h_ref, da_ref, dhin_ref, mod_ref, gn_ref, dh_ref, dsh_ref, dsc_ref, dgn_ref):
        i = pl.program_id(0)
        first = (i == 0) | (i == nl) if nseg == 2 else (i == 0)

        @pl.when(first)
        def _():
            dsh_ref[...] = jnp.zeros_like(dsh_ref)
            dsc_ref[...] = jnp.zeros_like(dsc_ref)
            dgn_ref[...] = jnp.zeros_like(dgn_ref)

        n, r = _rms(h_ref[...])
        da_v = da_ref[...].astype(F32)
        gn_v = gn_ref[...]
        sc1 = 1.0 + mod_ref[pl.ds(sc_i, 1), :]
        dsh_ref[...] += _sum0(da_v)
        dsc_ref[...] += _sum0(da_v * (n * gn_v))
        dgn_ref[...] += _sum0(da_v * n * sc1)
        dn = da_v * (gn_v * sc1)
        dh_ref[...] = dhin_ref[...] + r * (dn - n * jnp.mean(dn * n, axis=-1, keepdims=True))

    row = pl.BlockSpec((tm, D), lambda i: (i, 0))
    acc = SDS((nseg, 1, D), F32)
    return pl.pallas_call(
        body, name=name, grid=(rows // tm,),
        in_specs=[row, row, row, pl.BlockSpec((None, 6, D), lambda i: (seg(i), 0, 0)), pl.BlockSpec((1, D), lambda i: (0, 0))],
        out_specs=[row, _acc_spec(D, seg), _acc_spec(D, seg), _acc_spec(D, seg)],
        out_shape=[SDS((rows, D), F32), acc, acc, acc], compiler_params=_cparams("arbitrary"))(h, da, dh_in, mod, gn)


def _postnorm(h, y, bias, mod, gn, gate_i, rows, nl, tm, name):
    D = h.shape[1]
    nseg = mod.shape[0]
    seg = _seg_of(nl, nseg)

    def body(h_ref, y_ref, b_ref, mod_ref, gn_ref, o_ref):
        ny, _ = _rms(y_ref[...] + b_ref[...])
        o_ref[...] = h_ref[...] + mod_ref[pl.ds(gate_i, 1), :] * (ny * gn_ref[...])

    row = pl.BlockSpec((tm, D), lambda i: (i, 0))
    vec = pl.BlockSpec((1, D), lambda i: (0, 0))
    return pl.pallas_call(
        body, name=name, grid=(rows // tm,),
        in_specs=[row, row, vec, pl.BlockSpec((None, 6, D), lambda i: (seg(i), 0, 0)), vec],
        out_specs=row, out_shape=SDS((rows, D), F32), compiler_params=_cparams("parallel"))(h, y, bias, mod, gn)


def _postnorm_bwd(dh, y, bias, mod, gn, gate_i, rows, nl, tm, name):
    D = y.shape[1]
    nseg = mod.shape[0]
    seg = _seg_of(nl, nseg)

    def body(dh_ref, y_ref, b_ref, mod_ref, gn_ref, dy_ref, dg_ref, dgn_ref, db_ref):
        i = pl.program_id(0)
        first = (i == 0) | (i == nl) if nseg == 2 else (i == 0)

        @pl.when(first)
        def _():
            dg_ref[...] = jnp.zeros_like(dg_ref)
            dgn_ref[...] = jnp.zeros_like(dgn_ref)
            db_ref[...] = jnp.zeros_like(db_ref)

        ny, ry = _rms(y_ref[...] + b_ref[...])
        g = mod_ref[pl.ds(gate_i, 1), :]
        gn_v = gn_ref[...]
        dh_v = dh_ref[...]
        dg_ref[...] += _sum0(dh_v * (ny * gn_v))
        dgn_ref[...] += _sum0(dh_v * ny * g)
        dny = dh_v * (g * gn_v)
        dy = ry * (dny - ny * jnp.mean(dny * ny, axis=-1, keepdims=True))
        db_ref[...] += _sum0(dy)
        dy_ref[...] = dy.astype(dy_ref.dtype)

    row = pl.BlockSpec((tm, D), lambda i: (i, 0))
    vec = pl.BlockSpec((1, D), lambda i: (0, 0))
    acc = SDS((nseg, 1, D), F32)
    return pl.pallas_call(
        body, name=name, grid=(rows // tm,),
        in_specs=[row, row, vec, pl.BlockSpec((None, 6, D), lambda i: (seg(i), 0, 0)), vec],
        out_specs=[row, _acc_spec(D, seg), _acc_spec(D, seg), _acc_spec(D, seg)],
        out_shape=[SDS((rows, D), MMT), acc, acc, acc], compiler_params=_cparams("arbitrary"))(dh, y, bias, mod, gn)


def _seg_layout(segs, H):
    out, base = [], H
    for s0, n in segs:
        out.append((s0, n, base))
        base += n + H
    return out, base


def _zero_pads(ref, lay, H):
    width = ref.shape[1]
    ref[pl.ds(0, H), :] = jnp.zeros((H, width), ref.dtype)
    for _, n, base in lay:
        ref[pl.ds(base + n, H), :] = jnp.zeros((H, width), ref.dtype)


def _window(ref, base, off, H):
    return ref[pl.ds(base - H + off, CONV_ROWS + 2 * H), :]


def _taps(win, H, offs):
    W = CONV_ROWS + 2 * H
    rolled, out = {}, {}
    for o in offs:
        s = H + o
        b = s % SUBLANES
        if b not in rolled:
            rolled[b] = win if b == 0 else pltpu.roll(win, shift=W - b, axis=0)
        out[o] = rolled[b][s - b:s - b + CONV_ROWS, :]
    return out


def _chunks(lay, fn):
    for s0, n, base in lay:
        def step(r, carry, s0=s0, base=base):
            fn(s0, base, pl.multiple_of(r * CONV_ROWS, CONV_ROWS))
            return carry
        lax.fori_loop(0, n // CONV_ROWS, step, 0)


def _ffn_gate(z0, conv_w, conv_b, segs, name):
    T, F2 = z0.shape
    F = F2 // 2
    tc = _tile(F, 256)
    nF = F // tc
    H = SUBLANES
    lay, srows = _seg_layout(segs, H)
    offs = [-1, 0, 1]

    def body(zg_ref, zv_ref, wg_ref, wv_ref, bg_ref, bv_ref, u_ref, xg, xv):
        _zero_pads(xg, lay, H)
        _zero_pads(xv, lay, H)
        for s0, n, base in lay:
            xg[pl.ds(base, n), :] = zg_ref[pl.ds(s0, n), :]
            xv[pl.ds(base, n), :] = zv_ref[pl.ds(s0, n), :]

        def chunk(s0, base, off):
            tg = _taps(_window(xg, base, off, H), H, offs)
            tv = _taps(_window(xv, base, off, H), H, offs)
            zg = bg_ref[...] + sum(tg[k - 1] * wg_ref[pl.ds(k, 1), :] for k in range(3))
            zv = bv_ref[...] + sum(tv[k - 1] * wv_ref[pl.ds(k, 1), :] for k in range(3))
            u_ref[pl.ds(s0 + off, CONV_ROWS), :] = (zg * _sig(zg) * zv).astype(u_ref.dtype)

        _chunks(lay, chunk)

    colg = lambda r: pl.BlockSpec((r, tc), lambda j: (0, j))
    colv = lambda r: pl.BlockSpec((r, tc), lambda j: (0, j + nF))
    return pl.pallas_call(
        body, name=name, grid=(nF,),
        in_specs=[colg(T), colv(T), colg(3), colv(3), colg(1), colv(1)],
        out_specs=colg(T), out_shape=SDS((T, F), MMT),
        scratch_shapes=[pltpu.VMEM((srows, tc), F32), pltpu.VMEM((srows, tc), F32)],
        compiler_params=_cparams("parallel"))(z0, z0, conv_w, conv_w, conv_b, conv_b)


def _ffn_gate_bwd(z0, du, conv_w, conv_b, segs, name):
    T, F2 = z0.shape
    F = F2 // 2
    tc = _tile(F, 256)
    nF = F // tc
    H = SUBLANES
    lay, srows = _seg_layout(segs, H)
    offs = [-1, 0, 1]

    def body(zo_ref, zt_ref, du_ref, wo_ref, wt_ref, bo_ref, bt_ref, u_ref, dz0_ref, dw_ref, db_ref, xo, xt, dzp):
        own_is_gate = pl.program_id(1) == 0
        for ref in (xo, xt, dzp):
            _zero_pads(ref, lay, H)
        for s0, n, base in lay:
            xo[pl.ds(base, n), :] = zo_ref[pl.ds(s0, n), :]
            xt[pl.ds(base, n), :] = zt_ref[pl.ds(s0, n), :]

        def grads(s0, base, off):
            to = _taps(_window(xo, base, off, H), H, offs)
            tt = _taps(_window(xt, base, off, H), H, offs)
            zo = bo_ref[...] + sum(to[k - 1] * wo_ref[pl.ds(k, 1), :] for k in range(3))
            zt = bt_ref[...] + sum(tt[k - 1] * wt_ref[pl.ds(k, 1), :] for k in range(3))
            so, st = _sig(zo), _sig(zt)
            du_v = du_ref[pl.ds(s0 + off, CONV_ROWS), :]
            d_gate = du_v * zt * (so * (1.0 + zo * (1.0 - so)))
            d_val = du_v * (zt * st)
            dzp[pl.ds(base + off, CONV_ROWS), :] = jnp.where(own_is_gate, d_gate, d_val)

            @pl.when(own_is_gate)
            def _():
                u_ref[pl.ds(s0 + off, CONV_ROWS), :] = (zo * so * zt).astype(u_ref.dtype)

        _chunks(lay, grads)
        dw_ref[...] = jnp.zeros_like(dw_ref)
        db_ref[...] = jnp.zeros_like(db_ref)

        def back(s0, base, off):
            td = _taps(_window(dzp, base, off, H), H, offs)
            tx = _taps(_window(xo, base, off, H), H, offs)
            dz0 = sum(td[1 - k] * wo_ref[pl.ds(k, 1), :] for k in range(3))
            dz0_ref[pl.ds(s0 + off, CONV_ROWS), :] = dz0.astype(dz0_ref.dtype)
            db_ref[...] += _sum0(td[0])
            for k in range(3):
                dw_ref[pl.ds(k, 1), :] += _sum0(td[0] * tx[k - 1])

        _chunks(lay, back)

    own = lambda r: pl.BlockSpec((r, tc), lambda j, hf: (0, hf * nF + j))
    oth = lambda r: pl.BlockSpec((r, tc), lambda j, hf: (0, (1 - hf) * nF + j))
    ucol = pl.BlockSpec((T, tc), lambda j, hf: (0, j))
    return pl.pallas_call(
        body, name=name, grid=(nF, 2),
        in_specs=[own(T), oth(T), ucol, own(3), oth(3), own(1), oth(1)],
        out_specs=[ucol, own(T), own(3), own(1)],
        out_shape=[SDS((T, F), MMT), SDS((T, F2), MMT), SDS((3, F2), F32), SDS((1, F2), F32)],
        scratch_shapes=[pltpu.VMEM((srows, tc), F32)] * 3,
        compiler_params=_cparams("parallel", "arbitrary"))(z0, z0, du, conv_w, conv_w, conv_b, conv_b)


def _glu_conv(p0, b_in, dw_w, dw_b, segs, name):
    T, D2 = p0.shape
    D = D2 // 2
    tc = _tile(D, 256)
    nD = D // tc
    H = 2 * SUBLANES
    half = (CONV_WIDTH - 1) // 2
    lay, srows = _seg_layout(segs, H)
    offs = list(range(-half, half + 1))

    def body(pa_ref, pg_ref, ba_ref, bg_ref, w_ref, b_ref, z2_ref, z1p):
        _zero_pads(z1p, lay, H)

        def glu(s0, base, off):
            rows = pl.ds(s0 + off, CONV_ROWS)
            z1p[pl.ds(base + off, CONV_ROWS), :] = (pa_ref[rows, :] + ba_ref[...]) * _sig(pg_ref[rows, :] + bg_ref[...])

        _chunks(lay, glu)

        def conv(s0, base, off):
            t = _taps(_window(z1p, base, off, H), H, offs)
            acc = b_ref[...] + t[-half] * w_ref[pl.ds(0, 1), :]
            for k in range(1, CONV_WIDTH):
                acc = acc + t[k - half] * w_ref[pl.ds(k, 1), :]
            z2_ref[pl.ds(s0 + off, CONV_ROWS), :] = acc

        _chunks(lay, conv)

    cola = lambda r: pl.BlockSpec((r, tc), lambda j: (0, j))
    colg = lambda r: pl.BlockSpec((r, tc), lambda j: (0, j + nD))
    return pl.pallas_call(
        body, name=name, grid=(nD,),
        in_specs=[cola(T), colg(T), cola(1), colg(1), cola(CONV_WIDTH), cola(1)],
        out_specs=cola(T), out_shape=SDS((T, D), F32), scratch_shapes=[pltpu.VMEM((srows, tc), F32)],
        compiler_params=_cparams("parallel"))(p0, p0, b_in, b_in, dw_w, dw_b)


def _glu_conv_bwd(p0, b_in, dw_w, dz2, segs, name):
    T, D2 = p0.shape
    D = D2 // 2
    tc = _tile(D, 256)
    nD = D // tc
    H = 2 * SUBLANES
    half = (CONV_WIDTH - 1) // 2
    lay, srows = _seg_layout(segs, H)
    offs = list(range(-half, half + 1))

    def body(pa_ref, pg_ref, ba_ref, bg_ref, w_ref, dz2_ref, dpa_ref, dpg_ref, dw_ref, db_ref, dba_ref, dbg_ref, z1p, dzp):
        _zero_pads(z1p, lay, H)
        _zero_pads(dzp, lay, H)
        for s0, n, base in lay:
            dzp[pl.ds(base, n), :] = dz2_ref[pl.ds(s0, n), :]

        def glu(s0, base, off):
            rows = pl.ds(s0 + off, CONV_ROWS)
            z1p[pl.ds(base + off, CONV_ROWS), :] = (pa_ref[rows, :] + ba_ref[...]) * _sig(pg_ref[rows, :] + bg_ref[...])

        _chunks(lay, glu)
        for ref in (dw_ref, db_ref, dba_ref, dbg_ref):
            ref[...] = jnp.zeros_like(ref)

        def back(s0, base, off):
            td = _taps(_window(dzp, base, off, H), H, offs)
            tz = _taps(_window(z1p, base, off, H), H, offs)
            dz1 = td[half] * w_ref[pl.ds(0, 1), :]
            for k in range(1, CONV_WIDTH):
                dz1 = dz1 + td[half - k] * w_ref[pl.ds(k, 1), :]
            db_ref[...] += _sum0(td[0])
            for k in range(CONV_WIDTH):
                dw_ref[pl.ds(k, 1), :] += _sum0(td[0] * tz[k - half])
            rows = pl.ds(s0 + off, CONV_ROWS)
            pa = pa_ref[rows, :] + ba_ref[...]
            sg = _sig(pg_ref[rows, :] + bg_ref[...])
            dpa = dz1 * sg
            dpg = dz1 * pa * (sg * (1.0 - sg))
            dba_ref[...] += _sum0(dpa)
            dbg_ref[...] += _sum0(dpg)
            dpa_ref[rows, :] = dpa.astype(dpa_ref.dtype)
            dpg_ref[rows, :] = dpg.astype(dpg_ref.dtype)

        _chunks(lay, back)

    cola = lambda r: pl.BlockSpec((r, tc), lambda j: (0, j))
    colg = lambda r: pl.BlockSpec((r, tc), lambda j: (0, j + nD))
    return pl.pallas_call(
        body, name=name, grid=(nD,),
        in_specs=[cola(T), colg(T), cola(1), colg(1), cola(CONV_WIDTH), cola(T)],
        out_specs=[cola(T), cola(T), cola(CONV_WIDTH), cola(1), cola(1), cola(1)],
        out_shape=[SDS((T, D), MMT), SDS((T, D), MMT), SDS((CONV_WIDTH, D), F32), SDS((1, D), F32), SDS((1, D), F32),
                   SDS((1, D), F32)],
        scratch_shapes=[pltpu.VMEM((srows, tc), F32)] * 2, compiler_params=_cparams("parallel"))(p0, p0, b_in, b_in, dw_w, dz2)


def _layer_norm_stats(v):
    mu = jnp.mean(v, axis=-1, keepdims=True)
    var = jnp.mean(jnp.square(v - mu), axis=-1, keepdims=True)
    rstd = lax.rsqrt(var + EPS)
    return (v - mu) * rstd, rstd


def _ln_silu(z2, ln_g, ln_b, rows, tm, name):
    D = z2.shape[1]

    def body(z_ref, g_ref, b_ref, o_ref):
        xh, _ = _layer_norm_stats(z_ref[...])
        z3 = xh * g_ref[...] + b_ref[...]
        o_ref[...] = (z3 * _sig(z3)).astype(o_ref.dtype)

    row = pl.BlockSpec((tm, D), lambda i: (i, 0))
    vec = pl.BlockSpec((1, D), lambda i: (0, 0))
    return pl.pallas_call(body, name=name, grid=(rows // tm,), in_specs=[row, vec, vec], out_specs=row,
                          out_shape=SDS((rows, D), MMT), compiler_params=_cparams("parallel"))(z2, ln_g, ln_b)


def _ln_silu_bwd(z2, dz4, ln_g, ln_b, rows, tm, name):
    D = z2.shape[1]

    def body(z_ref, d_ref, g_ref, b_ref, dz_ref, dg_ref, db_ref):
        @pl.when(pl.program_id(0) == 0)
        def _():
            dg_ref[...] = jnp.zeros_like(dg_ref)
            db_ref[...] = jnp.zeros_like(db_ref)

        xh, rstd = _layer_norm_stats(z_ref[...])
        z3 = xh * g_ref[...] + b_ref[...]
        s = _sig(z3)
        dz3 = d_ref[...] * (s * (1.0 + z3 * (1.0 - s)))
        dg_ref[...] += _sum0(dz3 * xh)
        db_ref[...] += _sum0(dz3)
        dxh = dz3 * g_ref[...]
        dz_ref[...] = rstd * (dxh - jnp.mean(dxh, axis=-1, keepdims=True) - xh * jnp.mean(dxh * xh, axis=-1, keepdims=True))

    row = pl.BlockSpec((tm, D), lambda i: (i, 0))
    vec = pl.BlockSpec((1, D), lambda i: (0, 0))
    return pl.pallas_call(body, name=name, grid=(rows // tm,), in_specs=[row, row, vec, vec], out_specs=[row, vec, vec],
                          out_shape=[SDS((rows, D), F32), SDS((1, D), F32), SDS((1, D), F32)],
                          compiler_params=_cparams("arbitrary"))(z2, dz4, ln_g, ln_b)


def _rot_half_pairs(v):
    width = v.shape[1]
    lane = lax.broadcasted_iota(jnp.int32, v.shape, 1)
    return jnp.where((lane % 32) < 16, -pltpu.roll(v, shift=width - 16, axis=1), pltpu.roll(v, shift=16, axis=1))


def _rope(qkv, cos, sin, L, qk, tm, name):
    width = qkv.shape[1]
    kv = width - qk

    def body(x_ref, c_ref, s_ref, qk_ref, v_ref):
        xv = x_ref[:, pl.ds(0, qk)]
        c = jnp.tile(c_ref[...], (1, qk // LANES))
        s = jnp.tile(s_ref[...], (1, qk // LANES))
        qk_ref[...] = (xv * c + _rot_half_pairs(xv) * s).astype(qk_ref.dtype)
        v_ref[...] = x_ref[:, pl.ds(qk, kv)].astype(v_ref.dtype)

    tab = pl.BlockSpec((tm, LANES), lambda i: (i, 0))
    return pl.pallas_call(
        body, name=name, grid=(L // tm,), in_specs=[pl.BlockSpec((tm, width), lambda i: (i, 0)), tab, tab],
        out_specs=[pl.BlockSpec((tm, qk), lambda i: (i, 0)), pl.BlockSpec((tm, kv), lambda i: (i, 0))],
        out_shape=[SDS((L, qk), MMT), SDS((L, kv), MMT)], compiler_params=_cparams("parallel"))(qkv, cos, sin)


def _rope_bwd(dqk, dv, cos, sin, tm, name):
    L, qk = dqk.shape
    kv = dv.shape[1]

    def body(d_ref, dv_ref, c_ref, s_ref, o_ref):
        dv_ = d_ref[...]
        c = jnp.tile(c_ref[...], (1, qk // LANES))
        s = jnp.tile(s_ref[...], (1, qk // LANES))
        o_ref[:, pl.ds(0, qk)] = (dv_ * c - _rot_half_pairs(dv_ * s)).astype(o_ref.dtype)
        o_ref[:, pl.ds(qk, kv)] = dv_ref[...].astype(o_ref.dtype)

    tab = pl.BlockSpec((tm, LANES), lambda i: (i, 0))
    return pl.pallas_call(
        body, name=name, grid=(L // tm,),
        in_specs=[pl.BlockSpec((tm, qk), lambda i: (i, 0)), pl.BlockSpec((tm, kv), lambda i: (i, 0)), tab, tab],
        out_specs=pl.BlockSpec((tm, qk + kv), lambda i: (i, 0)), out_shape=SDS((L, qk + kv), MMT),
        compiler_params=_cparams("parallel"))(dqk, dv, cos, sin)


def _band_specs(nb, width):
    blk = lambda f: pl.BlockSpec((None, ATTN_BLOCK, width), f)
    return [blk(lambda h, n: (h, jnp.maximum(n - 1, 0), 0)), blk(lambda h, n: (h, n, 0)),
            blk(lambda h, n: (h, jnp.minimum(n + 1, nb - 1), 0))]


def _window_mask(n, L):
    qi = lax.broadcasted_iota(jnp.int32, (ATTN_BLOCK, 3 * ATTN_BLOCK), 0)
    kk = lax.broadcasted_iota(jnp.int32, (ATTN_BLOCK, 3 * ATTN_BLOCK), 1)
    key_abs = (n - 1) * ATTN_BLOCK + kk
    return (jnp.abs(qi + ATTN_BLOCK - kk) <= ATTN_BLOCK) & (key_abs >= 0) & (key_abs < L)


def _attn_fwd(q, k, v, kc, vc, sink, name):
    nkv, _, L, hd = q.shape
    C = kc.shape[1]
    nb = L // ATTN_BLOCK
    scale = HEAD_DIM ** -0.5

    def body(sink_ref, q_ref, k0, k1, k2, v0, v1, v2, kc_ref, vc_ref, o_ref, lse_ref):
        hh, n = pl.program_id(0), pl.program_id(1)
        kw = jnp.concatenate([k0[...], k1[...], k2[...]], axis=0)
        vw = jnp.concatenate([v0[...], v1[...], v2[...]], axis=0)
        mask = _window_mask(n, L)
        for g in range(Q_PER_KV):
            qg = q_ref[g]
            sw = jnp.where(mask, _dot(qg, kw, 1, 1) * scale, NEG)
            sc = _dot(qg, kc_ref[...], 1, 1) * scale
            sk = sink_ref[hh * Q_PER_KV + g]
            m = jnp.maximum(jnp.maximum(jnp.max(sw, axis=-1, keepdims=True), jnp.max(sc, axis=-1, keepdims=True)), sk)
            pw, pc = jnp.exp(sw - m), jnp.exp(sc - m)
            den = jnp.sum(pw, axis=-1, keepdims=True) + jnp.sum(pc, axis=-1, keepdims=True) + jnp.exp(sk - m)
            inv = 1.0 / den
            o_ref[g] = _dot(pw * inv, vw, 1, 0) + _dot(pc * inv, vc_ref[...], 1, 0)
            lse_ref[g] = m + jnp.log(den)

    qspec = pl.BlockSpec((None, Q_PER_KV, ATTN_BLOCK, hd), lambda h, n: (h, 0, n, 0))
    cspec = pl.BlockSpec((None, C, hd), lambda h, n: (h, 0, 0))
    return pl.pallas_call(
        body, name=name, grid=(nkv, nb),
        in_specs=[pl.BlockSpec(memory_space=pltpu.SMEM), qspec] + _band_specs(nb, hd) + _band_specs(nb, hd) + [cspec, cspec],
        out_specs=[qspec, pl.BlockSpec((None, Q_PER_KV, ATTN_BLOCK, 1), lambda h, n: (h, 0, n, 0))],
        out_shape=[SDS((nkv, Q_PER_KV, L, hd), F32), SDS((nkv, Q_PER_KV, L, 1), F32)],
        compiler_params=_cparams("parallel", "parallel"))(sink, q, k, k, k, v, v, v, kc, vc)


def _attn_bwd_q(q, k, v, kc, vc, sink, o, do, lse, name):
    nkv, _, L, hd = q.shape
    C = kc.shape[1]
    nb = L // ATTN_BLOCK
    scale = HEAD_DIM ** -0.5

    def body(sink_ref, q_ref, k0, k1, k2, v0, v1, v2, kc_ref, vc_ref, o_ref, do_ref, lse_ref, dq_ref, dkc_ref, dvc_ref, dsk_ref):
        hh, n = pl.program_id(0), pl.program_id(1)

        @pl.when(n == 0)
        def _():
            dkc_ref[...] = jnp.zeros_like(dkc_ref)
            dvc_ref[...] = jnp.zeros_like(dvc_ref)
            dsk_ref[...] = jnp.zeros_like(dsk_ref)

        kw = jnp.concatenate([k0[...], k1[...], k2[...]], axis=0)
        vw = jnp.concatenate([v0[...], v1[...], v2[...]], axis=0)
        mask = _window_mask(n, L)
        for g in range(Q_PER_KV):
            qg, dog, lse_g = q_ref[g], do_ref[g], lse_ref[g]
            delta = jnp.sum(dog.astype(F32) * o_ref[g], axis=-1, keepdims=True)
            pw = jnp.exp(jnp.where(mask, _dot(qg, kw, 1, 1) * scale, NEG) - lse_g)
            pc = jnp.exp(_dot(qg, kc_ref[...], 1, 1) * scale - lse_g)
            dsw = pw * (_dot(dog, vw, 1, 1) - delta)
            dsc = pc * (_dot(dog, vc_ref[...], 1, 1) - delta)
            dq_ref[g] = (_dot(dsw, kw, 1, 0) + _dot(dsc, kc_ref[...], 1, 0)) * scale
            dkc_ref[...] += _dot(dsc, qg, 0, 0) * scale
            dvc_ref[...] += _dot(pc, dog, 0, 0)
            psk = jnp.exp(sink_ref[hh * Q_PER_KV + g] - lse_g)
            dsk_ref[pl.ds(g, 1), :] += jnp.broadcast_to(jnp.sum(-psk * delta, axis=0, keepdims=True), (1, LANES))

    qspec = pl.BlockSpec((None, Q_PER_KV, ATTN_BLOCK, hd), lambda h, n: (h, 0, n, 0))
    lspec = pl.BlockSpec((None, Q_PER_KV, ATTN_BLOCK, 1), lambda h, n: (h, 0, n, 0))
    cspec = pl.BlockSpec((None, C, hd), lambda h, n: (h, 0, 0))
    return pl.pallas_call(
        body, name=name, grid=(nkv, nb),
        in_specs=[pl.BlockSpec(memory_space=pltpu.SMEM), qspec] + _band_specs(nb, hd) + _band_specs(nb, hd)
        + [cspec, cspec, qspec, qspec, lspec],
        out_specs=[qspec, cspec, cspec, pl.BlockSpec((None, SUBLANES, LANES), lambda h, n: (h, 0, 0))],
        out_shape=[SDS((nkv, Q_PER_KV, L, hd), F32), SDS((nkv, C, hd), F32), SDS((nkv, C, hd), F32),
                   SDS((nkv, SUBLANES, LANES), F32)],
        compiler_params=_cparams("parallel", "arbitrary"))(sink, q, k, k, k, v, v, v, kc, vc, o, do, lse)


def _attn_bwd_kv(q, k, v, o, do, lse, name):
    nkv, _, L, hd = q.shape
    nb = L // ATTN_BLOCK
    scale = HEAD_DIM ** -0.5

    def body(q0, q1, q2, do0, do1, do2, o0, o1, o2, l0, l1, l2, k_ref, v_ref, dk_ref, dv_ref):
        j = pl.program_id(1)
        qi = lax.broadcasted_iota(jnp.int32, (ATTN_BLOCK, ATTN_BLOCK), 0)
        kk = lax.broadcasted_iota(jnp.int32, (ATTN_BLOCK, ATTN_BLOCK), 1)
        kj, vj = k_ref[...], v_ref[...]
        dk = jnp.zeros((ATTN_BLOCK, hd), F32)
        dv = jnp.zeros((ATTN_BLOCK, hd), F32)
        for slot, (q_r, do_r, o_r, l_r) in enumerate(((q0, do0, o0, l0), (q1, do1, o1, l1), (q2, do2, o2, l2))):
            n = j - 1 + slot
            ok = (n >= 0) & (n < nb) & (jnp.abs(qi + ATTN_BLOCK - ((2 - slot) * ATTN_BLOCK + kk)) <= ATTN_BLOCK)
            for g in range(Q_PER_KV):
                qg, dog = q_r[g], do_r[g]
                delta = jnp.sum(dog.astype(F32) * o_r[g], axis=-1, keepdims=True)
                p = jnp.exp(jnp.where(ok, _dot(qg, kj, 1, 1) * scale - l_r[g], NEG))
                ds = p * (_dot(dog, vj, 1, 1) - delta)
                dk = dk + _dot(ds, qg, 0, 0) * scale
                dv = dv + _dot(p, dog, 0, 0)
        dk_ref[...] = dk
        dv_ref[...] = dv

    def band(width):
        blk = lambda f: pl.BlockSpec((None, Q_PER_KV, ATTN_BLOCK, width), f)
        return [blk(lambda h, j: (h, 0, jnp.maximum(j - 1, 0), 0)), blk(lambda h, j: (h, 0, j, 0)),
                blk(lambda h, j: (h, 0, jnp.minimum(j + 1, nb - 1), 0))]

    kspec = pl.BlockSpec((None, ATTN_BLOCK, hd), lambda h, j: (h, j, 0))
    return pl.pallas_call(
        body, name=name, grid=(nkv, nb), in_specs=band(hd) + band(hd) + band(hd) + band(1) + [kspec, kspec],
        out_specs=[kspec, kspec], out_shape=[SDS((nkv, L, hd), F32), SDS((nkv, L, hd), F32)],
        compiler_params=_cparams("parallel", "parallel"))(q, q, q, do, do, do, o, o, o, lse, lse, lse, k, v)


_GELU_K = math.sqrt(2.0 / math.pi)


def _gelu(v):
    return 0.5 * v * (1.0 + jnp.tanh(_GELU_K * (v + 0.044715 * (v * v * v))))


def _gelu_grad(v):
    t = jnp.tanh(_GELU_K * (v + 0.044715 * (v * v * v)))
    return 0.5 * (1.0 + t) + 0.5 * v * (1.0 - t * t) * (_GELU_K * (1.0 + 3.0 * 0.044715 * (v * v)))


def _gmlp_fwd(p0, b_in, ln_g, ln_b, w_s, b_s, name):
    L, W2 = p0.shape
    W = W2 // 2
    G = W // GMLP_GROUP_DIM

    def body(p_ref, bi_ref, g_ref, b_ref, ws_ref, bs_ref, o_ref):
        ge = _gelu(p_ref[...] + bi_ref[...])
        xh, _ = _layer_norm_stats(ge[:, W:])
        vln = xh * g_ref[...] + b_ref[...]
        for gi in range(G):
            cols = slice(gi * GMLP_GROUP_DIM, (gi + 1) * GMLP_GROUP_DIM)
            s = _dot(ws_ref[gi], vln[:, cols], 1, 0) + bs_ref[gi]
            o_ref[:, cols] = (ge[:, cols] * s).astype(o_ref.dtype)

    full = lambda shape: pl.BlockSpec(shape, lambda i: (0,) * len(shape))
    return pl.pallas_call(
        body, name=name, grid=(L // GMLP_CHUNK,),
        in_specs=[pl.BlockSpec((GMLP_CHUNK, W2), lambda i: (i, 0)), full((1, W2)), full((1, W)), full((1, W)),
                  full((G, GMLP_CHUNK, GMLP_CHUNK)), full((G, GMLP_CHUNK, 1))],
        out_specs=pl.BlockSpec((GMLP_CHUNK, W), lambda i: (i, 0)), out_shape=SDS((L, W), MMT),
        compiler_params=_cparams("parallel"))(p0, b_in, ln_g, ln_b, w_s, b_s)


def _gmlp_bwd(p0, dus, b_in, ln_g, ln_b, w_s, w_st, b_s, name):
    L, W2 = p0.shape
    W = W2 // 2
    G = W // GMLP_GROUP_DIM

    def body(p_ref, d_ref, bi_ref, g_ref, b_ref, ws_ref, wst_ref, bs_ref, dpre_ref, dbi_ref, dg_ref, db_ref, dws_ref, dbs_ref, dvln):
        @pl.when(pl.program_id(0) == 0)
        def _():
            for ref in (dbi_ref, dg_ref, db_ref, dws_ref, dbs_ref):
                ref[...] = jnp.zeros_like(ref)

        pre = p_ref[...] + bi_ref[...]
        ge = _gelu(pre)
        xh, rstd = _layer_norm_stats(ge[:, W:])
        vln = xh * g_ref[...] + b_ref[...]
        dge_u = []
        for gi in range(G):
            cols = slice(gi * GMLP_GROUP_DIM, (gi + 1) * GMLP_GROUP_DIM)
            vg = vln[:, cols]
            s = _dot(ws_ref[gi], vg, 1, 0) + bs_ref[gi]
            dus_g = d_ref[:, cols]
            dge_u.append(dus_g * s)
            ds = dus_g * ge[:, cols]
            dbs_ref[gi] += jnp.sum(ds, axis=1, keepdims=True)
            dws_ref[gi] += _dot(ds, vg, 1, 1)
            dvln[:, cols] = _dot(wst_ref[gi], ds, 1, 0)
        dv = dvln[...]
        dg_ref[...] += _sum0(dv * xh)
        db_ref[...] += _sum0(dv)
        dxh = dv * g_ref[...]
        dv0 = rstd * (dxh - jnp.mean(dxh, axis=-1, keepdims=True) - xh * jnp.mean(dxh * xh, axis=-1, keepdims=True))
        dpre = jnp.concatenate(dge_u + [dv0], axis=1) * _gelu_grad(pre)
        dbi_ref[...] += _sum0(dpre)
        dpre_ref[...] = dpre.astype(dpre_ref.dtype)

    full = lambda shape: pl.BlockSpec(shape, lambda i: (0,) * len(shape))
    mats = (G, GMLP_CHUNK, GMLP_CHUNK)
    return pl.pallas_call(
        body, name=name, grid=(L // GMLP_CHUNK,),
        in_specs=[pl.BlockSpec((GMLP_CHUNK, W2), lambda i: (i, 0)), pl.BlockSpec((GMLP_CHUNK, W), lambda i: (i, 0)),
                  full((1, W2)), full((1, W)), full((1, W)), full(mats), full(mats), full((G, GMLP_CHUNK, 1))],
        out_specs=[pl.BlockSpec((GMLP_CHUNK, W2), lambda i: (i, 0)), full((1, W2)), full((1, W)), full((1, W)), full(mats),
                   full((G, GMLP_CHUNK, 1))],
        out_shape=[SDS((L, W2), MMT), SDS((1, W2), F32), SDS((1, W), F32), SDS((1, W), F32), SDS(mats, F32),
                   SDS((G, GMLP_CHUNK, 1), F32)],
        scratch_shapes=[pltpu.VMEM((GMLP_CHUNK, W), F32)], compiler_params=_cparams("arbitrary"))(
            p0, dus, b_in, ln_g, ln_b, w_s, w_st, b_s)


def _loss_head(h, target, tm, name):
    L, D = h.shape

    def body(h_ref, t_ref, l_ref, d_ref):
        @pl.when(pl.program_id(0) == 0)
        def _():
            l_ref[...] = jnp.zeros_like(l_ref)

        e = h_ref[...] - t_ref[...]
        l_ref[...] += 0.5 * jnp.sum(jnp.mean(e * e, axis=-1, keepdims=True), axis=0, keepdims=True)
        d_ref[...] = e * (1.0 / D)

    row = pl.BlockSpec((tm, D), lambda i: (i, 0))
    return pl.pallas_call(body, name=name, grid=(L // tm,), in_specs=[row, row],
                          out_specs=[pl.BlockSpec((1, 1), lambda i: (0, 0)), row],
                          out_shape=[SDS((1, 1), F32), SDS((L, D), F32)], compiler_params=_cparams("arbitrary"))(h, target)


def _ada_fwd(cond, ada_w, ada_b, name):
    NL, D, n = ada_w.shape
    tn = _tile(n, 768)

    def body(c_ref, w_ref, b_ref, o_ref):
        cv = c_ref[...]
        o_ref[...] = _dot(cv * _sig(cv), w_ref[...], 1, 0) + b_ref[...]

    return pl.pallas_call(
        body, name=name, grid=(NL, n // tn),
        in_specs=[pl.BlockSpec((2 * SUBLANES, D), lambda i, j: (0, 0)), pl.BlockSpec((None, D, tn), lambda i, j: (i, 0, j)),
                  pl.BlockSpec((None, 1, tn), lambda i, j: (i, 0, j))],
        out_specs=pl.BlockSpec((None, 2 * SUBLANES, tn), lambda i, j: (i, 0, j)), out_shape=SDS((NL, 2 * SUBLANES, n), F32),
        compiler_params=_cparams("parallel", "parallel"))(cond, ada_w, ada_b)


def _ada_bwd(cond, ada_w, dm_lat, dm_ctx, name):
    NL, D, n = ada_w.shape
    tn = _tile(n, 768)

    def body(c_ref, w_ref, dl_ref, dc_ref, dw_ref, ds_ref):
        @pl.when((pl.program_id(0) == 0) & (pl.program_id(1) == 0))
        def _():
            ds_ref[...] = jnp.zeros_like(ds_ref)

        cv = c_ref[...]
        row = lax.broadcasted_iota(jnp.int32, (SUBLANES, tn), 0)
        ctx_rows = jnp.where(row == 0, _sum0(dc_ref[...]), 0.0)
        dm = jnp.concatenate([dl_ref[...], ctx_rows], axis=0)
        dw_ref[...] = _dot(cv * _sig(cv), dm, 0, 0)
        ds_ref[...] += _dot(dm, w_ref[...], 1, 1)

    dspec = pl.BlockSpec((None, SUBLANES, tn), lambda i, j: (i, 0, j))
    return pl.pallas_call(
        body, name=name, grid=(NL, n // tn),
        in_specs=[pl.BlockSpec((2 * SUBLANES, D), lambda i, j: (0, 0)), pl.BlockSpec((None, D, tn), lambda i, j: (i, 0, j)),
                  dspec, dspec],
        out_specs=[pl.BlockSpec((None, D, tn), lambda i, j: (i, 0, j)), pl.BlockSpec((2 * SUBLANES, D), lambda i, j: (0, 0))],
        out_shape=[SDS((NL, D, n), F32), SDS((2 * SUBLANES, D), F32)],
        compiler_params=_cparams("arbitrary", "arbitrary"))(cond, ada_w, dm_lat, dm_ctx)


def _adam_math(w, g, m, v):
    m = ADAM_B1 * m + (1.0 - ADAM_B1) * g
    v = ADAM_B2 * v + (1.0 - ADAM_B2) * jnp.square(g)
    m_hat = m / (1.0 - ADAM_B1 ** ADAM_STEP)
    v_hat = v / (1.0 - ADAM_B2 ** ADAM_STEP)
    return -ADAM_LR * (m_hat / (jnp.sqrt(v_hat) + ADAM_EPS) + ADAM_WD * w), m, v


def _row_tile(rows, cols, elems):
    want = max(SUBLANES, elems // cols)
    best = SUBLANES if rows % SUBLANES == 0 else rows
    for d in range(SUBLANES, min(rows, want) + 1, SUBLANES):
        if rows % d == 0:
            best = d
    return best


def _adamw(w, m, v, parts, name):
    R, C = w.shape
    tr = _row_tile(R, C, 128 * 1024)
    npart = len(parts)

    def body(*refs):
        w_ref, m_ref, v_ref = refs[:3]
        g_ref, d_ref, nm_ref, nv_ref = refs[3 + npart:]
        g = refs[3][...]
        for p_ref in refs[4:3 + npart]:
            g = g + p_ref[...]
        d, nm, nv = _adam_math(w_ref[...], g, m_ref[...], v_ref[...])
        g_ref[...], d_ref[...], nm_ref[...], nv_ref[...] = g, d, nm, nv

    blk = pl.BlockSpec((tr, C), lambda i: (i, 0))
    return pl.pallas_call(body, name=name, grid=(R // tr,), in_specs=[blk] * (3 + npart), out_specs=[blk] * 4,
                          out_shape=[SDS((R, C), F32)] * 4, compiler_params=_cparams("parallel"))(w, m, v, *parts)


def _adamw_layer(w, m, v, layer, parts, prev, name):
    _, R, C = w.shape
    tr = _row_tile(R, C, 128 * 1024)
    npart = len(parts)
    nprev = 0 if prev is None else 4

    def body(*refs):
        w_ref, m_ref, v_ref = refs[:3]
        g_ref, d_ref, nm_ref, nv_ref = refs[3 + npart + nprev:]
        g = refs[3][...]
        for p_ref in refs[4:3 + npart]:
            g = g + p_ref[...]
        d, nm, nv = _adam_math(w_ref[...], g, m_ref[...], v_ref[...])
        g_ref[...], d_ref[...], nm_ref[...], nv_ref[...] = g, d, nm, nv

    stacked = pl.BlockSpec((None, tr, C), lambda i: (layer, i, 0))
    flat = pl.BlockSpec((tr, C), lambda i: (i, 0))
    return pl.pallas_call(
        body, name=name, grid=(R // tr,),
        in_specs=[stacked] * 3 + [flat] * npart + [pl.BlockSpec(memory_space=pl.ANY)] * nprev, out_specs=[stacked] * 4,
        out_shape=[SDS(w.shape, F32)] * 4, input_output_aliases={3 + npart + k: k for k in range(nprev)},
        compiler_params=_cparams("parallel"))(w, m, v, *parts, *(prev or ()))


def _sum_slots(x, name, out_dtype=F32):
    S, R, C = x.shape
    tr = _row_tile(R, C, 128 * 1024)

    def body(x_ref, o_ref):
        acc = x_ref[0].astype(F32)
        for s in range(1, S):
            acc = acc + x_ref[s].astype(F32)
        o_ref[...] = acc.astype(o_ref.dtype)

    return pl.pallas_call(body, name=name, grid=(R // tr,), in_specs=[pl.BlockSpec((S, tr, C), lambda i: (0, i, 0))],
                          out_specs=pl.BlockSpec((tr, C), lambda i: (i, 0)), out_shape=SDS((R, C), out_dtype),
                          compiler_params=_cparams("parallel"))(x)


def _my_place():
    return lax.axis_index("x"), lax.axis_index("y"), lax.axis_index("c")


def _other_chips(x, y):
    return [(1 - x, y), (x, 1 - y), (1 - x, 1 - y)]


def _all_gather(v, name):
    R, C = v.shape

    def body(v_ref, o_ref, send_sems, recv_sems, local_sem):
        x, y, c = _my_place()
        me = 4 * x + 2 * y + c
        mine = pltpu.make_async_copy(v_ref, o_ref.at[me], local_sem)
        mine.start()
        copies = []
        for flip in range(1, N_DEV):
            fx, fy, fc = (flip >> 2) & 1, (flip >> 1) & 1, flip & 1
            peer = ((x + fx) % 2, (y + fy) % 2, (c + fc) % 2)
            cp = pltpu.make_async_remote_copy(src_ref=v_ref, dst_ref=o_ref.at[me], send_sem=send_sems.at[flip - 1],
                                              recv_sem=recv_sems.at[flip - 1], device_id=peer, device_id_type=MESH)
            cp.start()
            copies.append(cp)
        for cp in copies:
            cp.wait()
        mine.wait()

    return pl.pallas_call(
        body, name=name, in_specs=[pl.BlockSpec(memory_space=pl.ANY)], out_specs=pl.BlockSpec(memory_space=pl.ANY),
        out_shape=SDS((N_DEV, R, C), v.dtype),
        scratch_shapes=[pltpu.SemaphoreType.DMA((N_DEV - 1,)), pltpu.SemaphoreType.DMA((N_DEV - 1,)), pltpu.SemaphoreType.DMA],
        )(v)


def _shard_window(ref, axis, j, size):
    idx = [slice(None)] * len(ref.shape)
    idx[axis] = pl.ds(pl.multiple_of(j * size, SUBLANES), size)
    return ref.at[tuple(idx)]


def _gather_plan(axis):
    def plan(srcs, lands):
        x, y, c = _my_place()
        dst = _shard_window(lands[0], axis, 2 * x + y, srcs[0].shape[axis])
        return [(srcs[0], dst)], [(srcs[0], dst, (px, py, c)) for px, py in _other_chips(x, y)]
    return plan


def _scatter_plan(axis):
    def plan(srcs, lands):
        x, y, c = _my_place()
        j = 2 * x + y
        size = srcs[0].shape[axis] // N_CHIPS
        local = [(_shard_window(srcs[0], axis, j, size), lands[0].at[j])]
        remote = [(_shard_window(srcs[0], axis, 2 * px + py, size), lands[0].at[j], (px, py, c)) for px, py in _other_chips(x, y)]
        return local, remote
    return plan


def _all_gather_plan(srcs, lands):
    x, y, c = _my_place()
    dst = lands[0].at[4 * x + 2 * y + c]
    remote = []
    for flip in range(1, N_DEV):
        fx, fy, fc = (flip >> 2) & 1, (flip >> 1) & 1, flip & 1
        remote.append((srcs[0], dst, ((x + fx) % 2, (y + fy) % 2, (c + fc) % 2)))
    return [(srcs[0], dst)], remote


HBM_SPEC = pl.BlockSpec(memory_space=pltpu.HBM)
SEM_SPEC = pl.BlockSpec(memory_space=pltpu.SEMAPHORE)


def _in_hbm(a):
    return pltpu.with_memory_space_constraint(a, pltpu.HBM)


def _split_start(name, exchanges, after):
    srcs = [s for e in exchanges for s in e[0]]
    lands = [lax.empty(s.shape, s.dtype) for e in exchanges for s in e[1]]
    ns, nl, ne = len(srcs), len(lands), len(exchanges)

    def body(*refs):
        src_refs, land_refs = refs[:ns], refs[ns:ns + nl]
        sem_refs = refs[ns + nl + 1:ns + nl + 1 + 2 * ne]
        token, local_sem = refs[-2], refs[-1]
        si = li = 0
        plans = []
        for e_srcs, e_lands, plan, n in exchanges:
            plans.append(plan(src_refs[si:si + len(e_srcs)], land_refs[li:li + len(e_lands)]))
            si, li = si + len(e_srcs), li + len(e_lands)
            assert len(plans[-1][1]) == n
        for local, _ in plans:
            for src, dst in local:
                cp = pltpu.make_async_copy(src, dst, local_sem)
                cp.start()
                cp.wait()
        for e, (_, remote) in enumerate(plans):
            for k, (src, dst, peer) in enumerate(remote):
                pltpu.make_async_remote_copy(src_ref=src, dst_ref=dst, send_sem=sem_refs[2 * e].at[k], recv_sem=sem_refs[2 * e + 1].at[k],
                                             device_id=peer, device_id_type=MESH).start()
        token[...] = jnp.zeros_like(token)

    sems = [pltpu.SemaphoreType.DMA((e[3],)) for e in exchanges for _ in range(2)]
    res = pl.pallas_call(
        body, name=name, in_specs=[HBM_SPEC] * (ns + nl) + [pl.BlockSpec(memory_space=pl.ANY)],
        out_shape=sems + [pltpu.HBM(a.shape, a.dtype) for a in srcs + lands] + [SDS((SUBLANES, LANES), F32)],
        out_specs=[SEM_SPEC] * (2 * ne) + [HBM_SPEC] * (ns + nl) + [pl.BlockSpec(memory_space=pltpu.VMEM)],
        input_output_aliases={i: 2 * ne + i for i in range(ns + nl)}, scratch_shapes=[pltpu.SemaphoreType.DMA],
        compiler_params=pltpu.CompilerParams(has_side_effects=pltpu.SideEffectType.DATAFLOW_SIDE_EFFECTING))(
            *[_in_hbm(a) for a in srcs + lands], after)
    sem_out, thru, token = res[:2 * ne], res[2 * ne:2 * ne + ns + nl], res[-1]
    handles, si, li = [], 0, 0
    for e, (e_srcs, e_lands, plan, n) in enumerate(exchanges):
        handles.append(dict(srcs=list(thru[si:si + len(e_srcs)]), lands=list(thru[ns + li:ns + li + len(e_lands)]),
                            sems=(sem_out[2 * e], sem_out[2 * e + 1]), plan=plan, n=n))
        si, li = si + len(e_srcs), li + len(e_lands)
    return handles, token


def _split_wait(name, handles, after):
    srcs = [s for h in handles for s in h["srcs"]]
    lands = [s for h in handles for s in h["lands"]]
    ns, nl, nh = len(srcs), len(lands), len(handles)

    def body(*refs):
        src_refs, land_refs = refs[:ns], refs[ns:ns + nl]
        sem_refs = refs[ns + nl:ns + nl + 2 * nh]
        si = li = 0
        for e, h in enumerate(handles):
            _, remote = h["plan"](src_refs[si:si + len(h["srcs"])], land_refs[li:li + len(h["lands"])])
            si, li = si + len(h["srcs"]), li + len(h["lands"])
            for k, (src, dst, peer) in enumerate(remote):
                cp = pltpu.make_async_remote_copy(src_ref=src, dst_ref=dst, send_sem=sem_refs[2 * e].at[k],
                                                  recv_sem=sem_refs[2 * e + 1].at[k], device_id=peer, device_id_type=MESH)
                cp.wait_send()
                cp.wait_recv()

    res = pl.pallas_call(
        body, name=name, in_specs=[HBM_SPEC] * (ns + nl) + [SEM_SPEC] * (2 * nh) + [pl.BlockSpec(memory_space=pl.ANY)],
        out_shape=[pltpu.HBM(a.shape, a.dtype) for a in srcs + lands], out_specs=[HBM_SPEC] * (ns + nl),
        input_output_aliases={i: i for i in range(ns + nl)},
        compiler_params=pltpu.CompilerParams(has_side_effects=pltpu.SideEffectType.DATAFLOW_SIDE_EFFECTING))(
            *srcs, *lands, *[s for h in handles for s in h["sems"]], after)
    out, li = [], 0
    for h in handles:
        out.append(list(res[ns + li:ns + li + len(h["lands"])]))
        li += len(h["lands"])
    return out


def _swap_with_sibling(parts, name):
    nt = len(parts)

    def body(*refs):
        ins, outs = refs[:nt], refs[nt:2 * nt]
        send_sems, recv_sems = refs[2 * nt:]
        x, y, c = _my_place()
        copies = []
        for t in range(nt):
            cp = pltpu.make_async_remote_copy(src_ref=ins[t], dst_ref=outs[t], send_sem=send_sems.at[t], recv_sem=recv_sems.at[t],
                                              device_id=(x, y, 1 - c), device_id_type=MESH)
            cp.start()
            copies.append(cp)
        for cp in copies:
            cp.wait()

    any_spec = pl.BlockSpec(memory_space=pl.ANY)
    return pl.pallas_call(
        body, name=name, in_specs=[any_spec] * nt, out_specs=[any_spec] * nt, out_shape=[SDS(p.shape, p.dtype) for p in parts],
        scratch_shapes=[pltpu.SemaphoreType.DMA((nt,)), pltpu.SemaphoreType.DMA((nt,))],
        )(*parts)


PACK_COLS = 1024


def _pack(arrays):
    flat = jnp.concatenate([a.reshape(-1) for a in arrays])
    pad = (-flat.shape[0]) % (SUBLANES * PACK_COLS)
    return jnp.pad(flat, (0, pad)).reshape(-1, PACK_COLS)


def _unpack(packed, shapes):
    flat, out, pos = packed.reshape(-1), [], 0
    for shape in shapes:
        n = math.prod(shape)
        out.append(flat[pos:pos + n].reshape(shape))
        pos += n
    return out


def _unshard_last(stacked):
    moved = jnp.moveaxis(stacked, 0, -2)
    return moved.reshape(moved.shape[:-2] + (moved.shape[-2] * moved.shape[-1],))


def _my_block_last(full, j):
    s = full.shape[-1] // N_CHIPS
    return lax.dynamic_index_in_dim(full.reshape(full.shape[:-1] + (N_CHIPS, s)), j, axis=full.ndim - 1, keepdims=False)


def _rope_tables(L):
    rows = L // GRID_W
    row = jnp.repeat(jnp.arange(rows), GRID_W).astype(F32)
    col = jnp.tile(jnp.arange(GRID_W), rows).astype(F32)
    axis_dim = HEAD_DIM // 2
    inv_freq = ROPE_BASE ** (-jnp.arange(0, axis_dim, 2, dtype=F32) / axis_dim)
    ang_r, ang_c = row[:, None] * inv_freq[None, :], col[:, None] * inv_freq[None, :]
    ang = jnp.concatenate([ang_r, ang_r, ang_c, ang_c] * 2, axis=-1)
    return jnp.cos(ang), jnp.sin(ang)


SMALL_SHARDED = ("norm_g", "ffn_conv_w", "cm_b_in", "cm_dw_w", "cm_dw_b", "cm_ln_g", "cm_ln_b", "cm_b_out", "gm_b_in", "gm_ln_g",
                 "gm_ln_b")
SMALL_REPLICATED = ("c_ctx", "ada_b", "ffn_conv_b", "attn_sink", "gm_w_s", "gm_b_s")
BIG = ("ffn_w_up", "ffn_w_down", "cm_w_in", "cm_w_out", "attn_w_qkv", "attn_w_o", "gm_w_in", "gm_w_out")
BIG_AXIS = {"ffn_w_up": 2, "ffn_w_down": 1, "cm_w_in": 2, "cm_w_out": 1, "attn_w_qkv": 2, "attn_w_o": 1, "gm_w_in": 2, "gm_w_out": 1}
WEIGHTS = ("c_ctx", "ada_w", "ada_b", "norm_g", "ffn_w_up", "ffn_conv_w", "ffn_conv_b", "ffn_w_down", "cm_w_in", "cm_b_in",
           "cm_dw_w", "cm_dw_b", "cm_ln_g", "cm_ln_b", "cm_w_out", "cm_b_out", "attn_w_qkv", "attn_sink", "attn_w_o", "gm_w_in",
           "gm_b_in", "gm_ln_g", "gm_ln_b", "gm_w_s", "gm_b_s", "gm_w_out")


def _step(x, c, ctx, target, W, M, V):
    L, D = x.shape[1], x.shape[2]
    C = ctx.shape[1]
    T = L + C
    NL = W["ada_w"].shape[0]
    tm = 256 if C % 256 == 0 else 128
    nl = L // tm
    xi, yi, ci = _my_place()
    chip = 2 * xi + yi
    dev = 4 * xi + 2 * yi + ci
    segs2, segs1 = [(0, L), (L, C)], [(0, L)]
    vec = lambda a: a.reshape(1, -1)

    use_order = [("cm_w_in", 0), ("cm_w_out", 0), ("ffn_w_up", 0), ("ffn_w_down", 0), ("attn_w_qkv", 0), ("attn_w_o", 0),
                 ("ffn_w_up", 1), ("ffn_w_down", 1), ("gm_w_in", 0), ("gm_w_out", 0), ("ffn_w_up", 2), ("ffn_w_down", 2),
                 ("cm_w_in", 1), ("cm_w_out", 1), ("ffn_w_up", 3), ("ffn_w_down", 3)]
    exchanges = []
    for n, i in use_order:
        shard = W[n][i].astype(MMT)
        whole = list(shard.shape)
        whole[BIG_AXIS[n] - 1] *= N_CHIPS
        exchanges.append(([shard], [SDS(tuple(whole), MMT)], _gather_plan(BIG_AXIS[n] - 1), N_CHIPS - 1))
    in_flight, arrived = {}, {}

    def big(n, i, after=None):
        if (n, i) not in arrived:
            arrived[(n, i)] = _split_wait(f"wait_{n}_{i}", [in_flight[(n, i)]], after)[0][0]
        return arrived[(n, i)]

    small_shapes = [W[n].shape for n in SMALL_SHARDED]
    ag1 = _all_gather(_pack([c.reshape(-1)] + [W[n] for n in SMALL_SHARDED]), "gather_small")
    parts = [_unpack(ag1[2 * s], [(D,)] + small_shapes) for s in range(N_CHIPS)]
    c_rows = jnp.stack([_unpack(ag1[d], [(D,)])[0] for d in range(N_DEV)])
    P = {n: _unshard_last(jnp.stack([parts[s][1 + i] for s in range(N_CHIPS)])) for i, n in enumerate(SMALL_SHARDED)}
    for n in SMALL_REPLICATED:
        P[n] = W[n]

    cond = jnp.concatenate([c_rows, W["c_ctx"][None, :], jnp.zeros((2 * SUBLANES - N_DEV - 1, D), F32)], axis=0)
    ncol = W["ada_w"].shape[2]
    ada_b_mine = lax.dynamic_slice_in_dim(W["ada_b"], chip * ncol, ncol, axis=1)[:, None, :]
    mods_mine = _ada_fwd(cond, W["ada_w"], ada_b_mine, "ada_fwd")
    ag2 = _all_gather(mods_mine.reshape(NL * 2 * SUBLANES, ncol), "gather_mods").reshape(N_DEV, NL, 2 * SUBLANES, ncol)
    mods_all = _unshard_last(jnp.stack([ag2[2 * s] for s in range(N_CHIPS)]))
    mod_lat = lax.dynamic_index_in_dim(mods_all, dev, axis=1, keepdims=False).reshape(NL, 6, D)
    mod_ctx = mods_all[:, N_DEV].reshape(NL, 6, D)
    mod2 = jnp.stack([mod_lat, mod_ctx], axis=1)
    mod1 = mod_lat[:, None]

    handles, started = _split_start("gather_weights_start", exchanges, mod2)
    in_flight.update(zip(use_order, handles))
    zero_d = jnp.zeros((1, D), F32)
    cos, sin = _rope_tables(L)
    nkv = D // HEAD_DIM // Q_PER_KV
    qdim, kvdim = D, nkv * HEAD_DIM

    def ffn_fwd(i, h, mod, rows, segs, tag):
        a2 = _prenorm(h, mod, vec(P["norm_g"][i, 2]), 1, rows, nl, tm, f"pre_ffn_{tag}")
        z0 = _mm(a2, big("ffn_w_up", i, a2), "nn", F32, f"ffn_up_{tag}")
        u = _ffn_gate(z0, P["ffn_conv_w"][i], vec(P["ffn_conv_b"][i]), segs, f"ffn_gate_{tag}")
        f = _mm(u, big("ffn_w_down", i, u), "nn", F32, f"ffn_down_{tag}")
        h_out = _postnorm(h, f, zero_d, mod, vec(P["norm_g"][i, 3]), 5, rows, nl, tm, f"post_ffn_{tag}")
        return h_out, dict(h=h, a2=a2, z0=z0, f=f)

    def ffn_bwd(i, dh, sv, mod, rows, segs, tag, G):
        df, dg2, dgn3, _ = _postnorm_bwd(dh, sv["f"], zero_d, mod, vec(P["norm_g"][i, 3]), 5, rows, nl, tm, f"post_ffn_bwd_{tag}")
        du = _mm(df, big("ffn_w_down", i), "nt", F32, f"ffn_down_dx_{tag}")
        u, dz0, dcw, dcb = _ffn_gate_bwd(sv["z0"], du, P["ffn_conv_w"][i], vec(P["ffn_conv_b"][i]), segs, f"ffn_gate_bwd_{tag}")
        G["ffn_w_down"][i] = _mm(u, df, "tn", MMT, f"ffn_down_dw_{tag}")
        G["ffn_w_up"][i] = _mm(sv["a2"], dz0, "tn", MMT, f"ffn_up_dw_{tag}")
        da2 = _mm(dz0, big("ffn_w_up", i), "nt", F32, f"ffn_up_dx_{tag}")
        dh, dsh2, dsc2, dgn2 = _prenorm_bwd(sv["h"], da2, dh, mod, vec(P["norm_g"][i, 2]), 1, rows, nl, tm, f"pre_ffn_bwd_{tag}")
        G["ffn_conv_w"][i], G["ffn_conv_b"][i] = dcw, dcb[0]
        return dh, (dsh2, dsc2, dg2), (dgn2, dgn3)

    def conformer_fwd(i, j, h, mod, rows, segs, tag):
        a = _prenorm(h, mod, vec(P["norm_g"][i, 0]), 0, rows, nl, tm, f"pre_mix_{tag}")
        p0 = _mm(a, big("cm_w_in", j, a), "nn", F32, f"cm_in_{tag}")
        z2 = _glu_conv(p0, vec(P["cm_b_in"][j]), P["cm_dw_w"][j], vec(P["cm_dw_b"][j]), segs, f"cm_conv_{tag}")
        z4 = _ln_silu(z2, vec(P["cm_ln_g"][j]), vec(P["cm_ln_b"][j]), rows, tm, f"cm_ln_{tag}")
        y = _mm(z4, big("cm_w_out", j, z4), "nn", F32, f"cm_out_{tag}")
        h_out = _postnorm(h, y, vec(P["cm_b_out"][j]), mod, vec(P["norm_g"][i, 1]), 2, rows, nl, tm, f"post_mix_{tag}")
        return h_out, dict(h=h, a=a, p0=p0, z2=z2, z4=z4, y=y)

    def conformer_bwd(i, j, dh, sv, mod, rows, segs, tag, G):
        dy, dg1, dgn1, dbo = _postnorm_bwd(dh, sv["y"], vec(P["cm_b_out"][j]) + zero_d, mod, vec(P["norm_g"][i, 1]), 2, rows, nl,
                                           tm, f"post_mix_bwd_{tag}")
        G["cm_w_out"][j] = _mm(sv["z4"], dy, "tn", MMT, f"cm_out_dw_{tag}")
        dz4 = _mm(dy, big("cm_w_out", j), "nt", F32, f"cm_out_dx_{tag}")
        dz2, dlg, dlb = _ln_silu_bwd(sv["z2"], dz4, vec(P["cm_ln_g"][j]), vec(P["cm_ln_b"][j]), rows, tm, f"cm_ln_bwd_{tag}")
        dpa, dpg, ddw, ddb, dba, dbg = _glu_conv_bwd(sv["p0"], vec(P["cm_b_in"][j]), P["cm_dw_w"][j], dz2, segs, f"cm_conv_bwd_{tag}")
        dp = jnp.concatenate([dpa, dpg], axis=1)
        G["cm_w_in"][j] = _mm(sv["a"], dp, "tn", MMT, f"cm_in_dw_{tag}")
        da = _mm(dp, big("cm_w_in", j), "nt", F32, f"cm_in_dx_{tag}")
        dh, dsh1, dsc1, dgn0 = _prenorm_bwd(sv["h"], da, dh, mod, vec(P["norm_g"][i, 0]), 0, rows, nl, tm, f"pre_mix_bwd_{tag}")
        G["cm_b_out"][j] = jnp.sum(dbo, axis=0)[0]
        G["cm_ln_g"][j], G["cm_ln_b"][j], G["cm_dw_w"][j], G["cm_dw_b"][j] = dlg[0], dlb[0], ddw, ddb[0]
        G["cm_b_in"][j] = jnp.concatenate([dba[0], dbg[0]])
        return dh, (dsh1, dsc1, dg1), (dgn0, dgn1)

    def heads(a, n):
        return a.reshape(a.shape[0], n, HEAD_DIM).transpose(1, 0, 2)

    def unheads(a):
        return a.transpose(1, 0, 2).reshape(a.shape[1], -1)

    G = {n: [None] * W[n].shape[0] for n in WEIGHTS if n not in ("c_ctx", "ada_w", "ada_b", "norm_g")}
    saved = []
    h = jnp.concatenate([x[0], ctx[0]], axis=0) + started[0:1, 0:1]
    h, s_mix = conformer_fwd(0, 0, h, mod2[0], T, segs2, "l0")
    h, s_ffn = ffn_fwd(0, h, mod2[0], T, segs2, "l0")
    saved.append((s_mix, s_ffn))
    a_all = _prenorm(h, mod2[1], vec(P["norm_g"][1, 0]), 0, T, nl, tm, "pre_mix_l1")
    qkv = _mm(a_all, big("attn_w_qkv", 0, a_all), "nn", F32, "attn_qkv")
    qk_rot, v_lat = _rope(qkv, cos, sin, L, qdim + kvdim, tm, "rope")
    q_h = heads(qk_rot[:, :qdim], nkv * Q_PER_KV).reshape(nkv, Q_PER_KV, L, HEAD_DIM)
    k_h, v_h = heads(qk_rot[:, qdim:], nkv), heads(v_lat, nkv)
    kc_h = heads(qkv[L:, qdim:qdim + kvdim].astype(MMT), nkv)
    vc_h = heads(qkv[L:, qdim + kvdim:].astype(MMT), nkv)
    sink = P["attn_sink"][0]
    o_h, lse = _attn_fwd(q_h, k_h, v_h, kc_h, vc_h, sink, "attn")
    o_nat = unheads(o_h.reshape(nkv * Q_PER_KV, L, HEAD_DIM)).astype(MMT)
    y1 = _mm(o_nat, big("attn_w_o", 0, o_nat), "nn", F32, "attn_out")
    h_in1 = h
    h = _postnorm(h, y1, zero_d, mod1[1], vec(P["norm_g"][1, 1]), 2, L, nl, tm, "post_mix_l1")
    h, s_ffn1 = ffn_fwd(1, h, mod1[1], L, segs1, "lat")
    h_in2 = h
    a_2 = _prenorm(h, mod1[2], vec(P["norm_g"][2, 0]), 0, L, nl, tm, "pre_mix_l2")
    p0_2 = _mm(a_2, big("gm_w_in", 0, a_2), "nn", F32, "gm_in")
    ws_bf = P["gm_w_s"][0].astype(MMT)
    bs_col = P["gm_b_s"][0][:, :, None]
    us = _gmlp_fwd(p0_2, vec(P["gm_b_in"][0]), vec(P["gm_ln_g"][0]), vec(P["gm_ln_b"][0]), ws_bf, bs_col, "gmlp")
    y2 = _mm(us, big("gm_w_out", 0, us), "nn", F32, "gm_out")
    h = _postnorm(h, y2, zero_d, mod1[2], vec(P["norm_g"][2, 1]), 2, L, nl, tm, "post_mix_l2")
    h, s_ffn2 = ffn_fwd(2, h, mod1[2], L, segs1, "lat")
    h, s_mix3 = conformer_fwd(3, 1, h, mod1[3], L, segs1, "l3")
    h, s_ffn3 = ffn_fwd(3, h, mod1[3], L, segs1, "lat")

    loss_mine, dh = _loss_head(h, target[0], tm, "loss_head")

    dmod = [None] * NL
    dgn = [None] * NL

    def finish(i, mix, ffn, gns_mix, gns_ffn):
        dmod[i] = jnp.concatenate(list(mix) + list(ffn), axis=1)
        dgn[i] = jnp.stack([jnp.sum(g, axis=0)[0] for g in (gns_mix[0], gns_mix[1], gns_ffn[0], gns_ffn[1])])

    sent, so_far = {}, {}

    def send(tag, tensors, after):
        exchanges = []
        for n, l in tensors:
            g = G[n][l]
            shard = list(g.shape)
            shard[BIG_AXIS[n] - 1] //= N_CHIPS
            exchanges.append(([g], [SDS((N_CHIPS,) + tuple(shard), g.dtype)], _scatter_plan(BIG_AXIS[n] - 1), N_CHIPS - 1))
        handles, token = _split_start(f"scatter_start_{tag}", exchanges, after)
        sent[tag] = (tensors, handles)
        return token[0:1, 0:1]

    def land(tag, after):
        tensors, handles = sent[tag]
        landed = _split_wait(f"scatter_wait_{tag}", handles, after)
        mine = [_sum_slots(lands[0], f"sum_chips_{n}_{l}") for (n, l), lands in zip(tensors, landed)]
        theirs = _swap_with_sibling(mine, f"swap_cores_{tag}")
        for (n, l), a, b in zip(tensors, mine, theirs):
            so_far[n] = _adamw_layer(W[n], M[n], V[n], l, [a, b], so_far.get(n), f"adamw_{n}_{l}")
        return so_far[tensors[0][0]][0]

    dh, m_ffn, n_ffn = ffn_bwd(3, dh, s_ffn3, mod1[3], L, segs1, "lat", G)
    dh, m_mix, n_mix = conformer_bwd(3, 1, dh, s_mix3, mod1[3], L, segs1, "l3", G)
    finish(3, m_mix, m_ffn, n_mix, n_ffn)
    zero_d = zero_d + send("l3", [("ffn_w_up", 3), ("ffn_w_down", 3), ("cm_w_in", 1), ("cm_w_out", 1)], dh)

    dh, m_ffn, n_ffn = ffn_bwd(2, dh, s_ffn2, mod1[2], L, segs1, "lat", G)
    dy2, dg1, dgn1, _ = _postnorm_bwd(dh, y2, zero_d, mod1[2], vec(P["norm_g"][2, 1]), 2, L, nl, tm, "post_mix_bwd_l2")
    G["gm_w_out"][0] = _mm(us, dy2, "tn", MMT, "gm_out_dw")
    dus = _mm(dy2, big("gm_w_out", 0), "nt", F32, "gm_out_dx")
    ws_t = jnp.swapaxes(P["gm_w_s"][0], 1, 2).astype(MMT)
    dpre, dbi, dlg, dlb, dws, dbs = _gmlp_bwd(p0_2, dus, vec(P["gm_b_in"][0]), vec(P["gm_ln_g"][0]), vec(P["gm_ln_b"][0]), ws_bf,
                                              ws_t, bs_col, "gmlp_bwd")
    G["gm_w_in"][0] = _mm(a_2, dpre, "tn", MMT, "gm_in_dw")
    da = _mm(dpre, big("gm_w_in", 0), "nt", F32, "gm_in_dx")
    dh, dsh1, dsc1, dgn0 = _prenorm_bwd(h_in2, da, dh, mod1[2], vec(P["norm_g"][2, 0]), 0, L, nl, tm, "pre_mix_bwd_l2")
    G["gm_b_in"][0], G["gm_ln_g"][0], G["gm_ln_b"][0], G["gm_w_s"][0], G["gm_b_s"][0] = dbi[0], dlg[0], dlb[0], dws, dbs[:, :, 0]
    finish(2, (dsh1, dsc1, dg1), m_ffn, (dgn0, dgn1), n_ffn)
    zero_d = zero_d + send("l2", [("ffn_w_up", 2), ("ffn_w_down", 2), ("gm_w_in", 0), ("gm_w_out", 0)], land("l3", dh))

    dh, m_ffn, n_ffn = ffn_bwd(1, dh, s_ffn1, mod1[1], L, segs1, "lat", G)
    dy1, dg1, dgn1, _ = _postnorm_bwd(dh, y1, zero_d, mod1[1], vec(P["norm_g"][1, 1]), 2, L, nl, tm, "post_mix_bwd_l1")
    G["attn_w_o"][0] = _mm(o_nat, dy1, "tn", MMT, "attn_out_dw")
    do_nat = _mm(dy1, big("attn_w_o", 0), "nt", MMT, "attn_out_dx")
    do_h = heads(do_nat, nkv * Q_PER_KV).reshape(nkv, Q_PER_KV, L, HEAD_DIM)
    dq_h, dkc_h, dvc_h, dsk = _attn_bwd_q(q_h, k_h, v_h, kc_h, vc_h, sink, o_h, do_h, lse, "attn_bwd_q")
    dk_h, dv_h = _attn_bwd_kv(q_h, k_h, v_h, o_h, do_h, lse, "attn_bwd_kv")
    dqk = jnp.concatenate([unheads(dq_h.reshape(nkv * Q_PER_KV, L, HEAD_DIM)), unheads(dk_h)], axis=1)
    dqkv_lat = _rope_bwd(dqk, unheads(dv_h), cos, sin, tm, "rope_bwd")
    dqkv_ctx = jnp.concatenate([jnp.zeros((C, qdim), MMT), unheads(dkc_h).astype(MMT), unheads(dvc_h).astype(MMT)], axis=1)
    dqkv = jnp.concatenate([dqkv_lat, dqkv_ctx], axis=0)
    G["attn_w_qkv"][0] = _mm(a_all, dqkv, "tn", MMT, "attn_qkv_dw")
    da_all = _mm(dqkv, big("attn_w_qkv", 0), "nt", F32, "attn_qkv_dx")
    dh_all = jnp.concatenate([dh, jnp.zeros((C, D), F32)], axis=0)
    dh, dsh1, dsc1, dgn0 = _prenorm_bwd(h_in1, da_all, dh_all, mod2[1], vec(P["norm_g"][1, 0]), 0, T, nl, tm, "pre_mix_bwd_l1")
    G["attn_sink"][0] = dsk[:, :Q_PER_KV, 0].reshape(-1)
    pad_ctx = lambda a: jnp.concatenate([a, jnp.zeros_like(a)], axis=0)
    finish(1, (dsh1, dsc1, pad_ctx(dg1)), [pad_ctx(a) for a in m_ffn], (dgn0, dgn1), n_ffn)
    zero_d = zero_d + send("l1", [("ffn_w_up", 1), ("ffn_w_down", 1), ("attn_w_qkv", 0), ("attn_w_o", 0)], land("l2", dh))

    s_mix0, s_ffn0 = saved[0]
    dh, m_ffn, n_ffn = ffn_bwd(0, dh, s_ffn0, mod2[0], T, segs2, "l0", G)
    zero_d = zero_d + send("l0_ffn", [("ffn_w_up", 0), ("ffn_w_down", 0)], land("l1", dh))
    dh, m_mix, n_mix = conformer_bwd(0, 0, dh, s_mix0, mod2[0], T, segs2, "l0", G)
    finish(0, m_mix, m_ffn, n_mix, n_ffn)
    grad_x = dh[:L][None]
    started_l0 = send("l0_mix", [("cm_w_in", 0), ("cm_w_out", 0)], dh)

    for i in range(2, NL):
        dmod[i] = pad_ctx(dmod[i])
    dmod_all = jnp.stack(dmod).reshape(NL, 2, 6 * D) + started_l0

    ag3 = _all_gather(dmod_all.reshape(NL * 2, 6 * D), "gather_dmods").reshape(N_DEV, NL, 2, N_CHIPS, ncol)
    dm_cols = lax.dynamic_index_in_dim(ag3, chip, axis=3, keepdims=False)
    dm_lat, dm_ctx = jnp.moveaxis(dm_cols[:, :, 0], 0, 1), jnp.moveaxis(dm_cols[:, :, 1], 0, 1)
    g_ada_w, dsilu = _ada_bwd(cond, W["ada_w"], dm_lat, dm_ctx, "ada_bwd")
    cc = W["c_ctx"]
    sg = jax.nn.sigmoid(cc)
    dcctx_part = jnp.where(ci == 0, 1.0, 0.0) * dsilu[N_DEV] * (sg * (1.0 + cc * (1.0 - sg)))

    Gs = {n: jnp.stack(G[n]) for n in G if n not in BIG}
    Gs["norm_g"] = jnp.stack(dgn)
    Gs["ada_b"] = jnp.sum(dmod_all, axis=1)
    Gs["c_ctx"] = dcctx_part
    small_names = list(SMALL_SHARDED) + list(SMALL_REPLICATED)
    small_full_shapes = [P[n].shape for n in small_names]
    small_pack = _pack([Gs[n] for n in small_names])
    (ag4_handle,), _ = _split_start("gather_small_grads_start", [
        ([small_pack], [SDS((N_DEV,) + small_pack.shape, F32)], _all_gather_plan, N_DEV - 1)], dsilu)

    flat2 = lambda a: a.reshape(-1, a.shape[-1])
    res = {}
    outs = _adamw(flat2(W["ada_w"]), flat2(M["ada_w"]), flat2(V["ada_w"]), [flat2(g_ada_w)], "adamw_ada_w")
    res["ada_w"] = tuple(o.reshape(W["ada_w"].shape) for o in outs)

    after = land("l0_mix", land("l0_ffn", outs[0]))
    for n in BIG:
        res[n] = tuple(so_far[n])

    ag4 = _split_wait("gather_small_grads_wait", [ag4_handle], after)[0][0]
    small_sum = _unpack(_sum_slots(ag4, "sum_small_grads"), small_full_shapes)
    g_small = {}
    for n, g in zip(small_names, small_sum):
        g_small[n] = _my_block_last(g, chip) if n in SMALL_SHARDED else g
    packed = [_pack([d[n] for n in small_names]) for d in (W, M, V)]
    outs_small = _adamw(packed[0], packed[1], packed[2], [_pack([g_small[n] for n in small_names])], "adamw_small")
    shard_shapes = [W[n].shape for n in small_names]
    for k, n in enumerate(small_names):
        res[n] = tuple(_unpack(o, shard_shapes)[k] for o in outs_small)

    loss = lax.psum(loss_mine[0, 0], ("x", "y", "c"))
    return (loss, grad_x) + tuple(res[n][k] for k in range(4) for n in WEIGHTS)


def kernel(x, c, ctx, c_ctx, ada_w, ada_b, norm_g, ffn_w_up, ffn_conv_w, ffn_conv_b, ffn_w_down, cm_w_in, cm_b_in, cm_dw_w, cm_dw_b, cm_ln_g, cm_ln_b, cm_w_out, cm_b_out, attn_w_qkv, attn_sink, attn_w_o, gm_w_in, gm_b_in, gm_ln_g, gm_ln_b, gm_w_s, gm_b_s, gm_w_out, loss_target, m_c_ctx, m_ada_w, m_ada_b, m_norm_g, m_ffn_w_up, m_ffn_conv_w, m_ffn_conv_b, m_ffn_w_down, m_cm_w_in, m_cm_b_in, m_cm_dw_w, m_cm_dw_b, m_cm_ln_g, m_cm_ln_b, m_cm_w_out, m_cm_b_out, m_attn_w_qkv, m_attn_sink, m_attn_w_o, m_gm_w_in, m_gm_b_in, m_gm_ln_g, m_gm_ln_b, m_gm_w_s, m_gm_b_s, m_gm_w_out, v_c_ctx, v_ada_w, v_ada_b, v_norm_g, v_ffn_w_up, v_ffn_conv_w, v_ffn_conv_b, v_ffn_w_down, v_cm_w_in, v_cm_b_in, v_cm_dw_w, v_cm_dw_b, v_cm_ln_g, v_cm_ln_b, v_cm_w_out, v_cm_b_out, v_attn_w_qkv, v_attn_sink, v_attn_w_o, v_gm_w_in, v_gm_b_in, v_gm_ln_g, v_gm_ln_b, v_gm_w_s, v_gm_b_s, v_gm_w_out):
    args = locals()
    W = {n: args[n] for n in WEIGHTS}
    M = {n: args["m_" + n] for n in WEIGHTS}
    V = {n: args["v_" + n] for n in WEIGHTS}
    return _step(x, c, ctx, loss_target, W, M, V)
```

```python
import functools
import math

import jax
import jax.numpy as jnp
from jax import lax
from jax.experimental import pallas as pl
from jax.experimental.pallas import tpu as pltpu
from jax.experimental.pallas import tpu_sc as plsc

F32 = jnp.float32
MMT = jnp.bfloat16
SDS = jax.ShapeDtypeStruct
MESH = pl.DeviceIdType.MESH

EPS = 1e-6
HEAD_DIM = 64
Q_PER_KV = 4
ATTN_BLOCK = 128
GRID_W = 64
ROPE_BASE = 10000.0
GMLP_CHUNK = 128
GMLP_GROUP_DIM = 128
CONV_WIDTH = 31
FFN_CONV_WIDTH = 3
NEG = -1e30

ADAM_LR, ADAM_B1, ADAM_B2, ADAM_EPS, ADAM_WD, ADAM_STEP = 0.001, 0.9, 0.999, 1e-08, 0.01, 10

LANES = 128
SUBLANES = 8
VMEM_LIMIT = 52 * 1024 * 1024
CONV_ROWS = 128
N_CHIPS = 4
N_DEV = 8
FETCH_IDS = (1, 2, 3, 4)
SEND_IDS = (5, 6, 7, 8, 9)
SMALL_GRADS_ID = 10


def _cparams(*sem):
    return pltpu.CompilerParams(dimension_semantics=sem if sem else None, vmem_limit_bytes=VMEM_LIMIT)


def _tile(n, cap, mult=LANES):
    best = None
    for d in range(mult, min(n, cap) + 1, mult):
        if n % d == 0:
            best = d
    return best if best is not None else n


def _sum0(v):
    return jnp.sum(v, axis=0, keepdims=True)


def _rms(v):
    r = lax.rsqrt(jnp.mean(v * v, axis=-1, keepdims=True) + EPS)
    return v * r, r


def _sig(v):
    return jax.nn.sigmoid(v)


def _dot(a, b, ca, cb):
    return lax.dot_general(a.astype(MMT), b.astype(MMT), (((ca,), (cb,)), ((), ())), preferred_element_type=F32)


def _mm(a, b, mode, out_dtype, name):
    if mode == "nn":
        (M, K), N = a.shape, b.shape[1]
    elif mode == "nt":
        (M, K), N = a.shape, b.shape[0]
    else:
        (K, M), N = a.shape, b.shape[1]
    tm, tn, tk = _tile(M, 512), _tile(N, 1408), _tile(K, 1536)
    nk = K // tk
    ca, cb = {"nn": (1, 0), "nt": (1, 1), "tn": (0, 0)}[mode]

    def body(a_ref, b_ref, o_ref, acc):
        k = pl.program_id(2)

        @pl.when(k == 0)
        def _():
            acc[...] = jnp.zeros_like(acc)

        acc[...] += _dot(a_ref[...], b_ref[...], ca, cb)

        @pl.when(k == nk - 1)
        def _():
            o_ref[...] = acc[...].astype(o_ref.dtype)

    a_spec = pl.BlockSpec((tk, tm), lambda i, j, k: (k, i)) if mode == "tn" else pl.BlockSpec((tm, tk), lambda i, j, k: (i, k))
    b_spec = pl.BlockSpec((tn, tk), lambda i, j, k: (j, k)) if mode == "nt" else pl.BlockSpec((tk, tn), lambda i, j, k: (k, j))
    return pl.pallas_call(
        body, name=name, grid=(M // tm, N // tn, nk), in_specs=[a_spec, b_spec],
        out_specs=pl.BlockSpec((tm, tn), lambda i, j, k: (i, j)), out_shape=SDS((M, N), out_dtype),
        scratch_shapes=[pltpu.VMEM((tm, tn), F32)], compiler_params=_cparams("parallel", "parallel", "arbitrary"))(a, b)


def _seg_of(nl, nseg):
    return (lambda i: jnp.where(i >= nl, 1, 0)) if nseg == 2 else (lambda i: 0)


def _prenorm(h, mod, gn, which, rows, nl, tm, name):
    D = h.shape[1]
    nseg = mod.shape[0]
    seg = _seg_of(nl, nseg)
    sh_i, sc_i = (0, 1) if which == 0 else (3, 4)

    def body(h_ref, mod_ref, gn_ref, a_ref):
        n, _ = _rms(h_ref[...])
        a_ref[...] = (n * gn_ref[...] * (1.0 + mod_ref[pl.ds(sc_i, 1), :]) + mod_ref[pl.ds(sh_i, 1), :]).astype(a_ref.dtype)

    return pl.pallas_call(
        body, name=name, grid=(rows // tm,),
        in_specs=[pl.BlockSpec((tm, D), lambda i: (i, 0)), pl.BlockSpec((None, 6, D), lambda i: (seg(i), 0, 0)),
                  pl.BlockSpec((1, D), lambda i: (0, 0))],
        out_specs=pl.BlockSpec((tm, D), lambda i: (i, 0)), out_shape=SDS((rows, D), MMT),
        compiler_params=_cparams("parallel"))(h, mod, gn)


def _acc_spec(D, seg):
    return pl.BlockSpec((None, 1, D), lambda i: (seg(i), 0, 0))


def _prenorm_bwd(h, da, dh_in, mod, gn, which, rows, nl, tm, name):
    D = h.shape[1]
    nseg = mod.shape[0]
    seg = _seg_of(nl, nseg)
    sc_i = 1 if which == 0 else 4

    def body(h_ref, da_ref, dhin_ref, mod_ref, gn_ref, dh_ref, dsh_ref, dsc_ref, dgn_ref):
        i = pl.program_id(0)
        first = (i == 0) | (i == nl) if nseg == 2 else (i == 0)

        @pl.when(first)
        def _():
            dsh_ref[...] = jnp.zeros_like(dsh_ref)
            dsc_ref[...] = jnp.zeros_like(dsc_ref)
            dgn_ref[...] = jnp.zeros_like(dgn_ref)

        n, r = _rms(h_ref[...])
        da_v = da_ref[...].astype(F32)
        gn_v = gn_ref[...]
        sc1 = 1.0 + mod_ref[pl.ds(sc_i, 1), :]
        dsh_ref[...] += _sum0(da_v)
        dsc_ref[...] += _sum0(da_v * (n * gn_v))
        dgn_ref[...] += _sum0(da_v * n * sc1)
        dn = da_v * (gn_v * sc1)
        dh_ref[...] = dhin_ref[...] + r * (dn - n * jnp.mean(dn * n, axis=-1, keepdims=True))

    row = pl.BlockSpec((tm, D), lambda i: (i, 0))
    acc = SDS((nseg, 1, D), F32)
    return pl.pallas_call(
        body, name=name, grid=(rows // tm,),
        in_specs=[row, row, row, pl.BlockSpec((None, 6, D), lambda i: (seg(i), 0, 0)), pl.BlockSpec((1, D), lambda i: (0, 0))],
        out_specs=[row, _acc_spec(D, seg), _acc_spec(D, seg), _acc_spec(D, seg)],
        out_shape=[SDS((rows, D), F32), acc, acc, acc], compiler_params=_cparams("arbitrary"))(h, da, dh_in, mod, gn)


def _postnorm(h, y, bias, mod, gn, gate_i, rows, nl, tm, name):
    D = h.shape[1]
    nseg = mod.shape[0]
    seg = _seg_of(nl, nseg)

    def body(h_ref, y_ref, b_ref, mod_ref, gn_ref, o_ref):
        ny, _ = _rms(y_ref[...] + b_ref[...])
        o_ref[...] = h_ref[...] + mod_ref[pl.ds(gate_i, 1), :] * (ny * gn_ref[...])

    row = pl.BlockSpec((tm, D), lambda i: (i, 0))
    vec = pl.BlockSpec((1, D), lambda i: (0, 0))
    return pl.pallas_call(
        body, name=name, grid=(rows // tm,),
        in_specs=[row, row, vec, pl.BlockSpec((None, 6, D), lambda i: (seg(i), 0, 0)), vec],
        out_specs=row, out_shape=SDS((rows, D), F32), compiler_params=_cparams("parallel"))(h, y, bias, mod, gn)


def _postnorm_bwd(dh, y, bias, mod, gn, gate_i, rows, nl, tm, name):
    D = y.shape[1]
    nseg = mod.shape[0]
    seg = _seg_of(nl, nseg)

    def body(dh_ref, y_ref, b_ref, mod_ref, gn_ref, dy_ref, dg_ref, dgn_ref, db_ref):
        i = pl.program_id(0)
        first = (i == 0) | (i == nl) if nseg == 2 else (i == 0)

        @pl.when(first)
        def _():
            dg_ref[...] = jnp.zeros_like(dg_ref)
            dgn_ref[...] = jnp.zeros_like(dgn_ref)
            db_ref[...] = jnp.zeros_like(db_ref)

        ny, ry = _rms(y_ref[...] + b_ref[...])
        g = mod_ref[pl.ds(gate_i, 1), :]
        gn_v = gn_ref[...]
        dh_v = dh_ref[...]
        dg_ref[...] += _sum0(dh_v * (ny * gn_v))
        dgn_ref[...] += _sum0(dh_v * ny * g)
        dny = dh_v * (g * gn_v)
        dy = ry * (dny - ny * jnp.mean(dny * ny, axis=-1, keepdims=True))
        db_ref[...] += _sum0(dy)
        dy_ref[...] = dy.astype(dy_ref.dtype)

    row = pl.BlockSpec((tm, D), lambda i: (i, 0))
    vec = pl.BlockSpec((1, D), lambda i: (0, 0))
    acc = SDS((nseg, 1, D), F32)
    return pl.pallas_call(
        body, name=name, grid=(rows // tm,),
        in_specs=[row, row, vec, pl.BlockSpec((None, 6, D), lambda i: (seg(i), 0, 0)), vec],
        out_specs=[row, _acc_spec(D, seg), _acc_spec(D, seg), _acc_spec(D, seg)],
        out_shape=[SDS((rows, D), MMT), acc, acc, acc], compiler_params=_cparams("arbitrary"))(dh, y, bias, mod, gn)


def _seg_layout(segs, H):
    out, base = [], H
    for s0, n in segs:
        out.append((s0, n, base))
        base += n + H
    return out, base


def _zero_pads(ref, lay, H):
    width = ref.shape[1]
    ref[pl.ds(0, H), :] = jnp.zeros((H, width), ref.dtype)
    for _, n, base in lay:
        ref[pl.ds(base + n, H), :] = jnp.zeros((H, width), ref.dtype)


def _window(ref, base, off, H):
    return ref[pl.ds(base - H + off, CONV_ROWS + 2 * H), :]


def _taps(win, H, offs):
    W = CONV_ROWS + 2 * H
    rolled, out = {}, {}
    for o in offs:
        s = H + o
        b = s % SUBLANES
        if b not in rolled:
            rolled[b] = win if b == 0 else pltpu.roll(win, shift=W - b, axis=0)
        out[o] = rolled[b][s - b:s - b + CONV_ROWS, :]
    return out


def _chunks(lay, fn):
    for s0, n, base in lay:
        def step(r, carry, s0=s0, base=base):
            fn(s0, base, pl.multiple_of(r * CONV_ROWS, CONV_ROWS))
            return carry
        lax.fori_loop(0, n // CONV_ROWS, step, 0)


def _ffn_gate(z0, conv_w, conv_b, segs, name):
    T, F2 = z0.shape
    F = F2 // 2
    tc = _tile(F, 256)
    nF = F // tc
    H = SUBLANES
    lay, srows = _seg_layout(segs, H)
    offs = [-1, 0, 1]

    def body(zg_ref, zv_ref, wg_ref, wv_ref, bg_ref, bv_ref, u_ref, xg, xv):
        _zero_pads(xg, lay, H)
        _zero_pads(xv, lay, H)
        for s0, n, base in lay:
            xg[pl.ds(base, n), :] = zg_ref[pl.ds(s0, n), :]
            xv[pl.ds(base, n), :] = zv_ref[pl.ds(s0, n), :]

        def chunk(s0, base, off):
            tg = _taps(_window(xg, base, off, H), H, offs)
            tv = _taps(_window(xv, base, off, H), H, offs)
            zg = bg_ref[...] + sum(tg[k - 1] * wg_ref[pl.ds(k, 1), :] for k in range(3))
            zv = bv_ref[...] + sum(tv[k - 1] * wv_ref[pl.ds(k, 1), :] for k in range(3))
            u_ref[pl.ds(s0 + off, CONV_ROWS), :] = (zg * _sig(zg) * zv).astype(u_ref.dtype)

        _chunks(lay, chunk)

    colg = lambda r: pl.BlockSpec((r, tc), lambda j: (0, j))
    colv = lambda r: pl.BlockSpec((r, tc), lambda j: (0, j + nF))
    return pl.pallas_call(
        body, name=name, grid=(nF,),
        in_specs=[colg(T), colv(T), colg(3), colv(3), colg(1), colv(1)],
        out_specs=colg(T), out_shape=SDS((T, F), MMT),
        scratch_shapes=[pltpu.VMEM((srows, tc), F32), pltpu.VMEM((srows, tc), F32)],
        compiler_params=_cparams("parallel"))(z0, z0, conv_w, conv_w, conv_b, conv_b)


def _ffn_gate_bwd(z0, du, conv_w, conv_b, segs, name):
    T, F2 = z0.shape
    F = F2 // 2
    tc = _tile(F, 256)
    nF = F // tc
    H = SUBLANES
    lay, srows = _seg_layout(segs, H)
    offs = [-1, 0, 1]

    def body(zo_ref, zt_ref, du_ref, wo_ref, wt_ref, bo_ref, bt_ref, u_ref, dz0_ref, dw_ref, db_ref, xo, xt, dzp):
        own_is_gate = pl.program_id(1) == 0
        for ref in (xo, xt, dzp):
            _zero_pads(ref, lay, H)
        for s0, n, base in lay:
            xo[pl.ds(base, n), :] = zo_ref[pl.ds(s0, n), :]
            xt[pl.ds(base, n), :] = zt_ref[pl.ds(s0, n), :]

        def grads(s0, base, off):
            to = _taps(_window(xo, base, off, H), H, offs)
            tt = _taps(_window(xt, base, off, H), H, offs)
            zo = bo_ref[...] + sum(to[k - 1] * wo_ref[pl.ds(k, 1), :] for k in range(3))
            zt = bt_ref[...] + sum(tt[k - 1] * wt_ref[pl.ds(k, 1), :] for k in range(3))
            so, st = _sig(zo), _sig(zt)
            du_v = du_ref[pl.ds(s0 + off, CONV_ROWS), :]
            d_gate = du_v * zt * (so * (1.0 + zo * (1.0 - so)))
            d_val = du_v * (zt * st)
            dzp[pl.ds(base + off, CONV_ROWS), :] = jnp.where(own_is_gate, d_gate, d_val)

            @pl.when(own_is_gate)
            def _():
                u_ref[pl.ds(s0 + off, CONV_ROWS), :] = (zo * so * zt).astype(u_ref.dtype)

        _chunks(lay, grads)
        dw_ref[...] = jnp.zeros_like(dw_ref)
        db_ref[...] = jnp.zeros_like(db_ref)

        def back(s0, base, off):
            td = _taps(_window(dzp, base, off, H), H, offs)
            tx = _taps(_window(xo, base, off, H), H, offs)
            dz0 = sum(td[1 - k] * wo_ref[pl.ds(k, 1), :] for k in range(3))
            dz0_ref[pl.ds(s0 + off, CONV_ROWS), :] = dz0.astype(dz0_ref.dtype)
            db_ref[...] += _sum0(td[0])
            for k in range(3):
                dw_ref[pl.ds(k, 1), :] += _sum0(td[0] * tx[k - 1])

        _chunks(lay, back)

    own = lambda r: pl.BlockSpec((r, tc), lambda j, hf: (0, hf * nF + j))
    oth = lambda r: pl.BlockSpec((r, tc), lambda j, hf: (0, (1 - hf) * nF + j))
    ucol = pl.BlockSpec((T, tc), lambda j, hf: (0, j))
    return pl.pallas_call(
        body, name=name, grid=(nF, 2),
        in_specs=[own(T), oth(T), ucol, own(3), oth(3), own(1), oth(1)],
        out_specs=[ucol, own(T), own(3), own(1)],
        out_shape=[SDS((T, F), MMT), SDS((T, F2), MMT), SDS((3, F2), F32), SDS((1, F2), F32)],
        scratch_shapes=[pltpu.VMEM((srows, tc), F32)] * 3,
        compiler_params=_cparams("parallel", "arbitrary"))(z0, z0, du, conv_w, conv_w, conv_b, conv_b)


def _glu_conv(p0, b_in, dw_w, dw_b, segs, name):
    T, D2 = p0.shape
    D = D2 // 2
    tc = _tile(D, 256)
    nD = D // tc
    H = 2 * SUBLANES
    half = (CONV_WIDTH - 1) // 2
    lay, srows = _seg_layout(segs, H)
    offs = list(range(-half, half + 1))

    def body(pa_ref, pg_ref, ba_ref, bg_ref, w_ref, b_ref, z2_ref, z1p):
        _zero_pads(z1p, lay, H)

        def glu(s0, base, off):
            rows = pl.ds(s0 + off, CONV_ROWS)
            z1p[pl.ds(base + off, CONV_ROWS), :] = (pa_ref[rows, :] + ba_ref[...]) * _sig(pg_ref[rows, :] + bg_ref[...])

        _chunks(lay, glu)

        def conv(s0, base, off):
            t = _taps(_window(z1p, base, off, H), H, offs)
            acc = b_ref[...] + t[-half] * w_ref[pl.ds(0, 1), :]
            for k in range(1, CONV_WIDTH):
                acc = acc + t[k - half] * w_ref[pl.ds(k, 1), :]
            z2_ref[pl.ds(s0 + off, CONV_ROWS), :] = acc

        _chunks(lay, conv)

    cola = lambda r: pl.BlockSpec((r, tc), lambda j: (0, j))
    colg = lambda r: pl.BlockSpec((r, tc), lambda j: (0, j + nD))
    return pl.pallas_call(
        body, name=name, grid=(nD,),
        in_specs=[cola(T), colg(T), cola(1), colg(1), cola(CONV_WIDTH), cola(1)],
        out_specs=cola(T), out_shape=SDS((T, D), F32), scratch_shapes=[pltpu.VMEM((srows, tc), F32)],
        compiler_params=_cparams("parallel"))(p0, p0, b_in, b_in, dw_w, dw_b)


def _glu_conv_bwd(p0, b_in, dw_w, dz2, segs, name):
    T, D2 = p0.shape
    D = D2 // 2
    tc = _tile(D, 256)
    nD = D // tc
    H = 2 * SUBLANES
    half = (CONV_WIDTH - 1) // 2
    lay, srows = _seg_layout(segs, H)
    offs = list(range(-half, half + 1))

    def body(pa_ref, pg_ref, ba_ref, bg_ref, w_ref, dz2_ref, dpa_ref, dpg_ref, dw_ref, db_ref, dba_ref, dbg_ref, z1p, dzp):
        _zero_pads(z1p, lay, H)
        _zero_pads(dzp, lay, H)
        for s0, n, base in lay:
            dzp[pl.ds(base, n), :] = dz2_ref[pl.ds(s0, n), :]

        def glu(s0, base, off):
            rows = pl.ds(s0 + off, CONV_ROWS)
            z1p[pl.ds(base + off, CONV_ROWS), :] = (pa_ref[rows, :] + ba_ref[...]) * _sig(pg_ref[rows, :] + bg_ref[...])

        _chunks(lay, glu)
        for ref in (dw_ref, db_ref, dba_ref, dbg_ref):
            ref[...] = jnp.zeros_like(ref)

        def back(s0, base, off):
            td = _taps(_window(dzp, base, off, H), H, offs)
            tz = _taps(_window(z1p, base, off, H), H, offs)
            dz1 = td[half] * w_ref[pl.ds(0, 1), :]
            for k in range(1, CONV_WIDTH):
                dz1 = dz1 + td[half - k] * w_ref[pl.ds(k, 1), :]
            db_ref[...] += _sum0(td[0])
            for k in range(CONV_WIDTH):
                dw_ref[pl.ds(k, 1), :] += _sum0(td[0] * tz[k - half])
            rows = pl.ds(s0 + off, CONV_ROWS)
            pa = pa_ref[rows, :] + ba_ref[...]
            sg = _sig(pg_ref[rows, :] + bg_ref[...])
            dpa = dz1 * sg
            dpg = dz1 * pa * (sg * (1.0 - sg))
            dba_ref[...] += _sum0(dpa)
            dbg_ref[...] += _sum0(dpg)
            dpa_ref[rows, :] = dpa.astype(dpa_ref.dtype)
            dpg_ref[rows, :] = dpg.astype(dpg_ref.dtype)

        _chunks(lay, back)

    cola = lambda r: pl.BlockSpec((r, tc), lambda j: (0, j))
    colg = lambda r: pl.BlockSpec((r, tc), lambda j: (0, j + nD))
    return pl.pallas_call(
        body, name=name, grid=(nD,),
        in_specs=[cola(T), colg(T), cola(1), colg(1), cola(CONV_WIDTH), cola(T)],
        out_specs=[cola(T), cola(T), cola(CONV_WIDTH), cola(1), cola(1), cola(1)],
        out_shape=[SDS((T, D), MMT), SDS((T, D), MMT), SDS((CONV_WIDTH, D), F32), SDS((1, D), F32), SDS((1, D), F32),
                   SDS((1, D), F32)],
        scratch_shapes=[pltpu.VMEM((srows, tc), F32)] * 2, compiler_params=_cparams("parallel"))(p0, p0, b_in, b_in, dw_w, dz2)


def _layer_norm_stats(v):
    mu = jnp.mean(v, axis=-1, keepdims=True)
    var = jnp.mean(jnp.square(v - mu), axis=-1, keepdims=True)
    rstd = lax.rsqrt(var + EPS)
    return (v - mu) * rstd, rstd


def _ln_silu(z2, ln_g, ln_b, rows, tm, name):
    D = z2.shape[1]

    def body(z_ref, g_ref, b_ref, o_ref):
        xh, _ = _layer_norm_stats(z_ref[...])
        z3 = xh * g_ref[...] + b_ref[...]
        o_ref[...] = (z3 * _sig(z3)).astype(o_ref.dtype)

    row = pl.BlockSpec((tm, D), lambda i: (i, 0))
    vec = pl.BlockSpec((1, D), lambda i: (0, 0))
    return pl.pallas_call(body, name=name, grid=(rows // tm,), in_specs=[row, vec, vec], out_specs=row,
                          out_shape=SDS((rows, D), MMT), compiler_params=_cparams("parallel"))(z2, ln_g, ln_b)


def _ln_silu_bwd(z2, dz4, ln_g, ln_b, rows, tm, name):
    D = z2.shape[1]

    def body(z_ref, d_ref, g_ref, b_ref, dz_ref, dg_ref, db_ref):
        @pl.when(pl.program_id(0) == 0)
        def _():
            dg_ref[...] = jnp.zeros_like(dg_ref)
            db_ref[...] = jnp.zeros_like(db_ref)

        xh, rstd = _layer_norm_stats(z_ref[...])
        z3 = xh * g_ref[...] + b_ref[...]
        s = _sig(z3)
        dz3 = d_ref[...] * (s * (1.0 + z3 * (1.0 - s)))
        dg_ref[...] += _sum0(dz3 * xh)
        db_ref[...] += _sum0(dz3)
        dxh = dz3 * g_ref[...]
        dz_ref[...] = rstd * (dxh - jnp.mean(dxh, axis=-1, keepdims=True) - xh * jnp.mean(dxh * xh, axis=-1, keepdims=True))

    row = pl.BlockSpec((tm, D), lambda i: (i, 0))
    vec = pl.BlockSpec((1, D), lambda i: (0, 0))
    return pl.pallas_call(body, name=name, grid=(rows // tm,), in_specs=[row, row, vec, vec], out_specs=[row, vec, vec],
                          out_shape=[SDS((rows, D), F32), SDS((1, D), F32), SDS((1, D), F32)],
                          compiler_params=_cparams("arbitrary"))(z2, dz4, ln_g, ln_b)


def _rot_half_pairs(v):
    width = v.shape[1]
    lane = lax.broadcasted_iota(jnp.int32, v.shape, 1)
    return jnp.where((lane % 32) < 16, -pltpu.roll(v, shift=width - 16, axis=1), pltpu.roll(v, shift=16, axis=1))


def _rope(qkv, cos, sin, L, qk, tm, name):
    width = qkv.shape[1]
    kv = width - qk

    def body(x_ref, c_ref, s_ref, qk_ref, v_ref):
        xv = x_ref[:, pl.ds(0, qk)]
        c = jnp.tile(c_ref[...], (1, qk // LANES))
        s = jnp.tile(s_ref[...], (1, qk // LANES))
        qk_ref[...] = (xv * c + _rot_half_pairs(xv) * s).astype(qk_ref.dtype)
        v_ref[...] = x_ref[:, pl.ds(qk, kv)].astype(v_ref.dtype)

    tab = pl.BlockSpec((tm, LANES), lambda i: (i, 0))
    return pl.pallas_call(
        body, name=name, grid=(L // tm,), in_specs=[pl.BlockSpec((tm, width), lambda i: (i, 0)), tab, tab],
        out_specs=[pl.BlockSpec((tm, qk), lambda i: (i, 0)), pl.BlockSpec((tm, kv), lambda i: (i, 0))],
        out_shape=[SDS((L, qk), MMT), SDS((L, kv), MMT)], compiler_params=_cparams("parallel"))(qkv, cos, sin)


def _rope_bwd(dqk, dv, cos, sin, tm, name):
    L, qk = dqk.shape
    kv = dv.shape[1]

    def body(d_ref, dv_ref, c_ref, s_ref, o_ref):
        dv_ = d_ref[...]
        c = jnp.tile(c_ref[...], (1, qk // LANES))
        s = jnp.tile(s_ref[...], (1, qk // LANES))
        o_ref[:, pl.ds(0, qk)] = (dv_ * c - _rot_half_pairs(dv_ * s)).astype(o_ref.dtype)
        o_ref[:, pl.ds(qk, kv)] = dv_ref[...].astype(o_ref.dtype)

    tab = pl.BlockSpec((tm, LANES), lambda i: (i, 0))
    return pl.pallas_call(
        body, name=name, grid=(L // tm,),
        in_specs=[pl.BlockSpec((tm, qk), lambda i: (i, 0)), pl.BlockSpec((tm, kv), lambda i: (i, 0)), tab, tab],
        out_specs=pl.BlockSpec((tm, qk + kv), lambda i: (i, 0)), out_shape=SDS((L, qk + kv), MMT),
        compiler_params=_cparams("parallel"))(dqk, dv, cos, sin)


def _band_specs(nb, width):
    blk = lambda f: pl.BlockSpec((None, ATTN_BLOCK, width), f)
    return [blk(lambda h, n: (h, jnp.maximum(n - 1, 0), 0)), blk(lambda h, n: (h, n, 0)),
            blk(lambda h, n: (h, jnp.minimum(n + 1, nb - 1), 0))]


def _window_mask(n, L):
    qi = lax.broadcasted_iota(jnp.int32, (ATTN_BLOCK, 3 * ATTN_BLOCK), 0)
    kk = lax.broadcasted_iota(jnp.int32, (ATTN_BLOCK, 3 * ATTN_BLOCK), 1)
    key_abs = (n - 1) * ATTN_BLOCK + kk
    return (jnp.abs(qi + ATTN_BLOCK - kk) <= ATTN_BLOCK) & (key_abs >= 0) & (key_abs < L)


def _attn_fwd(q, k, v, kc, vc, sink, name):
    nkv, _, L, hd = q.shape
    C = kc.shape[1]
    nb = L // ATTN_BLOCK
    scale = HEAD_DIM ** -0.5

    def body(sink_ref, q_ref, k0, k1, k2, v0, v1, v2, kc_ref, vc_ref, o_ref, lse_ref):
        hh, n = pl.program_id(0), pl.program_id(1)
        kw = jnp.concatenate([k0[...], k1[...], k2[...]], axis=0)
        vw = jnp.concatenate([v0[...], v1[...], v2[...]], axis=0)
        mask = _window_mask(n, L)
        for g in range(Q_PER_KV):
            qg = q_ref[g]
            sw = jnp.where(mask, _dot(qg, kw, 1, 1) * scale, NEG)
            sc = _dot(qg, kc_ref[...], 1, 1) * scale
            sk = sink_ref[hh * Q_PER_KV + g]
            m = jnp.maximum(jnp.maximum(jnp.max(sw, axis=-1, keepdims=True), jnp.max(sc, axis=-1, keepdims=True)), sk)
            pw, pc = jnp.exp(sw - m), jnp.exp(sc - m)
            den = jnp.sum(pw, axis=-1, keepdims=True) + jnp.sum(pc, axis=-1, keepdims=True) + jnp.exp(sk - m)
            inv = 1.0 / den
            o_ref[g] = _dot(pw * inv, vw, 1, 0) + _dot(pc * inv, vc_ref[...], 1, 0)
            lse_ref[g] = m + jnp.log(den)

    qspec = pl.BlockSpec((None, Q_PER_KV, ATTN_BLOCK, hd), lambda h, n: (h, 0, n, 0))
    cspec = pl.BlockSpec((None, C, hd), lambda h, n: (h, 0, 0))
    return pl.pallas_call(
        body, name=name, grid=(nkv, nb),
        in_specs=[pl.BlockSpec(memory_space=pltpu.SMEM), qspec] + _band_specs(nb, hd) + _band_specs(nb, hd) + [cspec, cspec],
        out_specs=[qspec, pl.BlockSpec((None, Q_PER_KV, ATTN_BLOCK, 1), lambda h, n: (h, 0, n, 0))],
        out_shape=[SDS((nkv, Q_PER_KV, L, hd), F32), SDS((nkv, Q_PER_KV, L, 1), F32)],
        compiler_params=_cparams("parallel", "parallel"))(sink, q, k, k, k, v, v, v, kc, vc)


def _attn_bwd_q(q, k, v, kc, vc, sink, o, do, lse, name):
    nkv, _, L, hd = q.shape
    C = kc.shape[1]
    nb = L // ATTN_BLOCK
    scale = HEAD_DIM ** -0.5

    def body(sink_ref, q_ref, k0, k1, k2, v0, v1, v2, kc_ref, vc_ref, o_ref, do_ref, lse_ref, dq_ref, dkc_ref, dvc_ref, dsk_ref):
        hh, n = pl.program_id(0), pl.program_id(1)

        @pl.when(n == 0)
        def _():
            dkc_ref[...] = jnp.zeros_like(dkc_ref)
            dvc_ref[...] = jnp.zeros_like(dvc_ref)
            dsk_ref[...] = jnp.zeros_like(dsk_ref)

        kw = jnp.concatenate([k0[...], k1[...], k2[...]], axis=0)
        vw = jnp.concatenate([v0[...], v1[...], v2[...]], axis=0)
        mask = _window_mask(n, L)
        for g in range(Q_PER_KV):
            qg, dog, lse_g = q_ref[g], do_ref[g], lse_ref[g]
            delta = jnp.sum(dog.astype(F32) * o_ref[g], axis=-1, keepdims=True)
            pw = jnp.exp(jnp.where(mask, _dot(qg, kw, 1, 1) * scale, NEG) - lse_g)
            pc = jnp.exp(_dot(qg, kc_ref[...], 1, 1) * scale - lse_g)
            dsw = pw * (_dot(dog, vw, 1, 1) - delta)
            dsc = pc * (_dot(dog, vc_ref[...], 1, 1) - delta)
            dq_ref[g] = (_dot(dsw, kw, 1, 0) + _dot(dsc, kc_ref[...], 1, 0)) * scale
            dkc_ref[...] += _dot(dsc, qg, 0, 0) * scale
            dvc_ref[...] += _dot(pc, dog, 0, 0)
            psk = jnp.exp(sink_ref[hh * Q_PER_KV + g] - lse_g)
            dsk_ref[pl.ds(g, 1), :] += jnp.broadcast_to(jnp.sum(-psk * delta, axis=0, keepdims=True), (1, LANES))

    qspec = pl.BlockSpec((None, Q_PER_KV, ATTN_BLOCK, hd), lambda h, n: (h, 0, n, 0))
    lspec = pl.BlockSpec((None, Q_PER_KV, ATTN_BLOCK, 1), lambda h, n: (h, 0, n, 0))
    cspec = pl.BlockSpec((None, C, hd), lambda h, n: (h, 0, 0))
    return pl.pallas_call(
        body, name=name, grid=(nkv, nb),
        in_specs=[pl.BlockSpec(memory_space=pltpu.SMEM), qspec] + _band_specs(nb, hd) + _band_specs(nb, hd)
        + [cspec, cspec, qspec, qspec, lspec],
        out_specs=[qspec, cspec, cspec, pl.BlockSpec((None, SUBLANES, LANES), lambda h, n: (h, 0, 0))],
        out_shape=[SDS((nkv, Q_PER_KV, L, hd), F32), SDS((nkv, C, hd), F32), SDS((nkv, C, hd), F32),
                   SDS((nkv, SUBLANES, LANES), F32)],
        compiler_params=_cparams("parallel", "arbitrary"))(sink, q, k, k, k, v, v, v, kc, vc, o, do, lse)


def _attn_bwd_kv(q, k, v, o, do, lse, name):
    nkv, _, L, hd = q.shape
    nb = L // ATTN_BLOCK
    scale = HEAD_DIM ** -0.5

    def body(q0, q1, q2, do0, do1, do2, o0, o1, o2, l0, l1, l2, k_ref, v_ref, dk_ref, dv_ref):
        j = pl.program_id(1)
        qi = lax.broadcasted_iota(jnp.int32, (ATTN_BLOCK, ATTN_BLOCK), 0)
        kk = lax.broadcasted_iota(jnp.int32, (ATTN_BLOCK, ATTN_BLOCK), 1)
        kj, vj = k_ref[...], v_ref[...]
        dk = jnp.zeros((ATTN_BLOCK, hd), F32)
        dv = jnp.zeros((ATTN_BLOCK, hd), F32)
        for slot, (q_r, do_r, o_r, l_r) in enumerate(((q0, do0, o0, l0), (q1, do1, o1, l1), (q2, do2, o2, l2))):
            n = j - 1 + slot
            ok = (n >= 0) & (n < nb) & (jnp.abs(qi + ATTN_BLOCK - ((2 - slot) * ATTN_BLOCK + kk)) <= ATTN_BLOCK)
            for g in range(Q_PER_KV):
                qg, dog = q_r[g], do_r[g]
                delta = jnp.sum(dog.astype(F32) * o_r[g], axis=-1, keepdims=True)
                p = jnp.exp(jnp.where(ok, _dot(qg, kj, 1, 1) * scale - l_r[g], NEG))
                ds = p * (_dot(dog, vj, 1, 1) - delta)
                dk = dk + _dot(ds, qg, 0, 0) * scale
                dv = dv + _dot(p, dog, 0, 0)
        dk_ref[...] = dk
        dv_ref[...] = dv

    def band(width):
        blk = lambda f: pl.BlockSpec((None, Q_PER_KV, ATTN_BLOCK, width), f)
        return [blk(lambda h, j: (h, 0, jnp.maximum(j - 1, 0), 0)), blk(lambda h, j: (h, 0, j, 0)),
                blk(lambda h, j: (h, 0, jnp.minimum(j + 1, nb - 1), 0))]

    kspec = pl.BlockSpec((None, ATTN_BLOCK, hd), lambda h, j: (h, j, 0))
    return pl.pallas_call(
        body, name=name, grid=(nkv, nb), in_specs=band(hd) + band(hd) + band(hd) + band(1) + [kspec, kspec],
        out_specs=[kspec, kspec], out_shape=[SDS((nkv, L, hd), F32), SDS((nkv, L, hd), F32)],
        compiler_params=_cparams("parallel", "parallel"))(q, q, q, do, do, do, o, o, o, lse, lse, lse, k, v)


_GELU_K = math.sqrt(2.0 / math.pi)


def _gelu(v):
    return 0.5 * v * (1.0 + jnp.tanh(_GELU_K * (v + 0.044715 * (v * v * v))))


def _gelu_grad(v):
    t = jnp.tanh(_GELU_K * (v + 0.044715 * (v * v * v)))
    return 0.5 * (1.0 + t) + 0.5 * v * (1.0 - t * t) * (_GELU_K * (1.0 + 3.0 * 0.044715 * (v * v)))


def _gmlp_fwd(p0, b_in, ln_g, ln_b, w_s, b_s, name):
    L, W2 = p0.shape
    W = W2 // 2
    G = W // GMLP_GROUP_DIM

    def body(p_ref, bi_ref, g_ref, b_ref, ws_ref, bs_ref, o_ref):
        ge = _gelu(p_ref[...] + bi_ref[...])
        xh, _ = _layer_norm_stats(ge[:, W:])
        vln = xh * g_ref[...] + b_ref[...]
        for gi in range(G):
            cols = slice(gi * GMLP_GROUP_DIM, (gi + 1) * GMLP_GROUP_DIM)
            s = _dot(ws_ref[gi], vln[:, cols], 1, 0) + bs_ref[gi]
            o_ref[:, cols] = (ge[:, cols] * s).astype(o_ref.dtype)

    full = lambda shape: pl.BlockSpec(shape, lambda i: (0,) * len(shape))
    return pl.pallas_call(
        body, name=name, grid=(L // GMLP_CHUNK,),
        in_specs=[pl.BlockSpec((GMLP_CHUNK, W2), lambda i: (i, 0)), full((1, W2)), full((1, W)), full((1, W)),
                  full((G, GMLP_CHUNK, GMLP_CHUNK)), full((G, GMLP_CHUNK, 1))],
        out_specs=pl.BlockSpec((GMLP_CHUNK, W), lambda i: (i, 0)), out_shape=SDS((L, W), MMT),
        compiler_params=_cparams("parallel"))(p0, b_in, ln_g, ln_b, w_s, b_s)


def _gmlp_bwd(p0, dus, b_in, ln_g, ln_b, w_s, w_st, b_s, name):
    L, W2 = p0.shape
    W = W2 // 2
    G = W // GMLP_GROUP_DIM

    def body(p_ref, d_ref, bi_ref, g_ref, b_ref, ws_ref, wst_ref, bs_ref, dpre_ref, dbi_ref, dg_ref, db_ref, dws_ref, dbs_ref, dvln):
        @pl.when(pl.program_id(0) == 0)
        def _():
            for ref in (dbi_ref, dg_ref, db_ref, dws_ref, dbs_ref):
                ref[...] = jnp.zeros_like(ref)

        pre = p_ref[...] + bi_ref[...]
        ge = _gelu(pre)
        xh, rstd = _layer_norm_stats(ge[:, W:])
        vln = xh * g_ref[...] + b_ref[...]
        dge_u = []
        for gi in range(G):
            cols = slice(gi * GMLP_GROUP_DIM, (gi + 1) * GMLP_GROUP_DIM)
            vg = vln[:, cols]
            s = _dot(ws_ref[gi], vg, 1, 0) + bs_ref[gi]
            dus_g = d_ref[:, cols]
            dge_u.append(dus_g * s)
            ds = dus_g * ge[:, cols]
            dbs_ref[gi] += jnp.sum(ds, axis=1, keepdims=True)
            dws_ref[gi] += _dot(ds, vg, 1, 1)
            dvln[:, cols] = _dot(wst_ref[gi], ds, 1, 0)
        dv = dvln[...]
        dg_ref[...] += _sum0(dv * xh)
        db_ref[...] += _sum0(dv)
        dxh = dv * g_ref[...]
        dv0 = rstd * (dxh - jnp.mean(dxh, axis=-1, keepdims=True) - xh * jnp.mean(dxh * xh, axis=-1, keepdims=True))
        dpre = jnp.concatenate(dge_u + [dv0], axis=1) * _gelu_grad(pre)
        dbi_ref[...] += _sum0(dpre)
        dpre_ref[...] = dpre.astype(dpre_ref.dtype)

    full = lambda shape: pl.BlockSpec(shape, lambda i: (0,) * len(shape))
    mats = (G, GMLP_CHUNK, GMLP_CHUNK)
    return pl.pallas_call(
        body, name=name, grid=(L // GMLP_CHUNK,),
        in_specs=[pl.BlockSpec((GMLP_CHUNK, W2), lambda i: (i, 0)), pl.BlockSpec((GMLP_CHUNK, W), lambda i: (i, 0)),
                  full((1, W2)), full((1, W)), full((1, W)), full(mats), full(mats), full((G, GMLP_CHUNK, 1))],
        out_specs=[pl.BlockSpec((GMLP_CHUNK, W2), lambda i: (i, 0)), full((1, W2)), full((1, W)), full((1, W)), full(mats),
                   full((G, GMLP_CHUNK, 1))],
        out_shape=[SDS((L, W2), MMT), SDS((1, W2), F32), SDS((1, W), F32), SDS((1, W), F32), SDS(mats, F32),
                   SDS((G, GMLP_CHUNK, 1), F32)],
        scratch_shapes=[pltpu.VMEM((GMLP_CHUNK, W), F32)], compiler_params=_cparams("arbitrary"))(
            p0, dus, b_in, ln_g, ln_b, w_s, w_st, b_s)


def _loss_head(h, target, tm, name):
    L, D = h.shape

    def body(h_ref, t_ref, l_ref, d_ref):
        @pl.when(pl.program_id(0) == 0)
        def _():
            l_ref[...] = jnp.zeros_like(l_ref)

        e = h_ref[...] - t_ref[...]
        l_ref[...] += 0.5 * jnp.sum(jnp.mean(e * e, axis=-1, keepdims=True), axis=0, keepdims=True)
        d_ref[...] = e * (1.0 / D)

    row = pl.BlockSpec((tm, D), lambda i: (i, 0))
    return pl.pallas_call(body, name=name, grid=(L // tm,), in_specs=[row, row],
                          out_specs=[pl.BlockSpec((1, 1), lambda i: (0, 0)), row],
                          out_shape=[SDS((1, 1), F32), SDS((L, D), F32)], compiler_params=_cparams("arbitrary"))(h, target)


def _ada_fwd(cond, ada_w, ada_b, name):
    NL, D, n = ada_w.shape
    tn = _tile(n, 768)

    def body(c_ref, w_ref, b_ref, o_ref):
        cv = c_ref[...]
        o_ref[...] = _dot(cv * _sig(cv), w_ref[...], 1, 0) + b_ref[...]

    return pl.pallas_call(
        body, name=name, grid=(NL, n // tn),
        in_specs=[pl.BlockSpec((2 * SUBLANES, D), lambda i, j: (0, 0)), pl.BlockSpec((None, D, tn), lambda i, j: (i, 0, j)),
                  pl.BlockSpec((None, 1, tn), lambda i, j: (i, 0, j))],
        out_specs=pl.BlockSpec((None, 2 * SUBLANES, tn), lambda i, j: (i, 0, j)), out_shape=SDS((NL, 2 * SUBLANES, n), F32),
        compiler_params=_cparams("parallel", "parallel"))(cond, ada_w, ada_b)


def _ada_bwd(cond, ada_w, dm_lat, dm_ctx, name):
    NL, D, n = ada_w.shape
    tn = _tile(n, 768)

    def body(c_ref, w_ref, dl_ref, dc_ref, dw_ref, ds_ref):
        @pl.when((pl.program_id(0) == 0) & (pl.program_id(1) == 0))
        def _():
            ds_ref[...] = jnp.zeros_like(ds_ref)

        cv = c_ref[...]
        row = lax.broadcasted_iota(jnp.int32, (SUBLANES, tn), 0)
        ctx_rows = jnp.where(row == 0, _sum0(dc_ref[...]), 0.0)
        dm = jnp.concatenate([dl_ref[...], ctx_rows], axis=0)
        dw_ref[...] = _dot(cv * _sig(cv), dm, 0, 0)
        ds_ref[...] += _dot(dm, w_ref[...], 1, 1)

    dspec = pl.BlockSpec((None, SUBLANES, tn), lambda i, j: (i, 0, j))
    return pl.pallas_call(
        body, name=name, grid=(NL, n // tn),
        in_specs=[pl.BlockSpec((2 * SUBLANES, D), lambda i, j: (0, 0)), pl.BlockSpec((None, D, tn), lambda i, j: (i, 0, j)),
                  dspec, dspec],
        out_specs=[pl.BlockSpec((None, D, tn), lambda i, j: (i, 0, j)), pl.BlockSpec((2 * SUBLANES, D), lambda i, j: (0, 0))],
        out_shape=[SDS((NL, D, n), F32), SDS((2 * SUBLANES, D), F32)],
        compiler_params=_cparams("arbitrary", "arbitrary"))(cond, ada_w, dm_lat, dm_ctx)


def _adam_math(w, g, m, v):
    m = ADAM_B1 * m + (1.0 - ADAM_B1) * g
    v = ADAM_B2 * v + (1.0 - ADAM_B2) * jnp.square(g)
    m_hat = m / (1.0 - ADAM_B1 ** ADAM_STEP)
    v_hat = v / (1.0 - ADAM_B2 ** ADAM_STEP)
    return -ADAM_LR * (m_hat / (jnp.sqrt(v_hat) + ADAM_EPS) + ADAM_WD * w), m, v


def _row_tile(rows, cols, elems):
    want = max(SUBLANES, elems // cols)
    best = SUBLANES if rows % SUBLANES == 0 else rows
    for d in range(SUBLANES, min(rows, want) + 1, SUBLANES):
        if rows % d == 0:
            best = d
    return best


def _adamw(w, m, v, parts, name):
    R, C = w.shape
    tr = _row_tile(R, C, 128 * 1024)
    npart = len(parts)

    def body(*refs):
        w_ref, m_ref, v_ref = refs[:3]
        g_ref, d_ref, nm_ref, nv_ref = refs[3 + npart:]
        g = refs[3][...]
        for p_ref in refs[4:3 + npart]:
            g = g + p_ref[...]
        d, nm, nv = _adam_math(w_ref[...], g, m_ref[...], v_ref[...])
        g_ref[...], d_ref[...], nm_ref[...], nv_ref[...] = g, d, nm, nv

    blk = pl.BlockSpec((tr, C), lambda i: (i, 0))
    return pl.pallas_call(body, name=name, grid=(R // tr,), in_specs=[blk] * (3 + npart), out_specs=[blk] * 4,
                          out_shape=[SDS((R, C), F32)] * 4, compiler_params=_cparams("parallel"))(w, m, v, *parts)


def _adamw_layer(w, m, v, layer, parts, prev, name):
    _, R, C = w.shape
    tr = _row_tile(R, C, 128 * 1024)
    npart = len(parts)
    nprev = 0 if prev is None else 4

    def body(*refs):
        w_ref, m_ref, v_ref = refs[:3]
        g_ref, d_ref, nm_ref, nv_ref = refs[3 + npart + nprev:]
        g = refs[3][...]
        for p_ref in refs[4:3 + npart]:
            g = g + p_ref[...]
        d, nm, nv = _adam_math(w_ref[...], g, m_ref[...], v_ref[...])
        g_ref[...], d_ref[...], nm_ref[...], nv_ref[...] = g, d, nm, nv

    stacked = pl.BlockSpec((None, tr, C), lambda i: (layer, i, 0))
    flat = pl.BlockSpec((tr, C), lambda i: (i, 0))
    return pl.pallas_call(
        body, name=name, grid=(R // tr,),
        in_specs=[stacked] * 3 + [flat] * npart + [pl.BlockSpec(memory_space=pl.ANY)] * nprev, out_specs=[stacked] * 4,
        out_shape=[SDS(w.shape, F32)] * 4, input_output_aliases={3 + npart + k: k for k in range(nprev)},
        compiler_params=_cparams("parallel"))(w, m, v, *parts, *(prev or ()))


def _sum_slots(x, name, out_dtype=F32):
    S, R, C = x.shape
    tr = _row_tile(R, C, 128 * 1024)

    def body(x_ref, o_ref):
        acc = x_ref[0].astype(F32)
        for s in range(1, S):
            acc = acc + x_ref[s].astype(F32)
        o_ref[...] = acc.astype(o_ref.dtype)

    return pl.pallas_call(body, name=name, grid=(R // tr,), in_specs=[pl.BlockSpec((S, tr, C), lambda i: (0, i, 0))],
                          out_specs=pl.BlockSpec((tr, C), lambda i: (i, 0)), out_shape=SDS((R, C), out_dtype),
                          compiler_params=_cparams("parallel"))(x)


def _my_place():
    return lax.axis_index("x"), lax.axis_index("y"), lax.axis_index("c")


def _other_chips(x, y):
    return [(1 - x, y), (x, 1 - y), (1 - x, 1 - y)]


def _all_gather(v, name):
    R, C = v.shape

    def body(v_ref, o_ref, send_sems, recv_sems, local_sem):
        x, y, c = _my_place()
        me = 4 * x + 2 * y + c
        mine = pltpu.make_async_copy(v_ref, o_ref.at[me], local_sem)
        mine.start()
        copies = []
        for flip in range(1, N_DEV):
            fx, fy, fc = (flip >> 2) & 1, (flip >> 1) & 1, flip & 1
            peer = ((x + fx) % 2, (y + fy) % 2, (c + fc) % 2)
            cp = pltpu.make_async_remote_copy(src_ref=v_ref, dst_ref=o_ref.at[me], send_sem=send_sems.at[flip - 1],
                                              recv_sem=recv_sems.at[flip - 1], device_id=peer, device_id_type=MESH)
            cp.start()
            copies.append(cp)
        for cp in copies:
            cp.wait()
        mine.wait()

    return pl.pallas_call(
        body, name=name, in_specs=[pl.BlockSpec(memory_space=pl.ANY)], out_specs=pl.BlockSpec(memory_space=pl.ANY),
        out_shape=SDS((N_DEV, R, C), v.dtype),
        scratch_shapes=[pltpu.SemaphoreType.DMA((N_DEV - 1,)), pltpu.SemaphoreType.DMA((N_DEV - 1,)), pltpu.SemaphoreType.DMA],
        )(v)


def _shard_window(ref, axis, j, size):
    idx = [slice(None)] * len(ref.shape)
    idx[axis] = pl.ds(pl.multiple_of(j * size, SUBLANES), size)
    return ref.at[tuple(idx)]


def _gather_plan(axis):
    def plan(srcs, lands):
        x, y, c = _my_place()
        dst = _shard_window(lands[0], axis, 2 * x + y, srcs[0].shape[axis])
        return [(srcs[0], dst)], [(srcs[0], dst, (px, py, c)) for px, py in _other_chips(x, y)]
    return plan


def _scatter_plan(axis):
    def plan(srcs, lands):
        x, y, c = _my_place()
        j = 2 * x + y
        size = srcs[0].shape[axis] // N_CHIPS
        local = [(_shard_window(srcs[0], axis, j, size), lands[0].at[j])]
        remote = [(_shard_window(srcs[0], axis, 2 * px + py, size), lands[0].at[j], (px, py, c)) for px, py in _other_chips(x, y)]
        return local, remote
    return plan


def _all_gather_plan(srcs, lands):
    x, y, c = _my_place()
    dst = lands[0].at[4 * x + 2 * y + c]
    remote = []
    for flip in range(1, N_DEV):
        fx, fy, fc = (flip >> 2) & 1, (flip >> 1) & 1, flip & 1
        remote.append((srcs[0], dst, ((x + fx) % 2, (y + fy) % 2, (c + fc) % 2)))
    return [(srcs[0], dst)], remote


def _same_core_peers():
    x, y, c = _my_place()
    return [(px, py, c) for px, py in _other_chips(x, y)]


def _all_peers():
    x, y, c = _my_place()
    return [((x + (f >> 2 & 1)) % 2, (y + (f >> 1 & 1)) % 2, (c + (f & 1)) % 2) for f in range(1, N_DEV)]


def _sequencer_exchange(name, collective_id, exchanges, peers_fn):
    hbm = pltpu.MemorySpace.HBM
    src_refs = [[jax.new_ref(s, memory_space=hbm) for s in e[0]] for e in exchanges]
    land_refs = [[jax.empty_ref(s, memory_space=hbm) for s in e[1]] for e in exchanges]
    first = [sum(e[3] for e in exchanges[:i]) for i in range(len(exchanges))]
    ncopy = sum(e[3] for e in exchanges)

    @pl.kernel(mesh=plsc.ScalarSubcoreMesh(axis_name="sequencer", num_cores=1), name=name,
               scratch_types=(pltpu.SemaphoreType.DMA((ncopy,)), pltpu.SemaphoreType.DMA((ncopy,)), pltpu.SemaphoreType.DMA),
               compiler_params=pltpu.CompilerParams(collective_id=collective_id))
    def launch(send_sems, recv_sems, local_sem):
        peers = peers_fn()
        barrier = pltpu.get_barrier_semaphore()
        for peer in peers:
            pl.semaphore_signal(barrier, inc=1, device_id=peer, device_id_type=MESH)
        pl.semaphore_wait(barrier, len(peers))
        plans = [e[2](src_refs[i], land_refs[i]) for i, e in enumerate(exchanges)]
        for local, _ in plans:
            for src, dst in local:
                cp = pltpu.make_async_copy(src, dst, local_sem)
                cp.start()
                cp.wait()
        copies = []
        for i, (_, remote) in enumerate(plans):
            assert len(remote) == exchanges[i][3]
            for k, (src, dst, peer) in enumerate(remote):
                cp = pltpu.make_async_remote_copy(src_ref=src, dst_ref=dst, send_sem=send_sems.at[first[i] + k],
                                                  recv_sem=recv_sems.at[first[i] + k], device_id=peer, device_id_type=MESH)
                cp.start()
                copies.append(cp)
        for cp in copies:
            cp.wait()

    launch()
    return [[r[...] for r in refs] for refs in land_refs]


def _swap_with_sibling(parts, name):
    nt = len(parts)

    def body(*refs):
        ins, outs = refs[:nt], refs[nt:2 * nt]
        send_sems, recv_sems = refs[2 * nt:]
        x, y, c = _my_place()
        copies = []
        for t in range(nt):
            cp = pltpu.make_async_remote_copy(src_ref=ins[t], dst_ref=outs[t], send_sem=send_sems.at[t], recv_sem=recv_sems.at[t],
                                              device_id=(x, y, 1 - c), device_id_type=MESH)
            cp.start()
            copies.append(cp)
        for cp in copies:
            cp.wait()

    any_spec = pl.BlockSpec(memory_space=pl.ANY)
    return pl.pallas_call(
        body, name=name, in_specs=[any_spec] * nt, out_specs=[any_spec] * nt, out_shape=[SDS(p.shape, p.dtype) for p in parts],
        scratch_shapes=[pltpu.SemaphoreType.DMA((nt,)), pltpu.SemaphoreType.DMA((nt,))],
        )(*parts)


PACK_COLS = 1024


def _pack(arrays):
    flat = jnp.concatenate([a.reshape(-1) for a in arrays])
    pad = (-flat.shape[0]) % (SUBLANES * PACK_COLS)
    return jnp.pad(flat, (0, pad)).reshape(-1, PACK_COLS)


def _unpack(packed, shapes):
    flat, out, pos = packed.reshape(-1), [], 0
    for shape in shapes:
        n = math.prod(shape)
        out.append(flat[pos:pos + n].reshape(shape))
        pos += n
    return out


def _unshard_last(stacked):
    moved = jnp.moveaxis(stacked, 0, -2)
    return moved.reshape(moved.shape[:-2] + (moved.shape[-2] * moved.shape[-1],))


def _my_block_last(full, j):
    s = full.shape[-1] // N_CHIPS
    return lax.dynamic_index_in_dim(full.reshape(full.shape[:-1] + (N_CHIPS, s)), j, axis=full.ndim - 1, keepdims=False)


def _rope_tables(L):
    rows = L // GRID_W
    row = jnp.repeat(jnp.arange(rows), GRID_W).astype(F32)
    col = jnp.tile(jnp.arange(GRID_W), rows).astype(F32)
    axis_dim = HEAD_DIM // 2
    inv_freq = ROPE_BASE ** (-jnp.arange(0, axis_dim, 2, dtype=F32) / axis_dim)
    ang_r, ang_c = row[:, None] * inv_freq[None, :], col[:, None] * inv_freq[None, :]
    ang = jnp.concatenate([ang_r, ang_r, ang_c, ang_c] * 2, axis=-1)
    return jnp.cos(ang), jnp.sin(ang)


SMALL_SHARDED = ("norm_g", "ffn_conv_w", "cm_b_in", "cm_dw_w", "cm_dw_b", "cm_ln_g", "cm_ln_b", "cm_b_out", "gm_b_in", "gm_ln_g",
                 "gm_ln_b")
SMALL_REPLICATED = ("c_ctx", "ada_b", "ffn_conv_b", "attn_sink", "gm_w_s", "gm_b_s")
BIG = ("ffn_w_up", "ffn_w_down", "cm_w_in", "cm_w_out", "attn_w_qkv", "attn_w_o", "gm_w_in", "gm_w_out")
BIG_AXIS = {"ffn_w_up": 2, "ffn_w_down": 1, "cm_w_in": 2, "cm_w_out": 1, "attn_w_qkv": 2, "attn_w_o": 1, "gm_w_in": 2, "gm_w_out": 1}
WEIGHTS = ("c_ctx", "ada_w", "ada_b", "norm_g", "ffn_w_up", "ffn_conv_w", "ffn_conv_b", "ffn_w_down", "cm_w_in", "cm_b_in",
           "cm_dw_w", "cm_dw_b", "cm_ln_g", "cm_ln_b", "cm_w_out", "cm_b_out", "attn_w_qkv", "attn_sink", "attn_w_o", "gm_w_in",
           "gm_b_in", "gm_ln_g", "gm_ln_b", "gm_w_s", "gm_b_s", "gm_w_out")


def _step(x, c, ctx, target, W, M, V):
    L, D = x.shape[1], x.shape[2]
    C = ctx.shape[1]
    T = L + C
    NL = W["ada_w"].shape[0]
    tm = 256 if C % 256 == 0 else 128
    nl = L // tm
    xi, yi, ci = _my_place()
    chip = 2 * xi + yi
    dev = 4 * xi + 2 * yi + ci
    segs2, segs1 = [(0, L), (L, C)], [(0, L)]
    vec = lambda a: a.reshape(1, -1)

    layer_sets = [[("cm_w_in", 0), ("cm_w_out", 0), ("ffn_w_up", 0), ("ffn_w_down", 0)],
                  [("attn_w_qkv", 0), ("attn_w_o", 0), ("ffn_w_up", 1), ("ffn_w_down", 1)],
                  [("gm_w_in", 0), ("gm_w_out", 0), ("ffn_w_up", 2), ("ffn_w_down", 2)],
                  [("cm_w_in", 1), ("cm_w_out", 1), ("ffn_w_up", 3), ("ffn_w_down", 3)]]
    arrived = {}

    def fetch(k, zero):
        exchanges = []
        for n, i in layer_sets[k]:
            shard = (W[n][i] + zero).astype(MMT)
            whole = list(shard.shape)
            whole[BIG_AXIS[n] - 1] *= N_CHIPS
            exchanges.append(([shard], [SDS(tuple(whole), MMT)], _gather_plan(BIG_AXIS[n] - 1), N_CHIPS - 1))
        lands = _sequencer_exchange(f"fetch_weights_{k}", FETCH_IDS[k], exchanges, _same_core_peers)
        for key, land in zip(layer_sets[k], lands):
            arrived[key] = land[0]
        return arrived[layer_sets[k][0]][0, 0].astype(F32) * 0.0

    def big(n, i, after=None):
        return arrived[(n, i)]

    small_shapes = [W[n].shape for n in SMALL_SHARDED]
    ag1 = _all_gather(_pack([c.reshape(-1)] + [W[n] for n in SMALL_SHARDED]), "gather_small")
    parts = [_unpack(ag1[2 * s], [(D,)] + small_shapes) for s in range(N_CHIPS)]
    c_rows = jnp.stack([_unpack(ag1[d], [(D,)])[0] for d in range(N_DEV)])
    P = {n: _unshard_last(jnp.stack([parts[s][1 + i] for s in range(N_CHIPS)])) for i, n in enumerate(SMALL_SHARDED)}
    for n in SMALL_REPLICATED:
        P[n] = W[n]

    cond = jnp.concatenate([c_rows, W["c_ctx"][None, :], jnp.zeros((2 * SUBLANES - N_DEV - 1, D), F32)], axis=0)
    ncol = W["ada_w"].shape[2]
    ada_b_mine = lax.dynamic_slice_in_dim(W["ada_b"], chip * ncol, ncol, axis=1)[:, None, :]
    mods_mine = _ada_fwd(cond, W["ada_w"], ada_b_mine, "ada_fwd")
    ag2 = _all_gather(mods_mine.reshape(NL * 2 * SUBLANES, ncol), "gather_mods").reshape(N_DEV, NL, 2 * SUBLANES, ncol)
    mods_all = _unshard_last(jnp.stack([ag2[2 * s] for s in range(N_CHIPS)]))
    mod_lat = lax.dynamic_index_in_dim(mods_all, dev, axis=1, keepdims=False).reshape(NL, 6, D)
    mod_ctx = mods_all[:, N_DEV].reshape(NL, 6, D)
    mod2 = jnp.stack([mod_lat, mod_ctx], axis=1)
    mod1 = mod_lat[:, None]

    fetched = fetch(0, mod2[0, 0, 0, 0] * 0.0)
    fetched = fetch(1, fetched)
    zero_d = jnp.zeros((1, D), F32)
    cos, sin = _rope_tables(L)
    nkv = D // HEAD_DIM // Q_PER_KV
    qdim, kvdim = D, nkv * HEAD_DIM

    def ffn_fwd(i, h, mod, rows, segs, tag):
        a2 = _prenorm(h, mod, vec(P["norm_g"][i, 2]), 1, rows, nl, tm, f"pre_ffn_{tag}")
        z0 = _mm(a2, big("ffn_w_up", i, a2), "nn", F32, f"ffn_up_{tag}")
        u = _ffn_gate(z0, P["ffn_conv_w"][i], vec(P["ffn_conv_b"][i]), segs, f"ffn_gate_{tag}")
        f = _mm(u, big("ffn_w_down", i, u), "nn", F32, f"ffn_down_{tag}")
        h_out = _postnorm(h, f, zero_d, mod, vec(P["norm_g"][i, 3]), 5, rows, nl, tm, f"post_ffn_{tag}")
        return h_out, dict(h=h, a2=a2, z0=z0, f=f)

    def ffn_bwd(i, dh, sv, mod, rows, segs, tag, G):
        df, dg2, dgn3, _ = _postnorm_bwd(dh, sv["f"], zero_d, mod, vec(P["norm_g"][i, 3]), 5, rows, nl, tm, f"post_ffn_bwd_{tag}")
        du = _mm(df, big("ffn_w_down", i), "nt", F32, f"ffn_down_dx_{tag}")
        u, dz0, dcw, dcb = _ffn_gate_bwd(sv["z0"], du, P["ffn_conv_w"][i], vec(P["ffn_conv_b"][i]), segs, f"ffn_gate_bwd_{tag}")
        G["ffn_w_down"][i] = _mm(u, df, "tn", MMT, f"ffn_down_dw_{tag}")
        G["ffn_w_up"][i] = _mm(sv["a2"], dz0, "tn", MMT, f"ffn_up_dw_{tag}")
        da2 = _mm(dz0, big("ffn_w_up", i), "nt", F32, f"ffn_up_dx_{tag}")
        dh, dsh2, dsc2, dgn2 = _prenorm_bwd(sv["h"], da2, dh, mod, vec(P["norm_g"][i, 2]), 1, rows, nl, tm, f"pre_ffn_bwd_{tag}")
        G["ffn_conv_w"][i], G["ffn_conv_b"][i] = dcw, dcb[0]
        return dh, (dsh2, dsc2, dg2), (dgn2, dgn3)

    def conformer_fwd(i, j, h, mod, rows, segs, tag):
        a = _prenorm(h, mod, vec(P["norm_g"][i, 0]), 0, rows, nl, tm, f"pre_mix_{tag}")
        p0 = _mm(a, big("cm_w_in", j, a), "nn", F32, f"cm_in_{tag}")
        z2 = _glu_conv(p0, vec(P["cm_b_in"][j]), P["cm_dw_w"][j], vec(P["cm_dw_b"][j]), segs, f"cm_conv_{tag}")
        z4 = _ln_silu(z2, vec(P["cm_ln_g"][j]), vec(P["cm_ln_b"][j]), rows, tm, f"cm_ln_{tag}")
        y = _mm(z4, big("cm_w_out", j, z4), "nn", F32, f"cm_out_{tag}")
        h_out = _postnorm(h, y, vec(P["cm_b_out"][j]), mod, vec(P["norm_g"][i, 1]), 2, rows, nl, tm, f"post_mix_{tag}")
        return h_out, dict(h=h, a=a, p0=p0, z2=z2, z4=z4, y=y)

    def conformer_bwd(i, j, dh, sv, mod, rows, segs, tag, G):
        dy, dg1, dgn1, dbo = _postnorm_bwd(dh, sv["y"], vec(P["cm_b_out"][j]) + zero_d, mod, vec(P["norm_g"][i, 1]), 2, rows, nl,
                                           tm, f"post_mix_bwd_{tag}")
        G["cm_w_out"][j] = _mm(sv["z4"], dy, "tn", MMT, f"cm_out_dw_{tag}")
        dz4 = _mm(dy, big("cm_w_out", j), "nt", F32, f"cm_out_dx_{tag}")
        dz2, dlg, dlb = _ln_silu_bwd(sv["z2"], dz4, vec(P["cm_ln_g"][j]), vec(P["cm_ln_b"][j]), rows, tm, f"cm_ln_bwd_{tag}")
        dpa, dpg, ddw, ddb, dba, dbg = _glu_conv_bwd(sv["p0"], vec(P["cm_b_in"][j]), P["cm_dw_w"][j], dz2, segs, f"cm_conv_bwd_{tag}")
        dp = jnp.concatenate([dpa, dpg], axis=1)
        G["cm_w_in"][j] = _mm(sv["a"], dp, "tn", MMT, f"cm_in_dw_{tag}")
        da = _mm(dp, big("cm_w_in", j), "nt", F32, f"cm_in_dx_{tag}")
        dh, dsh1, dsc1, dgn0 = _prenorm_bwd(sv["h"], da, dh, mod, vec(P["norm_g"][i, 0]), 0, rows, nl, tm, f"pre_mix_bwd_{tag}")
        G["cm_b_out"][j] = jnp.sum(dbo, axis=0)[0]
        G["cm_ln_g"][j], G["cm_ln_b"][j], G["cm_dw_w"][j], G["cm_dw_b"][j] = dlg[0], dlb[0], ddw, ddb[0]
        G["cm_b_in"][j] = jnp.concatenate([dba[0], dbg[0]])
        return dh, (dsh1, dsc1, dg1), (dgn0, dgn1)

    def heads(a, n):
        return a.reshape(a.shape[0], n, HEAD_DIM).transpose(1, 0, 2)

    def unheads(a):
        return a.transpose(1, 0, 2).reshape(a.shape[1], -1)

    G = {n: [None] * W[n].shape[0] for n in WEIGHTS if n not in ("c_ctx", "ada_w", "ada_b", "norm_g")}
    saved = []
    h = jnp.concatenate([x[0], ctx[0]], axis=0)
    h, s_mix = conformer_fwd(0, 0, h, mod2[0], T, segs2, "l0")
    h, s_ffn = ffn_fwd(0, h, mod2[0], T, segs2, "l0")
    saved.append((s_mix, s_ffn))
    fetched = fetch(2, fetched)
    a_all = _prenorm(h, mod2[1], vec(P["norm_g"][1, 0]), 0, T, nl, tm, "pre_mix_l1")
    qkv = _mm(a_all, big("attn_w_qkv", 0, a_all), "nn", F32, "attn_qkv")
    qk_rot, v_lat = _rope(qkv, cos, sin, L, qdim + kvdim, tm, "rope")
    q_h = heads(qk_rot[:, :qdim], nkv * Q_PER_KV).reshape(nkv, Q_PER_KV, L, HEAD_DIM)
    k_h, v_h = heads(qk_rot[:, qdim:], nkv), heads(v_lat, nkv)
    kc_h = heads(qkv[L:, qdim:qdim + kvdim].astype(MMT), nkv)
    vc_h = heads(qkv[L:, qdim + kvdim:].astype(MMT), nkv)
    sink = P["attn_sink"][0]
    o_h, lse = _attn_fwd(q_h, k_h, v_h, kc_h, vc_h, sink, "attn")
    o_nat = unheads(o_h.reshape(nkv * Q_PER_KV, L, HEAD_DIM)).astype(MMT)
    y1 = _mm(o_nat, big("attn_w_o", 0, o_nat), "nn", F32, "attn_out")
    h_in1 = h
    h = _postnorm(h, y1, zero_d, mod1[1], vec(P["norm_g"][1, 1]), 2, L, nl, tm, "post_mix_l1")
    h, s_ffn1 = ffn_fwd(1, h, mod1[1], L, segs1, "lat")
    fetch(3, fetched)
    h_in2 = h
    a_2 = _prenorm(h, mod1[2], vec(P["norm_g"][2, 0]), 0, L, nl, tm, "pre_mix_l2")
    p0_2 = _mm(a_2, big("gm_w_in", 0, a_2), "nn", F32, "gm_in")
    ws_bf = P["gm_w_s"][0].astype(MMT)
    bs_col = P["gm_b_s"][0][:, :, None]
    us = _gmlp_fwd(p0_2, vec(P["gm_b_in"][0]), vec(P["gm_ln_g"][0]), vec(P["gm_ln_b"][0]), ws_bf, bs_col, "gmlp")
    y2 = _mm(us, big("gm_w_out", 0, us), "nn", F32, "gm_out")
    h = _postnorm(h, y2, zero_d, mod1[2], vec(P["norm_g"][2, 1]), 2, L, nl, tm, "post_mix_l2")
    h, s_ffn2 = ffn_fwd(2, h, mod1[2], L, segs1, "lat")
    h, s_mix3 = conformer_fwd(3, 1, h, mod1[3], L, segs1, "l3")
    h, s_ffn3 = ffn_fwd(3, h, mod1[3], L, segs1, "lat")

    loss_mine, dh = _loss_head(h, target[0], tm, "loss_head")

    dmod = [None] * NL
    dgn = [None] * NL

    def finish(i, mix, ffn, gns_mix, gns_ffn):
        dmod[i] = jnp.concatenate(list(mix) + list(ffn), axis=1)
        dgn[i] = jnp.stack([jnp.sum(g, axis=0)[0] for g in (gns_mix[0], gns_mix[1], gns_ffn[0], gns_ffn[1])])

    sent, so_far = {}, {}

    def send(tag, collective_id, tensors):
        exchanges = []
        for n, l in tensors:
            g = G[n][l]
            shard = list(g.shape)
            shard[BIG_AXIS[n] - 1] //= N_CHIPS
            exchanges.append(([g], [SDS((N_CHIPS,) + tuple(shard), g.dtype)], _scatter_plan(BIG_AXIS[n] - 1), N_CHIPS - 1))
        sent[tag] = (tensors, _sequencer_exchange(f"send_grads_{tag}", collective_id, exchanges, _same_core_peers))

    def land(tag):
        tensors, landed = sent[tag]
        mine = [_sum_slots(lands[0], f"sum_chips_{n}_{l}") for (n, l), lands in zip(tensors, landed)]
        theirs = _swap_with_sibling(mine, f"swap_cores_{tag}")
        for (n, l), a, b in zip(tensors, mine, theirs):
            so_far[n] = _adamw_layer(W[n], M[n], V[n], l, [a, b], so_far.get(n), f"adamw_{n}_{l}")

    dh, m_ffn, n_ffn = ffn_bwd(3, dh, s_ffn3, mod1[3], L, segs1, "lat", G)
    dh, m_mix, n_mix = conformer_bwd(3, 1, dh, s_mix3, mod1[3], L, segs1, "l3", G)
    finish(3, m_mix, m_ffn, n_mix, n_ffn)
    send("l3", SEND_IDS[0], [("ffn_w_up", 3), ("ffn_w_down", 3), ("cm_w_in", 1), ("cm_w_out", 1)])

    dh, m_ffn, n_ffn = ffn_bwd(2, dh, s_ffn2, mod1[2], L, segs1, "lat", G)
    dy2, dg1, dgn1, _ = _postnorm_bwd(dh, y2, zero_d, mod1[2], vec(P["norm_g"][2, 1]), 2, L, nl, tm, "post_mix_bwd_l2")
    G["gm_w_out"][0] = _mm(us, dy2, "tn", MMT, "gm_out_dw")
    dus = _mm(dy2, big("gm_w_out", 0), "nt", F32, "gm_out_dx")
    ws_t = jnp.swapaxes(P["gm_w_s"][0], 1, 2).astype(MMT)
    dpre, dbi, dlg, dlb, dws, dbs = _gmlp_bwd(p0_2, dus, vec(P["gm_b_in"][0]), vec(P["gm_ln_g"][0]), vec(P["gm_ln_b"][0]), ws_bf,
                                              ws_t, bs_col, "gmlp_bwd")
    G["gm_w_in"][0] = _mm(a_2, dpre, "tn", MMT, "gm_in_dw")
    da = _mm(dpre, big("gm_w_in", 0), "nt", F32, "gm_in_dx")
    dh, dsh1, dsc1, dgn0 = _prenorm_bwd(h_in2, da, dh, mod1[2], vec(P["norm_g"][2, 0]), 0, L, nl, tm, "pre_mix_bwd_l2")
    G["gm_b_in"][0], G["gm_ln_g"][0], G["gm_ln_b"][0], G["gm_w_s"][0], G["gm_b_s"][0] = dbi[0], dlg[0], dlb[0], dws, dbs[:, :, 0]
    finish(2, (dsh1, dsc1, dg1), m_ffn, (dgn0, dgn1), n_ffn)
    send("l2", SEND_IDS[1], [("ffn_w_up", 2), ("ffn_w_down", 2), ("gm_w_in", 0), ("gm_w_out", 0)])
    land("l3")

    dh, m_ffn, n_ffn = ffn_bwd(1, dh, s_ffn1, mod1[1], L, segs1, "lat", G)
    dy1, dg1, dgn1, _ = _postnorm_bwd(dh, y1, zero_d, mod1[1], vec(P["norm_g"][1, 1]), 2, L, nl, tm, "post_mix_bwd_l1")
    G["attn_w_o"][0] = _mm(o_nat, dy1, "tn", MMT, "attn_out_dw")
    do_nat = _mm(dy1, big("attn_w_o", 0), "nt", MMT, "attn_out_dx")
    do_h = heads(do_nat, nkv * Q_PER_KV).reshape(nkv, Q_PER_KV, L, HEAD_DIM)
    dq_h, dkc_h, dvc_h, dsk = _attn_bwd_q(q_h, k_h, v_h, kc_h, vc_h, sink, o_h, do_h, lse, "attn_bwd_q")
    dk_h, dv_h = _attn_bwd_kv(q_h, k_h, v_h, o_h, do_h, lse, "attn_bwd_kv")
    dqk = jnp.concatenate([unheads(dq_h.reshape(nkv * Q_PER_KV, L, HEAD_DIM)), unheads(dk_h)], axis=1)
    dqkv_lat = _rope_bwd(dqk, unheads(dv_h), cos, sin, tm, "rope_bwd")
    dqkv_ctx = jnp.concatenate([jnp.zeros((C, qdim), MMT), unheads(dkc_h).astype(MMT), unheads(dvc_h).astype(MMT)], axis=1)
    dqkv = jnp.concatenate([dqkv_lat, dqkv_ctx], axis=0)
    G["attn_w_qkv"][0] = _mm(a_all, dqkv, "tn", MMT, "attn_qkv_dw")
    da_all = _mm(dqkv, big("attn_w_qkv", 0), "nt", F32, "attn_qkv_dx")
    dh_all = jnp.concatenate([dh, jnp.zeros((C, D), F32)], axis=0)
    dh, dsh1, dsc1, dgn0 = _prenorm_bwd(h_in1, da_all, dh_all, mod2[1], vec(P["norm_g"][1, 0]), 0, T, nl, tm, "pre_mix_bwd_l1")
    G["attn_sink"][0] = dsk[:, :Q_PER_KV, 0].reshape(-1)
    pad_ctx = lambda a: jnp.concatenate([a, jnp.zeros_like(a)], axis=0)
    finish(1, (dsh1, dsc1, pad_ctx(dg1)), [pad_ctx(a) for a in m_ffn], (dgn0, dgn1), n_ffn)
    send("l1", SEND_IDS[2], [("ffn_w_up", 1), ("ffn_w_down", 1), ("attn_w_qkv", 0), ("attn_w_o", 0)])
    land("l2")

    s_mix0, s_ffn0 = saved[0]
    dh, m_ffn, n_ffn = ffn_bwd(0, dh, s_ffn0, mod2[0], T, segs2, "l0", G)
    send("l0_ffn", SEND_IDS[3], [("ffn_w_up", 0), ("ffn_w_down", 0)])
    land("l1")
    dh, m_mix, n_mix = conformer_bwd(0, 0, dh, s_mix0, mod2[0], T, segs2, "l0", G)
    finish(0, m_mix, m_ffn, n_mix, n_ffn)
    grad_x = dh[:L][None]
    send("l0_mix", SEND_IDS[4], [("cm_w_in", 0), ("cm_w_out", 0)])

    for i in range(2, NL):
        dmod[i] = pad_ctx(dmod[i])
    dmod_all = jnp.stack(dmod).reshape(NL, 2, 6 * D)

    ag3 = _all_gather(dmod_all.reshape(NL * 2, 6 * D), "gather_dmods").reshape(N_DEV, NL, 2, N_CHIPS, ncol)
    dm_cols = lax.dynamic_index_in_dim(ag3, chip, axis=3, keepdims=False)
    dm_lat, dm_ctx = jnp.moveaxis(dm_cols[:, :, 0], 0, 1), jnp.moveaxis(dm_cols[:, :, 1], 0, 1)
    g_ada_w, dsilu = _ada_bwd(cond, W["ada_w"], dm_lat, dm_ctx, "ada_bwd")
    cc = W["c_ctx"]
    sg = jax.nn.sigmoid(cc)
    dcctx_part = jnp.where(ci == 0, 1.0, 0.0) * dsilu[N_DEV] * (sg * (1.0 + cc * (1.0 - sg)))

    Gs = {n: jnp.stack(G[n]) for n in G if n not in BIG}
    Gs["norm_g"] = jnp.stack(dgn)
    Gs["ada_b"] = jnp.sum(dmod_all, axis=1)
    Gs["c_ctx"] = dcctx_part
    small_names = list(SMALL_SHARDED) + list(SMALL_REPLICATED)
    small_full_shapes = [P[n].shape for n in small_names]
    small_pack = _pack([Gs[n] for n in small_names])
    ((ag4,),) = _sequencer_exchange("gather_small_grads", SMALL_GRADS_ID, [
        ([small_pack], [SDS((N_DEV,) + small_pack.shape, F32)], _all_gather_plan, N_DEV - 1)], _all_peers)

    flat2 = lambda a: a.reshape(-1, a.shape[-1])
    res = {}
    outs = _adamw(flat2(W["ada_w"]), flat2(M["ada_w"]), flat2(V["ada_w"]), [flat2(g_ada_w)], "adamw_ada_w")
    res["ada_w"] = tuple(o.reshape(W["ada_w"].shape) for o in outs)

    land("l0_ffn")
    land("l0_mix")
    for n in BIG:
        res[n] = tuple(so_far[n])

    small_sum = _unpack(_sum_slots(ag4, "sum_small_grads"), small_full_shapes)
    g_small = {}
    for n, g in zip(small_names, small_sum):
        g_small[n] = _my_block_last(g, chip) if n in SMALL_SHARDED else g
    packed = [_pack([d[n] for n in small_names]) for d in (W, M, V)]
    outs_small = _adamw(packed[0], packed[1], packed[2], [_pack([g_small[n] for n in small_names])], "adamw_small")
    shard_shapes = [W[n].shape for n in small_names]
    for k, n in enumerate(small_names):
        res[n] = tuple(_unpack(o, shard_shapes)[k] for o in outs_small)

    loss = lax.psum(loss_mine[0, 0], ("x", "y", "c"))
    return (loss, grad_x) + tuple(res[n][k] for k in range(4) for n in WEIGHTS)


def kernel(x, c, ctx, c_ctx, ada_w, ada_b, norm_g, ffn_w_up, ffn_conv_w, ffn_conv_b, ffn_w_down, cm_w_in, cm_b_in, cm_dw_w, cm_dw_b, cm_ln_g, cm_ln_b, cm_w_out, cm_b_out, attn_w_qkv, attn_sink, attn_w_o, gm_w_in, gm_b_in, gm_ln_g, gm_ln_b, gm_w_s, gm_b_s, gm_w_out, loss_target, m_c_ctx, m_ada_w, m_ada_b, m_norm_g, m_ffn_w_up, m_ffn_conv_w, m_ffn_conv_b, m_ffn_w_down, m_cm_w_in, m_cm_b_in, m_cm_dw_w, m_cm_dw_b, m_cm_ln_g, m_cm_ln_b, m_cm_w_out, m_cm_b_out, m_attn_w_qkv, m_attn_sink, m_attn_w_o, m_gm_w_in, m_gm_b_in, m_gm_ln_g, m_gm_ln_b, m_gm_w_s, m_gm_b_s, m_gm_w_out, v_c_ctx, v_ada_w, v_ada_b, v_norm_g, v_ffn_w_up, v_ffn_conv_w, v_ffn_conv_b, v_ffn_w_down, v_cm_w_in, v_cm_b_in, v_cm_dw_w, v_cm_dw_b, v_cm_ln_g, v_cm_ln_b, v_cm_w_out, v_cm_b_out, v_attn_w_qkv, v_attn_sink, v_attn_w_o, v_gm_w_in, v_gm_b_in, v_gm_ln_g, v_gm_ln_b, v_gm_w_s, v_gm_b_s, v_gm_w_out):
    args = locals()
    W = {n: args[n] for n in WEIGHTS}
    M = {n: args["m_" + n] for n in WEIGHTS}
    V = {n: args["v_" + n] for n in WEIGHTS}
    return _step(x, c, ctx, loss_target, W, M, V)
```

```python
import functools
import math

import jax
import jax.numpy as jnp
from jax import lax
from jax.experimental import pallas as pl
from jax.experimental.pallas import tpu as pltpu
from jax.experimental.pallas import tpu_sc as plsc

F32 = jnp.float32
MMT = jnp.bfloat16
SDS = jax.ShapeDtypeStruct
MESH = pl.DeviceIdType.MESH

EPS = 1e-6
HEAD_DIM = 64
Q_PER_KV = 4
ATTN_BLOCK = 128
GRID_W = 64
ROPE_BASE = 10000.0
GMLP_CHUNK = 128
GMLP_GROUP_DIM = 128
CONV_WIDTH = 31
FFN_CONV_WIDTH = 3
NEG = -1e30

ADAM_LR, ADAM_B1, ADAM_B2, ADAM_EPS, ADAM_WD, ADAM_STEP = 0.001, 0.9, 0.999, 1e-08, 0.01, 10

LANES = 128
SUBLANES = 8
VMEM_LIMIT = 52 * 1024 * 1024
CONV_ROWS = 128
N_CHIPS = 4
N_DEV = 8
FETCH_IDS = (1, 2, 3, 4)
SEND_IDS = (5, 6, 7, 8, 9)
SMALL_GRADS_ID = 10


def _cparams(*sem):
    return pltpu.CompilerParams(dimension_semantics=sem if sem else None, vmem_limit_bytes=VMEM_LIMIT)


def _tile(n, cap, mult=LANES):
    best = None
    for d in range(mult, min(n, cap) + 1, mult):
        if n % d == 0:
            best = d
    return best if best is not None else n


def _sum0(v):
    return jnp.sum(v, axis=0, keepdims=True)


def _rms(v):
    r = lax.rsqrt(jnp.mean(v * v, axis=-1, keepdims=True) + EPS)
    return v * r, r


def _sig(v):
    return jax.nn.sigmoid(v)


def _dot(a, b, ca, cb):
    return lax.dot_general(a.astype(MMT), b.astype(MMT), (((ca,), (cb,)), ((), ())), preferred_element_type=F32)


def _mm(a, b, mode, out_dtype, name):
    if mode == "nn":
        (M, K), N = a.shape, b.shape[1]
    elif mode == "nt":
        (M, K), N = a.shape, b.shape[0]
    else:
        (K, M), N = a.shape, b.shape[1]
    tm, tn, tk = _tile(M, 512), _tile(N, 1408), _tile(K, 1536)
    nk = K // tk
    ca, cb = {"nn": (1, 0), "nt": (1, 1), "tn": (0, 0)}[mode]

    def body(a_ref, b_ref, o_ref, acc):
        k = pl.program_id(2)

        @pl.when(k == 0)
        def _():
            acc[...] = jnp.zeros_like(acc)

        acc[...] += _dot(a_ref[...], b_ref[...], ca, cb)

        @pl.when(k == nk - 1)
        def _():
            o_ref[...] = acc[...].astype(o_ref.dtype)

    a_spec = pl.BlockSpec((tk, tm), lambda i, j, k: (k, i)) if mode == "tn" else pl.BlockSpec((tm, tk), lambda i, j, k: (i, k))
    b_spec = pl.BlockSpec((tn, tk), lambda i, j, k: (j, k)) if mode == "nt" else pl.BlockSpec((tk, tn), lambda i, j, k: (k, j))
    return pl.pallas_call(
        body, name=name, grid=(M // tm, N // tn, nk), in_specs=[a_spec, b_spec],
        out_specs=pl.BlockSpec((tm, tn), lambda i, j, k: (i, j)), out_shape=SDS((M, N), out_dtype),
        scratch_shapes=[pltpu.VMEM((tm, tn), F32)], compiler_params=_cparams("parallel", "parallel", "arbitrary"))(a, b)


def _seg_of(nl, nseg):
    return (lambda i: jnp.where(i >= nl, 1, 0)) if nseg == 2 else (lambda i: 0)


def _prenorm(h, mod, gn, which, rows, nl, tm, name):
    D = h.shape[1]
    nseg = mod.shape[0]
    seg = _seg_of(nl, nseg)
    sh_i, sc_i = (0, 1) if which == 0 else (3, 4)

    def body(h_ref, mod_ref, gn_ref, a_ref):
        n, _ = _rms(h_ref[...])
        a_ref[...] = (n * gn_ref[...] * (1.0 + mod_ref[pl.ds(sc_i, 1), :]) + mod_ref[pl.ds(sh_i, 1), :]).astype(a_ref.dtype)

    return pl.pallas_call(
        body, name=name, grid=(rows // tm,),
        in_specs=[pl.BlockSpec((tm, D), lambda i: (i, 0)), pl.BlockSpec((None, 6, D), lambda i: (seg(i), 0, 0)),
                  pl.BlockSpec((1, D), lambda i: (0, 0))],
        out_specs=pl.BlockSpec((tm, D), lambda i: (i, 0)), out_shape=SDS((rows, D), MMT),
        compiler_params=_cparams("parallel"))(h, mod, gn)


def _acc_spec(D, seg):
    return pl.BlockSpec((None, 1, D), lambda i: (seg(i), 0, 0))


def _prenorm_bwd(h, da, dh_in, mod, gn, which, rows, nl, tm, name):
    D = h.shape[1]
    nseg = mod.shape[0]
    seg = _seg_of(nl, nseg)
    sc_i = 1 if which == 0 else 4

    def body(h_ref, da_ref, dhin_ref, mod_ref, gn_ref, dh_ref, dsh_ref, dsc_ref, dgn_ref):
        i = pl.program_id(0)
        first = (i == 0) | (i == nl) if nseg == 2 else (i == 0)

        @pl.when(first)
        def _():
            dsh_ref[...] = jnp.zeros_like(dsh_ref)
            dsc_ref[...] = jnp.zeros_like(dsc_ref)
            dgn_ref[...] = jnp.zeros_like(dgn_ref)

        n, r = _rms(h_ref[...])
        da_v = da_ref[...].astype(F32)
        gn_v = gn_ref[...]
        sc1 = 1.0 + mod_ref[pl.ds(sc_i, 1), :]
        dsh_ref[...] += _sum0(da_v)
        dsc_ref[...] += _sum0(da_v * (n * gn_v))
        dgn_ref[...] += _sum0(da_v * n * sc1)
        dn = da_v * (gn_v * sc1)
        dh_ref[...] = dhin_ref[...] + r * (dn - n * jnp.mean(dn * n, axis=-1, keepdims=True))

    row = pl.BlockSpec((tm, D), lambda i: (i, 0))
    acc = SDS((nseg, 1, D), F32)
    return pl.pallas_call(
        body, name=name, grid=(rows // tm,),
        in_specs=[row, row, row, pl.BlockSpec((None, 6, D), lambda i: (seg(i), 0, 0)), pl.BlockSpec((1, D), lambda i: (0, 0))],
        out_specs=[row, _acc_spec(D, seg), _acc_spec(D, seg), _acc_spec(D, seg)],
        out_shape=[SDS((rows, D), F32), acc, acc, acc], compiler_params=_cparams("arbitrary"))(h, da, dh_in, mod, gn)


def _postnorm(h, y, bias, mod, gn, gate_i, rows, nl, tm, name):
    D = h.shape[1]
    nseg = mod.shape[0]
    seg = _seg_of(nl, nseg)

    def body(h_ref, y_ref, b_ref, mod_ref, gn_ref, o_ref):
        ny, _ = _rms(y_ref[...] + b_ref[...])
        o_ref[...] = h_ref[...] + mod_ref[pl.ds(gate_i, 1), :] * (ny * gn_ref[...])

    row = pl.BlockSpec((tm, D), lambda i: (i, 0))
    vec = pl.BlockSpec((1, D), lambda i: (0, 0))
    return pl.pallas_call(
        body, name=name, grid=(rows // tm,),
        in_specs=[row, row, vec, pl.BlockSpec((None, 6, D), lambda i: (seg(i), 0, 0)), vec],
        out_specs=row, out_shape=SDS((rows, D), F32), compiler_params=_cparams("parallel"))(h, y, bias, mod, gn)


def _postnorm_bwd(dh, y, bias, mod, gn, gate_i, rows, nl, tm, name):
    D = y.shape[1]
    nseg = mod.shape[0]
    seg = _seg_of(nl, nseg)

    def body(dh_ref, y_ref, b_ref, mod_ref, gn_ref, dy_ref, dg_ref, dgn_ref, db_ref):
        i = pl.program_id(0)
        first = (i == 0) | (i == nl) if nseg == 2 else (i == 0)

        @pl.when(first)
        def _():
            dg_ref[...] = jnp.zeros_like(dg_ref)
            dgn_ref[...] = jnp.zeros_like(dgn_ref)
            db_ref[...] = jnp.zeros_like(db_ref)

        ny, ry = _rms(y_ref[...] + b_ref[...])
        g = mod_ref[pl.ds(gate_i, 1), :]
        gn_v = gn_ref[...]
        dh_v = dh_ref[...]
        dg_ref[...] += _sum0(dh_v * (ny * gn_v))
        dgn_ref[...] += _sum0(dh_v * ny * g)
        dny = dh_v * (g * gn_v)
        dy = ry * (dny - ny * jnp.mean(dny * ny, axis=-1, keepdims=True))
        db_ref[...] += _sum0(dy)
        dy_ref[...] = dy.astype(dy_ref.dtype)

    row = pl.BlockSpec((tm, D), lambda i: (i, 0))
    vec = pl.BlockSpec((1, D), lambda i: (0, 0))
    acc = SDS((nseg, 1, D), F32)
    return pl.pallas_call(
        body, name=name, grid=(rows // tm,),
        in_specs=[row, row, vec, pl.BlockSpec((None, 6, D), lambda i: (seg(i), 0, 0)), vec],
        out_specs=[row, _acc_spec(D, seg), _acc_spec(D, seg), _acc_spec(D, seg)],
        out_shape=[SDS((rows, D), MMT), acc, acc, acc], compiler_params=_cparams("arbitrary"))(dh, y, bias, mod, gn)


def _seg_layout(segs, H):
    out, base = [], H
    for s0, n in segs:
        out.append((s0, n, base))
        base += n + H
    return out, base


def _zero_pads(ref, lay, H):
    width = ref.shape[1]
    ref[pl.ds(0, H), :] = jnp.zeros((H, width), ref.dtype)
    for _, n, base in lay:
        ref[pl.ds(base + n, H), :] = jnp.zeros((H, width), ref.dtype)


def _window(ref, base, off, H):
    return ref[pl.ds(base - H + off, CONV_ROWS + 2 * H), :]


def _taps(win, H, offs):
    W = CONV_ROWS + 2 * H
    rolled, out = {}, {}
    for o in offs:
        s = H + o
        b = s % SUBLANES
        if b not in rolled:
            rolled[b] = win if b == 0 else pltpu.roll(win, shift=W - b, axis=0)
        out[o] = rolled[b][s - b:s - b + CONV_ROWS, :]
    return out


def _chunks(lay, fn):
    for s0, n, base in lay:
        def step(r, carry, s0=s0, base=base):
            fn(s0, base, pl.multiple_of(r * CONV_ROWS, CONV_ROWS))
            return carry
        lax.fori_loop(0, n // CONV_ROWS, step, 0)


def _ffn_gate(z0, conv_w, conv_b, segs, name):
    T, F2 = z0.shape
    F = F2 // 2
    tc = _tile(F, 256)
    nF = F // tc
    H = SUBLANES
    lay, srows = _seg_layout(segs, H)
    offs = [-1, 0, 1]

    def body(zg_ref, zv_ref, wg_ref, wv_ref, bg_ref, bv_ref, u_ref, xg, xv):
        _zero_pads(xg, lay, H)
        _zero_pads(xv, lay, H)
        for s0, n, base in lay:
            xg[pl.ds(base, n), :] = zg_ref[pl.ds(s0, n), :]
            xv[pl.ds(base, n), :] = zv_ref[pl.ds(s0, n), :]

        def chunk(s0, base, off):
            tg = _taps(_window(xg, base, off, H), H, offs)
            tv = _taps(_window(xv, base, off, H), H, offs)
            zg = bg_ref[...] + sum(tg[k - 1] * wg_ref[pl.ds(k, 1), :] for k in range(3))
            zv = bv_ref[...] + sum(tv[k - 1] * wv_ref[pl.ds(k, 1), :] for k in range(3))
            u_ref[pl.ds(s0 + off, CONV_ROWS), :] = (zg * _sig(zg) * zv).astype(u_ref.dtype)

        _chunks(lay, chunk)

    colg = lambda r: pl.BlockSpec((r, tc), lambda j: (0, j))
    colv = lambda r: pl.BlockSpec((r, tc), lambda j: (0, j + nF))
    return pl.pallas_call(
        body, name=name, grid=(nF,),
        in_specs=[colg(T), colv(T), colg(3), colv(3), colg(1), colv(1)],
        out_specs=colg(T), out_shape=SDS((T, F), MMT),
        scratch_shapes=[pltpu.VMEM((srows, tc), F32), pltpu.VMEM((srows, tc), F32)],
        compiler_params=_cparams("parallel"))(z0, z0, conv_w, conv_w, conv_b, conv_b)


def _ffn_gate_bwd(z0, du, conv_w, conv_b, segs, name):
    T, F2 = z0.shape
    F = F2 // 2
    tc = _tile(F, 256)
    nF = F // tc
    H = SUBLANES
    lay, srows = _seg_layout(segs, H)
    offs = [-1, 0, 1]

    def body(zo_ref, zt_ref, du_ref, wo_ref, wt_ref, bo_ref, bt_ref, u_ref, dz0_ref, dw_ref, db_ref, xo, xt, dzp):
        own_is_gate = pl.program_id(1) == 0
        for ref in (xo, xt, dzp):
            _zero_pads(ref, lay, H)
        for s0, n, base in lay:
            xo[pl.ds(base, n), :] = zo_ref[pl.ds(s0, n), :]
            xt[pl.ds(base, n), :] = zt_ref[pl.ds(s0, n), :]

        def grads(s0, base, off):
            to = _taps(_window(xo, base, off, H), H, offs)
            tt = _taps(_window(xt, base, off, H), H, offs)
            zo = bo_ref[...] + sum(to[k - 1] * wo_ref[pl.ds(k, 1), :] for k in range(3))
            zt = bt_ref[...] + sum(tt[k - 1] * wt_ref[pl.ds(k, 1), :] for k in range(3))
            so, st = _sig(zo), _sig(zt)
            du_v = du_ref[pl.ds(s0 + off, CONV_ROWS), :]
            d_gate = du_v * zt * (so * (1.0 + zo * (1.0 - so)))
            d_val = du_v * (zt * st)
            dzp[pl.ds(base + off, CONV_ROWS), :] = jnp.where(own_is_gate, d_gate, d_val)

            @pl.when(own_is_gate)
            def _():
                u_ref[pl.ds(s0 + off, CONV_ROWS), :] = (zo * so * zt).astype(u_ref.dtype)

        _chunks(lay, grads)
        dw_ref[...] = jnp.zeros_like(dw_ref)
        db_ref[...] = jnp.zeros_like(db_ref)

        def back(s0, base, off):
            td = _taps(_window(dzp, base, off, H), H, offs)
            tx = _taps(_window(xo, base, off, H), H, offs)
            dz0 = sum(td[1 - k] * wo_ref[pl.ds(k, 1), :] for k in range(3))
            dz0_ref[pl.ds(s0 + off, CONV_ROWS), :] = dz0.astype(dz0_ref.dtype)
            db_ref[...] += _sum0(td[0])
            for k in range(3):
                dw_ref[pl.ds(k, 1), :] += _sum0(td[0] * tx[k - 1])

        _chunks(lay, back)

    own = lambda r: pl.BlockSpec((r, tc), lambda j, hf: (0, hf * nF + j))
    oth = lambda r: pl.BlockSpec((r, tc), lambda j, hf: (0, (1 - hf) * nF + j))
    ucol = pl.BlockSpec((T, tc), lambda j, hf: (0, j))
    return pl.pallas_call(
        body, name=name, grid=(nF, 2),
        in_specs=[own(T), oth(T), ucol, own(3), oth(3), own(1), oth(1)],
        out_specs=[ucol, own(T), own(3), own(1)],
        out_shape=[SDS((T, F), MMT), SDS((T, F2), MMT), SDS((3, F2), F32), SDS((1, F2), F32)],
        scratch_shapes=[pltpu.VMEM((srows, tc), F32)] * 3,
        compiler_params=_cparams("parallel", "arbitrary"))(z0, z0, du, conv_w, conv_w, conv_b, conv_b)


def _glu_conv(p0, b_in, dw_w, dw_b, segs, name):
    T, D2 = p0.shape
    D = D2 // 2
    tc = _tile(D, 256)
    nD = D // tc
    H = 2 * SUBLANES
    half = (CONV_WIDTH - 1) // 2
    lay, srows = _seg_layout(segs, H)
    offs = list(range(-half, half + 1))

    def body(pa_ref, pg_ref, ba_ref, bg_ref, w_ref, b_ref, z2_ref, z1p):
        _zero_pads(z1p, lay, H)

        def glu(s0, base, off):
            rows = pl.ds(s0 + off, CONV_ROWS)
            z1p[pl.ds(base + off, CONV_ROWS), :] = (pa_ref[rows, :] + ba_ref[...]) * _sig(pg_ref[rows, :] + bg_ref[...])

        _chunks(lay, glu)

        def conv(s0, base, off):
            t = _taps(_window(z1p, base, off, H), H, offs)
            acc = b_ref[...] + t[-half] * w_ref[pl.ds(0, 1), :]
            for k in range(1, CONV_WIDTH):
                acc = acc + t[k - half] * w_ref[pl.ds(k, 1), :]
            z2_ref[pl.ds(s0 + off, CONV_ROWS), :] = acc

        _chunks(lay, conv)

    cola = lambda r: pl.BlockSpec((r, tc), lambda j: (0, j))
    colg = lambda r: pl.BlockSpec((r, tc), lambda j: (0, j + nD))
    return pl.pallas_call(
        body, name=name, grid=(nD,),
        in_specs=[cola(T), colg(T), cola(1), colg(1), cola(CONV_WIDTH), cola(1)],
        out_specs=cola(T), out_shape=SDS((T, D), F32), scratch_shapes=[pltpu.VMEM((srows, tc), F32)],
        compiler_params=_cparams("parallel"))(p0, p0, b_in, b_in, dw_w, dw_b)


def _glu_conv_bwd(p0, b_in, dw_w, dz2, segs, name):
    T, D2 = p0.shape
    D = D2 // 2
    tc = _tile(D, 256)
    nD = D // tc
    H = 2 * SUBLANES
    half = (CONV_WIDTH - 1) // 2
    lay, srows = _seg_layout(segs, H)
    offs = list(range(-half, half + 1))

    def body(pa_ref, pg_ref, ba_ref, bg_ref, w_ref, dz2_ref, dpa_ref, dpg_ref, dw_ref, db_ref, dba_ref, dbg_ref, z1p, dzp):
        _zero_pads(z1p, lay, H)
        _zero_pads(dzp, lay, H)
        for s0, n, base in lay:
            dzp[pl.ds(base, n), :] = dz2_ref[pl.ds(s0, n), :]

        def glu(s0, base, off):
            rows = pl.ds(s0 + off, CONV_ROWS)
            z1p[pl.ds(base + off, CONV_ROWS), :] = (pa_ref[rows, :] + ba_ref[...]) * _sig(pg_ref[rows, :] + bg_ref[...])

        _chunks(lay, glu)
        for ref in (dw_ref, db_ref, dba_ref, dbg_ref):
            ref[...] = jnp.zeros_like(ref)

        def back(s0, base, off):
            td = _taps(_window(dzp, base, off, H), H, offs)
            tz = _taps(_window(z1p, base, off, H), H, offs)
            dz1 = td[half] * w_ref[pl.ds(0, 1), :]
            for k in range(1, CONV_WIDTH):
                dz1 = dz1 + td[half - k] * w_ref[pl.ds(k, 1), :]
            db_ref[...] += _sum0(td[0])
            for k in range(CONV_WIDTH):
                dw_ref[pl.ds(k, 1), :] += _sum0(td[0] * tz[k - half])
            rows = pl.ds(s0 + off, CONV_ROWS)
            pa = pa_ref[rows, :] + ba_ref[...]
            sg = _sig(pg_ref[rows, :] + bg_ref[...])
            dpa = dz1 * sg
            dpg = dz1 * pa * (sg * (1.0 - sg))
            dba_ref[...] += _sum0(dpa)
            dbg_ref[...] += _sum0(dpg)
            dpa_ref[rows, :] = dpa.astype(dpa_ref.dtype)
            dpg_ref[rows, :] = dpg.astype(dpg_ref.dtype)

        _chunks(lay, back)

    cola = lambda r: pl.BlockSpec((r, tc), lambda j: (0, j))
    colg = lambda r: pl.BlockSpec((r, tc), lambda j: (0, j + nD))
    return pl.pallas_call(
        body, name=name, grid=(nD,),
        in_specs=[cola(T), colg(T), cola(1), colg(1), cola(CONV_WIDTH), cola(T)],
        out_specs=[cola(T), cola(T), cola(CONV_WIDTH), cola(1), cola(1), cola(1)],
        out_shape=[SDS((T, D), MMT), SDS((T, D), MMT), SDS((CONV_WIDTH, D), F32), SDS((1, D), F32), SDS((1, D), F32),
                   SDS((1, D), F32)],
        scratch_shapes=[pltpu.VMEM((srows, tc), F32)] * 2, compiler_params=_cparams("parallel"))(p0, p0, b_in, b_in, dw_w, dz2)


def _layer_norm_stats(v):
    mu = jnp.mean(v, axis=-1, keepdims=True)
    var = jnp.mean(jnp.square(v - mu), axis=-1, keepdims=True)
    rstd = lax.rsqrt(var + EPS)
    return (v - mu) * rstd, rstd


def _ln_silu(z2, ln_g, ln_b, rows, tm, name):
    D = z2.shape[1]

    def body(z_ref, g_ref, b_ref, o_ref):
        xh, _ = _layer_norm_stats(z_ref[...])
        z3 = xh * g_ref[...] + b_ref[...]
        o_ref[...] = (z3 * _sig(z3)).astype(o_ref.dtype)

    row = pl.BlockSpec((tm, D), lambda i: (i, 0))
    vec = pl.BlockSpec((1, D), lambda i: (0, 0))
    return pl.pallas_call(body, name=name, grid=(rows // tm,), in_specs=[row, vec, vec], out_specs=row,
                          out_shape=SDS((rows, D), MMT), compiler_params=_cparams("parallel"))(z2, ln_g, ln_b)


def _ln_silu_bwd(z2, dz4, ln_g, ln_b, rows, tm, name):
    D = z2.shape[1]

    def body(z_ref, d_ref, g_ref, b_ref, dz_ref, dg_ref, db_ref):
        @pl.when(pl.program_id(0) == 0)
        def _():
            dg_ref[...] = jnp.zeros_like(dg_ref)
            db_ref[...] = jnp.zeros_like(db_ref)

        xh, rstd = _layer_norm_stats(z_ref[...])
        z3 = xh * g_ref[...] + b_ref[...]
        s = _sig(z3)
        dz3 = d_ref[...] * (s * (1.0 + z3 * (1.0 - s)))
        dg_ref[...] += _sum0(dz3 * xh)
        db_ref[...] += _sum0(dz3)
        dxh = dz3 * g_ref[...]
        dz_ref[...] = rstd * (dxh - jnp.mean(dxh, axis=-1, keepdims=True) - xh * jnp.mean(dxh * xh, axis=-1, keepdims=True))

    row = pl.BlockSpec((tm, D), lambda i: (i, 0))
    vec = pl.BlockSpec((1, D), lambda i: (0, 0))
    return pl.pallas_call(body, name=name, grid=(rows // tm,), in_specs=[row, row, vec, vec], out_specs=[row, vec, vec],
                          out_shape=[SDS((rows, D), F32), SDS((1, D), F32), SDS((1, D), F32)],
                          compiler_params=_cparams("arbitrary"))(z2, dz4, ln_g, ln_b)


def _rot_half_pairs(v):
    width = v.shape[1]
    lane = lax.broadcasted_iota(jnp.int32, v.shape, 1)
    return jnp.where((lane % 32) < 16, -pltpu.roll(v, shift=width - 16, axis=1), pltpu.roll(v, shift=16, axis=1))


def _rope(qkv, cos, sin, L, qk, tm, name):
    width = qkv.shape[1]
    kv = width - qk

    def body(x_ref, c_ref, s_ref, qk_ref, v_ref):
        xv = x_ref[:, pl.ds(0, qk)]
        c = jnp.tile(c_ref[...], (1, qk // LANES))
        s = jnp.tile(s_ref[...], (1, qk // LANES))
        qk_ref[...] = (xv * c + _rot_half_pairs(xv) * s).astype(qk_ref.dtype)
        v_ref[...] = x_ref[:, pl.ds(qk, kv)].astype(v_ref.dtype)

    tab = pl.BlockSpec((tm, LANES), lambda i: (i, 0))
    return pl.pallas_call(
        body, name=name, grid=(L // tm,), in_specs=[pl.BlockSpec((tm, width), lambda i: (i, 0)), tab, tab],
        out_specs=[pl.BlockSpec((tm, qk), lambda i: (i, 0)), pl.BlockSpec((tm, kv), lambda i: (i, 0))],
        out_shape=[SDS((L, qk), MMT), SDS((L, kv), MMT)], compiler_params=_cparams("parallel"))(qkv, cos, sin)


def _rope_bwd(dqk, dv, cos, sin, tm, name):
    L, qk = dqk.shape
    kv = dv.shape[1]

    def body(d_ref, dv_ref, c_ref, s_ref, o_ref):
        dv_ = d_ref[...]
        c = jnp.tile(c_ref[...], (1, qk // LANES))
        s = jnp.tile(s_ref[...], (1, qk // LANES))
        o_ref[:, pl.ds(0, qk)] = (dv_ * c - _rot_half_pairs(dv_ * s)).astype(o_ref.dtype)
        o_ref[:, pl.ds(qk, kv)] = dv_ref[...].astype(o_ref.dtype)

    tab = pl.BlockSpec((tm, LANES), lambda i: (i, 0))
    return pl.pallas_call(
        body, name=name, grid=(L // tm,),
        in_specs=[pl.BlockSpec((tm, qk), lambda i: (i, 0)), pl.BlockSpec((tm, kv), lambda i: (i, 0)), tab, tab],
        out_specs=pl.BlockSpec((tm, qk + kv), lambda i: (i, 0)), out_shape=SDS((L, qk + kv), MMT),
        compiler_params=_cparams("parallel"))(dqk, dv, cos, sin)


def _band_specs(nb, width):
    blk = lambda f: pl.BlockSpec((None, ATTN_BLOCK, width), f)
    return [blk(lambda h, n: (h, jnp.maximum(n - 1, 0), 0)), blk(lambda h, n: (h, n, 0)),
            blk(lambda h, n: (h, jnp.minimum(n + 1, nb - 1), 0))]


def _window_mask(n, L):
    qi = lax.broadcasted_iota(jnp.int32, (ATTN_BLOCK, 3 * ATTN_BLOCK), 0)
    kk = lax.broadcasted_iota(jnp.int32, (ATTN_BLOCK, 3 * ATTN_BLOCK), 1)
    key_abs = (n - 1) * ATTN_BLOCK + kk
    return (jnp.abs(qi + ATTN_BLOCK - kk) <= ATTN_BLOCK) & (key_abs >= 0) & (key_abs < L)


def _attn_fwd(q, k, v, kc, vc, sink, name):
    nkv, _, L, hd = q.shape
    C = kc.shape[1]
    nb = L // ATTN_BLOCK
    scale = HEAD_DIM ** -0.5

    def body(sink_ref, q_ref, k0, k1, k2, v0, v1, v2, kc_ref, vc_ref, o_ref, lse_ref):
        hh, n = pl.program_id(0), pl.program_id(1)
        kw = jnp.concatenate([k0[...], k1[...], k2[...]], axis=0)
        vw = jnp.concatenate([v0[...], v1[...], v2[...]], axis=0)
        mask = _window_mask(n, L)
        for g in range(Q_PER_KV):
            qg = q_ref[g]
            sw = jnp.where(mask, _dot(qg, kw, 1, 1) * scale, NEG)
            sc = _dot(qg, kc_ref[...], 1, 1) * scale
            sk = sink_ref[hh * Q_PER_KV + g]
            m = jnp.maximum(jnp.maximum(jnp.max(sw, axis=-1, keepdims=True), jnp.max(sc, axis=-1, keepdims=True)), sk)
            pw, pc = jnp.exp(sw - m), jnp.exp(sc - m)
            den = jnp.sum(pw, axis=-1, keepdims=True) + jnp.sum(pc, axis=-1, keepdims=True) + jnp.exp(sk - m)
            inv = 1.0 / den
            o_ref[g] = _dot(pw * inv, vw, 1, 0) + _dot(pc * inv, vc_ref[...], 1, 0)
            lse_ref[g] = m + jnp.log(den)

    qspec = pl.BlockSpec((None, Q_PER_KV, ATTN_BLOCK, hd), lambda h, n: (h, 0, n, 0))
    cspec = pl.BlockSpec((None, C, hd), lambda h, n: (h, 0, 0))
    return pl.pallas_call(
        body, name=name, grid=(nkv, nb),
        in_specs=[pl.BlockSpec(memory_space=pltpu.SMEM), qspec] + _band_specs(nb, hd) + _band_specs(nb, hd) + [cspec, cspec],
        out_specs=[qspec, pl.BlockSpec((None, Q_PER_KV, ATTN_BLOCK, 1), lambda h, n: (h, 0, n, 0))],
        out_shape=[SDS((nkv, Q_PER_KV, L, hd), F32), SDS((nkv, Q_PER_KV, L, 1), F32)],
        compiler_params=_cparams("parallel", "parallel"))(sink, q, k, k, k, v, v, v, kc, vc)


def _attn_bwd_q(q, k, v, kc, vc, sink, o, do, lse, name):
    nkv, _, L, hd = q.shape
    C = kc.shape[1]
    nb = L // ATTN_BLOCK
    scale = HEAD_DIM ** -0.5

    def body(sink_ref, q_ref, k0, k1, k2, v0, v1, v2, kc_ref, vc_ref, o_ref, do_ref, lse_ref, dq_ref, dkc_ref, dvc_ref, dsk_ref):
        hh, n = pl.program_id(0), pl.program_id(1)

        @pl.when(n == 0)
        def _():
            dkc_ref[...] = jnp.zeros_like(dkc_ref)
            dvc_ref[...] = jnp.zeros_like(dvc_ref)
            dsk_ref[...] = jnp.zeros_like(dsk_ref)

        kw = jnp.concatenate([k0[...], k1[...], k2[...]], axis=0)
        vw = jnp.concatenate([v0[...], v1[...], v2[...]], axis=0)
        mask = _window_mask(n, L)
        for g in range(Q_PER_KV):
            qg, dog, lse_g = q_ref[g], do_ref[g], lse_ref[g]
            delta = jnp.sum(dog.astype(F32) * o_ref[g], axis=-1, keepdims=True)
            pw = jnp.exp(jnp.where(mask, _dot(qg, kw, 1, 1) * scale, NEG) - lse_g)
            pc = jnp.exp(_dot(qg, kc_ref[...], 1, 1) * scale - lse_g)
            dsw = pw * (_dot(dog, vw, 1, 1) - delta)
            dsc = pc * (_dot(dog, vc_ref[...], 1, 1) - delta)
            dq_ref[g] = (_dot(dsw, kw, 1, 0) + _dot(dsc, kc_ref[...], 1, 0)) * scale
            dkc_ref[...] += _dot(dsc, qg, 0, 0) * scale
            dvc_ref[...] += _dot(pc, dog, 0, 0)
            psk = jnp.exp(sink_ref[hh * Q_PER_KV + g] - lse_g)
            dsk_ref[pl.ds(g, 1), :] += jnp.broadcast_to(jnp.sum(-psk * delta, axis=0, keepdims=True), (1, LANES))

    qspec = pl.BlockSpec((None, Q_PER_KV, ATTN_BLOCK, hd), lambda h, n: (h, 0, n, 0))
    lspec = pl.BlockSpec((None, Q_PER_KV, ATTN_BLOCK, 1), lambda h, n: (h, 0, n, 0))
    cspec = pl.BlockSpec((None, C, hd), lambda h, n: (h, 0, 0))
    return pl.pallas_call(
        body, name=name, grid=(nkv, nb),
        in_specs=[pl.BlockSpec(memory_space=pltpu.SMEM), qspec] + _band_specs(nb, hd) + _band_specs(nb, hd)
        + [cspec, cspec, qspec, qspec, lspec],
        out_specs=[qspec, cspec, cspec, pl.BlockSpec((None, SUBLANES, LANES), lambda h, n: (h, 0, 0))],
        out_shape=[SDS((nkv, Q_PER_KV, L, hd), F32), SDS((nkv, C, hd), F32), SDS((nkv, C, hd), F32),
                   SDS((nkv, SUBLANES, LANES), F32)],
        compiler_params=_cparams("parallel", "arbitrary"))(sink, q, k, k, k, v, v, v, kc, vc, o, do, lse)


def _attn_bwd_kv(q, k, v, o, do, lse, name):
    nkv, _, L, hd = q.shape
    nb = L // ATTN_BLOCK
    scale = HEAD_DIM ** -0.5

    def body(q0, q1, q2, do0, do1, do2, o0, o1, o2, l0, l1, l2, k_ref, v_ref, dk_ref, dv_ref):
        j = pl.program_id(1)
        qi = lax.broadcasted_iota(jnp.int32, (ATTN_BLOCK, ATTN_BLOCK), 0)
        kk = lax.broadcasted_iota(jnp.int32, (ATTN_BLOCK, ATTN_BLOCK), 1)
        kj, vj = k_ref[...], v_ref[...]
        dk = jnp.zeros((ATTN_BLOCK, hd), F32)
        dv = jnp.zeros((ATTN_BLOCK, hd), F32)
        for slot, (q_r, do_r, o_r, l_r) in enumerate(((q0, do0, o0, l0), (q1, do1, o1, l1), (q2, do2, o2, l2))):
            n = j - 1 + slot
            ok = (n >= 0) & (n < nb) & (jnp.abs(qi + ATTN_BLOCK - ((2 - slot) * ATTN_BLOCK + kk)) <= ATTN_BLOCK)
            for g in range(Q_PER_KV):
                qg, dog = q_r[g], do_r[g]
                delta = jnp.sum(dog.astype(F32) * o_r[g], axis=-1, keepdims=True)
                p = jnp.exp(jnp.where(ok, _dot(qg, kj, 1, 1) * scale - l_r[g], NEG))
                ds = p * (_dot(dog, vj, 1, 1) - delta)
                dk = dk + _dot(ds, qg, 0, 0) * scale
                dv = dv + _dot(p, dog, 0, 0)
        dk_ref[...] = dk
        dv_ref[...] = dv

    def band(width):
        blk = lambda f: pl.BlockSpec((None, Q_PER_KV, ATTN_BLOCK, width), f)
        return [blk(lambda h, j: (h, 0, jnp.maximum(j - 1, 0), 0)), blk(lambda h, j: (h, 0, j, 0)),
                blk(lambda h, j: (h, 0, jnp.minimum(j + 1, nb - 1), 0))]

    kspec = pl.BlockSpec((None, ATTN_BLOCK, hd), lambda h, j: (h, j, 0))
    return pl.pallas_call(
        body, name=name, grid=(nkv, nb), in_specs=band(hd) + band(hd) + band(hd) + band(1) + [kspec, kspec],
        out_specs=[kspec, kspec], out_shape=[SDS((nkv, L, hd), F32), SDS((nkv, L, hd), F32)],
        compiler_params=_cparams("parallel", "parallel"))(q, q, q, do, do, do, o, o, o, lse, lse, lse, k, v)


_GELU_K = math.sqrt(2.0 / math.pi)


def _gelu(v):
    return 0.5 * v * (1.0 + jnp.tanh(_GELU_K * (v + 0.044715 * (v * v * v))))


def _gelu_grad(v):
    t = jnp.tanh(_GELU_K * (v + 0.044715 * (v * v * v)))
    return 0.5 * (1.0 + t) + 0.5 * v * (1.0 - t * t) * (_GELU_K * (1.0 + 3.0 * 0.044715 * (v * v)))


def _gmlp_fwd(p0, b_in, ln_g, ln_b, w_s, b_s, name):
    L, W2 = p0.shape
    W = W2 // 2
    G = W // GMLP_GROUP_DIM

    def body(p_ref, bi_ref, g_ref, b_ref, ws_ref, bs_ref, o_ref):
        ge = _gelu(p_ref[...] + bi_ref[...])
        xh, _ = _layer_norm_stats(ge[:, W:])
        vln = xh * g_ref[...] + b_ref[...]
        for gi in range(G):
            cols = slice(gi * GMLP_GROUP_DIM, (gi + 1) * GMLP_GROUP_DIM)
            s = _dot(ws_ref[gi], vln[:, cols], 1, 0) + bs_ref[gi]
            o_ref[:, cols] = (ge[:, cols] * s).astype(o_ref.dtype)

    full = lambda shape: pl.BlockSpec(shape, lambda i: (0,) * len(shape))
    return pl.pallas_call(
        body, name=name, grid=(L // GMLP_CHUNK,),
        in_specs=[pl.BlockSpec((GMLP_CHUNK, W2), lambda i: (i, 0)), full((1, W2)), full((1, W)), full((1, W)),
                  full((G, GMLP_CHUNK, GMLP_CHUNK)), full((G, GMLP_CHUNK, 1))],
        out_specs=pl.BlockSpec((GMLP_CHUNK, W), lambda i: (i, 0)), out_shape=SDS((L, W), MMT),
        compiler_params=_cparams("parallel"))(p0, b_in, ln_g, ln_b, w_s, b_s)


def _gmlp_bwd(p0, dus, b_in, ln_g, ln_b, w_s, w_st, b_s, name):
    L, W2 = p0.shape
    W = W2 // 2
    G = W // GMLP_GROUP_DIM

    def body(p_ref, d_ref, bi_ref, g_ref, b_ref, ws_ref, wst_ref, bs_ref, dpre_ref, dbi_ref, dg_ref, db_ref, dws_ref, dbs_ref, dvln):
        @pl.when(pl.program_id(0) == 0)
        def _():
            for ref in (dbi_ref, dg_ref, db_ref, dws_ref, dbs_ref):
                ref[...] = jnp.zeros_like(ref)

        pre = p_ref[...] + bi_ref[...]
        ge = _gelu(pre)
        xh, rstd = _layer_norm_stats(ge[:, W:])
        vln = xh * g_ref[...] + b_ref[...]
        dge_u = []
        for gi in range(G):
            cols = slice(gi * GMLP_GROUP_DIM, (gi + 1) * GMLP_GROUP_DIM)
            vg = vln[:, cols]
            s = _dot(ws_ref[gi], vg, 1, 0) + bs_ref[gi]
            dus_g = d_ref[:, cols]
            dge_u.append(dus_g * s)
            ds = dus_g * ge[:, cols]
            dbs_ref[gi] += jnp.sum(ds, axis=1, keepdims=True)
            dws_ref[gi] += _dot(ds, vg, 1, 1)
            dvln[:, cols] = _dot(wst_ref[gi], ds, 1, 0)
        dv = dvln[...]
        dg_ref[...] += _sum0(dv * xh)
        db_ref[...] += _sum0(dv)
        dxh = dv * g_ref[...]
        dv0 = rstd * (dxh - jnp.mean(dxh, axis=-1, keepdims=True) - xh * jnp.mean(dxh * xh, axis=-1, keepdims=True))
        dpre = jnp.concatenate(dge_u + [dv0], axis=1) * _gelu_grad(pre)
        dbi_ref[...] += _sum0(dpre)
        dpre_ref[...] = dpre.astype(dpre_ref.dtype)

    full = lambda shape: pl.BlockSpec(shape, lambda i: (0,) * len(shape))
    mats = (G, GMLP_CHUNK, GMLP_CHUNK)
    return pl.pallas_call(
        body, name=name, grid=(L // GMLP_CHUNK,),
        in_specs=[pl.BlockSpec((GMLP_CHUNK, W2), lambda i: (i, 0)), pl.BlockSpec((GMLP_CHUNK, W), lambda i: (i, 0)),
                  full((1, W2)), full((1, W)), full((1, W)), full(mats), full(mats), full((G, GMLP_CHUNK, 1))],
        out_specs=[pl.BlockSpec((GMLP_CHUNK, W2), lambda i: (i, 0)), full((1, W2)), full((1, W)), full((1, W)), full(mats),
                   full((G, GMLP_CHUNK, 1))],
        out_shape=[SDS((L, W2), MMT), SDS((1, W2), F32), SDS((1, W), F32), SDS((1, W), F32), SDS(mats, F32),
                   SDS((G, GMLP_CHUNK, 1), F32)],
        scratch_shapes=[pltpu.VMEM((GMLP_CHUNK, W), F32)], compiler_params=_cparams("arbitrary"))(
            p0, dus, b_in, ln_g, ln_b, w_s, w_st, b_s)


def _loss_head(h, target, tm, name):
    L, D = h.shape

    def body(h_ref, t_ref, l_ref, d_ref):
        @pl.when(pl.program_id(0) == 0)
        def _():
            l_ref[...] = jnp.zeros_like(l_ref)

        e = h_ref[...] - t_ref[...]
        l_ref[...] += 0.5 * jnp.sum(jnp.mean(e * e, axis=-1, keepdims=True), axis=0, keepdims=True)
        d_ref[...] = e * (1.0 / D)

    row = pl.BlockSpec((tm, D), lambda i: (i, 0))
    return pl.pallas_call(body, name=name, grid=(L // tm,), in_specs=[row, row],
                          out_specs=[pl.BlockSpec((1, 1), lambda i: (0, 0)), row],
                          out_shape=[SDS((1, 1), F32), SDS((L, D), F32)], compiler_params=_cparams("arbitrary"))(h, target)


def _ada_fwd(cond, ada_w, ada_b, name):
    NL, D, n = ada_w.shape
    tn = _tile(n, 768)

    def body(c_ref, w_ref, b_ref, o_ref):
        cv = c_ref[...]
        o_ref[...] = _dot(cv * _sig(cv), w_ref[...], 1, 0) + b_ref[...]

    return pl.pallas_call(
        body, name=name, grid=(NL, n // tn),
        in_specs=[pl.BlockSpec((2 * SUBLANES, D), lambda i, j: (0, 0)), pl.BlockSpec((None, D, tn), lambda i, j: (i, 0, j)),
                  pl.BlockSpec((None, 1, tn), lambda i, j: (i, 0, j))],
        out_specs=pl.BlockSpec((None, 2 * SUBLANES, tn), lambda i, j: (i, 0, j)), out_shape=SDS((NL, 2 * SUBLANES, n), F32),
        compiler_params=_cparams("parallel", "parallel"))(cond, ada_w, ada_b)


def _ada_bwd(cond, ada_w, dm_lat, dm_ctx, name):
    NL, D, n = ada_w.shape
    tn = _tile(n, 768)

    def body(c_ref, w_ref, dl_ref, dc_ref, dw_ref, ds_ref):
        @pl.when((pl.program_id(0) == 0) & (pl.program_id(1) == 0))
        def _():
            ds_ref[...] = jnp.zeros_like(ds_ref)

        cv = c_ref[...]
        row = lax.broadcasted_iota(jnp.int32, (SUBLANES, tn), 0)
        ctx_rows = jnp.where(row == 0, _sum0(dc_ref[...]), 0.0)
        dm = jnp.concatenate([dl_ref[...], ctx_rows], axis=0)
        dw_ref[...] = _dot(cv * _sig(cv), dm, 0, 0)
        ds_ref[...] += _dot(dm, w_ref[...], 1, 1)

    dspec = pl.BlockSpec((None, SUBLANES, tn), lambda i, j: (i, 0, j))
    return pl.pallas_call(
        body, name=name, grid=(NL, n // tn),
        in_specs=[pl.BlockSpec((2 * SUBLANES, D), lambda i, j: (0, 0)), pl.BlockSpec((None, D, tn), lambda i, j: (i, 0, j)),
                  dspec, dspec],
        out_specs=[pl.BlockSpec((None, D, tn), lambda i, j: (i, 0, j)), pl.BlockSpec((2 * SUBLANES, D), lambda i, j: (0, 0))],
        out_shape=[SDS((NL, D, n), F32), SDS((2 * SUBLANES, D), F32)],
        compiler_params=_cparams("arbitrary", "arbitrary"))(cond, ada_w, dm_lat, dm_ctx)


def _adam_math(w, g, m, v):
    m = ADAM_B1 * m + (1.0 - ADAM_B1) * g
    v = ADAM_B2 * v + (1.0 - ADAM_B2) * jnp.square(g)
    m_hat = m / (1.0 - ADAM_B1 ** ADAM_STEP)
    v_hat = v / (1.0 - ADAM_B2 ** ADAM_STEP)
    return -ADAM_LR * (m_hat / (jnp.sqrt(v_hat) + ADAM_EPS) + ADAM_WD * w), m, v


def _row_tile(rows, cols, elems):
    want = max(SUBLANES, elems // cols)
    best = SUBLANES if rows % SUBLANES == 0 else rows
    for d in range(SUBLANES, min(rows, want) + 1, SUBLANES):
        if rows % d == 0:
            best = d
    return best


def _adamw(w, m, v, parts, name):
    R, C = w.shape
    tr = _row_tile(R, C, 128 * 1024)
    npart = len(parts)

    def body(*refs):
        w_ref, m_ref, v_ref = refs[:3]
        g_ref, d_ref, nm_ref, nv_ref = refs[3 + npart:]
        g = refs[3][...]
        for p_ref in refs[4:3 + npart]:
            g = g + p_ref[...]
        d, nm, nv = _adam_math(w_ref[...], g, m_ref[...], v_ref[...])
        g_ref[...], d_ref[...], nm_ref[...], nv_ref[...] = g, d, nm, nv

    blk = pl.BlockSpec((tr, C), lambda i: (i, 0))
    return pl.pallas_call(body, name=name, grid=(R // tr,), in_specs=[blk] * (3 + npart), out_specs=[blk] * 4,
                          out_shape=[SDS((R, C), F32)] * 4, compiler_params=_cparams("parallel"))(w, m, v, *parts)


def _adamw_layer(w, m, v, layer, parts, prev, name):
    _, R, C = w.shape
    tr = _row_tile(R, C, 128 * 1024)
    npart = len(parts)
    nprev = 0 if prev is None else 4

    def body(*refs):
        w_ref, m_ref, v_ref = refs[:3]
        g_ref, d_ref, nm_ref, nv_ref = refs[3 + npart + nprev:]
        g = refs[3][...]
        for p_ref in refs[4:3 + npart]:
            g = g + p_ref[...]
        d, nm, nv = _adam_math(w_ref[...], g, m_ref[...], v_ref[...])
        g_ref[...], d_ref[...], nm_ref[...], nv_ref[...] = g, d, nm, nv

    stacked = pl.BlockSpec((None, tr, C), lambda i: (layer, i, 0))
    flat = pl.BlockSpec((tr, C), lambda i: (i, 0))
    return pl.pallas_call(
        body, name=name, grid=(R // tr,),
        in_specs=[stacked] * 3 + [flat] * npart + [pl.BlockSpec(memory_space=pl.ANY)] * nprev, out_specs=[stacked] * 4,
        out_shape=[SDS(w.shape, F32)] * 4, input_output_aliases={3 + npart + k: k for k in range(nprev)},
        compiler_params=_cparams("parallel"))(w, m, v, *parts, *(prev or ()))


def _sum_slots(x, name, after=None):
    S, R, C = x.shape
    tr = _row_tile(R, C, 128 * 1024)
    extra = [] if after is None else [after]

    def body(x_ref, *rest):
        o_ref = rest[-1]
        acc = x_ref[0].astype(F32)
        for s in range(1, S):
            acc = acc + x_ref[s].astype(F32)
        o_ref[...] = acc

    return pl.pallas_call(
        body, name=name, grid=(R // tr,),
        in_specs=[pl.BlockSpec((S, tr, C), lambda i: (0, i, 0))] + [pl.BlockSpec(memory_space=pl.ANY)] * len(extra),
        out_specs=pl.BlockSpec((tr, C), lambda i: (i, 0)), out_shape=SDS((R, C), F32),
        compiler_params=_cparams("parallel"))(x, *extra)


def _my_place():
    return lax.axis_index("x"), lax.axis_index("y"), lax.axis_index("c")


def _other_chips(x, y):
    return [(1 - x, y), (x, 1 - y), (1 - x, 1 - y)]


def _all_gather(v, name):
    R, C = v.shape

    def body(v_ref, o_ref, send_sems, recv_sems, local_sem):
        x, y, c = _my_place()
        me = 4 * x + 2 * y + c
        mine = pltpu.make_async_copy(v_ref, o_ref.at[me], local_sem)
        mine.start()
        copies = []
        for flip in range(1, N_DEV):
            fx, fy, fc = (flip >> 2) & 1, (flip >> 1) & 1, flip & 1
            peer = ((x + fx) % 2, (y + fy) % 2, (c + fc) % 2)
            cp = pltpu.make_async_remote_copy(src_ref=v_ref, dst_ref=o_ref.at[me], send_sem=send_sems.at[flip - 1],
                                              recv_sem=recv_sems.at[flip - 1], device_id=peer, device_id_type=MESH)
            cp.start()
            copies.append(cp)
        for cp in copies:
            cp.wait()
        mine.wait()

    return pl.pallas_call(
        body, name=name, in_specs=[pl.BlockSpec(memory_space=pl.ANY)], out_specs=pl.BlockSpec(memory_space=pl.ANY),
        out_shape=SDS((N_DEV, R, C), v.dtype),
        scratch_shapes=[pltpu.SemaphoreType.DMA((N_DEV - 1,)), pltpu.SemaphoreType.DMA((N_DEV - 1,)), pltpu.SemaphoreType.DMA],
        )(v)


def _shard_window(ref, axis, j, size):
    idx = [slice(None)] * len(ref.shape)
    idx[axis] = pl.ds(pl.multiple_of(j * size, SUBLANES), size)
    return ref.at[tuple(idx)]


def _gather_plan(axis):
    def plan(srcs, lands):
        x, y, c = _my_place()
        dst = _shard_window(lands[0], axis, 2 * x + y, srcs[0].shape[axis])
        return [(srcs[0], dst)], [(srcs[0], dst, (px, py, c)) for px, py in _other_chips(x, y)]
    return plan


def _scatter_plan(axis):
    def plan(srcs, lands):
        x, y, c = _my_place()
        j = 2 * x + y
        size = srcs[0].shape[axis] // N_CHIPS
        local = [(_shard_window(srcs[0], axis, j, size), lands[0].at[j])]
        remote = [(_shard_window(srcs[0], axis, 2 * px + py, size), lands[0].at[j], (px, py, c)) for px, py in _other_chips(x, y)]
        return local, remote
    return plan


def _all_gather_plan(srcs, lands):
    x, y, c = _my_place()
    dst = lands[0].at[4 * x + 2 * y + c]
    remote = []
    for flip in range(1, N_DEV):
        fx, fy, fc = (flip >> 2) & 1, (flip >> 1) & 1, flip & 1
        remote.append((srcs[0], dst, ((x + fx) % 2, (y + fy) % 2, (c + fc) % 2)))
    return [(srcs[0], dst)], remote


def _same_core_peers():
    x, y, c = _my_place()
    return [(px, py, c) for px, py in _other_chips(x, y)]


def _all_peers():
    x, y, c = _my_place()
    return [((x + (f >> 2 & 1)) % 2, (y + (f >> 1 & 1)) % 2, (c + (f & 1)) % 2) for f in range(1, N_DEV)]


def _tensorcore_exchange(name, exchanges):
    srcs = [s for e in exchanges for s in e[0]]
    shapes = [s for e in exchanges for s in e[1]]
    ns, nl = len(srcs), len(shapes)
    first = [sum(e[3] for e in exchanges[:i]) for i in range(len(exchanges))]
    ncopy = sum(e[3] for e in exchanges)

    def body(*refs):
        src_refs, land_refs = refs[:ns], refs[ns:ns + nl]
        send_sems, recv_sems, local_sem = refs[ns + nl:]
        si = li = 0
        plans = []
        for e_srcs, e_lands, plan, _ in exchanges:
            plans.append(plan(src_refs[si:si + len(e_srcs)], land_refs[li:li + len(e_lands)]))
            si, li = si + len(e_srcs), li + len(e_lands)
        for local, _ in plans:
            for src, dst in local:
                cp = pltpu.make_async_copy(src, dst, local_sem)
                cp.start()
                cp.wait()
        copies = []
        for i, (_, remote) in enumerate(plans):
            assert len(remote) == exchanges[i][3]
            for k, (src, dst, peer) in enumerate(remote):
                cp = pltpu.make_async_remote_copy(src_ref=src, dst_ref=dst, send_sem=send_sems.at[first[i] + k],
                                                  recv_sem=recv_sems.at[first[i] + k], device_id=peer, device_id_type=MESH)
                cp.start()
                copies.append(cp)
        for cp in copies:
            cp.wait()

    any_spec = pl.BlockSpec(memory_space=pl.ANY)
    res = pl.pallas_call(
        body, name=name, in_specs=[any_spec] * ns, out_specs=[any_spec] * nl, out_shape=list(shapes),
        scratch_shapes=[pltpu.SemaphoreType.DMA((ncopy,)), pltpu.SemaphoreType.DMA((ncopy,)), pltpu.SemaphoreType.DMA])(*srcs)
    out, li = [], 0
    for e in exchanges:
        out.append(list(res[li:li + len(e[1])]))
        li += len(e[1])
    return out


def _sequencer_exchange(name, collective_id, exchanges, peers_fn):
    hbm = pltpu.MemorySpace.HBM
    src_refs = [[jax.new_ref(s, memory_space=hbm) for s in e[0]] for e in exchanges]
    land_refs = [[jax.empty_ref(s, memory_space=hbm) for s in e[1]] for e in exchanges]
    first = [sum(e[3] for e in exchanges[:i]) for i in range(len(exchanges))]
    ncopy = sum(e[3] for e in exchanges)

    @pl.kernel(mesh=plsc.ScalarSubcoreMesh(axis_name="sequencer", num_cores=1), name=name,
               scratch_types=(pltpu.SemaphoreType.DMA((ncopy,)), pltpu.SemaphoreType.DMA((ncopy,)), pltpu.SemaphoreType.DMA),
               compiler_params=pltpu.CompilerParams(collective_id=collective_id))
    def launch(send_sems, recv_sems, local_sem):
        peers = peers_fn()
        barrier = pltpu.get_barrier_semaphore()
        for peer in peers:
            pl.semaphore_signal(barrier, inc=1, device_id=peer, device_id_type=MESH)
        pl.semaphore_wait(barrier, len(peers))
        plans = [e[2](src_refs[i], land_refs[i]) for i, e in enumerate(exchanges)]
        for local, _ in plans:
            for src, dst in local:
                cp = pltpu.make_async_copy(src, dst, local_sem)
                cp.start()
                cp.wait()
        copies = []
        for i, (_, remote) in enumerate(plans):
            assert len(remote) == exchanges[i][3]
            for k, (src, dst, peer) in enumerate(remote):
                cp = pltpu.make_async_remote_copy(src_ref=src, dst_ref=dst, send_sem=send_sems.at[first[i] + k],
                                                  recv_sem=recv_sems.at[first[i] + k], device_id=peer, device_id_type=MESH)
                cp.start()
                copies.append(cp)
        for cp in copies:
            cp.wait()

    launch()
    return [[r[...] for r in refs] for refs in land_refs]


def _swap_with_sibling(parts, name):
    nt = len(parts)

    def body(*refs):
        ins, outs = refs[:nt], refs[nt:2 * nt]
        send_sems, recv_sems = refs[2 * nt:]
        x, y, c = _my_place()
        copies = []
        for t in range(nt):
            cp = pltpu.make_async_remote_copy(src_ref=ins[t], dst_ref=outs[t], send_sem=send_sems.at[t], recv_sem=recv_sems.at[t],
                                              device_id=(x, y, 1 - c), device_id_type=MESH)
            cp.start()
            copies.append(cp)
        for cp in copies:
            cp.wait()

    any_spec = pl.BlockSpec(memory_space=pl.ANY)
    return pl.pallas_call(
        body, name=name, in_specs=[any_spec] * nt, out_specs=[any_spec] * nt, out_shape=[SDS(p.shape, p.dtype) for p in parts],
        scratch_shapes=[pltpu.SemaphoreType.DMA((nt,)), pltpu.SemaphoreType.DMA((nt,))],
        )(*parts)


PACK_COLS = 1024


def _pack(arrays):
    flat = jnp.concatenate([a.reshape(-1) for a in arrays])
    pad = (-flat.shape[0]) % (SUBLANES * PACK_COLS)
    return jnp.pad(flat, (0, pad)).reshape(-1, PACK_COLS)


def _unpack(packed, shapes):
    flat, out, pos = packed.reshape(-1), [], 0
    for shape in shapes:
        n = math.prod(shape)
        out.append(flat[pos:pos + n].reshape(shape))
        pos += n
    return out


def _unshard_last(stacked):
    moved = jnp.moveaxis(stacked, 0, -2)
    return moved.reshape(moved.shape[:-2] + (moved.shape[-2] * moved.shape[-1],))


def _my_block_last(full, j):
    s = full.shape[-1] // N_CHIPS
    return lax.dynamic_index_in_dim(full.reshape(full.shape[:-1] + (N_CHIPS, s)), j, axis=full.ndim - 1, keepdims=False)


def _rope_tables(L):
    rows = L // GRID_W
    row = jnp.repeat(jnp.arange(rows), GRID_W).astype(F32)
    col = jnp.tile(jnp.arange(GRID_W), rows).astype(F32)
    axis_dim = HEAD_DIM // 2
    inv_freq = ROPE_BASE ** (-jnp.arange(0, axis_dim, 2, dtype=F32) / axis_dim)
    ang_r, ang_c = row[:, None] * inv_freq[None, :], col[:, None] * inv_freq[None, :]
    ang = jnp.concatenate([ang_r, ang_r, ang_c, ang_c] * 2, axis=-1)
    return jnp.cos(ang), jnp.sin(ang)


SMALL_SHARDED = ("norm_g", "ffn_conv_w", "cm_b_in", "cm_dw_w", "cm_dw_b", "cm_ln_g", "cm_ln_b", "cm_b_out", "gm_b_in", "gm_ln_g",
                 "gm_ln_b")
SMALL_REPLICATED = ("c_ctx", "ada_b", "ffn_conv_b", "attn_sink", "gm_w_s", "gm_b_s")
BIG = ("ffn_w_up", "ffn_w_down", "cm_w_in", "cm_w_out", "attn_w_qkv", "attn_w_o", "gm_w_in", "gm_w_out")
BIG_AXIS = {"ffn_w_up": 2, "ffn_w_down": 1, "cm_w_in": 2, "cm_w_out": 1, "attn_w_qkv": 2, "attn_w_o": 1, "gm_w_in": 2, "gm_w_out": 1}
WEIGHTS = ("c_ctx", "ada_w", "ada_b", "norm_g", "ffn_w_up", "ffn_conv_w", "ffn_conv_b", "ffn_w_down", "cm_w_in", "cm_b_in",
           "cm_dw_w", "cm_dw_b", "cm_ln_g", "cm_ln_b", "cm_w_out", "cm_b_out", "attn_w_qkv", "attn_sink", "attn_w_o", "gm_w_in",
           "gm_b_in", "gm_ln_g", "gm_ln_b", "gm_w_s", "gm_b_s", "gm_w_out")


def _step(x, c, ctx, target, W, M, V):
    L, D = x.shape[1], x.shape[2]
    C = ctx.shape[1]
    T = L + C
    NL = W["ada_w"].shape[0]
    tm = 256 if C % 256 == 0 else 128
    nl = L // tm
    xi, yi, ci = _my_place()
    chip = 2 * xi + yi
    dev = 4 * xi + 2 * yi + ci
    segs2, segs1 = [(0, L), (L, C)], [(0, L)]
    vec = lambda a: a.reshape(1, -1)

    layer_sets = [[("cm_w_in", 0), ("cm_w_out", 0), ("ffn_w_up", 0), ("ffn_w_down", 0)],
                  [("attn_w_qkv", 0), ("attn_w_o", 0), ("ffn_w_up", 1), ("ffn_w_down", 1)],
                  [("gm_w_in", 0), ("gm_w_out", 0), ("ffn_w_up", 2), ("ffn_w_down", 2)],
                  [("cm_w_in", 1), ("cm_w_out", 1), ("ffn_w_up", 3), ("ffn_w_down", 3)]]
    arrived = {}

    def fetch(keys, zero, sequencer_id=None):
        exchanges = []
        for n, i in keys:
            shard = (W[n][i] + zero).astype(MMT)
            whole = list(shard.shape)
            whole[BIG_AXIS[n] - 1] *= N_CHIPS
            exchanges.append(([shard], [SDS(tuple(whole), MMT)], _gather_plan(BIG_AXIS[n] - 1), N_CHIPS - 1))
        if sequencer_id is None:
            lands = _tensorcore_exchange("fetch_weights_first", exchanges)
        else:
            lands = _sequencer_exchange(f"fetch_weights_{keys[0][0]}_{keys[0][1]}", sequencer_id, exchanges, _same_core_peers)
        for key, land in zip(keys, lands):
            arrived[key] = land[0]

    def big(n, i, after=None):
        return arrived[(n, i)]

    small_shapes = [W[n].shape for n in SMALL_SHARDED]
    ag1 = _all_gather(_pack([c.reshape(-1)] + [W[n] for n in SMALL_SHARDED]), "gather_small")
    parts = [_unpack(ag1[2 * s], [(D,)] + small_shapes) for s in range(N_CHIPS)]
    c_rows = jnp.stack([_unpack(ag1[d], [(D,)])[0] for d in range(N_DEV)])
    P = {n: _unshard_last(jnp.stack([parts[s][1 + i] for s in range(N_CHIPS)])) for i, n in enumerate(SMALL_SHARDED)}
    for n in SMALL_REPLICATED:
        P[n] = W[n]

    cond = jnp.concatenate([c_rows, W["c_ctx"][None, :], jnp.zeros((2 * SUBLANES - N_DEV - 1, D), F32)], axis=0)
    ncol = W["ada_w"].shape[2]
    ada_b_mine = lax.dynamic_slice_in_dim(W["ada_b"], chip * ncol, ncol, axis=1)[:, None, :]
    mods_mine = _ada_fwd(cond, W["ada_w"], ada_b_mine, "ada_fwd")
    ag2 = _all_gather(mods_mine.reshape(NL * 2 * SUBLANES, ncol), "gather_mods").reshape(N_DEV, NL, 2 * SUBLANES, ncol)
    mods_all = _unshard_last(jnp.stack([ag2[2 * s] for s in range(N_CHIPS)]))
    mod_lat = lax.dynamic_index_in_dim(mods_all, dev, axis=1, keepdims=False).reshape(NL, 6, D)
    mod_ctx = mods_all[:, N_DEV].reshape(NL, 6, D)
    mod2 = jnp.stack([mod_lat, mod_ctx], axis=1)
    mod1 = mod_lat[:, None]

    behind_small = mod2[0, 0, 0, 0] * 0.0
    fetch(layer_sets[0] + layer_sets[1] + layer_sets[2] + layer_sets[3], behind_small)
    zero_d = jnp.zeros((1, D), F32)
    cos, sin = _rope_tables(L)
    nkv = D // HEAD_DIM // Q_PER_KV
    qdim, kvdim = D, nkv * HEAD_DIM

    def ffn_fwd(i, h, mod, rows, segs, tag):
        a2 = _prenorm(h, mod, vec(P["norm_g"][i, 2]), 1, rows, nl, tm, f"pre_ffn_{tag}")
        z0 = _mm(a2, big("ffn_w_up", i, a2), "nn", F32, f"ffn_up_{tag}")
        u = _ffn_gate(z0, P["ffn_conv_w"][i], vec(P["ffn_conv_b"][i]), segs, f"ffn_gate_{tag}")
        f = _mm(u, big("ffn_w_down", i, u), "nn", F32, f"ffn_down_{tag}")
        h_out = _postnorm(h, f, zero_d, mod, vec(P["norm_g"][i, 3]), 5, rows, nl, tm, f"post_ffn_{tag}")
        return h_out, dict(h=h, a2=a2, z0=z0, f=f)

    def ffn_bwd(i, dh, sv, mod, rows, segs, tag, G):
        df, dg2, dgn3, _ = _postnorm_bwd(dh, sv["f"], zero_d, mod, vec(P["norm_g"][i, 3]), 5, rows, nl, tm, f"post_ffn_bwd_{tag}")
        du = _mm(df, big("ffn_w_down", i), "nt", F32, f"ffn_down_dx_{tag}")
        u, dz0, dcw, dcb = _ffn_gate_bwd(sv["z0"], du, P["ffn_conv_w"][i], vec(P["ffn_conv_b"][i]), segs, f"ffn_gate_bwd_{tag}")
        G["ffn_w_down"][i] = _mm(u, df, "tn", MMT, f"ffn_down_dw_{tag}")
        G["ffn_w_up"][i] = _mm(sv["a2"], dz0, "tn", MMT, f"ffn_up_dw_{tag}")
        da2 = _mm(dz0, big("ffn_w_up", i), "nt", F32, f"ffn_up_dx_{tag}")
        dh, dsh2, dsc2, dgn2 = _prenorm_bwd(sv["h"], da2, dh, mod, vec(P["norm_g"][i, 2]), 1, rows, nl, tm, f"pre_ffn_bwd_{tag}")
        G["ffn_conv_w"][i], G["ffn_conv_b"][i] = dcw, dcb[0]
        return dh, (dsh2, dsc2, dg2), (dgn2, dgn3)

    def conformer_fwd(i, j, h, mod, rows, segs, tag):
        a = _prenorm(h, mod, vec(P["norm_g"][i, 0]), 0, rows, nl, tm, f"pre_mix_{tag}")
        p0 = _mm(a, big("cm_w_in", j, a), "nn", F32, f"cm_in_{tag}")
        z2 = _glu_conv(p0, vec(P["cm_b_in"][j]), P["cm_dw_w"][j], vec(P["cm_dw_b"][j]), segs, f"cm_conv_{tag}")
        z4 = _ln_silu(z2, vec(P["cm_ln_g"][j]), vec(P["cm_ln_b"][j]), rows, tm, f"cm_ln_{tag}")
        y = _mm(z4, big("cm_w_out", j, z4), "nn", F32, f"cm_out_{tag}")
        h_out = _postnorm(h, y, vec(P["cm_b_out"][j]), mod, vec(P["norm_g"][i, 1]), 2, rows, nl, tm, f"post_mix_{tag}")
        return h_out, dict(h=h, a=a, p0=p0, z2=z2, z4=z4, y=y)

    def conformer_bwd(i, j, dh, sv, mod, rows, segs, tag, G):
        dy, dg1, dgn1, dbo = _postnorm_bwd(dh, sv["y"], vec(P["cm_b_out"][j]) + zero_d, mod, vec(P["norm_g"][i, 1]), 2, rows, nl,
                                           tm, f"post_mix_bwd_{tag}")
        G["cm_w_out"][j] = _mm(sv["z4"], dy, "tn", MMT, f"cm_out_dw_{tag}")
        dz4 = _mm(dy, big("cm_w_out", j), "nt", F32, f"cm_out_dx_{tag}")
        dz2, dlg, dlb = _ln_silu_bwd(sv["z2"], dz4, vec(P["cm_ln_g"][j]), vec(P["cm_ln_b"][j]), rows, tm, f"cm_ln_bwd_{tag}")
        dpa, dpg, ddw, ddb, dba, dbg = _glu_conv_bwd(sv["p0"], vec(P["cm_b_in"][j]), P["cm_dw_w"][j], dz2, segs, f"cm_conv_bwd_{tag}")
        dp = jnp.concatenate([dpa, dpg], axis=1)
        G["cm_w_in"][j] = _mm(sv["a"], dp, "tn", MMT, f"cm_in_dw_{tag}")
        da = _mm(dp, big("cm_w_in", j), "nt", F32, f"cm_in_dx_{tag}")
        dh, dsh1, dsc1, dgn0 = _prenorm_bwd(sv["h"], da, dh, mod, vec(P["norm_g"][i, 0]), 0, rows, nl, tm, f"pre_mix_bwd_{tag}")
        G["cm_b_out"][j] = jnp.sum(dbo, axis=0)[0]
        G["cm_ln_g"][j], G["cm_ln_b"][j], G["cm_dw_w"][j], G["cm_dw_b"][j] = dlg[0], dlb[0], ddw, ddb[0]
        G["cm_b_in"][j] = jnp.concatenate([dba[0], dbg[0]])
        return dh, (dsh1, dsc1, dg1), (dgn0, dgn1)

    def heads(a, n):
        return a.reshape(a.shape[0], n, HEAD_DIM).transpose(1, 0, 2)

    def unheads(a):
        return a.transpose(1, 0, 2).reshape(a.shape[1], -1)

    G = {n: [None] * W[n].shape[0] for n in WEIGHTS if n not in ("c_ctx", "ada_w", "ada_b", "norm_g")}
    saved = []
    h = jnp.concatenate([x[0], ctx[0]], axis=0)
    h, s_mix = conformer_fwd(0, 0, h, mod2[0], T, segs2, "l0")
    h, s_ffn = ffn_fwd(0, h, mod2[0], T, segs2, "l0")
    saved.append((s_mix, s_ffn))
    a_all = _prenorm(h, mod2[1], vec(P["norm_g"][1, 0]), 0, T, nl, tm, "pre_mix_l1")
    qkv = _mm(a_all, big("attn_w_qkv", 0, a_all), "nn", F32, "attn_qkv")
    qk_rot, v_lat = _rope(qkv, cos, sin, L, qdim + kvdim, tm, "rope")
    q_h = heads(qk_rot[:, :qdim], nkv * Q_PER_KV).reshape(nkv, Q_PER_KV, L, HEAD_DIM)
    k_h, v_h = heads(qk_rot[:, qdim:], nkv), heads(v_lat, nkv)
    kc_h = heads(qkv[L:, qdim:qdim + kvdim].astype(MMT), nkv)
    vc_h = heads(qkv[L:, qdim + kvdim:].astype(MMT), nkv)
    sink = P["attn_sink"][0]
    o_h, lse = _attn_fwd(q_h, k_h, v_h, kc_h, vc_h, sink, "attn")
    o_nat = unheads(o_h.reshape(nkv * Q_PER_KV, L, HEAD_DIM)).astype(MMT)
    y1 = _mm(o_nat, big("attn_w_o", 0, o_nat), "nn", F32, "attn_out")
    h_in1 = h
    h = _postnorm(h, y1, zero_d, mod1[1], vec(P["norm_g"][1, 1]), 2, L, nl, tm, "post_mix_l1")
    h, s_ffn1 = ffn_fwd(1, h, mod1[1], L, segs1, "lat")
    h_in2 = h
    a_2 = _prenorm(h, mod1[2], vec(P["norm_g"][2, 0]), 0, L, nl, tm, "pre_mix_l2")
    p0_2 = _mm(a_2, big("gm_w_in", 0, a_2), "nn", F32, "gm_in")
    ws_bf = P["gm_w_s"][0].astype(MMT)
    bs_col = P["gm_b_s"][0][:, :, None]
    us = _gmlp_fwd(p0_2, vec(P["gm_b_in"][0]), vec(P["gm_ln_g"][0]), vec(P["gm_ln_b"][0]), ws_bf, bs_col, "gmlp")
    y2 = _mm(us, big("gm_w_out", 0, us), "nn", F32, "gm_out")
    h = _postnorm(h, y2, zero_d, mod1[2], vec(P["norm_g"][2, 1]), 2, L, nl, tm, "post_mix_l2")
    h, s_ffn2 = ffn_fwd(2, h, mod1[2], L, segs1, "lat")
    h, s_mix3 = conformer_fwd(3, 1, h, mod1[3], L, segs1, "l3")
    h, s_ffn3 = ffn_fwd(3, h, mod1[3], L, segs1, "lat")

    loss_mine, dh = _loss_head(h, target[0], tm, "loss_head")

    dmod = [None] * NL
    dgn = [None] * NL

    def finish(i, mix, ffn, gns_mix, gns_ffn):
        dmod[i] = jnp.concatenate(list(mix) + list(ffn), axis=1)
        dgn[i] = jnp.stack([jnp.sum(g, axis=0)[0] for g in (gns_mix[0], gns_mix[1], gns_ffn[0], gns_ffn[1])])

    sent, so_far = {}, {}

    def send(tag, collective_id, tensors):
        exchanges = []
        for n, l in tensors:
            g = G[n][l]
            shard = list(g.shape)
            shard[BIG_AXIS[n] - 1] //= N_CHIPS
            exchanges.append(([g], [SDS((N_CHIPS,) + tuple(shard), g.dtype)], _scatter_plan(BIG_AXIS[n] - 1), N_CHIPS - 1))
        sent[tag] = (tensors, _sequencer_exchange(f"send_grads_{tag}", collective_id, exchanges, _same_core_peers))
        return sum(G[n][l][0:1, 0:1].astype(F32) for n, l in tensors) * 0.0

    def land(tag, after):
        tensors, landed = sent[tag]
        mine = [_sum_slots(lands[0], f"sum_chips_{n}_{l}", after) for (n, l), lands in zip(tensors, landed)]
        theirs = _swap_with_sibling(mine, f"swap_cores_{tag}")
        for (n, l), a, b in zip(tensors, mine, theirs):
            so_far[n] = _adamw_layer(W[n], M[n], V[n], l, [a, b], so_far.get(n), f"adamw_{n}_{l}")

    dh, m_ffn, n_ffn = ffn_bwd(3, dh, s_ffn3, mod1[3], L, segs1, "lat", G)
    dh, m_mix, n_mix = conformer_bwd(3, 1, dh, s_mix3, mod1[3], L, segs1, "l3", G)
    finish(3, m_mix, m_ffn, n_mix, n_ffn)
    zero_d = zero_d + send("l3", SEND_IDS[0], [("ffn_w_up", 3), ("ffn_w_down", 3), ("cm_w_in", 1), ("cm_w_out", 1)])

    dh, m_ffn, n_ffn = ffn_bwd(2, dh, s_ffn2, mod1[2], L, segs1, "lat", G)
    dy2, dg1, dgn1, _ = _postnorm_bwd(dh, y2, zero_d, mod1[2], vec(P["norm_g"][2, 1]), 2, L, nl, tm, "post_mix_bwd_l2")
    G["gm_w_out"][0] = _mm(us, dy2, "tn", MMT, "gm_out_dw")
    dus = _mm(dy2, big("gm_w_out", 0), "nt", F32, "gm_out_dx")
    ws_t = jnp.swapaxes(P["gm_w_s"][0], 1, 2).astype(MMT)
    dpre, dbi, dlg, dlb, dws, dbs = _gmlp_bwd(p0_2, dus, vec(P["gm_b_in"][0]), vec(P["gm_ln_g"][0]), vec(P["gm_ln_b"][0]), ws_bf,
                                              ws_t, bs_col, "gmlp_bwd")
    G["gm_w_in"][0] = _mm(a_2, dpre, "tn", MMT, "gm_in_dw")
    da = _mm(dpre, big("gm_w_in", 0), "nt", F32, "gm_in_dx")
    dh, dsh1, dsc1, dgn0 = _prenorm_bwd(h_in2, da, dh, mod1[2], vec(P["norm_g"][2, 0]), 0, L, nl, tm, "pre_mix_bwd_l2")
    G["gm_b_in"][0], G["gm_ln_g"][0], G["gm_ln_b"][0], G["gm_w_s"][0], G["gm_b_s"][0] = dbi[0], dlg[0], dlb[0], dws, dbs[:, :, 0]
    finish(2, (dsh1, dsc1, dg1), m_ffn, (dgn0, dgn1), n_ffn)
    zero_d = zero_d + send("l2", SEND_IDS[1], [("ffn_w_up", 2), ("ffn_w_down", 2), ("gm_w_in", 0), ("gm_w_out", 0)])
    land("l3", dh)

    dh, m_ffn, n_ffn = ffn_bwd(1, dh, s_ffn1, mod1[1], L, segs1, "lat", G)
    dy1, dg1, dgn1, _ = _postnorm_bwd(dh, y1, zero_d, mod1[1], vec(P["norm_g"][1, 1]), 2, L, nl, tm, "post_mix_bwd_l1")
    G["attn_w_o"][0] = _mm(o_nat, dy1, "tn", MMT, "attn_out_dw")
    do_nat = _mm(dy1, big("attn_w_o", 0), "nt", MMT, "attn_out_dx")
    do_h = heads(do_nat, nkv * Q_PER_KV).reshape(nkv, Q_PER_KV, L, HEAD_DIM)
    dq_h, dkc_h, dvc_h, dsk = _attn_bwd_q(q_h, k_h, v_h, kc_h, vc_h, sink, o_h, do_h, lse, "attn_bwd_q")
    dk_h, dv_h = _attn_bwd_kv(q_h, k_h, v_h, o_h, do_h, lse, "attn_bwd_kv")
    dqk = jnp.concatenate([unheads(dq_h.reshape(nkv * Q_PER_KV, L, HEAD_DIM)), unheads(dk_h)], axis=1)
    dqkv_lat = _rope_bwd(dqk, unheads(dv_h), cos, sin, tm, "rope_bwd")
    dqkv_ctx = jnp.concatenate([jnp.zeros((C, qdim), MMT), unheads(dkc_h).astype(MMT), unheads(dvc_h).astype(MMT)], axis=1)
    dqkv = jnp.concatenate([dqkv_lat, dqkv_ctx], axis=0)
    G["attn_w_qkv"][0] = _mm(a_all, dqkv, "tn", MMT, "attn_qkv_dw")
    da_all = _mm(dqkv, big("attn_w_qkv", 0), "nt", F32, "attn_qkv_dx")
    dh_all = jnp.concatenate([dh, jnp.zeros((C, D), F32)], axis=0)
    dh, dsh1, dsc1, dgn0 = _prenorm_bwd(h_in1, da_all, dh_all, mod2[1], vec(P["norm_g"][1, 0]), 0, T, nl, tm, "pre_mix_bwd_l1")
    G["attn_sink"][0] = dsk[:, :Q_PER_KV, 0].reshape(-1)
    pad_ctx = lambda a: jnp.concatenate([a, jnp.zeros_like(a)], axis=0)
    finish(1, (dsh1, dsc1, pad_ctx(dg1)), [pad_ctx(a) for a in m_ffn], (dgn0, dgn1), n_ffn)
    zero_d = zero_d + send("l1", SEND_IDS[2], [("ffn_w_up", 1), ("ffn_w_down", 1), ("attn_w_qkv", 0), ("attn_w_o", 0)])
    land("l2", dh)

    s_mix0, s_ffn0 = saved[0]
    dh, m_ffn, n_ffn = ffn_bwd(0, dh, s_ffn0, mod2[0], T, segs2, "l0", G)
    zero_d = zero_d + send("l0_ffn", SEND_IDS[3], [("ffn_w_up", 0), ("ffn_w_down", 0)])
    dh, m_mix, n_mix = conformer_bwd(0, 0, dh, s_mix0, mod2[0], T, segs2, "l0", G)
    land("l1", dh)
    finish(0, m_mix, m_ffn, n_mix, n_ffn)
    grad_x = dh[:L][None]
    sent_l0 = send("l0_mix", SEND_IDS[4], [("cm_w_in", 0), ("cm_w_out", 0)])

    for i in range(2, NL):
        dmod[i] = pad_ctx(dmod[i])
    dmod_all = jnp.stack(dmod).reshape(NL, 2, 6 * D) + sent_l0

    ag3 = _all_gather(dmod_all.reshape(NL * 2, 6 * D), "gather_dmods").reshape(N_DEV, NL, 2, N_CHIPS, ncol)
    dm_cols = lax.dynamic_index_in_dim(ag3, chip, axis=3, keepdims=False)
    dm_lat, dm_ctx = jnp.moveaxis(dm_cols[:, :, 0], 0, 1), jnp.moveaxis(dm_cols[:, :, 1], 0, 1)
    g_ada_w, dsilu = _ada_bwd(cond, W["ada_w"], dm_lat, dm_ctx, "ada_bwd")
    cc = W["c_ctx"]
    sg = jax.nn.sigmoid(cc)
    dcctx_part = jnp.where(ci == 0, 1.0, 0.0) * dsilu[N_DEV] * (sg * (1.0 + cc * (1.0 - sg)))

    Gs = {n: jnp.stack(G[n]) for n in G if n not in BIG}
    Gs["norm_g"] = jnp.stack(dgn)
    Gs["ada_b"] = jnp.sum(dmod_all, axis=1)
    Gs["c_ctx"] = dcctx_part
    small_names = list(SMALL_SHARDED) + list(SMALL_REPLICATED)
    small_full_shapes = [P[n].shape for n in small_names]
    small_pack = _pack([Gs[n] for n in small_names])
    ((ag4,),) = _sequencer_exchange("gather_small_grads", SMALL_GRADS_ID, [
        ([small_pack], [SDS((N_DEV,) + small_pack.shape, F32)], _all_gather_plan, N_DEV - 1)], _all_peers)

    flat2 = lambda a: a.reshape(-1, a.shape[-1])
    res = {}
    outs = _adamw(flat2(W["ada_w"]), flat2(M["ada_w"]), flat2(V["ada_w"]), [flat2(g_ada_w)], "adamw_ada_w")
    res["ada_w"] = tuple(o.reshape(W["ada_w"].shape) for o in outs)

    land("l0_ffn", outs[0])
    land("l0_mix", so_far["ffn_w_up"][0])
    for n in BIG:
        res[n] = tuple(so_far[n])

    small_sum = _unpack(_sum_slots(ag4, "sum_small_grads"), small_full_shapes)
    g_small = {}
    for n, g in zip(small_names, small_sum):
        g_small[n] = _my_block_last(g, chip) if n in SMALL_SHARDED else g
    packed = [_pack([d[n] for n in small_names]) for d in (W, M, V)]
    outs_small = _adamw(packed[0], packed[1], packed[2], [_pack([g_small[n] for n in small_names])], "adamw_small")
    shard_shapes = [W[n].shape for n in small_names]
    for k, n in enumerate(small_names):
        res[n] = tuple(_unpack(o, shard_shapes)[k] for o in outs_small)

    loss = lax.psum(loss_mine[0, 0], ("x", "y", "c"))
    return (loss, grad_x) + tuple(res[n][k] for k in range(4) for n in WEIGHTS)


def kernel(x, c, ctx, c_ctx, ada_w, ada_b, norm_g, ffn_w_up, ffn_conv_w, ffn_conv_b, ffn_w_down, cm_w_in, cm_b_in, cm_dw_w, cm_dw_b, cm_ln_g, cm_ln_b, cm_w_out, cm_b_out, attn_w_qkv, attn_sink, attn_w_o, gm_w_in, gm_b_in, gm_ln_g, gm_ln_b, gm_w_s, gm_b_s, gm_w_out, loss_target, m_c_ctx, m_ada_w, m_ada_b, m_norm_g, m_ffn_w_up, m_ffn_conv_w, m_ffn_conv_b, m_ffn_w_down, m_cm_w_in, m_cm_b_in, m_cm_dw_w, m_cm_dw_b, m_cm_ln_g, m_cm_ln_b, m_cm_w_out, m_cm_b_out, m_attn_w_qkv, m_attn_sink, m_attn_w_o, m_gm_w_in, m_gm_b_in, m_gm_ln_g, m_gm_ln_b, m_gm_w_s, m_gm_b_s, m_gm_w_out, v_c_ctx, v_ada_w, v_ada_b, v_norm_g, v_ffn_w_up, v_ffn_conv_w, v_ffn_conv_b, v_ffn_w_down, v_cm_w_in, v_cm_b_in, v_cm_dw_w, v_cm_dw_b, v_cm_ln_g, v_cm_ln_b, v_cm_w_out, v_cm_b_out, v_attn_w_qkv, v_attn_sink, v_attn_w_o, v_gm_w_in, v_gm_b_in, v_gm_ln_g, v_gm_ln_b, v_gm_w_s, v_gm_b_s, v_gm_w_out):
    args = locals()
    W = {n: args[n] for n in WEIGHTS}
    M = {n: args["m_" + n] for n in WEIGHTS}
    V = {n: args["v_" + n] for n in WEIGHTS}
    return _step(x, c, ctx, loss_target, W, M, V)
```

```python
import functools
import math

import jax
import jax.numpy as jnp
from jax import lax
from jax.experimental import pallas as pl
from jax.experimental.pallas import tpu as pltpu
from jax.experimental.pallas import tpu_sc as plsc

F32 = jnp.float32
MMT = jnp.bfloat16
SDS = jax.ShapeDtypeStruct
MESH = pl.DeviceIdType.MESH

EPS = 1e-6
HEAD_DIM = 64
Q_PER_KV = 4
ATTN_BLOCK = 128
GRID_W = 64
ROPE_BASE = 10000.0
GMLP_CHUNK = 128
GMLP_GROUP_DIM = 128
CONV_WIDTH = 31
FFN_CONV_WIDTH = 3
NEG = -1e30

ADAM_LR, ADAM_B1, ADAM_B2, ADAM_EPS, ADAM_WD, ADAM_STEP = 0.001, 0.9, 0.999, 1e-08, 0.01, 10

LANES = 128
SUBLANES = 8
VMEM_LIMIT = 52 * 1024 * 1024
CONV_ROWS = 128
N_CHIPS = 4
N_DEV = 8
FETCH_IDS = (1, 2, 3, 4)
SEND_IDS = (5, 6, 7, 8, 9)
SMALL_GRADS_ID = 10


def _cparams(*sem):
    return pltpu.CompilerParams(dimension_semantics=sem if sem else None, vmem_limit_bytes=VMEM_LIMIT)


def _tile(n, cap, mult=LANES):
    best = None
    for d in range(mult, min(n, cap) + 1, mult):
        if n % d == 0:
            best = d
    return best if best is not None else n


def _sum0(v):
    return jnp.sum(v, axis=0, keepdims=True)


def _rms(v):
    r = lax.rsqrt(jnp.mean(v * v, axis=-1, keepdims=True) + EPS)
    return v * r, r


def _sig(v):
    return jax.nn.sigmoid(v)


def _dot(a, b, ca, cb):
    return lax.dot_general(a.astype(MMT), b.astype(MMT), (((ca,), (cb,)), ((), ())), preferred_element_type=F32)


def _mm(a, b, mode, out_dtype, name):
    if mode == "nn":
        (M, K), N = a.shape, b.shape[1]
    elif mode == "nt":
        (M, K), N = a.shape, b.shape[0]
    else:
        (K, M), N = a.shape, b.shape[1]
    tm, tn, tk = _tile(M, 512), _tile(N, 1408), _tile(K, 1536)
    nk = K // tk
    ca, cb = {"nn": (1, 0), "nt": (1, 1), "tn": (0, 0)}[mode]

    def body(a_ref, b_ref, o_ref, acc):
        k = pl.program_id(2)

        @pl.when(k == 0)
        def _():
            acc[...] = jnp.zeros_like(acc)

        acc[...] += _dot(a_ref[...], b_ref[...], ca, cb)

        @pl.when(k == nk - 1)
        def _():
            o_ref[...] = acc[...].astype(o_ref.dtype)

    a_spec = pl.BlockSpec((tk, tm), lambda i, j, k: (k, i)) if mode == "tn" else pl.BlockSpec((tm, tk), lambda i, j, k: (i, k))
    b_spec = pl.BlockSpec((tn, tk), lambda i, j, k: (j, k)) if mode == "nt" else pl.BlockSpec((tk, tn), lambda i, j, k: (k, j))
    return pl.pallas_call(
        body, name=name, grid=(M // tm, N // tn, nk), in_specs=[a_spec, b_spec],
        out_specs=pl.BlockSpec((tm, tn), lambda i, j, k: (i, j)), out_shape=SDS((M, N), out_dtype),
        scratch_shapes=[pltpu.VMEM((tm, tn), F32)], compiler_params=_cparams("parallel", "parallel", "arbitrary"))(a, b)


def _seg_of(nl, nseg):
    return (lambda i: jnp.where(i >= nl, 1, 0)) if nseg == 2 else (lambda i: 0)


def _prenorm(h, mod, gn, which, rows, nl, tm, name):
    D = h.shape[1]
    nseg = mod.shape[0]
    seg = _seg_of(nl, nseg)
    sh_i, sc_i = (0, 1) if which == 0 else (3, 4)

    def body(h_ref, mod_ref, gn_ref, a_ref):
        n, _ = _rms(h_ref[...])
        a_ref[...] = (n * gn_ref[...] * (1.0 + mod_ref[pl.ds(sc_i, 1), :]) + mod_ref[pl.ds(sh_i, 1), :]).astype(a_ref.dtype)

    return pl.pallas_call(
        body, name=name, grid=(rows // tm,),
        in_specs=[pl.BlockSpec((tm, D), lambda i: (i, 0)), pl.BlockSpec((None, 6, D), lambda i: (seg(i), 0, 0)),
                  pl.BlockSpec((1, D), lambda i: (0, 0))],
        out_specs=pl.BlockSpec((tm, D), lambda i: (i, 0)), out_shape=SDS((rows, D), MMT),
        compiler_params=_cparams("parallel"))(h, mod, gn)


def _acc_spec(D, seg):
    return pl.BlockSpec((None, 1, D), lambda i: (seg(i), 0, 0))


def _prenorm_bwd(h, da, dh_in, mod, gn, which, rows, nl, tm, name):
    D = h.shape[1]
    nseg = mod.shape[0]
    seg = _seg_of(nl, nseg)
    sc_i = 1 if which == 0 else 4

    def body(h_ref, da_ref, dhin_ref, mod_ref, gn_ref, dh_ref, dsh_ref, dsc_ref, dgn_ref):
        i = pl.program_id(0)
        first = (i == 0) | (i == nl) if nseg == 2 else (i == 0)

        @pl.when(first)
        def _():
            dsh_ref[...] = jnp.zeros_like(dsh_ref)
            dsc_ref[...] = jnp.zeros_like(dsc_ref)
            dgn_ref[...] = jnp.zeros_like(dgn_ref)

        n, r = _rms(h_ref[...])
        da_v = da_ref[...].astype(F32)
        gn_v = gn_ref[...]
        sc1 = 1.0 + mod_ref[pl.ds(sc_i, 1), :]
        dsh_ref[...] += _sum0(da_v)
        dsc_ref[...] += _sum0(da_v * (n * gn_v))
        dgn_ref[...] += _sum0(da_v * n * sc1)
        dn = da_v * (gn_v * sc1)
        dh_ref[...] = dhin_ref[...] + r * (dn - n * jnp.mean(dn * n, axis=-1, keepdims=True))

    row = pl.BlockSpec((tm, D), lambda i: (i, 0))
    acc = SDS((nseg, 1, D), F32)
    return pl.pallas_call(
        body, name=name, grid=(rows // tm,),
        in_specs=[row, row, row, pl.BlockSpec((None, 6, D), lambda i: (seg(i), 0, 0)), pl.BlockSpec((1, D), lambda i: (0, 0))],
        out_specs=[row, _acc_spec(D, seg), _acc_spec(D, seg), _acc_spec(D, seg)],
        out_shape=[SDS((rows, D), F32), acc, acc, acc], compiler_params=_cparams("arbitrary"))(h, da, dh_in, mod, gn)


def _postnorm(h, y, bias, mod, gn, gate_i, rows, nl, tm, name):
    D = h.shape[1]
    nseg = mod.shape[0]
    seg = _seg_of(nl, nseg)

    def body(h_ref, y_ref, b_ref, mod_ref, gn_ref, o_ref):
        ny, _ = _rms(y_ref[...] + b_ref[...])
        o_ref[...] = h_ref[...] + mod_ref[pl.ds(gate_i, 1), :] * (ny * gn_ref[...])

    row = pl.BlockSpec((tm, D), lambda i: (i, 0))
    vec = pl.BlockSpec((1, D), lambda i: (0, 0))
    return pl.pallas_call(
        body, name=name, grid=(rows // tm,),
        in_specs=[row, row, vec, pl.BlockSpec((None, 6, D), lambda i: (seg(i), 0, 0)), vec],
        out_specs=row, out_shape=SDS((rows, D), F32), compiler_params=_cparams("parallel"))(h, y, bias, mod, gn)


def _postnorm_bwd(dh, y, bias, mod, gn, gate_i, rows, nl, tm, name):
    D = y.shape[1]
    nseg = mod.shape[0]
    seg = _seg_of(nl, nseg)

    def body(dh_ref, y_ref, b_ref, mod_ref, gn_ref, dy_ref, dg_ref, dgn_ref, db_ref):
        i = pl.program_id(0)
        first = (i == 0) | (i == nl) if nseg == 2 else (i == 0)

        @pl.when(first)
        def _():
            dg_ref[...] = jnp.zeros_like(dg_ref)
            dgn_ref[...] = jnp.zeros_like(dgn_ref)
            db_ref[...] = jnp.zeros_like(db_ref)

        ny, ry = _rms(y_ref[...] + b_ref[...])
        g = mod_ref[pl.ds(gate_i, 1), :]
        gn_v = gn_ref[...]
        dh_v = dh_ref[...]
        dg_ref[...] += _sum0(dh_v * (ny * gn_v))
        dgn_ref[...] += _sum0(dh_v * ny * g)
        dny = dh_v * (g * gn_v)
        dy = ry * (dny - ny * jnp.mean(dny * ny, axis=-1, keepdims=True))
        db_ref[...] += _sum0(dy)
        dy_ref[...] = dy.astype(dy_ref.dtype)

    row = pl.BlockSpec((tm, D), lambda i: (i, 0))
    vec = pl.BlockSpec((1, D), lambda i: (0, 0))
    acc = SDS((nseg, 1, D), F32)
    return pl.pallas_call(
        body, name=name, grid=(rows // tm,),
        in_specs=[row, row, vec, pl.BlockSpec((None, 6, D), lambda i: (seg(i), 0, 0)), vec],
        out_specs=[row, _acc_spec(D, seg), _acc_spec(D, seg), _acc_spec(D, seg)],
        out_shape=[SDS((rows, D), MMT), acc, acc, acc], compiler_params=_cparams("arbitrary"))(dh, y, bias, mod, gn)


def _seg_layout(segs, H):
    out, base = [], H
    for s0, n in segs:
        out.append((s0, n, base))
        base += n + H
    return out, base


def _zero_pads(ref, lay, H):
    width = ref.shape[1]
    ref[pl.ds(0, H), :] = jnp.zeros((H, width), ref.dtype)
    for _, n, base in lay:
        ref[pl.ds(base + n, H), :] = jnp.zeros((H, width), ref.dtype)


def _window(ref, base, off, H):
    return ref[pl.ds(base - H + off, CONV_ROWS + 2 * H), :]


def _taps(win, H, offs):
    W = CONV_ROWS + 2 * H
    rolled, out = {}, {}
    for o in offs:
        s = H + o
        b = s % SUBLANES
        if b not in rolled:
            rolled[b] = win if b == 0 else pltpu.roll(win, shift=W - b, axis=0)
        out[o] = rolled[b][s - b:s - b + CONV_ROWS, :]
    return out


def _chunks(lay, fn):
    for s0, n, base in lay:
        def step(r, carry, s0=s0, base=base):
            fn(s0, base, pl.multiple_of(r * CONV_ROWS, CONV_ROWS))
            return carry
        lax.fori_loop(0, n // CONV_ROWS, step, 0)


def _ffn_gate(z0, conv_w, conv_b, segs, name):
    T, F2 = z0.shape
    F = F2 // 2
    tc = _tile(F, 256)
    nF = F // tc
    H = SUBLANES
    lay, srows = _seg_layout(segs, H)
    offs = [-1, 0, 1]

    def body(zg_ref, zv_ref, wg_ref, wv_ref, bg_ref, bv_ref, u_ref, xg, xv):
        _zero_pads(xg, lay, H)
        _zero_pads(xv, lay, H)
        for s0, n, base in lay:
            xg[pl.ds(base, n), :] = zg_ref[pl.ds(s0, n), :]
            xv[pl.ds(base, n), :] = zv_ref[pl.ds(s0, n), :]

        def chunk(s0, base, off):
            tg = _taps(_window(xg, base, off, H), H, offs)
            tv = _taps(_window(xv, base, off, H), H, offs)
            zg = bg_ref[...] + sum(tg[k - 1] * wg_ref[pl.ds(k, 1), :] for k in range(3))
            zv = bv_ref[...] + sum(tv[k - 1] * wv_ref[pl.ds(k, 1), :] for k in range(3))
            u_ref[pl.ds(s0 + off, CONV_ROWS), :] = (zg * _sig(zg) * zv).astype(u_ref.dtype)

        _chunks(lay, chunk)

    colg = lambda r: pl.BlockSpec((r, tc), lambda j: (0, j))
    colv = lambda r: pl.BlockSpec((r, tc), lambda j: (0, j + nF))
    return pl.pallas_call(
        body, name=name, grid=(nF,),
        in_specs=[colg(T), colv(T), colg(3), colv(3), colg(1), colv(1)],
        out_specs=colg(T), out_shape=SDS((T, F), MMT),
        scratch_shapes=[pltpu.VMEM((srows, tc), F32), pltpu.VMEM((srows, tc), F32)],
        compiler_params=_cparams("parallel"))(z0, z0, conv_w, conv_w, conv_b, conv_b)


def _ffn_gate_bwd(z0, du, conv_w, conv_b, segs, name):
    T, F2 = z0.shape
    F = F2 // 2
    tc = _tile(F, 256)
    nF = F // tc
    H = SUBLANES
    lay, srows = _seg_layout(segs, H)
    offs = [-1, 0, 1]

    def body(zo_ref, zt_ref, du_ref, wo_ref, wt_ref, bo_ref, bt_ref, u_ref, dz0_ref, dw_ref, db_ref, xo, xt, dzp):
        own_is_gate = pl.program_id(1) == 0
        for ref in (xo, xt, dzp):
            _zero_pads(ref, lay, H)
        for s0, n, base in lay:
            xo[pl.ds(base, n), :] = zo_ref[pl.ds(s0, n), :]
            xt[pl.ds(base, n), :] = zt_ref[pl.ds(s0, n), :]

        def grads(s0, base, off):
            to = _taps(_window(xo, base, off, H), H, offs)
            tt = _taps(_window(xt, base, off, H), H, offs)
            zo = bo_ref[...] + sum(to[k - 1] * wo_ref[pl.ds(k, 1), :] for k in range(3))
            zt = bt_ref[...] + sum(tt[k - 1] * wt_ref[pl.ds(k, 1), :] for k in range(3))
            so, st = _sig(zo), _sig(zt)
            du_v = du_ref[pl.ds(s0 + off, CONV_ROWS), :]
            d_gate = du_v * zt * (so * (1.0 + zo * (1.0 - so)))
            d_val = du_v * (zt * st)
            dzp[pl.ds(base + off, CONV_ROWS), :] = jnp.where(own_is_gate, d_gate, d_val)

            @pl.when(own_is_gate)
            def _():
                u_ref[pl.ds(s0 + off, CONV_ROWS), :] = (zo * so * zt).astype(u_ref.dtype)

        _chunks(lay, grads)
        dw_ref[...] = jnp.zeros_like(dw_ref)
        db_ref[...] = jnp.zeros_like(db_ref)

        def back(s0, base, off):
            td = _taps(_window(dzp, base, off, H), H, offs)
            tx = _taps(_window(xo, base, off, H), H, offs)
            dz0 = sum(td[1 - k] * wo_ref[pl.ds(k, 1), :] for k in range(3))
            dz0_ref[pl.ds(s0 + off, CONV_ROWS), :] = dz0.astype(dz0_ref.dtype)
            db_ref[...] += _sum0(td[0])
            for k in range(3):
                dw_ref[pl.ds(k, 1), :] += _sum0(td[0] * tx[k - 1])

        _chunks(lay, back)

    own = lambda r: pl.BlockSpec((r, tc), lambda j, hf: (0, hf * nF + j))
    oth = lambda r: pl.BlockSpec((r, tc), lambda j, hf: (0, (1 - hf) * nF + j))
    ucol = pl.BlockSpec((T, tc), lambda j, hf: (0, j))
    return pl.pallas_call(
        body, name=name, grid=(nF, 2),
        in_specs=[own(T), oth(T), ucol, own(3), oth(3), own(1), oth(1)],
        out_specs=[ucol, own(T), own(3), own(1)],
        out_shape=[SDS((T, F), MMT), SDS((T, F2), MMT), SDS((3, F2), F32), SDS((1, F2), F32)],
        scratch_shapes=[pltpu.VMEM((srows, tc), F32)] * 3,
        compiler_params=_cparams("parallel", "arbitrary"))(z0, z0, du, conv_w, conv_w, conv_b, conv_b)


def _glu_conv(p0, b_in, dw_w, dw_b, segs, name):
    T, D2 = p0.shape
    D = D2 // 2
    tc = _tile(D, 256)
    nD = D // tc
    H = 2 * SUBLANES
    half = (CONV_WIDTH - 1) // 2
    lay, srows = _seg_layout(segs, H)
    offs = list(range(-half, half + 1))

    def body(pa_ref, pg_ref, ba_ref, bg_ref, w_ref, b_ref, z2_ref, z1p):
        _zero_pads(z1p, lay, H)

        def glu(s0, base, off):
            rows = pl.ds(s0 + off, CONV_ROWS)
            z1p[pl.ds(base + off, CONV_ROWS), :] = (pa_ref[rows, :] + ba_ref[...]) * _sig(pg_ref[rows, :] + bg_ref[...])

        _chunks(lay, glu)

        def conv(s0, base, off):
            t = _taps(_window(z1p, base, off, H), H, offs)
            acc = b_ref[...] + t[-half] * w_ref[pl.ds(0, 1), :]
            for k in range(1, CONV_WIDTH):
                acc = acc + t[k - half] * w_ref[pl.ds(k, 1), :]
            z2_ref[pl.ds(s0 + off, CONV_ROWS), :] = acc

        _chunks(lay, conv)

    cola = lambda r: pl.BlockSpec((r, tc), lambda j: (0, j))
    colg = lambda r: pl.BlockSpec((r, tc), lambda j: (0, j + nD))
    return pl.pallas_call(
        body, name=name, grid=(nD,),
        in_specs=[cola(T), colg(T), cola(1), colg(1), cola(CONV_WIDTH), cola(1)],
        out_specs=cola(T), out_shape=SDS((T, D), F32), scratch_shapes=[pltpu.VMEM((srows, tc), F32)],
        compiler_params=_cparams("parallel"))(p0, p0, b_in, b_in, dw_w, dw_b)


def _glu_conv_bwd(p0, b_in, dw_w, dz2, segs, name):
    T, D2 = p0.shape
    D = D2 // 2
    tc = _tile(D, 256)
    nD = D // tc
    H = 2 * SUBLANES
    half = (CONV_WIDTH - 1) // 2
    lay, srows = _seg_layout(segs, H)
    offs = list(range(-half, half + 1))

    def body(pa_ref, pg_ref, ba_ref, bg_ref, w_ref, dz2_ref, dpa_ref, dpg_ref, dw_ref, db_ref, dba_ref, dbg_ref, z1p, dzp):
        _zero_pads(z1p, lay, H)
        _zero_pads(dzp, lay, H)
        for s0, n, base in lay:
            dzp[pl.ds(base, n), :] = dz2_ref[pl.ds(s0, n), :]

        def glu(s0, base, off):
            rows = pl.ds(s0 + off, CONV_ROWS)
            z1p[pl.ds(base + off, CONV_ROWS), :] = (pa_ref[rows, :] + ba_ref[...]) * _sig(pg_ref[rows, :] + bg_ref[...])

        _chunks(lay, glu)
        for ref in (dw_ref, db_ref, dba_ref, dbg_ref):
            ref[...] = jnp.zeros_like(ref)

        def back(s0, base, off):
            td = _taps(_window(dzp, base, off, H), H, offs)
            tz = _taps(_window(z1p, base, off, H), H, offs)
            dz1 = td[half] * w_ref[pl.ds(0, 1), :]
            for k in range(1, CONV_WIDTH):
                dz1 = dz1 + td[half - k] * w_ref[pl.ds(k, 1), :]
            db_ref[...] += _sum0(td[0])
            for k in range(CONV_WIDTH):
                dw_ref[pl.ds(k, 1), :] += _sum0(td[0] * tz[k - half])
            rows = pl.ds(s0 + off, CONV_ROWS)
            pa = pa_ref[rows, :] + ba_ref[...]
            sg = _sig(pg_ref[rows, :] + bg_ref[...])
            dpa = dz1 * sg
            dpg = dz1 * pa * (sg * (1.0 - sg))
            dba_ref[...] += _sum0(dpa)
            dbg_ref[...] += _sum0(dpg)
            dpa_ref[rows, :] = dpa.astype(dpa_ref.dtype)
            dpg_ref[rows, :] = dpg.astype(dpg_ref.dtype)

        _chunks(lay, back)

    cola = lambda r: pl.BlockSpec((r, tc), lambda j: (0, j))
    colg = lambda r: pl.BlockSpec((r, tc), lambda j: (0, j + nD))
    return pl.pallas_call(
        body, name=name, grid=(nD,),
        in_specs=[cola(T), colg(T), cola(1), colg(1), cola(CONV_WIDTH), cola(T)],
        out_specs=[cola(T), cola(T), cola(CONV_WIDTH), cola(1), cola(1), cola(1)],
        out_shape=[SDS((T, D), MMT), SDS((T, D), MMT), SDS((CONV_WIDTH, D), F32), SDS((1, D), F32), SDS((1, D), F32),
                   SDS((1, D), F32)],
        scratch_shapes=[pltpu.VMEM((srows, tc), F32)] * 2, compiler_params=_cparams("parallel"))(p0, p0, b_in, b_in, dw_w, dz2)


def _layer_norm_stats(v):
    mu = jnp.mean(v, axis=-1, keepdims=True)
    var = jnp.mean(jnp.square(v - mu), axis=-1, keepdims=True)
    rstd = lax.rsqrt(var + EPS)
    return (v - mu) * rstd, rstd


def _ln_silu(z2, ln_g, ln_b, rows, tm, name):
    D = z2.shape[1]

    def body(z_ref, g_ref, b_ref, o_ref):
        xh, _ = _layer_norm_stats(z_ref[...])
        z3 = xh * g_ref[...] + b_ref[...]
        o_ref[...] = (z3 * _sig(z3)).astype(o_ref.dtype)

    row = pl.BlockSpec((tm, D), lambda i: (i, 0))
    vec = pl.BlockSpec((1, D), lambda i: (0, 0))
    return pl.pallas_call(body, name=name, grid=(rows // tm,), in_specs=[row, vec, vec], out_specs=row,
                          out_shape=SDS((rows, D), MMT), compiler_params=_cparams("parallel"))(z2, ln_g, ln_b)


def _ln_silu_bwd(z2, dz4, ln_g, ln_b, rows, tm, name):
    D = z2.shape[1]

    def body(z_ref, d_ref, g_ref, b_ref, dz_ref, dg_ref, db_ref):
        @pl.when(pl.program_id(0) == 0)
        def _():
            dg_ref[...] = jnp.zeros_like(dg_ref)
            db_ref[...] = jnp.zeros_like(db_ref)

        xh, rstd = _layer_norm_stats(z_ref[...])
        z3 = xh * g_ref[...] + b_ref[...]
        s = _sig(z3)
        dz3 = d_ref[...] * (s * (1.0 + z3 * (1.0 - s)))
        dg_ref[...] += _sum0(dz3 * xh)
        db_ref[...] += _sum0(dz3)
        dxh = dz3 * g_ref[...]
        dz_ref[...] = rstd * (dxh - jnp.mean(dxh, axis=-1, keepdims=True) - xh * jnp.mean(dxh * xh, axis=-1, keepdims=True))

    row = pl.BlockSpec((tm, D), lambda i: (i, 0))
    vec = pl.BlockSpec((1, D), lambda i: (0, 0))
    return pl.pallas_call(body, name=name, grid=(rows // tm,), in_specs=[row, row, vec, vec], out_specs=[row, vec, vec],
                          out_shape=[SDS((rows, D), F32), SDS((1, D), F32), SDS((1, D), F32)],
                          compiler_params=_cparams("arbitrary"))(z2, dz4, ln_g, ln_b)


def _rot_half_pairs(v):
    width = v.shape[1]
    lane = lax.broadcasted_iota(jnp.int32, v.shape, 1)
    return jnp.where((lane % 32) < 16, -pltpu.roll(v, shift=width - 16, axis=1), pltpu.roll(v, shift=16, axis=1))


def _rope(qkv, cos, sin, L, qk, tm, name):
    width = qkv.shape[1]
    kv = width - qk

    def body(x_ref, c_ref, s_ref, qk_ref, v_ref):
        xv = x_ref[:, pl.ds(0, qk)]
        c = jnp.tile(c_ref[...], (1, qk // LANES))
        s = jnp.tile(s_ref[...], (1, qk // LANES))
        qk_ref[...] = (xv * c + _rot_half_pairs(xv) * s).astype(qk_ref.dtype)
        v_ref[...] = x_ref[:, pl.ds(qk, kv)].astype(v_ref.dtype)

    tab = pl.BlockSpec((tm, LANES), lambda i: (i, 0))
    return pl.pallas_call(
        body, name=name, grid=(L // tm,), in_specs=[pl.BlockSpec((tm, width), lambda i: (i, 0)), tab, tab],
        out_specs=[pl.BlockSpec((tm, qk), lambda i: (i, 0)), pl.BlockSpec((tm, kv), lambda i: (i, 0))],
        out_shape=[SDS((L, qk), MMT), SDS((L, kv), MMT)], compiler_params=_cparams("parallel"))(qkv, cos, sin)


def _rope_bwd(dqk, dv, cos, sin, tm, name):
    L, qk = dqk.shape
    kv = dv.shape[1]

    def body(d_ref, dv_ref, c_ref, s_ref, o_ref):
        dv_ = d_ref[...]
        c = jnp.tile(c_ref[...], (1, qk // LANES))
        s = jnp.tile(s_ref[...], (1, qk // LANES))
        o_ref[:, pl.ds(0, qk)] = (dv_ * c - _rot_half_pairs(dv_ * s)).astype(o_ref.dtype)
        o_ref[:, pl.ds(qk, kv)] = dv_ref[...].astype(o_ref.dtype)

    tab = pl.BlockSpec((tm, LANES), lambda i: (i, 0))
    return pl.pallas_call(
        body, name=name, grid=(L // tm,),
        in_specs=[pl.BlockSpec((tm, qk), lambda i: (i, 0)), pl.BlockSpec((tm, kv), lambda i: (i, 0)), tab, tab],
        out_specs=pl.BlockSpec((tm, qk + kv), lambda i: (i, 0)), out_shape=SDS((L, qk + kv), MMT),
        compiler_params=_cparams("parallel"))(dqk, dv, cos, sin)


def _band_specs(nb, width):
    blk = lambda f: pl.BlockSpec((None, ATTN_BLOCK, width), f)
    return [blk(lambda h, n: (h, jnp.maximum(n - 1, 0), 0)), blk(lambda h, n: (h, n, 0)),
            blk(lambda h, n: (h, jnp.minimum(n + 1, nb - 1), 0))]


def _window_mask(n, L):
    qi = lax.broadcasted_iota(jnp.int32, (ATTN_BLOCK, 3 * ATTN_BLOCK), 0)
    kk = lax.broadcasted_iota(jnp.int32, (ATTN_BLOCK, 3 * ATTN_BLOCK), 1)
    key_abs = (n - 1) * ATTN_BLOCK + kk
    return (jnp.abs(qi + ATTN_BLOCK - kk) <= ATTN_BLOCK) & (key_abs >= 0) & (key_abs < L)


def _attn_fwd(q, k, v, kc, vc, sink, name):
    nkv, _, L, hd = q.shape
    C = kc.shape[1]
    nb = L // ATTN_BLOCK
    scale = HEAD_DIM ** -0.5

    def body(sink_ref, q_ref, k0, k1, k2, v0, v1, v2, kc_ref, vc_ref, o_ref, lse_ref):
        hh, n = pl.program_id(0), pl.program_id(1)
        kw = jnp.concatenate([k0[...], k1[...], k2[...]], axis=0)
        vw = jnp.concatenate([v0[...], v1[...], v2[...]], axis=0)
        mask = _window_mask(n, L)
        for g in range(Q_PER_KV):
            qg = q_ref[g]
            sw = jnp.where(mask, _dot(qg, kw, 1, 1) * scale, NEG)
            sc = _dot(qg, kc_ref[...], 1, 1) * scale
            sk = sink_ref[hh * Q_PER_KV + g]
            m = jnp.maximum(jnp.maximum(jnp.max(sw, axis=-1, keepdims=True), jnp.max(sc, axis=-1, keepdims=True)), sk)
            pw, pc = jnp.exp(sw - m), jnp.exp(sc - m)
            den = jnp.sum(pw, axis=-1, keepdims=True) + jnp.sum(pc, axis=-1, keepdims=True) + jnp.exp(sk - m)
            inv = 1.0 / den
            o_ref[g] = _dot(pw * inv, vw, 1, 0) + _dot(pc * inv, vc_ref[...], 1, 0)
            lse_ref[g] = m + jnp.log(den)

    qspec = pl.BlockSpec((None, Q_PER_KV, ATTN_BLOCK, hd), lambda h, n: (h, 0, n, 0))
    cspec = pl.BlockSpec((None, C, hd), lambda h, n: (h, 0, 0))
    return pl.pallas_call(
        body, name=name, grid=(nkv, nb),
        in_specs=[pl.BlockSpec(memory_space=pltpu.SMEM), qspec] + _band_specs(nb, hd) + _band_specs(nb, hd) + [cspec, cspec],
        out_specs=[qspec, pl.BlockSpec((None, Q_PER_KV, ATTN_BLOCK, 1), lambda h, n: (h, 0, n, 0))],
        out_shape=[SDS((nkv, Q_PER_KV, L, hd), F32), SDS((nkv, Q_PER_KV, L, 1), F32)],
        compiler_params=_cparams("parallel", "parallel"))(sink, q, k, k, k, v, v, v, kc, vc)


def _attn_bwd_q(q, k, v, kc, vc, sink, o, do, lse, name):
    nkv, _, L, hd = q.shape
    C = kc.shape[1]
    nb = L // ATTN_BLOCK
    scale = HEAD_DIM ** -0.5

    def body(sink_ref, q_ref, k0, k1, k2, v0, v1, v2, kc_ref, vc_ref, o_ref, do_ref, lse_ref, dq_ref, dkc_ref, dvc_ref, dsk_ref):
        hh, n = pl.program_id(0), pl.program_id(1)

        @pl.when(n == 0)
        def _():
            dkc_ref[...] = jnp.zeros_like(dkc_ref)
            dvc_ref[...] = jnp.zeros_like(dvc_ref)
            dsk_ref[...] = jnp.zeros_like(dsk_ref)

        kw = jnp.concatenate([k0[...], k1[...], k2[...]], axis=0)
        vw = jnp.concatenate([v0[...], v1[...], v2[...]], axis=0)
        mask = _window_mask(n, L)
        for g in range(Q_PER_KV):
            qg, dog, lse_g = q_ref[g], do_ref[g], lse_ref[g]
            delta = jnp.sum(dog.astype(F32) * o_ref[g], axis=-1, keepdims=True)
            pw = jnp.exp(jnp.where(mask, _dot(qg, kw, 1, 1) * scale, NEG) - lse_g)
            pc = jnp.exp(_dot(qg, kc_ref[...], 1, 1) * scale - lse_g)
            dsw = pw * (_dot(dog, vw, 1, 1) - delta)
            dsc = pc * (_dot(dog, vc_ref[...], 1, 1) - delta)
            dq_ref[g] = (_dot(dsw, kw, 1, 0) + _dot(dsc, kc_ref[...], 1, 0)) * scale
            dkc_ref[...] += _dot(dsc, qg, 0, 0) * scale
            dvc_ref[...] += _dot(pc, dog, 0, 0)
            psk = jnp.exp(sink_ref[hh * Q_PER_KV + g] - lse_g)
            dsk_ref[pl.ds(g, 1), :] += jnp.broadcast_to(jnp.sum(-psk * delta, axis=0, keepdims=True), (1, LANES))

    qspec = pl.BlockSpec((None, Q_PER_KV, ATTN_BLOCK, hd), lambda h, n: (h, 0, n, 0))
    lspec = pl.BlockSpec((None, Q_PER_KV, ATTN_BLOCK, 1), lambda h, n: (h, 0, n, 0))
    cspec = pl.BlockSpec((None, C, hd), lambda h, n: (h, 0, 0))
    return pl.pallas_call(
        body, name=name, grid=(nkv, nb),
        in_specs=[pl.BlockSpec(memory_space=pltpu.SMEM), qspec] + _band_specs(nb, hd) + _band_specs(nb, hd)
        + [cspec, cspec, qspec, qspec, lspec],
        out_specs=[qspec, cspec, cspec, pl.BlockSpec((None, SUBLANES, LANES), lambda h, n: (h, 0, 0))],
        out_shape=[SDS((nkv, Q_PER_KV, L, hd), F32), SDS((nkv, C, hd), F32), SDS((nkv, C, hd), F32),
                   SDS((nkv, SUBLANES, LANES), F32)],
        compiler_params=_cparams("parallel", "arbitrary"))(sink, q, k, k, k, v, v, v, kc, vc, o, do, lse)


def _attn_bwd_kv(q, k, v, o, do, lse, name):
    nkv, _, L, hd = q.shape
    nb = L // ATTN_BLOCK
    scale = HEAD_DIM ** -0.5

    def body(q0, q1, q2, do0, do1, do2, o0, o1, o2, l0, l1, l2, k_ref, v_ref, dk_ref, dv_ref):
        j = pl.program_id(1)
        qi = lax.broadcasted_iota(jnp.int32, (ATTN_BLOCK, ATTN_BLOCK), 0)
        kk = lax.broadcasted_iota(jnp.int32, (ATTN_BLOCK, ATTN_BLOCK), 1)
        kj, vj = k_ref[...], v_ref[...]
        dk = jnp.zeros((ATTN_BLOCK, hd), F32)
        dv = jnp.zeros((ATTN_BLOCK, hd), F32)
        for slot, (q_r, do_r, o_r, l_r) in enumerate(((q0, do0, o0, l0), (q1, do1, o1, l1), (q2, do2, o2, l2))):
            n = j - 1 + slot
            ok = (n >= 0) & (n < nb) & (jnp.abs(qi + ATTN_BLOCK - ((2 - slot) * ATTN_BLOCK + kk)) <= ATTN_BLOCK)
            for g in range(Q_PER_KV):
                qg, dog = q_r[g], do_r[g]
                delta = jnp.sum(dog.astype(F32) * o_r[g], axis=-1, keepdims=True)
                p = jnp.exp(jnp.where(ok, _dot(qg, kj, 1, 1) * scale - l_r[g], NEG))
                ds = p * (_dot(dog, vj, 1, 1) - delta)
                dk = dk + _dot(ds, qg, 0, 0) * scale
                dv = dv + _dot(p, dog, 0, 0)
        dk_ref[...] = dk
        dv_ref[...] = dv

    def band(width):
        blk = lambda f: pl.BlockSpec((None, Q_PER_KV, ATTN_BLOCK, width), f)
        return [blk(lambda h, j: (h, 0, jnp.maximum(j - 1, 0), 0)), blk(lambda h, j: (h, 0, j, 0)),
                blk(lambda h, j: (h, 0, jnp.minimum(j + 1, nb - 1), 0))]

    kspec = pl.BlockSpec((None, ATTN_BLOCK, hd), lambda h, j: (h, j, 0))
    return pl.pallas_call(
        body, name=name, grid=(nkv, nb), in_specs=band(hd) + band(hd) + band(hd) + band(1) + [kspec, kspec],
        out_specs=[kspec, kspec], out_shape=[SDS((nkv, L, hd), F32), SDS((nkv, L, hd), F32)],
        compiler_params=_cparams("parallel", "parallel"))(q, q, q, do, do, do, o, o, o, lse, lse, lse, k, v)


_GELU_K = math.sqrt(2.0 / math.pi)


def _gelu(v):
    return 0.5 * v * (1.0 + jnp.tanh(_GELU_K * (v + 0.044715 * (v * v * v))))


def _gelu_grad(v):
    t = jnp.tanh(_GELU_K * (v + 0.044715 * (v * v * v)))
    return 0.5 * (1.0 + t) + 0.5 * v * (1.0 - t * t) * (_GELU_K * (1.0 + 3.0 * 0.044715 * (v * v)))


def _gmlp_fwd(p0, b_in, ln_g, ln_b, w_s, b_s, name):
    L, W2 = p0.shape
    W = W2 // 2
    G = W // GMLP_GROUP_DIM

    def body(p_ref, bi_ref, g_ref, b_ref, ws_ref, bs_ref, o_ref):
        ge = _gelu(p_ref[...] + bi_ref[...])
        xh, _ = _layer_norm_stats(ge[:, W:])
        vln = xh * g_ref[...] + b_ref[...]
        for gi in range(G):
            cols = slice(gi * GMLP_GROUP_DIM, (gi + 1) * GMLP_GROUP_DIM)
            s = _dot(ws_ref[gi], vln[:, cols], 1, 0) + bs_ref[gi]
            o_ref[:, cols] = (ge[:, cols] * s).astype(o_ref.dtype)

    full = lambda shape: pl.BlockSpec(shape, lambda i: (0,) * len(shape))
    return pl.pallas_call(
        body, name=name, grid=(L // GMLP_CHUNK,),
        in_specs=[pl.BlockSpec((GMLP_CHUNK, W2), lambda i: (i, 0)), full((1, W2)), full((1, W)), full((1, W)),
                  full((G, GMLP_CHUNK, GMLP_CHUNK)), full((G, GMLP_CHUNK, 1))],
        out_specs=pl.BlockSpec((GMLP_CHUNK, W), lambda i: (i, 0)), out_shape=SDS((L, W), MMT),
        compiler_params=_cparams("parallel"))(p0, b_in, ln_g, ln_b, w_s, b_s)


def _gmlp_bwd(p0, dus, b_in, ln_g, ln_b, w_s, w_st, b_s, name):
    L, W2 = p0.shape
    W = W2 // 2
    G = W // GMLP_GROUP_DIM

    def body(p_ref, d_ref, bi_ref, g_ref, b_ref, ws_ref, wst_ref, bs_ref, dpre_ref, dbi_ref, dg_ref, db_ref, dws_ref, dbs_ref, dvln):
        @pl.when(pl.program_id(0) == 0)
        def _():
            for ref in (dbi_ref, dg_ref, db_ref, dws_ref, dbs_ref):
                ref[...] = jnp.zeros_like(ref)

        pre = p_ref[...] + bi_ref[...]
        ge = _gelu(pre)
        xh, rstd = _layer_norm_stats(ge[:, W:])
        vln = xh * g_ref[...] + b_ref[...]
        dge_u = []
        for gi in range(G):
            cols = slice(gi * GMLP_GROUP_DIM, (gi + 1) * GMLP_GROUP_DIM)
            vg = vln[:, cols]
            s = _dot(ws_ref[gi], vg, 1, 0) + bs_ref[gi]
            dus_g = d_ref[:, cols]
            dge_u.append(dus_g * s)
            ds = dus_g * ge[:, cols]
            dbs_ref[gi] += jnp.sum(ds, axis=1, keepdims=True)
            dws_ref[gi] += _dot(ds, vg, 1, 1)
            dvln[:, cols] = _dot(wst_ref[gi], ds, 1, 0)
        dv = dvln[...]
        dg_ref[...] += _sum0(dv * xh)
        db_ref[...] += _sum0(dv)
        dxh = dv * g_ref[...]
        dv0 = rstd * (dxh - jnp.mean(dxh, axis=-1, keepdims=True) - xh * jnp.mean(dxh * xh, axis=-1, keepdims=True))
        dpre = jnp.concatenate(dge_u + [dv0], axis=1) * _gelu_grad(pre)
        dbi_ref[...] += _sum0(dpre)
        dpre_ref[...] = dpre.astype(dpre_ref.dtype)

    full = lambda shape: pl.BlockSpec(shape, lambda i: (0,) * len(shape))
    mats = (G, GMLP_CHUNK, GMLP_CHUNK)
    return pl.pallas_call(
        body, name=name, grid=(L // GMLP_CHUNK,),
        in_specs=[pl.BlockSpec((GMLP_CHUNK, W2), lambda i: (i, 0)), pl.BlockSpec((GMLP_CHUNK, W), lambda i: (i, 0)),
                  full((1, W2)), full((1, W)), full((1, W)), full(mats), full(mats), full((G, GMLP_CHUNK, 1))],
        out_specs=[pl.BlockSpec((GMLP_CHUNK, W2), lambda i: (i, 0)), full((1, W2)), full((1, W)), full((1, W)), full(mats),
                   full((G, GMLP_CHUNK, 1))],
        out_shape=[SDS((L, W2), MMT), SDS((1, W2), F32), SDS((1, W), F32), SDS((1, W), F32), SDS(mats, F32),
                   SDS((G, GMLP_CHUNK, 1), F32)],
        scratch_shapes=[pltpu.VMEM((GMLP_CHUNK, W), F32)], compiler_params=_cparams("arbitrary"))(
            p0, dus, b_in, ln_g, ln_b, w_s, w_st, b_s)


def _loss_head(h, target, tm, name):
    L, D = h.shape

    def body(h_ref, t_ref, l_ref, d_ref):
        @pl.when(pl.program_id(0) == 0)
        def _():
            l_ref[...] = jnp.zeros_like(l_ref)

        e = h_ref[...] - t_ref[...]
        l_ref[...] += 0.5 * jnp.sum(jnp.mean(e * e, axis=-1, keepdims=True), axis=0, keepdims=True)
        d_ref[...] = e * (1.0 / D)

    row = pl.BlockSpec((tm, D), lambda i: (i, 0))
    return pl.pallas_call(body, name=name, grid=(L // tm,), in_specs=[row, row],
                          out_specs=[pl.BlockSpec((1, 1), lambda i: (0, 0)), row],
                          out_shape=[SDS((1, 1), F32), SDS((L, D), F32)], compiler_params=_cparams("arbitrary"))(h, target)


def _ada_fwd(cond, ada_w, ada_b, name):
    NL, D, n = ada_w.shape
    tn = _tile(n, 768)

    def body(c_ref, w_ref, b_ref, o_ref):
        cv = c_ref[...]
        o_ref[...] = _dot(cv * _sig(cv), w_ref[...], 1, 0) + b_ref[...]

    return pl.pallas_call(
        body, name=name, grid=(NL, n // tn),
        in_specs=[pl.BlockSpec((2 * SUBLANES, D), lambda i, j: (0, 0)), pl.BlockSpec((None, D, tn), lambda i, j: (i, 0, j)),
                  pl.BlockSpec((None, 1, tn), lambda i, j: (i, 0, j))],
        out_specs=pl.BlockSpec((None, 2 * SUBLANES, tn), lambda i, j: (i, 0, j)), out_shape=SDS((NL, 2 * SUBLANES, n), F32),
        compiler_params=_cparams("parallel", "parallel"))(cond, ada_w, ada_b)


def _ada_bwd(cond, ada_w, dm_lat, dm_ctx, name):
    NL, D, n = ada_w.shape
    tn = _tile(n, 768)

    def body(c_ref, w_ref, dl_ref, dc_ref, dw_ref, ds_ref):
        @pl.when((pl.program_id(0) == 0) & (pl.program_id(1) == 0))
        def _():
            ds_ref[...] = jnp.zeros_like(ds_ref)

        cv = c_ref[...]
        row = lax.broadcasted_iota(jnp.int32, (SUBLANES, tn), 0)
        ctx_rows = jnp.where(row == 0, _sum0(dc_ref[...]), 0.0)
        dm = jnp.concatenate([dl_ref[...], ctx_rows], axis=0)
        dw_ref[...] = _dot(cv * _sig(cv), dm, 0, 0)
        ds_ref[...] += _dot(dm, w_ref[...], 1, 1)

    dspec = pl.BlockSpec((None, SUBLANES, tn), lambda i, j: (i, 0, j))
    return pl.pallas_call(
        body, name=name, grid=(NL, n // tn),
        in_specs=[pl.BlockSpec((2 * SUBLANES, D), lambda i, j: (0, 0)), pl.BlockSpec((None, D, tn), lambda i, j: (i, 0, j)),
                  dspec, dspec],
        out_specs=[pl.BlockSpec((None, D, tn), lambda i, j: (i, 0, j)), pl.BlockSpec((2 * SUBLANES, D), lambda i, j: (0, 0))],
        out_shape=[SDS((NL, D, n), F32), SDS((2 * SUBLANES, D), F32)],
        compiler_params=_cparams("arbitrary", "arbitrary"))(cond, ada_w, dm_lat, dm_ctx)


def _adam_math(w, g, m, v):
    m = ADAM_B1 * m + (1.0 - ADAM_B1) * g
    v = ADAM_B2 * v + (1.0 - ADAM_B2) * jnp.square(g)
    m_hat = m / (1.0 - ADAM_B1 ** ADAM_STEP)
    v_hat = v / (1.0 - ADAM_B2 ** ADAM_STEP)
    return -ADAM_LR * (m_hat / (jnp.sqrt(v_hat) + ADAM_EPS) + ADAM_WD * w), m, v


def _row_tile(rows, cols, elems):
    want = max(SUBLANES, elems // cols)
    best = SUBLANES if rows % SUBLANES == 0 else rows
    for d in range(SUBLANES, min(rows, want) + 1, SUBLANES):
        if rows % d == 0:
            best = d
    return best


def _adamw(w, m, v, parts, name):
    R, C = w.shape
    tr = _row_tile(R, C, 128 * 1024)
    npart = len(parts)

    def body(*refs):
        w_ref, m_ref, v_ref = refs[:3]
        g_ref, d_ref, nm_ref, nv_ref = refs[3 + npart:]
        g = refs[3][...]
        for p_ref in refs[4:3 + npart]:
            g = g + p_ref[...]
        d, nm, nv = _adam_math(w_ref[...], g, m_ref[...], v_ref[...])
        g_ref[...], d_ref[...], nm_ref[...], nv_ref[...] = g, d, nm, nv

    blk = pl.BlockSpec((tr, C), lambda i: (i, 0))
    return pl.pallas_call(body, name=name, grid=(R // tr,), in_specs=[blk] * (3 + npart), out_specs=[blk] * 4,
                          out_shape=[SDS((R, C), F32)] * 4, compiler_params=_cparams("parallel"))(w, m, v, *parts)


def _adamw_layer(w, m, v, layer, parts, prev, name):
    _, R, C = w.shape
    tr = _row_tile(R, C, 128 * 1024)
    npart = len(parts)
    nprev = 0 if prev is None else 4

    def body(*refs):
        w_ref, m_ref, v_ref = refs[:3]
        g_ref, d_ref, nm_ref, nv_ref = refs[3 + npart + nprev:]
        g = refs[3][...]
        for p_ref in refs[4:3 + npart]:
            g = g + p_ref[...]
        d, nm, nv = _adam_math(w_ref[...], g, m_ref[...], v_ref[...])
        g_ref[...], d_ref[...], nm_ref[...], nv_ref[...] = g, d, nm, nv

    stacked = pl.BlockSpec((None, tr, C), lambda i: (layer, i, 0))
    flat = pl.BlockSpec((tr, C), lambda i: (i, 0))
    return pl.pallas_call(
        body, name=name, grid=(R // tr,),
        in_specs=[stacked] * 3 + [flat] * npart + [pl.BlockSpec(memory_space=pl.ANY)] * nprev, out_specs=[stacked] * 4,
        out_shape=[SDS(w.shape, F32)] * 4, input_output_aliases={3 + npart + k: k for k in range(nprev)},
        compiler_params=_cparams("parallel"))(w, m, v, *parts, *(prev or ()))


def _sum_slots(x, name, after=None):
    S, R, C = x.shape
    tr = _row_tile(R, C, 128 * 1024)
    extra = [] if after is None else [after]

    def body(x_ref, *rest):
        o_ref = rest[-1]
        acc = x_ref[0].astype(F32)
        for s in range(1, S):
            acc = acc + x_ref[s].astype(F32)
        o_ref[...] = acc

    return pl.pallas_call(
        body, name=name, grid=(R // tr,),
        in_specs=[pl.BlockSpec((S, tr, C), lambda i: (0, i, 0))] + [pl.BlockSpec(memory_space=pl.ANY)] * len(extra),
        out_specs=pl.BlockSpec((tr, C), lambda i: (i, 0)), out_shape=SDS((R, C), F32),
        compiler_params=_cparams("parallel"))(x, *extra)


def _my_place():
    return lax.axis_index("x"), lax.axis_index("y"), lax.axis_index("c")


def _other_chips(x, y):
    return [(1 - x, y), (x, 1 - y), (1 - x, 1 - y)]


def _all_gather(v, name):
    R, C = v.shape

    def body(v_ref, o_ref, send_sems, recv_sems, local_sem):
        x, y, c = _my_place()
        me = 4 * x + 2 * y + c
        mine = pltpu.make_async_copy(v_ref, o_ref.at[me], local_sem)
        mine.start()
        copies = []
        for flip in range(1, N_DEV):
            fx, fy, fc = (flip >> 2) & 1, (flip >> 1) & 1, flip & 1
            peer = ((x + fx) % 2, (y + fy) % 2, (c + fc) % 2)
            cp = pltpu.make_async_remote_copy(src_ref=v_ref, dst_ref=o_ref.at[me], send_sem=send_sems.at[flip - 1],
                                              recv_sem=recv_sems.at[flip - 1], device_id=peer, device_id_type=MESH)
            cp.start()
            copies.append(cp)
        for cp in copies:
            cp.wait()
        mine.wait()

    return pl.pallas_call(
        body, name=name, in_specs=[pl.BlockSpec(memory_space=pl.ANY)], out_specs=pl.BlockSpec(memory_space=pl.ANY),
        out_shape=SDS((N_DEV, R, C), v.dtype),
        scratch_shapes=[pltpu.SemaphoreType.DMA((N_DEV - 1,)), pltpu.SemaphoreType.DMA((N_DEV - 1,)), pltpu.SemaphoreType.DMA],
        )(v)


def _shard_window(ref, axis, j, size):
    idx = [slice(None)] * len(ref.shape)
    idx[axis] = pl.ds(pl.multiple_of(j * size, SUBLANES), size)
    return ref.at[tuple(idx)]


def _gather_plan(axis):
    def plan(srcs, lands):
        x, y, c = _my_place()
        dst = _shard_window(lands[0], axis, 2 * x + y, srcs[0].shape[axis])
        return [(srcs[0], dst)], [(srcs[0], dst, (px, py, c)) for px, py in _other_chips(x, y)]
    return plan


def _scatter_plan(axis):
    def plan(srcs, lands):
        x, y, c = _my_place()
        j = 2 * x + y
        size = srcs[0].shape[axis] // N_CHIPS
        local = [(_shard_window(srcs[0], axis, j, size), lands[0].at[j])]
        remote = [(_shard_window(srcs[0], axis, 2 * px + py, size), lands[0].at[j], (px, py, c)) for px, py in _other_chips(x, y)]
        return local, remote
    return plan


def _all_gather_plan(srcs, lands):
    x, y, c = _my_place()
    dst = lands[0].at[4 * x + 2 * y + c]
    remote = []
    for flip in range(1, N_DEV):
        fx, fy, fc = (flip >> 2) & 1, (flip >> 1) & 1, flip & 1
        remote.append((srcs[0], dst, ((x + fx) % 2, (y + fy) % 2, (c + fc) % 2)))
    return [(srcs[0], dst)], remote


def _same_core_peers():
    x, y, c = _my_place()
    return [(px, py, c) for px, py in _other_chips(x, y)]


def _all_peers():
    x, y, c = _my_place()
    return [((x + (f >> 2 & 1)) % 2, (y + (f >> 1 & 1)) % 2, (c + (f & 1)) % 2) for f in range(1, N_DEV)]


def _tensorcore_exchange(name, exchanges):
    srcs = [s for e in exchanges for s in e[0]]
    shapes = [s for e in exchanges for s in e[1]]
    ns, nl = len(srcs), len(shapes)
    first = [sum(e[3] for e in exchanges[:i]) for i in range(len(exchanges))]
    ncopy = sum(e[3] for e in exchanges)

    def body(*refs):
        src_refs, land_refs = refs[:ns], refs[ns:ns + nl]
        send_sems, recv_sems, local_sem = refs[ns + nl:]
        si = li = 0
        plans = []
        for e_srcs, e_lands, plan, _ in exchanges:
            plans.append(plan(src_refs[si:si + len(e_srcs)], land_refs[li:li + len(e_lands)]))
            si, li = si + len(e_srcs), li + len(e_lands)
        for local, _ in plans:
            for src, dst in local:
                cp = pltpu.make_async_copy(src, dst, local_sem)
                cp.start()
                cp.wait()
        copies = []
        for i, (_, remote) in enumerate(plans):
            assert len(remote) == exchanges[i][3]
            for k, (src, dst, peer) in enumerate(remote):
                cp = pltpu.make_async_remote_copy(src_ref=src, dst_ref=dst, send_sem=send_sems.at[first[i] + k],
                                                  recv_sem=recv_sems.at[first[i] + k], device_id=peer, device_id_type=MESH)
                cp.start()
                copies.append(cp)
        for cp in copies:
            cp.wait()

    any_spec = pl.BlockSpec(memory_space=pl.ANY)
    res = pl.pallas_call(
        body, name=name, in_specs=[any_spec] * ns, out_specs=[any_spec] * nl, out_shape=list(shapes),
        scratch_shapes=[pltpu.SemaphoreType.DMA((ncopy,)), pltpu.SemaphoreType.DMA((ncopy,)), pltpu.SemaphoreType.DMA])(*srcs)
    out, li = [], 0
    for e in exchanges:
        out.append(list(res[li:li + len(e[1])]))
        li += len(e[1])
    return out


def _sequencer_exchange(name, collective_id, exchanges, peers_fn):
    hbm = pltpu.MemorySpace.HBM
    src_refs = [[jax.new_ref(s, memory_space=hbm) for s in e[0]] for e in exchanges]
    land_refs = [[jax.empty_ref(s, memory_space=hbm) for s in e[1]] for e in exchanges]
    first = [sum(e[3] for e in exchanges[:i]) for i in range(len(exchanges))]
    ncopy = sum(e[3] for e in exchanges)

    @pl.kernel(mesh=plsc.ScalarSubcoreMesh(axis_name="sequencer", num_cores=1), name=name,
               scratch_types=(pltpu.SemaphoreType.DMA((ncopy,)), pltpu.SemaphoreType.DMA((ncopy,)), pltpu.SemaphoreType.DMA),
               compiler_params=pltpu.CompilerParams(collective_id=collective_id))
    def launch(send_sems, recv_sems, local_sem):
        peers = peers_fn()
        barrier = pltpu.get_barrier_semaphore()
        for peer in peers:
            pl.semaphore_signal(barrier, inc=1, device_id=peer, device_id_type=MESH)
        pl.semaphore_wait(barrier, len(peers))
        plans = [e[2](src_refs[i], land_refs[i]) for i, e in enumerate(exchanges)]
        for local, _ in plans:
            for src, dst in local:
                cp = pltpu.make_async_copy(src, dst, local_sem)
                cp.start()
                cp.wait()
        copies = []
        for i, (_, remote) in enumerate(plans):
            assert len(remote) == exchanges[i][3]
            for k, (src, dst, peer) in enumerate(remote):
                cp = pltpu.make_async_remote_copy(src_ref=src, dst_ref=dst, send_sem=send_sems.at[first[i] + k],
                                                  recv_sem=recv_sems.at[first[i] + k], device_id=peer, device_id_type=MESH)
                cp.start()
                copies.append(cp)
        for cp in copies:
            cp.wait()

    launch()
    return [[r[...] for r in refs] for refs in land_refs]


def _swap_with_sibling(parts, name):
    nt = len(parts)

    def body(*refs):
        ins, outs = refs[:nt], refs[nt:2 * nt]
        send_sems, recv_sems = refs[2 * nt:]
        x, y, c = _my_place()
        copies = []
        for t in range(nt):
            cp = pltpu.make_async_remote_copy(src_ref=ins[t], dst_ref=outs[t], send_sem=send_sems.at[t], recv_sem=recv_sems.at[t],
                                              device_id=(x, y, 1 - c), device_id_type=MESH)
            cp.start()
            copies.append(cp)
        for cp in copies:
            cp.wait()

    any_spec = pl.BlockSpec(memory_space=pl.ANY)
    return pl.pallas_call(
        body, name=name, in_specs=[any_spec] * nt, out_specs=[any_spec] * nt, out_shape=[SDS(p.shape, p.dtype) for p in parts],
        scratch_shapes=[pltpu.SemaphoreType.DMA((nt,)), pltpu.SemaphoreType.DMA((nt,))],
        )(*parts)


PACK_COLS = 1024


def _pack(arrays):
    flat = jnp.concatenate([a.reshape(-1) for a in arrays])
    pad = (-flat.shape[0]) % (SUBLANES * PACK_COLS)
    return jnp.pad(flat, (0, pad)).reshape(-1, PACK_COLS)


def _unpack(packed, shapes):
    flat, out, pos = packed.reshape(-1), [], 0
    for shape in shapes:
        n = math.prod(shape)
        out.append(flat[pos:pos + n].reshape(shape))
        pos += n
    return out


def _unshard_last(stacked):
    moved = jnp.moveaxis(stacked, 0, -2)
    return moved.reshape(moved.shape[:-2] + (moved.shape[-2] * moved.shape[-1],))


def _my_block_last(full, j):
    s = full.shape[-1] // N_CHIPS
    return lax.dynamic_index_in_dim(full.reshape(full.shape[:-1] + (N_CHIPS, s)), j, axis=full.ndim - 1, keepdims=False)


def _rope_tables(L):
    rows = L // GRID_W
    row = jnp.repeat(jnp.arange(rows), GRID_W).astype(F32)
    col = jnp.tile(jnp.arange(GRID_W), rows).astype(F32)
    axis_dim = HEAD_DIM // 2
    inv_freq = ROPE_BASE ** (-jnp.arange(0, axis_dim, 2, dtype=F32) / axis_dim)
    ang_r, ang_c = row[:, None] * inv_freq[None, :], col[:, None] * inv_freq[None, :]
    ang = jnp.concatenate([ang_r, ang_r, ang_c, ang_c] * 2, axis=-1)
    return jnp.cos(ang), jnp.sin(ang)


SMALL_SHARDED = ("norm_g", "ffn_conv_w", "cm_b_in", "cm_dw_w", "cm_dw_b", "cm_ln_g", "cm_ln_b", "cm_b_out", "gm_b_in", "gm_ln_g",
                 "gm_ln_b")
SMALL_REPLICATED = ("c_ctx", "ada_b", "ffn_conv_b", "attn_sink", "gm_w_s", "gm_b_s")
BIG = ("ffn_w_up", "ffn_w_down", "cm_w_in", "cm_w_out", "attn_w_qkv", "attn_w_o", "gm_w_in", "gm_w_out")
BIG_AXIS = {"ffn_w_up": 2, "ffn_w_down": 1, "cm_w_in": 2, "cm_w_out": 1, "attn_w_qkv": 2, "attn_w_o": 1, "gm_w_in": 2, "gm_w_out": 1}
WEIGHTS = ("c_ctx", "ada_w", "ada_b", "norm_g", "ffn_w_up", "ffn_conv_w", "ffn_conv_b", "ffn_w_down", "cm_w_in", "cm_b_in",
           "cm_dw_w", "cm_dw_b", "cm_ln_g", "cm_ln_b", "cm_w_out", "cm_b_out", "attn_w_qkv", "attn_sink", "attn_w_o", "gm_w_in",
           "gm_b_in", "gm_ln_g", "gm_ln_b", "gm_w_s", "gm_b_s", "gm_w_out")


def _step(x, c, ctx, target, W, M, V):
    L, D = x.shape[1], x.shape[2]
    C = ctx.shape[1]
    T = L + C
    NL = W["ada_w"].shape[0]
    tm = 256 if C % 256 == 0 else 128
    nl = L // tm
    xi, yi, ci = _my_place()
    chip = 2 * xi + yi
    dev = 4 * xi + 2 * yi + ci
    segs2, segs1 = [(0, L), (L, C)], [(0, L)]
    vec = lambda a: a.reshape(1, -1)

    layer_sets = [[("cm_w_in", 0), ("cm_w_out", 0), ("ffn_w_up", 0), ("ffn_w_down", 0)],
                  [("attn_w_qkv", 0), ("attn_w_o", 0), ("ffn_w_up", 1), ("ffn_w_down", 1)],
                  [("gm_w_in", 0), ("gm_w_out", 0), ("ffn_w_up", 2), ("ffn_w_down", 2)],
                  [("cm_w_in", 1), ("cm_w_out", 1), ("ffn_w_up", 3), ("ffn_w_down", 3)]]
    arrived = {}

    def fetch(keys, zero, sequencer_id=None):
        exchanges = []
        for n, i in keys:
            shard = (W[n][i] + zero).astype(MMT)
            whole = list(shard.shape)
            whole[BIG_AXIS[n] - 1] *= N_CHIPS
            exchanges.append(([shard], [SDS(tuple(whole), MMT)], _gather_plan(BIG_AXIS[n] - 1), N_CHIPS - 1))
        if sequencer_id is None:
            lands = _tensorcore_exchange("fetch_weights_first", exchanges)
        else:
            lands = _sequencer_exchange(f"fetch_weights_{keys[0][0]}_{keys[0][1]}", sequencer_id, exchanges, _same_core_peers)
        for key, land in zip(keys, lands):
            arrived[key] = land[0]

    def big(n, i, after=None):
        return arrived[(n, i)]

    small_shapes = [W[n].shape for n in SMALL_SHARDED]
    ag1 = _all_gather(_pack([c.reshape(-1)] + [W[n] for n in SMALL_SHARDED]), "gather_small")
    parts = [_unpack(ag1[2 * s], [(D,)] + small_shapes) for s in range(N_CHIPS)]
    c_rows = jnp.stack([_unpack(ag1[d], [(D,)])[0] for d in range(N_DEV)])
    P = {n: _unshard_last(jnp.stack([parts[s][1 + i] for s in range(N_CHIPS)])) for i, n in enumerate(SMALL_SHARDED)}
    for n in SMALL_REPLICATED:
        P[n] = W[n]

    cond = jnp.concatenate([c_rows, W["c_ctx"][None, :], jnp.zeros((2 * SUBLANES - N_DEV - 1, D), F32)], axis=0)
    ncol = W["ada_w"].shape[2]
    ada_b_mine = lax.dynamic_slice_in_dim(W["ada_b"], chip * ncol, ncol, axis=1)[:, None, :]
    mods_mine = _ada_fwd(cond, W["ada_w"], ada_b_mine, "ada_fwd")
    ag2 = _all_gather(mods_mine.reshape(NL * 2 * SUBLANES, ncol), "gather_mods").reshape(N_DEV, NL, 2 * SUBLANES, ncol)
    mods_all = _unshard_last(jnp.stack([ag2[2 * s] for s in range(N_CHIPS)]))
    mod_lat = lax.dynamic_index_in_dim(mods_all, dev, axis=1, keepdims=False).reshape(NL, 6, D)
    mod_ctx = mods_all[:, N_DEV].reshape(NL, 6, D)
    mod2 = jnp.stack([mod_lat, mod_ctx], axis=1)
    mod1 = mod_lat[:, None]

    corner = mod2[0, 0, 0, 0]
    behind_small = jnp.where(corner != corner, corner, 0.0)
    for k in range(4):
        fetch(layer_sets[k], behind_small, FETCH_IDS[k])
    zero_d = jnp.zeros((1, D), F32)
    cos, sin = _rope_tables(L)
    nkv = D // HEAD_DIM // Q_PER_KV
    qdim, kvdim = D, nkv * HEAD_DIM

    def ffn_fwd(i, h, mod, rows, segs, tag):
        a2 = _prenorm(h, mod, vec(P["norm_g"][i, 2]), 1, rows, nl, tm, f"pre_ffn_{tag}")
        z0 = _mm(a2, big("ffn_w_up", i, a2), "nn", F32, f"ffn_up_{tag}")
        u = _ffn_gate(z0, P["ffn_conv_w"][i], vec(P["ffn_conv_b"][i]), segs, f"ffn_gate_{tag}")
        f = _mm(u, big("ffn_w_down", i, u), "nn", F32, f"ffn_down_{tag}")
        h_out = _postnorm(h, f, zero_d, mod, vec(P["norm_g"][i, 3]), 5, rows, nl, tm, f"post_ffn_{tag}")
        return h_out, dict(h=h, a2=a2, z0=z0, f=f)

    def ffn_bwd(i, dh, sv, mod, rows, segs, tag, G):
        df, dg2, dgn3, _ = _postnorm_bwd(dh, sv["f"], zero_d, mod, vec(P["norm_g"][i, 3]), 5, rows, nl, tm, f"post_ffn_bwd_{tag}")
        du = _mm(df, big("ffn_w_down", i), "nt", F32, f"ffn_down_dx_{tag}")
        u, dz0, dcw, dcb = _ffn_gate_bwd(sv["z0"], du, P["ffn_conv_w"][i], vec(P["ffn_conv_b"][i]), segs, f"ffn_gate_bwd_{tag}")
        G["ffn_w_down"][i] = _mm(u, df, "tn", MMT, f"ffn_down_dw_{tag}")
        G["ffn_w_up"][i] = _mm(sv["a2"], dz0, "tn", MMT, f"ffn_up_dw_{tag}")
        da2 = _mm(dz0, big("ffn_w_up", i), "nt", F32, f"ffn_up_dx_{tag}")
        dh, dsh2, dsc2, dgn2 = _prenorm_bwd(sv["h"], da2, dh, mod, vec(P["norm_g"][i, 2]), 1, rows, nl, tm, f"pre_ffn_bwd_{tag}")
        G["ffn_conv_w"][i], G["ffn_conv_b"][i] = dcw, dcb[0]
        return dh, (dsh2, dsc2, dg2), (dgn2, dgn3)

    def conformer_fwd(i, j, h, mod, rows, segs, tag):
        a = _prenorm(h, mod, vec(P["norm_g"][i, 0]), 0, rows, nl, tm, f"pre_mix_{tag}")
        p0 = _mm(a, big("cm_w_in", j, a), "nn", F32, f"cm_in_{tag}")
        z2 = _glu_conv(p0, vec(P["cm_b_in"][j]), P["cm_dw_w"][j], vec(P["cm_dw_b"][j]), segs, f"cm_conv_{tag}")
        z4 = _ln_silu(z2, vec(P["cm_ln_g"][j]), vec(P["cm_ln_b"][j]), rows, tm, f"cm_ln_{tag}")
        y = _mm(z4, big("cm_w_out", j, z4), "nn", F32, f"cm_out_{tag}")
        h_out = _postnorm(h, y, vec(P["cm_b_out"][j]), mod, vec(P["norm_g"][i, 1]), 2, rows, nl, tm, f"post_mix_{tag}")
        return h_out, dict(h=h, a=a, p0=p0, z2=z2, z4=z4, y=y)

    def conformer_bwd(i, j, dh, sv, mod, rows, segs, tag, G):
        dy, dg1, dgn1, dbo = _postnorm_bwd(dh, sv["y"], vec(P["cm_b_out"][j]) + zero_d, mod, vec(P["norm_g"][i, 1]), 2, rows, nl,
                                           tm, f"post_mix_bwd_{tag}")
        G["cm_w_out"][j] = _mm(sv["z4"], dy, "tn", MMT, f"cm_out_dw_{tag}")
        dz4 = _mm(dy, big("cm_w_out", j), "nt", F32, f"cm_out_dx_{tag}")
        dz2, dlg, dlb = _ln_silu_bwd(sv["z2"], dz4, vec(P["cm_ln_g"][j]), vec(P["cm_ln_b"][j]), rows, tm, f"cm_ln_bwd_{tag}")
        dpa, dpg, ddw, ddb, dba, dbg = _glu_conv_bwd(sv["p0"], vec(P["cm_b_in"][j]), P["cm_dw_w"][j], dz2, segs, f"cm_conv_bwd_{tag}")
        dp = jnp.concatenate([dpa, dpg], axis=1)
        G["cm_w_in"][j] = _mm(sv["a"], dp, "tn", MMT, f"cm_in_dw_{tag}")
        da = _mm(dp, big("cm_w_in", j), "nt", F32, f"cm_in_dx_{tag}")
        dh, dsh1, dsc1, dgn0 = _prenorm_bwd(sv["h"], da, dh, mod, vec(P["norm_g"][i, 0]), 0, rows, nl, tm, f"pre_mix_bwd_{tag}")
        G["cm_b_out"][j] = jnp.sum(dbo, axis=0)[0]
        G["cm_ln_g"][j], G["cm_ln_b"][j], G["cm_dw_w"][j], G["cm_dw_b"][j] = dlg[0], dlb[0], ddw, ddb[0]
        G["cm_b_in"][j] = jnp.concatenate([dba[0], dbg[0]])
        return dh, (dsh1, dsc1, dg1), (dgn0, dgn1)

    def heads(a, n):
        return a.reshape(a.shape[0], n, HEAD_DIM).transpose(1, 0, 2)

    def unheads(a):
        return a.transpose(1, 0, 2).reshape(a.shape[1], -1)

    G = {n: [None] * W[n].shape[0] for n in WEIGHTS if n not in ("c_ctx", "ada_w", "ada_b", "norm_g")}
    saved = []
    h = jnp.concatenate([x[0], ctx[0]], axis=0)
    h, s_mix = conformer_fwd(0, 0, h, mod2[0], T, segs2, "l0")
    h, s_ffn = ffn_fwd(0, h, mod2[0], T, segs2, "l0")
    saved.append((s_mix, s_ffn))
    a_all = _prenorm(h, mod2[1], vec(P["norm_g"][1, 0]), 0, T, nl, tm, "pre_mix_l1")
    qkv = _mm(a_all, big("attn_w_qkv", 0, a_all), "nn", F32, "attn_qkv")
    qk_rot, v_lat = _rope(qkv, cos, sin, L, qdim + kvdim, tm, "rope")
    q_h = heads(qk_rot[:, :qdim], nkv * Q_PER_KV).reshape(nkv, Q_PER_KV, L, HEAD_DIM)
    k_h, v_h = heads(qk_rot[:, qdim:], nkv), heads(v_lat, nkv)
    kc_h = heads(qkv[L:, qdim:qdim + kvdim].astype(MMT), nkv)
    vc_h = heads(qkv[L:, qdim + kvdim:].astype(MMT), nkv)
    sink = P["attn_sink"][0]
    o_h, lse = _attn_fwd(q_h, k_h, v_h, kc_h, vc_h, sink, "attn")
    o_nat = unheads(o_h.reshape(nkv * Q_PER_KV, L, HEAD_DIM)).astype(MMT)
    y1 = _mm(o_nat, big("attn_w_o", 0, o_nat), "nn", F32, "attn_out")
    h_in1 = h
    h = _postnorm(h, y1, zero_d, mod1[1], vec(P["norm_g"][1, 1]), 2, L, nl, tm, "post_mix_l1")
    h, s_ffn1 = ffn_fwd(1, h, mod1[1], L, segs1, "lat")
    h_in2 = h
    a_2 = _prenorm(h, mod1[2], vec(P["norm_g"][2, 0]), 0, L, nl, tm, "pre_mix_l2")
    p0_2 = _mm(a_2, big("gm_w_in", 0, a_2), "nn", F32, "gm_in")
    ws_bf = P["gm_w_s"][0].astype(MMT)
    bs_col = P["gm_b_s"][0][:, :, None]
    us = _gmlp_fwd(p0_2, vec(P["gm_b_in"][0]), vec(P["gm_ln_g"][0]), vec(P["gm_ln_b"][0]), ws_bf, bs_col, "gmlp")
    y2 = _mm(us, big("gm_w_out", 0, us), "nn", F32, "gm_out")
    h = _postnorm(h, y2, zero_d, mod1[2], vec(P["norm_g"][2, 1]), 2, L, nl, tm, "post_mix_l2")
    h, s_ffn2 = ffn_fwd(2, h, mod1[2], L, segs1, "lat")
    h, s_mix3 = conformer_fwd(3, 1, h, mod1[3], L, segs1, "l3")
    h, s_ffn3 = ffn_fwd(3, h, mod1[3], L, segs1, "lat")

    loss_mine, dh = _loss_head(h, target[0], tm, "loss_head")

    dmod = [None] * NL
    dgn = [None] * NL

    def finish(i, mix, ffn, gns_mix, gns_ffn):
        dmod[i] = jnp.concatenate(list(mix) + list(ffn), axis=1)
        dgn[i] = jnp.stack([jnp.sum(g, axis=0)[0] for g in (gns_mix[0], gns_mix[1], gns_ffn[0], gns_ffn[1])])

    sent, so_far = {}, {}

    def send(tag, collective_id, tensors):
        exchanges = []
        for n, l in tensors:
            g = G[n][l]
            shard = list(g.shape)
            shard[BIG_AXIS[n] - 1] //= N_CHIPS
            exchanges.append(([g], [SDS((N_CHIPS,) + tuple(shard), g.dtype)], _scatter_plan(BIG_AXIS[n] - 1), N_CHIPS - 1))
        sent[tag] = (tensors, _sequencer_exchange(f"send_grads_{tag}", collective_id, exchanges, _same_core_peers))
        corner = sum(G[n][l][0:1, 0:1].astype(F32) for n, l in tensors)
        return jnp.where(corner != corner, corner, 0.0)

    def land(tag, after):
        tensors, landed = sent[tag]
        mine = [_sum_slots(lands[0], f"sum_chips_{n}_{l}", after) for (n, l), lands in zip(tensors, landed)]
        theirs = _swap_with_sibling(mine, f"swap_cores_{tag}")
        for (n, l), a, b in zip(tensors, mine, theirs):
            so_far[n] = _adamw_layer(W[n], M[n], V[n], l, [a, b], so_far.get(n), f"adamw_{n}_{l}")

    dh, m_ffn, n_ffn = ffn_bwd(3, dh, s_ffn3, mod1[3], L, segs1, "lat", G)
    dh, m_mix, n_mix = conformer_bwd(3, 1, dh, s_mix3, mod1[3], L, segs1, "l3", G)
    finish(3, m_mix, m_ffn, n_mix, n_ffn)
    zero_d = zero_d + send("l3", SEND_IDS[0], [("ffn_w_up", 3), ("ffn_w_down", 3), ("cm_w_in", 1), ("cm_w_out", 1)])
    land("l3", None)

    dh, m_ffn, n_ffn = ffn_bwd(2, dh, s_ffn2, mod1[2], L, segs1, "lat", G)
    dy2, dg1, dgn1, _ = _postnorm_bwd(dh, y2, zero_d, mod1[2], vec(P["norm_g"][2, 1]), 2, L, nl, tm, "post_mix_bwd_l2")
    G["gm_w_out"][0] = _mm(us, dy2, "tn", MMT, "gm_out_dw")
    dus = _mm(dy2, big("gm_w_out", 0), "nt", F32, "gm_out_dx")
    ws_t = jnp.swapaxes(P["gm_w_s"][0], 1, 2).astype(MMT)
    dpre, dbi, dlg, dlb, dws, dbs = _gmlp_bwd(p0_2, dus, vec(P["gm_b_in"][0]), vec(P["gm_ln_g"][0]), vec(P["gm_ln_b"][0]), ws_bf,
                                              ws_t, bs_col, "gmlp_bwd")
    G["gm_w_in"][0] = _mm(a_2, dpre, "tn", MMT, "gm_in_dw")
    da = _mm(dpre, big("gm_w_in", 0), "nt", F32, "gm_in_dx")
    dh, dsh1, dsc1, dgn0 = _prenorm_bwd(h_in2, da, dh, mod1[2], vec(P["norm_g"][2, 0]), 0, L, nl, tm, "pre_mix_bwd_l2")
    G["gm_b_in"][0], G["gm_ln_g"][0], G["gm_ln_b"][0], G["gm_w_s"][0], G["gm_b_s"][0] = dbi[0], dlg[0], dlb[0], dws, dbs[:, :, 0]
    finish(2, (dsh1, dsc1, dg1), m_ffn, (dgn0, dgn1), n_ffn)
    zero_d = zero_d + send("l2", SEND_IDS[1], [("ffn_w_up", 2), ("ffn_w_down", 2), ("gm_w_in", 0), ("gm_w_out", 0)])
    land("l2", None)

    dh, m_ffn, n_ffn = ffn_bwd(1, dh, s_ffn1, mod1[1], L, segs1, "lat", G)
    dy1, dg1, dgn1, _ = _postnorm_bwd(dh, y1, zero_d, mod1[1], vec(P["norm_g"][1, 1]), 2, L, nl, tm, "post_mix_bwd_l1")
    G["attn_w_o"][0] = _mm(o_nat, dy1, "tn", MMT, "attn_out_dw")
    do_nat = _mm(dy1, big("attn_w_o", 0), "nt", MMT, "attn_out_dx")
    do_h = heads(do_nat, nkv * Q_PER_KV).reshape(nkv, Q_PER_KV, L, HEAD_DIM)
    dq_h, dkc_h, dvc_h, dsk = _attn_bwd_q(q_h, k_h, v_h, kc_h, vc_h, sink, o_h, do_h, lse, "attn_bwd_q")
    dk_h, dv_h = _attn_bwd_kv(q_h, k_h, v_h, o_h, do_h, lse, "attn_bwd_kv")
    dqk = jnp.concatenate([unheads(dq_h.reshape(nkv * Q_PER_KV, L, HEAD_DIM)), unheads(dk_h)], axis=1)
    dqkv_lat = _rope_bwd(dqk, unheads(dv_h), cos, sin, tm, "rope_bwd")
    dqkv_ctx = jnp.concatenate([jnp.zeros((C, qdim), MMT), unheads(dkc_h).astype(MMT), unheads(dvc_h).astype(MMT)], axis=1)
    dqkv = jnp.concatenate([dqkv_lat, dqkv_ctx], axis=0)
    G["attn_w_qkv"][0] = _mm(a_all, dqkv, "tn", MMT, "attn_qkv_dw")
    da_all = _mm(dqkv, big("attn_w_qkv", 0), "nt", F32, "attn_qkv_dx")
    dh_all = jnp.concatenate([dh, jnp.zeros((C, D), F32)], axis=0)
    dh, dsh1, dsc1, dgn0 = _prenorm_bwd(h_in1, da_all, dh_all, mod2[1], vec(P["norm_g"][1, 0]), 0, T, nl, tm, "pre_mix_bwd_l1")
    G["attn_sink"][0] = dsk[:, :Q_PER_KV, 0].reshape(-1)
    pad_ctx = lambda a: jnp.concatenate([a, jnp.zeros_like(a)], axis=0)
    finish(1, (dsh1, dsc1, pad_ctx(dg1)), [pad_ctx(a) for a in m_ffn], (dgn0, dgn1), n_ffn)
    zero_d = zero_d + send("l1", SEND_IDS[2], [("ffn_w_up", 1), ("ffn_w_down", 1), ("attn_w_qkv", 0), ("attn_w_o", 0)])
    land("l1", None)

    s_mix0, s_ffn0 = saved[0]
    dh, m_ffn, n_ffn = ffn_bwd(0, dh, s_ffn0, mod2[0], T, segs2, "l0", G)
    zero_d = zero_d + send("l0_ffn", SEND_IDS[3], [("ffn_w_up", 0), ("ffn_w_down", 0)])
    dh, m_mix, n_mix = conformer_bwd(0, 0, dh, s_mix0, mod2[0], T, segs2, "l0", G)
    finish(0, m_mix, m_ffn, n_mix, n_ffn)
    grad_x = dh[:L][None]
    sent_l0 = send("l0_mix", SEND_IDS[4], [("cm_w_in", 0), ("cm_w_out", 0)])

    for i in range(2, NL):
        dmod[i] = pad_ctx(dmod[i])
    dmod_all = jnp.stack(dmod).reshape(NL, 2, 6 * D) + sent_l0

    ag3 = _all_gather(dmod_all.reshape(NL * 2, 6 * D), "gather_dmods").reshape(N_DEV, NL, 2, N_CHIPS, ncol)
    dm_cols = lax.dynamic_index_in_dim(ag3, chip, axis=3, keepdims=False)
    dm_lat, dm_ctx = jnp.moveaxis(dm_cols[:, :, 0], 0, 1), jnp.moveaxis(dm_cols[:, :, 1], 0, 1)
    g_ada_w, dsilu = _ada_bwd(cond, W["ada_w"], dm_lat, dm_ctx, "ada_bwd")
    cc = W["c_ctx"]
    sg = jax.nn.sigmoid(cc)
    dcctx_part = jnp.where(ci == 0, 1.0, 0.0) * dsilu[N_DEV] * (sg * (1.0 + cc * (1.0 - sg)))

    Gs = {n: jnp.stack(G[n]) for n in G if n not in BIG}
    Gs["norm_g"] = jnp.stack(dgn)
    Gs["ada_b"] = jnp.sum(dmod_all, axis=1)
    Gs["c_ctx"] = dcctx_part
    small_names = list(SMALL_SHARDED) + list(SMALL_REPLICATED)
    small_full_shapes = [P[n].shape for n in small_names]
    small_pack = _pack([Gs[n] for n in small_names])
    ((ag4,),) = _sequencer_exchange("gather_small_grads", SMALL_GRADS_ID, [
        ([small_pack], [SDS((N_DEV,) + small_pack.shape, F32)], _all_gather_plan, N_DEV - 1)], _all_peers)

    flat2 = lambda a: a.reshape(-1, a.shape[-1])
    res = {}
    outs = _adamw(flat2(W["ada_w"]), flat2(M["ada_w"]), flat2(V["ada_w"]), [flat2(g_ada_w)], "adamw_ada_w")
    res["ada_w"] = tuple(o.reshape(W["ada_w"].shape) for o in outs)

    land("l0_ffn", outs[0])
    land("l0_mix", so_far["ffn_w_up"][0])
    for n in BIG:
        res[n] = tuple(so_far[n])

    small_sum = _unpack(_sum_slots(ag4, "sum_small_grads"), small_full_shapes)
    g_small = {}
    for n, g in zip(small_names, small_sum):
        g_small[n] = _my_block_last(g, chip) if n in SMALL_SHARDED else g
    packed = [_pack([d[n] for n in small_names]) for d in (W, M, V)]
    outs_small = _adamw(packed[0], packed[1], packed[2], [_pack([g_small[n] for n in small_names])], "adamw_small")
    shard_shapes = [W[n].shape for n in small_names]
    for k, n in enumerate(small_names):
        res[n] = tuple(_unpack(o, shard_shapes)[k] for o in outs_small)

    loss = lax.psum(loss_mine[0, 0], ("x", "y", "c"))
    return (loss, grad_x) + tuple(res[n][k] for k in range(4) for n in WEIGHTS)


def kernel(x, c, ctx, c_ctx, ada_w, ada_b, norm_g, ffn_w_up, ffn_conv_w, ffn_conv_b, ffn_w_down, cm_w_in, cm_b_in, cm_dw_w, cm_dw_b, cm_ln_g, cm_ln_b, cm_w_out, cm_b_out, attn_w_qkv, attn_sink, attn_w_o, gm_w_in, gm_b_in, gm_ln_g, gm_ln_b, gm_w_s, gm_b_s, gm_w_out, loss_target, m_c_ctx, m_ada_w, m_ada_b, m_norm_g, m_ffn_w_up, m_ffn_conv_w, m_ffn_conv_b, m_ffn_w_down, m_cm_w_in, m_cm_b_in, m_cm_dw_w, m_cm_dw_b, m_cm_ln_g, m_cm_ln_b, m_cm_w_out, m_cm_b_out, m_attn_w_qkv, m_attn_sink, m_attn_w_o, m_gm_w_in, m_gm_b_in, m_gm_ln_g, m_gm_ln_b, m_gm_w_s, m_gm_b_s, m_gm_w_out, v_c_ctx, v_ada_w, v_ada_b, v_norm_g, v_ffn_w_up, v_ffn_conv_w, v_ffn_conv_b, v_ffn_w_down, v_cm_w_in, v_cm_b_in, v_cm_dw_w, v_cm_dw_b, v_cm_ln_g, v_cm_ln_b, v_cm_w_out, v_cm_b_out, v_attn_w_qkv, v_attn_sink, v_attn_w_o, v_gm_w_in, v_gm_b_in, v_gm_ln_g, v_gm_ln_b, v_gm_w_s, v_gm_b_s, v_gm_w_out):
    args = locals()
    W = {n: args[n] for n in WEIGHTS}
    M = {n: args["m_" + n] for n in WEIGHTS}
    V = {n: args["v_" + n] for n in WEIGHTS}
    return _step(x, c, ctx, loss_target, W, M, V)
```

```python
import functools
import math

import jax
import jax.numpy as jnp
from jax import lax
from jax.experimental import pallas as pl
from jax.experimental.pallas import tpu as pltpu
from jax.experimental.pallas import tpu_sc as plsc

F32 = jnp.float32
MMT = jnp.bfloat16
SDS = jax.ShapeDtypeStruct
MESH = pl.DeviceIdType.MESH

EPS = 1e-6
HEAD_DIM = 64
Q_PER_KV = 4
ATTN_BLOCK = 128
GRID_W = 64
ROPE_BASE = 10000.0
GMLP_CHUNK = 128
GMLP_GROUP_DIM = 128
CONV_WIDTH = 31
FFN_CONV_WIDTH = 3
NEG = -1e30

ADAM_LR, ADAM_B1, ADAM_B2, ADAM_EPS, ADAM_WD, ADAM_STEP = 0.001, 0.9, 0.999, 1e-08, 0.01, 10

LANES = 128
SUBLANES = 8
VMEM_LIMIT = 52 * 1024 * 1024
CONV_ROWS = 128
N_CHIPS = 4
N_DEV = 8
N_SEQUENCERS = 2
FETCH_IDS = (1, 2, 3, 4)
SEND_IDS = (5, 6, 7, 8, 9)
SMALL_GRADS_ID = 10


def _cparams(*sem):
    return pltpu.CompilerParams(dimension_semantics=sem if sem else None, vmem_limit_bytes=VMEM_LIMIT)


def _tile(n, cap, mult=LANES):
    best = None
    for d in range(mult, min(n, cap) + 1, mult):
        if n % d == 0:
            best = d
    return best if best is not None else n


def _sum0(v):
    return jnp.sum(v, axis=0, keepdims=True)


def _rms(v):
    r = lax.rsqrt(jnp.mean(v * v, axis=-1, keepdims=True) + EPS)
    return v * r, r


def _sig(v):
    return jax.nn.sigmoid(v)


def _dot(a, b, ca, cb):
    return lax.dot_general(a.astype(MMT), b.astype(MMT), (((ca,), (cb,)), ((), ())), preferred_element_type=F32)


def _mm(a, b, mode, out_dtype, name):
    if mode == "nn":
        (M, K), N = a.shape, b.shape[1]
    elif mode == "nt":
        (M, K), N = a.shape, b.shape[0]
    else:
        (K, M), N = a.shape, b.shape[1]
    tm, tn, tk = _tile(M, 512), _tile(N, 1408), _tile(K, 1536)
    nk = K // tk
    ca, cb = {"nn": (1, 0), "nt": (1, 1), "tn": (0, 0)}[mode]

    def body(a_ref, b_ref, o_ref, acc):
        k = pl.program_id(2)

        @pl.when(k == 0)
        def _():
            acc[...] = jnp.zeros_like(acc)

        acc[...] += _dot(a_ref[...], b_ref[...], ca, cb)

        @pl.when(k == nk - 1)
        def _():
            o_ref[...] = acc[...].astype(o_ref.dtype)

    a_spec = pl.BlockSpec((tk, tm), lambda i, j, k: (k, i)) if mode == "tn" else pl.BlockSpec((tm, tk), lambda i, j, k: (i, k))
    b_spec = pl.BlockSpec((tn, tk), lambda i, j, k: (j, k)) if mode == "nt" else pl.BlockSpec((tk, tn), lambda i, j, k: (k, j))
    return pl.pallas_call(
        body, name=name, grid=(M // tm, N // tn, nk), in_specs=[a_spec, b_spec],
        out_specs=pl.BlockSpec((tm, tn), lambda i, j, k: (i, j)), out_shape=SDS((M, N), out_dtype),
        scratch_shapes=[pltpu.VMEM((tm, tn), F32)], compiler_params=_cparams("parallel", "parallel", "arbitrary"))(a, b)


def _seg_of(nl, nseg):
    return (lambda i: jnp.where(i >= nl, 1, 0)) if nseg == 2 else (lambda i: 0)


def _prenorm(h, mod, gn, which, rows, nl, tm, name):
    D = h.shape[1]
    nseg = mod.shape[0]
    seg = _seg_of(nl, nseg)
    sh_i, sc_i = (0, 1) if which == 0 else (3, 4)

    def body(h_ref, mod_ref, gn_ref, a_ref):
        n, _ = _rms(h_ref[...])
        a_ref[...] = (n * gn_ref[...] * (1.0 + mod_ref[pl.ds(sc_i, 1), :]) + mod_ref[pl.ds(sh_i, 1), :]).astype(a_ref.dtype)

    return pl.pallas_call(
        body, name=name, grid=(rows // tm,),
        in_specs=[pl.BlockSpec((tm, D), lambda i: (i, 0)), pl.BlockSpec((None, 6, D), lambda i: (seg(i), 0, 0)),
                  pl.BlockSpec((1, D), lambda i: (0, 0))],
        out_specs=pl.BlockSpec((tm, D), lambda i: (i, 0)), out_shape=SDS((rows, D), MMT),
        compiler_params=_cparams("parallel"))(h, mod, gn)


def _acc_spec(D, seg):
    return pl.BlockSpec((None, 1, D), lambda i: (seg(i), 0, 0))


def _prenorm_bwd(h, da, dh_in, mod, gn, which, rows, nl, tm, name):
    D = h.shape[1]
    nseg = mod.shape[0]
    seg = _seg_of(nl, nseg)
    sc_i = 1 if which == 0 else 4

    def body(h_ref, da_ref, dhin_ref, mod_ref, gn_ref, dh_ref, dsh_ref, dsc_ref, dgn_ref):
        i = pl.program_id(0)
        first = (i == 0) | (i == nl) if nseg == 2 else (i == 0)

        @pl.when(first)
        def _():
            dsh_ref[...] = jnp.zeros_like(dsh_ref)
            dsc_ref[...] = jnp.zeros_like(dsc_ref)
            dgn_ref[...] = jnp.zeros_like(dgn_ref)

        n, r = _rms(h_ref[...])
        da_v = da_ref[...].astype(F32)
        gn_v = gn_ref[...]
        sc1 = 1.0 + mod_ref[pl.ds(sc_i, 1), :]
        dsh_ref[...] += _sum0(da_v)
        dsc_ref[...] += _sum0(da_v * (n * gn_v))
        dgn_ref[...] += _sum0(da_v * n * sc1)
        dn = da_v * (gn_v * sc1)
        dh_ref[...] = dhin_ref[...] + r * (dn - n * jnp.mean(dn * n, axis=-1, keepdims=True))

    row = pl.BlockSpec((tm, D), lambda i: (i, 0))
    acc = SDS((nseg, 1, D), F32)
    return pl.pallas_call(
        body, name=name, grid=(rows // tm,),
        in_specs=[row, row, row, pl.BlockSpec((None, 6, D), lambda i: (seg(i), 0, 0)), pl.BlockSpec((1, D), lambda i: (0, 0))],
        out_specs=[row, _acc_spec(D, seg), _acc_spec(D, seg), _acc_spec(D, seg)],
        out_shape=[SDS((rows, D), F32), acc, acc, acc], compiler_params=_cparams("arbitrary"))(h, da, dh_in, mod, gn)


def _postnorm(h, y, bias, mod, gn, gate_i, rows, nl, tm, name):
    D = h.shape[1]
    nseg = mod.shape[0]
    seg = _seg_of(nl, nseg)

    def body(h_ref, y_ref, b_ref, mod_ref, gn_ref, o_ref):
        ny, _ = _rms(y_ref[...] + b_ref[...])
        o_ref[...] = h_ref[...] + mod_ref[pl.ds(gate_i, 1), :] * (ny * gn_ref[...])

    row = pl.BlockSpec((tm, D), lambda i: (i, 0))
    vec = pl.BlockSpec((1, D), lambda i: (0, 0))
    return pl.pallas_call(
        body, name=name, grid=(rows // tm,),
        in_specs=[row, row, vec, pl.BlockSpec((None, 6, D), lambda i: (seg(i), 0, 0)), vec],
        out_specs=row, out_shape=SDS((rows, D), F32), compiler_params=_cparams("parallel"))(h, y, bias, mod, gn)


def _postnorm_bwd(dh, y, bias, mod, gn, gate_i, rows, nl, tm, name):
    D = y.shape[1]
    nseg = mod.shape[0]
    seg = _seg_of(nl, nseg)

    def body(dh_ref, y_ref, b_ref, mod_ref, gn_ref, dy_ref, dg_ref, dgn_ref, db_ref):
        i = pl.program_id(0)
        first = (i == 0) | (i == nl) if nseg == 2 else (i == 0)

        @pl.when(first)
        def _():
            dg_ref[...] = jnp.zeros_like(dg_ref)
            dgn_ref[...] = jnp.zeros_like(dgn_ref)
            db_ref[...] = jnp.zeros_like(db_ref)

        ny, ry = _rms(y_ref[...] + b_ref[...])
        g = mod_ref[pl.ds(gate_i, 1), :]
        gn_v = gn_ref[...]
        dh_v = dh_ref[...]
        dg_ref[...] += _sum0(dh_v * (ny * gn_v))
        dgn_ref[...] += _sum0(dh_v * ny * g)
        dny = dh_v * (g * gn_v)
        dy = ry * (dny - ny * jnp.mean(dny * ny, axis=-1, keepdims=True))
        db_ref[...] += _sum0(dy)
        dy_ref[...] = dy.astype(dy_ref.dtype)

    row = pl.BlockSpec((tm, D), lambda i: (i, 0))
    vec = pl.BlockSpec((1, D), lambda i: (0, 0))
    acc = SDS((nseg, 1, D), F32)
    return pl.pallas_call(
        body, name=name, grid=(rows // tm,),
        in_specs=[row, row, vec, pl.BlockSpec((None, 6, D), lambda i: (seg(i), 0, 0)), vec],
        out_specs=[row, _acc_spec(D, seg), _acc_spec(D, seg), _acc_spec(D, seg)],
        out_shape=[SDS((rows, D), MMT), acc, acc, acc], compiler_params=_cparams("arbitrary"))(dh, y, bias, mod, gn)


def _seg_layout(segs, H):
    out, base = [], H
    for s0, n in segs:
        out.append((s0, n, base))
        base += n + H
    return out, base


def _zero_pads(ref, lay, H):
    width = ref.shape[1]
    ref[pl.ds(0, H), :] = jnp.zeros((H, width), ref.dtype)
    for _, n, base in lay:
        ref[pl.ds(base + n, H), :] = jnp.zeros((H, width), ref.dtype)


def _window(ref, base, off, H):
    return ref[pl.ds(base - H + off, CONV_ROWS + 2 * H), :]


def _taps(win, H, offs):
    W = CONV_ROWS + 2 * H
    rolled, out = {}, {}
    for o in offs:
        s = H + o
        b = s % SUBLANES
        if b not in rolled:
            rolled[b] = win if b == 0 else pltpu.roll(win, shift=W - b, axis=0)
        out[o] = rolled[b][s - b:s - b + CONV_ROWS, :]
    return out


def _chunks(lay, fn):
    for s0, n, base in lay:
        def step(r, carry, s0=s0, base=base):
            fn(s0, base, pl.multiple_of(r * CONV_ROWS, CONV_ROWS))
            return carry
        lax.fori_loop(0, n // CONV_ROWS, step, 0)


def _ffn_gate(z0, conv_w, conv_b, segs, name):
    T, F2 = z0.shape
    F = F2 // 2
    tc = _tile(F, 256)
    nF = F // tc
    H = SUBLANES
    lay, srows = _seg_layout(segs, H)
    offs = [-1, 0, 1]

    def body(zg_ref, zv_ref, wg_ref, wv_ref, bg_ref, bv_ref, u_ref, xg, xv):
        _zero_pads(xg, lay, H)
        _zero_pads(xv, lay, H)
        for s0, n, base in lay:
            xg[pl.ds(base, n), :] = zg_ref[pl.ds(s0, n), :]
            xv[pl.ds(base, n), :] = zv_ref[pl.ds(s0, n), :]

        def chunk(s0, base, off):
            tg = _taps(_window(xg, base, off, H), H, offs)
            tv = _taps(_window(xv, base, off, H), H, offs)
            zg = bg_ref[...] + sum(tg[k - 1] * wg_ref[pl.ds(k, 1), :] for k in range(3))
            zv = bv_ref[...] + sum(tv[k - 1] * wv_ref[pl.ds(k, 1), :] for k in range(3))
            u_ref[pl.ds(s0 + off, CONV_ROWS), :] = (zg * _sig(zg) * zv).astype(u_ref.dtype)

        _chunks(lay, chunk)

    colg = lambda r: pl.BlockSpec((r, tc), lambda j: (0, j))
    colv = lambda r: pl.BlockSpec((r, tc), lambda j: (0, j + nF))
    return pl.pallas_call(
        body, name=name, grid=(nF,),
        in_specs=[colg(T), colv(T), colg(3), colv(3), colg(1), colv(1)],
        out_specs=colg(T), out_shape=SDS((T, F), MMT),
        scratch_shapes=[pltpu.VMEM((srows, tc), F32), pltpu.VMEM((srows, tc), F32)],
        compiler_params=_cparams("parallel"))(z0, z0, conv_w, conv_w, conv_b, conv_b)


def _ffn_gate_bwd(z0, du, conv_w, conv_b, segs, name):
    T, F2 = z0.shape
    F = F2 // 2
    tc = _tile(F, 256)
    nF = F // tc
    H = SUBLANES
    lay, srows = _seg_layout(segs, H)
    offs = [-1, 0, 1]

    def body(zo_ref, zt_ref, du_ref, wo_ref, wt_ref, bo_ref, bt_ref, u_ref, dz0_ref, dw_ref, db_ref, xo, xt, dzp):
        own_is_gate = pl.program_id(1) == 0
        for ref in (xo, xt, dzp):
            _zero_pads(ref, lay, H)
        for s0, n, base in lay:
            xo[pl.ds(base, n), :] = zo_ref[pl.ds(s0, n), :]
            xt[pl.ds(base, n), :] = zt_ref[pl.ds(s0, n), :]

        def grads(s0, base, off):
            to = _taps(_window(xo, base, off, H), H, offs)
            tt = _taps(_window(xt, base, off, H), H, offs)
            zo = bo_ref[...] + sum(to[k - 1] * wo_ref[pl.ds(k, 1), :] for k in range(3))
            zt = bt_ref[...] + sum(tt[k - 1] * wt_ref[pl.ds(k, 1), :] for k in range(3))
            so, st = _sig(zo), _sig(zt)
            du_v = du_ref[pl.ds(s0 + off, CONV_ROWS), :]
            d_gate = du_v * zt * (so * (1.0 + zo * (1.0 - so)))
            d_val = du_v * (zt * st)
            dzp[pl.ds(base + off, CONV_ROWS), :] = jnp.where(own_is_gate, d_gate, d_val)

            @pl.when(own_is_gate)
            def _():
                u_ref[pl.ds(s0 + off, CONV_ROWS), :] = (zo * so * zt).astype(u_ref.dtype)

        _chunks(lay, grads)
        dw_ref[...] = jnp.zeros_like(dw_ref)
        db_ref[...] = jnp.zeros_like(db_ref)

        def back(s0, base, off):
            td = _taps(_window(dzp, base, off, H), H, offs)
            tx = _taps(_window(xo, base, off, H), H, offs)
            dz0 = sum(td[1 - k] * wo_ref[pl.ds(k, 1), :] for k in range(3))
            dz0_ref[pl.ds(s0 + off, CONV_ROWS), :] = dz0.astype(dz0_ref.dtype)
            db_ref[...] += _sum0(td[0])
            for k in range(3):
                dw_ref[pl.ds(k, 1), :] += _sum0(td[0] * tx[k - 1])

        _chunks(lay, back)

    own = lambda r: pl.BlockSpec((r, tc), lambda j, hf: (0, hf * nF + j))
    oth = lambda r: pl.BlockSpec((r, tc), lambda j, hf: (0, (1 - hf) * nF + j))
    ucol = pl.BlockSpec((T, tc), lambda j, hf: (0, j))
    return pl.pallas_call(
        body, name=name, grid=(nF, 2),
        in_specs=[own(T), oth(T), ucol, own(3), oth(3), own(1), oth(1)],
        out_specs=[ucol, own(T), own(3), own(1)],
        out_shape=[SDS((T, F), MMT), SDS((T, F2), MMT), SDS((3, F2), F32), SDS((1, F2), F32)],
        scratch_shapes=[pltpu.VMEM((srows, tc), F32)] * 3,
        compiler_params=_cparams("parallel", "arbitrary"))(z0, z0, du, conv_w, conv_w, conv_b, conv_b)


def _glu_conv(p0, b_in, dw_w, dw_b, segs, name):
    T, D2 = p0.shape
    D = D2 // 2
    tc = _tile(D, 256)
    nD = D // tc
    H = 2 * SUBLANES
    half = (CONV_WIDTH - 1) // 2
    lay, srows = _seg_layout(segs, H)
    offs = list(range(-half, half + 1))

    def body(pa_ref, pg_ref, ba_ref, bg_ref, w_ref, b_ref, z2_ref, z1p):
        _zero_pads(z1p, lay, H)

        def glu(s0, base, off):
            rows = pl.ds(s0 + off, CONV_ROWS)
            z1p[pl.ds(base + off, CONV_ROWS), :] = (pa_ref[rows, :] + ba_ref[...]) * _sig(pg_ref[rows, :] + bg_ref[...])

        _chunks(lay, glu)

        def conv(s0, base, off):
            t = _taps(_window(z1p, base, off, H), H, offs)
            acc = b_ref[...] + t[-half] * w_ref[pl.ds(0, 1), :]
            for k in range(1, CONV_WIDTH):
                acc = acc + t[k - half] * w_ref[pl.ds(k, 1), :]
            z2_ref[pl.ds(s0 + off, CONV_ROWS), :] = acc

        _chunks(lay, conv)

    cola = lambda r: pl.BlockSpec((r, tc), lambda j: (0, j))
    colg = lambda r: pl.BlockSpec((r, tc), lambda j: (0, j + nD))
    return pl.pallas_call(
        body, name=name, grid=(nD,),
        in_specs=[cola(T), colg(T), cola(1), colg(1), cola(CONV_WIDTH), cola(1)],
        out_specs=cola(T), out_shape=SDS((T, D), F32), scratch_shapes=[pltpu.VMEM((srows, tc), F32)],
        compiler_params=_cparams("parallel"))(p0, p0, b_in, b_in, dw_w, dw_b)


def _glu_conv_bwd(p0, b_in, dw_w, dz2, segs, name):
    T, D2 = p0.shape
    D = D2 // 2
    tc = _tile(D, 256)
    nD = D // tc
    H = 2 * SUBLANES
    half = (CONV_WIDTH - 1) // 2
    lay, srows = _seg_layout(segs, H)
    offs = list(range(-half, half + 1))

    def body(pa_ref, pg_ref, ba_ref, bg_ref, w_ref, dz2_ref, dpa_ref, dpg_ref, dw_ref, db_ref, dba_ref, dbg_ref, z1p, dzp):
        _zero_pads(z1p, lay, H)
        _zero_pads(dzp, lay, H)
        for s0, n, base in lay:
            dzp[pl.ds(base, n), :] = dz2_ref[pl.ds(s0, n), :]

        def glu(s0, base, off):
            rows = pl.ds(s0 + off, CONV_ROWS)
            z1p[pl.ds(base + off, CONV_ROWS), :] = (pa_ref[rows, :] + ba_ref[...]) * _sig(pg_ref[rows, :] + bg_ref[...])

        _chunks(lay, glu)
        for ref in (dw_ref, db_ref, dba_ref, dbg_ref):
            ref[...] = jnp.zeros_like(ref)

        def back(s0, base, off):
            td = _taps(_window(dzp, base, off, H), H, offs)
            tz = _taps(_window(z1p, base, off, H), H, offs)
            dz1 = td[half] * w_ref[pl.ds(0, 1), :]
            for k in range(1, CONV_WIDTH):
                dz1 = dz1 + td[half - k] * w_ref[pl.ds(k, 1), :]
            db_ref[...] += _sum0(td[0])
            for k in range(CONV_WIDTH):
                dw_ref[pl.ds(k, 1), :] += _sum0(td[0] * tz[k - half])
            rows = pl.ds(s0 + off, CONV_ROWS)
            pa = pa_ref[rows, :] + ba_ref[...]
            sg = _sig(pg_ref[rows, :] + bg_ref[...])
            dpa = dz1 * sg
            dpg = dz1 * pa * (sg * (1.0 - sg))
            dba_ref[...] += _sum0(dpa)
            dbg_ref[...] += _sum0(dpg)
            dpa_ref[rows, :] = dpa.astype(dpa_ref.dtype)
            dpg_ref[rows, :] = dpg.astype(dpg_ref.dtype)

        _chunks(lay, back)

    cola = lambda r: pl.BlockSpec((r, tc), lambda j: (0, j))
    colg = lambda r: pl.BlockSpec((r, tc), lambda j: (0, j + nD))
    return pl.pallas_call(
        body, name=name, grid=(nD,),
        in_specs=[cola(T), colg(T), cola(1), colg(1), cola(CONV_WIDTH), cola(T)],
        out_specs=[cola(T), cola(T), cola(CONV_WIDTH), cola(1), cola(1), cola(1)],
        out_shape=[SDS((T, D), MMT), SDS((T, D), MMT), SDS((CONV_WIDTH, D), F32), SDS((1, D), F32), SDS((1, D), F32),
                   SDS((1, D), F32)],
        scratch_shapes=[pltpu.VMEM((srows, tc), F32)] * 2, compiler_params=_cparams("parallel"))(p0, p0, b_in, b_in, dw_w, dz2)


def _layer_norm_stats(v):
    mu = jnp.mean(v, axis=-1, keepdims=True)
    var = jnp.mean(jnp.square(v - mu), axis=-1, keepdims=True)
    rstd = lax.rsqrt(var + EPS)
    return (v - mu) * rstd, rstd


def _ln_silu(z2, ln_g, ln_b, rows, tm, name):
    D = z2.shape[1]

    def body(z_ref, g_ref, b_ref, o_ref):
        xh, _ = _layer_norm_stats(z_ref[...])
        z3 = xh * g_ref[...] + b_ref[...]
        o_ref[...] = (z3 * _sig(z3)).astype(o_ref.dtype)

    row = pl.BlockSpec((tm, D), lambda i: (i, 0))
    vec = pl.BlockSpec((1, D), lambda i: (0, 0))
    return pl.pallas_call(body, name=name, grid=(rows // tm,), in_specs=[row, vec, vec], out_specs=row,
                          out_shape=SDS((rows, D), MMT), compiler_params=_cparams("parallel"))(z2, ln_g, ln_b)


def _ln_silu_bwd(z2, dz4, ln_g, ln_b, rows, tm, name):
    D = z2.shape[1]

    def body(z_ref, d_ref, g_ref, b_ref, dz_ref, dg_ref, db_ref):
        @pl.when(pl.program_id(0) == 0)
        def _():
            dg_ref[...] = jnp.zeros_like(dg_ref)
            db_ref[...] = jnp.zeros_like(db_ref)

        xh, rstd = _layer_norm_stats(z_ref[...])
        z3 = xh * g_ref[...] + b_ref[...]
        s = _sig(z3)
        dz3 = d_ref[...] * (s * (1.0 + z3 * (1.0 - s)))
        dg_ref[...] += _sum0(dz3 * xh)
        db_ref[...] += _sum0(dz3)
        dxh = dz3 * g_ref[...]
        dz_ref[...] = rstd * (dxh - jnp.mean(dxh, axis=-1, keepdims=True) - xh * jnp.mean(dxh * xh, axis=-1, keepdims=True))

    row = pl.BlockSpec((tm, D), lambda i: (i, 0))
    vec = pl.BlockSpec((1, D), lambda i: (0, 0))
    return pl.pallas_call(body, name=name, grid=(rows // tm,), in_specs=[row, row, vec, vec], out_specs=[row, vec, vec],
                          out_shape=[SDS((rows, D), F32), SDS((1, D), F32), SDS((1, D), F32)],
                          compiler_params=_cparams("arbitrary"))(z2, dz4, ln_g, ln_b)


def _rot_half_pairs(v):
    width = v.shape[1]
    lane = lax.broadcasted_iota(jnp.int32, v.shape, 1)
    return jnp.where((lane % 32) < 16, -pltpu.roll(v, shift=width - 16, axis=1), pltpu.roll(v, shift=16, axis=1))


def _rope(qkv, cos, sin, L, qk, tm, name):
    width = qkv.shape[1]
    kv = width - qk

    def body(x_ref, c_ref, s_ref, qk_ref, v_ref):
        xv = x_ref[:, pl.ds(0, qk)]
        c = jnp.tile(c_ref[...], (1, qk // LANES))
        s = jnp.tile(s_ref[...], (1, qk // LANES))
        qk_ref[...] = (xv * c + _rot_half_pairs(xv) * s).astype(qk_ref.dtype)
        v_ref[...] = x_ref[:, pl.ds(qk, kv)].astype(v_ref.dtype)

    tab = pl.BlockSpec((tm, LANES), lambda i: (i, 0))
    return pl.pallas_call(
        body, name=name, grid=(L // tm,), in_specs=[pl.BlockSpec((tm, width), lambda i: (i, 0)), tab, tab],
        out_specs=[pl.BlockSpec((tm, qk), lambda i: (i, 0)), pl.BlockSpec((tm, kv), lambda i: (i, 0))],
        out_shape=[SDS((L, qk), MMT), SDS((L, kv), MMT)], compiler_params=_cparams("parallel"))(qkv, cos, sin)


def _rope_bwd(dqk, dv, cos, sin, tm, name):
    L, qk = dqk.shape
    kv = dv.shape[1]

    def body(d_ref, dv_ref, c_ref, s_ref, o_ref):
        dv_ = d_ref[...]
        c = jnp.tile(c_ref[...], (1, qk // LANES))
        s = jnp.tile(s_ref[...], (1, qk // LANES))
        o_ref[:, pl.ds(0, qk)] = (dv_ * c - _rot_half_pairs(dv_ * s)).astype(o_ref.dtype)
        o_ref[:, pl.ds(qk, kv)] = dv_ref[...].astype(o_ref.dtype)

    tab = pl.BlockSpec((tm, LANES), lambda i: (i, 0))
    return pl.pallas_call(
        body, name=name, grid=(L // tm,),
        in_specs=[pl.BlockSpec((tm, qk), lambda i: (i, 0)), pl.BlockSpec((tm, kv), lambda i: (i, 0)), tab, tab],
        out_specs=pl.BlockSpec((tm, qk + kv), lambda i: (i, 0)), out_shape=SDS((L, qk + kv), MMT),
        compiler_params=_cparams("parallel"))(dqk, dv, cos, sin)


def _band_specs(nb, width):
    blk = lambda f: pl.BlockSpec((None, ATTN_BLOCK, width), f)
    return [blk(lambda h, n: (h, jnp.maximum(n - 1, 0), 0)), blk(lambda h, n: (h, n, 0)),
            blk(lambda h, n: (h, jnp.minimum(n + 1, nb - 1), 0))]


def _window_mask(n, L):
    qi = lax.broadcasted_iota(jnp.int32, (ATTN_BLOCK, 3 * ATTN_BLOCK), 0)
    kk = lax.broadcasted_iota(jnp.int32, (ATTN_BLOCK, 3 * ATTN_BLOCK), 1)
    key_abs = (n - 1) * ATTN_BLOCK + kk
    return (jnp.abs(qi + ATTN_BLOCK - kk) <= ATTN_BLOCK) & (key_abs >= 0) & (key_abs < L)


def _attn_fwd(q, k, v, kc, vc, sink, name):
    nkv, _, L, hd = q.shape
    C = kc.shape[1]
    nb = L // ATTN_BLOCK
    scale = HEAD_DIM ** -0.5

    def body(sink_ref, q_ref, k0, k1, k2, v0, v1, v2, kc_ref, vc_ref, o_ref, lse_ref):
        hh, n = pl.program_id(0), pl.program_id(1)
        kw = jnp.concatenate([k0[...], k1[...], k2[...]], axis=0)
        vw = jnp.concatenate([v0[...], v1[...], v2[...]], axis=0)
        mask = _window_mask(n, L)
        for g in range(Q_PER_KV):
            qg = q_ref[g]
            sw = jnp.where(mask, _dot(qg, kw, 1, 1) * scale, NEG)
            sc = _dot(qg, kc_ref[...], 1, 1) * scale
            sk = sink_ref[hh * Q_PER_KV + g]
            m = jnp.maximum(jnp.maximum(jnp.max(sw, axis=-1, keepdims=True), jnp.max(sc, axis=-1, keepdims=True)), sk)
            pw, pc = jnp.exp(sw - m), jnp.exp(sc - m)
            den = jnp.sum(pw, axis=-1, keepdims=True) + jnp.sum(pc, axis=-1, keepdims=True) + jnp.exp(sk - m)
            inv = 1.0 / den
            o_ref[g] = _dot(pw * inv, vw, 1, 0) + _dot(pc * inv, vc_ref[...], 1, 0)
            lse_ref[g] = m + jnp.log(den)

    qspec = pl.BlockSpec((None, Q_PER_KV, ATTN_BLOCK, hd), lambda h, n: (h, 0, n, 0))
    cspec = pl.BlockSpec((None, C, hd), lambda h, n: (h, 0, 0))
    return pl.pallas_call(
        body, name=name, grid=(nkv, nb),
        in_specs=[pl.BlockSpec(memory_space=pltpu.SMEM), qspec] + _band_specs(nb, hd) + _band_specs(nb, hd) + [cspec, cspec],
        out_specs=[qspec, pl.BlockSpec((None, Q_PER_KV, ATTN_BLOCK, 1), lambda h, n: (h, 0, n, 0))],
        out_shape=[SDS((nkv, Q_PER_KV, L, hd), F32), SDS((nkv, Q_PER_KV, L, 1), F32)],
        compiler_params=_cparams("parallel", "parallel"))(sink, q, k, k, k, v, v, v, kc, vc)


def _attn_bwd_q(q, k, v, kc, vc, sink, o, do, lse, name):
    nkv, _, L, hd = q.shape
    C = kc.shape[1]
    nb = L // ATTN_BLOCK
    scale = HEAD_DIM ** -0.5

    def body(sink_ref, q_ref, k0, k1, k2, v0, v1, v2, kc_ref, vc_ref, o_ref, do_ref, lse_ref, dq_ref, dkc_ref, dvc_ref, dsk_ref):
        hh, n = pl.program_id(0), pl.program_id(1)

        @pl.when(n == 0)
        def _():
            dkc_ref[...] = jnp.zeros_like(dkc_ref)
            dvc_ref[...] = jnp.zeros_like(dvc_ref)
            dsk_ref[...] = jnp.zeros_like(dsk_ref)

        kw = jnp.concatenate([k0[...], k1[...], k2[...]], axis=0)
        vw = jnp.concatenate([v0[...], v1[...], v2[...]], axis=0)
        mask = _window_mask(n, L)
        for g in range(Q_PER_KV):
            qg, dog, lse_g = q_ref[g], do_ref[g], lse_ref[g]
            delta = jnp.sum(dog.astype(F32) * o_ref[g], axis=-1, keepdims=True)
            pw = jnp.exp(jnp.where(mask, _dot(qg, kw, 1, 1) * scale, NEG) - lse_g)
            pc = jnp.exp(_dot(qg, kc_ref[...], 1, 1) * scale - lse_g)
            dsw = pw * (_dot(dog, vw, 1, 1) - delta)
            dsc = pc * (_dot(dog, vc_ref[...], 1, 1) - delta)
            dq_ref[g] = (_dot(dsw, kw, 1, 0) + _dot(dsc, kc_ref[...], 1, 0)) * scale
            dkc_ref[...] += _dot(dsc, qg, 0, 0) * scale
            dvc_ref[...] += _dot(pc, dog, 0, 0)
            psk = jnp.exp(sink_ref[hh * Q_PER_KV + g] - lse_g)
            dsk_ref[pl.ds(g, 1), :] += jnp.broadcast_to(jnp.sum(-psk * delta, axis=0, keepdims=True), (1, LANES))

    qspec = pl.BlockSpec((None, Q_PER_KV, ATTN_BLOCK, hd), lambda h, n: (h, 0, n, 0))
    lspec = pl.BlockSpec((None, Q_PER_KV, ATTN_BLOCK, 1), lambda h, n: (h, 0, n, 0))
    cspec = pl.BlockSpec((None, C, hd), lambda h, n: (h, 0, 0))
    return pl.pallas_call(
        body, name=name, grid=(nkv, nb),
        in_specs=[pl.BlockSpec(memory_space=pltpu.SMEM), qspec] + _band_specs(nb, hd) + _band_specs(nb, hd)
        + [cspec, cspec, qspec, qspec, lspec],
        out_specs=[qspec, cspec, cspec, pl.BlockSpec((None, SUBLANES, LANES), lambda h, n: (h, 0, 0))],
        out_shape=[SDS((nkv, Q_PER_KV, L, hd), F32), SDS((nkv, C, hd), F32), SDS((nkv, C, hd), F32),
                   SDS((nkv, SUBLANES, LANES), F32)],
        compiler_params=_cparams("parallel", "arbitrary"))(sink, q, k, k, k, v, v, v, kc, vc, o, do, lse)


def _attn_bwd_kv(q, k, v, o, do, lse, name):
    nkv, _, L, hd = q.shape
    nb = L // ATTN_BLOCK
    scale = HEAD_DIM ** -0.5

    def body(q0, q1, q2, do0, do1, do2, o0, o1, o2, l0, l1, l2, k_ref, v_ref, dk_ref, dv_ref):
        j = pl.program_id(1)
        qi = lax.broadcasted_iota(jnp.int32, (ATTN_BLOCK, ATTN_BLOCK), 0)
        kk = lax.broadcasted_iota(jnp.int32, (ATTN_BLOCK, ATTN_BLOCK), 1)
        kj, vj = k_ref[...], v_ref[...]
        dk = jnp.zeros((ATTN_BLOCK, hd), F32)
        dv = jnp.zeros((ATTN_BLOCK, hd), F32)
        for slot, (q_r, do_r, o_r, l_r) in enumerate(((q0, do0, o0, l0), (q1, do1, o1, l1), (q2, do2, o2, l2))):
            n = j - 1 + slot
            ok = (n >= 0) & (n < nb) & (jnp.abs(qi + ATTN_BLOCK - ((2 - slot) * ATTN_BLOCK + kk)) <= ATTN_BLOCK)
            for g in range(Q_PER_KV):
                qg, dog = q_r[g], do_r[g]
                delta = jnp.sum(dog.astype(F32) * o_r[g], axis=-1, keepdims=True)
                p = jnp.exp(jnp.where(ok, _dot(qg, kj, 1, 1) * scale - l_r[g], NEG))
                ds = p * (_dot(dog, vj, 1, 1) - delta)
                dk = dk + _dot(ds, qg, 0, 0) * scale
                dv = dv + _dot(p, dog, 0, 0)
        dk_ref[...] = dk
        dv_ref[...] = dv

    def band(width):
        blk = lambda f: pl.BlockSpec((None, Q_PER_KV, ATTN_BLOCK, width), f)
        return [blk(lambda h, j: (h, 0, jnp.maximum(j - 1, 0), 0)), blk(lambda h, j: (h, 0, j, 0)),
                blk(lambda h, j: (h, 0, jnp.minimum(j + 1, nb - 1), 0))]

    kspec = pl.BlockSpec((None, ATTN_BLOCK, hd), lambda h, j: (h, j, 0))
    return pl.pallas_call(
        body, name=name, grid=(nkv, nb), in_specs=band(hd) + band(hd) + band(hd) + band(1) + [kspec, kspec],
        out_specs=[kspec, kspec], out_shape=[SDS((nkv, L, hd), F32), SDS((nkv, L, hd), F32)],
        compiler_params=_cparams("parallel", "parallel"))(q, q, q, do, do, do, o, o, o, lse, lse, lse, k, v)


_GELU_K = math.sqrt(2.0 / math.pi)


def _gelu(v):
    return 0.5 * v * (1.0 + jnp.tanh(_GELU_K * (v + 0.044715 * (v * v * v))))


def _gelu_grad(v):
    t = jnp.tanh(_GELU_K * (v + 0.044715 * (v * v * v)))
    return 0.5 * (1.0 + t) + 0.5 * v * (1.0 - t * t) * (_GELU_K * (1.0 + 3.0 * 0.044715 * (v * v)))


def _gmlp_fwd(p0, b_in, ln_g, ln_b, w_s, b_s, name):
    L, W2 = p0.shape
    W = W2 // 2
    G = W // GMLP_GROUP_DIM

    def body(p_ref, bi_ref, g_ref, b_ref, ws_ref, bs_ref, o_ref):
        ge = _gelu(p_ref[...] + bi_ref[...])
        xh, _ = _layer_norm_stats(ge[:, W:])
        vln = xh * g_ref[...] + b_ref[...]
        for gi in range(G):
            cols = slice(gi * GMLP_GROUP_DIM, (gi + 1) * GMLP_GROUP_DIM)
            s = _dot(ws_ref[gi], vln[:, cols], 1, 0) + bs_ref[gi]
            o_ref[:, cols] = (ge[:, cols] * s).astype(o_ref.dtype)

    full = lambda shape: pl.BlockSpec(shape, lambda i: (0,) * len(shape))
    return pl.pallas_call(
        body, name=name, grid=(L // GMLP_CHUNK,),
        in_specs=[pl.BlockSpec((GMLP_CHUNK, W2), lambda i: (i, 0)), full((1, W2)), full((1, W)), full((1, W)),
                  full((G, GMLP_CHUNK, GMLP_CHUNK)), full((G, GMLP_CHUNK, 1))],
        out_specs=pl.BlockSpec((GMLP_CHUNK, W), lambda i: (i, 0)), out_shape=SDS((L, W), MMT),
        compiler_params=_cparams("parallel"))(p0, b_in, ln_g, ln_b, w_s, b_s)


def _gmlp_bwd(p0, dus, b_in, ln_g, ln_b, w_s, w_st, b_s, name):
    L, W2 = p0.shape
    W = W2 // 2
    G = W // GMLP_GROUP_DIM

    def body(p_ref, d_ref, bi_ref, g_ref, b_ref, ws_ref, wst_ref, bs_ref, dpre_ref, dbi_ref, dg_ref, db_ref, dws_ref, dbs_ref, dvln):
        @pl.when(pl.program_id(0) == 0)
        def _():
            for ref in (dbi_ref, dg_ref, db_ref, dws_ref, dbs_ref):
                ref[...] = jnp.zeros_like(ref)

        pre = p_ref[...] + bi_ref[...]
        ge = _gelu(pre)
        xh, rstd = _layer_norm_stats(ge[:, W:])
        vln = xh * g_ref[...] + b_ref[...]
        dge_u = []
        for gi in range(G):
            cols = slice(gi * GMLP_GROUP_DIM, (gi + 1) * GMLP_GROUP_DIM)
            vg = vln[:, cols]
            s = _dot(ws_ref[gi], vg, 1, 0) + bs_ref[gi]
            dus_g = d_ref[:, cols]
            dge_u.append(dus_g * s)
            ds = dus_g * ge[:, cols]
            dbs_ref[gi] += jnp.sum(ds, axis=1, keepdims=True)
            dws_ref[gi] += _dot(ds, vg, 1, 1)
            dvln[:, cols] = _dot(wst_ref[gi], ds, 1, 0)
        dv = dvln[...]
        dg_ref[...] += _sum0(dv * xh)
        db_ref[...] += _sum0(dv)
        dxh = dv * g_ref[...]
        dv0 = rstd * (dxh - jnp.mean(dxh, axis=-1, keepdims=True) - xh * jnp.mean(dxh * xh, axis=-1, keepdims=True))
        dpre = jnp.concatenate(dge_u + [dv0], axis=1) * _gelu_grad(pre)
        dbi_ref[...] += _sum0(dpre)
        dpre_ref[...] = dpre.astype(dpre_ref.dtype)

    full = lambda shape: pl.BlockSpec(shape, lambda i: (0,) * len(shape))
    mats = (G, GMLP_CHUNK, GMLP_CHUNK)
    return pl.pallas_call(
        body, name=name, grid=(L // GMLP_CHUNK,),
        in_specs=[pl.BlockSpec((GMLP_CHUNK, W2), lambda i: (i, 0)), pl.BlockSpec((GMLP_CHUNK, W), lambda i: (i, 0)),
                  full((1, W2)), full((1, W)), full((1, W)), full(mats), full(mats), full((G, GMLP_CHUNK, 1))],
        out_specs=[pl.BlockSpec((GMLP_CHUNK, W2), lambda i: (i, 0)), full((1, W2)), full((1, W)), full((1, W)), full(mats),
                   full((G, GMLP_CHUNK, 1))],
        out_shape=[SDS((L, W2), MMT), SDS((1, W2), F32), SDS((1, W), F32), SDS((1, W), F32), SDS(mats, F32),
                   SDS((G, GMLP_CHUNK, 1), F32)],
        scratch_shapes=[pltpu.VMEM((GMLP_CHUNK, W), F32)], compiler_params=_cparams("arbitrary"))(
            p0, dus, b_in, ln_g, ln_b, w_s, w_st, b_s)


def _loss_head(h, target, tm, name):
    L, D = h.shape

    def body(h_ref, t_ref, l_ref, d_ref):
        @pl.when(pl.program_id(0) == 0)
        def _():
            l_ref[...] = jnp.zeros_like(l_ref)

        e = h_ref[...] - t_ref[...]
        l_ref[...] += 0.5 * jnp.sum(jnp.mean(e * e, axis=-1, keepdims=True), axis=0, keepdims=True)
        d_ref[...] = e * (1.0 / D)

    row = pl.BlockSpec((tm, D), lambda i: (i, 0))
    return pl.pallas_call(body, name=name, grid=(L // tm,), in_specs=[row, row],
                          out_specs=[pl.BlockSpec((1, 1), lambda i: (0, 0)), row],
                          out_shape=[SDS((1, 1), F32), SDS((L, D), F32)], compiler_params=_cparams("arbitrary"))(h, target)


def _ada_fwd(cond, ada_w, ada_b, name):
    NL, D, n = ada_w.shape
    tn = _tile(n, 768)

    def body(c_ref, w_ref, b_ref, o_ref):
        cv = c_ref[...]
        o_ref[...] = _dot(cv * _sig(cv), w_ref[...], 1, 0) + b_ref[...]

    return pl.pallas_call(
        body, name=name, grid=(NL, n // tn),
        in_specs=[pl.BlockSpec((2 * SUBLANES, D), lambda i, j: (0, 0)), pl.BlockSpec((None, D, tn), lambda i, j: (i, 0, j)),
                  pl.BlockSpec((None, 1, tn), lambda i, j: (i, 0, j))],
        out_specs=pl.BlockSpec((None, 2 * SUBLANES, tn), lambda i, j: (i, 0, j)), out_shape=SDS((NL, 2 * SUBLANES, n), F32),
        compiler_params=_cparams("parallel", "parallel"))(cond, ada_w, ada_b)


def _ada_bwd(cond, ada_w, dm_lat, dm_ctx, name):
    NL, D, n = ada_w.shape
    tn = _tile(n, 768)

    def body(c_ref, w_ref, dl_ref, dc_ref, dw_ref, ds_ref):
        @pl.when((pl.program_id(0) == 0) & (pl.program_id(1) == 0))
        def _():
            ds_ref[...] = jnp.zeros_like(ds_ref)

        cv = c_ref[...]
        row = lax.broadcasted_iota(jnp.int32, (SUBLANES, tn), 0)
        ctx_rows = jnp.where(row == 0, _sum0(dc_ref[...]), 0.0)
        dm = jnp.concatenate([dl_ref[...], ctx_rows], axis=0)
        dw_ref[...] = _dot(cv * _sig(cv), dm, 0, 0)
        ds_ref[...] += _dot(dm, w_ref[...], 1, 1)

    dspec = pl.BlockSpec((None, SUBLANES, tn), lambda i, j: (i, 0, j))
    return pl.pallas_call(
        body, name=name, grid=(NL, n // tn),
        in_specs=[pl.BlockSpec((2 * SUBLANES, D), lambda i, j: (0, 0)), pl.BlockSpec((None, D, tn), lambda i, j: (i, 0, j)),
                  dspec, dspec],
        out_specs=[pl.BlockSpec((None, D, tn), lambda i, j: (i, 0, j)), pl.BlockSpec((2 * SUBLANES, D), lambda i, j: (0, 0))],
        out_shape=[SDS((NL, D, n), F32), SDS((2 * SUBLANES, D), F32)],
        compiler_params=_cparams("arbitrary", "arbitrary"))(cond, ada_w, dm_lat, dm_ctx)


def _adam_math(w, g, m, v):
    m = ADAM_B1 * m + (1.0 - ADAM_B1) * g
    v = ADAM_B2 * v + (1.0 - ADAM_B2) * jnp.square(g)
    m_hat = m / (1.0 - ADAM_B1 ** ADAM_STEP)
    v_hat = v / (1.0 - ADAM_B2 ** ADAM_STEP)
    return -ADAM_LR * (m_hat / (jnp.sqrt(v_hat) + ADAM_EPS) + ADAM_WD * w), m, v


def _row_tile(rows, cols, elems):
    want = max(SUBLANES, elems // cols)
    best = SUBLANES if rows % SUBLANES == 0 else rows
    for d in range(SUBLANES, min(rows, want) + 1, SUBLANES):
        if rows % d == 0:
            best = d
    return best


def _adamw(w, m, v, parts, name):
    R, C = w.shape
    tr = _row_tile(R, C, 128 * 1024)
    npart = len(parts)

    def body(*refs):
        w_ref, m_ref, v_ref = refs[:3]
        g_ref, d_ref, nm_ref, nv_ref = refs[3 + npart:]
        g = refs[3][...]
        for p_ref in refs[4:3 + npart]:
            g = g + p_ref[...]
        d, nm, nv = _adam_math(w_ref[...], g, m_ref[...], v_ref[...])
        g_ref[...], d_ref[...], nm_ref[...], nv_ref[...] = g, d, nm, nv

    blk = pl.BlockSpec((tr, C), lambda i: (i, 0))
    return pl.pallas_call(body, name=name, grid=(R // tr,), in_specs=[blk] * (3 + npart), out_specs=[blk] * 4,
                          out_shape=[SDS((R, C), F32)] * 4, compiler_params=_cparams("parallel"))(w, m, v, *parts)


def _adamw_layer(w, m, v, layer, parts, prev, name):
    _, R, C = w.shape
    tr = _row_tile(R, C, 128 * 1024)
    npart = len(parts)
    nprev = 0 if prev is None else 4

    def body(*refs):
        w_ref, m_ref, v_ref = refs[:3]
        g_ref, d_ref, nm_ref, nv_ref = refs[3 + npart + nprev:]
        g = refs[3][...]
        for p_ref in refs[4:3 + npart]:
            g = g + p_ref[...]
        d, nm, nv = _adam_math(w_ref[...], g, m_ref[...], v_ref[...])
        g_ref[...], d_ref[...], nm_ref[...], nv_ref[...] = g, d, nm, nv

    stacked = pl.BlockSpec((None, tr, C), lambda i: (layer, i, 0))
    flat = pl.BlockSpec((tr, C), lambda i: (i, 0))
    return pl.pallas_call(
        body, name=name, grid=(R // tr,),
        in_specs=[stacked] * 3 + [flat] * npart + [pl.BlockSpec(memory_space=pl.ANY)] * nprev, out_specs=[stacked] * 4,
        out_shape=[SDS(w.shape, F32)] * 4, input_output_aliases={3 + npart + k: k for k in range(nprev)},
        compiler_params=_cparams("parallel"))(w, m, v, *parts, *(prev or ()))


def _sum_slots(x, name, after=None):
    S, R, C = x.shape
    tr = _row_tile(R, C, 128 * 1024)
    extra = [] if after is None else [after]

    def body(x_ref, *rest):
        o_ref = rest[-1]
        acc = x_ref[0].astype(F32)
        for s in range(1, S):
            acc = acc + x_ref[s].astype(F32)
        o_ref[...] = acc

    return pl.pallas_call(
        body, name=name, grid=(R // tr,),
        in_specs=[pl.BlockSpec((S, tr, C), lambda i: (0, i, 0))] + [pl.BlockSpec(memory_space=pl.ANY)] * len(extra),
        out_specs=pl.BlockSpec((tr, C), lambda i: (i, 0)), out_shape=SDS((R, C), F32),
        compiler_params=_cparams("parallel"))(x, *extra)


def _my_place():
    return lax.axis_index("x"), lax.axis_index("y"), lax.axis_index("c")


def _other_chips(x, y):
    return [(1 - x, y), (x, 1 - y), (1 - x, 1 - y)]


def _all_gather(v, name):
    R, C = v.shape

    def body(v_ref, o_ref, send_sems, recv_sems, local_sem):
        x, y, c = _my_place()
        me = 4 * x + 2 * y + c
        mine = pltpu.make_async_copy(v_ref, o_ref.at[me], local_sem)
        mine.start()
        copies = []
        for flip in range(1, N_DEV):
            fx, fy, fc = (flip >> 2) & 1, (flip >> 1) & 1, flip & 1
            peer = ((x + fx) % 2, (y + fy) % 2, (c + fc) % 2)
            cp = pltpu.make_async_remote_copy(src_ref=v_ref, dst_ref=o_ref.at[me], send_sem=send_sems.at[flip - 1],
                                              recv_sem=recv_sems.at[flip - 1], device_id=peer, device_id_type=MESH)
            cp.start()
            copies.append(cp)
        for cp in copies:
            cp.wait()
        mine.wait()

    return pl.pallas_call(
        body, name=name, in_specs=[pl.BlockSpec(memory_space=pl.ANY)], out_specs=pl.BlockSpec(memory_space=pl.ANY),
        out_shape=SDS((N_DEV, R, C), v.dtype),
        scratch_shapes=[pltpu.SemaphoreType.DMA((N_DEV - 1,)), pltpu.SemaphoreType.DMA((N_DEV - 1,)), pltpu.SemaphoreType.DMA],
        )(v)


def _shard_window(ref, axis, j, size):
    idx = [slice(None)] * len(ref.shape)
    idx[axis] = pl.ds(pl.multiple_of(j * size, SUBLANES), size)
    return ref.at[tuple(idx)]


def _gather_plan(axis):
    def plan(srcs, lands):
        x, y, c = _my_place()
        dst = _shard_window(lands[0], axis, 2 * x + y, srcs[0].shape[axis])
        return [(srcs[0], dst)], [(srcs[0], dst, (px, py, c)) for px, py in _other_chips(x, y)]
    return plan


def _scatter_plan(axis):
    def plan(srcs, lands):
        x, y, c = _my_place()
        j = 2 * x + y
        size = srcs[0].shape[axis] // N_CHIPS
        local = [(_shard_window(srcs[0], axis, j, size), lands[0].at[j])]
        remote = [(_shard_window(srcs[0], axis, 2 * px + py, size), lands[0].at[j], (px, py, c)) for px, py in _other_chips(x, y)]
        return local, remote
    return plan


def _all_gather_plan(srcs, lands):
    x, y, c = _my_place()
    dst = lands[0].at[4 * x + 2 * y + c]
    remote = []
    for flip in range(1, N_DEV):
        fx, fy, fc = (flip >> 2) & 1, (flip >> 1) & 1, flip & 1
        remote.append((srcs[0], dst, ((x + fx) % 2, (y + fy) % 2, (c + fc) % 2)))
    return [(srcs[0], dst)], remote


def _same_core_peers():
    x, y, c = _my_place()
    return [(px, py, c) for px, py in _other_chips(x, y)]


def _all_peers():
    x, y, c = _my_place()
    return [((x + (f >> 2 & 1)) % 2, (y + (f >> 1 & 1)) % 2, (c + (f & 1)) % 2) for f in range(1, N_DEV)]


def _tensorcore_exchange(name, exchanges):
    srcs = [s for e in exchanges for s in e[0]]
    shapes = [s for e in exchanges for s in e[1]]
    ns, nl = len(srcs), len(shapes)
    first = [sum(e[3] for e in exchanges[:i]) for i in range(len(exchanges))]
    ncopy = sum(e[3] for e in exchanges)

    def body(*refs):
        src_refs, land_refs = refs[:ns], refs[ns:ns + nl]
        send_sems, recv_sems, local_sem = refs[ns + nl:]
        si = li = 0
        plans = []
        for e_srcs, e_lands, plan, _ in exchanges:
            plans.append(plan(src_refs[si:si + len(e_srcs)], land_refs[li:li + len(e_lands)]))
            si, li = si + len(e_srcs), li + len(e_lands)
        for local, _ in plans:
            for src, dst in local:
                cp = pltpu.make_async_copy(src, dst, local_sem)
                cp.start()
                cp.wait()
        copies = []
        for i, (_, remote) in enumerate(plans):
            assert len(remote) == exchanges[i][3]
            for k, (src, dst, peer) in enumerate(remote):
                cp = pltpu.make_async_remote_copy(src_ref=src, dst_ref=dst, send_sem=send_sems.at[first[i] + k],
                                                  recv_sem=recv_sems.at[first[i] + k], device_id=peer, device_id_type=MESH)
                cp.start()
                copies.append(cp)
        for cp in copies:
            cp.wait()

    any_spec = pl.BlockSpec(memory_space=pl.ANY)
    res = pl.pallas_call(
        body, name=name, in_specs=[any_spec] * ns, out_specs=[any_spec] * nl, out_shape=list(shapes),
        scratch_shapes=[pltpu.SemaphoreType.DMA((ncopy,)), pltpu.SemaphoreType.DMA((ncopy,)), pltpu.SemaphoreType.DMA])(*srcs)
    out, li = [], 0
    for e in exchanges:
        out.append(list(res[li:li + len(e[1])]))
        li += len(e[1])
    return out


def _sequencer_exchange(name, collective_id, exchanges, peers_fn):
    hbm = pltpu.MemorySpace.HBM
    src_refs = [[jax.new_ref(s, memory_space=hbm) for s in e[0]] for e in exchanges]
    land_refs = [[jax.empty_ref(s, memory_space=hbm) for s in e[1]] for e in exchanges]
    first = [sum(e[3] for e in exchanges[:i]) for i in range(len(exchanges))]
    ncopy = sum(e[3] for e in exchanges)

    @pl.kernel(mesh=plsc.ScalarSubcoreMesh(axis_name="sequencer", num_cores=N_SEQUENCERS), name=name,
               scratch_types=(pltpu.SemaphoreType.DMA((ncopy,)), pltpu.SemaphoreType.DMA((ncopy,)), pltpu.SemaphoreType.DMA),
               compiler_params=pltpu.CompilerParams(collective_id=collective_id))
    def launch(send_sems, recv_sems, local_sem):
        me = lax.axis_index("sequencer")
        peers = peers_fn()
        barrier = pltpu.get_barrier_semaphore()
        for peer in peers:
            pl.semaphore_signal(barrier, inc=1, device_id=peer, device_id_type=MESH)
        pl.semaphore_wait(barrier, len(peers))
        plans = [e[2](src_refs[i], land_refs[i]) for i, e in enumerate(exchanges)]
        for s in range(N_SEQUENCERS):
            @pl.when(me == s)
            def _(s=s):
                q = 0
                for local, _ in plans:
                    for src, dst in local:
                        if q % N_SEQUENCERS == s:
                            cp = pltpu.make_async_copy(src, dst, local_sem)
                            cp.start()
                            cp.wait()
                        q += 1
                copies = []
                for i, (_, remote) in enumerate(plans):
                    assert len(remote) == exchanges[i][3]
                    for k, (src, dst, peer) in enumerate(remote):
                        if (first[i] + k) % N_SEQUENCERS == s:
                            cp = pltpu.make_async_remote_copy(src_ref=src, dst_ref=dst, send_sem=send_sems.at[first[i] + k],
                                                              recv_sem=recv_sems.at[first[i] + k], device_id=peer,
                                                              device_id_type=MESH)
                            cp.start()
                            copies.append(cp)
                for cp in copies:
                    cp.wait()

    launch()
    return [[r[...] for r in refs] for refs in land_refs]


def _swap_with_sibling(parts, name):
    nt = len(parts)

    def body(*refs):
        ins, outs = refs[:nt], refs[nt:2 * nt]
        send_sems, recv_sems = refs[2 * nt:]
        x, y, c = _my_place()
        copies = []
        for t in range(nt):
            cp = pltpu.make_async_remote_copy(src_ref=ins[t], dst_ref=outs[t], send_sem=send_sems.at[t], recv_sem=recv_sems.at[t],
                                              device_id=(x, y, 1 - c), device_id_type=MESH)
            cp.start()
            copies.append(cp)
        for cp in copies:
            cp.wait()

    any_spec = pl.BlockSpec(memory_space=pl.ANY)
    return pl.pallas_call(
        body, name=name, in_specs=[any_spec] * nt, out_specs=[any_spec] * nt, out_shape=[SDS(p.shape, p.dtype) for p in parts],
        scratch_shapes=[pltpu.SemaphoreType.DMA((nt,)), pltpu.SemaphoreType.DMA((nt,))],
        )(*parts)


PACK_COLS = 1024


def _pack(arrays):
    flat = jnp.concatenate([a.reshape(-1) for a in arrays])
    pad = (-flat.shape[0]) % (SUBLANES * PACK_COLS)
    return jnp.pad(flat, (0, pad)).reshape(-1, PACK_COLS)


def _unpack(packed, shapes):
    flat, out, pos = packed.reshape(-1), [], 0
    for shape in shapes:
        n = math.prod(shape)
        out.append(flat[pos:pos + n].reshape(shape))
        pos += n
    return out


def _unshard_last(stacked):
    moved = jnp.moveaxis(stacked, 0, -2)
    return moved.reshape(moved.shape[:-2] + (moved.shape[-2] * moved.shape[-1],))


def _my_block_last(full, j):
    s = full.shape[-1] // N_CHIPS
    return lax.dynamic_index_in_dim(full.reshape(full.shape[:-1] + (N_CHIPS, s)), j, axis=full.ndim - 1, keepdims=False)


def _rope_tables(L):
    rows = L // GRID_W
    row = jnp.repeat(jnp.arange(rows), GRID_W).astype(F32)
    col = jnp.tile(jnp.arange(GRID_W), rows).astype(F32)
    axis_dim = HEAD_DIM // 2
    inv_freq = ROPE_BASE ** (-jnp.arange(0, axis_dim, 2, dtype=F32) / axis_dim)
    ang_r, ang_c = row[:, None] * inv_freq[None, :], col[:, None] * inv_freq[None, :]
    ang = jnp.concatenate([ang_r, ang_r, ang_c, ang_c] * 2, axis=-1)
    return jnp.cos(ang), jnp.sin(ang)


SMALL_SHARDED = ("norm_g", "ffn_conv_w", "cm_b_in", "cm_dw_w", "cm_dw_b", "cm_ln_g", "cm_ln_b", "cm_b_out", "gm_b_in", "gm_ln_g",
                 "gm_ln_b")
SMALL_REPLICATED = ("c_ctx", "ada_b", "ffn_conv_b", "attn_sink", "gm_w_s", "gm_b_s")
BIG = ("ffn_w_up", "ffn_w_down", "cm_w_in", "cm_w_out", "attn_w_qkv", "attn_w_o", "gm_w_in", "gm_w_out")
BIG_AXIS = {"ffn_w_up": 2, "ffn_w_down": 1, "cm_w_in": 2, "cm_w_out": 1, "attn_w_qkv": 2, "attn_w_o": 1, "gm_w_in": 2, "gm_w_out": 1}
WEIGHTS = ("c_ctx", "ada_w", "ada_b", "norm_g", "ffn_w_up", "ffn_conv_w", "ffn_conv_b", "ffn_w_down", "cm_w_in", "cm_b_in",
           "cm_dw_w", "cm_dw_b", "cm_ln_g", "cm_ln_b", "cm_w_out", "cm_b_out", "attn_w_qkv", "attn_sink", "attn_w_o", "gm_w_in",
           "gm_b_in", "gm_ln_g", "gm_ln_b", "gm_w_s", "gm_b_s", "gm_w_out")


def _step(x, c, ctx, target, W, M, V):
    L, D = x.shape[1], x.shape[2]
    C = ctx.shape[1]
    T = L + C
    NL = W["ada_w"].shape[0]
    tm = 256 if C % 256 == 0 else 128
    nl = L // tm
    xi, yi, ci = _my_place()
    chip = 2 * xi + yi
    dev = 4 * xi + 2 * yi + ci
    segs2, segs1 = [(0, L), (L, C)], [(0, L)]
    vec = lambda a: a.reshape(1, -1)

    layer_sets = [[("cm_w_in", 0), ("cm_w_out", 0), ("ffn_w_up", 0), ("ffn_w_down", 0)],
                  [("attn_w_qkv", 0), ("attn_w_o", 0), ("ffn_w_up", 1), ("ffn_w_down", 1)],
                  [("gm_w_in", 0), ("gm_w_out", 0), ("ffn_w_up", 2), ("ffn_w_down", 2)],
                  [("cm_w_in", 1), ("cm_w_out", 1), ("ffn_w_up", 3), ("ffn_w_down", 3)]]
    arrived = {}

    def fetch(keys, zero, sequencer_id=None):
        exchanges = []
        for n, i in keys:
            shard = (W[n][i] + zero).astype(MMT)
            whole = list(shard.shape)
            whole[BIG_AXIS[n] - 1] *= N_CHIPS
            exchanges.append(([shard], [SDS(tuple(whole), MMT)], _gather_plan(BIG_AXIS[n] - 1), N_CHIPS - 1))
        if sequencer_id is None:
            lands = _tensorcore_exchange("fetch_weights_first", exchanges)
        else:
            lands = _sequencer_exchange(f"fetch_weights_{keys[0][0]}_{keys[0][1]}", sequencer_id, exchanges, _same_core_peers)
        for key, land in zip(keys, lands):
            arrived[key] = land[0]

    def big(n, i, after=None):
        return arrived[(n, i)]

    small_shapes = [W[n].shape for n in SMALL_SHARDED]
    ag1 = _all_gather(_pack([c.reshape(-1)] + [W[n] for n in SMALL_SHARDED]), "gather_small")
    parts = [_unpack(ag1[2 * s], [(D,)] + small_shapes) for s in range(N_CHIPS)]
    c_rows = jnp.stack([_unpack(ag1[d], [(D,)])[0] for d in range(N_DEV)])
    P = {n: _unshard_last(jnp.stack([parts[s][1 + i] for s in range(N_CHIPS)])) for i, n in enumerate(SMALL_SHARDED)}
    for n in SMALL_REPLICATED:
        P[n] = W[n]

    cond = jnp.concatenate([c_rows, W["c_ctx"][None, :], jnp.zeros((2 * SUBLANES - N_DEV - 1, D), F32)], axis=0)
    ncol = W["ada_w"].shape[2]
    ada_b_mine = lax.dynamic_slice_in_dim(W["ada_b"], chip * ncol, ncol, axis=1)[:, None, :]
    mods_mine = _ada_fwd(cond, W["ada_w"], ada_b_mine, "ada_fwd")
    ag2 = _all_gather(mods_mine.reshape(NL * 2 * SUBLANES, ncol), "gather_mods").reshape(N_DEV, NL, 2 * SUBLANES, ncol)
    mods_all = _unshard_last(jnp.stack([ag2[2 * s] for s in range(N_CHIPS)]))
    mod_lat = lax.dynamic_index_in_dim(mods_all, dev, axis=1, keepdims=False).reshape(NL, 6, D)
    mod_ctx = mods_all[:, N_DEV].reshape(NL, 6, D)
    mod2 = jnp.stack([mod_lat, mod_ctx], axis=1)
    mod1 = mod_lat[:, None]

    corner = mod2[0, 0, 0, 0]
    behind_small = jnp.where(corner != corner, corner, 0.0)
    for k in range(4):
        fetch(layer_sets[k], behind_small, FETCH_IDS[k])
    zero_d = jnp.zeros((1, D), F32)
    cos, sin = _rope_tables(L)
    nkv = D // HEAD_DIM // Q_PER_KV
    qdim, kvdim = D, nkv * HEAD_DIM

    def ffn_fwd(i, h, mod, rows, segs, tag):
        a2 = _prenorm(h, mod, vec(P["norm_g"][i, 2]), 1, rows, nl, tm, f"pre_ffn_{tag}")
        z0 = _mm(a2, big("ffn_w_up", i, a2), "nn", F32, f"ffn_up_{tag}")
        u = _ffn_gate(z0, P["ffn_conv_w"][i], vec(P["ffn_conv_b"][i]), segs, f"ffn_gate_{tag}")
        f = _mm(u, big("ffn_w_down", i, u), "nn", F32, f"ffn_down_{tag}")
        h_out = _postnorm(h, f, zero_d, mod, vec(P["norm_g"][i, 3]), 5, rows, nl, tm, f"post_ffn_{tag}")
        return h_out, dict(h=h, a2=a2, z0=z0, f=f)

    def ffn_bwd(i, dh, sv, mod, rows, segs, tag, G):
        df, dg2, dgn3, _ = _postnorm_bwd(dh, sv["f"], zero_d, mod, vec(P["norm_g"][i, 3]), 5, rows, nl, tm, f"post_ffn_bwd_{tag}")
        du = _mm(df, big("ffn_w_down", i), "nt", F32, f"ffn_down_dx_{tag}")
        u, dz0, dcw, dcb = _ffn_gate_bwd(sv["z0"], du, P["ffn_conv_w"][i], vec(P["ffn_conv_b"][i]), segs, f"ffn_gate_bwd_{tag}")
        G["ffn_w_down"][i] = _mm(u, df, "tn", MMT, f"ffn_down_dw_{tag}")
        G["ffn_w_up"][i] = _mm(sv["a2"], dz0, "tn", MMT, f"ffn_up_dw_{tag}")
        da2 = _mm(dz0, big("ffn_w_up", i), "nt", F32, f"ffn_up_dx_{tag}")
        dh, dsh2, dsc2, dgn2 = _prenorm_bwd(sv["h"], da2, dh, mod, vec(P["norm_g"][i, 2]), 1, rows, nl, tm, f"pre_ffn_bwd_{tag}")
        G["ffn_conv_w"][i], G["ffn_conv_b"][i] = dcw, dcb[0]
        return dh, (dsh2, dsc2, dg2), (dgn2, dgn3)

    def conformer_fwd(i, j, h, mod, rows, segs, tag):
        a = _prenorm(h, mod, vec(P["norm_g"][i, 0]), 0, rows, nl, tm, f"pre_mix_{tag}")
        p0 = _mm(a, big("cm_w_in", j, a), "nn", F32, f"cm_in_{tag}")
        z2 = _glu_conv(p0, vec(P["cm_b_in"][j]), P["cm_dw_w"][j], vec(P["cm_dw_b"][j]), segs, f"cm_conv_{tag}")
        z4 = _ln_silu(z2, vec(P["cm_ln_g"][j]), vec(P["cm_ln_b"][j]), rows, tm, f"cm_ln_{tag}")
        y = _mm(z4, big("cm_w_out", j, z4), "nn", F32, f"cm_out_{tag}")
        h_out = _postnorm(h, y, vec(P["cm_b_out"][j]), mod, vec(P["norm_g"][i, 1]), 2, rows, nl, tm, f"post_mix_{tag}")
        return h_out, dict(h=h, a=a, p0=p0, z2=z2, z4=z4, y=y)

    def conformer_bwd(i, j, dh, sv, mod, rows, segs, tag, G):
        dy, dg1, dgn1, dbo = _postnorm_bwd(dh, sv["y"], vec(P["cm_b_out"][j]) + zero_d, mod, vec(P["norm_g"][i, 1]), 2, rows, nl,
                                           tm, f"post_mix_bwd_{tag}")
        G["cm_w_out"][j] = _mm(sv["z4"], dy, "tn", MMT, f"cm_out_dw_{tag}")
        dz4 = _mm(dy, big("cm_w_out", j), "nt", F32, f"cm_out_dx_{tag}")
        dz2, dlg, dlb = _ln_silu_bwd(sv["z2"], dz4, vec(P["cm_ln_g"][j]), vec(P["cm_ln_b"][j]), rows, tm, f"cm_ln_bwd_{tag}")
        dpa, dpg, ddw, ddb, dba, dbg = _glu_conv_bwd(sv["p0"], vec(P["cm_b_in"][j]), P["cm_dw_w"][j], dz2, segs, f"cm_conv_bwd_{tag}")
        dp = jnp.concatenate([dpa, dpg], axis=1)
        G["cm_w_in"][j] = _mm(sv["a"], dp, "tn", MMT, f"cm_in_dw_{tag}")
        da = _mm(dp, big("cm_w_in", j), "nt", F32, f"cm_in_dx_{tag}")
        dh, dsh1, dsc1, dgn0 = _prenorm_bwd(sv["h"], da, dh, mod, vec(P["norm_g"][i, 0]), 0, rows, nl, tm, f"pre_mix_bwd_{tag}")
        G["cm_b_out"][j] = jnp.sum(dbo, axis=0)[0]
        G["cm_ln_g"][j], G["cm_ln_b"][j], G["cm_dw_w"][j], G["cm_dw_b"][j] = dlg[0], dlb[0], ddw, ddb[0]
        G["cm_b_in"][j] = jnp.concatenate([dba[0], dbg[0]])
        return dh, (dsh1, dsc1, dg1), (dgn0, dgn1)

    def heads(a, n):
        return a.reshape(a.shape[0], n, HEAD_DIM).transpose(1, 0, 2)

    def unheads(a):
        return a.transpose(1, 0, 2).reshape(a.shape[1], -1)

    G = {n: [None] * W[n].shape[0] for n in WEIGHTS if n not in ("c_ctx", "ada_w", "ada_b", "norm_g")}
    saved = []
    h = jnp.concatenate([x[0], ctx[0]], axis=0)
    h, s_mix = conformer_fwd(0, 0, h, mod2[0], T, segs2, "l0")
    h, s_ffn = ffn_fwd(0, h, mod2[0], T, segs2, "l0")
    saved.append((s_mix, s_ffn))
    a_all = _prenorm(h, mod2[1], vec(P["norm_g"][1, 0]), 0, T, nl, tm, "pre_mix_l1")
    qkv = _mm(a_all, big("attn_w_qkv", 0, a_all), "nn", F32, "attn_qkv")
    qk_rot, v_lat = _rope(qkv, cos, sin, L, qdim + kvdim, tm, "rope")
    q_h = heads(qk_rot[:, :qdim], nkv * Q_PER_KV).reshape(nkv, Q_PER_KV, L, HEAD_DIM)
    k_h, v_h = heads(qk_rot[:, qdim:], nkv), heads(v_lat, nkv)
    kc_h = heads(qkv[L:, qdim:qdim + kvdim].astype(MMT), nkv)
    vc_h = heads(qkv[L:, qdim + kvdim:].astype(MMT), nkv)
    sink = P["attn_sink"][0]
    o_h, lse = _attn_fwd(q_h, k_h, v_h, kc_h, vc_h, sink, "attn")
    o_nat = unheads(o_h.reshape(nkv * Q_PER_KV, L, HEAD_DIM)).astype(MMT)
    y1 = _mm(o_nat, big("attn_w_o", 0, o_nat), "nn", F32, "attn_out")
    h_in1 = h
    h = _postnorm(h, y1, zero_d, mod1[1], vec(P["norm_g"][1, 1]), 2, L, nl, tm, "post_mix_l1")
    h, s_ffn1 = ffn_fwd(1, h, mod1[1], L, segs1, "lat")
    h_in2 = h
    a_2 = _prenorm(h, mod1[2], vec(P["norm_g"][2, 0]), 0, L, nl, tm, "pre_mix_l2")
    p0_2 = _mm(a_2, big("gm_w_in", 0, a_2), "nn", F32, "gm_in")
    ws_bf = P["gm_w_s"][0].astype(MMT)
    bs_col = P["gm_b_s"][0][:, :, None]
    us = _gmlp_fwd(p0_2, vec(P["gm_b_in"][0]), vec(P["gm_ln_g"][0]), vec(P["gm_ln_b"][0]), ws_bf, bs_col, "gmlp")
    y2 = _mm(us, big("gm_w_out", 0, us), "nn", F32, "gm_out")
    h = _postnorm(h, y2, zero_d, mod1[2], vec(P["norm_g"][2, 1]), 2, L, nl, tm, "post_mix_l2")
    h, s_ffn2 = ffn_fwd(2, h, mod1[2], L, segs1, "lat")
    h, s_mix3 = conformer_fwd(3, 1, h, mod1[3], L, segs1, "l3")
    h, s_ffn3 = ffn_fwd(3, h, mod1[3], L, segs1, "lat")

    loss_mine, dh = _loss_head(h, target[0], tm, "loss_head")

    dmod = [None] * NL
    dgn = [None] * NL

    def finish(i, mix, ffn, gns_mix, gns_ffn):
        dmod[i] = jnp.concatenate(list(mix) + list(ffn), axis=1)
        dgn[i] = jnp.stack([jnp.sum(g, axis=0)[0] for g in (gns_mix[0], gns_mix[1], gns_ffn[0], gns_ffn[1])])

    sent, so_far = {}, {}

    def send(tag, collective_id, tensors):
        exchanges = []
        for n, l in tensors:
            g = G[n][l]
            shard = list(g.shape)
            shard[BIG_AXIS[n] - 1] //= N_CHIPS
            exchanges.append(([g], [SDS((N_CHIPS,) + tuple(shard), g.dtype)], _scatter_plan(BIG_AXIS[n] - 1), N_CHIPS - 1))
        sent[tag] = (tensors, _sequencer_exchange(f"send_grads_{tag}", collective_id, exchanges, _same_core_peers))
        corner = sum(G[n][l][0:1, 0:1].astype(F32) for n, l in tensors)
        return jnp.where(corner != corner, corner, 0.0)

    def land(tag, after):
        tensors, landed = sent[tag]
        mine = [_sum_slots(lands[0], f"sum_chips_{n}_{l}", after) for (n, l), lands in zip(tensors, landed)]
        theirs = _swap_with_sibling(mine, f"swap_cores_{tag}")
        for (n, l), a, b in zip(tensors, mine, theirs):
            so_far[n] = _adamw_layer(W[n], M[n], V[n], l, [a, b], so_far.get(n), f"adamw_{n}_{l}")

    dh, m_ffn, n_ffn = ffn_bwd(3, dh, s_ffn3, mod1[3], L, segs1, "lat", G)
    dh, m_mix, n_mix = conformer_bwd(3, 1, dh, s_mix3, mod1[3], L, segs1, "l3", G)
    finish(3, m_mix, m_ffn, n_mix, n_ffn)
    zero_d = zero_d + send("l3", SEND_IDS[0], [("ffn_w_up", 3), ("ffn_w_down", 3), ("cm_w_in", 1), ("cm_w_out", 1)])
    land("l3", None)

    dh, m_ffn, n_ffn = ffn_bwd(2, dh, s_ffn2, mod1[2], L, segs1, "lat", G)
    dy2, dg1, dgn1, _ = _postnorm_bwd(dh, y2, zero_d, mod1[2], vec(P["norm_g"][2, 1]), 2, L, nl, tm, "post_mix_bwd_l2")
    G["gm_w_out"][0] = _mm(us, dy2, "tn", MMT, "gm_out_dw")
    dus = _mm(dy2, big("gm_w_out", 0), "nt", F32, "gm_out_dx")
    ws_t = jnp.swapaxes(P["gm_w_s"][0], 1, 2).astype(MMT)
    dpre, dbi, dlg, dlb, dws, dbs = _gmlp_bwd(p0_2, dus, vec(P["gm_b_in"][0]), vec(P["gm_ln_g"][0]), vec(P["gm_ln_b"][0]), ws_bf,
                                              ws_t, bs_col, "gmlp_bwd")
    G["gm_w_in"][0] = _mm(a_2, dpre, "tn", MMT, "gm_in_dw")
    da = _mm(dpre, big("gm_w_in", 0), "nt", F32, "gm_in_dx")
    dh, dsh1, dsc1, dgn0 = _prenorm_bwd(h_in2, da, dh, mod1[2], vec(P["norm_g"][2, 0]), 0, L, nl, tm, "pre_mix_bwd_l2")
    G["gm_b_in"][0], G["gm_ln_g"][0], G["gm_ln_b"][0], G["gm_w_s"][0], G["gm_b_s"][0] = dbi[0], dlg[0], dlb[0], dws, dbs[:, :, 0]
    finish(2, (dsh1, dsc1, dg1), m_ffn, (dgn0, dgn1), n_ffn)
    zero_d = zero_d + send("l2", SEND_IDS[1], [("ffn_w_up", 2), ("ffn_w_down", 2), ("gm_w_in", 0), ("gm_w_out", 0)])
    land("l2", None)

    dh, m_ffn, n_ffn = ffn_bwd(1, dh, s_ffn1, mod1[1], L, segs1, "lat", G)
    dy1, dg1, dgn1, _ = _postnorm_bwd(dh, y1, zero_d, mod1[1], vec(P["norm_g"][1, 1]), 2, L, nl, tm, "post_mix_bwd_l1")
    G["attn_w_o"][0] = _mm(o_nat, dy1, "tn", MMT, "attn_out_dw")
    do_nat = _mm(dy1, big("attn_w_o", 0), "nt", MMT, "attn_out_dx")
    do_h = heads(do_nat, nkv * Q_PER_KV).reshape(nkv, Q_PER_KV, L, HEAD_DIM)
    dq_h, dkc_h, dvc_h, dsk = _attn_bwd_q(q_h, k_h, v_h, kc_h, vc_h, sink, o_h, do_h, lse, "attn_bwd_q")
    dk_h, dv_h = _attn_bwd_kv(q_h, k_h, v_h, o_h, do_h, lse, "attn_bwd_kv")
    dqk = jnp.concatenate([unheads(dq_h.reshape(nkv * Q_PER_KV, L, HEAD_DIM)), unheads(dk_h)], axis=1)
    dqkv_lat = _rope_bwd(dqk, unheads(dv_h), cos, sin, tm, "rope_bwd")
    dqkv_ctx = jnp.concatenate([jnp.zeros((C, qdim), MMT), unheads(dkc_h).astype(MMT), unheads(dvc_h).astype(MMT)], axis=1)
    dqkv = jnp.concatenate([dqkv_lat, dqkv_ctx], axis=0)
    G["attn_w_qkv"][0] = _mm(a_all, dqkv, "tn", MMT, "attn_qkv_dw")
    da_all = _mm(dqkv, big("attn_w_qkv", 0), "nt", F32, "attn_qkv_dx")
    dh_all = jnp.concatenate([dh, jnp.zeros((C, D), F32)], axis=0)
    dh, dsh1, dsc1, dgn0 = _prenorm_bwd(h_in1, da_all, dh_all, mod2[1], vec(P["norm_g"][1, 0]), 0, T, nl, tm, "pre_mix_bwd_l1")
    G["attn_sink"][0] = dsk[:, :Q_PER_KV, 0].reshape(-1)
    pad_ctx = lambda a: jnp.concatenate([a, jnp.zeros_like(a)], axis=0)
    finish(1, (dsh1, dsc1, pad_ctx(dg1)), [pad_ctx(a) for a in m_ffn], (dgn0, dgn1), n_ffn)
    zero_d = zero_d + send("l1", SEND_IDS[2], [("ffn_w_up", 1), ("ffn_w_down", 1), ("attn_w_qkv", 0), ("attn_w_o", 0)])
    land("l1", None)

    s_mix0, s_ffn0 = saved[0]
    dh, m_ffn, n_ffn = ffn_bwd(0, dh, s_ffn0, mod2[0], T, segs2, "l0", G)
    zero_d = zero_d + send("l0_ffn", SEND_IDS[3], [("ffn_w_up", 0), ("ffn_w_down", 0)])
    dh, m_mix, n_mix = conformer_bwd(0, 0, dh, s_mix0, mod2[0], T, segs2, "l0", G)
    finish(0, m_mix, m_ffn, n_mix, n_ffn)
    grad_x = dh[:L][None]
    sent_l0 = send("l0_mix", SEND_IDS[4], [("cm_w_in", 0), ("cm_w_out", 0)])

    for i in range(2, NL):
        dmod[i] = pad_ctx(dmod[i])
    dmod_all = jnp.stack(dmod).reshape(NL, 2, 6 * D) + sent_l0

    ag3 = _all_gather(dmod_all.reshape(NL * 2, 6 * D), "gather_dmods").reshape(N_DEV, NL, 2, N_CHIPS, ncol)
    dm_cols = lax.dynamic_index_in_dim(ag3, chip, axis=3, keepdims=False)
    dm_lat, dm_ctx = jnp.moveaxis(dm_cols[:, :, 0], 0, 1), jnp.moveaxis(dm_cols[:, :, 1], 0, 1)
    g_ada_w, dsilu = _ada_bwd(cond, W["ada_w"], dm_lat, dm_ctx, "ada_bwd")
    cc = W["c_ctx"]
    sg = jax.nn.sigmoid(cc)
    dcctx_part = jnp.where(ci == 0, 1.0, 0.0) * dsilu[N_DEV] * (sg * (1.0 + cc * (1.0 - sg)))

    Gs = {n: jnp.stack(G[n]) for n in G if n not in BIG}
    Gs["norm_g"] = jnp.stack(dgn)
    Gs["ada_b"] = jnp.sum(dmod_all, axis=1)
    Gs["c_ctx"] = dcctx_part
    small_names = list(SMALL_SHARDED) + list(SMALL_REPLICATED)
    small_full_shapes = [P[n].shape for n in small_names]
    small_pack = _pack([Gs[n] for n in small_names])
    ((ag4,),) = _sequencer_exchange("gather_small_grads", SMALL_GRADS_ID, [
        ([small_pack], [SDS((N_DEV,) + small_pack.shape, F32)], _all_gather_plan, N_DEV - 1)], _all_peers)

    flat2 = lambda a: a.reshape(-1, a.shape[-1])
    res = {}
    outs = _adamw(flat2(W["ada_w"]), flat2(M["ada_w"]), flat2(V["ada_w"]), [flat2(g_ada_w)], "adamw_ada_w")
    res["ada_w"] = tuple(o.reshape(W["ada_w"].shape) for o in outs)

    land("l0_ffn", outs[0])
    land("l0_mix", so_far["ffn_w_up"][0])
    for n in BIG:
        res[n] = tuple(so_far[n])

    small_sum = _unpack(_sum_slots(ag4, "sum_small_grads"), small_full_shapes)
    g_small = {}
    for n, g in zip(small_names, small_sum):
        g_small[n] = _my_block_last(g, chip) if n in SMALL_SHARDED else g
    packed = [_pack([d[n] for n in small_names]) for d in (W, M, V)]
    outs_small = _adamw(packed[0], packed[1], packed[2], [_pack([g_small[n] for n in small_names])], "adamw_small")
    shard_shapes = [W[n].shape for n in small_names]
    for k, n in enumerate(small_names):
        res[n] = tuple(_unpack(o, shard_shapes)[k] for o in outs_small)

    loss = lax.psum(loss_mine[0, 0], ("x", "y", "c"))
    return (loss, grad_x) + tuple(res[n][k] for k in range(4) for n in WEIGHTS)


def kernel(x, c, ctx, c_ctx, ada_w, ada_b, norm_g, ffn_w_up, ffn_conv_w, ffn_conv_b, ffn_w_down, cm_w_in, cm_b_in, cm_dw_w, cm_dw_b, cm_ln_g, cm_ln_b, cm_w_out, cm_b_out, attn_w_qkv, attn_sink, attn_w_o, gm_w_in, gm_b_in, gm_ln_g, gm_ln_b, gm_w_s, gm_b_s, gm_w_out, loss_target, m_c_ctx, m_ada_w, m_ada_b, m_norm_g, m_ffn_w_up, m_ffn_conv_w, m_ffn_conv_b, m_ffn_w_down, m_cm_w_in, m_cm_b_in, m_cm_dw_w, m_cm_dw_b, m_cm_ln_g, m_cm_ln_b, m_cm_w_out, m_cm_b_out, m_attn_w_qkv, m_attn_sink, m_attn_w_o, m_gm_w_in, m_gm_b_in, m_gm_ln_g, m_gm_ln_b, m_gm_w_s, m_gm_b_s, m_gm_w_out, v_c_ctx, v_ada_w, v_ada_b, v_norm_g, v_ffn_w_up, v_ffn_conv_w, v_ffn_conv_b, v_ffn_w_down, v_cm_w_in, v_cm_b_in, v_cm_dw_w, v_cm_dw_b, v_cm_ln_g, v_cm_ln_b, v_cm_w_out, v_cm_b_out, v_attn_w_qkv, v_attn_sink, v_attn_w_o, v_gm_w_in, v_gm_b_in, v_gm_ln_g, v_gm_ln_b, v_gm_w_s, v_gm_b_s, v_gm_w_out):
    args = locals()
    W = {n: args[n] for n in WEIGHTS}
    M = {n: args["m_" + n] for n in WEIGHTS}
    V = {n: args["v_" + n] for n in WEIGHTS}
    return _step(x, c, ctx, loss_target, W, M, V)
```

```python
import functools
import math

import jax
import jax.numpy as jnp
from jax import lax
from jax.experimental import pallas as pl
from jax.experimental.pallas import tpu as pltpu
from jax.experimental.pallas import tpu_sc as plsc

F32 = jnp.float32
MMT = jnp.bfloat16
SDS = jax.ShapeDtypeStruct
MESH = pl.DeviceIdType.MESH

EPS = 1e-6
HEAD_DIM = 64
Q_PER_KV = 4
ATTN_BLOCK = 128
GRID_W = 64
ROPE_BASE = 10000.0
GMLP_CHUNK = 128
GMLP_GROUP_DIM = 128
CONV_WIDTH = 31
FFN_CONV_WIDTH = 3
NEG = -1e30

ADAM_LR, ADAM_B1, ADAM_B2, ADAM_EPS, ADAM_WD, ADAM_STEP = 0.001, 0.9, 0.999, 1e-08, 0.01, 10

LANES = 128
SUBLANES = 8
VMEM_LIMIT = 52 * 1024 * 1024
CONV_ROWS = 128
N_CHIPS = 4
N_DEV = 8
N_SEQUENCERS = 2
FETCH_IDS = (1, 2, 3, 4)
SEND_IDS = (5, 6, 7, 8, 9)
SMALL_GRADS_ID = 10


def _cparams(*sem):
    return pltpu.CompilerParams(dimension_semantics=sem if sem else None, vmem_limit_bytes=VMEM_LIMIT)


def _tile(n, cap, mult=LANES):
    best = None
    for d in range(mult, min(n, cap) + 1, mult):
        if n % d == 0:
            best = d
    return best if best is not None else n


def _sum0(v):
    return jnp.sum(v, axis=0, keepdims=True)


def _rms(v):
    r = lax.rsqrt(jnp.mean(v * v, axis=-1, keepdims=True) + EPS)
    return v * r, r


def _sig(v):
    return jax.nn.sigmoid(v)


def _dot(a, b, ca, cb):
    return lax.dot_general(a.astype(MMT), b.astype(MMT), (((ca,), (cb,)), ((), ())), preferred_element_type=F32)


def _mm(a, b, mode, out_dtype, name):
    if mode == "nn":
        (M, K), N = a.shape, b.shape[1]
    elif mode == "nt":
        (M, K), N = a.shape, b.shape[0]
    else:
        (K, M), N = a.shape, b.shape[1]
    tm, tn, tk = _tile(M, 512), _tile(N, 1408), _tile(K, 1536)
    nk = K // tk
    ca, cb = {"nn": (1, 0), "nt": (1, 1), "tn": (0, 0)}[mode]

    def body(a_ref, b_ref, o_ref, acc):
        k = pl.program_id(2)

        @pl.when(k == 0)
        def _():
            acc[...] = jnp.zeros_like(acc)

        acc[...] += _dot(a_ref[...], b_ref[...], ca, cb)

        @pl.when(k == nk - 1)
        def _():
            o_ref[...] = acc[...].astype(o_ref.dtype)

    a_spec = pl.BlockSpec((tk, tm), lambda i, j, k: (k, i)) if mode == "tn" else pl.BlockSpec((tm, tk), lambda i, j, k: (i, k))
    b_spec = pl.BlockSpec((tn, tk), lambda i, j, k: (j, k)) if mode == "nt" else pl.BlockSpec((tk, tn), lambda i, j, k: (k, j))
    return pl.pallas_call(
        body, name=name, grid=(M // tm, N // tn, nk), in_specs=[a_spec, b_spec],
        out_specs=pl.BlockSpec((tm, tn), lambda i, j, k: (i, j)), out_shape=SDS((M, N), out_dtype),
        scratch_shapes=[pltpu.VMEM((tm, tn), F32)], compiler_params=_cparams("parallel", "parallel", "arbitrary"))(a, b)


def _seg_of(nl, nseg):
    return (lambda i: jnp.where(i >= nl, 1, 0)) if nseg == 2 else (lambda i: 0)


def _prenorm(h, mod, gn, which, rows, nl, tm, name):
    D = h.shape[1]
    nseg = mod.shape[0]
    seg = _seg_of(nl, nseg)
    sh_i, sc_i = (0, 1) if which == 0 else (3, 4)

    def body(h_ref, mod_ref, gn_ref, a_ref):
        n, _ = _rms(h_ref[...])
        a_ref[...] = (n * gn_ref[...] * (1.0 + mod_ref[pl.ds(sc_i, 1), :]) + mod_ref[pl.ds(sh_i, 1), :]).astype(a_ref.dtype)

    return pl.pallas_call(
        body, name=name, grid=(rows // tm,),
        in_specs=[pl.BlockSpec((tm, D), lambda i: (i, 0)), pl.BlockSpec((None, 6, D), lambda i: (seg(i), 0, 0)),
                  pl.BlockSpec((1, D), lambda i: (0, 0))],
        out_specs=pl.BlockSpec((tm, D), lambda i: (i, 0)), out_shape=SDS((rows, D), MMT),
        compiler_params=_cparams("parallel"))(h, mod, gn)


def _acc_spec(D, seg):
    return pl.BlockSpec((None, 1, D), lambda i: (seg(i), 0, 0))


def _prenorm_bwd(h, da, dh_in, mod, gn, which, rows, nl, tm, name):
    D = h.shape[1]
    nseg = mod.shape[0]
    seg = _seg_of(nl, nseg)
    sc_i = 1 if which == 0 else 4

    def body(h_ref, da_ref, dhin_ref, mod_ref, gn_ref, dh_ref, dsh_ref, dsc_ref, dgn_ref):
        i = pl.program_id(0)
        first = (i == 0) | (i == nl) if nseg == 2 else (i == 0)

        @pl.when(first)
        def _():
            dsh_ref[...] = jnp.zeros_like(dsh_ref)
            dsc_ref[...] = jnp.zeros_like(dsc_ref)
            dgn_ref[...] = jnp.zeros_like(dgn_ref)

        n, r = _rms(h_ref[...])
        da_v = da_ref[...].astype(F32)
        gn_v = gn_ref[...]
        sc1 = 1.0 + mod_ref[pl.ds(sc_i, 1), :]
        dsh_ref[...] += _sum0(da_v)
        dsc_ref[...] += _sum0(da_v * (n * gn_v))
        dgn_ref[...] += _sum0(da_v * n * sc1)
        dn = da_v * (gn_v * sc1)
        dh_ref[...] = dhin_ref[...] + r * (dn - n * jnp.mean(dn * n, axis=-1, keepdims=True))

    row = pl.BlockSpec((tm, D), lambda i: (i, 0))
    acc = SDS((nseg, 1, D), F32)
    return pl.pallas_call(
        body, name=name, grid=(rows // tm,),
        in_specs=[row, row, row, pl.BlockSpec((None, 6, D), lambda i: (seg(i), 0, 0)), pl.BlockSpec((1, D), lambda i: (0, 0))],
        out_specs=[row, _acc_spec(D, seg), _acc_spec(D, seg), _acc_spec(D, seg)],
        out_shape=[SDS((rows, D), F32), acc, acc, acc], compiler_params=_cparams("arbitrary"))(h, da, dh_in, mod, gn)


def _postnorm(h, y, bias, mod, gn, gate_i, rows, nl, tm, name):
    D = h.shape[1]
    nseg = mod.shape[0]
    seg = _seg_of(nl, nseg)

    def body(h_ref, y_ref, b_ref, mod_ref, gn_ref, o_ref):
        ny, _ = _rms(y_ref[...] + b_ref[...])
        o_ref[...] = h_ref[...] + mod_ref[pl.ds(gate_i, 1), :] * (ny * gn_ref[...])

    row = pl.BlockSpec((tm, D), lambda i: (i, 0))
    vec = pl.BlockSpec((1, D), lambda i: (0, 0))
    return pl.pallas_call(
        body, name=name, grid=(rows // tm,),
        in_specs=[row, row, vec, pl.BlockSpec((None, 6, D), lambda i: (seg(i), 0, 0)), vec],
        out_specs=row, out_shape=SDS((rows, D), F32), compiler_params=_cparams("parallel"))(h, y, bias, mod, gn)


def _postnorm_bwd(dh, y, bias, mod, gn, gate_i, rows, nl, tm, name):
    D = y.shape[1]
    nseg = mod.shape[0]
    seg = _seg_of(nl, nseg)

    def body(dh_ref, y_ref, b_ref, mod_ref, gn_ref, dy_ref, dg_ref, dgn_ref, db_ref):
        i = pl.program_id(0)
        first = (i == 0) | (i == nl) if nseg == 2 else (i == 0)

        @pl.when(first)
        def _():
            dg_ref[...] = jnp.zeros_like(dg_ref)
            dgn_ref[...] = jnp.zeros_like(dgn_ref)
            db_ref[...] = jnp.zeros_like(db_ref)

        ny, ry = _rms(y_ref[...] + b_ref[...])
        g = mod_ref[pl.ds(gate_i, 1), :]
        gn_v = gn_ref[...]
        dh_v = dh_ref[...]
        dg_ref[...] += _sum0(dh_v * (ny * gn_v))
        dgn_ref[...] += _sum0(dh_v * ny * g)
        dny = dh_v * (g * gn_v)
        dy = ry * (dny - ny * jnp.mean(dny * ny, axis=-1, keepdims=True))
        db_ref[...] += _sum0(dy)
        dy_ref[...] = dy.astype(dy_ref.dtype)

    row = pl.BlockSpec((tm, D), lambda i: (i, 0))
    vec = pl.BlockSpec((1, D), lambda i: (0, 0))
    acc = SDS((nseg, 1, D), F32)
    return pl.pallas_call(
        body, name=name, grid=(rows // tm,),
        in_specs=[row, row, vec, pl.BlockSpec((None, 6, D), lambda i: (seg(i), 0, 0)), vec],
        out_specs=[row, _acc_spec(D, seg), _acc_spec(D, seg), _acc_spec(D, seg)],
        out_shape=[SDS((rows, D), MMT), acc, acc, acc], compiler_params=_cparams("arbitrary"))(dh, y, bias, mod, gn)


def _seg_layout(segs, H):
    out, base = [], H
    for s0, n in segs:
        out.append((s0, n, base))
        base += n + H
    return out, base


def _zero_pads(ref, lay, H):
    width = ref.shape[1]
    ref[pl.ds(0, H), :] = jnp.zeros((H, width), ref.dtype)
    for _, n, base in lay:
        ref[pl.ds(base + n, H), :] = jnp.zeros((H, width), ref.dtype)


def _window(ref, base, off, H):
    return ref[pl.ds(base - H + off, CONV_ROWS + 2 * H), :]


def _taps(win, H, offs):
    W = CONV_ROWS + 2 * H
    rolled, out = {}, {}
    for o in offs:
        s = H + o
        b = s % SUBLANES
        if b not in rolled:
            rolled[b] = win if b == 0 else pltpu.roll(win, shift=W - b, axis=0)
        out[o] = rolled[b][s - b:s - b + CONV_ROWS, :]
    return out


def _chunks(lay, fn):
    for s0, n, base in lay:
        def step(r, carry, s0=s0, base=base):
            fn(s0, base, pl.multiple_of(r * CONV_ROWS, CONV_ROWS))
            return carry
        lax.fori_loop(0, n // CONV_ROWS, step, 0)


def _ffn_gate(z0, conv_w, conv_b, segs, name):
    T, F2 = z0.shape
    F = F2 // 2
    tc = _tile(F, 256)
    nF = F // tc
    H = SUBLANES
    lay, srows = _seg_layout(segs, H)
    offs = [-1, 0, 1]

    def body(zg_ref, zv_ref, wg_ref, wv_ref, bg_ref, bv_ref, u_ref, xg, xv):
        _zero_pads(xg, lay, H)
        _zero_pads(xv, lay, H)
        for s0, n, base in lay:
            xg[pl.ds(base, n), :] = zg_ref[pl.ds(s0, n), :]
            xv[pl.ds(base, n), :] = zv_ref[pl.ds(s0, n), :]

        def chunk(s0, base, off):
            tg = _taps(_window(xg, base, off, H), H, offs)
            tv = _taps(_window(xv, base, off, H), H, offs)
            zg = bg_ref[...] + sum(tg[k - 1] * wg_ref[pl.ds(k, 1), :] for k in range(3))
            zv = bv_ref[...] + sum(tv[k - 1] * wv_ref[pl.ds(k, 1), :] for k in range(3))
            u_ref[pl.ds(s0 + off, CONV_ROWS), :] = (zg * _sig(zg) * zv).astype(u_ref.dtype)

        _chunks(lay, chunk)

    colg = lambda r: pl.BlockSpec((r, tc), lambda j: (0, j))
    colv = lambda r: pl.BlockSpec((r, tc), lambda j: (0, j + nF))
    return pl.pallas_call(
        body, name=name, grid=(nF,),
        in_specs=[colg(T), colv(T), colg(3), colv(3), colg(1), colv(1)],
        out_specs=colg(T), out_shape=SDS((T, F), MMT),
        scratch_shapes=[pltpu.VMEM((srows, tc), F32), pltpu.VMEM((srows, tc), F32)],
        compiler_params=_cparams("parallel"))(z0, z0, conv_w, conv_w, conv_b, conv_b)


def _ffn_gate_bwd(z0, du, conv_w, conv_b, segs, name):
    T, F2 = z0.shape
    F = F2 // 2
    tc = _tile(F, 256)
    nF = F // tc
    H = SUBLANES
    lay, srows = _seg_layout(segs, H)
    offs = [-1, 0, 1]

    def body(zo_ref, zt_ref, du_ref, wo_ref, wt_ref, bo_ref, bt_ref, u_ref, dz0_ref, dw_ref, db_ref, xo, xt, dzp):
        own_is_gate = pl.program_id(1) == 0
        for ref in (xo, xt, dzp):
            _zero_pads(ref, lay, H)
        for s0, n, base in lay:
            xo[pl.ds(base, n), :] = zo_ref[pl.ds(s0, n), :]
            xt[pl.ds(base, n), :] = zt_ref[pl.ds(s0, n), :]

        def grads(s0, base, off):
            to = _taps(_window(xo, base, off, H), H, offs)
            tt = _taps(_window(xt, base, off, H), H, offs)
            zo = bo_ref[...] + sum(to[k - 1] * wo_ref[pl.ds(k, 1), :] for k in range(3))
            zt = bt_ref[...] + sum(tt[k - 1] * wt_ref[pl.ds(k, 1), :] for k in range(3))
            so, st = _sig(zo), _sig(zt)
            du_v = du_ref[pl.ds(s0 + off, CONV_ROWS), :]
            d_gate = du_v * zt * (so * (1.0 + zo * (1.0 - so)))
            d_val = du_v * (zt * st)
            dzp[pl.ds(base + off, CONV_ROWS), :] = jnp.where(own_is_gate, d_gate, d_val)

            @pl.when(own_is_gate)
            def _():
                u_ref[pl.ds(s0 + off, CONV_ROWS), :] = (zo * so * zt).astype(u_ref.dtype)

        _chunks(lay, grads)
        dw_ref[...] = jnp.zeros_like(dw_ref)
        db_ref[...] = jnp.zeros_like(db_ref)

        def back(s0, base, off):
            td = _taps(_window(dzp, base, off, H), H, offs)
            tx = _taps(_window(xo, base, off, H), H, offs)
            dz0 = sum(td[1 - k] * wo_ref[pl.ds(k, 1), :] for k in range(3))
            dz0_ref[pl.ds(s0 + off, CONV_ROWS), :] = dz0.astype(dz0_ref.dtype)
            db_ref[...] += _sum0(td[0])
            for k in range(3):
                dw_ref[pl.ds(k, 1), :] += _sum0(td[0] * tx[k - 1])

        _chunks(lay, back)

    own = lambda r: pl.BlockSpec((r, tc), lambda j, hf: (0, hf * nF + j))
    oth = lambda r: pl.BlockSpec((r, tc), lambda j, hf: (0, (1 - hf) * nF + j))
    ucol = pl.BlockSpec((T, tc), lambda j, hf: (0, j))
    return pl.pallas_call(
        body, name=name, grid=(nF, 2),
        in_specs=[own(T), oth(T), ucol, own(3), oth(3), own(1), oth(1)],
        out_specs=[ucol, own(T), own(3), own(1)],
        out_shape=[SDS((T, F), MMT), SDS((T, F2), MMT), SDS((3, F2), F32), SDS((1, F2), F32)],
        scratch_shapes=[pltpu.VMEM((srows, tc), F32)] * 3,
        compiler_params=_cparams("parallel", "arbitrary"))(z0, z0, du, conv_w, conv_w, conv_b, conv_b)


def _glu_conv(p0, b_in, dw_w, dw_b, segs, name):
    T, D2 = p0.shape
    D = D2 // 2
    tc = _tile(D, 256)
    nD = D // tc
    H = 2 * SUBLANES
    half = (CONV_WIDTH - 1) // 2
    lay, srows = _seg_layout(segs, H)
    offs = list(range(-half, half + 1))

    def body(pa_ref, pg_ref, ba_ref, bg_ref, w_ref, b_ref, z2_ref, z1p):
        _zero_pads(z1p, lay, H)

        def glu(s0, base, off):
            rows = pl.ds(s0 + off, CONV_ROWS)
            z1p[pl.ds(base + off, CONV_ROWS), :] = (pa_ref[rows, :] + ba_ref[...]) * _sig(pg_ref[rows, :] + bg_ref[...])

        _chunks(lay, glu)

        def conv(s0, base, off):
            t = _taps(_window(z1p, base, off, H), H, offs)
            acc = b_ref[...] + t[-half] * w_ref[pl.ds(0, 1), :]
            for k in range(1, CONV_WIDTH):
                acc = acc + t[k - half] * w_ref[pl.ds(k, 1), :]
            z2_ref[pl.ds(s0 + off, CONV_ROWS), :] = acc

        _chunks(lay, conv)

    cola = lambda r: pl.BlockSpec((r, tc), lambda j: (0, j))
    colg = lambda r: pl.BlockSpec((r, tc), lambda j: (0, j + nD))
    return pl.pallas_call(
        body, name=name, grid=(nD,),
        in_specs=[cola(T), colg(T), cola(1), colg(1), cola(CONV_WIDTH), cola(1)],
        out_specs=cola(T), out_shape=SDS((T, D), F32), scratch_shapes=[pltpu.VMEM((srows, tc), F32)],
        compiler_params=_cparams("parallel"))(p0, p0, b_in, b_in, dw_w, dw_b)


def _glu_conv_bwd(p0, b_in, dw_w, dz2, segs, name):
    T, D2 = p0.shape
    D = D2 // 2
    tc = _tile(D, 256)
    nD = D // tc
    H = 2 * SUBLANES
    half = (CONV_WIDTH - 1) // 2
    lay, srows = _seg_layout(segs, H)
    offs = list(range(-half, half + 1))

    def body(pa_ref, pg_ref, ba_ref, bg_ref, w_ref, dz2_ref, dpa_ref, dpg_ref, dw_ref, db_ref, dba_ref, dbg_ref, z1p, dzp):
        _zero_pads(z1p, lay, H)
        _zero_pads(dzp, lay, H)
        for s0, n, base in lay:
            dzp[pl.ds(base, n), :] = dz2_ref[pl.ds(s0, n), :]

        def glu(s0, base, off):
            rows = pl.ds(s0 + off, CONV_ROWS)
            z1p[pl.ds(base + off, CONV_ROWS), :] = (pa_ref[rows, :] + ba_ref[...]) * _sig(pg_ref[rows, :] + bg_ref[...])

        _chunks(lay, glu)
        for ref in (dw_ref, db_ref, dba_ref, dbg_ref):
            ref[...] = jnp.zeros_like(ref)

        def back(s0, base, off):
            td = _taps(_window(dzp, base, off, H), H, offs)
            tz = _taps(_window(z1p, base, off, H), H, offs)
            dz1 = td[half] * w_ref[pl.ds(0, 1), :]
            for k in range(1, CONV_WIDTH):
                dz1 = dz1 + td[half - k] * w_ref[pl.ds(k, 1), :]
            db_ref[...] += _sum0(td[0])
            for k in range(CONV_WIDTH):
                dw_ref[pl.ds(k, 1), :] += _sum0(td[0] * tz[k - half])
            rows = pl.ds(s0 + off, CONV_ROWS)
            pa = pa_ref[rows, :] + ba_ref[...]
            sg = _sig(pg_ref[rows, :] + bg_ref[...])
            dpa = dz1 * sg
            dpg = dz1 * pa * (sg * (1.0 - sg))
            dba_ref[...] += _sum0(dpa)
            dbg_ref[...] += _sum0(dpg)
            dpa_ref[rows, :] = dpa.astype(dpa_ref.dtype)
            dpg_ref[rows, :] = dpg.astype(dpg_ref.dtype)

        _chunks(lay, back)

    cola = lambda r: pl.BlockSpec((r, tc), lambda j: (0, j))
    colg = lambda r: pl.BlockSpec((r, tc), lambda j: (0, j + nD))
    return pl.pallas_call(
        body, name=name, grid=(nD,),
        in_specs=[cola(T), colg(T), cola(1), colg(1), cola(CONV_WIDTH), cola(T)],
        out_specs=[cola(T), cola(T), cola(CONV_WIDTH), cola(1), cola(1), cola(1)],
        out_shape=[SDS((T, D), MMT), SDS((T, D), MMT), SDS((CONV_WIDTH, D), F32), SDS((1, D), F32), SDS((1, D), F32),
                   SDS((1, D), F32)],
        scratch_shapes=[pltpu.VMEM((srows, tc), F32)] * 2, compiler_params=_cparams("parallel"))(p0, p0, b_in, b_in, dw_w, dz2)


def _layer_norm_stats(v):
    mu = jnp.mean(v, axis=-1, keepdims=True)
    var = jnp.mean(jnp.square(v - mu), axis=-1, keepdims=True)
    rstd = lax.rsqrt(var + EPS)
    return (v - mu) * rstd, rstd


def _ln_silu(z2, ln_g, ln_b, rows, tm, name):
    D = z2.shape[1]

    def body(z_ref, g_ref, b_ref, o_ref):
        xh, _ = _layer_norm_stats(z_ref[...])
        z3 = xh * g_ref[...] + b_ref[...]
        o_ref[...] = (z3 * _sig(z3)).astype(o_ref.dtype)

    row = pl.BlockSpec((tm, D), lambda i: (i, 0))
    vec = pl.BlockSpec((1, D), lambda i: (0, 0))
    return pl.pallas_call(body, name=name, grid=(rows // tm,), in_specs=[row, vec, vec], out_specs=row,
                          out_shape=SDS((rows, D), MMT), compiler_params=_cparams("parallel"))(z2, ln_g, ln_b)


def _ln_silu_bwd(z2, dz4, ln_g, ln_b, rows, tm, name):
    D = z2.shape[1]

    def body(z_ref, d_ref, g_ref, b_ref, dz_ref, dg_ref, db_ref):
        @pl.when(pl.program_id(0) == 0)
        def _():
            dg_ref[...] = jnp.zeros_like(dg_ref)
            db_ref[...] = jnp.zeros_like(db_ref)

        xh, rstd = _layer_norm_stats(z_ref[...])
        z3 = xh * g_ref[...] + b_ref[...]
        s = _sig(z3)
        dz3 = d_ref[...] * (s * (1.0 + z3 * (1.0 - s)))
        dg_ref[...] += _sum0(dz3 * xh)
        db_ref[...] += _sum0(dz3)
        dxh = dz3 * g_ref[...]
        dz_ref[...] = rstd * (dxh - jnp.mean(dxh, axis=-1, keepdims=True) - xh * jnp.mean(dxh * xh, axis=-1, keepdims=True))

    row = pl.BlockSpec((tm, D), lambda i: (i, 0))
    vec = pl.BlockSpec((1, D), lambda i: (0, 0))
    return pl.pallas_call(body, name=name, grid=(rows // tm,), in_specs=[row, row, vec, vec], out_specs=[row, vec, vec],
                          out_shape=[SDS((rows, D), F32), SDS((1, D), F32), SDS((1, D), F32)],
                          compiler_params=_cparams("arbitrary"))(z2, dz4, ln_g, ln_b)


def _rot_half_pairs(v):
    width = v.shape[1]
    lane = lax.broadcasted_iota(jnp.int32, v.shape, 1)
    return jnp.where((lane % 32) < 16, -pltpu.roll(v, shift=width - 16, axis=1), pltpu.roll(v, shift=16, axis=1))


def _rope(qkv, cos, sin, L, qk, tm, name):
    width = qkv.shape[1]
    kv = width - qk

    def body(x_ref, c_ref, s_ref, qk_ref, v_ref):
        xv = x_ref[:, pl.ds(0, qk)]
        c = jnp.tile(c_ref[...], (1, qk // LANES))
        s = jnp.tile(s_ref[...], (1, qk // LANES))
        qk_ref[...] = (xv * c + _rot_half_pairs(xv) * s).astype(qk_ref.dtype)
        v_ref[...] = x_ref[:, pl.ds(qk, kv)].astype(v_ref.dtype)

    tab = pl.BlockSpec((tm, LANES), lambda i: (i, 0))
    return pl.pallas_call(
        body, name=name, grid=(L // tm,), in_specs=[pl.BlockSpec((tm, width), lambda i: (i, 0)), tab, tab],
        out_specs=[pl.BlockSpec((tm, qk), lambda i: (i, 0)), pl.BlockSpec((tm, kv), lambda i: (i, 0))],
        out_shape=[SDS((L, qk), MMT), SDS((L, kv), MMT)], compiler_params=_cparams("parallel"))(qkv, cos, sin)


def _rope_bwd(dqk, dv, cos, sin, tm, name):
    L, qk = dqk.shape
    kv = dv.shape[1]

    def body(d_ref, dv_ref, c_ref, s_ref, o_ref):
        dv_ = d_ref[...]
        c = jnp.tile(c_ref[...], (1, qk // LANES))
        s = jnp.tile(s_ref[...], (1, qk // LANES))
        o_ref[:, pl.ds(0, qk)] = (dv_ * c - _rot_half_pairs(dv_ * s)).astype(o_ref.dtype)
        o_ref[:, pl.ds(qk, kv)] = dv_ref[...].astype(o_ref.dtype)

    tab = pl.BlockSpec((tm, LANES), lambda i: (i, 0))
    return pl.pallas_call(
        body, name=name, grid=(L // tm,),
        in_specs=[pl.BlockSpec((tm, qk), lambda i: (i, 0)), pl.BlockSpec((tm, kv), lambda i: (i, 0)), tab, tab],
        out_specs=pl.BlockSpec((tm, qk + kv), lambda i: (i, 0)), out_shape=SDS((L, qk + kv), MMT),
        compiler_params=_cparams("parallel"))(dqk, dv, cos, sin)


def _band_specs(nb, width):
    blk = lambda f: pl.BlockSpec((None, ATTN_BLOCK, width), f)
    return [blk(lambda h, n: (h, jnp.maximum(n - 1, 0), 0)), blk(lambda h, n: (h, n, 0)),
            blk(lambda h, n: (h, jnp.minimum(n + 1, nb - 1), 0))]


def _window_mask(n, L):
    qi = lax.broadcasted_iota(jnp.int32, (ATTN_BLOCK, 3 * ATTN_BLOCK), 0)
    kk = lax.broadcasted_iota(jnp.int32, (ATTN_BLOCK, 3 * ATTN_BLOCK), 1)
    key_abs = (n - 1) * ATTN_BLOCK + kk
    return (jnp.abs(qi + ATTN_BLOCK - kk) <= ATTN_BLOCK) & (key_abs >= 0) & (key_abs < L)


def _attn_fwd(q, k, v, kc, vc, sink, name):
    nkv, _, L, hd = q.shape
    C = kc.shape[1]
    nb = L // ATTN_BLOCK
    scale = HEAD_DIM ** -0.5

    def body(sink_ref, q_ref, k0, k1, k2, v0, v1, v2, kc_ref, vc_ref, o_ref, lse_ref):
        hh, n = pl.program_id(0), pl.program_id(1)
        kw = jnp.concatenate([k0[...], k1[...], k2[...]], axis=0)
        vw = jnp.concatenate([v0[...], v1[...], v2[...]], axis=0)
        mask = _window_mask(n, L)
        for g in range(Q_PER_KV):
            qg = q_ref[g]
            sw = jnp.where(mask, _dot(qg, kw, 1, 1) * scale, NEG)
            sc = _dot(qg, kc_ref[...], 1, 1) * scale
            sk = sink_ref[hh * Q_PER_KV + g]
            m = jnp.maximum(jnp.maximum(jnp.max(sw, axis=-1, keepdims=True), jnp.max(sc, axis=-1, keepdims=True)), sk)
            pw, pc = jnp.exp(sw - m), jnp.exp(sc - m)
            den = jnp.sum(pw, axis=-1, keepdims=True) + jnp.sum(pc, axis=-1, keepdims=True) + jnp.exp(sk - m)
            inv = 1.0 / den
            o_ref[g] = _dot(pw * inv, vw, 1, 0) + _dot(pc * inv, vc_ref[...], 1, 0)
            lse_ref[g] = m + jnp.log(den)

    qspec = pl.BlockSpec((None, Q_PER_KV, ATTN_BLOCK, hd), lambda h, n: (h, 0, n, 0))
    cspec = pl.BlockSpec((None, C, hd), lambda h, n: (h, 0, 0))
    return pl.pallas_call(
        body, name=name, grid=(nkv, nb),
        in_specs=[pl.BlockSpec(memory_space=pltpu.SMEM), qspec] + _band_specs(nb, hd) + _band_specs(nb, hd) + [cspec, cspec],
        out_specs=[qspec, pl.BlockSpec((None, Q_PER_KV, ATTN_BLOCK, 1), lambda h, n: (h, 0, n, 0))],
        out_shape=[SDS((nkv, Q_PER_KV, L, hd), F32), SDS((nkv, Q_PER_KV, L, 1), F32)],
        compiler_params=_cparams("parallel", "parallel"))(sink, q, k, k, k, v, v, v, kc, vc)


def _attn_bwd_q(q, k, v, kc, vc, sink, o, do, lse, name):
    nkv, _, L, hd = q.shape
    C = kc.shape[1]
    nb = L // ATTN_BLOCK
    scale = HEAD_DIM ** -0.5

    def body(sink_ref, q_ref, k0, k1, k2, v0, v1, v2, kc_ref, vc_ref, o_ref, do_ref, lse_ref, dq_ref, dkc_ref, dvc_ref, dsk_ref):
        hh, n = pl.program_id(0), pl.program_id(1)

        @pl.when(n == 0)
        def _():
            dkc_ref[...] = jnp.zeros_like(dkc_ref)
            dvc_ref[...] = jnp.zeros_like(dvc_ref)
            dsk_ref[...] = jnp.zeros_like(dsk_ref)

        kw = jnp.concatenate([k0[...], k1[...], k2[...]], axis=0)
        vw = jnp.concatenate([v0[...], v1[...], v2[...]], axis=0)
        mask = _window_mask(n, L)
        for g in range(Q_PER_KV):
            qg, dog, lse_g = q_ref[g], do_ref[g], lse_ref[g]
            delta = jnp.sum(dog.astype(F32) * o_ref[g], axis=-1, keepdims=True)
            pw = jnp.exp(jnp.where(mask, _dot(qg, kw, 1, 1) * scale, NEG) - lse_g)
            pc = jnp.exp(_dot(qg, kc_ref[...], 1, 1) * scale - lse_g)
            dsw = pw * (_dot(dog, vw, 1, 1) - delta)
            dsc = pc * (_dot(dog, vc_ref[...], 1, 1) - delta)
            dq_ref[g] = (_dot(dsw, kw, 1, 0) + _dot(dsc, kc_ref[...], 1, 0)) * scale
            dkc_ref[...] += _dot(dsc, qg, 0, 0) * scale
            dvc_ref[...] += _dot(pc, dog, 0, 0)
            psk = jnp.exp(sink_ref[hh * Q_PER_KV + g] - lse_g)
            dsk_ref[pl.ds(g, 1), :] += jnp.broadcast_to(jnp.sum(-psk * delta, axis=0, keepdims=True), (1, LANES))

    qspec = pl.BlockSpec((None, Q_PER_KV, ATTN_BLOCK, hd), lambda h, n: (h, 0, n, 0))
    lspec = pl.BlockSpec((None, Q_PER_KV, ATTN_BLOCK, 1), lambda h, n: (h, 0, n, 0))
    cspec = pl.BlockSpec((None, C, hd), lambda h, n: (h, 0, 0))
    return pl.pallas_call(
        body, name=name, grid=(nkv, nb),
        in_specs=[pl.BlockSpec(memory_space=pltpu.SMEM), qspec] + _band_specs(nb, hd) + _band_specs(nb, hd)
        + [cspec, cspec, qspec, qspec, lspec],
        out_specs=[qspec, cspec, cspec, pl.BlockSpec((None, SUBLANES, LANES), lambda h, n: (h, 0, 0))],
        out_shape=[SDS((nkv, Q_PER_KV, L, hd), F32), SDS((nkv, C, hd), F32), SDS((nkv, C, hd), F32),
                   SDS((nkv, SUBLANES, LANES), F32)],
        compiler_params=_cparams("parallel", "arbitrary"))(sink, q, k, k, k, v, v, v, kc, vc, o, do, lse)


def _attn_bwd_kv(q, k, v, o, do, lse, name):
    nkv, _, L, hd = q.shape
    nb = L // ATTN_BLOCK
    scale = HEAD_DIM ** -0.5

    def body(q0, q1, q2, do0, do1, do2, o0, o1, o2, l0, l1, l2, k_ref, v_ref, dk_ref, dv_ref):
        j = pl.program_id(1)
        qi = lax.broadcasted_iota(jnp.int32, (ATTN_BLOCK, ATTN_BLOCK), 0)
        kk = lax.broadcasted_iota(jnp.int32, (ATTN_BLOCK, ATTN_BLOCK), 1)
        kj, vj = k_ref[...], v_ref[...]
        dk = jnp.zeros((ATTN_BLOCK, hd), F32)
        dv = jnp.zeros((ATTN_BLOCK, hd), F32)
        for slot, (q_r, do_r, o_r, l_r) in enumerate(((q0, do0, o0, l0), (q1, do1, o1, l1), (q2, do2, o2, l2))):
            n = j - 1 + slot
            ok = (n >= 0) & (n < nb) & (jnp.abs(qi + ATTN_BLOCK - ((2 - slot) * ATTN_BLOCK + kk)) <= ATTN_BLOCK)
            for g in range(Q_PER_KV):
                qg, dog = q_r[g], do_r[g]
                delta = jnp.sum(dog.astype(F32) * o_r[g], axis=-1, keepdims=True)
                p = jnp.exp(jnp.where(ok, _dot(qg, kj, 1, 1) * scale - l_r[g], NEG))
                ds = p * (_dot(dog, vj, 1, 1) - delta)
                dk = dk + _dot(ds, qg, 0, 0) * scale
                dv = dv + _dot(p, dog, 0, 0)
        dk_ref[...] = dk
        dv_ref[...] = dv

    def band(width):
        blk = lambda f: pl.BlockSpec((None, Q_PER_KV, ATTN_BLOCK, width), f)
        return [blk(lambda h, j: (h, 0, jnp.maximum(j - 1, 0), 0)), blk(lambda h, j: (h, 0, j, 0)),
                blk(lambda h, j: (h, 0, jnp.minimum(j + 1, nb - 1), 0))]

    kspec = pl.BlockSpec((None, ATTN_BLOCK, hd), lambda h, j: (h, j, 0))
    return pl.pallas_call(
        body, name=name, grid=(nkv, nb), in_specs=band(hd) + band(hd) + band(hd) + band(1) + [kspec, kspec],
        out_specs=[kspec, kspec], out_shape=[SDS((nkv, L, hd), F32), SDS((nkv, L, hd), F32)],
        compiler_params=_cparams("parallel", "parallel"))(q, q, q, do, do, do, o, o, o, lse, lse, lse, k, v)


_GELU_K = math.sqrt(2.0 / math.pi)


def _gelu(v):
    return 0.5 * v * (1.0 + jnp.tanh(_GELU_K * (v + 0.044715 * (v * v * v))))


def _gelu_grad(v):
    t = jnp.tanh(_GELU_K * (v + 0.044715 * (v * v * v)))
    return 0.5 * (1.0 + t) + 0.5 * v * (1.0 - t * t) * (_GELU_K * (1.0 + 3.0 * 0.044715 * (v * v)))


def _gmlp_fwd(p0, b_in, ln_g, ln_b, w_s, b_s, name):
    L, W2 = p0.shape
    W = W2 // 2
    G = W // GMLP_GROUP_DIM

    def body(p_ref, bi_ref, g_ref, b_ref, ws_ref, bs_ref, o_ref):
        ge = _gelu(p_ref[...] + bi_ref[...])
        xh, _ = _layer_norm_stats(ge[:, W:])
        vln = xh * g_ref[...] + b_ref[...]
        for gi in range(G):
            cols = slice(gi * GMLP_GROUP_DIM, (gi + 1) * GMLP_GROUP_DIM)
            s = _dot(ws_ref[gi], vln[:, cols], 1, 0) + bs_ref[gi]
            o_ref[:, cols] = (ge[:, cols] * s).astype(o_ref.dtype)

    full = lambda shape: pl.BlockSpec(shape, lambda i: (0,) * len(shape))
    return pl.pallas_call(
        body, name=name, grid=(L // GMLP_CHUNK,),
        in_specs=[pl.BlockSpec((GMLP_CHUNK, W2), lambda i: (i, 0)), full((1, W2)), full((1, W)), full((1, W)),
                  full((G, GMLP_CHUNK, GMLP_CHUNK)), full((G, GMLP_CHUNK, 1))],
        out_specs=pl.BlockSpec((GMLP_CHUNK, W), lambda i: (i, 0)), out_shape=SDS((L, W), MMT),
        compiler_params=_cparams("parallel"))(p0, b_in, ln_g, ln_b, w_s, b_s)


def _gmlp_bwd(p0, dus, b_in, ln_g, ln_b, w_s, w_st, b_s, name):
    L, W2 = p0.shape
    W = W2 // 2
    G = W // GMLP_GROUP_DIM

    def body(p_ref, d_ref, bi_ref, g_ref, b_ref, ws_ref, wst_ref, bs_ref, dpre_ref, dbi_ref, dg_ref, db_ref, dws_ref, dbs_ref, dvln):
        @pl.when(pl.program_id(0) == 0)
        def _():
            for ref in (dbi_ref, dg_ref, db_ref, dws_ref, dbs_ref):
                ref[...] = jnp.zeros_like(ref)

        pre = p_ref[...] + bi_ref[...]
        ge = _gelu(pre)
        xh, rstd = _layer_norm_stats(ge[:, W:])
        vln = xh * g_ref[...] + b_ref[...]
        dge_u = []
        for gi in range(G):
            cols = slice(gi * GMLP_GROUP_DIM, (gi + 1) * GMLP_GROUP_DIM)
            vg = vln[:, cols]
            s = _dot(ws_ref[gi], vg, 1, 0) + bs_ref[gi]
            dus_g = d_ref[:, cols]
            dge_u.append(dus_g * s)
            ds = dus_g * ge[:, cols]
            dbs_ref[gi] += jnp.sum(ds, axis=1, keepdims=True)
            dws_ref[gi] += _dot(ds, vg, 1, 1)
            dvln[:, cols] = _dot(wst_ref[gi], ds, 1, 0)
        dv = dvln[...]
        dg_ref[...] += _sum0(dv * xh)
        db_ref[...] += _sum0(dv)
        dxh = dv * g_ref[...]
        dv0 = rstd * (dxh - jnp.mean(dxh, axis=-1, keepdims=True) - xh * jnp.mean(dxh * xh, axis=-1, keepdims=True))
        dpre = jnp.concatenate(dge_u + [dv0], axis=1) * _gelu_grad(pre)
        dbi_ref[...] += _sum0(dpre)
        dpre_ref[...] = dpre.astype(dpre_ref.dtype)

    full = lambda shape: pl.BlockSpec(shape, lambda i: (0,) * len(shape))
    mats = (G, GMLP_CHUNK, GMLP_CHUNK)
    return pl.pallas_call(
        body, name=name, grid=(L // GMLP_CHUNK,),
        in_specs=[pl.BlockSpec((GMLP_CHUNK, W2), lambda i: (i, 0)), pl.BlockSpec((GMLP_CHUNK, W), lambda i: (i, 0)),
                  full((1, W2)), full((1, W)), full((1, W)), full(mats), full(mats), full((G, GMLP_CHUNK, 1))],
        out_specs=[pl.BlockSpec((GMLP_CHUNK, W2), lambda i: (i, 0)), full((1, W2)), full((1, W)), full((1, W)), full(mats),
                   full((G, GMLP_CHUNK, 1))],
        out_shape=[SDS((L, W2), MMT), SDS((1, W2), F32), SDS((1, W), F32), SDS((1, W), F32), SDS(mats, F32),
                   SDS((G, GMLP_CHUNK, 1), F32)],
        scratch_shapes=[pltpu.VMEM((GMLP_CHUNK, W), F32)], compiler_params=_cparams("arbitrary"))(
            p0, dus, b_in, ln_g, ln_b, w_s, w_st, b_s)


def _loss_head(h, target, tm, name):
    L, D = h.shape

    def body(h_ref, t_ref, l_ref, d_ref):
        @pl.when(pl.program_id(0) == 0)
        def _():
            l_ref[...] = jnp.zeros_like(l_ref)

        e = h_ref[...] - t_ref[...]
        l_ref[...] += 0.5 * jnp.sum(jnp.mean(e * e, axis=-1, keepdims=True), axis=0, keepdims=True)
        d_ref[...] = e * (1.0 / D)

    row = pl.BlockSpec((tm, D), lambda i: (i, 0))
    return pl.pallas_call(body, name=name, grid=(L // tm,), in_specs=[row, row],
                          out_specs=[pl.BlockSpec((1, 1), lambda i: (0, 0)), row],
                          out_shape=[SDS((1, 1), F32), SDS((L, D), F32)], compiler_params=_cparams("arbitrary"))(h, target)


def _ada_fwd(cond, ada_w, ada_b, name):
    NL, D, n = ada_w.shape
    tn = _tile(n, 768)

    def body(c_ref, w_ref, b_ref, o_ref):
        cv = c_ref[...]
        o_ref[...] = _dot(cv * _sig(cv), w_ref[...], 1, 0) + b_ref[...]

    return pl.pallas_call(
        body, name=name, grid=(NL, n // tn),
        in_specs=[pl.BlockSpec((2 * SUBLANES, D), lambda i, j: (0, 0)), pl.BlockSpec((None, D, tn), lambda i, j: (i, 0, j)),
                  pl.BlockSpec((None, 1, tn), lambda i, j: (i, 0, j))],
        out_specs=pl.BlockSpec((None, 2 * SUBLANES, tn), lambda i, j: (i, 0, j)), out_shape=SDS((NL, 2 * SUBLANES, n), F32),
        compiler_params=_cparams("parallel", "parallel"))(cond, ada_w, ada_b)


def _ada_bwd(cond, ada_w, dm_lat, dm_ctx, name):
    NL, D, n = ada_w.shape
    tn = _tile(n, 768)

    def body(c_ref, w_ref, dl_ref, dc_ref, dw_ref, ds_ref):
        @pl.when((pl.program_id(0) == 0) & (pl.program_id(1) == 0))
        def _():
            ds_ref[...] = jnp.zeros_like(ds_ref)

        cv = c_ref[...]
        row = lax.broadcasted_iota(jnp.int32, (SUBLANES, tn), 0)
        ctx_rows = jnp.where(row == 0, _sum0(dc_ref[...]), 0.0)
        dm = jnp.concatenate([dl_ref[...], ctx_rows], axis=0)
        dw_ref[...] = _dot(cv * _sig(cv), dm, 0, 0)
        ds_ref[...] += _dot(dm, w_ref[...], 1, 1)

    dspec = pl.BlockSpec((None, SUBLANES, tn), lambda i, j: (i, 0, j))
    return pl.pallas_call(
        body, name=name, grid=(NL, n // tn),
        in_specs=[pl.BlockSpec((2 * SUBLANES, D), lambda i, j: (0, 0)), pl.BlockSpec((None, D, tn), lambda i, j: (i, 0, j)),
                  dspec, dspec],
        out_specs=[pl.BlockSpec((None, D, tn), lambda i, j: (i, 0, j)), pl.BlockSpec((2 * SUBLANES, D), lambda i, j: (0, 0))],
        out_shape=[SDS((NL, D, n), F32), SDS((2 * SUBLANES, D), F32)],
        compiler_params=_cparams("arbitrary", "arbitrary"))(cond, ada_w, dm_lat, dm_ctx)


def _adam_math(w, g, m, v):
    m = ADAM_B1 * m + (1.0 - ADAM_B1) * g
    v = ADAM_B2 * v + (1.0 - ADAM_B2) * jnp.square(g)
    m_hat = m / (1.0 - ADAM_B1 ** ADAM_STEP)
    v_hat = v / (1.0 - ADAM_B2 ** ADAM_STEP)
    return -ADAM_LR * (m_hat / (jnp.sqrt(v_hat) + ADAM_EPS) + ADAM_WD * w), m, v


def _row_tile(rows, cols, elems):
    want = max(SUBLANES, elems // cols)
    best = SUBLANES if rows % SUBLANES == 0 else rows
    for d in range(SUBLANES, min(rows, want) + 1, SUBLANES):
        if rows % d == 0:
            best = d
    return best


def _adamw(w, m, v, parts, name):
    R, C = w.shape
    tr = _row_tile(R, C, 128 * 1024)
    npart = len(parts)

    def body(*refs):
        w_ref, m_ref, v_ref = refs[:3]
        g_ref, d_ref, nm_ref, nv_ref = refs[3 + npart:]
        g = refs[3][...]
        for p_ref in refs[4:3 + npart]:
            g = g + p_ref[...]
        d, nm, nv = _adam_math(w_ref[...], g, m_ref[...], v_ref[...])
        g_ref[...], d_ref[...], nm_ref[...], nv_ref[...] = g, d, nm, nv

    blk = pl.BlockSpec((tr, C), lambda i: (i, 0))
    return pl.pallas_call(body, name=name, grid=(R // tr,), in_specs=[blk] * (3 + npart), out_specs=[blk] * 4,
                          out_shape=[SDS((R, C), F32)] * 4, compiler_params=_cparams("parallel"))(w, m, v, *parts)


def _adamw_layer(w, m, v, layer, parts, prev, name):
    _, R, C = w.shape
    tr = _row_tile(R, C, 128 * 1024)
    npart = len(parts)
    nprev = 0 if prev is None else 4

    def body(*refs):
        w_ref, m_ref, v_ref = refs[:3]
        g_ref, d_ref, nm_ref, nv_ref = refs[3 + npart + nprev:]
        g = refs[3][...]
        for p_ref in refs[4:3 + npart]:
            g = g + p_ref[...]
        d, nm, nv = _adam_math(w_ref[...], g, m_ref[...], v_ref[...])
        g_ref[...], d_ref[...], nm_ref[...], nv_ref[...] = g, d, nm, nv

    stacked = pl.BlockSpec((None, tr, C), lambda i: (layer, i, 0))
    flat = pl.BlockSpec((tr, C), lambda i: (i, 0))
    return pl.pallas_call(
        body, name=name, grid=(R // tr,),
        in_specs=[stacked] * 3 + [flat] * npart + [pl.BlockSpec(memory_space=pl.ANY)] * nprev, out_specs=[stacked] * 4,
        out_shape=[SDS(w.shape, F32)] * 4, input_output_aliases={3 + npart + k: k for k in range(nprev)},
        compiler_params=_cparams("parallel"))(w, m, v, *parts, *(prev or ()))


def _sum_slots(x, name, after=None):
    S, R, C = x.shape
    tr = _row_tile(R, C, 128 * 1024)
    extra = [] if after is None else [after]

    def body(x_ref, *rest):
        o_ref = rest[-1]
        acc = x_ref[0].astype(F32)
        for s in range(1, S):
            acc = acc + x_ref[s].astype(F32)
        o_ref[...] = acc

    return pl.pallas_call(
        body, name=name, grid=(R // tr,),
        in_specs=[pl.BlockSpec((S, tr, C), lambda i: (0, i, 0))] + [pl.BlockSpec(memory_space=pl.ANY)] * len(extra),
        out_specs=pl.BlockSpec((tr, C), lambda i: (i, 0)), out_shape=SDS((R, C), F32),
        compiler_params=_cparams("parallel"))(x, *extra)


def _my_place():
    return lax.axis_index("x"), lax.axis_index("y"), lax.axis_index("c")


def _other_chips(x, y):
    return [(1 - x, y), (x, 1 - y), (1 - x, 1 - y)]


def _all_gather(v, name):
    R, C = v.shape

    def body(v_ref, o_ref, send_sems, recv_sems, local_sem):
        x, y, c = _my_place()
        me = 4 * x + 2 * y + c
        mine = pltpu.make_async_copy(v_ref, o_ref.at[me], local_sem)
        mine.start()
        copies = []
        for flip in range(1, N_DEV):
            fx, fy, fc = (flip >> 2) & 1, (flip >> 1) & 1, flip & 1
            peer = ((x + fx) % 2, (y + fy) % 2, (c + fc) % 2)
            cp = pltpu.make_async_remote_copy(src_ref=v_ref, dst_ref=o_ref.at[me], send_sem=send_sems.at[flip - 1],
                                              recv_sem=recv_sems.at[flip - 1], device_id=peer, device_id_type=MESH)
            cp.start()
            copies.append(cp)
        for cp in copies:
            cp.wait()
        mine.wait()

    return pl.pallas_call(
        body, name=name, in_specs=[pl.BlockSpec(memory_space=pl.ANY)], out_specs=pl.BlockSpec(memory_space=pl.ANY),
        out_shape=SDS((N_DEV, R, C), v.dtype),
        scratch_shapes=[pltpu.SemaphoreType.DMA((N_DEV - 1,)), pltpu.SemaphoreType.DMA((N_DEV - 1,)), pltpu.SemaphoreType.DMA],
        )(v)


def _shard_window(ref, axis, j, size):
    idx = [slice(None)] * len(ref.shape)
    idx[axis] = pl.ds(pl.multiple_of(j * size, SUBLANES), size)
    return ref.at[tuple(idx)]


def _gather_plan(axis):
    def plan(srcs, lands):
        x, y, c = _my_place()
        shard, whole = srcs[0], lands[0]
        half = shard.shape[0] // 2
        size = shard.shape[axis]

        def window(chip, which):
            if axis == 1:
                return whole.at[pl.ds(pl.multiple_of(which * half, SUBLANES), half), pl.ds(pl.multiple_of(chip * size, LANES), size)]
            return whole.at[pl.ds(pl.multiple_of(chip * size + which * half, SUBLANES), half), :]

        j = 2 * x + y
        local = [(shard, _shard_window(whole, axis, j, size))]
        mine = shard.at[pl.ds(pl.multiple_of(c * half, SUBLANES), half), :]
        remote = [(mine, window(j, c), (px, py, c)) for px, py in _other_chips(x, y)]
        forward = [(window(2 * px + py, c), window(2 * px + py, c), (x, y, 1 - c)) for px, py in _other_chips(x, y)]
        return local, remote, forward
    return plan


def _scatter_plan(axis):
    def plan(srcs, lands):
        x, y, c = _my_place()
        j = 2 * x + y
        size = srcs[0].shape[axis] // N_CHIPS
        local = [(_shard_window(srcs[0], axis, j, size), lands[0].at[j])]
        remote = [(_shard_window(srcs[0], axis, 2 * px + py, size), lands[0].at[j], (px, py, c)) for px, py in _other_chips(x, y)]
        return local, remote, []
    return plan


def _all_gather_plan(srcs, lands):
    x, y, c = _my_place()
    dst = lands[0].at[4 * x + 2 * y + c]
    remote = []
    for flip in range(1, N_DEV):
        fx, fy, fc = (flip >> 2) & 1, (flip >> 1) & 1, flip & 1
        remote.append((srcs[0], dst, ((x + fx) % 2, (y + fy) % 2, (c + fc) % 2)))
    return [(srcs[0], dst)], remote, []


def _same_core_peers():
    x, y, c = _my_place()
    return [(px, py, c) for px, py in _other_chips(x, y)]


def _same_core_peers_and_sibling():
    x, y, c = _my_place()
    return _same_core_peers() + [(x, y, 1 - c)]


def _all_peers():
    x, y, c = _my_place()
    return [((x + (f >> 2 & 1)) % 2, (y + (f >> 1 & 1)) % 2, (c + (f & 1)) % 2) for f in range(1, N_DEV)]


def _sequencer_exchange(name, collective_id, exchanges, peers_fn):
    hbm = pltpu.MemorySpace.HBM
    src_refs = [[jax.new_ref(s, memory_space=hbm) for s in e[0]] for e in exchanges]
    land_refs = [[jax.empty_ref(s, memory_space=hbm) for s in e[1]] for e in exchanges]
    first = [sum(e[3] for e in exchanges[:i]) for i in range(len(exchanges))]
    ncopy = sum(e[3] for e in exchanges)
    dma = pltpu.SemaphoreType.DMA

    @pl.kernel(mesh=plsc.ScalarSubcoreMesh(axis_name="sequencer", num_cores=N_SEQUENCERS), name=name,
               scratch_types=(dma((ncopy,)), dma((ncopy,)), dma((ncopy,)), dma((ncopy,)), dma),
               compiler_params=pltpu.CompilerParams(collective_id=collective_id))
    def launch(send_sems, recv_sems, onward_send_sems, onward_recv_sems, local_sem):
        me = lax.axis_index("sequencer")
        peers = peers_fn()
        barrier = pltpu.get_barrier_semaphore()
        for peer in peers:
            pl.semaphore_signal(barrier, inc=1, device_id=peer, device_id_type=MESH)
        pl.semaphore_wait(barrier, len(peers))
        plans = [e[2](src_refs[i], land_refs[i]) for i, e in enumerate(exchanges)]
        for s in range(N_SEQUENCERS):
            @pl.when(me == s)
            def _(s=s):
                q = 0
                for local, _, _ in plans:
                    for src, dst in local:
                        if q % N_SEQUENCERS == s:
                            cp = pltpu.make_async_copy(src, dst, local_sem)
                            cp.start()
                            cp.wait()
                        q += 1
                copies, onward = [], []
                for i, (_, remote, forward) in enumerate(plans):
                    assert len(remote) == exchanges[i][3] and len(forward) in (0, len(remote))
                    for k, (src, dst, peer) in enumerate(remote):
                        if (first[i] + k) % N_SEQUENCERS == s:
                            cp = pltpu.make_async_remote_copy(src_ref=src, dst_ref=dst, send_sem=send_sems.at[first[i] + k],
                                                              recv_sem=recv_sems.at[first[i] + k], device_id=peer,
                                                              device_id_type=MESH)
                            cp.start()
                            copies.append(cp)
                            if forward:
                                src, dst, peer = forward[k]
                                onward.append(pltpu.make_async_remote_copy(
                                    src_ref=src, dst_ref=dst, send_sem=onward_send_sems.at[first[i] + k],
                                    recv_sem=onward_recv_sems.at[first[i] + k], device_id=peer, device_id_type=MESH))
                            else:
                                onward.append(None)
                for cp, on in zip(copies, onward):
                    cp.wait_recv()
                    if on is not None:
                        on.start()
                for cp, on in zip(copies, onward):
                    cp.wait_send()
                    if on is not None:
                        on.wait()

    launch()
    return [[r[...] for r in refs] for refs in land_refs]


def _swap_with_sibling(parts, name):
    nt = len(parts)

    def body(*refs):
        ins, outs = refs[:nt], refs[nt:2 * nt]
        send_sems, recv_sems = refs[2 * nt:]
        x, y, c = _my_place()
        copies = []
        for t in range(nt):
            cp = pltpu.make_async_remote_copy(src_ref=ins[t], dst_ref=outs[t], send_sem=send_sems.at[t], recv_sem=recv_sems.at[t],
                                              device_id=(x, y, 1 - c), device_id_type=MESH)
            cp.start()
            copies.append(cp)
        for cp in copies:
            cp.wait()

    any_spec = pl.BlockSpec(memory_space=pl.ANY)
    return pl.pallas_call(
        body, name=name, in_specs=[any_spec] * nt, out_specs=[any_spec] * nt, out_shape=[SDS(p.shape, p.dtype) for p in parts],
        scratch_shapes=[pltpu.SemaphoreType.DMA((nt,)), pltpu.SemaphoreType.DMA((nt,))],
        )(*parts)


PACK_COLS = 1024


def _pack(arrays):
    flat = jnp.concatenate([a.reshape(-1) for a in arrays])
    pad = (-flat.shape[0]) % (SUBLANES * PACK_COLS)
    return jnp.pad(flat, (0, pad)).reshape(-1, PACK_COLS)


def _unpack(packed, shapes):
    flat, out, pos = packed.reshape(-1), [], 0
    for shape in shapes:
        n = math.prod(shape)
        out.append(flat[pos:pos + n].reshape(shape))
        pos += n
    return out


def _unshard_last(stacked):
    moved = jnp.moveaxis(stacked, 0, -2)
    return moved.reshape(moved.shape[:-2] + (moved.shape[-2] * moved.shape[-1],))


def _my_block_last(full, j):
    s = full.shape[-1] // N_CHIPS
    return lax.dynamic_index_in_dim(full.reshape(full.shape[:-1] + (N_CHIPS, s)), j, axis=full.ndim - 1, keepdims=False)


def _rope_tables(L):
    rows = L // GRID_W
    row = jnp.repeat(jnp.arange(rows), GRID_W).astype(F32)
    col = jnp.tile(jnp.arange(GRID_W), rows).astype(F32)
    axis_dim = HEAD_DIM // 2
    inv_freq = ROPE_BASE ** (-jnp.arange(0, axis_dim, 2, dtype=F32) / axis_dim)
    ang_r, ang_c = row[:, None] * inv_freq[None, :], col[:, None] * inv_freq[None, :]
    ang = jnp.concatenate([ang_r, ang_r, ang_c, ang_c] * 2, axis=-1)
    return jnp.cos(ang), jnp.sin(ang)


SMALL_SHARDED = ("norm_g", "ffn_conv_w", "cm_b_in", "cm_dw_w", "cm_dw_b", "cm_ln_g", "cm_ln_b", "cm_b_out", "gm_b_in", "gm_ln_g",
                 "gm_ln_b")
SMALL_REPLICATED = ("c_ctx", "ada_b", "ffn_conv_b", "attn_sink", "gm_w_s", "gm_b_s")
BIG = ("ffn_w_up", "ffn_w_down", "cm_w_in", "cm_w_out", "attn_w_qkv", "attn_w_o", "gm_w_in", "gm_w_out")
BIG_AXIS = {"ffn_w_up": 2, "ffn_w_down": 1, "cm_w_in": 2, "cm_w_out": 1, "attn_w_qkv": 2, "attn_w_o": 1, "gm_w_in": 2, "gm_w_out": 1}
WEIGHTS = ("c_ctx", "ada_w", "ada_b", "norm_g", "ffn_w_up", "ffn_conv_w", "ffn_conv_b", "ffn_w_down", "cm_w_in", "cm_b_in",
           "cm_dw_w", "cm_dw_b", "cm_ln_g", "cm_ln_b", "cm_w_out", "cm_b_out", "attn_w_qkv", "attn_sink", "attn_w_o", "gm_w_in",
           "gm_b_in", "gm_ln_g", "gm_ln_b", "gm_w_s", "gm_b_s", "gm_w_out")


def _step(x, c, ctx, target, W, M, V):
    L, D = x.shape[1], x.shape[2]
    C = ctx.shape[1]
    T = L + C
    NL = W["ada_w"].shape[0]
    tm = 256 if C % 256 == 0 else 128
    nl = L // tm
    xi, yi, ci = _my_place()
    chip = 2 * xi + yi
    dev = 4 * xi + 2 * yi + ci
    segs2, segs1 = [(0, L), (L, C)], [(0, L)]
    vec = lambda a: a.reshape(1, -1)

    layer_sets = [[("cm_w_in", 0), ("cm_w_out", 0), ("ffn_w_up", 0), ("ffn_w_down", 0)],
                  [("attn_w_qkv", 0), ("attn_w_o", 0), ("ffn_w_up", 1), ("ffn_w_down", 1)],
                  [("gm_w_in", 0), ("gm_w_out", 0), ("ffn_w_up", 2), ("ffn_w_down", 2)],
                  [("cm_w_in", 1), ("cm_w_out", 1), ("ffn_w_up", 3), ("ffn_w_down", 3)]]
    arrived = {}

    def fetch(keys, zero, sequencer_id):
        exchanges = []
        for n, i in keys:
            shard = (W[n][i] + zero).astype(MMT)
            whole = list(shard.shape)
            whole[BIG_AXIS[n] - 1] *= N_CHIPS
            exchanges.append(([shard], [SDS(tuple(whole), MMT)], _gather_plan(BIG_AXIS[n] - 1), N_CHIPS - 1))
        lands = _sequencer_exchange(f"fetch_weights_{keys[0][0]}_{keys[0][1]}", sequencer_id, exchanges,
                                    _same_core_peers_and_sibling)
        for key, land in zip(keys, lands):
            arrived[key] = land[0]

    def big(n, i, after=None):
        return arrived[(n, i)]

    small_shapes = [W[n].shape for n in SMALL_SHARDED]
    ag1 = _all_gather(_pack([c.reshape(-1)] + [W[n] for n in SMALL_SHARDED]), "gather_small")
    parts = [_unpack(ag1[2 * s], [(D,)] + small_shapes) for s in range(N_CHIPS)]
    c_rows = jnp.stack([_unpack(ag1[d], [(D,)])[0] for d in range(N_DEV)])
    P = {n: _unshard_last(jnp.stack([parts[s][1 + i] for s in range(N_CHIPS)])) for i, n in enumerate(SMALL_SHARDED)}
    for n in SMALL_REPLICATED:
        P[n] = W[n]

    cond = jnp.concatenate([c_rows, W["c_ctx"][None, :], jnp.zeros((2 * SUBLANES - N_DEV - 1, D), F32)], axis=0)
    ncol = W["ada_w"].shape[2]
    ada_b_mine = lax.dynamic_slice_in_dim(W["ada_b"], chip * ncol, ncol, axis=1)[:, None, :]
    mods_mine = _ada_fwd(cond, W["ada_w"], ada_b_mine, "ada_fwd")
    ag2 = _all_gather(mods_mine.reshape(NL * 2 * SUBLANES, ncol), "gather_mods").reshape(N_DEV, NL, 2 * SUBLANES, ncol)
    mods_all = _unshard_last(jnp.stack([ag2[2 * s] for s in range(N_CHIPS)]))
    mod_lat = lax.dynamic_index_in_dim(mods_all, dev, axis=1, keepdims=False).reshape(NL, 6, D)
    mod_ctx = mods_all[:, N_DEV].reshape(NL, 6, D)
    mod2 = jnp.stack([mod_lat, mod_ctx], axis=1)
    mod1 = mod_lat[:, None]

    corner = mod2[0, 0, 0, 0]
    behind_small = jnp.where(corner != corner, corner, 0.0)
    for k in range(4):
        fetch(layer_sets[k], behind_small, FETCH_IDS[k])
    zero_d = jnp.zeros((1, D), F32)
    cos, sin = _rope_tables(L)
    nkv = D // HEAD_DIM // Q_PER_KV
    qdim, kvdim = D, nkv * HEAD_DIM

    def ffn_fwd(i, h, mod, rows, segs, tag):
        a2 = _prenorm(h, mod, vec(P["norm_g"][i, 2]), 1, rows, nl, tm, f"pre_ffn_{tag}")
        z0 = _mm(a2, big("ffn_w_up", i, a2), "nn", F32, f"ffn_up_{tag}")
        u = _ffn_gate(z0, P["ffn_conv_w"][i], vec(P["ffn_conv_b"][i]), segs, f"ffn_gate_{tag}")
        f = _mm(u, big("ffn_w_down", i, u), "nn", F32, f"ffn_down_{tag}")
        h_out = _postnorm(h, f, zero_d, mod, vec(P["norm_g"][i, 3]), 5, rows, nl, tm, f"post_ffn_{tag}")
        return h_out, dict(h=h, a2=a2, z0=z0, f=f)

    def ffn_bwd(i, dh, sv, mod, rows, segs, tag, G):
        df, dg2, dgn3, _ = _postnorm_bwd(dh, sv["f"], zero_d, mod, vec(P["norm_g"][i, 3]), 5, rows, nl, tm, f"post_ffn_bwd_{tag}")
        du = _mm(df, big("ffn_w_down", i), "nt", F32, f"ffn_down_dx_{tag}")
        u, dz0, dcw, dcb = _ffn_gate_bwd(sv["z0"], du, P["ffn_conv_w"][i], vec(P["ffn_conv_b"][i]), segs, f"ffn_gate_bwd_{tag}")
        G["ffn_w_down"][i] = _mm(u, df, "tn", MMT, f"ffn_down_dw_{tag}")
        G["ffn_w_up"][i] = _mm(sv["a2"], dz0, "tn", MMT, f"ffn_up_dw_{tag}")
        da2 = _mm(dz0, big("ffn_w_up", i), "nt", F32, f"ffn_up_dx_{tag}")
        dh, dsh2, dsc2, dgn2 = _prenorm_bwd(sv["h"], da2, dh, mod, vec(P["norm_g"][i, 2]), 1, rows, nl, tm, f"pre_ffn_bwd_{tag}")
        G["ffn_conv_w"][i], G["ffn_conv_b"][i] = dcw, dcb[0]
        return dh, (dsh2, dsc2, dg2), (dgn2, dgn3)

    def conformer_fwd(i, j, h, mod, rows, segs, tag):
        a = _prenorm(h, mod, vec(P["norm_g"][i, 0]), 0, rows, nl, tm, f"pre_mix_{tag}")
        p0 = _mm(a, big("cm_w_in", j, a), "nn", F32, f"cm_in_{tag}")
        z2 = _glu_conv(p0, vec(P["cm_b_in"][j]), P["cm_dw_w"][j], vec(P["cm_dw_b"][j]), segs, f"cm_conv_{tag}")
        z4 = _ln_silu(z2, vec(P["cm_ln_g"][j]), vec(P["cm_ln_b"][j]), rows, tm, f"cm_ln_{tag}")
        y = _mm(z4, big("cm_w_out", j, z4), "nn", F32, f"cm_out_{tag}")
        h_out = _postnorm(h, y, vec(P["cm_b_out"][j]), mod, vec(P["norm_g"][i, 1]), 2, rows, nl, tm, f"post_mix_{tag}")
        return h_out, dict(h=h, a=a, p0=p0, z2=z2, z4=z4, y=y)

    def conformer_bwd(i, j, dh, sv, mod, rows, segs, tag, G):
        dy, dg1, dgn1, dbo = _postnorm_bwd(dh, sv["y"], vec(P["cm_b_out"][j]) + zero_d, mod, vec(P["norm_g"][i, 1]), 2, rows, nl,
                                           tm, f"post_mix_bwd_{tag}")
        G["cm_w_out"][j] = _mm(sv["z4"], dy, "tn", MMT, f"cm_out_dw_{tag}")
        dz4 = _mm(dy, big("cm_w_out", j), "nt", F32, f"cm_out_dx_{tag}")
        dz2, dlg, dlb = _ln_silu_bwd(sv["z2"], dz4, vec(P["cm_ln_g"][j]), vec(P["cm_ln_b"][j]), rows, tm, f"cm_ln_bwd_{tag}")
        dpa, dpg, ddw, ddb, dba, dbg = _glu_conv_bwd(sv["p0"], vec(P["cm_b_in"][j]), P["cm_dw_w"][j], dz2, segs, f"cm_conv_bwd_{tag}")
        dp = jnp.concatenate([dpa, dpg], axis=1)
        G["cm_w_in"][j] = _mm(sv["a"], dp, "tn", MMT, f"cm_in_dw_{tag}")
        da = _mm(dp, big("cm_w_in", j), "nt", F32, f"cm_in_dx_{tag}")
        dh, dsh1, dsc1, dgn0 = _prenorm_bwd(sv["h"], da, dh, mod, vec(P["norm_g"][i, 0]), 0, rows, nl, tm, f"pre_mix_bwd_{tag}")
        G["cm_b_out"][j] = jnp.sum(dbo, axis=0)[0]
        G["cm_ln_g"][j], G["cm_ln_b"][j], G["cm_dw_w"][j], G["cm_dw_b"][j] = dlg[0], dlb[0], ddw, ddb[0]
        G["cm_b_in"][j] = jnp.concatenate([dba[0], dbg[0]])
        return dh, (dsh1, dsc1, dg1), (dgn0, dgn1)

    def heads(a, n):
        return a.reshape(a.shape[0], n, HEAD_DIM).transpose(1, 0, 2)

    def unheads(a):
        return a.transpose(1, 0, 2).reshape(a.shape[1], -1)

    G = {n: [None] * W[n].shape[0] for n in WEIGHTS if n not in ("c_ctx", "ada_w", "ada_b", "norm_g")}
    saved = []
    h = jnp.concatenate([x[0], ctx[0]], axis=0)
    h, s_mix = conformer_fwd(0, 0, h, mod2[0], T, segs2, "l0")
    h, s_ffn = ffn_fwd(0, h, mod2[0], T, segs2, "l0")
    saved.append((s_mix, s_ffn))
    a_all = _prenorm(h, mod2[1], vec(P["norm_g"][1, 0]), 0, T, nl, tm, "pre_mix_l1")
    qkv = _mm(a_all, big("attn_w_qkv", 0, a_all), "nn", F32, "attn_qkv")
    qk_rot, v_lat = _rope(qkv, cos, sin, L, qdim + kvdim, tm, "rope")
    q_h = heads(qk_rot[:, :qdim], nkv * Q_PER_KV).reshape(nkv, Q_PER_KV, L, HEAD_DIM)
    k_h, v_h = heads(qk_rot[:, qdim:], nkv), heads(v_lat, nkv)
    kc_h = heads(qkv[L:, qdim:qdim + kvdim].astype(MMT), nkv)
    vc_h = heads(qkv[L:, qdim + kvdim:].astype(MMT), nkv)
    sink = P["attn_sink"][0]
    o_h, lse = _attn_fwd(q_h, k_h, v_h, kc_h, vc_h, sink, "attn")
    o_nat = unheads(o_h.reshape(nkv * Q_PER_KV, L, HEAD_DIM)).astype(MMT)
    y1 = _mm(o_nat, big("attn_w_o", 0, o_nat), "nn", F32, "attn_out")
    h_in1 = h
    h = _postnorm(h, y1, zero_d, mod1[1], vec(P["norm_g"][1, 1]), 2, L, nl, tm, "post_mix_l1")
    h, s_ffn1 = ffn_fwd(1, h, mod1[1], L, segs1, "lat")
    h_in2 = h
    a_2 = _prenorm(h, mod1[2], vec(P["norm_g"][2, 0]), 0, L, nl, tm, "pre_mix_l2")
    p0_2 = _mm(a_2, big("gm_w_in", 0, a_2), "nn", F32, "gm_in")
    ws_bf = P["gm_w_s"][0].astype(MMT)
    bs_col = P["gm_b_s"][0][:, :, None]
    us = _gmlp_fwd(p0_2, vec(P["gm_b_in"][0]), vec(P["gm_ln_g"][0]), vec(P["gm_ln_b"][0]), ws_bf, bs_col, "gmlp")
    y2 = _mm(us, big("gm_w_out", 0, us), "nn", F32, "gm_out")
    h = _postnorm(h, y2, zero_d, mod1[2], vec(P["norm_g"][2, 1]), 2, L, nl, tm, "post_mix_l2")
    h, s_ffn2 = ffn_fwd(2, h, mod1[2], L, segs1, "lat")
    h, s_mix3 = conformer_fwd(3, 1, h, mod1[3], L, segs1, "l3")
    h, s_ffn3 = ffn_fwd(3, h, mod1[3], L, segs1, "lat")

    loss_mine, dh = _loss_head(h, target[0], tm, "loss_head")

    dmod = [None] * NL
    dgn = [None] * NL

    def finish(i, mix, ffn, gns_mix, gns_ffn):
        dmod[i] = jnp.concatenate(list(mix) + list(ffn), axis=1)
        dgn[i] = jnp.stack([jnp.sum(g, axis=0)[0] for g in (gns_mix[0], gns_mix[1], gns_ffn[0], gns_ffn[1])])

    sent, so_far = {}, {}

    def send(tag, collective_id, tensors):
        exchanges = []
        for n, l in tensors:
            g = G[n][l]
            shard = list(g.shape)
            shard[BIG_AXIS[n] - 1] //= N_CHIPS
            exchanges.append(([g], [SDS((N_CHIPS,) + tuple(shard), g.dtype)], _scatter_plan(BIG_AXIS[n] - 1), N_CHIPS - 1))
        sent[tag] = (tensors, _sequencer_exchange(f"send_grads_{tag}", collective_id, exchanges, _same_core_peers))
        corner = sum(G[n][l][0:1, 0:1].astype(F32) for n, l in tensors)
        return jnp.where(corner != corner, corner, 0.0)

    def land(tag, after):
        tensors, landed = sent[tag]
        mine = [_sum_slots(lands[0], f"sum_chips_{n}_{l}", after) for (n, l), lands in zip(tensors, landed)]
        theirs = _swap_with_sibling(mine, f"swap_cores_{tag}")
        for (n, l), a, b in zip(tensors, mine, theirs):
            so_far[n] = _adamw_layer(W[n], M[n], V[n], l, [a, b], so_far.get(n), f"adamw_{n}_{l}")

    dh, m_ffn, n_ffn = ffn_bwd(3, dh, s_ffn3, mod1[3], L, segs1, "lat", G)
    dh, m_mix, n_mix = conformer_bwd(3, 1, dh, s_mix3, mod1[3], L, segs1, "l3", G)
    finish(3, m_mix, m_ffn, n_mix, n_ffn)
    zero_d = zero_d + send("l3", SEND_IDS[0], [("ffn_w_up", 3), ("ffn_w_down", 3), ("cm_w_in", 1), ("cm_w_out", 1)])
    land("l3", None)

    dh, m_ffn, n_ffn = ffn_bwd(2, dh, s_ffn2, mod1[2], L, segs1, "lat", G)
    dy2, dg1, dgn1, _ = _postnorm_bwd(dh, y2, zero_d, mod1[2], vec(P["norm_g"][2, 1]), 2, L, nl, tm, "post_mix_bwd_l2")
    G["gm_w_out"][0] = _mm(us, dy2, "tn", MMT, "gm_out_dw")
    dus = _mm(dy2, big("gm_w_out", 0), "nt", F32, "gm_out_dx")
    ws_t = jnp.swapaxes(P["gm_w_s"][0], 1, 2).astype(MMT)
    dpre, dbi, dlg, dlb, dws, dbs = _gmlp_bwd(p0_2, dus, vec(P["gm_b_in"][0]), vec(P["gm_ln_g"][0]), vec(P["gm_ln_b"][0]), ws_bf,
                                              ws_t, bs_col, "gmlp_bwd")
    G["gm_w_in"][0] = _mm(a_2, dpre, "tn", MMT, "gm_in_dw")
    da = _mm(dpre, big("gm_w_in", 0), "nt", F32, "gm_in_dx")
    dh, dsh1, dsc1, dgn0 = _prenorm_bwd(h_in2, da, dh, mod1[2], vec(P["norm_g"][2, 0]), 0, L, nl, tm, "pre_mix_bwd_l2")
    G["gm_b_in"][0], G["gm_ln_g"][0], G["gm_ln_b"][0], G["gm_w_s"][0], G["gm_b_s"][0] = dbi[0], dlg[0], dlb[0], dws, dbs[:, :, 0]
    finish(2, (dsh1, dsc1, dg1), m_ffn, (dgn0, dgn1), n_ffn)
    zero_d = zero_d + send("l2", SEND_IDS[1], [("ffn_w_up", 2), ("ffn_w_down", 2), ("gm_w_in", 0), ("gm_w_out", 0)])
    land("l2", None)

    dh, m_ffn, n_ffn = ffn_bwd(1, dh, s_ffn1, mod1[1], L, segs1, "lat", G)
    dy1, dg1, dgn1, _ = _postnorm_bwd(dh, y1, zero_d, mod1[1], vec(P["norm_g"][1, 1]), 2, L, nl, tm, "post_mix_bwd_l1")
    G["attn_w_o"][0] = _mm(o_nat, dy1, "tn", MMT, "attn_out_dw")
    do_nat = _mm(dy1, big("attn_w_o", 0), "nt", MMT, "attn_out_dx")
    do_h = heads(do_nat, nkv * Q_PER_KV).reshape(nkv, Q_PER_KV, L, HEAD_DIM)
    dq_h, dkc_h, dvc_h, dsk = _attn_bwd_q(q_h, k_h, v_h, kc_h, vc_h, sink, o_h, do_h, lse, "attn_bwd_q")
    dk_h, dv_h = _attn_bwd_kv(q_h, k_h, v_h, o_h, do_h, lse, "attn_bwd_kv")
    dqk = jnp.concatenate([unheads(dq_h.reshape(nkv * Q_PER_KV, L, HEAD_DIM)), unheads(dk_h)], axis=1)
    dqkv_lat = _rope_bwd(dqk, unheads(dv_h), cos, sin, tm, "rope_bwd")
    dqkv_ctx = jnp.concatenate([jnp.zeros((C, qdim), MMT), unheads(dkc_h).astype(MMT), unheads(dvc_h).astype(MMT)], axis=1)
    dqkv = jnp.concatenate([dqkv_lat, dqkv_ctx], axis=0)
    G["attn_w_qkv"][0] = _mm(a_all, dqkv, "tn", MMT, "attn_qkv_dw")
    da_all = _mm(dqkv, big("attn_w_qkv", 0), "nt", F32, "attn_qkv_dx")
    dh_all = jnp.concatenate([dh, jnp.zeros((C, D), F32)], axis=0)
    dh, dsh1, dsc1, dgn0 = _prenorm_bwd(h_in1, da_all, dh_all, mod2[1], vec(P["norm_g"][1, 0]), 0, T, nl, tm, "pre_mix_bwd_l1")
    G["attn_sink"][0] = dsk[:, :Q_PER_KV, 0].reshape(-1)
    pad_ctx = lambda a: jnp.concatenate([a, jnp.zeros_like(a)], axis=0)
    finish(1, (dsh1, dsc1, pad_ctx(dg1)), [pad_ctx(a) for a in m_ffn], (dgn0, dgn1), n_ffn)
    zero_d = zero_d + send("l1", SEND_IDS[2], [("ffn_w_up", 1), ("ffn_w_down", 1), ("attn_w_qkv", 0), ("attn_w_o", 0)])
    land("l1", None)

    s_mix0, s_ffn0 = saved[0]
    dh, m_ffn, n_ffn = ffn_bwd(0, dh, s_ffn0, mod2[0], T, segs2, "l0", G)
    zero_d = zero_d + send("l0_ffn", SEND_IDS[3], [("ffn_w_up", 0), ("ffn_w_down", 0)])
    dh, m_mix, n_mix = conformer_bwd(0, 0, dh, s_mix0, mod2[0], T, segs2, "l0", G)
    finish(0, m_mix, m_ffn, n_mix, n_ffn)
    grad_x = dh[:L][None]
    sent_l0 = send("l0_mix", SEND_IDS[4], [("cm_w_in", 0), ("cm_w_out", 0)])

    for i in range(2, NL):
        dmod[i] = pad_ctx(dmod[i])
    dmod_all = jnp.stack(dmod).reshape(NL, 2, 6 * D) + sent_l0

    ag3 = _all_gather(dmod_all.reshape(NL * 2, 6 * D), "gather_dmods").reshape(N_DEV, NL, 2, N_CHIPS, ncol)
    dm_cols = lax.dynamic_index_in_dim(ag3, chip, axis=3, keepdims=False)
    dm_lat, dm_ctx = jnp.moveaxis(dm_cols[:, :, 0], 0, 1), jnp.moveaxis(dm_cols[:, :, 1], 0, 1)
    g_ada_w, dsilu = _ada_bwd(cond, W["ada_w"], dm_lat, dm_ctx, "ada_bwd")
    cc = W["c_ctx"]
    sg = jax.nn.sigmoid(cc)
    dcctx_part = jnp.where(ci == 0, 1.0, 0.0) * dsilu[N_DEV] * (sg * (1.0 + cc * (1.0 - sg)))

    Gs = {n: jnp.stack(G[n]) for n in G if n not in BIG}
    Gs["norm_g"] = jnp.stack(dgn)
    Gs["ada_b"] = jnp.sum(dmod_all, axis=1)
    Gs["c_ctx"] = dcctx_part
    small_names = list(SMALL_SHARDED) + list(SMALL_REPLICATED)
    small_full_shapes = [P[n].shape for n in small_names]
    small_pack = _pack([Gs[n] for n in small_names])
    ((ag4,),) = _sequencer_exchange("gather_small_grads", SMALL_GRADS_ID, [
        ([small_pack], [SDS((N_DEV,) + small_pack.shape, F32)], _all_gather_plan, N_DEV - 1)], _all_peers)

    flat2 = lambda a: a.reshape(-1, a.shape[-1])
    res = {}
    outs = _adamw(flat2(W["ada_w"]), flat2(M["ada_w"]), flat2(V["ada_w"]), [flat2(g_ada_w)], "adamw_ada_w")
    res["ada_w"] = tuple(o.reshape(W["ada_w"].shape) for o in outs)

    land("l0_ffn", outs[0])
    land("l0_mix", so_far["ffn_w_up"][0])
    for n in BIG:
        res[n] = tuple(so_far[n])

    small_sum = _unpack(_sum_slots(ag4, "sum_small_grads"), small_full_shapes)
    g_small = {}
    for n, g in zip(small_names, small_sum):
        g_small[n] = _my_block_last(g, chip) if n in SMALL_SHARDED else g
    packed = [_pack([d[n] for n in small_names]) for d in (W, M, V)]
    outs_small = _adamw(packed[0], packed[1], packed[2], [_pack([g_small[n] for n in small_names])], "adamw_small")
    shard_shapes = [W[n].shape for n in small_names]
    for k, n in enumerate(small_names):
        res[n] = tuple(_unpack(o, shard_shapes)[k] for o in outs_small)

    loss = lax.psum(loss_mine[0, 0], ("x", "y", "c"))
    return (loss, grad_x) + tuple(res[n][k] for k in range(4) for n in WEIGHTS)


def kernel(x, c, ctx, c_ctx, ada_w, ada_b, norm_g, ffn_w_up, ffn_conv_w, ffn_conv_b, ffn_w_down, cm_w_in, cm_b_in, cm_dw_w, cm_dw_b, cm_ln_g, cm_ln_b, cm_w_out, cm_b_out, attn_w_qkv, attn_sink, attn_w_o, gm_w_in, gm_b_in, gm_ln_g, gm_ln_b, gm_w_s, gm_b_s, gm_w_out, loss_target, m_c_ctx, m_ada_w, m_ada_b, m_norm_g, m_ffn_w_up, m_ffn_conv_w, m_ffn_conv_b, m_ffn_w_down, m_cm_w_in, m_cm_b_in, m_cm_dw_w, m_cm_dw_b, m_cm_ln_g, m_cm_ln_b, m_cm_w_out, m_cm_b_out, m_attn_w_qkv, m_attn_sink, m_attn_w_o, m_gm_w_in, m_gm_b_in, m_gm_ln_g, m_gm_ln_b, m_gm_w_s, m_gm_b_s, m_gm_w_out, v_c_ctx, v_ada_w, v_ada_b, v_norm_g, v_ffn_w_up, v_ffn_conv_w, v_ffn_conv_b, v_ffn_w_down, v_cm_w_in, v_cm_b_in, v_cm_dw_w, v_cm_dw_b, v_cm_ln_g, v_cm_ln_b, v_cm_w_out, v_cm_b_out, v_attn_w_qkv, v_attn_sink, v_attn_w_o, v_gm_w_in, v_gm_b_in, v_gm_ln_g, v_gm_ln_b, v_gm_w_s, v_gm_b_s, v_gm_w_out):
    args = locals()
    W = {n: args[n] for n in WEIGHTS}
    M = {n: args["m_" + n] for n in WEIGHTS}
    V = {n: args["v_" + n] for n in WEIGHTS}
    return _step(x, c, ctx, loss_target, W, M, V)
```

```python
import functools
import math

import jax
import jax.numpy as jnp
from jax import lax
from jax.experimental import pallas as pl
from jax.experimental.pallas import tpu as pltpu
from jax.experimental.pallas import tpu_sc as plsc

F32 = jnp.float32
MMT = jnp.bfloat16
SDS = jax.ShapeDtypeStruct
MESH = pl.DeviceIdType.MESH

EPS = 1e-6
HEAD_DIM = 64
Q_PER_KV = 4
ATTN_BLOCK = 128
GRID_W = 64
ROPE_BASE = 10000.0
GMLP_CHUNK = 128
GMLP_GROUP_DIM = 128
CONV_WIDTH = 31
FFN_CONV_WIDTH = 3
NEG = -1e30

ADAM_LR, ADAM_B1, ADAM_B2, ADAM_EPS, ADAM_WD, ADAM_STEP = 0.001, 0.9, 0.999, 1e-08, 0.01, 10

LANES = 128
SUBLANES = 8
VMEM_LIMIT = 52 * 1024 * 1024
CONV_ROWS = 128
N_CHIPS = 4
N_DEV = 8
N_SEQUENCERS = 2
FETCH_IDS = (1, 2, 3, 4, 11)
SEND_IDS = (5, 6, 7, 8, 9)
SMALL_GRADS_ID = 10


def _cparams(*sem):
    return pltpu.CompilerParams(dimension_semantics=sem if sem else None, vmem_limit_bytes=VMEM_LIMIT)


def _tile(n, cap, mult=LANES):
    best = None
    for d in range(mult, min(n, cap) + 1, mult):
        if n % d == 0:
            best = d
    return best if best is not None else n


def _sum0(v):
    return jnp.sum(v, axis=0, keepdims=True)


def _rms(v):
    r = lax.rsqrt(jnp.mean(v * v, axis=-1, keepdims=True) + EPS)
    return v * r, r


def _sig(v):
    return jax.nn.sigmoid(v)


def _dot(a, b, ca, cb):
    return lax.dot_general(a.astype(MMT), b.astype(MMT), (((ca,), (cb,)), ((), ())), preferred_element_type=F32)


def _mm(a, b, mode, out_dtype, name):
    if mode == "nn":
        (M, K), N = a.shape, b.shape[1]
    elif mode == "nt":
        (M, K), N = a.shape, b.shape[0]
    else:
        (K, M), N = a.shape, b.shape[1]
    tm, tn, tk = _tile(M, 512), _tile(N, 1408), _tile(K, 1536)
    nk = K // tk
    ca, cb = {"nn": (1, 0), "nt": (1, 1), "tn": (0, 0)}[mode]

    def body(a_ref, b_ref, o_ref, acc):
        k = pl.program_id(2)

        @pl.when(k == 0)
        def _():
            acc[...] = jnp.zeros_like(acc)

        acc[...] += _dot(a_ref[...], b_ref[...], ca, cb)

        @pl.when(k == nk - 1)
        def _():
            o_ref[...] = acc[...].astype(o_ref.dtype)

    a_spec = pl.BlockSpec((tk, tm), lambda i, j, k: (k, i)) if mode == "tn" else pl.BlockSpec((tm, tk), lambda i, j, k: (i, k))
    b_spec = pl.BlockSpec((tn, tk), lambda i, j, k: (j, k)) if mode == "nt" else pl.BlockSpec((tk, tn), lambda i, j, k: (k, j))
    return pl.pallas_call(
        body, name=name, grid=(M // tm, N // tn, nk), in_specs=[a_spec, b_spec],
        out_specs=pl.BlockSpec((tm, tn), lambda i, j, k: (i, j)), out_shape=SDS((M, N), out_dtype),
        scratch_shapes=[pltpu.VMEM((tm, tn), F32)], compiler_params=_cparams("parallel", "parallel", "arbitrary"))(a, b)


def _seg_of(nl, nseg):
    return (lambda i: jnp.where(i >= nl, 1, 0)) if nseg == 2 else (lambda i: 0)


def _prenorm(h, mod, gn, which, rows, nl, tm, name):
    D = h.shape[1]
    nseg = mod.shape[0]
    seg = _seg_of(nl, nseg)
    sh_i, sc_i = (0, 1) if which == 0 else (3, 4)

    def body(h_ref, mod_ref, gn_ref, a_ref):
        n, _ = _rms(h_ref[...])
        a_ref[...] = (n * gn_ref[...] * (1.0 + mod_ref[pl.ds(sc_i, 1), :]) + mod_ref[pl.ds(sh_i, 1), :]).astype(a_ref.dtype)

    return pl.pallas_call(
        body, name=name, grid=(rows // tm,),
        in_specs=[pl.BlockSpec((tm, D), lambda i: (i, 0)), pl.BlockSpec((None, 6, D), lambda i: (seg(i), 0, 0)),
                  pl.BlockSpec((1, D), lambda i: (0, 0))],
        out_specs=pl.BlockSpec((tm, D), lambda i: (i, 0)), out_shape=SDS((rows, D), MMT),
        compiler_params=_cparams("parallel"))(h, mod, gn)


def _acc_spec(D, seg):
    return pl.BlockSpec((None, 1, D), lambda i: (seg(i), 0, 0))


def _prenorm_bwd(h, da, dh_in, mod, gn, which, rows, nl, tm, name):
    D = h.shape[1]
    nseg = mod.shape[0]
    seg = _seg_of(nl, nseg)
    sc_i = 1 if which == 0 else 4

    def body(h_ref, da_ref, dhin_ref, mod_ref, gn_ref, dh_ref, dsh_ref, dsc_ref, dgn_ref):
        i = pl.program_id(0)
        first = (i == 0) | (i == nl) if nseg == 2 else (i == 0)

        @pl.when(first)
        def _():
            dsh_ref[...] = jnp.zeros_like(dsh_ref)
            dsc_ref[...] = jnp.zeros_like(dsc_ref)
            dgn_ref[...] = jnp.zeros_like(dgn_ref)

        n, r = _rms(h_ref[...])
        da_v = da_ref[...].astype(F32)
        gn_v = gn_ref[...]
        sc1 = 1.0 + mod_ref[pl.ds(sc_i, 1), :]
        dsh_ref[...] += _sum0(da_v)
        dsc_ref[...] += _sum0(da_v * (n * gn_v))
        dgn_ref[...] += _sum0(da_v * n * sc1)
        dn = da_v * (gn_v * sc1)
        dh_ref[...] = dhin_ref[...] + r * (dn - n * jnp.mean(dn * n, axis=-1, keepdims=True))

    row = pl.BlockSpec((tm, D), lambda i: (i, 0))
    acc = SDS((nseg, 1, D), F32)
    return pl.pallas_call(
        body, name=name, grid=(rows // tm,),
        in_specs=[row, row, row, pl.BlockSpec((None, 6, D), lambda i: (seg(i), 0, 0)), pl.BlockSpec((1, D), lambda i: (0, 0))],
        out_specs=[row, _acc_spec(D, seg), _acc_spec(D, seg), _acc_spec(D, seg)],
        out_shape=[SDS((rows, D), F32), acc, acc, acc], compiler_params=_cparams("arbitrary"))(h, da, dh_in, mod, gn)


def _postnorm(h, y, bias, mod, gn, gate_i, rows, nl, tm, name):
    D = h.shape[1]
    nseg = mod.shape[0]
    seg = _seg_of(nl, nseg)

    def body(h_ref, y_ref, b_ref, mod_ref, gn_ref, o_ref):
        ny, _ = _rms(y_ref[...] + b_ref[...])
        o_ref[...] = h_ref[...] + mod_ref[pl.ds(gate_i, 1), :] * (ny * gn_ref[...])

    row = pl.BlockSpec((tm, D), lambda i: (i, 0))
    vec = pl.BlockSpec((1, D), lambda i: (0, 0))
    return pl.pallas_call(
        body, name=name, grid=(rows // tm,),
        in_specs=[row, row, vec, pl.BlockSpec((None, 6, D), lambda i: (seg(i), 0, 0)), vec],
        out_specs=row, out_shape=SDS((rows, D), F32), compiler_params=_cparams("parallel"))(h, y, bias, mod, gn)


def _postnorm_bwd(dh, y, bias, mod, gn, gate_i, rows, nl, tm, name):
    D = y.shape[1]
    nseg = mod.shape[0]
    seg = _seg_of(nl, nseg)

    def body(dh_ref, y_ref, b_ref, mod_ref, gn_ref, dy_ref, dg_ref, dgn_ref, db_ref):
        i = pl.program_id(0)
        first = (i == 0) | (i == nl) if nseg == 2 else (i == 0)

        @pl.when(first)
        def _():
            dg_ref[...] = jnp.zeros_like(dg_ref)
            dgn_ref[...] = jnp.zeros_like(dgn_ref)
            db_ref[...] = jnp.zeros_like(db_ref)

        ny, ry = _rms(y_ref[...] + b_ref[...])
        g = mod_ref[pl.ds(gate_i, 1), :]
        gn_v = gn_ref[...]
        dh_v = dh_ref[...]
        dg_ref[...] += _sum0(dh_v * (ny * gn_v))
        dgn_ref[...] += _sum0(dh_v * ny * g)
        dny = dh_v * (g * gn_v)
        dy = ry * (dny - ny * jnp.mean(dny * ny, axis=-1, keepdims=True))
        db_ref[...] += _sum0(dy)
        dy_ref[...] = dy.astype(dy_ref.dtype)

    row = pl.BlockSpec((tm, D), lambda i: (i, 0))
    vec = pl.BlockSpec((1, D), lambda i: (0, 0))
    acc = SDS((nseg, 1, D), F32)
    return pl.pallas_call(
        body, name=name, grid=(rows // tm,),
        in_specs=[row, row, vec, pl.BlockSpec((None, 6, D), lambda i: (seg(i), 0, 0)), vec],
        out_specs=[row, _acc_spec(D, seg), _acc_spec(D, seg), _acc_spec(D, seg)],
        out_shape=[SDS((rows, D), MMT), acc, acc, acc], compiler_params=_cparams("arbitrary"))(dh, y, bias, mod, gn)


def _seg_layout(segs, H):
    out, base = [], H
    for s0, n in segs:
        out.append((s0, n, base))
        base += n + H
    return out, base


def _zero_pads(ref, lay, H):
    width = ref.shape[1]
    ref[pl.ds(0, H), :] = jnp.zeros((H, width), ref.dtype)
    for _, n, base in lay:
        ref[pl.ds(base + n, H), :] = jnp.zeros((H, width), ref.dtype)


def _window(ref, base, off, H):
    return ref[pl.ds(base - H + off, CONV_ROWS + 2 * H), :]


def _taps(win, H, offs):
    W = CONV_ROWS + 2 * H
    rolled, out = {}, {}
    for o in offs:
        s = H + o
        b = s % SUBLANES
        if b not in rolled:
            rolled[b] = win if b == 0 else pltpu.roll(win, shift=W - b, axis=0)
        out[o] = rolled[b][s - b:s - b + CONV_ROWS, :]
    return out


def _chunks(lay, fn):
    for s0, n, base in lay:
        def step(r, carry, s0=s0, base=base):
            fn(s0, base, pl.multiple_of(r * CONV_ROWS, CONV_ROWS))
            return carry
        lax.fori_loop(0, n // CONV_ROWS, step, 0)


def _ffn_gate(z0, conv_w, conv_b, segs, name):
    T, F2 = z0.shape
    F = F2 // 2
    tc = _tile(F, 256)
    nF = F // tc
    H = SUBLANES
    lay, srows = _seg_layout(segs, H)
    offs = [-1, 0, 1]

    def body(zg_ref, zv_ref, wg_ref, wv_ref, bg_ref, bv_ref, u_ref, xg, xv):
        _zero_pads(xg, lay, H)
        _zero_pads(xv, lay, H)
        for s0, n, base in lay:
            xg[pl.ds(base, n), :] = zg_ref[pl.ds(s0, n), :]
            xv[pl.ds(base, n), :] = zv_ref[pl.ds(s0, n), :]

        def chunk(s0, base, off):
            tg = _taps(_window(xg, base, off, H), H, offs)
            tv = _taps(_window(xv, base, off, H), H, offs)
            zg = bg_ref[...] + sum(tg[k - 1] * wg_ref[pl.ds(k, 1), :] for k in range(3))
            zv = bv_ref[...] + sum(tv[k - 1] * wv_ref[pl.ds(k, 1), :] for k in range(3))
            u_ref[pl.ds(s0 + off, CONV_ROWS), :] = (zg * _sig(zg) * zv).astype(u_ref.dtype)

        _chunks(lay, chunk)

    colg = lambda r: pl.BlockSpec((r, tc), lambda j: (0, j))
    colv = lambda r: pl.BlockSpec((r, tc), lambda j: (0, j + nF))
    return pl.pallas_call(
        body, name=name, grid=(nF,),
        in_specs=[colg(T), colv(T), colg(3), colv(3), colg(1), colv(1)],
        out_specs=colg(T), out_shape=SDS((T, F), MMT),
        scratch_shapes=[pltpu.VMEM((srows, tc), F32), pltpu.VMEM((srows, tc), F32)],
        compiler_params=_cparams("parallel"))(z0, z0, conv_w, conv_w, conv_b, conv_b)


def _ffn_gate_bwd(z0, du, conv_w, conv_b, segs, name):
    T, F2 = z0.shape
    F = F2 // 2
    tc = _tile(F, 256)
    nF = F // tc
    H = SUBLANES
    lay, srows = _seg_layout(segs, H)
    offs = [-1, 0, 1]

    def body(zo_ref, zt_ref, du_ref, wo_ref, wt_ref, bo_ref, bt_ref, u_ref, dz0_ref, dw_ref, db_ref, xo, xt, dzp):
        own_is_gate = pl.program_id(1) == 0
        for ref in (xo, xt, dzp):
            _zero_pads(ref, lay, H)
        for s0, n, base in lay:
            xo[pl.ds(base, n), :] = zo_ref[pl.ds(s0, n), :]
            xt[pl.ds(base, n), :] = zt_ref[pl.ds(s0, n), :]

        def grads(s0, base, off):
            to = _taps(_window(xo, base, off, H), H, offs)
            tt = _taps(_window(xt, base, off, H), H, offs)
            zo = bo_ref[...] + sum(to[k - 1] * wo_ref[pl.ds(k, 1), :] for k in range(3))
            zt = bt_ref[...] + sum(tt[k - 1] * wt_ref[pl.ds(k, 1), :] for k in range(3))
            so, st = _sig(zo), _sig(zt)
            du_v = du_ref[pl.ds(s0 + off, CONV_ROWS), :]
            d_gate = du_v * zt * (so * (1.0 + zo * (1.0 - so)))
            d_val = du_v * (zt * st)
            dzp[pl.ds(base + off, CONV_ROWS), :] = jnp.where(own_is_gate, d_gate, d_val)

            @pl.when(own_is_gate)
            def _():
                u_ref[pl.ds(s0 + off, CONV_ROWS), :] = (zo * so * zt).astype(u_ref.dtype)

        _chunks(lay, grads)
        dw_ref[...] = jnp.zeros_like(dw_ref)
        db_ref[...] = jnp.zeros_like(db_ref)

        def back(s0, base, off):
            td = _taps(_window(dzp, base, off, H), H, offs)
            tx = _taps(_window(xo, base, off, H), H, offs)
            dz0 = sum(td[1 - k] * wo_ref[pl.ds(k, 1), :] for k in range(3))
            dz0_ref[pl.ds(s0 + off, CONV_ROWS), :] = dz0.astype(dz0_ref.dtype)
            db_ref[...] += _sum0(td[0])
            for k in range(3):
                dw_ref[pl.ds(k, 1), :] += _sum0(td[0] * tx[k - 1])

        _chunks(lay, back)

    own = lambda r: pl.BlockSpec((r, tc), lambda j, hf: (0, hf * nF + j))
    oth = lambda r: pl.BlockSpec((r, tc), lambda j, hf: (0, (1 - hf) * nF + j))
    ucol = pl.BlockSpec((T, tc), lambda j, hf: (0, j))
    return pl.pallas_call(
        body, name=name, grid=(nF, 2),
        in_specs=[own(T), oth(T), ucol, own(3), oth(3), own(1), oth(1)],
        out_specs=[ucol, own(T), own(3), own(1)],
        out_shape=[SDS((T, F), MMT), SDS((T, F2), MMT), SDS((3, F2), F32), SDS((1, F2), F32)],
        scratch_shapes=[pltpu.VMEM((srows, tc), F32)] * 3,
        compiler_params=_cparams("parallel", "arbitrary"))(z0, z0, du, conv_w, conv_w, conv_b, conv_b)


def _glu_conv(p0, b_in, dw_w, dw_b, segs, name):
    T, D2 = p0.shape
    D = D2 // 2
    tc = _tile(D, 256)
    nD = D // tc
    H = 2 * SUBLANES
    half = (CONV_WIDTH - 1) // 2
    lay, srows = _seg_layout(segs, H)
    offs = list(range(-half, half + 1))

    def body(pa_ref, pg_ref, ba_ref, bg_ref, w_ref, b_ref, z2_ref, z1p):
        _zero_pads(z1p, lay, H)

        def glu(s0, base, off):
            rows = pl.ds(s0 + off, CONV_ROWS)
            z1p[pl.ds(base + off, CONV_ROWS), :] = (pa_ref[rows, :] + ba_ref[...]) * _sig(pg_ref[rows, :] + bg_ref[...])

        _chunks(lay, glu)

        def conv(s0, base, off):
            t = _taps(_window(z1p, base, off, H), H, offs)
            acc = b_ref[...] + t[-half] * w_ref[pl.ds(0, 1), :]
            for k in range(1, CONV_WIDTH):
                acc = acc + t[k - half] * w_ref[pl.ds(k, 1), :]
            z2_ref[pl.ds(s0 + off, CONV_ROWS), :] = acc

        _chunks(lay, conv)

    cola = lambda r: pl.BlockSpec((r, tc), lambda j: (0, j))
    colg = lambda r: pl.BlockSpec((r, tc), lambda j: (0, j + nD))
    return pl.pallas_call(
        body, name=name, grid=(nD,),
        in_specs=[cola(T), colg(T), cola(1), colg(1), cola(CONV_WIDTH), cola(1)],
        out_specs=cola(T), out_shape=SDS((T, D), F32), scratch_shapes=[pltpu.VMEM((srows, tc), F32)],
        compiler_params=_cparams("parallel"))(p0, p0, b_in, b_in, dw_w, dw_b)


def _glu_conv_bwd(p0, b_in, dw_w, dz2, segs, name):
    T, D2 = p0.shape
    D = D2 // 2
    tc = _tile(D, 256)
    nD = D // tc
    H = 2 * SUBLANES
    half = (CONV_WIDTH - 1) // 2
    lay, srows = _seg_layout(segs, H)
    offs = list(range(-half, half + 1))

    def body(pa_ref, pg_ref, ba_ref, bg_ref, w_ref, dz2_ref, dpa_ref, dpg_ref, dw_ref, db_ref, dba_ref, dbg_ref, z1p, dzp):
        _zero_pads(z1p, lay, H)
        _zero_pads(dzp, lay, H)
        for s0, n, base in lay:
            dzp[pl.ds(base, n), :] = dz2_ref[pl.ds(s0, n), :]

        def glu(s0, base, off):
            rows = pl.ds(s0 + off, CONV_ROWS)
            z1p[pl.ds(base + off, CONV_ROWS), :] = (pa_ref[rows, :] + ba_ref[...]) * _sig(pg_ref[rows, :] + bg_ref[...])

        _chunks(lay, glu)
        for ref in (dw_ref, db_ref, dba_ref, dbg_ref):
            ref[...] = jnp.zeros_like(ref)

        def back(s0, base, off):
            td = _taps(_window(dzp, base, off, H), H, offs)
            tz = _taps(_window(z1p, base, off, H), H, offs)
            dz1 = td[half] * w_ref[pl.ds(0, 1), :]
            for k in range(1, CONV_WIDTH):
                dz1 = dz1 + td[half - k] * w_ref[pl.ds(k, 1), :]
            db_ref[...] += _sum0(td[0])
            for k in range(CONV_WIDTH):
                dw_ref[pl.ds(k, 1), :] += _sum0(td[0] * tz[k - half])
            rows = pl.ds(s0 + off, CONV_ROWS)
            pa = pa_ref[rows, :] + ba_ref[...]
            sg = _sig(pg_ref[rows, :] + bg_ref[...])
            dpa = dz1 * sg
            dpg = dz1 * pa * (sg * (1.0 - sg))
            dba_ref[...] += _sum0(dpa)
            dbg_ref[...] += _sum0(dpg)
            dpa_ref[rows, :] = dpa.astype(dpa_ref.dtype)
            dpg_ref[rows, :] = dpg.astype(dpg_ref.dtype)

        _chunks(lay, back)

    cola = lambda r: pl.BlockSpec((r, tc), lambda j: (0, j))
    colg = lambda r: pl.BlockSpec((r, tc), lambda j: (0, j + nD))
    return pl.pallas_call(
        body, name=name, grid=(nD,),
        in_specs=[cola(T), colg(T), cola(1), colg(1), cola(CONV_WIDTH), cola(T)],
        out_specs=[cola(T), cola(T), cola(CONV_WIDTH), cola(1), cola(1), cola(1)],
        out_shape=[SDS((T, D), MMT), SDS((T, D), MMT), SDS((CONV_WIDTH, D), F32), SDS((1, D), F32), SDS((1, D), F32),
                   SDS((1, D), F32)],
        scratch_shapes=[pltpu.VMEM((srows, tc), F32)] * 2, compiler_params=_cparams("parallel"))(p0, p0, b_in, b_in, dw_w, dz2)


def _layer_norm_stats(v):
    mu = jnp.mean(v, axis=-1, keepdims=True)
    var = jnp.mean(jnp.square(v - mu), axis=-1, keepdims=True)
    rstd = lax.rsqrt(var + EPS)
    return (v - mu) * rstd, rstd


def _ln_silu(z2, ln_g, ln_b, rows, tm, name):
    D = z2.shape[1]

    def body(z_ref, g_ref, b_ref, o_ref):
        xh, _ = _layer_norm_stats(z_ref[...])
        z3 = xh * g_ref[...] + b_ref[...]
        o_ref[...] = (z3 * _sig(z3)).astype(o_ref.dtype)

    row = pl.BlockSpec((tm, D), lambda i: (i, 0))
    vec = pl.BlockSpec((1, D), lambda i: (0, 0))
    return pl.pallas_call(body, name=name, grid=(rows // tm,), in_specs=[row, vec, vec], out_specs=row,
                          out_shape=SDS((rows, D), MMT), compiler_params=_cparams("parallel"))(z2, ln_g, ln_b)


def _ln_silu_bwd(z2, dz4, ln_g, ln_b, rows, tm, name):
    D = z2.shape[1]

    def body(z_ref, d_ref, g_ref, b_ref, dz_ref, dg_ref, db_ref):
        @pl.when(pl.program_id(0) == 0)
        def _():
            dg_ref[...] = jnp.zeros_like(dg_ref)
            db_ref[...] = jnp.zeros_like(db_ref)

        xh, rstd = _layer_norm_stats(z_ref[...])
        z3 = xh * g_ref[...] + b_ref[...]
        s = _sig(z3)
        dz3 = d_ref[...] * (s * (1.0 + z3 * (1.0 - s)))
        dg_ref[...] += _sum0(dz3 * xh)
        db_ref[...] += _sum0(dz3)
        dxh = dz3 * g_ref[...]
        dz_ref[...] = rstd * (dxh - jnp.mean(dxh, axis=-1, keepdims=True) - xh * jnp.mean(dxh * xh, axis=-1, keepdims=True))

    row = pl.BlockSpec((tm, D), lambda i: (i, 0))
    vec = pl.BlockSpec((1, D), lambda i: (0, 0))
    return pl.pallas_call(body, name=name, grid=(rows // tm,), in_specs=[row, row, vec, vec], out_specs=[row, vec, vec],
                          out_shape=[SDS((rows, D), F32), SDS((1, D), F32), SDS((1, D), F32)],
                          compiler_params=_cparams("arbitrary"))(z2, dz4, ln_g, ln_b)


def _rot_half_pairs(v):
    width = v.shape[1]
    lane = lax.broadcasted_iota(jnp.int32, v.shape, 1)
    return jnp.where((lane % 32) < 16, -pltpu.roll(v, shift=width - 16, axis=1), pltpu.roll(v, shift=16, axis=1))


def _rope(qkv, cos, sin, L, qk, tm, name):
    width = qkv.shape[1]
    kv = width - qk

    def body(x_ref, c_ref, s_ref, qk_ref, v_ref):
        xv = x_ref[:, pl.ds(0, qk)]
        c = jnp.tile(c_ref[...], (1, qk // LANES))
        s = jnp.tile(s_ref[...], (1, qk // LANES))
        qk_ref[...] = (xv * c + _rot_half_pairs(xv) * s).astype(qk_ref.dtype)
        v_ref[...] = x_ref[:, pl.ds(qk, kv)].astype(v_ref.dtype)

    tab = pl.BlockSpec((tm, LANES), lambda i: (i, 0))
    return pl.pallas_call(
        body, name=name, grid=(L // tm,), in_specs=[pl.BlockSpec((tm, width), lambda i: (i, 0)), tab, tab],
        out_specs=[pl.BlockSpec((tm, qk), lambda i: (i, 0)), pl.BlockSpec((tm, kv), lambda i: (i, 0))],
        out_shape=[SDS((L, qk), MMT), SDS((L, kv), MMT)], compiler_params=_cparams("parallel"))(qkv, cos, sin)


def _rope_bwd(dqk, dv, cos, sin, tm, name):
    L, qk = dqk.shape
    kv = dv.shape[1]

    def body(d_ref, dv_ref, c_ref, s_ref, o_ref):
        dv_ = d_ref[...]
        c = jnp.tile(c_ref[...], (1, qk // LANES))
        s = jnp.tile(s_ref[...], (1, qk // LANES))
        o_ref[:, pl.ds(0, qk)] = (dv_ * c - _rot_half_pairs(dv_ * s)).astype(o_ref.dtype)
        o_ref[:, pl.ds(qk, kv)] = dv_ref[...].astype(o_ref.dtype)

    tab = pl.BlockSpec((tm, LANES), lambda i: (i, 0))
    return pl.pallas_call(
        body, name=name, grid=(L // tm,),
        in_specs=[pl.BlockSpec((tm, qk), lambda i: (i, 0)), pl.BlockSpec((tm, kv), lambda i: (i, 0)), tab, tab],
        out_specs=pl.BlockSpec((tm, qk + kv), lambda i: (i, 0)), out_shape=SDS((L, qk + kv), MMT),
        compiler_params=_cparams("parallel"))(dqk, dv, cos, sin)


def _band_specs(nb, width):
    blk = lambda f: pl.BlockSpec((None, ATTN_BLOCK, width), f)
    return [blk(lambda h, n: (h, jnp.maximum(n - 1, 0), 0)), blk(lambda h, n: (h, n, 0)),
            blk(lambda h, n: (h, jnp.minimum(n + 1, nb - 1), 0))]


def _window_mask(n, L):
    qi = lax.broadcasted_iota(jnp.int32, (ATTN_BLOCK, 3 * ATTN_BLOCK), 0)
    kk = lax.broadcasted_iota(jnp.int32, (ATTN_BLOCK, 3 * ATTN_BLOCK), 1)
    key_abs = (n - 1) * ATTN_BLOCK + kk
    return (jnp.abs(qi + ATTN_BLOCK - kk) <= ATTN_BLOCK) & (key_abs >= 0) & (key_abs < L)


def _attn_fwd(q, k, v, kc, vc, sink, name):
    nkv, _, L, hd = q.shape
    C = kc.shape[1]
    nb = L // ATTN_BLOCK
    scale = HEAD_DIM ** -0.5

    def body(sink_ref, q_ref, k0, k1, k2, v0, v1, v2, kc_ref, vc_ref, o_ref, lse_ref):
        hh, n = pl.program_id(0), pl.program_id(1)
        kw = jnp.concatenate([k0[...], k1[...], k2[...]], axis=0)
        vw = jnp.concatenate([v0[...], v1[...], v2[...]], axis=0)
        mask = _window_mask(n, L)
        for g in range(Q_PER_KV):
            qg = q_ref[g]
            sw = jnp.where(mask, _dot(qg, kw, 1, 1) * scale, NEG)
            sc = _dot(qg, kc_ref[...], 1, 1) * scale
            sk = sink_ref[hh * Q_PER_KV + g]
            m = jnp.maximum(jnp.maximum(jnp.max(sw, axis=-1, keepdims=True), jnp.max(sc, axis=-1, keepdims=True)), sk)
            pw, pc = jnp.exp(sw - m), jnp.exp(sc - m)
            den = jnp.sum(pw, axis=-1, keepdims=True) + jnp.sum(pc, axis=-1, keepdims=True) + jnp.exp(sk - m)
            inv = 1.0 / den
            o_ref[g] = _dot(pw * inv, vw, 1, 0) + _dot(pc * inv, vc_ref[...], 1, 0)
            lse_ref[g] = m + jnp.log(den)

    qspec = pl.BlockSpec((None, Q_PER_KV, ATTN_BLOCK, hd), lambda h, n: (h, 0, n, 0))
    cspec = pl.BlockSpec((None, C, hd), lambda h, n: (h, 0, 0))
    return pl.pallas_call(
        body, name=name, grid=(nkv, nb),
        in_specs=[pl.BlockSpec(memory_space=pltpu.SMEM), qspec] + _band_specs(nb, hd) + _band_specs(nb, hd) + [cspec, cspec],
        out_specs=[qspec, pl.BlockSpec((None, Q_PER_KV, ATTN_BLOCK, 1), lambda h, n: (h, 0, n, 0))],
        out_shape=[SDS((nkv, Q_PER_KV, L, hd), F32), SDS((nkv, Q_PER_KV, L, 1), F32)],
        compiler_params=_cparams("parallel", "parallel"))(sink, q, k, k, k, v, v, v, kc, vc)


def _attn_bwd_q(q, k, v, kc, vc, sink, o, do, lse, name):
    nkv, _, L, hd = q.shape
    C = kc.shape[1]
    nb = L // ATTN_BLOCK
    scale = HEAD_DIM ** -0.5

    def body(sink_ref, q_ref, k0, k1, k2, v0, v1, v2, kc_ref, vc_ref, o_ref, do_ref, lse_ref, dq_ref, dkc_ref, dvc_ref, dsk_ref):
        hh, n = pl.program_id(0), pl.program_id(1)

        @pl.when(n == 0)
        def _():
            dkc_ref[...] = jnp.zeros_like(dkc_ref)
            dvc_ref[...] = jnp.zeros_like(dvc_ref)
            dsk_ref[...] = jnp.zeros_like(dsk_ref)

        kw = jnp.concatenate([k0[...], k1[...], k2[...]], axis=0)
        vw = jnp.concatenate([v0[...], v1[...], v2[...]], axis=0)
        mask = _window_mask(n, L)
        for g in range(Q_PER_KV):
            qg, dog, lse_g = q_ref[g], do_ref[g], lse_ref[g]
            delta = jnp.sum(dog.astype(F32) * o_ref[g], axis=-1, keepdims=True)
            pw = jnp.exp(jnp.where(mask, _dot(qg, kw, 1, 1) * scale, NEG) - lse_g)
            pc = jnp.exp(_dot(qg, kc_ref[...], 1, 1) * scale - lse_g)
            dsw = pw * (_dot(dog, vw, 1, 1) - delta)
            dsc = pc * (_dot(dog, vc_ref[...], 1, 1) - delta)
            dq_ref[g] = (_dot(dsw, kw, 1, 0) + _dot(dsc, kc_ref[...], 1, 0)) * scale
            dkc_ref[...] += _dot(dsc, qg, 0, 0) * scale
            dvc_ref[...] += _dot(pc, dog, 0, 0)
            psk = jnp.exp(sink_ref[hh * Q_PER_KV + g] - lse_g)
            dsk_ref[pl.ds(g, 1), :] += jnp.broadcast_to(jnp.sum(-psk * delta, axis=0, keepdims=True), (1, LANES))

    qspec = pl.BlockSpec((None, Q_PER_KV, ATTN_BLOCK, hd), lambda h, n: (h, 0, n, 0))
    lspec = pl.BlockSpec((None, Q_PER_KV, ATTN_BLOCK, 1), lambda h, n: (h, 0, n, 0))
    cspec = pl.BlockSpec((None, C, hd), lambda h, n: (h, 0, 0))
    return pl.pallas_call(
        body, name=name, grid=(nkv, nb),
        in_specs=[pl.BlockSpec(memory_space=pltpu.SMEM), qspec] + _band_specs(nb, hd) + _band_specs(nb, hd)
        + [cspec, cspec, qspec, qspec, lspec],
        out_specs=[qspec, cspec, cspec, pl.BlockSpec((None, SUBLANES, LANES), lambda h, n: (h, 0, 0))],
        out_shape=[SDS((nkv, Q_PER_KV, L, hd), F32), SDS((nkv, C, hd), F32), SDS((nkv, C, hd), F32),
                   SDS((nkv, SUBLANES, LANES), F32)],
        compiler_params=_cparams("parallel", "arbitrary"))(sink, q, k, k, k, v, v, v, kc, vc, o, do, lse)


def _attn_bwd_kv(q, k, v, o, do, lse, name):
    nkv, _, L, hd = q.shape
    nb = L // ATTN_BLOCK
    scale = HEAD_DIM ** -0.5

    def body(q0, q1, q2, do0, do1, do2, o0, o1, o2, l0, l1, l2, k_ref, v_ref, dk_ref, dv_ref):
        j = pl.program_id(1)
        qi = lax.broadcasted_iota(jnp.int32, (ATTN_BLOCK, ATTN_BLOCK), 0)
        kk = lax.broadcasted_iota(jnp.int32, (ATTN_BLOCK, ATTN_BLOCK), 1)
        kj, vj = k_ref[...], v_ref[...]
        dk = jnp.zeros((ATTN_BLOCK, hd), F32)
        dv = jnp.zeros((ATTN_BLOCK, hd), F32)
        for slot, (q_r, do_r, o_r, l_r) in enumerate(((q0, do0, o0, l0), (q1, do1, o1, l1), (q2, do2, o2, l2))):
            n = j - 1 + slot
            ok = (n >= 0) & (n < nb) & (jnp.abs(qi + ATTN_BLOCK - ((2 - slot) * ATTN_BLOCK + kk)) <= ATTN_BLOCK)
            for g in range(Q_PER_KV):
                qg, dog = q_r[g], do_r[g]
                delta = jnp.sum(dog.astype(F32) * o_r[g], axis=-1, keepdims=True)
                p = jnp.exp(jnp.where(ok, _dot(qg, kj, 1, 1) * scale - l_r[g], NEG))
                ds = p * (_dot(dog, vj, 1, 1) - delta)
                dk = dk + _dot(ds, qg, 0, 0) * scale
                dv = dv + _dot(p, dog, 0, 0)
        dk_ref[...] = dk
        dv_ref[...] = dv

    def band(width):
        blk = lambda f: pl.BlockSpec((None, Q_PER_KV, ATTN_BLOCK, width), f)
        return [blk(lambda h, j: (h, 0, jnp.maximum(j - 1, 0), 0)), blk(lambda h, j: (h, 0, j, 0)),
                blk(lambda h, j: (h, 0, jnp.minimum(j + 1, nb - 1), 0))]

    kspec = pl.BlockSpec((None, ATTN_BLOCK, hd), lambda h, j: (h, j, 0))
    return pl.pallas_call(
        body, name=name, grid=(nkv, nb), in_specs=band(hd) + band(hd) + band(hd) + band(1) + [kspec, kspec],
        out_specs=[kspec, kspec], out_shape=[SDS((nkv, L, hd), F32), SDS((nkv, L, hd), F32)],
        compiler_params=_cparams("parallel", "parallel"))(q, q, q, do, do, do, o, o, o, lse, lse, lse, k, v)


_GELU_K = math.sqrt(2.0 / math.pi)


def _gelu(v):
    return 0.5 * v * (1.0 + jnp.tanh(_GELU_K * (v + 0.044715 * (v * v * v))))


def _gelu_grad(v):
    t = jnp.tanh(_GELU_K * (v + 0.044715 * (v * v * v)))
    return 0.5 * (1.0 + t) + 0.5 * v * (1.0 - t * t) * (_GELU_K * (1.0 + 3.0 * 0.044715 * (v * v)))


def _gmlp_fwd(p0, b_in, ln_g, ln_b, w_s, b_s, name):
    L, W2 = p0.shape
    W = W2 // 2
    G = W // GMLP_GROUP_DIM

    def body(p_ref, bi_ref, g_ref, b_ref, ws_ref, bs_ref, o_ref):
        ge = _gelu(p_ref[...] + bi_ref[...])
        xh, _ = _layer_norm_stats(ge[:, W:])
        vln = xh * g_ref[...] + b_ref[...]
        for gi in range(G):
            cols = slice(gi * GMLP_GROUP_DIM, (gi + 1) * GMLP_GROUP_DIM)
            s = _dot(ws_ref[gi], vln[:, cols], 1, 0) + bs_ref[gi]
            o_ref[:, cols] = (ge[:, cols] * s).astype(o_ref.dtype)

    full = lambda shape: pl.BlockSpec(shape, lambda i: (0,) * len(shape))
    return pl.pallas_call(
        body, name=name, grid=(L // GMLP_CHUNK,),
        in_specs=[pl.BlockSpec((GMLP_CHUNK, W2), lambda i: (i, 0)), full((1, W2)), full((1, W)), full((1, W)),
                  full((G, GMLP_CHUNK, GMLP_CHUNK)), full((G, GMLP_CHUNK, 1))],
        out_specs=pl.BlockSpec((GMLP_CHUNK, W), lambda i: (i, 0)), out_shape=SDS((L, W), MMT),
        compiler_params=_cparams("parallel"))(p0, b_in, ln_g, ln_b, w_s, b_s)


def _gmlp_bwd(p0, dus, b_in, ln_g, ln_b, w_s, w_st, b_s, name):
    L, W2 = p0.shape
    W = W2 // 2
    G = W // GMLP_GROUP_DIM

    def body(p_ref, d_ref, bi_ref, g_ref, b_ref, ws_ref, wst_ref, bs_ref, dpre_ref, dbi_ref, dg_ref, db_ref, dws_ref, dbs_ref, dvln):
        @pl.when(pl.program_id(0) == 0)
        def _():
            for ref in (dbi_ref, dg_ref, db_ref, dws_ref, dbs_ref):
                ref[...] = jnp.zeros_like(ref)

        pre = p_ref[...] + bi_ref[...]
        ge = _gelu(pre)
        xh, rstd = _layer_norm_stats(ge[:, W:])
        vln = xh * g_ref[...] + b_ref[...]
        dge_u = []
        for gi in range(G):
            cols = slice(gi * GMLP_GROUP_DIM, (gi + 1) * GMLP_GROUP_DIM)
            vg = vln[:, cols]
            s = _dot(ws_ref[gi], vg, 1, 0) + bs_ref[gi]
            dus_g = d_ref[:, cols]
            dge_u.append(dus_g * s)
            ds = dus_g * ge[:, cols]
            dbs_ref[gi] += jnp.sum(ds, axis=1, keepdims=True)
            dws_ref[gi] += _dot(ds, vg, 1, 1)
            dvln[:, cols] = _dot(wst_ref[gi], ds, 1, 0)
        dv = dvln[...]
        dg_ref[...] += _sum0(dv * xh)
        db_ref[...] += _sum0(dv)
        dxh = dv * g_ref[...]
        dv0 = rstd * (dxh - jnp.mean(dxh, axis=-1, keepdims=True) - xh * jnp.mean(dxh * xh, axis=-1, keepdims=True))
        dpre = jnp.concatenate(dge_u + [dv0], axis=1) * _gelu_grad(pre)
        dbi_ref[...] += _sum0(dpre)
        dpre_ref[...] = dpre.astype(dpre_ref.dtype)

    full = lambda shape: pl.BlockSpec(shape, lambda i: (0,) * len(shape))
    mats = (G, GMLP_CHUNK, GMLP_CHUNK)
    return pl.pallas_call(
        body, name=name, grid=(L // GMLP_CHUNK,),
        in_specs=[pl.BlockSpec((GMLP_CHUNK, W2), lambda i: (i, 0)), pl.BlockSpec((GMLP_CHUNK, W), lambda i: (i, 0)),
                  full((1, W2)), full((1, W)), full((1, W)), full(mats), full(mats), full((G, GMLP_CHUNK, 1))],
        out_specs=[pl.BlockSpec((GMLP_CHUNK, W2), lambda i: (i, 0)), full((1, W2)), full((1, W)), full((1, W)), full(mats),
                   full((G, GMLP_CHUNK, 1))],
        out_shape=[SDS((L, W2), MMT), SDS((1, W2), F32), SDS((1, W), F32), SDS((1, W), F32), SDS(mats, F32),
                   SDS((G, GMLP_CHUNK, 1), F32)],
        scratch_shapes=[pltpu.VMEM((GMLP_CHUNK, W), F32)], compiler_params=_cparams("arbitrary"))(
            p0, dus, b_in, ln_g, ln_b, w_s, w_st, b_s)


def _loss_head(h, target, tm, name):
    L, D = h.shape

    def body(h_ref, t_ref, l_ref, d_ref):
        @pl.when(pl.program_id(0) == 0)
        def _():
            l_ref[...] = jnp.zeros_like(l_ref)

        e = h_ref[...] - t_ref[...]
        l_ref[...] += 0.5 * jnp.sum(jnp.mean(e * e, axis=-1, keepdims=True), axis=0, keepdims=True)
        d_ref[...] = e * (1.0 / D)

    row = pl.BlockSpec((tm, D), lambda i: (i, 0))
    return pl.pallas_call(body, name=name, grid=(L // tm,), in_specs=[row, row],
                          out_specs=[pl.BlockSpec((1, 1), lambda i: (0, 0)), row],
                          out_shape=[SDS((1, 1), F32), SDS((L, D), F32)], compiler_params=_cparams("arbitrary"))(h, target)


def _ada_fwd(cond, ada_w, ada_b, name):
    NL, D, n = ada_w.shape
    tn = _tile(n, 768)

    def body(c_ref, w_ref, b_ref, o_ref):
        cv = c_ref[...]
        o_ref[...] = _dot(cv * _sig(cv), w_ref[...], 1, 0) + b_ref[...]

    return pl.pallas_call(
        body, name=name, grid=(NL, n // tn),
        in_specs=[pl.BlockSpec((2 * SUBLANES, D), lambda i, j: (0, 0)), pl.BlockSpec((None, D, tn), lambda i, j: (i, 0, j)),
                  pl.BlockSpec((None, 1, tn), lambda i, j: (i, 0, j))],
        out_specs=pl.BlockSpec((None, 2 * SUBLANES, tn), lambda i, j: (i, 0, j)), out_shape=SDS((NL, 2 * SUBLANES, n), F32),
        compiler_params=_cparams("parallel", "parallel"))(cond, ada_w, ada_b)


def _ada_bwd(cond, ada_w, dm_lat, dm_ctx, name):
    NL, D, n = ada_w.shape
    tn = _tile(n, 768)

    def body(c_ref, w_ref, dl_ref, dc_ref, dw_ref, ds_ref):
        @pl.when((pl.program_id(0) == 0) & (pl.program_id(1) == 0))
        def _():
            ds_ref[...] = jnp.zeros_like(ds_ref)

        cv = c_ref[...]
        row = lax.broadcasted_iota(jnp.int32, (SUBLANES, tn), 0)
        ctx_rows = jnp.where(row == 0, _sum0(dc_ref[...]), 0.0)
        dm = jnp.concatenate([dl_ref[...], ctx_rows], axis=0)
        dw_ref[...] = _dot(cv * _sig(cv), dm, 0, 0)
        ds_ref[...] += _dot(dm, w_ref[...], 1, 1)

    dspec = pl.BlockSpec((None, SUBLANES, tn), lambda i, j: (i, 0, j))
    return pl.pallas_call(
        body, name=name, grid=(NL, n // tn),
        in_specs=[pl.BlockSpec((2 * SUBLANES, D), lambda i, j: (0, 0)), pl.BlockSpec((None, D, tn), lambda i, j: (i, 0, j)),
                  dspec, dspec],
        out_specs=[pl.BlockSpec((None, D, tn), lambda i, j: (i, 0, j)), pl.BlockSpec((2 * SUBLANES, D), lambda i, j: (0, 0))],
        out_shape=[SDS((NL, D, n), F32), SDS((2 * SUBLANES, D), F32)],
        compiler_params=_cparams("arbitrary", "arbitrary"))(cond, ada_w, dm_lat, dm_ctx)


def _adam_math(w, g, m, v):
    m = ADAM_B1 * m + (1.0 - ADAM_B1) * g
    v = ADAM_B2 * v + (1.0 - ADAM_B2) * jnp.square(g)
    m_hat = m / (1.0 - ADAM_B1 ** ADAM_STEP)
    v_hat = v / (1.0 - ADAM_B2 ** ADAM_STEP)
    return -ADAM_LR * (m_hat / (jnp.sqrt(v_hat) + ADAM_EPS) + ADAM_WD * w), m, v


def _row_tile(rows, cols, elems, mult=SUBLANES):
    want = max(mult, elems // cols)
    best = mult if rows % mult == 0 else rows
    for d in range(mult, min(rows, want) + 1, mult):
        if rows % d == 0:
            best = d
    return best


def _adamw(w, m, v, parts, name):
    R, C = w.shape
    tr = _row_tile(R, C, 128 * 1024)
    npart = len(parts)

    def body(*refs):
        w_ref, m_ref, v_ref = refs[:3]
        g_ref, d_ref, nm_ref, nv_ref = refs[3 + npart:]
        g = refs[3][...]
        for p_ref in refs[4:3 + npart]:
            g = g + p_ref[...]
        d, nm, nv = _adam_math(w_ref[...], g, m_ref[...], v_ref[...])
        g_ref[...], d_ref[...], nm_ref[...], nv_ref[...] = g, d, nm, nv

    blk = pl.BlockSpec((tr, C), lambda i: (i, 0))
    return pl.pallas_call(body, name=name, grid=(R // tr,), in_specs=[blk] * (3 + npart), out_specs=[blk] * 4,
                          out_shape=[SDS((R, C), F32)] * 4, compiler_params=_cparams("parallel"))(w, m, v, *parts)


def _adamw_layer(w, m, v, layer, parts, prev, name):
    _, R, C = w.shape
    tr = _row_tile(R, C, 128 * 1024)
    npart = len(parts)
    nprev = 0 if prev is None else 4

    def body(*refs):
        w_ref, m_ref, v_ref = refs[:3]
        g_ref, d_ref, nm_ref, nv_ref = refs[3 + npart + nprev:]
        g = refs[3][...]
        for p_ref in refs[4:3 + npart]:
            g = g + p_ref[...]
        d, nm, nv = _adam_math(w_ref[...], g, m_ref[...], v_ref[...])
        g_ref[...], d_ref[...], nm_ref[...], nv_ref[...] = g, d, nm, nv

    stacked = pl.BlockSpec((None, tr, C), lambda i: (layer, i, 0))
    flat = pl.BlockSpec((tr, C), lambda i: (i, 0))
    return pl.pallas_call(
        body, name=name, grid=(R // tr,),
        in_specs=[stacked] * 3 + [flat] * npart + [pl.BlockSpec(memory_space=pl.ANY)] * nprev, out_specs=[stacked] * 4,
        out_shape=[SDS(w.shape, F32)] * 4, input_output_aliases={3 + npart + k: k for k in range(nprev)},
        compiler_params=_cparams("parallel"))(w, m, v, *parts, *(prev or ()))


def _sum_slots(x, name, after=None):
    S, R, C = x.shape
    tr = _row_tile(R, C, 128 * 1024, SUBLANES * 4 // jnp.dtype(x.dtype).itemsize)
    extra = [] if after is None else [after]

    def body(x_ref, *rest):
        o_ref = rest[-1]
        acc = x_ref[0].astype(F32)
        for s in range(1, S):
            acc = acc + x_ref[s].astype(F32)
        o_ref[...] = acc

    return pl.pallas_call(
        body, name=name, grid=(R // tr,),
        in_specs=[pl.BlockSpec((S, tr, C), lambda i: (0, i, 0))] + [pl.BlockSpec(memory_space=pl.ANY)] * len(extra),
        out_specs=pl.BlockSpec((tr, C), lambda i: (i, 0)), out_shape=SDS((R, C), F32),
        compiler_params=_cparams("parallel"))(x, *extra)


def _my_place():
    return lax.axis_index("x"), lax.axis_index("y"), lax.axis_index("c")


def _other_chips(x, y):
    return [(1 - x, y), (x, 1 - y), (1 - x, 1 - y)]


def _all_gather(v, name):
    R, C = v.shape

    def body(v_ref, o_ref, send_sems, recv_sems, local_sem):
        x, y, c = _my_place()
        me = 4 * x + 2 * y + c
        mine = pltpu.make_async_copy(v_ref, o_ref.at[me], local_sem)
        mine.start()
        copies = []
        for flip in range(1, N_DEV):
            fx, fy, fc = (flip >> 2) & 1, (flip >> 1) & 1, flip & 1
            peer = ((x + fx) % 2, (y + fy) % 2, (c + fc) % 2)
            cp = pltpu.make_async_remote_copy(src_ref=v_ref, dst_ref=o_ref.at[me], send_sem=send_sems.at[flip - 1],
                                              recv_sem=recv_sems.at[flip - 1], device_id=peer, device_id_type=MESH)
            cp.start()
            copies.append(cp)
        for cp in copies:
            cp.wait()
        mine.wait()

    return pl.pallas_call(
        body, name=name, in_specs=[pl.BlockSpec(memory_space=pl.ANY)], out_specs=pl.BlockSpec(memory_space=pl.ANY),
        out_shape=SDS((N_DEV, R, C), v.dtype),
        scratch_shapes=[pltpu.SemaphoreType.DMA((N_DEV - 1,)), pltpu.SemaphoreType.DMA((N_DEV - 1,)), pltpu.SemaphoreType.DMA],
        )(v)


def _shard_window(ref, axis, j, size):
    idx = [slice(None)] * len(ref.shape)
    idx[axis] = pl.ds(pl.multiple_of(j * size, SUBLANES), size)
    return ref.at[tuple(idx)]


def _gather_plan(axis):
    def plan(srcs, lands):
        x, y, c = _my_place()
        shard, whole = srcs[0], lands[0]
        half = shard.shape[0] // 2
        size = shard.shape[axis]

        def window(chip, which):
            if axis == 1:
                return whole.at[pl.ds(pl.multiple_of(which * half, SUBLANES), half), pl.ds(pl.multiple_of(chip * size, LANES), size)]
            return whole.at[pl.ds(pl.multiple_of(chip * size + which * half, SUBLANES), half), :]

        j = 2 * x + y
        local = [(shard, _shard_window(whole, axis, j, size))]
        mine = shard.at[pl.ds(pl.multiple_of(c * half, SUBLANES), half), :]
        remote = [(mine, window(j, c), (px, py, c)) for px, py in _other_chips(x, y)]
        forward = [(window(2 * px + py, c), window(2 * px + py, c), (x, y, 1 - c)) for px, py in _other_chips(x, y)]
        return local, remote, forward
    return plan


def _scatter_plan(axis):
    def plan(srcs, lands):
        x, y, c = _my_place()
        j = 2 * x + y
        size = srcs[0].shape[axis] // N_CHIPS
        local = [(_shard_window(srcs[0], axis, j, size), lands[0].at[j])]
        remote = [(_shard_window(srcs[0], axis, 2 * px + py, size), lands[0].at[j], (px, py, c)) for px, py in _other_chips(x, y)]
        return local, remote, []
    return plan


def _all_gather_plan(srcs, lands):
    x, y, c = _my_place()
    dst = lands[0].at[4 * x + 2 * y + c]
    remote = []
    for flip in range(1, N_DEV):
        fx, fy, fc = (flip >> 2) & 1, (flip >> 1) & 1, flip & 1
        remote.append((srcs[0], dst, ((x + fx) % 2, (y + fy) % 2, (c + fc) % 2)))
    return [(srcs[0], dst)], remote, []


def _same_core_peers():
    x, y, c = _my_place()
    return [(px, py, c) for px, py in _other_chips(x, y)]


def _same_core_peers_and_sibling():
    x, y, c = _my_place()
    return _same_core_peers() + [(x, y, 1 - c)]


def _all_peers():
    x, y, c = _my_place()
    return [((x + (f >> 2 & 1)) % 2, (y + (f >> 1 & 1)) % 2, (c + (f & 1)) % 2) for f in range(1, N_DEV)]


def _sequencer_exchange(name, collective_id, exchanges, peers_fn):
    hbm = pltpu.MemorySpace.HBM
    src_refs = [[jax.new_ref(s, memory_space=hbm) for s in e[0]] for e in exchanges]
    land_refs = [[jax.empty_ref(s, memory_space=hbm) for s in e[1]] for e in exchanges]
    first = [sum(e[3] for e in exchanges[:i]) for i in range(len(exchanges))]
    ncopy = sum(e[3] for e in exchanges)
    dma = pltpu.SemaphoreType.DMA

    @pl.kernel(mesh=plsc.ScalarSubcoreMesh(axis_name="sequencer", num_cores=N_SEQUENCERS), name=name,
               scratch_types=(dma((ncopy,)), dma((ncopy,)), dma((ncopy,)), dma((ncopy,)), dma),
               compiler_params=pltpu.CompilerParams(collective_id=collective_id))
    def launch(send_sems, recv_sems, onward_send_sems, onward_recv_sems, local_sem):
        me = lax.axis_index("sequencer")
        peers = peers_fn()
        barrier = pltpu.get_barrier_semaphore()
        for peer in peers:
            pl.semaphore_signal(barrier, inc=1, device_id=peer, device_id_type=MESH)
        pl.semaphore_wait(barrier, len(peers))
        plans = [e[2](src_refs[i], land_refs[i]) for i, e in enumerate(exchanges)]
        nbytes = lambda ref: math.prod(ref.shape) * jnp.dtype(ref.dtype).itemsize
        jobs = [(nbytes(dst), "local", (i, k)) for i, p in enumerate(plans) for k, (_, dst) in enumerate(p[0])]
        jobs += [(nbytes(src) * (2 if p[2] else 1), "remote", (i, k)) for i, p in enumerate(plans) for k, (src, _, _) in enumerate(p[1])]
        load, owner = [0] * N_SEQUENCERS, {}
        for size, kind, key in sorted(jobs, key=lambda job: -job[0]):
            owner[kind, key] = load.index(min(load))
            load[owner[kind, key]] += size
        for s in range(N_SEQUENCERS):
            @pl.when(me == s)
            def _(s=s):
                for i, (local, _, _) in enumerate(plans):
                    for k, (src, dst) in enumerate(local):
                        if owner["local", (i, k)] == s:
                            cp = pltpu.make_async_copy(src, dst, local_sem)
                            cp.start()
                            cp.wait()
                copies, onward = [], []
                for i, (_, remote, forward) in enumerate(plans):
                    assert len(remote) == exchanges[i][3] and len(forward) in (0, len(remote))
                    for k, (src, dst, peer) in enumerate(remote):
                        if owner["remote", (i, k)] == s:
                            cp = pltpu.make_async_remote_copy(src_ref=src, dst_ref=dst, send_sem=send_sems.at[first[i] + k],
                                                              recv_sem=recv_sems.at[first[i] + k], device_id=peer,
                                                              device_id_type=MESH)
                            cp.start()
                            copies.append(cp)
                            if forward:
                                src, dst, peer = forward[k]
                                onward.append(pltpu.make_async_remote_copy(
                                    src_ref=src, dst_ref=dst, send_sem=onward_send_sems.at[first[i] + k],
                                    recv_sem=onward_recv_sems.at[first[i] + k], device_id=peer, device_id_type=MESH))
                            else:
                                onward.append(None)
                for cp, on in zip(copies, onward):
                    cp.wait_recv()
                    if on is not None:
                        on.start()
                for cp, on in zip(copies, onward):
                    cp.wait_send()
                    if on is not None:
                        on.wait()

    launch()
    return [[r[...] for r in refs] for refs in land_refs]


def _swap_with_sibling(parts, name):
    nt = len(parts)

    def body(*refs):
        ins, outs = refs[:nt], refs[nt:2 * nt]
        send_sems, recv_sems = refs[2 * nt:]
        x, y, c = _my_place()
        copies = []
        for t in range(nt):
            cp = pltpu.make_async_remote_copy(src_ref=ins[t], dst_ref=outs[t], send_sem=send_sems.at[t], recv_sem=recv_sems.at[t],
                                              device_id=(x, y, 1 - c), device_id_type=MESH)
            cp.start()
            copies.append(cp)
        for cp in copies:
            cp.wait()

    any_spec = pl.BlockSpec(memory_space=pl.ANY)
    return pl.pallas_call(
        body, name=name, in_specs=[any_spec] * nt, out_specs=[any_spec] * nt, out_shape=[SDS(p.shape, p.dtype) for p in parts],
        scratch_shapes=[pltpu.SemaphoreType.DMA((nt,)), pltpu.SemaphoreType.DMA((nt,))],
        )(*parts)


PACK_COLS = 1024


def _pack(arrays):
    flat = jnp.concatenate([a.reshape(-1) for a in arrays])
    pad = (-flat.shape[0]) % (2 * SUBLANES * PACK_COLS)
    return jnp.pad(flat, (0, pad)).reshape(-1, PACK_COLS)


def _unpack(packed, shapes):
    flat, out, pos = packed.reshape(-1), [], 0
    for shape in shapes:
        n = math.prod(shape)
        out.append(flat[pos:pos + n].reshape(shape))
        pos += n
    return out


def _unshard_last(stacked):
    moved = jnp.moveaxis(stacked, 0, -2)
    return moved.reshape(moved.shape[:-2] + (moved.shape[-2] * moved.shape[-1],))


def _my_block_last(full, j):
    s = full.shape[-1] // N_CHIPS
    return lax.dynamic_index_in_dim(full.reshape(full.shape[:-1] + (N_CHIPS, s)), j, axis=full.ndim - 1, keepdims=False)


def _rope_tables(L):
    rows = L // GRID_W
    row = jnp.repeat(jnp.arange(rows), GRID_W).astype(F32)
    col = jnp.tile(jnp.arange(GRID_W), rows).astype(F32)
    axis_dim = HEAD_DIM // 2
    inv_freq = ROPE_BASE ** (-jnp.arange(0, axis_dim, 2, dtype=F32) / axis_dim)
    ang_r, ang_c = row[:, None] * inv_freq[None, :], col[:, None] * inv_freq[None, :]
    ang = jnp.concatenate([ang_r, ang_r, ang_c, ang_c] * 2, axis=-1)
    return jnp.cos(ang), jnp.sin(ang)


SMALL_SHARDED = ("norm_g", "ffn_conv_w", "cm_b_in", "cm_dw_w", "cm_dw_b", "cm_ln_g", "cm_ln_b", "cm_b_out", "gm_b_in", "gm_ln_g",
                 "gm_ln_b")
SMALL_REPLICATED = ("c_ctx", "ada_b", "ffn_conv_b", "attn_sink", "gm_w_s", "gm_b_s")
BIG = ("ffn_w_up", "ffn_w_down", "cm_w_in", "cm_w_out", "attn_w_qkv", "attn_w_o", "gm_w_in", "gm_w_out")
BIG_AXIS = {"ffn_w_up": 2, "ffn_w_down": 1, "cm_w_in": 2, "cm_w_out": 1, "attn_w_qkv": 2, "attn_w_o": 1, "gm_w_in": 2, "gm_w_out": 1}
WEIGHTS = ("c_ctx", "ada_w", "ada_b", "norm_g", "ffn_w_up", "ffn_conv_w", "ffn_conv_b", "ffn_w_down", "cm_w_in", "cm_b_in",
           "cm_dw_w", "cm_dw_b", "cm_ln_g", "cm_ln_b", "cm_w_out", "cm_b_out", "attn_w_qkv", "attn_sink", "attn_w_o", "gm_w_in",
           "gm_b_in", "gm_ln_g", "gm_ln_b", "gm_w_s", "gm_b_s", "gm_w_out")


def _step(x, c, ctx, target, W, M, V):
    L, D = x.shape[1], x.shape[2]
    C = ctx.shape[1]
    T = L + C
    NL = W["ada_w"].shape[0]
    tm = 256 if C % 256 == 0 else 128
    nl = L // tm
    xi, yi, ci = _my_place()
    chip = 2 * xi + yi
    dev = 4 * xi + 2 * yi + ci
    segs2, segs1 = [(0, L), (L, C)], [(0, L)]
    vec = lambda a: a.reshape(1, -1)

    layer_sets = [[("cm_w_in", 0), ("cm_w_out", 0), ("ffn_w_up", 0), ("ffn_w_down", 0)],
                  [("attn_w_qkv", 0), ("attn_w_o", 0), ("ffn_w_up", 1), ("ffn_w_down", 1)],
                  [("gm_w_in", 0), ("gm_w_out", 0), ("ffn_w_up", 2), ("ffn_w_down", 2)],
                  [("cm_w_in", 1), ("cm_w_out", 1), ("ffn_w_up", 3), ("ffn_w_down", 3)]]
    arrived = {}

    def fetch(keys, zero, sequencer_id):
        exchanges = []
        for n, i in keys:
            shard = (W[n][i] + zero).astype(MMT)
            whole = list(shard.shape)
            whole[BIG_AXIS[n] - 1] *= N_CHIPS
            exchanges.append(([shard], [SDS(tuple(whole), MMT)], _gather_plan(BIG_AXIS[n] - 1), N_CHIPS - 1))
        lands = _sequencer_exchange(f"fetch_weights_{keys[0][0]}_{keys[0][1]}", sequencer_id, exchanges,
                                    _same_core_peers_and_sibling)
        for key, land in zip(keys, lands):
            arrived[key] = land[0]

    def big(n, i, after=None):
        return arrived[(n, i)]

    small_shapes = [W[n].shape for n in SMALL_SHARDED]
    ag1 = _all_gather(_pack([c.reshape(-1)] + [W[n] for n in SMALL_SHARDED]), "gather_small")
    parts = [_unpack(ag1[2 * s], [(D,)] + small_shapes) for s in range(N_CHIPS)]
    c_rows = jnp.stack([_unpack(ag1[d], [(D,)])[0] for d in range(N_DEV)])
    P = {n: _unshard_last(jnp.stack([parts[s][1 + i] for s in range(N_CHIPS)])) for i, n in enumerate(SMALL_SHARDED)}
    for n in SMALL_REPLICATED:
        P[n] = W[n]

    cond = jnp.concatenate([c_rows, W["c_ctx"][None, :], jnp.zeros((2 * SUBLANES - N_DEV - 1, D), F32)], axis=0)
    ncol = W["ada_w"].shape[2]
    ada_b_mine = lax.dynamic_slice_in_dim(W["ada_b"], chip * ncol, ncol, axis=1)[:, None, :]
    mods_mine = _ada_fwd(cond, W["ada_w"], ada_b_mine, "ada_fwd")
    ag2 = _all_gather(mods_mine.reshape(NL * 2 * SUBLANES, ncol), "gather_mods").reshape(N_DEV, NL, 2 * SUBLANES, ncol)
    mods_all = _unshard_last(jnp.stack([ag2[2 * s] for s in range(N_CHIPS)]))
    mod_lat = lax.dynamic_index_in_dim(mods_all, dev, axis=1, keepdims=False).reshape(NL, 6, D)
    mod_ctx = mods_all[:, N_DEV].reshape(NL, 6, D)
    mod2 = jnp.stack([mod_lat, mod_ctx], axis=1)
    mod1 = mod_lat[:, None]

    corner = mod2[0, 0, 0, 0]
    behind_small = jnp.where(corner != corner, corner, 0.0)
    for k, keys in enumerate([layer_sets[0][:2], layer_sets[0][2:]] + layer_sets[1:]):
        fetch(keys, behind_small, FETCH_IDS[k])
    zero_d = jnp.zeros((1, D), F32)
    cos, sin = _rope_tables(L)
    nkv = D // HEAD_DIM // Q_PER_KV
    qdim, kvdim = D, nkv * HEAD_DIM

    def ffn_fwd(i, h, mod, rows, segs, tag):
        a2 = _prenorm(h, mod, vec(P["norm_g"][i, 2]), 1, rows, nl, tm, f"pre_ffn_{tag}")
        z0 = _mm(a2, big("ffn_w_up", i, a2), "nn", F32, f"ffn_up_{tag}")
        u = _ffn_gate(z0, P["ffn_conv_w"][i], vec(P["ffn_conv_b"][i]), segs, f"ffn_gate_{tag}")
        f = _mm(u, big("ffn_w_down", i, u), "nn", F32, f"ffn_down_{tag}")
        h_out = _postnorm(h, f, zero_d, mod, vec(P["norm_g"][i, 3]), 5, rows, nl, tm, f"post_ffn_{tag}")
        return h_out, dict(h=h, a2=a2, z0=z0, f=f)

    def ffn_bwd(i, dh, sv, mod, rows, segs, tag, G):
        df, dg2, dgn3, _ = _postnorm_bwd(dh, sv["f"], zero_d, mod, vec(P["norm_g"][i, 3]), 5, rows, nl, tm, f"post_ffn_bwd_{tag}")
        du = _mm(df, big("ffn_w_down", i), "nt", F32, f"ffn_down_dx_{tag}")
        u, dz0, dcw, dcb = _ffn_gate_bwd(sv["z0"], du, P["ffn_conv_w"][i], vec(P["ffn_conv_b"][i]), segs, f"ffn_gate_bwd_{tag}")
        G["ffn_w_down"][i] = _mm(u, df, "tn", MMT, f"ffn_down_dw_{tag}")
        G["ffn_w_up"][i] = _mm(sv["a2"], dz0, "tn", MMT, f"ffn_up_dw_{tag}")
        da2 = _mm(dz0, big("ffn_w_up", i), "nt", F32, f"ffn_up_dx_{tag}")
        dh, dsh2, dsc2, dgn2 = _prenorm_bwd(sv["h"], da2, dh, mod, vec(P["norm_g"][i, 2]), 1, rows, nl, tm, f"pre_ffn_bwd_{tag}")
        G["ffn_conv_w"][i], G["ffn_conv_b"][i] = dcw, dcb[0]
        return dh, (dsh2, dsc2, dg2), (dgn2, dgn3)

    def conformer_fwd(i, j, h, mod, rows, segs, tag):
        a = _prenorm(h, mod, vec(P["norm_g"][i, 0]), 0, rows, nl, tm, f"pre_mix_{tag}")
        p0 = _mm(a, big("cm_w_in", j, a), "nn", F32, f"cm_in_{tag}")
        z2 = _glu_conv(p0, vec(P["cm_b_in"][j]), P["cm_dw_w"][j], vec(P["cm_dw_b"][j]), segs, f"cm_conv_{tag}")
        z4 = _ln_silu(z2, vec(P["cm_ln_g"][j]), vec(P["cm_ln_b"][j]), rows, tm, f"cm_ln_{tag}")
        y = _mm(z4, big("cm_w_out", j, z4), "nn", F32, f"cm_out_{tag}")
        h_out = _postnorm(h, y, vec(P["cm_b_out"][j]), mod, vec(P["norm_g"][i, 1]), 2, rows, nl, tm, f"post_mix_{tag}")
        return h_out, dict(h=h, a=a, p0=p0, z2=z2, z4=z4, y=y)

    def conformer_bwd(i, j, dh, sv, mod, rows, segs, tag, G):
        dy, dg1, dgn1, dbo = _postnorm_bwd(dh, sv["y"], vec(P["cm_b_out"][j]) + zero_d, mod, vec(P["norm_g"][i, 1]), 2, rows, nl,
                                           tm, f"post_mix_bwd_{tag}")
        G["cm_w_out"][j] = _mm(sv["z4"], dy, "tn", MMT, f"cm_out_dw_{tag}")
        dz4 = _mm(dy, big("cm_w_out", j), "nt", F32, f"cm_out_dx_{tag}")
        dz2, dlg, dlb = _ln_silu_bwd(sv["z2"], dz4, vec(P["cm_ln_g"][j]), vec(P["cm_ln_b"][j]), rows, tm, f"cm_ln_bwd_{tag}")
        dpa, dpg, ddw, ddb, dba, dbg = _glu_conv_bwd(sv["p0"], vec(P["cm_b_in"][j]), P["cm_dw_w"][j], dz2, segs, f"cm_conv_bwd_{tag}")
        dp = jnp.concatenate([dpa, dpg], axis=1)
        G["cm_w_in"][j] = _mm(sv["a"], dp, "tn", MMT, f"cm_in_dw_{tag}")
        da = _mm(dp, big("cm_w_in", j), "nt", F32, f"cm_in_dx_{tag}")
        dh, dsh1, dsc1, dgn0 = _prenorm_bwd(sv["h"], da, dh, mod, vec(P["norm_g"][i, 0]), 0, rows, nl, tm, f"pre_mix_bwd_{tag}")
        G["cm_b_out"][j] = jnp.sum(dbo, axis=0)[0]
        G["cm_ln_g"][j], G["cm_ln_b"][j], G["cm_dw_w"][j], G["cm_dw_b"][j] = dlg[0], dlb[0], ddw, ddb[0]
        G["cm_b_in"][j] = jnp.concatenate([dba[0], dbg[0]])
        return dh, (dsh1, dsc1, dg1), (dgn0, dgn1)

    def heads(a, n):
        return a.reshape(a.shape[0], n, HEAD_DIM).transpose(1, 0, 2)

    def unheads(a):
        return a.transpose(1, 0, 2).reshape(a.shape[1], -1)

    G = {n: [None] * W[n].shape[0] for n in WEIGHTS if n not in ("c_ctx", "ada_w", "ada_b", "norm_g")}
    saved = []
    h = jnp.concatenate([x[0], ctx[0]], axis=0)
    h, s_mix = conformer_fwd(0, 0, h, mod2[0], T, segs2, "l0")
    h, s_ffn = ffn_fwd(0, h, mod2[0], T, segs2, "l0")
    saved.append((s_mix, s_ffn))
    a_all = _prenorm(h, mod2[1], vec(P["norm_g"][1, 0]), 0, T, nl, tm, "pre_mix_l1")
    qkv = _mm(a_all, big("attn_w_qkv", 0, a_all), "nn", F32, "attn_qkv")
    qk_rot, v_lat = _rope(qkv, cos, sin, L, qdim + kvdim, tm, "rope")
    q_h = heads(qk_rot[:, :qdim], nkv * Q_PER_KV).reshape(nkv, Q_PER_KV, L, HEAD_DIM)
    k_h, v_h = heads(qk_rot[:, qdim:], nkv), heads(v_lat, nkv)
    kc_h = heads(qkv[L:, qdim:qdim + kvdim].astype(MMT), nkv)
    vc_h = heads(qkv[L:, qdim + kvdim:].astype(MMT), nkv)
    sink = P["attn_sink"][0]
    o_h, lse = _attn_fwd(q_h, k_h, v_h, kc_h, vc_h, sink, "attn")
    o_nat = unheads(o_h.reshape(nkv * Q_PER_KV, L, HEAD_DIM)).astype(MMT)
    y1 = _mm(o_nat, big("attn_w_o", 0, o_nat), "nn", F32, "attn_out")
    h_in1 = h
    h = _postnorm(h, y1, zero_d, mod1[1], vec(P["norm_g"][1, 1]), 2, L, nl, tm, "post_mix_l1")
    h, s_ffn1 = ffn_fwd(1, h, mod1[1], L, segs1, "lat")
    h_in2 = h
    a_2 = _prenorm(h, mod1[2], vec(P["norm_g"][2, 0]), 0, L, nl, tm, "pre_mix_l2")
    p0_2 = _mm(a_2, big("gm_w_in", 0, a_2), "nn", F32, "gm_in")
    ws_bf = P["gm_w_s"][0].astype(MMT)
    bs_col = P["gm_b_s"][0][:, :, None]
    us = _gmlp_fwd(p0_2, vec(P["gm_b_in"][0]), vec(P["gm_ln_g"][0]), vec(P["gm_ln_b"][0]), ws_bf, bs_col, "gmlp")
    y2 = _mm(us, big("gm_w_out", 0, us), "nn", F32, "gm_out")
    h = _postnorm(h, y2, zero_d, mod1[2], vec(P["norm_g"][2, 1]), 2, L, nl, tm, "post_mix_l2")
    h, s_ffn2 = ffn_fwd(2, h, mod1[2], L, segs1, "lat")
    h, s_mix3 = conformer_fwd(3, 1, h, mod1[3], L, segs1, "l3")
    h, s_ffn3 = ffn_fwd(3, h, mod1[3], L, segs1, "lat")

    loss_mine, dh = _loss_head(h, target[0], tm, "loss_head")

    dmod = [None] * NL
    dgn = [None] * NL

    def finish(i, mix, ffn, gns_mix, gns_ffn):
        dmod[i] = jnp.concatenate(list(mix) + list(ffn), axis=1)
        dgn[i] = jnp.stack([jnp.sum(g, axis=0)[0] for g in (gns_mix[0], gns_mix[1], gns_ffn[0], gns_ffn[1])])

    sent, so_far = {}, {}

    def send(tag, collective_id, tensors):
        exchanges = []
        for n, l in tensors:
            g = G[n][l]
            shard = list(g.shape)
            shard[BIG_AXIS[n] - 1] //= N_CHIPS
            exchanges.append(([g], [SDS((N_CHIPS,) + tuple(shard), g.dtype)], _scatter_plan(BIG_AXIS[n] - 1), N_CHIPS - 1))
        sent[tag] = (tensors, _sequencer_exchange(f"send_grads_{tag}", collective_id, exchanges, _same_core_peers))
        corner = sum(G[n][l][0:1, 0:1].astype(F32) for n, l in tensors)
        return jnp.where(corner != corner, corner, 0.0)

    def land(tag, after):
        tensors, landed = sent[tag]
        mine = [_sum_slots(lands[0], f"sum_chips_{n}_{l}", after) for (n, l), lands in zip(tensors, landed)]
        theirs = _swap_with_sibling(mine, f"swap_cores_{tag}")
        for (n, l), a, b in zip(tensors, mine, theirs):
            so_far[n] = _adamw_layer(W[n], M[n], V[n], l, [a, b], so_far.get(n), f"adamw_{n}_{l}")

    dh, m_ffn, n_ffn = ffn_bwd(3, dh, s_ffn3, mod1[3], L, segs1, "lat", G)
    dh, m_mix, n_mix = conformer_bwd(3, 1, dh, s_mix3, mod1[3], L, segs1, "l3", G)
    finish(3, m_mix, m_ffn, n_mix, n_ffn)
    zero_d = zero_d + send("l3", SEND_IDS[0], [("ffn_w_up", 3), ("ffn_w_down", 3), ("cm_w_in", 1), ("cm_w_out", 1)])
    land("l3", None)

    dh, m_ffn, n_ffn = ffn_bwd(2, dh, s_ffn2, mod1[2], L, segs1, "lat", G)
    dy2, dg1, dgn1, _ = _postnorm_bwd(dh, y2, zero_d, mod1[2], vec(P["norm_g"][2, 1]), 2, L, nl, tm, "post_mix_bwd_l2")
    G["gm_w_out"][0] = _mm(us, dy2, "tn", MMT, "gm_out_dw")
    dus = _mm(dy2, big("gm_w_out", 0), "nt", F32, "gm_out_dx")
    ws_t = jnp.swapaxes(P["gm_w_s"][0], 1, 2).astype(MMT)
    dpre, dbi, dlg, dlb, dws, dbs = _gmlp_bwd(p0_2, dus, vec(P["gm_b_in"][0]), vec(P["gm_ln_g"][0]), vec(P["gm_ln_b"][0]), ws_bf,
                                              ws_t, bs_col, "gmlp_bwd")
    G["gm_w_in"][0] = _mm(a_2, dpre, "tn", MMT, "gm_in_dw")
    da = _mm(dpre, big("gm_w_in", 0), "nt", F32, "gm_in_dx")
    dh, dsh1, dsc1, dgn0 = _prenorm_bwd(h_in2, da, dh, mod1[2], vec(P["norm_g"][2, 0]), 0, L, nl, tm, "pre_mix_bwd_l2")
    G["gm_b_in"][0], G["gm_ln_g"][0], G["gm_ln_b"][0], G["gm_w_s"][0], G["gm_b_s"][0] = dbi[0], dlg[0], dlb[0], dws, dbs[:, :, 0]
    finish(2, (dsh1, dsc1, dg1), m_ffn, (dgn0, dgn1), n_ffn)
    zero_d = zero_d + send("l2", SEND_IDS[1], [("ffn_w_up", 2), ("ffn_w_down", 2), ("gm_w_in", 0), ("gm_w_out", 0)])
    land("l2", None)

    dh, m_ffn, n_ffn = ffn_bwd(1, dh, s_ffn1, mod1[1], L, segs1, "lat", G)
    dy1, dg1, dgn1, _ = _postnorm_bwd(dh, y1, zero_d, mod1[1], vec(P["norm_g"][1, 1]), 2, L, nl, tm, "post_mix_bwd_l1")
    G["attn_w_o"][0] = _mm(o_nat, dy1, "tn", MMT, "attn_out_dw")
    do_nat = _mm(dy1, big("attn_w_o", 0), "nt", MMT, "attn_out_dx")
    do_h = heads(do_nat, nkv * Q_PER_KV).reshape(nkv, Q_PER_KV, L, HEAD_DIM)
    dq_h, dkc_h, dvc_h, dsk = _attn_bwd_q(q_h, k_h, v_h, kc_h, vc_h, sink, o_h, do_h, lse, "attn_bwd_q")
    dk_h, dv_h = _attn_bwd_kv(q_h, k_h, v_h, o_h, do_h, lse, "attn_bwd_kv")
    dqk = jnp.concatenate([unheads(dq_h.reshape(nkv * Q_PER_KV, L, HEAD_DIM)), unheads(dk_h)], axis=1)
    dqkv_lat = _rope_bwd(dqk, unheads(dv_h), cos, sin, tm, "rope_bwd")
    dqkv_ctx = jnp.concatenate([jnp.zeros((C, qdim), MMT), unheads(dkc_h).astype(MMT), unheads(dvc_h).astype(MMT)], axis=1)
    dqkv = jnp.concatenate([dqkv_lat, dqkv_ctx], axis=0)
    G["attn_w_qkv"][0] = _mm(a_all, dqkv, "tn", MMT, "attn_qkv_dw")
    da_all = _mm(dqkv, big("attn_w_qkv", 0), "nt", F32, "attn_qkv_dx")
    dh_all = jnp.concatenate([dh, jnp.zeros((C, D), F32)], axis=0)
    dh, dsh1, dsc1, dgn0 = _prenorm_bwd(h_in1, da_all, dh_all, mod2[1], vec(P["norm_g"][1, 0]), 0, T, nl, tm, "pre_mix_bwd_l1")
    G["attn_sink"][0] = dsk[:, :Q_PER_KV, 0].reshape(-1)
    pad_ctx = lambda a: jnp.concatenate([a, jnp.zeros_like(a)], axis=0)
    finish(1, (dsh1, dsc1, pad_ctx(dg1)), [pad_ctx(a) for a in m_ffn], (dgn0, dgn1), n_ffn)
    zero_d = zero_d + send("l1", SEND_IDS[2], [("ffn_w_up", 1), ("ffn_w_down", 1), ("attn_w_qkv", 0), ("attn_w_o", 0)])
    land("l1", None)

    s_mix0, s_ffn0 = saved[0]
    dh, m_ffn, n_ffn = ffn_bwd(0, dh, s_ffn0, mod2[0], T, segs2, "l0", G)
    zero_d = zero_d + send("l0_ffn", SEND_IDS[3], [("ffn_w_up", 0), ("ffn_w_down", 0)])
    dh, m_mix, n_mix = conformer_bwd(0, 0, dh, s_mix0, mod2[0], T, segs2, "l0", G)
    finish(0, m_mix, m_ffn, n_mix, n_ffn)
    grad_x = dh[:L][None]
    sent_l0 = send("l0_mix", SEND_IDS[4], [("cm_w_in", 0), ("cm_w_out", 0)])

    for i in range(2, NL):
        dmod[i] = pad_ctx(dmod[i])
    dmod_all = jnp.stack(dmod).reshape(NL, 2, 6 * D) + sent_l0

    ag3 = _all_gather(dmod_all.reshape(NL * 2, 6 * D), "gather_dmods").reshape(N_DEV, NL, 2, N_CHIPS, ncol)
    dm_cols = lax.dynamic_index_in_dim(ag3, chip, axis=3, keepdims=False)
    dm_lat, dm_ctx = jnp.moveaxis(dm_cols[:, :, 0], 0, 1), jnp.moveaxis(dm_cols[:, :, 1], 0, 1)
    g_ada_w, dsilu = _ada_bwd(cond, W["ada_w"], dm_lat, dm_ctx, "ada_bwd")
    cc = W["c_ctx"]
    sg = jax.nn.sigmoid(cc)
    dcctx_part = jnp.where(ci == 0, 1.0, 0.0) * dsilu[N_DEV] * (sg * (1.0 + cc * (1.0 - sg)))

    Gs = {n: jnp.stack(G[n]) for n in G if n not in BIG}
    Gs["norm_g"] = jnp.stack(dgn)
    Gs["ada_b"] = jnp.sum(dmod_all, axis=1)
    Gs["c_ctx"] = dcctx_part
    small_names = list(SMALL_SHARDED) + list(SMALL_REPLICATED)
    small_full_shapes = [P[n].shape for n in small_names]
    small_pack = _pack([Gs[n] for n in small_names]).astype(MMT)
    ((ag4,),) = _sequencer_exchange("gather_small_grads", SMALL_GRADS_ID, [
        ([small_pack], [SDS((N_DEV,) + small_pack.shape, MMT)], _all_gather_plan, N_DEV - 1)], _all_peers)

    flat2 = lambda a: a.reshape(-1, a.shape[-1])
    res = {}
    outs = _adamw(flat2(W["ada_w"]), flat2(M["ada_w"]), flat2(V["ada_w"]), [flat2(g_ada_w)], "adamw_ada_w")
    res["ada_w"] = tuple(o.reshape(W["ada_w"].shape) for o in outs)

    land("l0_ffn", outs[0])
    land("l0_mix", so_far["ffn_w_up"][0])
    for n in BIG:
        res[n] = tuple(so_far[n])

    small_sum = _unpack(_sum_slots(ag4, "sum_small_grads"), small_full_shapes)
    g_small = {}
    for n, g in zip(small_names, small_sum):
        g_small[n] = _my_block_last(g, chip) if n in SMALL_SHARDED else g
    packed = [_pack([d[n] for n in small_names]) for d in (W, M, V)]
    outs_small = _adamw(packed[0], packed[1], packed[2], [_pack([g_small[n] for n in small_names])], "adamw_small")
    shard_shapes = [W[n].shape for n in small_names]
    for k, n in enumerate(small_names):
        res[n] = tuple(_unpack(o, shard_shapes)[k] for o in outs_small)

    loss = lax.psum(loss_mine[0, 0], ("x", "y", "c"))
    return (loss, grad_x) + tuple(res[n][k] for k in range(4) for n in WEIGHTS)


def kernel(x, c, ctx, c_ctx, ada_w, ada_b, norm_g, ffn_w_up, ffn_conv_w, ffn_conv_b, ffn_w_down, cm_w_in, cm_b_in, cm_dw_w, cm_dw_b, cm_ln_g, cm_ln_b, cm_w_out, cm_b_out, attn_w_qkv, attn_sink, attn_w_o, gm_w_in, gm_b_in, gm_ln_g, gm_ln_b, gm_w_s, gm_b_s, gm_w_out, loss_target, m_c_ctx, m_ada_w, m_ada_b, m_norm_g, m_ffn_w_up, m_ffn_conv_w, m_ffn_conv_b, m_ffn_w_down, m_cm_w_in, m_cm_b_in, m_cm_dw_w, m_cm_dw_b, m_cm_ln_g, m_cm_ln_b, m_cm_w_out, m_cm_b_out, m_attn_w_qkv, m_attn_sink, m_attn_w_o, m_gm_w_in, m_gm_b_in, m_gm_ln_g, m_gm_ln_b, m_gm_w_s, m_gm_b_s, m_gm_w_out, v_c_ctx, v_ada_w, v_ada_b, v_norm_g, v_ffn_w_up, v_ffn_conv_w, v_ffn_conv_b, v_ffn_w_down, v_cm_w_in, v_cm_b_in, v_cm_dw_w, v_cm_dw_b, v_cm_ln_g, v_cm_ln_b, v_cm_w_out, v_cm_b_out, v_attn_w_qkv, v_attn_sink, v_attn_w_o, v_gm_w_in, v_gm_b_in, v_gm_ln_g, v_gm_ln_b, v_gm_w_s, v_gm_b_s, v_gm_w_out):
    args = locals()
    W = {n: args[n] for n in WEIGHTS}
    M = {n: args["m_" + n] for n in WEIGHTS}
    V = {n: args["v_" + n] for n in WEIGHTS}
    return _step(x, c, ctx, loss_target, W, M, V)
```

```python
import functools
import math

import jax
import jax.numpy as jnp
from jax import lax
from jax.experimental import pallas as pl
from jax.experimental.pallas import tpu as pltpu
from jax.experimental.pallas import tpu_sc as plsc

F32 = jnp.float32
MMT = jnp.bfloat16
SDS = jax.ShapeDtypeStruct
MESH = pl.DeviceIdType.MESH

EPS = 1e-6
HEAD_DIM = 64
Q_PER_KV = 4
ATTN_BLOCK = 128
GRID_W = 64
ROPE_BASE = 10000.0
GMLP_CHUNK = 128
GMLP_GROUP_DIM = 128
CONV_WIDTH = 31
FFN_CONV_WIDTH = 3
NEG = -1e30

ADAM_LR, ADAM_B1, ADAM_B2, ADAM_EPS, ADAM_WD, ADAM_STEP = 0.001, 0.9, 0.999, 1e-08, 0.01, 10

LANES = 128
SUBLANES = 8
VMEM_LIMIT = 52 * 1024 * 1024
CONV_ROWS = 128
N_CHIPS = 4
N_DEV = 8
N_SEQUENCERS = 2
LOCAL_BYTES_PER_ICI_BYTE = 8
FETCH_IDS = (1, 2, 3, 4, 11)
SEND_IDS = (5, 6, 7, 8, 9)
SMALL_GRADS_ID = 10


def _cparams(*sem):
    return pltpu.CompilerParams(dimension_semantics=sem if sem else None, vmem_limit_bytes=VMEM_LIMIT)


def _tile(n, cap, mult=LANES):
    best = None
    for d in range(mult, min(n, cap) + 1, mult):
        if n % d == 0:
            best = d
    return best if best is not None else n


def _sum0(v):
    return jnp.sum(v, axis=0, keepdims=True)


def _rms(v):
    r = lax.rsqrt(jnp.mean(v * v, axis=-1, keepdims=True) + EPS)
    return v * r, r


def _sig(v):
    return jax.nn.sigmoid(v)


def _dot(a, b, ca, cb):
    return lax.dot_general(a.astype(MMT), b.astype(MMT), (((ca,), (cb,)), ((), ())), preferred_element_type=F32)


def _mm(a, b, mode, out_dtype, name):
    if mode == "nn":
        (M, K), N = a.shape, b.shape[1]
    elif mode == "nt":
        (M, K), N = a.shape, b.shape[0]
    else:
        (K, M), N = a.shape, b.shape[1]
    tm, tn, tk = _tile(M, 512), _tile(N, 1408), _tile(K, 1536)
    nk = K // tk
    ca, cb = {"nn": (1, 0), "nt": (1, 1), "tn": (0, 0)}[mode]

    def body(a_ref, b_ref, o_ref, acc):
        k = pl.program_id(2)

        @pl.when(k == 0)
        def _():
            acc[...] = jnp.zeros_like(acc)

        acc[...] += _dot(a_ref[...], b_ref[...], ca, cb)

        @pl.when(k == nk - 1)
        def _():
            o_ref[...] = acc[...].astype(o_ref.dtype)

    a_spec = pl.BlockSpec((tk, tm), lambda i, j, k: (k, i)) if mode == "tn" else pl.BlockSpec((tm, tk), lambda i, j, k: (i, k))
    b_spec = pl.BlockSpec((tn, tk), lambda i, j, k: (j, k)) if mode == "nt" else pl.BlockSpec((tk, tn), lambda i, j, k: (k, j))
    return pl.pallas_call(
        body, name=name, grid=(M // tm, N // tn, nk), in_specs=[a_spec, b_spec],
        out_specs=pl.BlockSpec((tm, tn), lambda i, j, k: (i, j)), out_shape=SDS((M, N), out_dtype),
        scratch_shapes=[pltpu.VMEM((tm, tn), F32)], compiler_params=_cparams("parallel", "parallel", "arbitrary"))(a, b)


def _seg_of(nl, nseg):
    return (lambda i: jnp.where(i >= nl, 1, 0)) if nseg == 2 else (lambda i: 0)


def _prenorm(h, mod, gn, which, rows, nl, tm, name):
    D = h.shape[1]
    nseg = mod.shape[0]
    seg = _seg_of(nl, nseg)
    sh_i, sc_i = (0, 1) if which == 0 else (3, 4)

    def body(h_ref, mod_ref, gn_ref, a_ref):
        n, _ = _rms(h_ref[...])
        a_ref[...] = (n * gn_ref[...] * (1.0 + mod_ref[pl.ds(sc_i, 1), :]) + mod_ref[pl.ds(sh_i, 1), :]).astype(a_ref.dtype)

    return pl.pallas_call(
        body, name=name, grid=(rows // tm,),
        in_specs=[pl.BlockSpec((tm, D), lambda i: (i, 0)), pl.BlockSpec((None, 6, D), lambda i: (seg(i), 0, 0)),
                  pl.BlockSpec((1, D), lambda i: (0, 0))],
        out_specs=pl.BlockSpec((tm, D), lambda i: (i, 0)), out_shape=SDS((rows, D), MMT),
        compiler_params=_cparams("parallel"))(h, mod, gn)


def _acc_spec(D, seg):
    return pl.BlockSpec((None, 1, D), lambda i: (seg(i), 0, 0))


def _prenorm_bwd(h, da, dh_in, mod, gn, which, rows, nl, tm, name):
    D = h.shape[1]
    nseg = mod.shape[0]
    seg = _seg_of(nl, nseg)
    sc_i = 1 if which == 0 else 4

    def body(h_ref, da_ref, dhin_ref, mod_ref, gn_ref, dh_ref, dsh_ref, dsc_ref, dgn_ref):
        i = pl.program_id(0)
        first = (i == 0) | (i == nl) if nseg == 2 else (i == 0)

        @pl.when(first)
        def _():
            dsh_ref[...] = jnp.zeros_like(dsh_ref)
            dsc_ref[...] = jnp.zeros_like(dsc_ref)
            dgn_ref[...] = jnp.zeros_like(dgn_ref)

        n, r = _rms(h_ref[...])
        da_v = da_ref[...].astype(F32)
        gn_v = gn_ref[...]
        sc1 = 1.0 + mod_ref[pl.ds(sc_i, 1), :]
        dsh_ref[...] += _sum0(da_v)
        dsc_ref[...] += _sum0(da_v * (n * gn_v))
        dgn_ref[...] += _sum0(da_v * n * sc1)
        dn = da_v * (gn_v * sc1)
        dh_ref[...] = dhin_ref[...] + r * (dn - n * jnp.mean(dn * n, axis=-1, keepdims=True))

    row = pl.BlockSpec((tm, D), lambda i: (i, 0))
    acc = SDS((nseg, 1, D), F32)
    return pl.pallas_call(
        body, name=name, grid=(rows // tm,),
        in_specs=[row, row, row, pl.BlockSpec((None, 6, D), lambda i: (seg(i), 0, 0)), pl.BlockSpec((1, D), lambda i: (0, 0))],
        out_specs=[row, _acc_spec(D, seg), _acc_spec(D, seg), _acc_spec(D, seg)],
        out_shape=[SDS((rows, D), F32), acc, acc, acc], compiler_params=_cparams("arbitrary"))(h, da, dh_in, mod, gn)


def _postnorm(h, y, bias, mod, gn, gate_i, rows, nl, tm, name):
    D = h.shape[1]
    nseg = mod.shape[0]
    seg = _seg_of(nl, nseg)

    def body(h_ref, y_ref, b_ref, mod_ref, gn_ref, o_ref):
        ny, _ = _rms(y_ref[...] + b_ref[...])
        o_ref[...] = h_ref[...] + mod_ref[pl.ds(gate_i, 1), :] * (ny * gn_ref[...])

    row = pl.BlockSpec((tm, D), lambda i: (i, 0))
    vec = pl.BlockSpec((1, D), lambda i: (0, 0))
    return pl.pallas_call(
        body, name=name, grid=(rows // tm,),
        in_specs=[row, row, vec, pl.BlockSpec((None, 6, D), lambda i: (seg(i), 0, 0)), vec],
        out_specs=row, out_shape=SDS((rows, D), F32), compiler_params=_cparams("parallel"))(h, y, bias, mod, gn)


def _postnorm_bwd(dh, y, bias, mod, gn, gate_i, rows, nl, tm, name):
    D = y.shape[1]
    nseg = mod.shape[0]
    seg = _seg_of(nl, nseg)

    def body(dh_ref, y_ref, b_ref, mod_ref, gn_ref, dy_ref, dg_ref, dgn_ref, db_ref):
        i = pl.program_id(0)
        first = (i == 0) | (i == nl) if nseg == 2 else (i == 0)

        @pl.when(first)
        def _():
            dg_ref[...] = jnp.zeros_like(dg_ref)
            dgn_ref[...] = jnp.zeros_like(dgn_ref)
            db_ref[...] = jnp.zeros_like(db_ref)

        ny, ry = _rms(y_ref[...] + b_ref[...])
        g = mod_ref[pl.ds(gate_i, 1), :]
        gn_v = gn_ref[...]
        dh_v = dh_ref[...]
        dg_ref[...] += _sum0(dh_v * (ny * gn_v))
        dgn_ref[...] += _sum0(dh_v * ny * g)
        dny = dh_v * (g * gn_v)
        dy = ry * (dny - ny * jnp.mean(dny * ny, axis=-1, keepdims=True))
        db_ref[...] += _sum0(dy)
        dy_ref[...] = dy.astype(dy_ref.dtype)

    row = pl.BlockSpec((tm, D), lambda i: (i, 0))
    vec = pl.BlockSpec((1, D), lambda i: (0, 0))
    acc = SDS((nseg, 1, D), F32)
    return pl.pallas_call(
        body, name=name, grid=(rows // tm,),
        in_specs=[row, row, vec, pl.BlockSpec((None, 6, D), lambda i: (seg(i), 0, 0)), vec],
        out_specs=[row, _acc_spec(D, seg), _acc_spec(D, seg), _acc_spec(D, seg)],
        out_shape=[SDS((rows, D), MMT), acc, acc, acc], compiler_params=_cparams("arbitrary"))(dh, y, bias, mod, gn)


def _seg_layout(segs, H):
    out, base = [], H
    for s0, n in segs:
        out.append((s0, n, base))
        base += n + H
    return out, base


def _zero_pads(ref, lay, H):
    width = ref.shape[1]
    ref[pl.ds(0, H), :] = jnp.zeros((H, width), ref.dtype)
    for _, n, base in lay:
        ref[pl.ds(base + n, H), :] = jnp.zeros((H, width), ref.dtype)


def _window(ref, base, off, H):
    return ref[pl.ds(base - H + off, CONV_ROWS + 2 * H), :]


def _taps(win, H, offs):
    W = CONV_ROWS + 2 * H
    rolled, out = {}, {}
    for o in offs:
        s = H + o
        b = s % SUBLANES
        if b not in rolled:
            rolled[b] = win if b == 0 else pltpu.roll(win, shift=W - b, axis=0)
        out[o] = rolled[b][s - b:s - b + CONV_ROWS, :]
    return out


def _chunks(lay, fn):
    for s0, n, base in lay:
        def step(r, carry, s0=s0, base=base):
            fn(s0, base, pl.multiple_of(r * CONV_ROWS, CONV_ROWS))
            return carry
        lax.fori_loop(0, n // CONV_ROWS, step, 0)


def _ffn_gate(z0, conv_w, conv_b, segs, name):
    T, F2 = z0.shape
    F = F2 // 2
    tc = _tile(F, 256)
    nF = F // tc
    H = SUBLANES
    lay, srows = _seg_layout(segs, H)
    offs = [-1, 0, 1]

    def body(zg_ref, zv_ref, wg_ref, wv_ref, bg_ref, bv_ref, u_ref, xg, xv):
        _zero_pads(xg, lay, H)
        _zero_pads(xv, lay, H)
        for s0, n, base in lay:
            xg[pl.ds(base, n), :] = zg_ref[pl.ds(s0, n), :]
            xv[pl.ds(base, n), :] = zv_ref[pl.ds(s0, n), :]

        def chunk(s0, base, off):
            tg = _taps(_window(xg, base, off, H), H, offs)
            tv = _taps(_window(xv, base, off, H), H, offs)
            zg = bg_ref[...] + sum(tg[k - 1] * wg_ref[pl.ds(k, 1), :] for k in range(3))
            zv = bv_ref[...] + sum(tv[k - 1] * wv_ref[pl.ds(k, 1), :] for k in range(3))
            u_ref[pl.ds(s0 + off, CONV_ROWS), :] = (zg * _sig(zg) * zv).astype(u_ref.dtype)

        _chunks(lay, chunk)

    colg = lambda r: pl.BlockSpec((r, tc), lambda j: (0, j))
    colv = lambda r: pl.BlockSpec((r, tc), lambda j: (0, j + nF))
    return pl.pallas_call(
        body, name=name, grid=(nF,),
        in_specs=[colg(T), colv(T), colg(3), colv(3), colg(1), colv(1)],
        out_specs=colg(T), out_shape=SDS((T, F), MMT),
        scratch_shapes=[pltpu.VMEM((srows, tc), F32), pltpu.VMEM((srows, tc), F32)],
        compiler_params=_cparams("parallel"))(z0, z0, conv_w, conv_w, conv_b, conv_b)


def _ffn_gate_bwd(z0, du, conv_w, conv_b, segs, name):
    T, F2 = z0.shape
    F = F2 // 2
    tc = _tile(F, 256)
    nF = F // tc
    H = SUBLANES
    lay, srows = _seg_layout(segs, H)
    offs = [-1, 0, 1]

    def body(zo_ref, zt_ref, du_ref, wo_ref, wt_ref, bo_ref, bt_ref, u_ref, dz0_ref, dw_ref, db_ref, xo, xt, dzp):
        own_is_gate = pl.program_id(1) == 0
        for ref in (xo, xt, dzp):
            _zero_pads(ref, lay, H)
        for s0, n, base in lay:
            xo[pl.ds(base, n), :] = zo_ref[pl.ds(s0, n), :]
            xt[pl.ds(base, n), :] = zt_ref[pl.ds(s0, n), :]

        def grads(s0, base, off):
            to = _taps(_window(xo, base, off, H), H, offs)
            tt = _taps(_window(xt, base, off, H), H, offs)
            zo = bo_ref[...] + sum(to[k - 1] * wo_ref[pl.ds(k, 1), :] for k in range(3))
            zt = bt_ref[...] + sum(tt[k - 1] * wt_ref[pl.ds(k, 1), :] for k in range(3))
            so, st = _sig(zo), _sig(zt)
            du_v = du_ref[pl.ds(s0 + off, CONV_ROWS), :]
            d_gate = du_v * zt * (so * (1.0 + zo * (1.0 - so)))
            d_val = du_v * (zt * st)
            dzp[pl.ds(base + off, CONV_ROWS), :] = jnp.where(own_is_gate, d_gate, d_val)

            @pl.when(own_is_gate)
            def _():
                u_ref[pl.ds(s0 + off, CONV_ROWS), :] = (zo * so * zt).astype(u_ref.dtype)

        _chunks(lay, grads)
        dw_ref[...] = jnp.zeros_like(dw_ref)
        db_ref[...] = jnp.zeros_like(db_ref)

        def back(s0, base, off):
            td = _taps(_window(dzp, base, off, H), H, offs)
            tx = _taps(_window(xo, base, off, H), H, offs)
            dz0 = sum(td[1 - k] * wo_ref[pl.ds(k, 1), :] for k in range(3))
            dz0_ref[pl.ds(s0 + off, CONV_ROWS), :] = dz0.astype(dz0_ref.dtype)
            db_ref[...] += _sum0(td[0])
            for k in range(3):
                dw_ref[pl.ds(k, 1), :] += _sum0(td[0] * tx[k - 1])

        _chunks(lay, back)

    own = lambda r: pl.BlockSpec((r, tc), lambda j, hf: (0, hf * nF + j))
    oth = lambda r: pl.BlockSpec((r, tc), lambda j, hf: (0, (1 - hf) * nF + j))
    ucol = pl.BlockSpec((T, tc), lambda j, hf: (0, j))
    return pl.pallas_call(
        body, name=name, grid=(nF, 2),
        in_specs=[own(T), oth(T), ucol, own(3), oth(3), own(1), oth(1)],
        out_specs=[ucol, own(T), own(3), own(1)],
        out_shape=[SDS((T, F), MMT), SDS((T, F2), MMT), SDS((3, F2), F32), SDS((1, F2), F32)],
        scratch_shapes=[pltpu.VMEM((srows, tc), F32)] * 3,
        compiler_params=_cparams("parallel", "arbitrary"))(z0, z0, du, conv_w, conv_w, conv_b, conv_b)


def _glu_conv(p0, b_in, dw_w, dw_b, segs, name):
    T, D2 = p0.shape
    D = D2 // 2
    tc = _tile(D, 256)
    nD = D // tc
    H = 2 * SUBLANES
    half = (CONV_WIDTH - 1) // 2
    lay, srows = _seg_layout(segs, H)
    offs = list(range(-half, half + 1))

    def body(pa_ref, pg_ref, ba_ref, bg_ref, w_ref, b_ref, z2_ref, z1p):
        _zero_pads(z1p, lay, H)

        def glu(s0, base, off):
            rows = pl.ds(s0 + off, CONV_ROWS)
            z1p[pl.ds(base + off, CONV_ROWS), :] = (pa_ref[rows, :] + ba_ref[...]) * _sig(pg_ref[rows, :] + bg_ref[...])

        _chunks(lay, glu)

        def conv(s0, base, off):
            t = _taps(_window(z1p, base, off, H), H, offs)
            acc = b_ref[...] + t[-half] * w_ref[pl.ds(0, 1), :]
            for k in range(1, CONV_WIDTH):
                acc = acc + t[k - half] * w_ref[pl.ds(k, 1), :]
            z2_ref[pl.ds(s0 + off, CONV_ROWS), :] = acc

        _chunks(lay, conv)

    cola = lambda r: pl.BlockSpec((r, tc), lambda j: (0, j))
    colg = lambda r: pl.BlockSpec((r, tc), lambda j: (0, j + nD))
    return pl.pallas_call(
        body, name=name, grid=(nD,),
        in_specs=[cola(T), colg(T), cola(1), colg(1), cola(CONV_WIDTH), cola(1)],
        out_specs=cola(T), out_shape=SDS((T, D), F32), scratch_shapes=[pltpu.VMEM((srows, tc), F32)],
        compiler_params=_cparams("parallel"))(p0, p0, b_in, b_in, dw_w, dw_b)


def _glu_conv_bwd(p0, b_in, dw_w, dz2, segs, name):
    T, D2 = p0.shape
    D = D2 // 2
    tc = _tile(D, 256)
    nD = D // tc
    H = 2 * SUBLANES
    half = (CONV_WIDTH - 1) // 2
    lay, srows = _seg_layout(segs, H)
    offs = list(range(-half, half + 1))

    def body(pa_ref, pg_ref, ba_ref, bg_ref, w_ref, dz2_ref, dpa_ref, dpg_ref, dw_ref, db_ref, dba_ref, dbg_ref, z1p, dzp):
        _zero_pads(z1p, lay, H)
        _zero_pads(dzp, lay, H)
        for s0, n, base in lay:
            dzp[pl.ds(base, n), :] = dz2_ref[pl.ds(s0, n), :]

        def glu(s0, base, off):
            rows = pl.ds(s0 + off, CONV_ROWS)
            z1p[pl.ds(base + off, CONV_ROWS), :] = (pa_ref[rows, :] + ba_ref[...]) * _sig(pg_ref[rows, :] + bg_ref[...])

        _chunks(lay, glu)
        for ref in (dw_ref, db_ref, dba_ref, dbg_ref):
            ref[...] = jnp.zeros_like(ref)

        def back(s0, base, off):
            td = _taps(_window(dzp, base, off, H), H, offs)
            tz = _taps(_window(z1p, base, off, H), H, offs)
            dz1 = td[half] * w_ref[pl.ds(0, 1), :]
            for k in range(1, CONV_WIDTH):
                dz1 = dz1 + td[half - k] * w_ref[pl.ds(k, 1), :]
            db_ref[...] += _sum0(td[0])
            for k in range(CONV_WIDTH):
                dw_ref[pl.ds(k, 1), :] += _sum0(td[0] * tz[k - half])
            rows = pl.ds(s0 + off, CONV_ROWS)
            pa = pa_ref[rows, :] + ba_ref[...]
            sg = _sig(pg_ref[rows, :] + bg_ref[...])
            dpa = dz1 * sg
            dpg = dz1 * pa * (sg * (1.0 - sg))
            dba_ref[...] += _sum0(dpa)
            dbg_ref[...] += _sum0(dpg)
            dpa_ref[rows, :] = dpa.astype(dpa_ref.dtype)
            dpg_ref[rows, :] = dpg.astype(dpg_ref.dtype)

        _chunks(lay, back)

    cola = lambda r: pl.BlockSpec((r, tc), lambda j: (0, j))
    colg = lambda r: pl.BlockSpec((r, tc), lambda j: (0, j + nD))
    return pl.pallas_call(
        body, name=name, grid=(nD,),
        in_specs=[cola(T), colg(T), cola(1), colg(1), cola(CONV_WIDTH), cola(T)],
        out_specs=[cola(T), cola(T), cola(CONV_WIDTH), cola(1), cola(1), cola(1)],
        out_shape=[SDS((T, D), MMT), SDS((T, D), MMT), SDS((CONV_WIDTH, D), F32), SDS((1, D), F32), SDS((1, D), F32),
                   SDS((1, D), F32)],
        scratch_shapes=[pltpu.VMEM((srows, tc), F32)] * 2, compiler_params=_cparams("parallel"))(p0, p0, b_in, b_in, dw_w, dz2)


def _layer_norm_stats(v):
    mu = jnp.mean(v, axis=-1, keepdims=True)
    var = jnp.mean(jnp.square(v - mu), axis=-1, keepdims=True)
    rstd = lax.rsqrt(var + EPS)
    return (v - mu) * rstd, rstd


def _ln_silu(z2, ln_g, ln_b, rows, tm, name):
    D = z2.shape[1]

    def body(z_ref, g_ref, b_ref, o_ref):
        xh, _ = _layer_norm_stats(z_ref[...])
        z3 = xh * g_ref[...] + b_ref[...]
        o_ref[...] = (z3 * _sig(z3)).astype(o_ref.dtype)

    row = pl.BlockSpec((tm, D), lambda i: (i, 0))
    vec = pl.BlockSpec((1, D), lambda i: (0, 0))
    return pl.pallas_call(body, name=name, grid=(rows // tm,), in_specs=[row, vec, vec], out_specs=row,
                          out_shape=SDS((rows, D), MMT), compiler_params=_cparams("parallel"))(z2, ln_g, ln_b)


def _ln_silu_bwd(z2, dz4, ln_g, ln_b, rows, tm, name):
    D = z2.shape[1]

    def body(z_ref, d_ref, g_ref, b_ref, dz_ref, dg_ref, db_ref):
        @pl.when(pl.program_id(0) == 0)
        def _():
            dg_ref[...] = jnp.zeros_like(dg_ref)
            db_ref[...] = jnp.zeros_like(db_ref)

        xh, rstd = _layer_norm_stats(z_ref[...])
        z3 = xh * g_ref[...] + b_ref[...]
        s = _sig(z3)
        dz3 = d_ref[...] * (s * (1.0 + z3 * (1.0 - s)))
        dg_ref[...] += _sum0(dz3 * xh)
        db_ref[...] += _sum0(dz3)
        dxh = dz3 * g_ref[...]
        dz_ref[...] = rstd * (dxh - jnp.mean(dxh, axis=-1, keepdims=True) - xh * jnp.mean(dxh * xh, axis=-1, keepdims=True))

    row = pl.BlockSpec((tm, D), lambda i: (i, 0))
    vec = pl.BlockSpec((1, D), lambda i: (0, 0))
    return pl.pallas_call(body, name=name, grid=(rows // tm,), in_specs=[row, row, vec, vec], out_specs=[row, vec, vec],
                          out_shape=[SDS((rows, D), F32), SDS((1, D), F32), SDS((1, D), F32)],
                          compiler_params=_cparams("arbitrary"))(z2, dz4, ln_g, ln_b)


def _rot_half_pairs(v):
    width = v.shape[1]
    lane = lax.broadcasted_iota(jnp.int32, v.shape, 1)
    return jnp.where((lane % 32) < 16, -pltpu.roll(v, shift=width - 16, axis=1), pltpu.roll(v, shift=16, axis=1))


def _rope(qkv, cos, sin, L, qk, tm, name):
    width = qkv.shape[1]
    kv = width - qk

    def body(x_ref, c_ref, s_ref, qk_ref, v_ref):
        xv = x_ref[:, pl.ds(0, qk)]
        c = jnp.tile(c_ref[...], (1, qk // LANES))
        s = jnp.tile(s_ref[...], (1, qk // LANES))
        qk_ref[...] = (xv * c + _rot_half_pairs(xv) * s).astype(qk_ref.dtype)
        v_ref[...] = x_ref[:, pl.ds(qk, kv)].astype(v_ref.dtype)

    tab = pl.BlockSpec((tm, LANES), lambda i: (i, 0))
    return pl.pallas_call(
        body, name=name, grid=(L // tm,), in_specs=[pl.BlockSpec((tm, width), lambda i: (i, 0)), tab, tab],
        out_specs=[pl.BlockSpec((tm, qk), lambda i: (i, 0)), pl.BlockSpec((tm, kv), lambda i: (i, 0))],
        out_shape=[SDS((L, qk), MMT), SDS((L, kv), MMT)], compiler_params=_cparams("parallel"))(qkv, cos, sin)


def _rope_bwd(dqk, dv, cos, sin, tm, name):
    L, qk = dqk.shape
    kv = dv.shape[1]

    def body(d_ref, dv_ref, c_ref, s_ref, o_ref):
        dv_ = d_ref[...]
        c = jnp.tile(c_ref[...], (1, qk // LANES))
        s = jnp.tile(s_ref[...], (1, qk // LANES))
        o_ref[:, pl.ds(0, qk)] = (dv_ * c - _rot_half_pairs(dv_ * s)).astype(o_ref.dtype)
        o_ref[:, pl.ds(qk, kv)] = dv_ref[...].astype(o_ref.dtype)

    tab = pl.BlockSpec((tm, LANES), lambda i: (i, 0))
    return pl.pallas_call(
        body, name=name, grid=(L // tm,),
        in_specs=[pl.BlockSpec((tm, qk), lambda i: (i, 0)), pl.BlockSpec((tm, kv), lambda i: (i, 0)), tab, tab],
        out_specs=pl.BlockSpec((tm, qk + kv), lambda i: (i, 0)), out_shape=SDS((L, qk + kv), MMT),
        compiler_params=_cparams("parallel"))(dqk, dv, cos, sin)


def _band_specs(nb, width):
    blk = lambda f: pl.BlockSpec((None, ATTN_BLOCK, width), f)
    return [blk(lambda h, n: (h, jnp.maximum(n - 1, 0), 0)), blk(lambda h, n: (h, n, 0)),
            blk(lambda h, n: (h, jnp.minimum(n + 1, nb - 1), 0))]


def _window_mask(n, L):
    qi = lax.broadcasted_iota(jnp.int32, (ATTN_BLOCK, 3 * ATTN_BLOCK), 0)
    kk = lax.broadcasted_iota(jnp.int32, (ATTN_BLOCK, 3 * ATTN_BLOCK), 1)
    key_abs = (n - 1) * ATTN_BLOCK + kk
    return (jnp.abs(qi + ATTN_BLOCK - kk) <= ATTN_BLOCK) & (key_abs >= 0) & (key_abs < L)


def _attn_fwd(q, k, v, kc, vc, sink, name):
    nkv, _, L, hd = q.shape
    C = kc.shape[1]
    nb = L // ATTN_BLOCK
    scale = HEAD_DIM ** -0.5

    def body(sink_ref, q_ref, k0, k1, k2, v0, v1, v2, kc_ref, vc_ref, o_ref, lse_ref):
        hh, n = pl.program_id(0), pl.program_id(1)
        kw = jnp.concatenate([k0[...], k1[...], k2[...]], axis=0)
        vw = jnp.concatenate([v0[...], v1[...], v2[...]], axis=0)
        mask = _window_mask(n, L)
        for g in range(Q_PER_KV):
            qg = q_ref[g]
            sw = jnp.where(mask, _dot(qg, kw, 1, 1) * scale, NEG)
            sc = _dot(qg, kc_ref[...], 1, 1) * scale
            sk = sink_ref[hh * Q_PER_KV + g]
            m = jnp.maximum(jnp.maximum(jnp.max(sw, axis=-1, keepdims=True), jnp.max(sc, axis=-1, keepdims=True)), sk)
            pw, pc = jnp.exp(sw - m), jnp.exp(sc - m)
            den = jnp.sum(pw, axis=-1, keepdims=True) + jnp.sum(pc, axis=-1, keepdims=True) + jnp.exp(sk - m)
            inv = 1.0 / den
            o_ref[g] = _dot(pw * inv, vw, 1, 0) + _dot(pc * inv, vc_ref[...], 1, 0)
            lse_ref[g] = m + jnp.log(den)

    qspec = pl.BlockSpec((None, Q_PER_KV, ATTN_BLOCK, hd), lambda h, n: (h, 0, n, 0))
    cspec = pl.BlockSpec((None, C, hd), lambda h, n: (h, 0, 0))
    return pl.pallas_call(
        body, name=name, grid=(nkv, nb),
        in_specs=[pl.BlockSpec(memory_space=pltpu.SMEM), qspec] + _band_specs(nb, hd) + _band_specs(nb, hd) + [cspec, cspec],
        out_specs=[qspec, pl.BlockSpec((None, Q_PER_KV, ATTN_BLOCK, 1), lambda h, n: (h, 0, n, 0))],
        out_shape=[SDS((nkv, Q_PER_KV, L, hd), F32), SDS((nkv, Q_PER_KV, L, 1), F32)],
        compiler_params=_cparams("parallel", "parallel"))(sink, q, k, k, k, v, v, v, kc, vc)


def _attn_bwd_q(q, k, v, kc, vc, sink, o, do, lse, name):
    nkv, _, L, hd = q.shape
    C = kc.shape[1]
    nb = L // ATTN_BLOCK
    scale = HEAD_DIM ** -0.5

    def body(sink_ref, q_ref, k0, k1, k2, v0, v1, v2, kc_ref, vc_ref, o_ref, do_ref, lse_ref, dq_ref, dkc_ref, dvc_ref, dsk_ref):
        hh, n = pl.program_id(0), pl.program_id(1)

        @pl.when(n == 0)
        def _():
            dkc_ref[...] = jnp.zeros_like(dkc_ref)
            dvc_ref[...] = jnp.zeros_like(dvc_ref)
            dsk_ref[...] = jnp.zeros_like(dsk_ref)

        kw = jnp.concatenate([k0[...], k1[...], k2[...]], axis=0)
        vw = jnp.concatenate([v0[...], v1[...], v2[...]], axis=0)
        mask = _window_mask(n, L)
        for g in range(Q_PER_KV):
            qg, dog, lse_g = q_ref[g], do_ref[g], lse_ref[g]
            delta = jnp.sum(dog.astype(F32) * o_ref[g], axis=-1, keepdims=True)
            pw = jnp.exp(jnp.where(mask, _dot(qg, kw, 1, 1) * scale, NEG) - lse_g)
            pc = jnp.exp(_dot(qg, kc_ref[...], 1, 1) * scale - lse_g)
            dsw = pw * (_dot(dog, vw, 1, 1) - delta)
            dsc = pc * (_dot(dog, vc_ref[...], 1, 1) - delta)
            dq_ref[g] = (_dot(dsw, kw, 1, 0) + _dot(dsc, kc_ref[...], 1, 0)) * scale
            dkc_ref[...] += _dot(dsc, qg, 0, 0) * scale
            dvc_ref[...] += _dot(pc, dog, 0, 0)
            psk = jnp.exp(sink_ref[hh * Q_PER_KV + g] - lse_g)
            dsk_ref[pl.ds(g, 1), :] += jnp.broadcast_to(jnp.sum(-psk * delta, axis=0, keepdims=True), (1, LANES))

    qspec = pl.BlockSpec((None, Q_PER_KV, ATTN_BLOCK, hd), lambda h, n: (h, 0, n, 0))
    lspec = pl.BlockSpec((None, Q_PER_KV, ATTN_BLOCK, 1), lambda h, n: (h, 0, n, 0))
    cspec = pl.BlockSpec((None, C, hd), lambda h, n: (h, 0, 0))
    return pl.pallas_call(
        body, name=name, grid=(nkv, nb),
        in_specs=[pl.BlockSpec(memory_space=pltpu.SMEM), qspec] + _band_specs(nb, hd) + _band_specs(nb, hd)
        + [cspec, cspec, qspec, qspec, lspec],
        out_specs=[qspec, cspec, cspec, pl.BlockSpec((None, SUBLANES, LANES), lambda h, n: (h, 0, 0))],
        out_shape=[SDS((nkv, Q_PER_KV, L, hd), F32), SDS((nkv, C, hd), F32), SDS((nkv, C, hd), F32),
                   SDS((nkv, SUBLANES, LANES), F32)],
        compiler_params=_cparams("parallel", "arbitrary"))(sink, q, k, k, k, v, v, v, kc, vc, o, do, lse)


def _attn_bwd_kv(q, k, v, o, do, lse, name):
    nkv, _, L, hd = q.shape
    nb = L // ATTN_BLOCK
    scale = HEAD_DIM ** -0.5

    def body(q0, q1, q2, do0, do1, do2, o0, o1, o2, l0, l1, l2, k_ref, v_ref, dk_ref, dv_ref):
        j = pl.program_id(1)
        qi = lax.broadcasted_iota(jnp.int32, (ATTN_BLOCK, ATTN_BLOCK), 0)
        kk = lax.broadcasted_iota(jnp.int32, (ATTN_BLOCK, ATTN_BLOCK), 1)
        kj, vj = k_ref[...], v_ref[...]
        dk = jnp.zeros((ATTN_BLOCK, hd), F32)
        dv = jnp.zeros((ATTN_BLOCK, hd), F32)
        for slot, (q_r, do_r, o_r, l_r) in enumerate(((q0, do0, o0, l0), (q1, do1, o1, l1), (q2, do2, o2, l2))):
            n = j - 1 + slot
            ok = (n >= 0) & (n < nb) & (jnp.abs(qi + ATTN_BLOCK - ((2 - slot) * ATTN_BLOCK + kk)) <= ATTN_BLOCK)
            for g in range(Q_PER_KV):
                qg, dog = q_r[g], do_r[g]
                delta = jnp.sum(dog.astype(F32) * o_r[g], axis=-1, keepdims=True)
                p = jnp.exp(jnp.where(ok, _dot(qg, kj, 1, 1) * scale - l_r[g], NEG))
                ds = p * (_dot(dog, vj, 1, 1) - delta)
                dk = dk + _dot(ds, qg, 0, 0) * scale
                dv = dv + _dot(p, dog, 0, 0)
        dk_ref[...] = dk
        dv_ref[...] = dv

    def band(width):
        blk = lambda f: pl.BlockSpec((None, Q_PER_KV, ATTN_BLOCK, width), f)
        return [blk(lambda h, j: (h, 0, jnp.maximum(j - 1, 0), 0)), blk(lambda h, j: (h, 0, j, 0)),
                blk(lambda h, j: (h, 0, jnp.minimum(j + 1, nb - 1), 0))]

    kspec = pl.BlockSpec((None, ATTN_BLOCK, hd), lambda h, j: (h, j, 0))
    return pl.pallas_call(
        body, name=name, grid=(nkv, nb), in_specs=band(hd) + band(hd) + band(hd) + band(1) + [kspec, kspec],
        out_specs=[kspec, kspec], out_shape=[SDS((nkv, L, hd), F32), SDS((nkv, L, hd), F32)],
        compiler_params=_cparams("parallel", "parallel"))(q, q, q, do, do, do, o, o, o, lse, lse, lse, k, v)


_GELU_K = math.sqrt(2.0 / math.pi)


def _gelu(v):
    return 0.5 * v * (1.0 + jnp.tanh(_GELU_K * (v + 0.044715 * (v * v * v))))


def _gelu_grad(v):
    t = jnp.tanh(_GELU_K * (v + 0.044715 * (v * v * v)))
    return 0.5 * (1.0 + t) + 0.5 * v * (1.0 - t * t) * (_GELU_K * (1.0 + 3.0 * 0.044715 * (v * v)))


def _gmlp_fwd(p0, b_in, ln_g, ln_b, w_s, b_s, name):
    L, W2 = p0.shape
    W = W2 // 2
    G = W // GMLP_GROUP_DIM

    def body(p_ref, bi_ref, g_ref, b_ref, ws_ref, bs_ref, o_ref):
        ge = _gelu(p_ref[...] + bi_ref[...])
        xh, _ = _layer_norm_stats(ge[:, W:])
        vln = xh * g_ref[...] + b_ref[...]
        for gi in range(G):
            cols = slice(gi * GMLP_GROUP_DIM, (gi + 1) * GMLP_GROUP_DIM)
            s = _dot(ws_ref[gi], vln[:, cols], 1, 0) + bs_ref[gi]
            o_ref[:, cols] = (ge[:, cols] * s).astype(o_ref.dtype)

    full = lambda shape: pl.BlockSpec(shape, lambda i: (0,) * len(shape))
    return pl.pallas_call(
        body, name=name, grid=(L // GMLP_CHUNK,),
        in_specs=[pl.BlockSpec((GMLP_CHUNK, W2), lambda i: (i, 0)), full((1, W2)), full((1, W)), full((1, W)),
                  full((G, GMLP_CHUNK, GMLP_CHUNK)), full((G, GMLP_CHUNK, 1))],
        out_specs=pl.BlockSpec((GMLP_CHUNK, W), lambda i: (i, 0)), out_shape=SDS((L, W), MMT),
        compiler_params=_cparams("parallel"))(p0, b_in, ln_g, ln_b, w_s, b_s)


def _gmlp_bwd(p0, dus, b_in, ln_g, ln_b, w_s, w_st, b_s, name):
    L, W2 = p0.shape
    W = W2 // 2
    G = W // GMLP_GROUP_DIM

    def body(p_ref, d_ref, bi_ref, g_ref, b_ref, ws_ref, wst_ref, bs_ref, dpre_ref, dbi_ref, dg_ref, db_ref, dws_ref, dbs_ref, dvln):
        @pl.when(pl.program_id(0) == 0)
        def _():
            for ref in (dbi_ref, dg_ref, db_ref, dws_ref, dbs_ref):
                ref[...] = jnp.zeros_like(ref)

        pre = p_ref[...] + bi_ref[...]
        ge = _gelu(pre)
        xh, rstd = _layer_norm_stats(ge[:, W:])
        vln = xh * g_ref[...] + b_ref[...]
        dge_u = []
        for gi in range(G):
            cols = slice(gi * GMLP_GROUP_DIM, (gi + 1) * GMLP_GROUP_DIM)
            vg = vln[:, cols]
            s = _dot(ws_ref[gi], vg, 1, 0) + bs_ref[gi]
            dus_g = d_ref[:, cols]
            dge_u.append(dus_g * s)
            ds = dus_g * ge[:, cols]
            dbs_ref[gi] += jnp.sum(ds, axis=1, keepdims=True)
            dws_ref[gi] += _dot(ds, vg, 1, 1)
            dvln[:, cols] = _dot(wst_ref[gi], ds, 1, 0)
        dv = dvln[...]
        dg_ref[...] += _sum0(dv * xh)
        db_ref[...] += _sum0(dv)
        dxh = dv * g_ref[...]
        dv0 = rstd * (dxh - jnp.mean(dxh, axis=-1, keepdims=True) - xh * jnp.mean(dxh * xh, axis=-1, keepdims=True))
        dpre = jnp.concatenate(dge_u + [dv0], axis=1) * _gelu_grad(pre)
        dbi_ref[...] += _sum0(dpre)
        dpre_ref[...] = dpre.astype(dpre_ref.dtype)

    full = lambda shape: pl.BlockSpec(shape, lambda i: (0,) * len(shape))
    mats = (G, GMLP_CHUNK, GMLP_CHUNK)
    return pl.pallas_call(
        body, name=name, grid=(L // GMLP_CHUNK,),
        in_specs=[pl.BlockSpec((GMLP_CHUNK, W2), lambda i: (i, 0)), pl.BlockSpec((GMLP_CHUNK, W), lambda i: (i, 0)),
                  full((1, W2)), full((1, W)), full((1, W)), full(mats), full(mats), full((G, GMLP_CHUNK, 1))],
        out_specs=[pl.BlockSpec((GMLP_CHUNK, W2), lambda i: (i, 0)), full((1, W2)), full((1, W)), full((1, W)), full(mats),
                   full((G, GMLP_CHUNK, 1))],
        out_shape=[SDS((L, W2), MMT), SDS((1, W2), F32), SDS((1, W), F32), SDS((1, W), F32), SDS(mats, F32),
                   SDS((G, GMLP_CHUNK, 1), F32)],
        scratch_shapes=[pltpu.VMEM((GMLP_CHUNK, W), F32)], compiler_params=_cparams("arbitrary"))(
            p0, dus, b_in, ln_g, ln_b, w_s, w_st, b_s)


def _loss_head(h, target, tm, name):
    L, D = h.shape

    def body(h_ref, t_ref, l_ref, d_ref):
        @pl.when(pl.program_id(0) == 0)
        def _():
            l_ref[...] = jnp.zeros_like(l_ref)

        e = h_ref[...] - t_ref[...]
        l_ref[...] += 0.5 * jnp.sum(jnp.mean(e * e, axis=-1, keepdims=True), axis=0, keepdims=True)
        d_ref[...] = e * (1.0 / D)

    row = pl.BlockSpec((tm, D), lambda i: (i, 0))
    return pl.pallas_call(body, name=name, grid=(L // tm,), in_specs=[row, row],
                          out_specs=[pl.BlockSpec((1, 1), lambda i: (0, 0)), row],
                          out_shape=[SDS((1, 1), F32), SDS((L, D), F32)], compiler_params=_cparams("arbitrary"))(h, target)


def _ada_fwd(cond, ada_w, ada_b, name):
    NL, D, n = ada_w.shape
    tn = _tile(n, 768)

    def body(c_ref, w_ref, b_ref, o_ref):
        cv = c_ref[...]
        o_ref[...] = _dot(cv * _sig(cv), w_ref[...], 1, 0) + b_ref[...]

    return pl.pallas_call(
        body, name=name, grid=(NL, n // tn),
        in_specs=[pl.BlockSpec((2 * SUBLANES, D), lambda i, j: (0, 0)), pl.BlockSpec((None, D, tn), lambda i, j: (i, 0, j)),
                  pl.BlockSpec((None, 1, tn), lambda i, j: (i, 0, j))],
        out_specs=pl.BlockSpec((None, 2 * SUBLANES, tn), lambda i, j: (i, 0, j)), out_shape=SDS((NL, 2 * SUBLANES, n), F32),
        compiler_params=_cparams("parallel", "parallel"))(cond, ada_w, ada_b)


def _ada_bwd(cond, ada_w, dm_lat, dm_ctx, name):
    NL, D, n = ada_w.shape
    tn = _tile(n, 768)

    def body(c_ref, w_ref, dl_ref, dc_ref, dw_ref, ds_ref):
        @pl.when((pl.program_id(0) == 0) & (pl.program_id(1) == 0))
        def _():
            ds_ref[...] = jnp.zeros_like(ds_ref)

        cv = c_ref[...]
        row = lax.broadcasted_iota(jnp.int32, (SUBLANES, tn), 0)
        ctx_rows = jnp.where(row == 0, _sum0(dc_ref[...]), 0.0)
        dm = jnp.concatenate([dl_ref[...], ctx_rows], axis=0)
        dw_ref[...] = _dot(cv * _sig(cv), dm, 0, 0)
        ds_ref[...] += _dot(dm, w_ref[...], 1, 1)

    dspec = pl.BlockSpec((None, SUBLANES, tn), lambda i, j: (i, 0, j))
    return pl.pallas_call(
        body, name=name, grid=(NL, n // tn),
        in_specs=[pl.BlockSpec((2 * SUBLANES, D), lambda i, j: (0, 0)), pl.BlockSpec((None, D, tn), lambda i, j: (i, 0, j)),
                  dspec, dspec],
        out_specs=[pl.BlockSpec((None, D, tn), lambda i, j: (i, 0, j)), pl.BlockSpec((2 * SUBLANES, D), lambda i, j: (0, 0))],
        out_shape=[SDS((NL, D, n), F32), SDS((2 * SUBLANES, D), F32)],
        compiler_params=_cparams("arbitrary", "arbitrary"))(cond, ada_w, dm_lat, dm_ctx)


def _adam_math(w, g, m, v):
    m = ADAM_B1 * m + (1.0 - ADAM_B1) * g
    v = ADAM_B2 * v + (1.0 - ADAM_B2) * jnp.square(g)
    m_hat = m / (1.0 - ADAM_B1 ** ADAM_STEP)
    v_hat = v / (1.0 - ADAM_B2 ** ADAM_STEP)
    return -ADAM_LR * (m_hat / (jnp.sqrt(v_hat) + ADAM_EPS) + ADAM_WD * w), m, v


def _row_tile(rows, cols, elems, mult=SUBLANES):
    want = max(mult, elems // cols)
    best = mult if rows % mult == 0 else rows
    for d in range(mult, min(rows, want) + 1, mult):
        if rows % d == 0:
            best = d
    return best


def _adamw(w, m, v, parts, name):
    R, C = w.shape
    tr = _row_tile(R, C, 128 * 1024)
    npart = len(parts)

    def body(*refs):
        w_ref, m_ref, v_ref = refs[:3]
        g_ref, d_ref, nm_ref, nv_ref = refs[3 + npart:]
        g = refs[3][...]
        for p_ref in refs[4:3 + npart]:
            g = g + p_ref[...]
        d, nm, nv = _adam_math(w_ref[...], g, m_ref[...], v_ref[...])
        g_ref[...], d_ref[...], nm_ref[...], nv_ref[...] = g, d, nm, nv

    blk = pl.BlockSpec((tr, C), lambda i: (i, 0))
    return pl.pallas_call(body, name=name, grid=(R // tr,), in_specs=[blk] * (3 + npart), out_specs=[blk] * 4,
                          out_shape=[SDS((R, C), F32)] * 4, compiler_params=_cparams("parallel"))(w, m, v, *parts)


def _adamw_layer(w, m, v, layer, parts, prev, name):
    _, R, C = w.shape
    tr = _row_tile(R, C, 128 * 1024, 2 * SUBLANES)
    npart = len(parts)
    nprev = 0 if prev is None else 4

    def body(*refs):
        w_ref, m_ref, v_ref = refs[:3]
        g_ref, d_ref, nm_ref, nv_ref = refs[3 + npart + nprev:]
        g = refs[3][...].astype(F32)
        for p_ref in refs[4:3 + npart]:
            g = g + p_ref[...].astype(F32)
        d, nm, nv = _adam_math(w_ref[...], g, m_ref[...], v_ref[...])
        g_ref[...], d_ref[...], nm_ref[...], nv_ref[...] = g, d, nm, nv

    stacked = pl.BlockSpec((None, tr, C), lambda i: (layer, i, 0))
    flat = pl.BlockSpec((tr, C), lambda i: (i, 0))
    return pl.pallas_call(
        body, name=name, grid=(R // tr,),
        in_specs=[stacked] * 3 + [flat] * npart + [pl.BlockSpec(memory_space=pl.ANY)] * nprev, out_specs=[stacked] * 4,
        out_shape=[SDS(w.shape, F32)] * 4, input_output_aliases={3 + npart + k: k for k in range(nprev)},
        compiler_params=_cparams("parallel"))(w, m, v, *parts, *(prev or ()))


def _sum_slots(x, name, after=None, out_dtype=F32):
    S, R, C = x.shape
    tr = _row_tile(R, C, 128 * 1024, SUBLANES * 4 // jnp.dtype(x.dtype).itemsize)
    extra = [] if after is None else [after]

    def body(x_ref, *rest):
        o_ref = rest[-1]
        acc = x_ref[0].astype(F32)
        for s in range(1, S):
            acc = acc + x_ref[s].astype(F32)
        o_ref[...] = acc.astype(o_ref.dtype)

    return pl.pallas_call(
        body, name=name, grid=(R // tr,),
        in_specs=[pl.BlockSpec((S, tr, C), lambda i: (0, i, 0))] + [pl.BlockSpec(memory_space=pl.ANY)] * len(extra),
        out_specs=pl.BlockSpec((tr, C), lambda i: (i, 0)), out_shape=SDS((R, C), out_dtype),
        compiler_params=_cparams("parallel"))(x, *extra)


def _my_place():
    return lax.axis_index("x"), lax.axis_index("y"), lax.axis_index("c")


def _other_chips(x, y):
    return [(1 - x, y), (x, 1 - y), (1 - x, 1 - y)]


def _all_gather(v, name):
    R, C = v.shape

    def body(v_ref, o_ref, send_sems, recv_sems, local_sem):
        x, y, c = _my_place()
        me = 4 * x + 2 * y + c
        mine = pltpu.make_async_copy(v_ref, o_ref.at[me], local_sem)
        mine.start()
        copies = []
        for flip in range(1, N_DEV):
            fx, fy, fc = (flip >> 2) & 1, (flip >> 1) & 1, flip & 1
            peer = ((x + fx) % 2, (y + fy) % 2, (c + fc) % 2)
            cp = pltpu.make_async_remote_copy(src_ref=v_ref, dst_ref=o_ref.at[me], send_sem=send_sems.at[flip - 1],
                                              recv_sem=recv_sems.at[flip - 1], device_id=peer, device_id_type=MESH)
            cp.start()
            copies.append(cp)
        for cp in copies:
            cp.wait()
        mine.wait()

    return pl.pallas_call(
        body, name=name, in_specs=[pl.BlockSpec(memory_space=pl.ANY)], out_specs=pl.BlockSpec(memory_space=pl.ANY),
        out_shape=SDS((N_DEV, R, C), v.dtype),
        scratch_shapes=[pltpu.SemaphoreType.DMA((N_DEV - 1,)), pltpu.SemaphoreType.DMA((N_DEV - 1,)), pltpu.SemaphoreType.DMA],
        )(v)


def _shard_window(ref, axis, j, size):
    idx = [slice(None)] * len(ref.shape)
    idx[axis] = pl.ds(pl.multiple_of(j * size, SUBLANES), size)
    return ref.at[tuple(idx)]


def _gather_plan(axis):
    def plan(srcs, lands):
        x, y, c = _my_place()
        shard, whole = srcs[0], lands[0]
        half = shard.shape[0] // 2
        size = shard.shape[axis]

        def window(chip, which):
            if axis == 1:
                return whole.at[pl.ds(pl.multiple_of(which * half, SUBLANES), half), pl.ds(pl.multiple_of(chip * size, LANES), size)]
            return whole.at[pl.ds(pl.multiple_of(chip * size + which * half, SUBLANES), half), :]

        j = 2 * x + y
        local = [(shard, _shard_window(whole, axis, j, size))]
        mine = shard.at[pl.ds(pl.multiple_of(c * half, SUBLANES), half), :]
        remote = [(mine, window(j, c), (px, py, c)) for px, py in _other_chips(x, y)]
        forward = [(window(2 * px + py, c), window(2 * px + py, c), (x, y, 1 - c)) for px, py in _other_chips(x, y)]
        return local, remote, forward
    return plan


def _scatter_plan(axis):
    def plan(srcs, lands):
        x, y, c = _my_place()
        j = 2 * x + y
        size = srcs[0].shape[axis] // N_CHIPS
        local = [(_shard_window(srcs[0], axis, j, size), lands[0].at[j])]
        remote = [(_shard_window(srcs[0], axis, 2 * px + py, size), lands[0].at[j], (px, py, c)) for px, py in _other_chips(x, y)]
        return local, remote, []
    return plan


def _all_gather_plan(srcs, lands):
    x, y, c = _my_place()
    dst = lands[0].at[4 * x + 2 * y + c]
    remote = []
    for flip in range(1, N_DEV):
        fx, fy, fc = (flip >> 2) & 1, (flip >> 1) & 1, flip & 1
        remote.append((srcs[0], dst, ((x + fx) % 2, (y + fy) % 2, (c + fc) % 2)))
    return [(srcs[0], dst)], remote, []


def _same_core_peers():
    x, y, c = _my_place()
    return [(px, py, c) for px, py in _other_chips(x, y)]


def _same_core_peers_and_sibling():
    x, y, c = _my_place()
    return _same_core_peers() + [(x, y, 1 - c)]


def _all_peers():
    x, y, c = _my_place()
    return [((x + (f >> 2 & 1)) % 2, (y + (f >> 1 & 1)) % 2, (c + (f & 1)) % 2) for f in range(1, N_DEV)]


def _sequencer_exchange(name, collective_id, exchanges, peers_fn):
    hbm = pltpu.MemorySpace.HBM
    src_refs = [[jax.new_ref(s, memory_space=hbm) for s in e[0]] for e in exchanges]
    land_refs = [[jax.empty_ref(s, memory_space=hbm) for s in e[1]] for e in exchanges]
    first = [sum(e[3] for e in exchanges[:i]) for i in range(len(exchanges))]
    ncopy = sum(e[3] for e in exchanges)
    dma = pltpu.SemaphoreType.DMA

    @pl.kernel(mesh=plsc.ScalarSubcoreMesh(axis_name="sequencer", num_cores=N_SEQUENCERS), name=name,
               scratch_types=(dma((ncopy,)), dma((ncopy,)), dma((ncopy,)), dma((ncopy,)), dma),
               compiler_params=pltpu.CompilerParams(collective_id=collective_id))
    def launch(send_sems, recv_sems, onward_send_sems, onward_recv_sems, local_sem):
        me = lax.axis_index("sequencer")
        peers = peers_fn()
        barrier = pltpu.get_barrier_semaphore()
        for peer in peers:
            pl.semaphore_signal(barrier, inc=1, device_id=peer, device_id_type=MESH)
        pl.semaphore_wait(barrier, len(peers))
        plans = [e[2](src_refs[i], land_refs[i]) for i, e in enumerate(exchanges)]
        nbytes = lambda ref: math.prod(ref.shape) * jnp.dtype(ref.dtype).itemsize
        jobs = [(nbytes(dst) // LOCAL_BYTES_PER_ICI_BYTE, "local", (i, k)) for i, p in enumerate(plans) for k, (_, dst) in enumerate(p[0])]
        jobs += [(nbytes(src) + (nbytes(src) // LOCAL_BYTES_PER_ICI_BYTE if p[2] else 0), "remote", (i, k))
                 for i, p in enumerate(plans) for k, (src, _, _) in enumerate(p[1])]
        load, owner = [0] * N_SEQUENCERS, {}
        for size, kind, key in sorted(jobs, key=lambda job: -job[0]):
            owner[kind, key] = load.index(min(load))
            load[owner[kind, key]] += size
        for s in range(N_SEQUENCERS):
            @pl.when(me == s)
            def _(s=s):
                for i, (local, _, _) in enumerate(plans):
                    for k, (src, dst) in enumerate(local):
                        if owner["local", (i, k)] == s:
                            cp = pltpu.make_async_copy(src, dst, local_sem)
                            cp.start()
                            cp.wait()
                copies, onward = [], []
                for i, (_, remote, forward) in enumerate(plans):
                    assert len(remote) == exchanges[i][3] and len(forward) in (0, len(remote))
                    for k, (src, dst, peer) in enumerate(remote):
                        if owner["remote", (i, k)] == s:
                            cp = pltpu.make_async_remote_copy(src_ref=src, dst_ref=dst, send_sem=send_sems.at[first[i] + k],
                                                              recv_sem=recv_sems.at[first[i] + k], device_id=peer,
                                                              device_id_type=MESH)
                            cp.start()
                            copies.append(cp)
                            if forward:
                                src, dst, peer = forward[k]
                                onward.append(pltpu.make_async_remote_copy(
                                    src_ref=src, dst_ref=dst, send_sem=onward_send_sems.at[first[i] + k],
                                    recv_sem=onward_recv_sems.at[first[i] + k], device_id=peer, device_id_type=MESH))
                            else:
                                onward.append(None)
                for cp, on in zip(copies, onward):
                    cp.wait_recv()
                    if on is not None:
                        on.start()
                for cp, on in zip(copies, onward):
                    cp.wait_send()
                    if on is not None:
                        on.wait()

    launch()
    return [[r[...] for r in refs] for refs in land_refs]


def _swap_with_sibling(parts, name):
    nt = len(parts)

    def body(*refs):
        ins, outs = refs[:nt], refs[nt:2 * nt]
        send_sems, recv_sems = refs[2 * nt:]
        x, y, c = _my_place()
        copies = []
        for t in range(nt):
            cp = pltpu.make_async_remote_copy(src_ref=ins[t], dst_ref=outs[t], send_sem=send_sems.at[t], recv_sem=recv_sems.at[t],
                                              device_id=(x, y, 1 - c), device_id_type=MESH)
            cp.start()
            copies.append(cp)
        for cp in copies:
            cp.wait()

    any_spec = pl.BlockSpec(memory_space=pl.ANY)
    return pl.pallas_call(
        body, name=name, in_specs=[any_spec] * nt, out_specs=[any_spec] * nt, out_shape=[SDS(p.shape, p.dtype) for p in parts],
        scratch_shapes=[pltpu.SemaphoreType.DMA((nt,)), pltpu.SemaphoreType.DMA((nt,))],
        )(*parts)


PACK_COLS = 1024


def _pack(arrays):
    flat = jnp.concatenate([a.reshape(-1) for a in arrays])
    pad = (-flat.shape[0]) % (2 * SUBLANES * PACK_COLS)
    return jnp.pad(flat, (0, pad)).reshape(-1, PACK_COLS)


def _unpack(packed, shapes):
    flat, out, pos = packed.reshape(-1), [], 0
    for shape in shapes:
        n = math.prod(shape)
        out.append(flat[pos:pos + n].reshape(shape))
        pos += n
    return out


def _unshard_last(stacked):
    moved = jnp.moveaxis(stacked, 0, -2)
    return moved.reshape(moved.shape[:-2] + (moved.shape[-2] * moved.shape[-1],))


def _my_block_last(full, j):
    s = full.shape[-1] // N_CHIPS
    return lax.dynamic_index_in_dim(full.reshape(full.shape[:-1] + (N_CHIPS, s)), j, axis=full.ndim - 1, keepdims=False)


def _rope_tables(L):
    rows = L // GRID_W
    row = jnp.repeat(jnp.arange(rows), GRID_W).astype(F32)
    col = jnp.tile(jnp.arange(GRID_W), rows).astype(F32)
    axis_dim = HEAD_DIM // 2
    inv_freq = ROPE_BASE ** (-jnp.arange(0, axis_dim, 2, dtype=F32) / axis_dim)
    ang_r, ang_c = row[:, None] * inv_freq[None, :], col[:, None] * inv_freq[None, :]
    ang = jnp.concatenate([ang_r, ang_r, ang_c, ang_c] * 2, axis=-1)
    return jnp.cos(ang), jnp.sin(ang)


SMALL_SHARDED = ("norm_g", "ffn_conv_w", "cm_b_in", "cm_dw_w", "cm_dw_b", "cm_ln_g", "cm_ln_b", "cm_b_out", "gm_b_in", "gm_ln_g",
                 "gm_ln_b")
SMALL_REPLICATED = ("c_ctx", "ada_b", "ffn_conv_b", "attn_sink", "gm_w_s", "gm_b_s")
BIG = ("ffn_w_up", "ffn_w_down", "cm_w_in", "cm_w_out", "attn_w_qkv", "attn_w_o", "gm_w_in", "gm_w_out")
BIG_AXIS = {"ffn_w_up": 2, "ffn_w_down": 1, "cm_w_in": 2, "cm_w_out": 1, "attn_w_qkv": 2, "attn_w_o": 1, "gm_w_in": 2, "gm_w_out": 1}
WEIGHTS = ("c_ctx", "ada_w", "ada_b", "norm_g", "ffn_w_up", "ffn_conv_w", "ffn_conv_b", "ffn_w_down", "cm_w_in", "cm_b_in",
           "cm_dw_w", "cm_dw_b", "cm_ln_g", "cm_ln_b", "cm_w_out", "cm_b_out", "attn_w_qkv", "attn_sink", "attn_w_o", "gm_w_in",
           "gm_b_in", "gm_ln_g", "gm_ln_b", "gm_w_s", "gm_b_s", "gm_w_out")


def _step(x, c, ctx, target, W, M, V):
    L, D = x.shape[1], x.shape[2]
    C = ctx.shape[1]
    T = L + C
    NL = W["ada_w"].shape[0]
    tm = 256 if C % 256 == 0 else 128
    nl = L // tm
    xi, yi, ci = _my_place()
    chip = 2 * xi + yi
    dev = 4 * xi + 2 * yi + ci
    segs2, segs1 = [(0, L), (L, C)], [(0, L)]
    vec = lambda a: a.reshape(1, -1)

    layer_sets = [[("cm_w_in", 0), ("cm_w_out", 0), ("ffn_w_up", 0), ("ffn_w_down", 0)],
                  [("attn_w_qkv", 0), ("attn_w_o", 0), ("ffn_w_up", 1), ("ffn_w_down", 1)],
                  [("gm_w_in", 0), ("gm_w_out", 0), ("ffn_w_up", 2), ("ffn_w_down", 2)],
                  [("cm_w_in", 1), ("cm_w_out", 1), ("ffn_w_up", 3), ("ffn_w_down", 3)]]
    arrived = {}

    def fetch(keys, zero, sequencer_id):
        exchanges = []
        for n, i in keys:
            shard = (W[n][i] + zero).astype(MMT)
            whole = list(shard.shape)
            whole[BIG_AXIS[n] - 1] *= N_CHIPS
            exchanges.append(([shard], [SDS(tuple(whole), MMT)], _gather_plan(BIG_AXIS[n] - 1), N_CHIPS - 1))
        lands = _sequencer_exchange(f"fetch_weights_{keys[0][0]}_{keys[0][1]}", sequencer_id, exchanges,
                                    _same_core_peers_and_sibling)
        for key, land in zip(keys, lands):
            arrived[key] = land[0]

    def big(n, i, after=None):
        return arrived[(n, i)]

    small_shapes = [W[n].shape for n in SMALL_SHARDED]
    ag1 = _all_gather(_pack([c.reshape(-1)] + [W[n] for n in SMALL_SHARDED]), "gather_small")
    parts = [_unpack(ag1[2 * s], [(D,)] + small_shapes) for s in range(N_CHIPS)]
    c_rows = jnp.stack([_unpack(ag1[d], [(D,)])[0] for d in range(N_DEV)])
    P = {n: _unshard_last(jnp.stack([parts[s][1 + i] for s in range(N_CHIPS)])) for i, n in enumerate(SMALL_SHARDED)}
    for n in SMALL_REPLICATED:
        P[n] = W[n]

    cond = jnp.concatenate([c_rows, W["c_ctx"][None, :], jnp.zeros((2 * SUBLANES - N_DEV - 1, D), F32)], axis=0)
    ncol = W["ada_w"].shape[2]
    ada_b_mine = lax.dynamic_slice_in_dim(W["ada_b"], chip * ncol, ncol, axis=1)[:, None, :]
    mods_mine = _ada_fwd(cond, W["ada_w"], ada_b_mine, "ada_fwd")
    ag2 = _all_gather(mods_mine.reshape(NL * 2 * SUBLANES, ncol), "gather_mods").reshape(N_DEV, NL, 2 * SUBLANES, ncol)
    mods_all = _unshard_last(jnp.stack([ag2[2 * s] for s in range(N_CHIPS)]))
    mod_lat = lax.dynamic_index_in_dim(mods_all, dev, axis=1, keepdims=False).reshape(NL, 6, D)
    mod_ctx = mods_all[:, N_DEV].reshape(NL, 6, D)
    mod2 = jnp.stack([mod_lat, mod_ctx], axis=1)
    mod1 = mod_lat[:, None]

    corner = mod2[0, 0, 0, 0]
    behind_small = jnp.where(corner != corner, corner, 0.0)
    for k, keys in enumerate([layer_sets[0][:2], layer_sets[0][2:]] + layer_sets[1:]):
        fetch(keys, behind_small, FETCH_IDS[k])
    zero_d = jnp.zeros((1, D), F32)
    cos, sin = _rope_tables(L)
    nkv = D // HEAD_DIM // Q_PER_KV
    qdim, kvdim = D, nkv * HEAD_DIM

    def ffn_fwd(i, h, mod, rows, segs, tag):
        a2 = _prenorm(h, mod, vec(P["norm_g"][i, 2]), 1, rows, nl, tm, f"pre_ffn_{tag}")
        z0 = _mm(a2, big("ffn_w_up", i, a2), "nn", F32, f"ffn_up_{tag}")
        u = _ffn_gate(z0, P["ffn_conv_w"][i], vec(P["ffn_conv_b"][i]), segs, f"ffn_gate_{tag}")
        f = _mm(u, big("ffn_w_down", i, u), "nn", F32, f"ffn_down_{tag}")
        h_out = _postnorm(h, f, zero_d, mod, vec(P["norm_g"][i, 3]), 5, rows, nl, tm, f"post_ffn_{tag}")
        return h_out, dict(h=h, a2=a2, z0=z0, f=f)

    def ffn_bwd(i, dh, sv, mod, rows, segs, tag, G):
        df, dg2, dgn3, _ = _postnorm_bwd(dh, sv["f"], zero_d, mod, vec(P["norm_g"][i, 3]), 5, rows, nl, tm, f"post_ffn_bwd_{tag}")
        du = _mm(df, big("ffn_w_down", i), "nt", F32, f"ffn_down_dx_{tag}")
        u, dz0, dcw, dcb = _ffn_gate_bwd(sv["z0"], du, P["ffn_conv_w"][i], vec(P["ffn_conv_b"][i]), segs, f"ffn_gate_bwd_{tag}")
        G["ffn_w_down"][i] = _mm(u, df, "tn", MMT, f"ffn_down_dw_{tag}")
        G["ffn_w_up"][i] = _mm(sv["a2"], dz0, "tn", MMT, f"ffn_up_dw_{tag}")
        da2 = _mm(dz0, big("ffn_w_up", i), "nt", F32, f"ffn_up_dx_{tag}")
        dh, dsh2, dsc2, dgn2 = _prenorm_bwd(sv["h"], da2, dh, mod, vec(P["norm_g"][i, 2]), 1, rows, nl, tm, f"pre_ffn_bwd_{tag}")
        G["ffn_conv_w"][i], G["ffn_conv_b"][i] = dcw, dcb[0]
        return dh, (dsh2, dsc2, dg2), (dgn2, dgn3)

    def conformer_fwd(i, j, h, mod, rows, segs, tag):
        a = _prenorm(h, mod, vec(P["norm_g"][i, 0]), 0, rows, nl, tm, f"pre_mix_{tag}")
        p0 = _mm(a, big("cm_w_in", j, a), "nn", F32, f"cm_in_{tag}")
        z2 = _glu_conv(p0, vec(P["cm_b_in"][j]), P["cm_dw_w"][j], vec(P["cm_dw_b"][j]), segs, f"cm_conv_{tag}")
        z4 = _ln_silu(z2, vec(P["cm_ln_g"][j]), vec(P["cm_ln_b"][j]), rows, tm, f"cm_ln_{tag}")
        y = _mm(z4, big("cm_w_out", j, z4), "nn", F32, f"cm_out_{tag}")
        h_out = _postnorm(h, y, vec(P["cm_b_out"][j]), mod, vec(P["norm_g"][i, 1]), 2, rows, nl, tm, f"post_mix_{tag}")
        return h_out, dict(h=h, a=a, p0=p0, z2=z2, z4=z4, y=y)

    def conformer_bwd(i, j, dh, sv, mod, rows, segs, tag, G):
        dy, dg1, dgn1, dbo = _postnorm_bwd(dh, sv["y"], vec(P["cm_b_out"][j]) + zero_d, mod, vec(P["norm_g"][i, 1]), 2, rows, nl,
                                           tm, f"post_mix_bwd_{tag}")
        G["cm_w_out"][j] = _mm(sv["z4"], dy, "tn", MMT, f"cm_out_dw_{tag}")
        dz4 = _mm(dy, big("cm_w_out", j), "nt", F32, f"cm_out_dx_{tag}")
        dz2, dlg, dlb = _ln_silu_bwd(sv["z2"], dz4, vec(P["cm_ln_g"][j]), vec(P["cm_ln_b"][j]), rows, tm, f"cm_ln_bwd_{tag}")
        dpa, dpg, ddw, ddb, dba, dbg = _glu_conv_bwd(sv["p0"], vec(P["cm_b_in"][j]), P["cm_dw_w"][j], dz2, segs, f"cm_conv_bwd_{tag}")
        dp = jnp.concatenate([dpa, dpg], axis=1)
        G["cm_w_in"][j] = _mm(sv["a"], dp, "tn", MMT, f"cm_in_dw_{tag}")
        da = _mm(dp, big("cm_w_in", j), "nt", F32, f"cm_in_dx_{tag}")
        dh, dsh1, dsc1, dgn0 = _prenorm_bwd(sv["h"], da, dh, mod, vec(P["norm_g"][i, 0]), 0, rows, nl, tm, f"pre_mix_bwd_{tag}")
        G["cm_b_out"][j] = jnp.sum(dbo, axis=0)[0]
        G["cm_ln_g"][j], G["cm_ln_b"][j], G["cm_dw_w"][j], G["cm_dw_b"][j] = dlg[0], dlb[0], ddw, ddb[0]
        G["cm_b_in"][j] = jnp.concatenate([dba[0], dbg[0]])
        return dh, (dsh1, dsc1, dg1), (dgn0, dgn1)

    def heads(a, n):
        return a.reshape(a.shape[0], n, HEAD_DIM).transpose(1, 0, 2)

    def unheads(a):
        return a.transpose(1, 0, 2).reshape(a.shape[1], -1)

    G = {n: [None] * W[n].shape[0] for n in WEIGHTS if n not in ("c_ctx", "ada_w", "ada_b", "norm_g")}
    saved = []
    h = jnp.concatenate([x[0], ctx[0]], axis=0)
    h, s_mix = conformer_fwd(0, 0, h, mod2[0], T, segs2, "l0")
    h, s_ffn = ffn_fwd(0, h, mod2[0], T, segs2, "l0")
    saved.append((s_mix, s_ffn))
    a_all = _prenorm(h, mod2[1], vec(P["norm_g"][1, 0]), 0, T, nl, tm, "pre_mix_l1")
    qkv = _mm(a_all, big("attn_w_qkv", 0, a_all), "nn", F32, "attn_qkv")
    qk_rot, v_lat = _rope(qkv, cos, sin, L, qdim + kvdim, tm, "rope")
    q_h = heads(qk_rot[:, :qdim], nkv * Q_PER_KV).reshape(nkv, Q_PER_KV, L, HEAD_DIM)
    k_h, v_h = heads(qk_rot[:, qdim:], nkv), heads(v_lat, nkv)
    kc_h = heads(qkv[L:, qdim:qdim + kvdim].astype(MMT), nkv)
    vc_h = heads(qkv[L:, qdim + kvdim:].astype(MMT), nkv)
    sink = P["attn_sink"][0]
    o_h, lse = _attn_fwd(q_h, k_h, v_h, kc_h, vc_h, sink, "attn")
    o_nat = unheads(o_h.reshape(nkv * Q_PER_KV, L, HEAD_DIM)).astype(MMT)
    y1 = _mm(o_nat, big("attn_w_o", 0, o_nat), "nn", F32, "attn_out")
    h_in1 = h
    h = _postnorm(h, y1, zero_d, mod1[1], vec(P["norm_g"][1, 1]), 2, L, nl, tm, "post_mix_l1")
    h, s_ffn1 = ffn_fwd(1, h, mod1[1], L, segs1, "lat")
    h_in2 = h
    a_2 = _prenorm(h, mod1[2], vec(P["norm_g"][2, 0]), 0, L, nl, tm, "pre_mix_l2")
    p0_2 = _mm(a_2, big("gm_w_in", 0, a_2), "nn", F32, "gm_in")
    ws_bf = P["gm_w_s"][0].astype(MMT)
    bs_col = P["gm_b_s"][0][:, :, None]
    us = _gmlp_fwd(p0_2, vec(P["gm_b_in"][0]), vec(P["gm_ln_g"][0]), vec(P["gm_ln_b"][0]), ws_bf, bs_col, "gmlp")
    y2 = _mm(us, big("gm_w_out", 0, us), "nn", F32, "gm_out")
    h = _postnorm(h, y2, zero_d, mod1[2], vec(P["norm_g"][2, 1]), 2, L, nl, tm, "post_mix_l2")
    h, s_ffn2 = ffn_fwd(2, h, mod1[2], L, segs1, "lat")
    h, s_mix3 = conformer_fwd(3, 1, h, mod1[3], L, segs1, "l3")
    h, s_ffn3 = ffn_fwd(3, h, mod1[3], L, segs1, "lat")

    loss_mine, dh = _loss_head(h, target[0], tm, "loss_head")

    dmod = [None] * NL
    dgn = [None] * NL

    def finish(i, mix, ffn, gns_mix, gns_ffn):
        dmod[i] = jnp.concatenate(list(mix) + list(ffn), axis=1)
        dgn[i] = jnp.stack([jnp.sum(g, axis=0)[0] for g in (gns_mix[0], gns_mix[1], gns_ffn[0], gns_ffn[1])])

    sent, so_far = {}, {}

    def send(tag, collective_id, tensors):
        exchanges = []
        for n, l in tensors:
            g = G[n][l]
            shard = list(g.shape)
            shard[BIG_AXIS[n] - 1] //= N_CHIPS
            exchanges.append(([g], [SDS((N_CHIPS,) + tuple(shard), g.dtype)], _scatter_plan(BIG_AXIS[n] - 1), N_CHIPS - 1))
        sent[tag] = (tensors, _sequencer_exchange(f"send_grads_{tag}", collective_id, exchanges, _same_core_peers))
        corner = sum(G[n][l][0:1, 0:1].astype(F32) for n, l in tensors)
        return jnp.where(corner != corner, corner, 0.0)

    def land(tag, after):
        tensors, landed = sent[tag]
        mine = [_sum_slots(lands[0], f"sum_chips_{n}_{l}", after, MMT) for (n, l), lands in zip(tensors, landed)]
        theirs = _swap_with_sibling(mine, f"swap_cores_{tag}")
        for (n, l), a, b in zip(tensors, mine, theirs):
            so_far[n] = _adamw_layer(W[n], M[n], V[n], l, [a, b], so_far.get(n), f"adamw_{n}_{l}")

    dh, m_ffn, n_ffn = ffn_bwd(3, dh, s_ffn3, mod1[3], L, segs1, "lat", G)
    dh, m_mix, n_mix = conformer_bwd(3, 1, dh, s_mix3, mod1[3], L, segs1, "l3", G)
    finish(3, m_mix, m_ffn, n_mix, n_ffn)
    zero_d = zero_d + send("l3", SEND_IDS[0], [("ffn_w_up", 3), ("ffn_w_down", 3), ("cm_w_in", 1), ("cm_w_out", 1)])
    land("l3", None)

    dh, m_ffn, n_ffn = ffn_bwd(2, dh, s_ffn2, mod1[2], L, segs1, "lat", G)
    dy2, dg1, dgn1, _ = _postnorm_bwd(dh, y2, zero_d, mod1[2], vec(P["norm_g"][2, 1]), 2, L, nl, tm, "post_mix_bwd_l2")
    G["gm_w_out"][0] = _mm(us, dy2, "tn", MMT, "gm_out_dw")
    dus = _mm(dy2, big("gm_w_out", 0), "nt", F32, "gm_out_dx")
    ws_t = jnp.swapaxes(P["gm_w_s"][0], 1, 2).astype(MMT)
    dpre, dbi, dlg, dlb, dws, dbs = _gmlp_bwd(p0_2, dus, vec(P["gm_b_in"][0]), vec(P["gm_ln_g"][0]), vec(P["gm_ln_b"][0]), ws_bf,
                                              ws_t, bs_col, "gmlp_bwd")
    G["gm_w_in"][0] = _mm(a_2, dpre, "tn", MMT, "gm_in_dw")
    da = _mm(dpre, big("gm_w_in", 0), "nt", F32, "gm_in_dx")
    dh, dsh1, dsc1, dgn0 = _prenorm_bwd(h_in2, da, dh, mod1[2], vec(P["norm_g"][2, 0]), 0, L, nl, tm, "pre_mix_bwd_l2")
    G["gm_b_in"][0], G["gm_ln_g"][0], G["gm_ln_b"][0], G["gm_w_s"][0], G["gm_b_s"][0] = dbi[0], dlg[0], dlb[0], dws, dbs[:, :, 0]
    finish(2, (dsh1, dsc1, dg1), m_ffn, (dgn0, dgn1), n_ffn)
    zero_d = zero_d + send("l2", SEND_IDS[1], [("ffn_w_up", 2), ("ffn_w_down", 2), ("gm_w_in", 0), ("gm_w_out", 0)])
    land("l2", None)

    dh, m_ffn, n_ffn = ffn_bwd(1, dh, s_ffn1, mod1[1], L, segs1, "lat", G)
    dy1, dg1, dgn1, _ = _postnorm_bwd(dh, y1, zero_d, mod1[1], vec(P["norm_g"][1, 1]), 2, L, nl, tm, "post_mix_bwd_l1")
    G["attn_w_o"][0] = _mm(o_nat, dy1, "tn", MMT, "attn_out_dw")
    do_nat = _mm(dy1, big("attn_w_o", 0), "nt", MMT, "attn_out_dx")
    do_h = heads(do_nat, nkv * Q_PER_KV).reshape(nkv, Q_PER_KV, L, HEAD_DIM)
    dq_h, dkc_h, dvc_h, dsk = _attn_bwd_q(q_h, k_h, v_h, kc_h, vc_h, sink, o_h, do_h, lse, "attn_bwd_q")
    dk_h, dv_h = _attn_bwd_kv(q_h, k_h, v_h, o_h, do_h, lse, "attn_bwd_kv")
    dqk = jnp.concatenate([unheads(dq_h.reshape(nkv * Q_PER_KV, L, HEAD_DIM)), unheads(dk_h)], axis=1)
    dqkv_lat = _rope_bwd(dqk, unheads(dv_h), cos, sin, tm, "rope_bwd")
    dqkv_ctx = jnp.concatenate([jnp.zeros((C, qdim), MMT), unheads(dkc_h).astype(MMT), unheads(dvc_h).astype(MMT)], axis=1)
    dqkv = jnp.concatenate([dqkv_lat, dqkv_ctx], axis=0)
    G["attn_w_qkv"][0] = _mm(a_all, dqkv, "tn", MMT, "attn_qkv_dw")
    da_all = _mm(dqkv, big("attn_w_qkv", 0), "nt", F32, "attn_qkv_dx")
    dh_all = jnp.concatenate([dh, jnp.zeros((C, D), F32)], axis=0)
    dh, dsh1, dsc1, dgn0 = _prenorm_bwd(h_in1, da_all, dh_all, mod2[1], vec(P["norm_g"][1, 0]), 0, T, nl, tm, "pre_mix_bwd_l1")
    G["attn_sink"][0] = dsk[:, :Q_PER_KV, 0].reshape(-1)
    pad_ctx = lambda a: jnp.concatenate([a, jnp.zeros_like(a)], axis=0)
    finish(1, (dsh1, dsc1, pad_ctx(dg1)), [pad_ctx(a) for a in m_ffn], (dgn0, dgn1), n_ffn)
    zero_d = zero_d + send("l1", SEND_IDS[2], [("ffn_w_up", 1), ("ffn_w_down", 1), ("attn_w_qkv", 0), ("attn_w_o", 0)])
    land("l1", None)

    s_mix0, s_ffn0 = saved[0]
    dh, m_ffn, n_ffn = ffn_bwd(0, dh, s_ffn0, mod2[0], T, segs2, "l0", G)
    zero_d = zero_d + send("l0_ffn", SEND_IDS[3], [("ffn_w_up", 0), ("ffn_w_down", 0)])
    dh, m_mix, n_mix = conformer_bwd(0, 0, dh, s_mix0, mod2[0], T, segs2, "l0", G)
    finish(0, m_mix, m_ffn, n_mix, n_ffn)
    grad_x = dh[:L][None]
    sent_l0 = send("l0_mix", SEND_IDS[4], [("cm_w_in", 0), ("cm_w_out", 0)])

    for i in range(2, NL):
        dmod[i] = pad_ctx(dmod[i])
    dmod_all = jnp.stack(dmod).reshape(NL, 2, 6 * D) + sent_l0

    ag3 = _all_gather(dmod_all.reshape(NL * 2, 6 * D), "gather_dmods").reshape(N_DEV, NL, 2, N_CHIPS, ncol)
    dm_cols = lax.dynamic_index_in_dim(ag3, chip, axis=3, keepdims=False)
    dm_lat, dm_ctx = jnp.moveaxis(dm_cols[:, :, 0], 0, 1), jnp.moveaxis(dm_cols[:, :, 1], 0, 1)
    g_ada_w, dsilu = _ada_bwd(cond, W["ada_w"], dm_lat, dm_ctx, "ada_bwd")
    cc = W["c_ctx"]
    sg = jax.nn.sigmoid(cc)
    dcctx_part = jnp.where(ci == 0, 1.0, 0.0) * dsilu[N_DEV] * (sg * (1.0 + cc * (1.0 - sg)))

    Gs = {n: jnp.stack(G[n]) for n in G if n not in BIG}
    Gs["norm_g"] = jnp.stack(dgn)
    Gs["ada_b"] = jnp.sum(dmod_all, axis=1)
    Gs["c_ctx"] = dcctx_part
    small_names = list(SMALL_SHARDED) + list(SMALL_REPLICATED)
    small_full_shapes = [P[n].shape for n in small_names]
    small_pack = _pack([Gs[n] for n in small_names]).astype(MMT)
    ((ag4,),) = _sequencer_exchange("gather_small_grads", SMALL_GRADS_ID, [
        ([small_pack], [SDS((N_DEV,) + small_pack.shape, MMT)], _all_gather_plan, N_DEV - 1)], _all_peers)

    flat2 = lambda a: a.reshape(-1, a.shape[-1])
    res = {}
    outs = _adamw(flat2(W["ada_w"]), flat2(M["ada_w"]), flat2(V["ada_w"]), [flat2(g_ada_w)], "adamw_ada_w")
    res["ada_w"] = tuple(o.reshape(W["ada_w"].shape) for o in outs)

    land("l0_ffn", outs[0])
    land("l0_mix", so_far["ffn_w_up"][0])
    for n in BIG:
        res[n] = tuple(so_far[n])

    small_sum = _unpack(_sum_slots(ag4, "sum_small_grads"), small_full_shapes)
    g_small = {}
    for n, g in zip(small_names, small_sum):
        g_small[n] = _my_block_last(g, chip) if n in SMALL_SHARDED else g
    packed = [_pack([d[n] for n in small_names]) for d in (W, M, V)]
    outs_small = _adamw(packed[0], packed[1], packed[2], [_pack([g_small[n] for n in small_names])], "adamw_small")
    shard_shapes = [W[n].shape for n in small_names]
    for k, n in enumerate(small_names):
        res[n] = tuple(_unpack(o, shard_shapes)[k] for o in outs_small)

    loss = lax.psum(loss_mine[0, 0], ("x", "y", "c"))
    return (loss, grad_x) + tuple(res[n][k] for k in range(4) for n in WEIGHTS)


def kernel(x, c, ctx, c_ctx, ada_w, ada_b, norm_g, ffn_w_up, ffn_conv_w, ffn_conv_b, ffn_w_down, cm_w_in, cm_b_in, cm_dw_w, cm_dw_b, cm_ln_g, cm_ln_b, cm_w_out, cm_b_out, attn_w_qkv, attn_sink, attn_w_o, gm_w_in, gm_b_in, gm_ln_g, gm_ln_b, gm_w_s, gm_b_s, gm_w_out, loss_target, m_c_ctx, m_ada_w, m_ada_b, m_norm_g, m_ffn_w_up, m_ffn_conv_w, m_ffn_conv_b, m_ffn_w_down, m_cm_w_in, m_cm_b_in, m_cm_dw_w, m_cm_dw_b, m_cm_ln_g, m_cm_ln_b, m_cm_w_out, m_cm_b_out, m_attn_w_qkv, m_attn_sink, m_attn_w_o, m_gm_w_in, m_gm_b_in, m_gm_ln_g, m_gm_ln_b, m_gm_w_s, m_gm_b_s, m_gm_w_out, v_c_ctx, v_ada_w, v_ada_b, v_norm_g, v_ffn_w_up, v_ffn_conv_w, v_ffn_conv_b, v_ffn_w_down, v_cm_w_in, v_cm_b_in, v_cm_dw_w, v_cm_dw_b, v_cm_ln_g, v_cm_ln_b, v_cm_w_out, v_cm_b_out, v_attn_w_qkv, v_attn_sink, v_attn_w_o, v_gm_w_in, v_gm_b_in, v_gm_ln_g, v_gm_ln_b, v_gm_w_s, v_gm_b_s, v_gm_w_out):
    args = locals()
    W = {n: args[n] for n in WEIGHTS}
    M = {n: args["m_" + n] for n in WEIGHTS}
    V = {n: args["v_" + n] for n in WEIGHTS}
    return _step(x, c, ctx, loss_target, W, M, V)
```

```python
import functools
import math

import jax
import jax.numpy as jnp
from jax import lax
from jax.experimental import pallas as pl
from jax.experimental.pallas import tpu as pltpu
from jax.experimental.pallas import tpu_sc as plsc

F32 = jnp.float32
MMT = jnp.bfloat16
SDS = jax.ShapeDtypeStruct
MESH = pl.DeviceIdType.MESH

EPS = 1e-6
HEAD_DIM = 64
Q_PER_KV = 4
ATTN_BLOCK = 128
GRID_W = 64
ROPE_BASE = 10000.0
GMLP_CHUNK = 128
GMLP_GROUP_DIM = 128
CONV_WIDTH = 31
FFN_CONV_WIDTH = 3
NEG = -1e30

ADAM_LR, ADAM_B1, ADAM_B2, ADAM_EPS, ADAM_WD, ADAM_STEP = 0.001, 0.9, 0.999, 1e-08, 0.01, 10

LANES = 128
SUBLANES = 8
VMEM_LIMIT = 52 * 1024 * 1024
CONV_ROWS = 128
N_CHIPS = 4
N_DEV = 8
N_SEQUENCERS = 2
FETCH_IDS = (1, 2, 3, 4, 11)
SEND_IDS = (5, 6, 7, 8, 9)
SMALL_GRADS_ID = 10


def _cparams(*sem):
    return pltpu.CompilerParams(dimension_semantics=sem if sem else None, vmem_limit_bytes=VMEM_LIMIT)


def _tile(n, cap, mult=LANES):
    best = None
    for d in range(mult, min(n, cap) + 1, mult):
        if n % d == 0:
            best = d
    return best if best is not None else n


def _sum0(v):
    return jnp.sum(v, axis=0, keepdims=True)


def _rms(v):
    r = lax.rsqrt(jnp.mean(v * v, axis=-1, keepdims=True) + EPS)
    return v * r, r


def _sig(v):
    return jax.nn.sigmoid(v)


def _dot(a, b, ca, cb):
    return lax.dot_general(a.astype(MMT), b.astype(MMT), (((ca,), (cb,)), ((), ())), preferred_element_type=F32)


def _mm(a, b, mode, out_dtype, name):
    if mode == "nn":
        (M, K), N = a.shape, b.shape[1]
    elif mode == "nt":
        (M, K), N = a.shape, b.shape[0]
    else:
        (K, M), N = a.shape, b.shape[1]
    tm, tn, tk = _tile(M, 512), _tile(N, 1408), _tile(K, 1536)
    nk = K // tk
    ca, cb = {"nn": (1, 0), "nt": (1, 1), "tn": (0, 0)}[mode]

    def body(a_ref, b_ref, o_ref, acc):
        k = pl.program_id(2)

        @pl.when(k == 0)
        def _():
            acc[...] = jnp.zeros_like(acc)

        acc[...] += _dot(a_ref[...], b_ref[...], ca, cb)

        @pl.when(k == nk - 1)
        def _():
            o_ref[...] = acc[...].astype(o_ref.dtype)

    a_spec = pl.BlockSpec((tk, tm), lambda i, j, k: (k, i)) if mode == "tn" else pl.BlockSpec((tm, tk), lambda i, j, k: (i, k))
    b_spec = pl.BlockSpec((tn, tk), lambda i, j, k: (j, k)) if mode == "nt" else pl.BlockSpec((tk, tn), lambda i, j, k: (k, j))
    return pl.pallas_call(
        body, name=name, grid=(M // tm, N // tn, nk), in_specs=[a_spec, b_spec],
        out_specs=pl.BlockSpec((tm, tn), lambda i, j, k: (i, j)), out_shape=SDS((M, N), out_dtype),
        scratch_shapes=[pltpu.VMEM((tm, tn), F32)], compiler_params=_cparams("parallel", "parallel", "arbitrary"))(a, b)


def _seg_of(nl, nseg):
    return (lambda i: jnp.where(i >= nl, 1, 0)) if nseg == 2 else (lambda i: 0)


def _prenorm(h, mod, gn, which, rows, nl, tm, name):
    D = h.shape[1]
    nseg = mod.shape[0]
    seg = _seg_of(nl, nseg)
    sh_i, sc_i = (0, 1) if which == 0 else (3, 4)

    def body(h_ref, mod_ref, gn_ref, a_ref):
        n, _ = _rms(h_ref[...])
        a_ref[...] = (n * gn_ref[...] * (1.0 + mod_ref[pl.ds(sc_i, 1), :]) + mod_ref[pl.ds(sh_i, 1), :]).astype(a_ref.dtype)

    return pl.pallas_call(
        body, name=name, grid=(rows // tm,),
        in_specs=[pl.BlockSpec((tm, D), lambda i: (i, 0)), pl.BlockSpec((None, 6, D), lambda i: (seg(i), 0, 0)),
                  pl.BlockSpec((1, D), lambda i: (0, 0))],
        out_specs=pl.BlockSpec((tm, D), lambda i: (i, 0)), out_shape=SDS((rows, D), MMT),
        compiler_params=_cparams("parallel"))(h, mod, gn)


def _acc_spec(D, seg):
    return pl.BlockSpec((None, 1, D), lambda i: (seg(i), 0, 0))


def _prenorm_bwd(h, da, dh_in, mod, gn, which, rows, nl, tm, name):
    D = h.shape[1]
    nseg = mod.shape[0]
    seg = _seg_of(nl, nseg)
    sc_i = 1 if which == 0 else 4

    def body(h_ref, da_ref, dhin_ref, mod_ref, gn_ref, dh_ref, dsh_ref, dsc_ref, dgn_ref):
        i = pl.program_id(0)
        first = (i == 0) | (i == nl) if nseg == 2 else (i == 0)

        @pl.when(first)
        def _():
            dsh_ref[...] = jnp.zeros_like(dsh_ref)
            dsc_ref[...] = jnp.zeros_like(dsc_ref)
            dgn_ref[...] = jnp.zeros_like(dgn_ref)

        n, r = _rms(h_ref[...])
        da_v = da_ref[...].astype(F32)
        gn_v = gn_ref[...]
        sc1 = 1.0 + mod_ref[pl.ds(sc_i, 1), :]
        dsh_ref[...] += _sum0(da_v)
        dsc_ref[...] += _sum0(da_v * (n * gn_v))
        dgn_ref[...] += _sum0(da_v * n * sc1)
        dn = da_v * (gn_v * sc1)
        dh_ref[...] = dhin_ref[...] + r * (dn - n * jnp.mean(dn * n, axis=-1, keepdims=True))

    row = pl.BlockSpec((tm, D), lambda i: (i, 0))
    acc = SDS((nseg, 1, D), F32)
    return pl.pallas_call(
        body, name=name, grid=(rows // tm,),
        in_specs=[row, row, row, pl.BlockSpec((None, 6, D), lambda i: (seg(i), 0, 0)), pl.BlockSpec((1, D), lambda i: (0, 0))],
        out_specs=[row, _acc_spec(D, seg), _acc_spec(D, seg), _acc_spec(D, seg)],
        out_shape=[SDS((rows, D), F32), acc, acc, acc], compiler_params=_cparams("arbitrary"))(h, da, dh_in, mod, gn)


def _postnorm(h, y, bias, mod, gn, gate_i, rows, nl, tm, name):
    D = h.shape[1]
    nseg = mod.shape[0]
    seg = _seg_of(nl, nseg)

    def body(h_ref, y_ref, b_ref, mod_ref, gn_ref, o_ref):
        ny, _ = _rms(y_ref[...] + b_ref[...])
        o_ref[...] = h_ref[...] + mod_ref[pl.ds(gate_i, 1), :] * (ny * gn_ref[...])

    row = pl.BlockSpec((tm, D), lambda i: (i, 0))
    vec = pl.BlockSpec((1, D), lambda i: (0, 0))
    return pl.pallas_call(
        body, name=name, grid=(rows // tm,),
        in_specs=[row, row, vec, pl.BlockSpec((None, 6, D), lambda i: (seg(i), 0, 0)), vec],
        out_specs=row, out_shape=SDS((rows, D), F32), compiler_params=_cparams("parallel"))(h, y, bias, mod, gn)


def _postnorm_bwd(dh, y, bias, mod, gn, gate_i, rows, nl, tm, name):
    D = y.shape[1]
    nseg = mod.shape[0]
    seg = _seg_of(nl, nseg)

    def body(dh_ref, y_ref, b_ref, mod_ref, gn_ref, dy_ref, dg_ref, dgn_ref, db_ref):
        i = pl.program_id(0)
        first = (i == 0) | (i == nl) if nseg == 2 else (i == 0)

        @pl.when(first)
        def _():
            dg_ref[...] = jnp.zeros_like(dg_ref)
            dgn_ref[...] = jnp.zeros_like(dgn_ref)
            db_ref[...] = jnp.zeros_like(db_ref)

        ny, ry = _rms(y_ref[...] + b_ref[...])
        g = mod_ref[pl.ds(gate_i, 1), :]
        gn_v = gn_ref[...]
        dh_v = dh_ref[...]
        dg_ref[...] += _sum0(dh_v * (ny * gn_v))
        dgn_ref[...] += _sum0(dh_v * ny * g)
        dny = dh_v * (g * gn_v)
        dy = ry * (dny - ny * jnp.mean(dny * ny, axis=-1, keepdims=True))
        db_ref[...] += _sum0(dy)
        dy_ref[...] = dy.astype(dy_ref.dtype)

    row = pl.BlockSpec((tm, D), lambda i: (i, 0))
    vec = pl.BlockSpec((1, D), lambda i: (0, 0))
    acc = SDS((nseg, 1, D), F32)
    return pl.pallas_call(
        body, name=name, grid=(rows // tm,),
        in_specs=[row, row, vec, pl.BlockSpec((None, 6, D), lambda i: (seg(i), 0, 0)), vec],
        out_specs=[row, _acc_spec(D, seg), _acc_spec(D, seg), _acc_spec(D, seg)],
        out_shape=[SDS((rows, D), MMT), acc, acc, acc], compiler_params=_cparams("arbitrary"))(dh, y, bias, mod, gn)


def _seg_layout(segs, H):
    out, base = [], H
    for s0, n in segs:
        out.append((s0, n, base))
        base += n + H
    return out, base


def _zero_pads(ref, lay, H):
    width = ref.shape[1]
    ref[pl.ds(0, H), :] = jnp.zeros((H, width), ref.dtype)
    for _, n, base in lay:
        ref[pl.ds(base + n, H), :] = jnp.zeros((H, width), ref.dtype)


def _window(ref, base, off, H):
    return ref[pl.ds(base - H + off, CONV_ROWS + 2 * H), :]


def _taps(win, H, offs):
    W = CONV_ROWS + 2 * H
    rolled, out = {}, {}
    for o in offs:
        s = H + o
        b = s % SUBLANES
        if b not in rolled:
            rolled[b] = win if b == 0 else pltpu.roll(win, shift=W - b, axis=0)
        out[o] = rolled[b][s - b:s - b + CONV_ROWS, :]
    return out


def _chunks(lay, fn):
    for s0, n, base in lay:
        def step(r, carry, s0=s0, base=base):
            fn(s0, base, pl.multiple_of(r * CONV_ROWS, CONV_ROWS))
            return carry
        lax.fori_loop(0, n // CONV_ROWS, step, 0)


def _ffn_gate(z0, conv_w, conv_b, segs, name):
    T, F2 = z0.shape
    F = F2 // 2
    tc = _tile(F, 256)
    nF = F // tc
    H = SUBLANES
    lay, srows = _seg_layout(segs, H)
    offs = [-1, 0, 1]

    def body(zg_ref, zv_ref, wg_ref, wv_ref, bg_ref, bv_ref, u_ref, xg, xv):
        _zero_pads(xg, lay, H)
        _zero_pads(xv, lay, H)
        for s0, n, base in lay:
            xg[pl.ds(base, n), :] = zg_ref[pl.ds(s0, n), :]
            xv[pl.ds(base, n), :] = zv_ref[pl.ds(s0, n), :]

        def chunk(s0, base, off):
            tg = _taps(_window(xg, base, off, H), H, offs)
            tv = _taps(_window(xv, base, off, H), H, offs)
            zg = bg_ref[...] + sum(tg[k - 1] * wg_ref[pl.ds(k, 1), :] for k in range(3))
            zv = bv_ref[...] + sum(tv[k - 1] * wv_ref[pl.ds(k, 1), :] for k in range(3))
            u_ref[pl.ds(s0 + off, CONV_ROWS), :] = (zg * _sig(zg) * zv).astype(u_ref.dtype)

        _chunks(lay, chunk)

    colg = lambda r: pl.BlockSpec((r, tc), lambda j: (0, j))
    colv = lambda r: pl.BlockSpec((r, tc), lambda j: (0, j + nF))
    return pl.pallas_call(
        body, name=name, grid=(nF,),
        in_specs=[colg(T), colv(T), colg(3), colv(3), colg(1), colv(1)],
        out_specs=colg(T), out_shape=SDS((T, F), MMT),
        scratch_shapes=[pltpu.VMEM((srows, tc), F32), pltpu.VMEM((srows, tc), F32)],
        compiler_params=_cparams("parallel"))(z0, z0, conv_w, conv_w, conv_b, conv_b)


def _ffn_gate_bwd(z0, du, conv_w, conv_b, segs, name):
    T, F2 = z0.shape
    F = F2 // 2
    tc = _tile(F, 256)
    nF = F // tc
    H = SUBLANES
    lay, srows = _seg_layout(segs, H)
    offs = [-1, 0, 1]

    def body(zo_ref, zt_ref, du_ref, wo_ref, wt_ref, bo_ref, bt_ref, u_ref, dz0_ref, dw_ref, db_ref, xo, xt, dzp):
        own_is_gate = pl.program_id(1) == 0
        for ref in (xo, xt, dzp):
            _zero_pads(ref, lay, H)
        for s0, n, base in lay:
            xo[pl.ds(base, n), :] = zo_ref[pl.ds(s0, n), :]
            xt[pl.ds(base, n), :] = zt_ref[pl.ds(s0, n), :]

        def grads(s0, base, off):
            to = _taps(_window(xo, base, off, H), H, offs)
            tt = _taps(_window(xt, base, off, H), H, offs)
            zo = bo_ref[...] + sum(to[k - 1] * wo_ref[pl.ds(k, 1), :] for k in range(3))
            zt = bt_ref[...] + sum(tt[k - 1] * wt_ref[pl.ds(k, 1), :] for k in range(3))
            so, st = _sig(zo), _sig(zt)
            du_v = du_ref[pl.ds(s0 + off, CONV_ROWS), :]
            d_gate = du_v * zt * (so * (1.0 + zo * (1.0 - so)))
            d_val = du_v * (zt * st)
            dzp[pl.ds(base + off, CONV_ROWS), :] = jnp.where(own_is_gate, d_gate, d_val)

            @pl.when(own_is_gate)
            def _():
                u_ref[pl.ds(s0 + off, CONV_ROWS), :] = (zo * so * zt).astype(u_ref.dtype)

        _chunks(lay, grads)
        dw_ref[...] = jnp.zeros_like(dw_ref)
        db_ref[...] = jnp.zeros_like(db_ref)

        def back(s0, base, off):
            td = _taps(_window(dzp, base, off, H), H, offs)
            tx = _taps(_window(xo, base, off, H), H, offs)
            dz0 = sum(td[1 - k] * wo_ref[pl.ds(k, 1), :] for k in range(3))
            dz0_ref[pl.ds(s0 + off, CONV_ROWS), :] = dz0.astype(dz0_ref.dtype)
            db_ref[...] += _sum0(td[0])
            for k in range(3):
                dw_ref[pl.ds(k, 1), :] += _sum0(td[0] * tx[k - 1])

        _chunks(lay, back)

    own = lambda r: pl.BlockSpec((r, tc), lambda j, hf: (0, hf * nF + j))
    oth = lambda r: pl.BlockSpec((r, tc), lambda j, hf: (0, (1 - hf) * nF + j))
    ucol = pl.BlockSpec((T, tc), lambda j, hf: (0, j))
    return pl.pallas_call(
        body, name=name, grid=(nF, 2),
        in_specs=[own(T), oth(T), ucol, own(3), oth(3), own(1), oth(1)],
        out_specs=[ucol, own(T), own(3), own(1)],
        out_shape=[SDS((T, F), MMT), SDS((T, F2), MMT), SDS((3, F2), F32), SDS((1, F2), F32)],
        scratch_shapes=[pltpu.VMEM((srows, tc), F32)] * 3,
        compiler_params=_cparams("parallel", "arbitrary"))(z0, z0, du, conv_w, conv_w, conv_b, conv_b)


def _glu_conv(p0, b_in, dw_w, dw_b, segs, name):
    T, D2 = p0.shape
    D = D2 // 2
    tc = _tile(D, 256)
    nD = D // tc
    H = 2 * SUBLANES
    half = (CONV_WIDTH - 1) // 2
    lay, srows = _seg_layout(segs, H)
    offs = list(range(-half, half + 1))

    def body(pa_ref, pg_ref, ba_ref, bg_ref, w_ref, b_ref, z2_ref, z1p):
        _zero_pads(z1p, lay, H)

        def glu(s0, base, off):
            rows = pl.ds(s0 + off, CONV_ROWS)
            z1p[pl.ds(base + off, CONV_ROWS), :] = (pa_ref[rows, :] + ba_ref[...]) * _sig(pg_ref[rows, :] + bg_ref[...])

        _chunks(lay, glu)

        def conv(s0, base, off):
            t = _taps(_window(z1p, base, off, H), H, offs)
            acc = b_ref[...] + t[-half] * w_ref[pl.ds(0, 1), :]
            for k in range(1, CONV_WIDTH):
                acc = acc + t[k - half] * w_ref[pl.ds(k, 1), :]
            z2_ref[pl.ds(s0 + off, CONV_ROWS), :] = acc

        _chunks(lay, conv)

    cola = lambda r: pl.BlockSpec((r, tc), lambda j: (0, j))
    colg = lambda r: pl.BlockSpec((r, tc), lambda j: (0, j + nD))
    return pl.pallas_call(
        body, name=name, grid=(nD,),
        in_specs=[cola(T), colg(T), cola(1), colg(1), cola(CONV_WIDTH), cola(1)],
        out_specs=cola(T), out_shape=SDS((T, D), F32), scratch_shapes=[pltpu.VMEM((srows, tc), F32)],
        compiler_params=_cparams("parallel"))(p0, p0, b_in, b_in, dw_w, dw_b)


def _glu_conv_bwd(p0, b_in, dw_w, dz2, segs, name):
    T, D2 = p0.shape
    D = D2 // 2
    tc = _tile(D, 256)
    nD = D // tc
    H = 2 * SUBLANES
    half = (CONV_WIDTH - 1) // 2
    lay, srows = _seg_layout(segs, H)
    offs = list(range(-half, half + 1))

    def body(pa_ref, pg_ref, ba_ref, bg_ref, w_ref, dz2_ref, dpa_ref, dpg_ref, dw_ref, db_ref, dba_ref, dbg_ref, z1p, dzp):
        _zero_pads(z1p, lay, H)
        _zero_pads(dzp, lay, H)
        for s0, n, base in lay:
            dzp[pl.ds(base, n), :] = dz2_ref[pl.ds(s0, n), :]

        def glu(s0, base, off):
            rows = pl.ds(s0 + off, CONV_ROWS)
            z1p[pl.ds(base + off, CONV_ROWS), :] = (pa_ref[rows, :] + ba_ref[...]) * _sig(pg_ref[rows, :] + bg_ref[...])

        _chunks(lay, glu)
        for ref in (dw_ref, db_ref, dba_ref, dbg_ref):
            ref[...] = jnp.zeros_like(ref)

        def back(s0, base, off):
            td = _taps(_window(dzp, base, off, H), H, offs)
            tz = _taps(_window(z1p, base, off, H), H, offs)
            dz1 = td[half] * w_ref[pl.ds(0, 1), :]
            for k in range(1, CONV_WIDTH):
                dz1 = dz1 + td[half - k] * w_ref[pl.ds(k, 1), :]
            db_ref[...] += _sum0(td[0])
            for k in range(CONV_WIDTH):
                dw_ref[pl.ds(k, 1), :] += _sum0(td[0] * tz[k - half])
            rows = pl.ds(s0 + off, CONV_ROWS)
            pa = pa_ref[rows, :] + ba_ref[...]
            sg = _sig(pg_ref[rows, :] + bg_ref[...])
            dpa = dz1 * sg
            dpg = dz1 * pa * (sg * (1.0 - sg))
            dba_ref[...] += _sum0(dpa)
            dbg_ref[...] += _sum0(dpg)
            dpa_ref[rows, :] = dpa.astype(dpa_ref.dtype)
            dpg_ref[rows, :] = dpg.astype(dpg_ref.dtype)

        _chunks(lay, back)

    cola = lambda r: pl.BlockSpec((r, tc), lambda j: (0, j))
    colg = lambda r: pl.BlockSpec((r, tc), lambda j: (0, j + nD))
    return pl.pallas_call(
        body, name=name, grid=(nD,),
        in_specs=[cola(T), colg(T), cola(1), colg(1), cola(CONV_WIDTH), cola(T)],
        out_specs=[cola(T), cola(T), cola(CONV_WIDTH), cola(1), cola(1), cola(1)],
        out_shape=[SDS((T, D), MMT), SDS((T, D), MMT), SDS((CONV_WIDTH, D), F32), SDS((1, D), F32), SDS((1, D), F32),
                   SDS((1, D), F32)],
        scratch_shapes=[pltpu.VMEM((srows, tc), F32)] * 2, compiler_params=_cparams("parallel"))(p0, p0, b_in, b_in, dw_w, dz2)


def _layer_norm_stats(v):
    mu = jnp.mean(v, axis=-1, keepdims=True)
    var = jnp.mean(jnp.square(v - mu), axis=-1, keepdims=True)
    rstd = lax.rsqrt(var + EPS)
    return (v - mu) * rstd, rstd


def _ln_silu(z2, ln_g, ln_b, rows, tm, name):
    D = z2.shape[1]

    def body(z_ref, g_ref, b_ref, o_ref):
        xh, _ = _layer_norm_stats(z_ref[...])
        z3 = xh * g_ref[...] + b_ref[...]
        o_ref[...] = (z3 * _sig(z3)).astype(o_ref.dtype)

    row = pl.BlockSpec((tm, D), lambda i: (i, 0))
    vec = pl.BlockSpec((1, D), lambda i: (0, 0))
    return pl.pallas_call(body, name=name, grid=(rows // tm,), in_specs=[row, vec, vec], out_specs=row,
                          out_shape=SDS((rows, D), MMT), compiler_params=_cparams("parallel"))(z2, ln_g, ln_b)


def _ln_silu_bwd(z2, dz4, ln_g, ln_b, rows, tm, name):
    D = z2.shape[1]

    def body(z_ref, d_ref, g_ref, b_ref, dz_ref, dg_ref, db_ref):
        @pl.when(pl.program_id(0) == 0)
        def _():
            dg_ref[...] = jnp.zeros_like(dg_ref)
            db_ref[...] = jnp.zeros_like(db_ref)

        xh, rstd = _layer_norm_stats(z_ref[...])
        z3 = xh * g_ref[...] + b_ref[...]
        s = _sig(z3)
        dz3 = d_ref[...] * (s * (1.0 + z3 * (1.0 - s)))
        dg_ref[...] += _sum0(dz3 * xh)
        db_ref[...] += _sum0(dz3)
        dxh = dz3 * g_ref[...]
        dz_ref[...] = rstd * (dxh - jnp.mean(dxh, axis=-1, keepdims=True) - xh * jnp.mean(dxh * xh, axis=-1, keepdims=True))

    row = pl.BlockSpec((tm, D), lambda i: (i, 0))
    vec = pl.BlockSpec((1, D), lambda i: (0, 0))
    return pl.pallas_call(body, name=name, grid=(rows // tm,), in_specs=[row, row, vec, vec], out_specs=[row, vec, vec],
                          out_shape=[SDS((rows, D), F32), SDS((1, D), F32), SDS((1, D), F32)],
                          compiler_params=_cparams("arbitrary"))(z2, dz4, ln_g, ln_b)


def _rot_half_pairs(v):
    width = v.shape[1]
    lane = lax.broadcasted_iota(jnp.int32, v.shape, 1)
    return jnp.where((lane % 32) < 16, -pltpu.roll(v, shift=width - 16, axis=1), pltpu.roll(v, shift=16, axis=1))


def _rope(qkv, cos, sin, L, qk, tm, name):
    width = qkv.shape[1]
    kv = width - qk

    def body(x_ref, c_ref, s_ref, qk_ref, v_ref):
        xv = x_ref[:, pl.ds(0, qk)]
        c = jnp.tile(c_ref[...], (1, qk // LANES))
        s = jnp.tile(s_ref[...], (1, qk // LANES))
        qk_ref[...] = (xv * c + _rot_half_pairs(xv) * s).astype(qk_ref.dtype)
        v_ref[...] = x_ref[:, pl.ds(qk, kv)].astype(v_ref.dtype)

    tab = pl.BlockSpec((tm, LANES), lambda i: (i, 0))
    return pl.pallas_call(
        body, name=name, grid=(L // tm,), in_specs=[pl.BlockSpec((tm, width), lambda i: (i, 0)), tab, tab],
        out_specs=[pl.BlockSpec((tm, qk), lambda i: (i, 0)), pl.BlockSpec((tm, kv), lambda i: (i, 0))],
        out_shape=[SDS((L, qk), MMT), SDS((L, kv), MMT)], compiler_params=_cparams("parallel"))(qkv, cos, sin)


def _rope_bwd(dqk, dv, cos, sin, tm, name):
    L, qk = dqk.shape
    kv = dv.shape[1]

    def body(d_ref, dv_ref, c_ref, s_ref, o_ref):
        dv_ = d_ref[...]
        c = jnp.tile(c_ref[...], (1, qk // LANES))
        s = jnp.tile(s_ref[...], (1, qk // LANES))
        o_ref[:, pl.ds(0, qk)] = (dv_ * c - _rot_half_pairs(dv_ * s)).astype(o_ref.dtype)
        o_ref[:, pl.ds(qk, kv)] = dv_ref[...].astype(o_ref.dtype)

    tab = pl.BlockSpec((tm, LANES), lambda i: (i, 0))
    return pl.pallas_call(
        body, name=name, grid=(L // tm,),
        in_specs=[pl.BlockSpec((tm, qk), lambda i: (i, 0)), pl.BlockSpec((tm, kv), lambda i: (i, 0)), tab, tab],
        out_specs=pl.BlockSpec((tm, qk + kv), lambda i: (i, 0)), out_shape=SDS((L, qk + kv), MMT),
        compiler_params=_cparams("parallel"))(dqk, dv, cos, sin)


def _band_specs(nb, width):
    blk = lambda f: pl.BlockSpec((None, ATTN_BLOCK, width), f)
    return [blk(lambda h, n: (h, jnp.maximum(n - 1, 0), 0)), blk(lambda h, n: (h, n, 0)),
            blk(lambda h, n: (h, jnp.minimum(n + 1, nb - 1), 0))]


def _window_mask(n, L):
    qi = lax.broadcasted_iota(jnp.int32, (ATTN_BLOCK, 3 * ATTN_BLOCK), 0)
    kk = lax.broadcasted_iota(jnp.int32, (ATTN_BLOCK, 3 * ATTN_BLOCK), 1)
    key_abs = (n - 1) * ATTN_BLOCK + kk
    return (jnp.abs(qi + ATTN_BLOCK - kk) <= ATTN_BLOCK) & (key_abs >= 0) & (key_abs < L)


def _attn_fwd(q, k, v, kc, vc, sink, name):
    nkv, _, L, hd = q.shape
    C = kc.shape[1]
    nb = L // ATTN_BLOCK
    scale = HEAD_DIM ** -0.5

    def body(sink_ref, q_ref, k0, k1, k2, v0, v1, v2, kc_ref, vc_ref, o_ref, lse_ref):
        hh, n = pl.program_id(0), pl.program_id(1)
        kw = jnp.concatenate([k0[...], k1[...], k2[...]], axis=0)
        vw = jnp.concatenate([v0[...], v1[...], v2[...]], axis=0)
        mask = _window_mask(n, L)
        for g in range(Q_PER_KV):
            qg = q_ref[g]
            sw = jnp.where(mask, _dot(qg, kw, 1, 1) * scale, NEG)
            sc = _dot(qg, kc_ref[...], 1, 1) * scale
            sk = sink_ref[hh * Q_PER_KV + g]
            m = jnp.maximum(jnp.maximum(jnp.max(sw, axis=-1, keepdims=True), jnp.max(sc, axis=-1, keepdims=True)), sk)
            pw, pc = jnp.exp(sw - m), jnp.exp(sc - m)
            den = jnp.sum(pw, axis=-1, keepdims=True) + jnp.sum(pc, axis=-1, keepdims=True) + jnp.exp(sk - m)
            inv = 1.0 / den
            o_ref[g] = _dot(pw * inv, vw, 1, 0) + _dot(pc * inv, vc_ref[...], 1, 0)
            lse_ref[g] = m + jnp.log(den)

    qspec = pl.BlockSpec((None, Q_PER_KV, ATTN_BLOCK, hd), lambda h, n: (h, 0, n, 0))
    cspec = pl.BlockSpec((None, C, hd), lambda h, n: (h, 0, 0))
    return pl.pallas_call(
        body, name=name, grid=(nkv, nb),
        in_specs=[pl.BlockSpec(memory_space=pltpu.SMEM), qspec] + _band_specs(nb, hd) + _band_specs(nb, hd) + [cspec, cspec],
        out_specs=[qspec, pl.BlockSpec((None, Q_PER_KV, ATTN_BLOCK, 1), lambda h, n: (h, 0, n, 0))],
        out_shape=[SDS((nkv, Q_PER_KV, L, hd), F32), SDS((nkv, Q_PER_KV, L, 1), F32)],
        compiler_params=_cparams("parallel", "parallel"))(sink, q, k, k, k, v, v, v, kc, vc)


def _attn_bwd_q(q, k, v, kc, vc, sink, o, do, lse, name):
    nkv, _, L, hd = q.shape
    C = kc.shape[1]
    nb = L // ATTN_BLOCK
    scale = HEAD_DIM ** -0.5

    def body(sink_ref, q_ref, k0, k1, k2, v0, v1, v2, kc_ref, vc_ref, o_ref, do_ref, lse_ref, dq_ref, dkc_ref, dvc_ref, dsk_ref):
        hh, n = pl.program_id(0), pl.program_id(1)

        @pl.when(n == 0)
        def _():
            dkc_ref[...] = jnp.zeros_like(dkc_ref)
            dvc_ref[...] = jnp.zeros_like(dvc_ref)
            dsk_ref[...] = jnp.zeros_like(dsk_ref)

        kw = jnp.concatenate([k0[...], k1[...], k2[...]], axis=0)
        vw = jnp.concatenate([v0[...], v1[...], v2[...]], axis=0)
        mask = _window_mask(n, L)
        for g in range(Q_PER_KV):
            qg, dog, lse_g = q_ref[g], do_ref[g], lse_ref[g]
            delta = jnp.sum(dog.astype(F32) * o_ref[g], axis=-1, keepdims=True)
            pw = jnp.exp(jnp.where(mask, _dot(qg, kw, 1, 1) * scale, NEG) - lse_g)
            pc = jnp.exp(_dot(qg, kc_ref[...], 1, 1) * scale - lse_g)
            dsw = pw * (_dot(dog, vw, 1, 1) - delta)
            dsc = pc * (_dot(dog, vc_ref[...], 1, 1) - delta)
            dq_ref[g] = (_dot(dsw, kw, 1, 0) + _dot(dsc, kc_ref[...], 1, 0)) * scale
            dkc_ref[...] += _dot(dsc, qg, 0, 0) * scale
            dvc_ref[...] += _dot(pc, dog, 0, 0)
            psk = jnp.exp(sink_ref[hh * Q_PER_KV + g] - lse_g)
            dsk_ref[pl.ds(g, 1), :] += jnp.broadcast_to(jnp.sum(-psk * delta, axis=0, keepdims=True), (1, LANES))

    qspec = pl.BlockSpec((None, Q_PER_KV, ATTN_BLOCK, hd), lambda h, n: (h, 0, n, 0))
    lspec = pl.BlockSpec((None, Q_PER_KV, ATTN_BLOCK, 1), lambda h, n: (h, 0, n, 0))
    cspec = pl.BlockSpec((None, C, hd), lambda h, n: (h, 0, 0))
    return pl.pallas_call(
        body, name=name, grid=(nkv, nb),
        in_specs=[pl.BlockSpec(memory_space=pltpu.SMEM), qspec] + _band_specs(nb, hd) + _band_specs(nb, hd)
        + [cspec, cspec, qspec, qspec, lspec],
        out_specs=[qspec, cspec, cspec, pl.BlockSpec((None, SUBLANES, LANES), lambda h, n: (h, 0, 0))],
        out_shape=[SDS((nkv, Q_PER_KV, L, hd), F32), SDS((nkv, C, hd), F32), SDS((nkv, C, hd), F32),
                   SDS((nkv, SUBLANES, LANES), F32)],
        compiler_params=_cparams("parallel", "arbitrary"))(sink, q, k, k, k, v, v, v, kc, vc, o, do, lse)


def _attn_bwd_kv(q, k, v, o, do, lse, name):
    nkv, _, L, hd = q.shape
    nb = L // ATTN_BLOCK
    scale = HEAD_DIM ** -0.5

    def body(q0, q1, q2, do0, do1, do2, o0, o1, o2, l0, l1, l2, k_ref, v_ref, dk_ref, dv_ref):
        j = pl.program_id(1)
        qi = lax.broadcasted_iota(jnp.int32, (ATTN_BLOCK, ATTN_BLOCK), 0)
        kk = lax.broadcasted_iota(jnp.int32, (ATTN_BLOCK, ATTN_BLOCK), 1)
        kj, vj = k_ref[...], v_ref[...]
        dk = jnp.zeros((ATTN_BLOCK, hd), F32)
        dv = jnp.zeros((ATTN_BLOCK, hd), F32)
        for slot, (q_r, do_r, o_r, l_r) in enumerate(((q0, do0, o0, l0), (q1, do1, o1, l1), (q2, do2, o2, l2))):
            n = j - 1 + slot
            ok = (n >= 0) & (n < nb) & (jnp.abs(qi + ATTN_BLOCK - ((2 - slot) * ATTN_BLOCK + kk)) <= ATTN_BLOCK)
            for g in range(Q_PER_KV):
                qg, dog = q_r[g], do_r[g]
                delta = jnp.sum(dog.astype(F32) * o_r[g], axis=-1, keepdims=True)
                p = jnp.exp(jnp.where(ok, _dot(qg, kj, 1, 1) * scale - l_r[g], NEG))
                ds = p * (_dot(dog, vj, 1, 1) - delta)
                dk = dk + _dot(ds, qg, 0, 0) * scale
                dv = dv + _dot(p, dog, 0, 0)
        dk_ref[...] = dk
        dv_ref[...] = dv

    def band(width):
        blk = lambda f: pl.BlockSpec((None, Q_PER_KV, ATTN_BLOCK, width), f)
        return [blk(lambda h, j: (h, 0, jnp.maximum(j - 1, 0), 0)), blk(lambda h, j: (h, 0, j, 0)),
                blk(lambda h, j: (h, 0, jnp.minimum(j + 1, nb - 1), 0))]

    kspec = pl.BlockSpec((None, ATTN_BLOCK, hd), lambda h, j: (h, j, 0))
    return pl.pallas_call(
        body, name=name, grid=(nkv, nb), in_specs=band(hd) + band(hd) + band(hd) + band(1) + [kspec, kspec],
        out_specs=[kspec, kspec], out_shape=[SDS((nkv, L, hd), F32), SDS((nkv, L, hd), F32)],
        compiler_params=_cparams("parallel", "parallel"))(q, q, q, do, do, do, o, o, o, lse, lse, lse, k, v)


_GELU_K = math.sqrt(2.0 / math.pi)


def _gelu(v):
    return 0.5 * v * (1.0 + jnp.tanh(_GELU_K * (v + 0.044715 * (v * v * v))))


def _gelu_grad(v):
    t = jnp.tanh(_GELU_K * (v + 0.044715 * (v * v * v)))
    return 0.5 * (1.0 + t) + 0.5 * v * (1.0 - t * t) * (_GELU_K * (1.0 + 3.0 * 0.044715 * (v * v)))


def _gmlp_fwd(p0, b_in, ln_g, ln_b, w_s, b_s, name):
    L, W2 = p0.shape
    W = W2 // 2
    G = W // GMLP_GROUP_DIM

    def body(p_ref, bi_ref, g_ref, b_ref, ws_ref, bs_ref, o_ref):
        ge = _gelu(p_ref[...] + bi_ref[...])
        xh, _ = _layer_norm_stats(ge[:, W:])
        vln = xh * g_ref[...] + b_ref[...]
        for gi in range(G):
            cols = slice(gi * GMLP_GROUP_DIM, (gi + 1) * GMLP_GROUP_DIM)
            s = _dot(ws_ref[gi], vln[:, cols], 1, 0) + bs_ref[gi]
            o_ref[:, cols] = (ge[:, cols] * s).astype(o_ref.dtype)

    full = lambda shape: pl.BlockSpec(shape, lambda i: (0,) * len(shape))
    return pl.pallas_call(
        body, name=name, grid=(L // GMLP_CHUNK,),
        in_specs=[pl.BlockSpec((GMLP_CHUNK, W2), lambda i: (i, 0)), full((1, W2)), full((1, W)), full((1, W)),
                  full((G, GMLP_CHUNK, GMLP_CHUNK)), full((G, GMLP_CHUNK, 1))],
        out_specs=pl.BlockSpec((GMLP_CHUNK, W), lambda i: (i, 0)), out_shape=SDS((L, W), MMT),
        compiler_params=_cparams("parallel"))(p0, b_in, ln_g, ln_b, w_s, b_s)


def _gmlp_bwd(p0, dus, b_in, ln_g, ln_b, w_s, w_st, b_s, name):
    L, W2 = p0.shape
    W = W2 // 2
    G = W // GMLP_GROUP_DIM

    def body(p_ref, d_ref, bi_ref, g_ref, b_ref, ws_ref, wst_ref, bs_ref, dpre_ref, dbi_ref, dg_ref, db_ref, dws_ref, dbs_ref, dvln):
        @pl.when(pl.program_id(0) == 0)
        def _():
            for ref in (dbi_ref, dg_ref, db_ref, dws_ref, dbs_ref):
                ref[...] = jnp.zeros_like(ref)

        pre = p_ref[...] + bi_ref[...]
        ge = _gelu(pre)
        xh, rstd = _layer_norm_stats(ge[:, W:])
        vln = xh * g_ref[...] + b_ref[...]
        dge_u = []
        for gi in range(G):
            cols = slice(gi * GMLP_GROUP_DIM, (gi + 1) * GMLP_GROUP_DIM)
            vg = vln[:, cols]
            s = _dot(ws_ref[gi], vg, 1, 0) + bs_ref[gi]
            dus_g = d_ref[:, cols]
            dge_u.append(dus_g * s)
            ds = dus_g * ge[:, cols]
            dbs_ref[gi] += jnp.sum(ds, axis=1, keepdims=True)
            dws_ref[gi] += _dot(ds, vg, 1, 1)
            dvln[:, cols] = _dot(wst_ref[gi], ds, 1, 0)
        dv = dvln[...]
        dg_ref[...] += _sum0(dv * xh)
        db_ref[...] += _sum0(dv)
        dxh = dv * g_ref[...]
        dv0 = rstd * (dxh - jnp.mean(dxh, axis=-1, keepdims=True) - xh * jnp.mean(dxh * xh, axis=-1, keepdims=True))
        dpre = jnp.concatenate(dge_u + [dv0], axis=1) * _gelu_grad(pre)
        dbi_ref[...] += _sum0(dpre)
        dpre_ref[...] = dpre.astype(dpre_ref.dtype)

    full = lambda shape: pl.BlockSpec(shape, lambda i: (0,) * len(shape))
    mats = (G, GMLP_CHUNK, GMLP_CHUNK)
    return pl.pallas_call(
        body, name=name, grid=(L // GMLP_CHUNK,),
        in_specs=[pl.BlockSpec((GMLP_CHUNK, W2), lambda i: (i, 0)), pl.BlockSpec((GMLP_CHUNK, W), lambda i: (i, 0)),
                  full((1, W2)), full((1, W)), full((1, W)), full(mats), full(mats), full((G, GMLP_CHUNK, 1))],
        out_specs=[pl.BlockSpec((GMLP_CHUNK, W2), lambda i: (i, 0)), full((1, W2)), full((1, W)), full((1, W)), full(mats),
                   full((G, GMLP_CHUNK, 1))],
        out_shape=[SDS((L, W2), MMT), SDS((1, W2), F32), SDS((1, W), F32), SDS((1, W), F32), SDS(mats, F32),
                   SDS((G, GMLP_CHUNK, 1), F32)],
        scratch_shapes=[pltpu.VMEM((GMLP_CHUNK, W), F32)], compiler_params=_cparams("arbitrary"))(
            p0, dus, b_in, ln_g, ln_b, w_s, w_st, b_s)


def _loss_head(h, target, tm, name):
    L, D = h.shape

    def body(h_ref, t_ref, l_ref, d_ref):
        @pl.when(pl.program_id(0) == 0)
        def _():
            l_ref[...] = jnp.zeros_like(l_ref)

        e = h_ref[...] - t_ref[...]
        l_ref[...] += 0.5 * jnp.sum(jnp.mean(e * e, axis=-1, keepdims=True), axis=0, keepdims=True)
        d_ref[...] = e * (1.0 / D)

    row = pl.BlockSpec((tm, D), lambda i: (i, 0))
    return pl.pallas_call(body, name=name, grid=(L // tm,), in_specs=[row, row],
                          out_specs=[pl.BlockSpec((1, 1), lambda i: (0, 0)), row],
                          out_shape=[SDS((1, 1), F32), SDS((L, D), F32)], compiler_params=_cparams("arbitrary"))(h, target)


def _ada_fwd(cond, ada_w, ada_b, name):
    NL, D, n = ada_w.shape
    tn = _tile(n, 768)

    def body(c_ref, w_ref, b_ref, o_ref):
        cv = c_ref[...]
        o_ref[...] = _dot(cv * _sig(cv), w_ref[...], 1, 0) + b_ref[...]

    return pl.pallas_call(
        body, name=name, grid=(NL, n // tn),
        in_specs=[pl.BlockSpec((2 * SUBLANES, D), lambda i, j: (0, 0)), pl.BlockSpec((None, D, tn), lambda i, j: (i, 0, j)),
                  pl.BlockSpec((None, 1, tn), lambda i, j: (i, 0, j))],
        out_specs=pl.BlockSpec((None, 2 * SUBLANES, tn), lambda i, j: (i, 0, j)), out_shape=SDS((NL, 2 * SUBLANES, n), F32),
        compiler_params=_cparams("parallel", "parallel"))(cond, ada_w, ada_b)


def _ada_bwd(cond, ada_w, dm_lat, dm_ctx, name):
    NL, D, n = ada_w.shape
    tn = _tile(n, 768)

    def body(c_ref, w_ref, dl_ref, dc_ref, dw_ref, ds_ref):
        @pl.when((pl.program_id(0) == 0) & (pl.program_id(1) == 0))
        def _():
            ds_ref[...] = jnp.zeros_like(ds_ref)

        cv = c_ref[...]
        row = lax.broadcasted_iota(jnp.int32, (SUBLANES, tn), 0)
        ctx_rows = jnp.where(row == 0, _sum0(dc_ref[...]), 0.0)
        dm = jnp.concatenate([dl_ref[...], ctx_rows], axis=0)
        dw_ref[...] = _dot(cv * _sig(cv), dm, 0, 0)
        ds_ref[...] += _dot(dm, w_ref[...], 1, 1)

    dspec = pl.BlockSpec((None, SUBLANES, tn), lambda i, j: (i, 0, j))
    return pl.pallas_call(
        body, name=name, grid=(NL, n // tn),
        in_specs=[pl.BlockSpec((2 * SUBLANES, D), lambda i, j: (0, 0)), pl.BlockSpec((None, D, tn), lambda i, j: (i, 0, j)),
                  dspec, dspec],
        out_specs=[pl.BlockSpec((None, D, tn), lambda i, j: (i, 0, j)), pl.BlockSpec((2 * SUBLANES, D), lambda i, j: (0, 0))],
        out_shape=[SDS((NL, D, n), F32), SDS((2 * SUBLANES, D), F32)],
        compiler_params=_cparams("arbitrary", "arbitrary"))(cond, ada_w, dm_lat, dm_ctx)


def _adam_math(w, g, m, v):
    m = ADAM_B1 * m + (1.0 - ADAM_B1) * g
    v = ADAM_B2 * v + (1.0 - ADAM_B2) * jnp.square(g)
    m_hat = m / (1.0 - ADAM_B1 ** ADAM_STEP)
    v_hat = v / (1.0 - ADAM_B2 ** ADAM_STEP)
    return -ADAM_LR * (m_hat / (jnp.sqrt(v_hat) + ADAM_EPS) + ADAM_WD * w), m, v


def _row_tile(rows, cols, elems, mult=SUBLANES):
    want = max(mult, elems // cols)
    best = mult if rows % mult == 0 else rows
    for d in range(mult, min(rows, want) + 1, mult):
        if rows % d == 0:
            best = d
    return best


def _adamw(w, m, v, parts, name):
    R, C = w.shape
    tr = _row_tile(R, C, 128 * 1024)
    npart = len(parts)

    def body(*refs):
        w_ref, m_ref, v_ref = refs[:3]
        g_ref, d_ref, nm_ref, nv_ref = refs[3 + npart:]
        g = refs[3][...]
        for p_ref in refs[4:3 + npart]:
            g = g + p_ref[...]
        d, nm, nv = _adam_math(w_ref[...], g, m_ref[...], v_ref[...])
        g_ref[...], d_ref[...], nm_ref[...], nv_ref[...] = g, d, nm, nv

    blk = pl.BlockSpec((tr, C), lambda i: (i, 0))
    return pl.pallas_call(body, name=name, grid=(R // tr,), in_specs=[blk] * (3 + npart), out_specs=[blk] * 4,
                          out_shape=[SDS((R, C), F32)] * 4, compiler_params=_cparams("parallel"))(w, m, v, *parts)


def _adamw_layer(w, m, v, layer, parts, prev, name):
    _, R, C = w.shape
    tr = _row_tile(R, C, 128 * 1024, 2 * SUBLANES)
    npart = len(parts)
    nprev = 0 if prev is None else 4

    def body(*refs):
        w_ref, m_ref, v_ref = refs[:3]
        g_ref, d_ref, nm_ref, nv_ref = refs[3 + npart + nprev:]
        g = refs[3][...].astype(F32)
        for p_ref in refs[4:3 + npart]:
            g = g + p_ref[...].astype(F32)
        d, nm, nv = _adam_math(w_ref[...], g, m_ref[...], v_ref[...])
        g_ref[...], d_ref[...], nm_ref[...], nv_ref[...] = g, d, nm, nv

    stacked = pl.BlockSpec((None, tr, C), lambda i: (layer, i, 0))
    flat = pl.BlockSpec((tr, C), lambda i: (i, 0))
    return pl.pallas_call(
        body, name=name, grid=(R // tr,),
        in_specs=[stacked] * 3 + [flat] * npart + [pl.BlockSpec(memory_space=pl.ANY)] * nprev, out_specs=[stacked] * 4,
        out_shape=[SDS(w.shape, F32)] * 4, input_output_aliases={3 + npart + k: k for k in range(nprev)},
        compiler_params=_cparams("parallel"))(w, m, v, *parts, *(prev or ()))


def _sum_slots(x, name, after=None, out_dtype=F32):
    S, R, C = x.shape
    tr = _row_tile(R, C, 128 * 1024, SUBLANES * 4 // jnp.dtype(x.dtype).itemsize)
    extra = [] if after is None else [after]

    def body(x_ref, *rest):
        o_ref = rest[-1]
        acc = x_ref[0].astype(F32)
        for s in range(1, S):
            acc = acc + x_ref[s].astype(F32)
        o_ref[...] = acc.astype(o_ref.dtype)

    return pl.pallas_call(
        body, name=name, grid=(R // tr,),
        in_specs=[pl.BlockSpec((S, tr, C), lambda i: (0, i, 0))] + [pl.BlockSpec(memory_space=pl.ANY)] * len(extra),
        out_specs=pl.BlockSpec((tr, C), lambda i: (i, 0)), out_shape=SDS((R, C), out_dtype),
        compiler_params=_cparams("parallel"))(x, *extra)


def _my_place():
    return lax.axis_index("x"), lax.axis_index("y"), lax.axis_index("c")


def _other_chips(x, y):
    return [(1 - x, y), (x, 1 - y), (1 - x, 1 - y)]


def _all_gather(v, name):
    R, C = v.shape

    def body(v_ref, o_ref, send_sems, recv_sems, local_sem):
        x, y, c = _my_place()
        me = 4 * x + 2 * y + c
        mine = pltpu.make_async_copy(v_ref, o_ref.at[me], local_sem)
        mine.start()
        copies = []
        for flip in range(1, N_DEV):
            fx, fy, fc = (flip >> 2) & 1, (flip >> 1) & 1, flip & 1
            peer = ((x + fx) % 2, (y + fy) % 2, (c + fc) % 2)
            cp = pltpu.make_async_remote_copy(src_ref=v_ref, dst_ref=o_ref.at[me], send_sem=send_sems.at[flip - 1],
                                              recv_sem=recv_sems.at[flip - 1], device_id=peer, device_id_type=MESH)
            cp.start()
            copies.append(cp)
        for cp in copies:
            cp.wait()
        mine.wait()

    return pl.pallas_call(
        body, name=name, in_specs=[pl.BlockSpec(memory_space=pl.ANY)], out_specs=pl.BlockSpec(memory_space=pl.ANY),
        out_shape=SDS((N_DEV, R, C), v.dtype),
        scratch_shapes=[pltpu.SemaphoreType.DMA((N_DEV - 1,)), pltpu.SemaphoreType.DMA((N_DEV - 1,)), pltpu.SemaphoreType.DMA],
        )(v)


def _shard_window(ref, axis, j, size):
    idx = [slice(None)] * len(ref.shape)
    idx[axis] = pl.ds(pl.multiple_of(j * size, SUBLANES), size)
    return ref.at[tuple(idx)]


def _gather_plan(axis):
    def plan(srcs, lands):
        x, y, c = _my_place()
        shard, whole = srcs[0], lands[0]
        half = shard.shape[0] // 2
        size = shard.shape[axis]

        def window(chip, which):
            if axis == 1:
                return whole.at[pl.ds(pl.multiple_of(which * half, SUBLANES), half), pl.ds(pl.multiple_of(chip * size, LANES), size)]
            return whole.at[pl.ds(pl.multiple_of(chip * size + which * half, SUBLANES), half), :]

        j = 2 * x + y
        local = [(shard, _shard_window(whole, axis, j, size))]
        mine = shard.at[pl.ds(pl.multiple_of(c * half, SUBLANES), half), :]
        remote = [(mine, window(j, c), (px, py, c)) for px, py in _other_chips(x, y)]
        forward = [(window(2 * px + py, c), window(2 * px + py, c), (x, y, 1 - c)) for px, py in _other_chips(x, y)]
        return local, remote, forward
    return plan


def _scatter_plan(axis):
    def plan(srcs, lands):
        x, y, c = _my_place()
        j = 2 * x + y
        size = srcs[0].shape[axis] // N_CHIPS
        local = [(_shard_window(srcs[0], axis, j, size), lands[0].at[j])]
        remote = [(_shard_window(srcs[0], axis, 2 * px + py, size), lands[0].at[j], (px, py, c)) for px, py in _other_chips(x, y)]
        return local, remote, []
    return plan


def _all_gather_plan(srcs, lands):
    x, y, c = _my_place()
    dst = lands[0].at[4 * x + 2 * y + c]
    remote = []
    for flip in range(1, N_DEV):
        fx, fy, fc = (flip >> 2) & 1, (flip >> 1) & 1, flip & 1
        remote.append((srcs[0], dst, ((x + fx) % 2, (y + fy) % 2, (c + fc) % 2)))
    return [(srcs[0], dst)], remote, []


def _same_core_peers():
    x, y, c = _my_place()
    return [(px, py, c) for px, py in _other_chips(x, y)]


def _same_core_peers_and_sibling():
    x, y, c = _my_place()
    return _same_core_peers() + [(x, y, 1 - c)]


def _all_peers():
    x, y, c = _my_place()
    return [((x + (f >> 2 & 1)) % 2, (y + (f >> 1 & 1)) % 2, (c + (f & 1)) % 2) for f in range(1, N_DEV)]


def _sequencer_exchange(name, collective_id, exchanges, peers_fn):
    hbm = pltpu.MemorySpace.HBM
    src_refs = [[jax.new_ref(s, memory_space=hbm) for s in e[0]] for e in exchanges]
    land_refs = [[jax.empty_ref(s, memory_space=hbm) for s in e[1]] for e in exchanges]
    first = [sum(e[3] for e in exchanges[:i]) for i in range(len(exchanges))]
    ncopy = sum(e[3] for e in exchanges)
    dma = pltpu.SemaphoreType.DMA

    @pl.kernel(mesh=plsc.ScalarSubcoreMesh(axis_name="sequencer", num_cores=N_SEQUENCERS), name=name,
               scratch_types=(dma((ncopy,)), dma((ncopy,)), dma((ncopy,)), dma((ncopy,)), dma),
               compiler_params=pltpu.CompilerParams(collective_id=collective_id))
    def launch(send_sems, recv_sems, onward_send_sems, onward_recv_sems, local_sem):
        me = lax.axis_index("sequencer")
        peers = peers_fn()
        barrier = pltpu.get_barrier_semaphore()
        for peer in peers:
            pl.semaphore_signal(barrier, inc=1, device_id=peer, device_id_type=MESH)
        pl.semaphore_wait(barrier, len(peers))
        plans = [e[2](src_refs[i], land_refs[i]) for i, e in enumerate(exchanges)]
        nbytes = lambda ref: math.prod(ref.shape) * jnp.dtype(ref.dtype).itemsize
        jobs = [(nbytes(dst), "local", (i, k)) for i, p in enumerate(plans) for k, (_, dst) in enumerate(p[0])]
        jobs += [(nbytes(src) * (2 if p[2] else 1), "remote", (i, k)) for i, p in enumerate(plans) for k, (src, _, _) in enumerate(p[1])]
        load, owner = [0] * N_SEQUENCERS, {}
        for size, kind, key in sorted(jobs, key=lambda job: -job[0]):
            owner[kind, key] = load.index(min(load))
            load[owner[kind, key]] += size
        for s in range(N_SEQUENCERS):
            @pl.when(me == s)
            def _(s=s):
                for i, (local, _, _) in enumerate(plans):
                    for k, (src, dst) in enumerate(local):
                        if owner["local", (i, k)] == s:
                            cp = pltpu.make_async_copy(src, dst, local_sem)
                            cp.start()
                            cp.wait()
                copies, onward = [], []
                for i, (_, remote, forward) in enumerate(plans):
                    assert len(remote) == exchanges[i][3] and len(forward) in (0, len(remote))
                    for k, (src, dst, peer) in enumerate(remote):
                        if owner["remote", (i, k)] == s:
                            cp = pltpu.make_async_remote_copy(src_ref=src, dst_ref=dst, send_sem=send_sems.at[first[i] + k],
                                                              recv_sem=recv_sems.at[first[i] + k], device_id=peer,
                                                              device_id_type=MESH)
                            cp.start()
                            copies.append(cp)
                            if forward:
                                src, dst, peer = forward[k]
                                onward.append(pltpu.make_async_remote_copy(
                                    src_ref=src, dst_ref=dst, send_sem=onward_send_sems.at[first[i] + k],
                                    recv_sem=onward_recv_sems.at[first[i] + k], device_id=peer, device_id_type=MESH))
                            else:
                                onward.append(None)
                for cp, on in zip(copies, onward):
                    cp.wait_recv()
                    if on is not None:
                        on.start()
                for cp, on in zip(copies, onward):
                    cp.wait_send()
                    if on is not None:
                        on.wait()

    launch()
    return [[r[...] for r in refs] for refs in land_refs]


def _swap_with_sibling(parts, name):
    nt = len(parts)

    def body(*refs):
        ins, outs = refs[:nt], refs[nt:2 * nt]
        send_sems, recv_sems = refs[2 * nt:]
        x, y, c = _my_place()
        copies = []
        for t in range(nt):
            cp = pltpu.make_async_remote_copy(src_ref=ins[t], dst_ref=outs[t], send_sem=send_sems.at[t], recv_sem=recv_sems.at[t],
                                              device_id=(x, y, 1 - c), device_id_type=MESH)
            cp.start()
            copies.append(cp)
        for cp in copies:
            cp.wait()

    any_spec = pl.BlockSpec(memory_space=pl.ANY)
    return pl.pallas_call(
        body, name=name, in_specs=[any_spec] * nt, out_specs=[any_spec] * nt, out_shape=[SDS(p.shape, p.dtype) for p in parts],
        scratch_shapes=[pltpu.SemaphoreType.DMA((nt,)), pltpu.SemaphoreType.DMA((nt,))],
        )(*parts)


PACK_COLS = 1024


def _pack(arrays):
    flat = jnp.concatenate([a.reshape(-1) for a in arrays])
    pad = (-flat.shape[0]) % (2 * SUBLANES * PACK_COLS)
    return jnp.pad(flat, (0, pad)).reshape(-1, PACK_COLS)


def _unpack(packed, shapes):
    flat, out, pos = packed.reshape(-1), [], 0
    for shape in shapes:
        n = math.prod(shape)
        out.append(flat[pos:pos + n].reshape(shape))
        pos += n
    return out


def _unshard_last(stacked):
    moved = jnp.moveaxis(stacked, 0, -2)
    return moved.reshape(moved.shape[:-2] + (moved.shape[-2] * moved.shape[-1],))


def _my_block_last(full, j):
    s = full.shape[-1] // N_CHIPS
    return lax.dynamic_index_in_dim(full.reshape(full.shape[:-1] + (N_CHIPS, s)), j, axis=full.ndim - 1, keepdims=False)


def _rope_tables(L):
    rows = L // GRID_W
    row = jnp.repeat(jnp.arange(rows), GRID_W).astype(F32)
    col = jnp.tile(jnp.arange(GRID_W), rows).astype(F32)
    axis_dim = HEAD_DIM // 2
    inv_freq = ROPE_BASE ** (-jnp.arange(0, axis_dim, 2, dtype=F32) / axis_dim)
    ang_r, ang_c = row[:, None] * inv_freq[None, :], col[:, None] * inv_freq[None, :]
    ang = jnp.concatenate([ang_r, ang_r, ang_c, ang_c] * 2, axis=-1)
    return jnp.cos(ang), jnp.sin(ang)


SMALL_SHARDED = ("norm_g", "ffn_conv_w", "cm_b_in", "cm_dw_w", "cm_dw_b", "cm_ln_g", "cm_ln_b", "cm_b_out", "gm_b_in", "gm_ln_g",
                 "gm_ln_b")
SMALL_REPLICATED = ("c_ctx", "ada_b", "ffn_conv_b", "attn_sink", "gm_w_s", "gm_b_s")
BIG = ("ffn_w_up", "ffn_w_down", "cm_w_in", "cm_w_out", "attn_w_qkv", "attn_w_o", "gm_w_in", "gm_w_out")
BIG_AXIS = {"ffn_w_up": 2, "ffn_w_down": 1, "cm_w_in": 2, "cm_w_out": 1, "attn_w_qkv": 2, "attn_w_o": 1, "gm_w_in": 2, "gm_w_out": 1}
WEIGHTS = ("c_ctx", "ada_w", "ada_b", "norm_g", "ffn_w_up", "ffn_conv_w", "ffn_conv_b", "ffn_w_down", "cm_w_in", "cm_b_in",
           "cm_dw_w", "cm_dw_b", "cm_ln_g", "cm_ln_b", "cm_w_out", "cm_b_out", "attn_w_qkv", "attn_sink", "attn_w_o", "gm_w_in",
           "gm_b_in", "gm_ln_g", "gm_ln_b", "gm_w_s", "gm_b_s", "gm_w_out")


def _step(x, c, ctx, target, W, M, V):
    L, D = x.shape[1], x.shape[2]
    C = ctx.shape[1]
    T = L + C
    NL = W["ada_w"].shape[0]
    tm = 256 if C % 256 == 0 else 128
    nl = L // tm
    xi, yi, ci = _my_place()
    chip = 2 * xi + yi
    dev = 4 * xi + 2 * yi + ci
    segs2, segs1 = [(0, L), (L, C)], [(0, L)]
    vec = lambda a: a.reshape(1, -1)

    layer_sets = [[("cm_w_in", 0), ("cm_w_out", 0), ("ffn_w_up", 0), ("ffn_w_down", 0)],
                  [("attn_w_qkv", 0), ("attn_w_o", 0), ("ffn_w_up", 1), ("ffn_w_down", 1)],
                  [("gm_w_in", 0), ("gm_w_out", 0), ("ffn_w_up", 2), ("ffn_w_down", 2)],
                  [("cm_w_in", 1), ("cm_w_out", 1), ("ffn_w_up", 3), ("ffn_w_down", 3)]]
    arrived = {}

    def fetch(keys, zero, sequencer_id):
        exchanges = []
        for n, i in keys:
            shard = (W[n][i] + zero).astype(MMT)
            whole = list(shard.shape)
            whole[BIG_AXIS[n] - 1] *= N_CHIPS
            exchanges.append(([shard], [SDS(tuple(whole), MMT)], _gather_plan(BIG_AXIS[n] - 1), N_CHIPS - 1))
        lands = _sequencer_exchange(f"fetch_weights_{keys[0][0]}_{keys[0][1]}", sequencer_id, exchanges,
                                    _same_core_peers_and_sibling)
        for key, land in zip(keys, lands):
            arrived[key] = land[0]

    def big(n, i, after=None):
        return arrived[(n, i)]

    small_shapes = [W[n].shape for n in SMALL_SHARDED]
    ag1 = _all_gather(_pack([c.reshape(-1)] + [W[n] for n in SMALL_SHARDED]), "gather_small")
    parts = [_unpack(ag1[2 * s], [(D,)] + small_shapes) for s in range(N_CHIPS)]
    c_rows = jnp.stack([_unpack(ag1[d], [(D,)])[0] for d in range(N_DEV)])
    P = {n: _unshard_last(jnp.stack([parts[s][1 + i] for s in range(N_CHIPS)])) for i, n in enumerate(SMALL_SHARDED)}
    for n in SMALL_REPLICATED:
        P[n] = W[n]

    cond = jnp.concatenate([c_rows, W["c_ctx"][None, :], jnp.zeros((2 * SUBLANES - N_DEV - 1, D), F32)], axis=0)
    ncol = W["ada_w"].shape[2]
    ada_b_mine = lax.dynamic_slice_in_dim(W["ada_b"], chip * ncol, ncol, axis=1)[:, None, :]
    mods_mine = _ada_fwd(cond, W["ada_w"], ada_b_mine, "ada_fwd")
    ag2 = _all_gather(mods_mine.reshape(NL * 2 * SUBLANES, ncol), "gather_mods").reshape(N_DEV, NL, 2 * SUBLANES, ncol)
    mods_all = _unshard_last(jnp.stack([ag2[2 * s] for s in range(N_CHIPS)]))
    mod_lat = lax.dynamic_index_in_dim(mods_all, dev, axis=1, keepdims=False).reshape(NL, 6, D)
    mod_ctx = mods_all[:, N_DEV].reshape(NL, 6, D)
    mod2 = jnp.stack([mod_lat, mod_ctx], axis=1)
    mod1 = mod_lat[:, None]

    corner = mod2[0, 0, 0, 0]
    behind_small = jnp.where(corner != corner, corner, 0.0)
    for k, keys in enumerate([layer_sets[0][:2], layer_sets[0][2:]] + layer_sets[1:]):
        fetch(keys, behind_small, FETCH_IDS[k])
    zero_d = jnp.zeros((1, D), F32)
    cos, sin = _rope_tables(L)
    nkv = D // HEAD_DIM // Q_PER_KV
    qdim, kvdim = D, nkv * HEAD_DIM

    def ffn_fwd(i, h, mod, rows, segs, tag):
        a2 = _prenorm(h, mod, vec(P["norm_g"][i, 2]), 1, rows, nl, tm, f"pre_ffn_{tag}")
        z0 = _mm(a2, big("ffn_w_up", i, a2), "nn", F32, f"ffn_up_{tag}")
        u = _ffn_gate(z0, P["ffn_conv_w"][i], vec(P["ffn_conv_b"][i]), segs, f"ffn_gate_{tag}")
        f = _mm(u, big("ffn_w_down", i, u), "nn", F32, f"ffn_down_{tag}")
        h_out = _postnorm(h, f, zero_d, mod, vec(P["norm_g"][i, 3]), 5, rows, nl, tm, f"post_ffn_{tag}")
        return h_out, dict(h=h, a2=a2, z0=z0, f=f)

    def ffn_bwd(i, dh, sv, mod, rows, segs, tag, G):
        df, dg2, dgn3, _ = _postnorm_bwd(dh, sv["f"], zero_d, mod, vec(P["norm_g"][i, 3]), 5, rows, nl, tm, f"post_ffn_bwd_{tag}")
        du = _mm(df, big("ffn_w_down", i), "nt", F32, f"ffn_down_dx_{tag}")
        u, dz0, dcw, dcb = _ffn_gate_bwd(sv["z0"], du, P["ffn_conv_w"][i], vec(P["ffn_conv_b"][i]), segs, f"ffn_gate_bwd_{tag}")
        G["ffn_w_down"][i] = _mm(u, df, "tn", MMT, f"ffn_down_dw_{tag}")
        G["ffn_w_up"][i] = _mm(sv["a2"], dz0, "tn", MMT, f"ffn_up_dw_{tag}")
        da2 = _mm(dz0, big("ffn_w_up", i), "nt", F32, f"ffn_up_dx_{tag}")
        dh, dsh2, dsc2, dgn2 = _prenorm_bwd(sv["h"], da2, dh, mod, vec(P["norm_g"][i, 2]), 1, rows, nl, tm, f"pre_ffn_bwd_{tag}")
        G["ffn_conv_w"][i], G["ffn_conv_b"][i] = dcw, dcb[0]
        return dh, (dsh2, dsc2, dg2), (dgn2, dgn3)

    def conformer_fwd(i, j, h, mod, rows, segs, tag):
        a = _prenorm(h, mod, vec(P["norm_g"][i, 0]), 0, rows, nl, tm, f"pre_mix_{tag}")
        p0 = _mm(a, big("cm_w_in", j, a), "nn", F32, f"cm_in_{tag}")
        z2 = _glu_conv(p0, vec(P["cm_b_in"][j]), P["cm_dw_w"][j], vec(P["cm_dw_b"][j]), segs, f"cm_conv_{tag}")
        z4 = _ln_silu(z2, vec(P["cm_ln_g"][j]), vec(P["cm_ln_b"][j]), rows, tm, f"cm_ln_{tag}")
        y = _mm(z4, big("cm_w_out", j, z4), "nn", F32, f"cm_out_{tag}")
        h_out = _postnorm(h, y, vec(P["cm_b_out"][j]), mod, vec(P["norm_g"][i, 1]), 2, rows, nl, tm, f"post_mix_{tag}")
        return h_out, dict(h=h, a=a, p0=p0, z2=z2, z4=z4, y=y)

    def conformer_bwd(i, j, dh, sv, mod, rows, segs, tag, G):
        dy, dg1, dgn1, dbo = _postnorm_bwd(dh, sv["y"], vec(P["cm_b_out"][j]) + zero_d, mod, vec(P["norm_g"][i, 1]), 2, rows, nl,
                                           tm, f"post_mix_bwd_{tag}")
        G["cm_w_out"][j] = _mm(sv["z4"], dy, "tn", MMT, f"cm_out_dw_{tag}")
        dz4 = _mm(dy, big("cm_w_out", j), "nt", F32, f"cm_out_dx_{tag}")
        dz2, dlg, dlb = _ln_silu_bwd(sv["z2"], dz4, vec(P["cm_ln_g"][j]), vec(P["cm_ln_b"][j]), rows, tm, f"cm_ln_bwd_{tag}")
        dpa, dpg, ddw, ddb, dba, dbg = _glu_conv_bwd(sv["p0"], vec(P["cm_b_in"][j]), P["cm_dw_w"][j], dz2, segs, f"cm_conv_bwd_{tag}")
        dp = jnp.concatenate([dpa, dpg], axis=1)
        G["cm_w_in"][j] = _mm(sv["a"], dp, "tn", MMT, f"cm_in_dw_{tag}")
        da = _mm(dp, big("cm_w_in", j), "nt", F32, f"cm_in_dx_{tag}")
        dh, dsh1, dsc1, dgn0 = _prenorm_bwd(sv["h"], da, dh, mod, vec(P["norm_g"][i, 0]), 0, rows, nl, tm, f"pre_mix_bwd_{tag}")
        G["cm_b_out"][j] = jnp.sum(dbo, axis=0)[0]
        G["cm_ln_g"][j], G["cm_ln_b"][j], G["cm_dw_w"][j], G["cm_dw_b"][j] = dlg[0], dlb[0], ddw, ddb[0]
        G["cm_b_in"][j] = jnp.concatenate([dba[0], dbg[0]])
        return dh, (dsh1, dsc1, dg1), (dgn0, dgn1)

    def heads(a, n):
        return a.reshape(a.shape[0], n, HEAD_DIM).transpose(1, 0, 2)

    def unheads(a):
        return a.transpose(1, 0, 2).reshape(a.shape[1], -1)

    G = {n: [None] * W[n].shape[0] for n in WEIGHTS if n not in ("c_ctx", "ada_w", "ada_b", "norm_g")}
    saved = []
    h = jnp.concatenate([x[0], ctx[0]], axis=0)
    h, s_mix = conformer_fwd(0, 0, h, mod2[0], T, segs2, "l0")
    h, s_ffn = ffn_fwd(0, h, mod2[0], T, segs2, "l0")
    saved.append((s_mix, s_ffn))
    a_all = _prenorm(h, mod2[1], vec(P["norm_g"][1, 0]), 0, T, nl, tm, "pre_mix_l1")
    qkv = _mm(a_all, big("attn_w_qkv", 0, a_all), "nn", F32, "attn_qkv")
    qk_rot, v_lat = _rope(qkv, cos, sin, L, qdim + kvdim, tm, "rope")
    q_h = heads(qk_rot[:, :qdim], nkv * Q_PER_KV).reshape(nkv, Q_PER_KV, L, HEAD_DIM)
    k_h, v_h = heads(qk_rot[:, qdim:], nkv), heads(v_lat, nkv)
    kc_h = heads(qkv[L:, qdim:qdim + kvdim].astype(MMT), nkv)
    vc_h = heads(qkv[L:, qdim + kvdim:].astype(MMT), nkv)
    sink = P["attn_sink"][0]
    o_h, lse = _attn_fwd(q_h, k_h, v_h, kc_h, vc_h, sink, "attn")
    o_nat = unheads(o_h.reshape(nkv * Q_PER_KV, L, HEAD_DIM)).astype(MMT)
    y1 = _mm(o_nat, big("attn_w_o", 0, o_nat), "nn", F32, "attn_out")
    h_in1 = h
    h = _postnorm(h, y1, zero_d, mod1[1], vec(P["norm_g"][1, 1]), 2, L, nl, tm, "post_mix_l1")
    h, s_ffn1 = ffn_fwd(1, h, mod1[1], L, segs1, "lat")
    h_in2 = h
    a_2 = _prenorm(h, mod1[2], vec(P["norm_g"][2, 0]), 0, L, nl, tm, "pre_mix_l2")
    p0_2 = _mm(a_2, big("gm_w_in", 0, a_2), "nn", F32, "gm_in")
    ws_bf = P["gm_w_s"][0].astype(MMT)
    bs_col = P["gm_b_s"][0][:, :, None]
    us = _gmlp_fwd(p0_2, vec(P["gm_b_in"][0]), vec(P["gm_ln_g"][0]), vec(P["gm_ln_b"][0]), ws_bf, bs_col, "gmlp")
    y2 = _mm(us, big("gm_w_out", 0, us), "nn", F32, "gm_out")
    h = _postnorm(h, y2, zero_d, mod1[2], vec(P["norm_g"][2, 1]), 2, L, nl, tm, "post_mix_l2")
    h, s_ffn2 = ffn_fwd(2, h, mod1[2], L, segs1, "lat")
    h, s_mix3 = conformer_fwd(3, 1, h, mod1[3], L, segs1, "l3")
    h, s_ffn3 = ffn_fwd(3, h, mod1[3], L, segs1, "lat")

    loss_mine, dh = _loss_head(h, target[0], tm, "loss_head")

    dmod = [None] * NL
    dgn = [None] * NL

    def finish(i, mix, ffn, gns_mix, gns_ffn):
        dmod[i] = jnp.concatenate(list(mix) + list(ffn), axis=1)
        dgn[i] = jnp.stack([jnp.sum(g, axis=0)[0] for g in (gns_mix[0], gns_mix[1], gns_ffn[0], gns_ffn[1])])

    sent, so_far = {}, {}

    def send(tag, collective_id, tensors):
        exchanges = []
        for n, l in tensors:
            g = G[n][l]
            shard = list(g.shape)
            shard[BIG_AXIS[n] - 1] //= N_CHIPS
            exchanges.append(([g], [SDS((N_CHIPS,) + tuple(shard), g.dtype)], _scatter_plan(BIG_AXIS[n] - 1), N_CHIPS - 1))
        sent[tag] = (tensors, _sequencer_exchange(f"send_grads_{tag}", collective_id, exchanges, _same_core_peers))
        corner = sum(G[n][l][0:1, 0:1].astype(F32) for n, l in tensors)
        return jnp.where(corner != corner, corner, 0.0)

    def land(tag, after):
        tensors, landed = sent[tag]
        mine = [_sum_slots(lands[0], f"sum_chips_{n}_{l}", after, MMT) for (n, l), lands in zip(tensors, landed)]
        theirs = _swap_with_sibling(mine, f"swap_cores_{tag}")
        for (n, l), a, b in zip(tensors, mine, theirs):
            so_far[n] = _adamw_layer(W[n], M[n], V[n], l, [a, b], so_far.get(n), f"adamw_{n}_{l}")

    dh, m_ffn, n_ffn = ffn_bwd(3, dh, s_ffn3, mod1[3], L, segs1, "lat", G)
    dh, m_mix, n_mix = conformer_bwd(3, 1, dh, s_mix3, mod1[3], L, segs1, "l3", G)
    finish(3, m_mix, m_ffn, n_mix, n_ffn)
    zero_d = zero_d + send("l3", SEND_IDS[0], [("ffn_w_up", 3), ("ffn_w_down", 3), ("cm_w_in", 1), ("cm_w_out", 1)])
    land("l3", None)

    dh, m_ffn, n_ffn = ffn_bwd(2, dh, s_ffn2, mod1[2], L, segs1, "lat", G)
    dy2, dg1, dgn1, _ = _postnorm_bwd(dh, y2, zero_d, mod1[2], vec(P["norm_g"][2, 1]), 2, L, nl, tm, "post_mix_bwd_l2")
    G["gm_w_out"][0] = _mm(us, dy2, "tn", MMT, "gm_out_dw")
    dus = _mm(dy2, big("gm_w_out", 0), "nt", F32, "gm_out_dx")
    ws_t = jnp.swapaxes(P["gm_w_s"][0], 1, 2).astype(MMT)
    dpre, dbi, dlg, dlb, dws, dbs = _gmlp_bwd(p0_2, dus, vec(P["gm_b_in"][0]), vec(P["gm_ln_g"][0]), vec(P["gm_ln_b"][0]), ws_bf,
                                              ws_t, bs_col, "gmlp_bwd")
    G["gm_w_in"][0] = _mm(a_2, dpre, "tn", MMT, "gm_in_dw")
    da = _mm(dpre, big("gm_w_in", 0), "nt", F32, "gm_in_dx")
    dh, dsh1, dsc1, dgn0 = _prenorm_bwd(h_in2, da, dh, mod1[2], vec(P["norm_g"][2, 0]), 0, L, nl, tm, "pre_mix_bwd_l2")
    G["gm_b_in"][0], G["gm_ln_g"][0], G["gm_ln_b"][0], G["gm_w_s"][0], G["gm_b_s"][0] = dbi[0], dlg[0], dlb[0], dws, dbs[:, :, 0]
    finish(2, (dsh1, dsc1, dg1), m_ffn, (dgn0, dgn1), n_ffn)
    zero_d = zero_d + send("l2", SEND_IDS[1], [("ffn_w_up", 2), ("ffn_w_down", 2), ("gm_w_in", 0), ("gm_w_out", 0)])
    land("l2", None)

    dh, m_ffn, n_ffn = ffn_bwd(1, dh, s_ffn1, mod1[1], L, segs1, "lat", G)
    dy1, dg1, dgn1, _ = _postnorm_bwd(dh, y1, zero_d, mod1[1], vec(P["norm_g"][1, 1]), 2, L, nl, tm, "post_mix_bwd_l1")
    G["attn_w_o"][0] = _mm(o_nat, dy1, "tn", MMT, "attn_out_dw")
    do_nat = _mm(dy1, big("attn_w_o", 0), "nt", MMT, "attn_out_dx")
    do_h = heads(do_nat, nkv * Q_PER_KV).reshape(nkv, Q_PER_KV, L, HEAD_DIM)
    dq_h, dkc_h, dvc_h, dsk = _attn_bwd_q(q_h, k_h, v_h, kc_h, vc_h, sink, o_h, do_h, lse, "attn_bwd_q")
    dk_h, dv_h = _attn_bwd_kv(q_h, k_h, v_h, o_h, do_h, lse, "attn_bwd_kv")
    dqk = jnp.concatenate([unheads(dq_h.reshape(nkv * Q_PER_KV, L, HEAD_DIM)), unheads(dk_h)], axis=1)
    dqkv_lat = _rope_bwd(dqk, unheads(dv_h), cos, sin, tm, "rope_bwd")
    dqkv_ctx = jnp.concatenate([jnp.zeros((C, qdim), MMT), unheads(dkc_h).astype(MMT), unheads(dvc_h).astype(MMT)], axis=1)
    dqkv = jnp.concatenate([dqkv_lat, dqkv_ctx], axis=0)
    G["attn_w_qkv"][0] = _mm(a_all, dqkv, "tn", MMT, "attn_qkv_dw")
    da_all = _mm(dqkv, big("attn_w_qkv", 0), "nt", F32, "attn_qkv_dx")
    dh_all = jnp.concatenate([dh, jnp.zeros((C, D), F32)], axis=0)
    dh, dsh1, dsc1, dgn0 = _prenorm_bwd(h_in1, da_all, dh_all, mod2[1], vec(P["norm_g"][1, 0]), 0, T, nl, tm, "pre_mix_bwd_l1")
    G["attn_sink"][0] = dsk[:, :Q_PER_KV, 0].reshape(-1)
    pad_ctx = lambda a: jnp.concatenate([a, jnp.zeros_like(a)], axis=0)
    finish(1, (dsh1, dsc1, pad_ctx(dg1)), [pad_ctx(a) for a in m_ffn], (dgn0, dgn1), n_ffn)
    zero_d = zero_d + send("l1", SEND_IDS[2], [("ffn_w_up", 1), ("ffn_w_down", 1), ("attn_w_qkv", 0), ("attn_w_o", 0)])
    land("l1", None)

    s_mix0, s_ffn0 = saved[0]
    dh, m_ffn, n_ffn = ffn_bwd(0, dh, s_ffn0, mod2[0], T, segs2, "l0", G)
    zero_d = zero_d + send("l0_ffn", SEND_IDS[3], [("ffn_w_up", 0), ("ffn_w_down", 0)])
    dh, m_mix, n_mix = conformer_bwd(0, 0, dh, s_mix0, mod2[0], T, segs2, "l0", G)
    finish(0, m_mix, m_ffn, n_mix, n_ffn)
    grad_x = dh[:L][None]
    sent_l0 = send("l0_mix", SEND_IDS[4], [("cm_w_in", 0), ("cm_w_out", 0)])

    for i in range(2, NL):
        dmod[i] = pad_ctx(dmod[i])
    dmod_all = jnp.stack(dmod).reshape(NL, 2, 6 * D) + sent_l0

    ag3 = _all_gather(dmod_all.reshape(NL * 2, 6 * D), "gather_dmods").reshape(N_DEV, NL, 2, N_CHIPS, ncol)
    dm_cols = lax.dynamic_index_in_dim(ag3, chip, axis=3, keepdims=False)
    dm_lat, dm_ctx = jnp.moveaxis(dm_cols[:, :, 0], 0, 1), jnp.moveaxis(dm_cols[:, :, 1], 0, 1)
    g_ada_w, dsilu = _ada_bwd(cond, W["ada_w"], dm_lat, dm_ctx, "ada_bwd")
    cc = W["c_ctx"]
    sg = jax.nn.sigmoid(cc)
    dcctx_part = jnp.where(ci == 0, 1.0, 0.0) * dsilu[N_DEV] * (sg * (1.0 + cc * (1.0 - sg)))

    Gs = {n: jnp.stack(G[n]) for n in G if n not in BIG}
    Gs["norm_g"] = jnp.stack(dgn)
    Gs["ada_b"] = jnp.sum(dmod_all, axis=1)
    Gs["c_ctx"] = dcctx_part
    small_names = list(SMALL_SHARDED) + list(SMALL_REPLICATED)
    small_full_shapes = [P[n].shape for n in small_names]
    small_pack = _pack([Gs[n] for n in small_names]).astype(MMT)
    ((ag4,),) = _sequencer_exchange("gather_small_grads", SMALL_GRADS_ID, [
        ([small_pack], [SDS((N_DEV,) + small_pack.shape, MMT)], _all_gather_plan, N_DEV - 1)], _all_peers)

    flat2 = lambda a: a.reshape(-1, a.shape[-1])
    res = {}
    outs = _adamw(flat2(W["ada_w"]), flat2(M["ada_w"]), flat2(V["ada_w"]), [flat2(g_ada_w)], "adamw_ada_w")
    res["ada_w"] = tuple(o.reshape(W["ada_w"].shape) for o in outs)

    land("l0_ffn", outs[0])
    land("l0_mix", so_far["ffn_w_up"][0])
    for n in BIG:
        res[n] = tuple(so_far[n])

    small_sum = _unpack(_sum_slots(ag4, "sum_small_grads"), small_full_shapes)
    g_small = {}
    for n, g in zip(small_names, small_sum):
        g_small[n] = _my_block_last(g, chip) if n in SMALL_SHARDED else g
    packed = [_pack([d[n] for n in small_names]) for d in (W, M, V)]
    outs_small = _adamw(packed[0], packed[1], packed[2], [_pack([g_small[n] for n in small_names])], "adamw_small")
    shard_shapes = [W[n].shape for n in small_names]
    for k, n in enumerate(small_names):
        res[n] = tuple(_unpack(o, shard_shapes)[k] for o in outs_small)

    loss = lax.psum(loss_mine[0, 0], ("x", "y", "c"))
    return (loss, grad_x) + tuple(res[n][k] for k in range(4) for n in WEIGHTS)


def kernel(x, c, ctx, c_ctx, ada_w, ada_b, norm_g, ffn_w_up, ffn_conv_w, ffn_conv_b, ffn_w_down, cm_w_in, cm_b_in, cm_dw_w, cm_dw_b, cm_ln_g, cm_ln_b, cm_w_out, cm_b_out, attn_w_qkv, attn_sink, attn_w_o, gm_w_in, gm_b_in, gm_ln_g, gm_ln_b, gm_w_s, gm_b_s, gm_w_out, loss_target, m_c_ctx, m_ada_w, m_ada_b, m_norm_g, m_ffn_w_up, m_ffn_conv_w, m_ffn_conv_b, m_ffn_w_down, m_cm_w_in, m_cm_b_in, m_cm_dw_w, m_cm_dw_b, m_cm_ln_g, m_cm_ln_b, m_cm_w_out, m_cm_b_out, m_attn_w_qkv, m_attn_sink, m_attn_w_o, m_gm_w_in, m_gm_b_in, m_gm_ln_g, m_gm_ln_b, m_gm_w_s, m_gm_b_s, m_gm_w_out, v_c_ctx, v_ada_w, v_ada_b, v_norm_g, v_ffn_w_up, v_ffn_conv_w, v_ffn_conv_b, v_ffn_w_down, v_cm_w_in, v_cm_b_in, v_cm_dw_w, v_cm_dw_b, v_cm_ln_g, v_cm_ln_b, v_cm_w_out, v_cm_b_out, v_attn_w_qkv, v_attn_sink, v_attn_w_o, v_gm_w_in, v_gm_b_in, v_gm_ln_g, v_gm_ln_b, v_gm_w_s, v_gm_b_s, v_gm_w_out):
    args = locals()
    W = {n: args[n] for n in WEIGHTS}
    M = {n: args["m_" + n] for n in WEIGHTS}
    V = {n: args["v_" + n] for n in WEIGHTS}
    return _step(x, c, ctx, loss_target, W, M, V)
```

```python
import functools
import math

import jax
import jax.numpy as jnp
from jax import lax
from jax.experimental import pallas as pl
from jax.experimental.pallas import tpu as pltpu
from jax.experimental.pallas import tpu_sc as plsc

F32 = jnp.float32
MMT = jnp.bfloat16
SDS = jax.ShapeDtypeStruct
MESH = pl.DeviceIdType.MESH

EPS = 1e-6
HEAD_DIM = 64
Q_PER_KV = 4
ATTN_BLOCK = 128
GRID_W = 64
ROPE_BASE = 10000.0
GMLP_CHUNK = 128
GMLP_GROUP_DIM = 128
CONV_WIDTH = 31
FFN_CONV_WIDTH = 3
NEG = -1e30

ADAM_LR, ADAM_B1, ADAM_B2, ADAM_EPS, ADAM_WD, ADAM_STEP = 0.001, 0.9, 0.999, 1e-08, 0.01, 10

LANES = 128
SUBLANES = 8
VMEM_LIMIT = 52 * 1024 * 1024
CONV_ROWS = 128
N_CHIPS = 4
N_DEV = 8
N_SEQUENCERS = 2
FETCH_IDS = (1, 2, 3, 4, 11, 12, 13, 14)
SEND_IDS = (5, 6, 7, 8, 9, 15)
SMALL_GRADS_ID = 10


def _cparams(*sem):
    return pltpu.CompilerParams(dimension_semantics=sem if sem else None, vmem_limit_bytes=VMEM_LIMIT)


def _tile(n, cap, mult=LANES):
    best = None
    for d in range(mult, min(n, cap) + 1, mult):
        if n % d == 0:
            best = d
    return best if best is not None else n


def _sum0(v):
    return jnp.sum(v, axis=0, keepdims=True)


def _rms(v):
    r = lax.rsqrt(jnp.mean(v * v, axis=-1, keepdims=True) + EPS)
    return v * r, r


def _sig(v):
    return jax.nn.sigmoid(v)


def _dot(a, b, ca, cb):
    return lax.dot_general(a.astype(MMT), b.astype(MMT), (((ca,), (cb,)), ((), ())), preferred_element_type=F32)


def _mm(a, b, mode, out_dtype, name):
    if mode == "nn":
        (M, K), N = a.shape, b.shape[1]
    elif mode == "nt":
        (M, K), N = a.shape, b.shape[0]
    else:
        (K, M), N = a.shape, b.shape[1]
    tm, tn, tk = _tile(M, 512), _tile(N, 1408), _tile(K, 1536)
    nk = K // tk
    ca, cb = {"nn": (1, 0), "nt": (1, 1), "tn": (0, 0)}[mode]

    def body(a_ref, b_ref, o_ref, acc):
        k = pl.program_id(2)

        @pl.when(k == 0)
        def _():
            acc[...] = jnp.zeros_like(acc)

        acc[...] += _dot(a_ref[...], b_ref[...], ca, cb)

        @pl.when(k == nk - 1)
        def _():
            o_ref[...] = acc[...].astype(o_ref.dtype)

    a_spec = pl.BlockSpec((tk, tm), lambda i, j, k: (k, i)) if mode == "tn" else pl.BlockSpec((tm, tk), lambda i, j, k: (i, k))
    b_spec = pl.BlockSpec((tn, tk), lambda i, j, k: (j, k)) if mode == "nt" else pl.BlockSpec((tk, tn), lambda i, j, k: (k, j))
    return pl.pallas_call(
        body, name=name, grid=(M // tm, N // tn, nk), in_specs=[a_spec, b_spec],
        out_specs=pl.BlockSpec((tm, tn), lambda i, j, k: (i, j)), out_shape=SDS((M, N), out_dtype),
        scratch_shapes=[pltpu.VMEM((tm, tn), F32)], compiler_params=_cparams("parallel", "parallel", "arbitrary"))(a, b)


def _seg_of(nl, nseg):
    return (lambda i: jnp.where(i >= nl, 1, 0)) if nseg == 2 else (lambda i: 0)


def _prenorm(h, mod, gn, which, rows, nl, tm, name):
    D = h.shape[1]
    nseg = mod.shape[0]
    seg = _seg_of(nl, nseg)
    sh_i, sc_i = (0, 1) if which == 0 else (3, 4)

    def body(h_ref, mod_ref, gn_ref, a_ref):
        n, _ = _rms(h_ref[...])
        a_ref[...] = (n * gn_ref[...] * (1.0 + mod_ref[pl.ds(sc_i, 1), :]) + mod_ref[pl.ds(sh_i, 1), :]).astype(a_ref.dtype)

    return pl.pallas_call(
        body, name=name, grid=(rows // tm,),
        in_specs=[pl.BlockSpec((tm, D), lambda i: (i, 0)), pl.BlockSpec((None, 6, D), lambda i: (seg(i), 0, 0)),
                  pl.BlockSpec((1, D), lambda i: (0, 0))],
        out_specs=pl.BlockSpec((tm, D), lambda i: (i, 0)), out_shape=SDS((rows, D), MMT),
        compiler_params=_cparams("parallel"))(h, mod, gn)


def _acc_spec(D, seg):
    return pl.BlockSpec((None, 1, D), lambda i: (seg(i), 0, 0))


def _prenorm_bwd(h, da, dh_in, mod, gn, which, rows, nl, tm, name):
    D = h.shape[1]
    nseg = mod.shape[0]
    seg = _seg_of(nl, nseg)
    sc_i = 1 if which == 0 else 4

    def body(h_ref, da_ref, dhin_ref, mod_ref, gn_ref, dh_ref, dsh_ref, dsc_ref, dgn_ref):
        i = pl.program_id(0)
        first = (i == 0) | (i == nl) if nseg == 2 else (i == 0)

        @pl.when(first)
        def _():
            dsh_ref[...] = jnp.zeros_like(dsh_ref)
            dsc_ref[...] = jnp.zeros_like(dsc_ref)
            dgn_ref[...] = jnp.zeros_like(dgn_ref)

        n, r = _rms(h_ref[...])
        da_v = da_ref[...].astype(F32)
        gn_v = gn_ref[...]
        sc1 = 1.0 + mod_ref[pl.ds(sc_i, 1), :]
        dsh_ref[...] += _sum0(da_v)
        dsc_ref[...] += _sum0(da_v * (n * gn_v))
        dgn_ref[...] += _sum0(da_v * n * sc1)
        dn = da_v * (gn_v * sc1)
        dh_ref[...] = dhin_ref[...] + r * (dn - n * jnp.mean(dn * n, axis=-1, keepdims=True))

    row = pl.BlockSpec((tm, D), lambda i: (i, 0))
    acc = SDS((nseg, 1, D), F32)
    return pl.pallas_call(
        body, name=name, grid=(rows // tm,),
        in_specs=[row, row, row, pl.BlockSpec((None, 6, D), lambda i: (seg(i), 0, 0)), pl.BlockSpec((1, D), lambda i: (0, 0))],
        out_specs=[row, _acc_spec(D, seg), _acc_spec(D, seg), _acc_spec(D, seg)],
        out_shape=[SDS((rows, D), F32), acc, acc, acc], compiler_params=_cparams("arbitrary"))(h, da, dh_in, mod, gn)


def _postnorm(h, y, bias, mod, gn, gate_i, rows, nl, tm, name):
    D = h.shape[1]
    nseg = mod.shape[0]
    seg = _seg_of(nl, nseg)

    def body(h_ref, y_ref, b_ref, mod_ref, gn_ref, o_ref):
        ny, _ = _rms(y_ref[...] + b_ref[...])
        o_ref[...] = h_ref[...] + mod_ref[pl.ds(gate_i, 1), :] * (ny * gn_ref[...])

    row = pl.BlockSpec((tm, D), lambda i: (i, 0))
    vec = pl.BlockSpec((1, D), lambda i: (0, 0))
    return pl.pallas_call(
        body, name=name, grid=(rows // tm,),
        in_specs=[row, row, vec, pl.BlockSpec((None, 6, D), lambda i: (seg(i), 0, 0)), vec],
        out_specs=row, out_shape=SDS((rows, D), F32), compiler_params=_cparams("parallel"))(h, y, bias, mod, gn)


def _postnorm_bwd(dh, y, bias, mod, gn, gate_i, rows, nl, tm, name):
    D = y.shape[1]
    nseg = mod.shape[0]
    seg = _seg_of(nl, nseg)

    def body(dh_ref, y_ref, b_ref, mod_ref, gn_ref, dy_ref, dg_ref, dgn_ref, db_ref):
        i = pl.program_id(0)
        first = (i == 0) | (i == nl) if nseg == 2 else (i == 0)

        @pl.when(first)
        def _():
            dg_ref[...] = jnp.zeros_like(dg_ref)
            dgn_ref[...] = jnp.zeros_like(dgn_ref)
            db_ref[...] = jnp.zeros_like(db_ref)

        ny, ry = _rms(y_ref[...] + b_ref[...])
        g = mod_ref[pl.ds(gate_i, 1), :]
        gn_v = gn_ref[...]
        dh_v = dh_ref[...]
        dg_ref[...] += _sum0(dh_v * (ny * gn_v))
        dgn_ref[...] += _sum0(dh_v * ny * g)
        dny = dh_v * (g * gn_v)
        dy = ry * (dny - ny * jnp.mean(dny * ny, axis=-1, keepdims=True))
        db_ref[...] += _sum0(dy)
        dy_ref[...] = dy.astype(dy_ref.dtype)

    row = pl.BlockSpec((tm, D), lambda i: (i, 0))
    vec = pl.BlockSpec((1, D), lambda i: (0, 0))
    acc = SDS((nseg, 1, D), F32)
    return pl.pallas_call(
        body, name=name, grid=(rows // tm,),
        in_specs=[row, row, vec, pl.BlockSpec((None, 6, D), lambda i: (seg(i), 0, 0)), vec],
        out_specs=[row, _acc_spec(D, seg), _acc_spec(D, seg), _acc_spec(D, seg)],
        out_shape=[SDS((rows, D), MMT), acc, acc, acc], compiler_params=_cparams("arbitrary"))(dh, y, bias, mod, gn)


def _seg_layout(segs, H):
    out, base = [], H
    for s0, n in segs:
        out.append((s0, n, base))
        base += n + H
    return out, base


def _zero_pads(ref, lay, H):
    width = ref.shape[1]
    ref[pl.ds(0, H), :] = jnp.zeros((H, width), ref.dtype)
    for _, n, base in lay:
        ref[pl.ds(base + n, H), :] = jnp.zeros((H, width), ref.dtype)


def _window(ref, base, off, H):
    return ref[pl.ds(base - H + off, CONV_ROWS + 2 * H), :]


def _taps(win, H, offs):
    W = CONV_ROWS + 2 * H
    rolled, out = {}, {}
    for o in offs:
        s = H + o
        b = s % SUBLANES
        if b not in rolled:
            rolled[b] = win if b == 0 else pltpu.roll(win, shift=W - b, axis=0)
        out[o] = rolled[b][s - b:s - b + CONV_ROWS, :]
    return out


def _chunks(lay, fn):
    for s0, n, base in lay:
        def step(r, carry, s0=s0, base=base):
            fn(s0, base, pl.multiple_of(r * CONV_ROWS, CONV_ROWS))
            return carry
        lax.fori_loop(0, n // CONV_ROWS, step, 0)


def _ffn_gate(z0, conv_w, conv_b, segs, name):
    T, F2 = z0.shape
    F = F2 // 2
    tc = _tile(F, 256)
    nF = F // tc
    H = SUBLANES
    lay, srows = _seg_layout(segs, H)
    offs = [-1, 0, 1]

    def body(zg_ref, zv_ref, wg_ref, wv_ref, bg_ref, bv_ref, u_ref, xg, xv):
        _zero_pads(xg, lay, H)
        _zero_pads(xv, lay, H)
        for s0, n, base in lay:
            xg[pl.ds(base, n), :] = zg_ref[pl.ds(s0, n), :]
            xv[pl.ds(base, n), :] = zv_ref[pl.ds(s0, n), :]

        def chunk(s0, base, off):
            tg = _taps(_window(xg, base, off, H), H, offs)
            tv = _taps(_window(xv, base, off, H), H, offs)
            zg = bg_ref[...] + sum(tg[k - 1] * wg_ref[pl.ds(k, 1), :] for k in range(3))
            zv = bv_ref[...] + sum(tv[k - 1] * wv_ref[pl.ds(k, 1), :] for k in range(3))
            u_ref[pl.ds(s0 + off, CONV_ROWS), :] = (zg * _sig(zg) * zv).astype(u_ref.dtype)

        _chunks(lay, chunk)

    colg = lambda r: pl.BlockSpec((r, tc), lambda j: (0, j))
    colv = lambda r: pl.BlockSpec((r, tc), lambda j: (0, j + nF))
    return pl.pallas_call(
        body, name=name, grid=(nF,),
        in_specs=[colg(T), colv(T), colg(3), colv(3), colg(1), colv(1)],
        out_specs=colg(T), out_shape=SDS((T, F), MMT),
        scratch_shapes=[pltpu.VMEM((srows, tc), F32), pltpu.VMEM((srows, tc), F32)],
        compiler_params=_cparams("parallel"))(z0, z0, conv_w, conv_w, conv_b, conv_b)


def _ffn_gate_bwd(z0, du, conv_w, conv_b, segs, name):
    T, F2 = z0.shape
    F = F2 // 2
    tc = _tile(F, 256)
    nF = F // tc
    H = SUBLANES
    lay, srows = _seg_layout(segs, H)
    offs = [-1, 0, 1]

    def body(zo_ref, zt_ref, du_ref, wo_ref, wt_ref, bo_ref, bt_ref, u_ref, dz0_ref, dw_ref, db_ref, xo, xt, dzp):
        own_is_gate = pl.program_id(1) == 0
        for ref in (xo, xt, dzp):
            _zero_pads(ref, lay, H)
        for s0, n, base in lay:
            xo[pl.ds(base, n), :] = zo_ref[pl.ds(s0, n), :]
            xt[pl.ds(base, n), :] = zt_ref[pl.ds(s0, n), :]

        def grads(s0, base, off):
            to = _taps(_window(xo, base, off, H), H, offs)
            tt = _taps(_window(xt, base, off, H), H, offs)
            zo = bo_ref[...] + sum(to[k - 1] * wo_ref[pl.ds(k, 1), :] for k in range(3))
            zt = bt_ref[...] + sum(tt[k - 1] * wt_ref[pl.ds(k, 1), :] for k in range(3))
            so, st = _sig(zo), _sig(zt)
            du_v = du_ref[pl.ds(s0 + off, CONV_ROWS), :]
            d_gate = du_v * zt * (so * (1.0 + zo * (1.0 - so)))
            d_val = du_v * (zt * st)
            dzp[pl.ds(base + off, CONV_ROWS), :] = jnp.where(own_is_gate, d_gate, d_val)

            @pl.when(own_is_gate)
            def _():
                u_ref[pl.ds(s0 + off, CONV_ROWS), :] = (zo * so * zt).astype(u_ref.dtype)

        _chunks(lay, grads)
        dw_ref[...] = jnp.zeros_like(dw_ref)
        db_ref[...] = jnp.zeros_like(db_ref)

        def back(s0, base, off):
            td = _taps(_window(dzp, base, off, H), H, offs)
            tx = _taps(_window(xo, base, off, H), H, offs)
            dz0 = sum(td[1 - k] * wo_ref[pl.ds(k, 1), :] for k in range(3))
            dz0_ref[pl.ds(s0 + off, CONV_ROWS), :] = dz0.astype(dz0_ref.dtype)
            db_ref[...] += _sum0(td[0])
            for k in range(3):
                dw_ref[pl.ds(k, 1), :] += _sum0(td[0] * tx[k - 1])

        _chunks(lay, back)

    own = lambda r: pl.BlockSpec((r, tc), lambda j, hf: (0, hf * nF + j))
    oth = lambda r: pl.BlockSpec((r, tc), lambda j, hf: (0, (1 - hf) * nF + j))
    ucol = pl.BlockSpec((T, tc), lambda j, hf: (0, j))
    return pl.pallas_call(
        body, name=name, grid=(nF, 2),
        in_specs=[own(T), oth(T), ucol, own(3), oth(3), own(1), oth(1)],
        out_specs=[ucol, own(T), own(3), own(1)],
        out_shape=[SDS((T, F), MMT), SDS((T, F2), MMT), SDS((3, F2), F32), SDS((1, F2), F32)],
        scratch_shapes=[pltpu.VMEM((srows, tc), F32)] * 3,
        compiler_params=_cparams("parallel", "arbitrary"))(z0, z0, du, conv_w, conv_w, conv_b, conv_b)


def _glu_conv(p0, b_in, dw_w, dw_b, segs, name):
    T, D2 = p0.shape
    D = D2 // 2
    tc = _tile(D, 256)
    nD = D // tc
    H = 2 * SUBLANES
    half = (CONV_WIDTH - 1) // 2
    lay, srows = _seg_layout(segs, H)
    offs = list(range(-half, half + 1))

    def body(pa_ref, pg_ref, ba_ref, bg_ref, w_ref, b_ref, z2_ref, z1p):
        _zero_pads(z1p, lay, H)

        def glu(s0, base, off):
            rows = pl.ds(s0 + off, CONV_ROWS)
            z1p[pl.ds(base + off, CONV_ROWS), :] = (pa_ref[rows, :] + ba_ref[...]) * _sig(pg_ref[rows, :] + bg_ref[...])

        _chunks(lay, glu)

        def conv(s0, base, off):
            t = _taps(_window(z1p, base, off, H), H, offs)
            acc = b_ref[...] + t[-half] * w_ref[pl.ds(0, 1), :]
            for k in range(1, CONV_WIDTH):
                acc = acc + t[k - half] * w_ref[pl.ds(k, 1), :]
            z2_ref[pl.ds(s0 + off, CONV_ROWS), :] = acc

        _chunks(lay, conv)

    cola = lambda r: pl.BlockSpec((r, tc), lambda j: (0, j))
    colg = lambda r: pl.BlockSpec((r, tc), lambda j: (0, j + nD))
    return pl.pallas_call(
        body, name=name, grid=(nD,),
        in_specs=[cola(T), colg(T), cola(1), colg(1), cola(CONV_WIDTH), cola(1)],
        out_specs=cola(T), out_shape=SDS((T, D), F32), scratch_shapes=[pltpu.VMEM((srows, tc), F32)],
        compiler_params=_cparams("parallel"))(p0, p0, b_in, b_in, dw_w, dw_b)


def _glu_conv_bwd(p0, b_in, dw_w, dz2, segs, name):
    T, D2 = p0.shape
    D = D2 // 2
    tc = _tile(D, 256)
    nD = D // tc
    H = 2 * SUBLANES
    half = (CONV_WIDTH - 1) // 2
    lay, srows = _seg_layout(segs, H)
    offs = list(range(-half, half + 1))

    def body(pa_ref, pg_ref, ba_ref, bg_ref, w_ref, dz2_ref, dpa_ref, dpg_ref, dw_ref, db_ref, dba_ref, dbg_ref, z1p, dzp):
        _zero_pads(z1p, lay, H)
        _zero_pads(dzp, lay, H)
        for s0, n, base in lay:
            dzp[pl.ds(base, n), :] = dz2_ref[pl.ds(s0, n), :]

        def glu(s0, base, off):
            rows = pl.ds(s0 + off, CONV_ROWS)
            z1p[pl.ds(base + off, CONV_ROWS), :] = (pa_ref[rows, :] + ba_ref[...]) * _sig(pg_ref[rows, :] + bg_ref[...])

        _chunks(lay, glu)
        for ref in (dw_ref, db_ref, dba_ref, dbg_ref):
            ref[...] = jnp.zeros_like(ref)

        def back(s0, base, off):
            td = _taps(_window(dzp, base, off, H), H, offs)
            tz = _taps(_window(z1p, base, off, H), H, offs)
            dz1 = td[half] * w_ref[pl.ds(0, 1), :]
            for k in range(1, CONV_WIDTH):
                dz1 = dz1 + td[half - k] * w_ref[pl.ds(k, 1), :]
            db_ref[...] += _sum0(td[0])
            for k in range(CONV_WIDTH):
                dw_ref[pl.ds(k, 1), :] += _sum0(td[0] * tz[k - half])
            rows = pl.ds(s0 + off, CONV_ROWS)
            pa = pa_ref[rows, :] + ba_ref[...]
            sg = _sig(pg_ref[rows, :] + bg_ref[...])
            dpa = dz1 * sg
            dpg = dz1 * pa * (sg * (1.0 - sg))
            dba_ref[...] += _sum0(dpa)
            dbg_ref[...] += _sum0(dpg)
            dpa_ref[rows, :] = dpa.astype(dpa_ref.dtype)
            dpg_ref[rows, :] = dpg.astype(dpg_ref.dtype)

        _chunks(lay, back)

    cola = lambda r: pl.BlockSpec((r, tc), lambda j: (0, j))
    colg = lambda r: pl.BlockSpec((r, tc), lambda j: (0, j + nD))
    return pl.pallas_call(
        body, name=name, grid=(nD,),
        in_specs=[cola(T), colg(T), cola(1), colg(1), cola(CONV_WIDTH), cola(T)],
        out_specs=[cola(T), cola(T), cola(CONV_WIDTH), cola(1), cola(1), cola(1)],
        out_shape=[SDS((T, D), MMT), SDS((T, D), MMT), SDS((CONV_WIDTH, D), F32), SDS((1, D), F32), SDS((1, D), F32),
                   SDS((1, D), F32)],
        scratch_shapes=[pltpu.VMEM((srows, tc), F32)] * 2, compiler_params=_cparams("parallel"))(p0, p0, b_in, b_in, dw_w, dz2)


def _layer_norm_stats(v):
    mu = jnp.mean(v, axis=-1, keepdims=True)
    var = jnp.mean(jnp.square(v - mu), axis=-1, keepdims=True)
    rstd = lax.rsqrt(var + EPS)
    return (v - mu) * rstd, rstd


def _ln_silu(z2, ln_g, ln_b, rows, tm, name):
    D = z2.shape[1]

    def body(z_ref, g_ref, b_ref, o_ref):
        xh, _ = _layer_norm_stats(z_ref[...])
        z3 = xh * g_ref[...] + b_ref[...]
        o_ref[...] = (z3 * _sig(z3)).astype(o_ref.dtype)

    row = pl.BlockSpec((tm, D), lambda i: (i, 0))
    vec = pl.BlockSpec((1, D), lambda i: (0, 0))
    return pl.pallas_call(body, name=name, grid=(rows // tm,), in_specs=[row, vec, vec], out_specs=row,
                          out_shape=SDS((rows, D), MMT), compiler_params=_cparams("parallel"))(z2, ln_g, ln_b)


def _ln_silu_bwd(z2, dz4, ln_g, ln_b, rows, tm, name):
    D = z2.shape[1]

    def body(z_ref, d_ref, g_ref, b_ref, dz_ref, dg_ref, db_ref):
        @pl.when(pl.program_id(0) == 0)
        def _():
            dg_ref[...] = jnp.zeros_like(dg_ref)
            db_ref[...] = jnp.zeros_like(db_ref)

        xh, rstd = _layer_norm_stats(z_ref[...])
        z3 = xh * g_ref[...] + b_ref[...]
        s = _sig(z3)
        dz3 = d_ref[...] * (s * (1.0 + z3 * (1.0 - s)))
        dg_ref[...] += _sum0(dz3 * xh)
        db_ref[...] += _sum0(dz3)
        dxh = dz3 * g_ref[...]
        dz_ref[...] = rstd * (dxh - jnp.mean(dxh, axis=-1, keepdims=True) - xh * jnp.mean(dxh * xh, axis=-1, keepdims=True))

    row = pl.BlockSpec((tm, D), lambda i: (i, 0))
    vec = pl.BlockSpec((1, D), lambda i: (0, 0))
    return pl.pallas_call(body, name=name, grid=(rows // tm,), in_specs=[row, row, vec, vec], out_specs=[row, vec, vec],
                          out_shape=[SDS((rows, D), F32), SDS((1, D), F32), SDS((1, D), F32)],
                          compiler_params=_cparams("arbitrary"))(z2, dz4, ln_g, ln_b)


def _rot_half_pairs(v):
    width = v.shape[1]
    lane = lax.broadcasted_iota(jnp.int32, v.shape, 1)
    return jnp.where((lane % 32) < 16, -pltpu.roll(v, shift=width - 16, axis=1), pltpu.roll(v, shift=16, axis=1))


def _rope(qkv, cos, sin, L, qk, tm, name):
    width = qkv.shape[1]
    kv = width - qk

    def body(x_ref, c_ref, s_ref, qk_ref, v_ref):
        xv = x_ref[:, pl.ds(0, qk)]
        c = jnp.tile(c_ref[...], (1, qk // LANES))
        s = jnp.tile(s_ref[...], (1, qk // LANES))
        qk_ref[...] = (xv * c + _rot_half_pairs(xv) * s).astype(qk_ref.dtype)
        v_ref[...] = x_ref[:, pl.ds(qk, kv)].astype(v_ref.dtype)

    tab = pl.BlockSpec((tm, LANES), lambda i: (i, 0))
    return pl.pallas_call(
        body, name=name, grid=(L // tm,), in_specs=[pl.BlockSpec((tm, width), lambda i: (i, 0)), tab, tab],
        out_specs=[pl.BlockSpec((tm, qk), lambda i: (i, 0)), pl.BlockSpec((tm, kv), lambda i: (i, 0))],
        out_shape=[SDS((L, qk), MMT), SDS((L, kv), MMT)], compiler_params=_cparams("parallel"))(qkv, cos, sin)


def _rope_bwd(dqk, dv, cos, sin, tm, name):
    L, qk = dqk.shape
    kv = dv.shape[1]

    def body(d_ref, dv_ref, c_ref, s_ref, o_ref):
        dv_ = d_ref[...]
        c = jnp.tile(c_ref[...], (1, qk // LANES))
        s = jnp.tile(s_ref[...], (1, qk // LANES))
        o_ref[:, pl.ds(0, qk)] = (dv_ * c - _rot_half_pairs(dv_ * s)).astype(o_ref.dtype)
        o_ref[:, pl.ds(qk, kv)] = dv_ref[...].astype(o_ref.dtype)

    tab = pl.BlockSpec((tm, LANES), lambda i: (i, 0))
    return pl.pallas_call(
        body, name=name, grid=(L // tm,),
        in_specs=[pl.BlockSpec((tm, qk), lambda i: (i, 0)), pl.BlockSpec((tm, kv), lambda i: (i, 0)), tab, tab],
        out_specs=pl.BlockSpec((tm, qk + kv), lambda i: (i, 0)), out_shape=SDS((L, qk + kv), MMT),
        compiler_params=_cparams("parallel"))(dqk, dv, cos, sin)


def _band_specs(nb, width):
    blk = lambda f: pl.BlockSpec((None, ATTN_BLOCK, width), f)
    return [blk(lambda h, n: (h, jnp.maximum(n - 1, 0), 0)), blk(lambda h, n: (h, n, 0)),
            blk(lambda h, n: (h, jnp.minimum(n + 1, nb - 1), 0))]


def _window_mask(n, L):
    qi = lax.broadcasted_iota(jnp.int32, (ATTN_BLOCK, 3 * ATTN_BLOCK), 0)
    kk = lax.broadcasted_iota(jnp.int32, (ATTN_BLOCK, 3 * ATTN_BLOCK), 1)
    key_abs = (n - 1) * ATTN_BLOCK + kk
    return (jnp.abs(qi + ATTN_BLOCK - kk) <= ATTN_BLOCK) & (key_abs >= 0) & (key_abs < L)


def _attn_fwd(q, k, v, kc, vc, sink, name):
    nkv, _, L, hd = q.shape
    C = kc.shape[1]
    nb = L // ATTN_BLOCK
    scale = HEAD_DIM ** -0.5

    def body(sink_ref, q_ref, k0, k1, k2, v0, v1, v2, kc_ref, vc_ref, o_ref, lse_ref):
        hh, n = pl.program_id(0), pl.program_id(1)
        kw = jnp.concatenate([k0[...], k1[...], k2[...]], axis=0)
        vw = jnp.concatenate([v0[...], v1[...], v2[...]], axis=0)
        mask = _window_mask(n, L)
        for g in range(Q_PER_KV):
            qg = q_ref[g]
            sw = jnp.where(mask, _dot(qg, kw, 1, 1) * scale, NEG)
            sc = _dot(qg, kc_ref[...], 1, 1) * scale
            sk = sink_ref[hh * Q_PER_KV + g]
            m = jnp.maximum(jnp.maximum(jnp.max(sw, axis=-1, keepdims=True), jnp.max(sc, axis=-1, keepdims=True)), sk)
            pw, pc = jnp.exp(sw - m), jnp.exp(sc - m)
            den = jnp.sum(pw, axis=-1, keepdims=True) + jnp.sum(pc, axis=-1, keepdims=True) + jnp.exp(sk - m)
            inv = 1.0 / den
            o_ref[g] = _dot(pw * inv, vw, 1, 0) + _dot(pc * inv, vc_ref[...], 1, 0)
            lse_ref[g] = m + jnp.log(den)

    qspec = pl.BlockSpec((None, Q_PER_KV, ATTN_BLOCK, hd), lambda h, n: (h, 0, n, 0))
    cspec = pl.BlockSpec((None, C, hd), lambda h, n: (h, 0, 0))
    return pl.pallas_call(
        body, name=name, grid=(nkv, nb),
        in_specs=[pl.BlockSpec(memory_space=pltpu.SMEM), qspec] + _band_specs(nb, hd) + _band_specs(nb, hd) + [cspec, cspec],
        out_specs=[qspec, pl.BlockSpec((None, Q_PER_KV, ATTN_BLOCK, 1), lambda h, n: (h, 0, n, 0))],
        out_shape=[SDS((nkv, Q_PER_KV, L, hd), F32), SDS((nkv, Q_PER_KV, L, 1), F32)],
        compiler_params=_cparams("parallel", "parallel"))(sink, q, k, k, k, v, v, v, kc, vc)


def _attn_bwd_q(q, k, v, kc, vc, sink, o, do, lse, name):
    nkv, _, L, hd = q.shape
    C = kc.shape[1]
    nb = L // ATTN_BLOCK
    scale = HEAD_DIM ** -0.5

    def body(sink_ref, q_ref, k0, k1, k2, v0, v1, v2, kc_ref, vc_ref, o_ref, do_ref, lse_ref, dq_ref, dkc_ref, dvc_ref, dsk_ref):
        hh, n = pl.program_id(0), pl.program_id(1)

        @pl.when(n == 0)
        def _():
            dkc_ref[...] = jnp.zeros_like(dkc_ref)
            dvc_ref[...] = jnp.zeros_like(dvc_ref)
            dsk_ref[...] = jnp.zeros_like(dsk_ref)

        kw = jnp.concatenate([k0[...], k1[...], k2[...]], axis=0)
        vw = jnp.concatenate([v0[...], v1[...], v2[...]], axis=0)
        mask = _window_mask(n, L)
        for g in range(Q_PER_KV):
            qg, dog, lse_g = q_ref[g], do_ref[g], lse_ref[g]
            delta = jnp.sum(dog.astype(F32) * o_ref[g], axis=-1, keepdims=True)
            pw = jnp.exp(jnp.where(mask, _dot(qg, kw, 1, 1) * scale, NEG) - lse_g)
            pc = jnp.exp(_dot(qg, kc_ref[...], 1, 1) * scale - lse_g)
            dsw = pw * (_dot(dog, vw, 1, 1) - delta)
            dsc = pc * (_dot(dog, vc_ref[...], 1, 1) - delta)
            dq_ref[g] = (_dot(dsw, kw, 1, 0) + _dot(dsc, kc_ref[...], 1, 0)) * scale
            dkc_ref[...] += _dot(dsc, qg, 0, 0) * scale
            dvc_ref[...] += _dot(pc, dog, 0, 0)
            psk = jnp.exp(sink_ref[hh * Q_PER_KV + g] - lse_g)
            dsk_ref[pl.ds(g, 1), :] += jnp.broadcast_to(jnp.sum(-psk * delta, axis=0, keepdims=True), (1, LANES))

    qspec = pl.BlockSpec((None, Q_PER_KV, ATTN_BLOCK, hd), lambda h, n: (h, 0, n, 0))
    lspec = pl.BlockSpec((None, Q_PER_KV, ATTN_BLOCK, 1), lambda h, n: (h, 0, n, 0))
    cspec = pl.BlockSpec((None, C, hd), lambda h, n: (h, 0, 0))
    return pl.pallas_call(
        body, name=name, grid=(nkv, nb),
        in_specs=[pl.BlockSpec(memory_space=pltpu.SMEM), qspec] + _band_specs(nb, hd) + _band_specs(nb, hd)
        + [cspec, cspec, qspec, qspec, lspec],
        out_specs=[qspec, cspec, cspec, pl.BlockSpec((None, SUBLANES, LANES), lambda h, n: (h, 0, 0))],
        out_shape=[SDS((nkv, Q_PER_KV, L, hd), F32), SDS((nkv, C, hd), F32), SDS((nkv, C, hd), F32),
                   SDS((nkv, SUBLANES, LANES), F32)],
        compiler_params=_cparams("parallel", "arbitrary"))(sink, q, k, k, k, v, v, v, kc, vc, o, do, lse)


def _attn_bwd_kv(q, k, v, o, do, lse, name):
    nkv, _, L, hd = q.shape
    nb = L // ATTN_BLOCK
    scale = HEAD_DIM ** -0.5

    def body(q0, q1, q2, do0, do1, do2, o0, o1, o2, l0, l1, l2, k_ref, v_ref, dk_ref, dv_ref):
        j = pl.program_id(1)
        qi = lax.broadcasted_iota(jnp.int32, (ATTN_BLOCK, ATTN_BLOCK), 0)
        kk = lax.broadcasted_iota(jnp.int32, (ATTN_BLOCK, ATTN_BLOCK), 1)
        kj, vj = k_ref[...], v_ref[...]
        dk = jnp.zeros((ATTN_BLOCK, hd), F32)
        dv = jnp.zeros((ATTN_BLOCK, hd), F32)
        for slot, (q_r, do_r, o_r, l_r) in enumerate(((q0, do0, o0, l0), (q1, do1, o1, l1), (q2, do2, o2, l2))):
            n = j - 1 + slot
            ok = (n >= 0) & (n < nb) & (jnp.abs(qi + ATTN_BLOCK - ((2 - slot) * ATTN_BLOCK + kk)) <= ATTN_BLOCK)
            for g in range(Q_PER_KV):
                qg, dog = q_r[g], do_r[g]
                delta = jnp.sum(dog.astype(F32) * o_r[g], axis=-1, keepdims=True)
                p = jnp.exp(jnp.where(ok, _dot(qg, kj, 1, 1) * scale - l_r[g], NEG))
                ds = p * (_dot(dog, vj, 1, 1) - delta)
                dk = dk + _dot(ds, qg, 0, 0) * scale
                dv = dv + _dot(p, dog, 0, 0)
        dk_ref[...] = dk
        dv_ref[...] = dv

    def band(width):
        blk = lambda f: pl.BlockSpec((None, Q_PER_KV, ATTN_BLOCK, width), f)
        return [blk(lambda h, j: (h, 0, jnp.maximum(j - 1, 0), 0)), blk(lambda h, j: (h, 0, j, 0)),
                blk(lambda h, j: (h, 0, jnp.minimum(j + 1, nb - 1), 0))]

    kspec = pl.BlockSpec((None, ATTN_BLOCK, hd), lambda h, j: (h, j, 0))
    return pl.pallas_call(
        body, name=name, grid=(nkv, nb), in_specs=band(hd) + band(hd) + band(hd) + band(1) + [kspec, kspec],
        out_specs=[kspec, kspec], out_shape=[SDS((nkv, L, hd), F32), SDS((nkv, L, hd), F32)],
        compiler_params=_cparams("parallel", "parallel"))(q, q, q, do, do, do, o, o, o, lse, lse, lse, k, v)


_GELU_K = math.sqrt(2.0 / math.pi)


def _gelu(v):
    return 0.5 * v * (1.0 + jnp.tanh(_GELU_K * (v + 0.044715 * (v * v * v))))


def _gelu_grad(v):
    t = jnp.tanh(_GELU_K * (v + 0.044715 * (v * v * v)))
    return 0.5 * (1.0 + t) + 0.5 * v * (1.0 - t * t) * (_GELU_K * (1.0 + 3.0 * 0.044715 * (v * v)))


def _gmlp_fwd(p0, b_in, ln_g, ln_b, w_s, b_s, name):
    L, W2 = p0.shape
    W = W2 // 2
    G = W // GMLP_GROUP_DIM

    def body(p_ref, bi_ref, g_ref, b_ref, ws_ref, bs_ref, o_ref):
        ge = _gelu(p_ref[...] + bi_ref[...])
        xh, _ = _layer_norm_stats(ge[:, W:])
        vln = xh * g_ref[...] + b_ref[...]
        for gi in range(G):
            cols = slice(gi * GMLP_GROUP_DIM, (gi + 1) * GMLP_GROUP_DIM)
            s = _dot(ws_ref[gi], vln[:, cols], 1, 0) + bs_ref[gi]
            o_ref[:, cols] = (ge[:, cols] * s).astype(o_ref.dtype)

    full = lambda shape: pl.BlockSpec(shape, lambda i: (0,) * len(shape))
    return pl.pallas_call(
        body, name=name, grid=(L // GMLP_CHUNK,),
        in_specs=[pl.BlockSpec((GMLP_CHUNK, W2), lambda i: (i, 0)), full((1, W2)), full((1, W)), full((1, W)),
                  full((G, GMLP_CHUNK, GMLP_CHUNK)), full((G, GMLP_CHUNK, 1))],
        out_specs=pl.BlockSpec((GMLP_CHUNK, W), lambda i: (i, 0)), out_shape=SDS((L, W), MMT),
        compiler_params=_cparams("parallel"))(p0, b_in, ln_g, ln_b, w_s, b_s)


def _gmlp_bwd(p0, dus, b_in, ln_g, ln_b, w_s, w_st, b_s, name):
    L, W2 = p0.shape
    W = W2 // 2
    G = W // GMLP_GROUP_DIM

    def body(p_ref, d_ref, bi_ref, g_ref, b_ref, ws_ref, wst_ref, bs_ref, dpre_ref, dbi_ref, dg_ref, db_ref, dws_ref, dbs_ref, dvln):
        @pl.when(pl.program_id(0) == 0)
        def _():
            for ref in (dbi_ref, dg_ref, db_ref, dws_ref, dbs_ref):
                ref[...] = jnp.zeros_like(ref)

        pre = p_ref[...] + bi_ref[...]
        ge = _gelu(pre)
        xh, rstd = _layer_norm_stats(ge[:, W:])
        vln = xh * g_ref[...] + b_ref[...]
        dge_u = []
        for gi in range(G):
            cols = slice(gi * GMLP_GROUP_DIM, (gi + 1) * GMLP_GROUP_DIM)
            vg = vln[:, cols]
            s = _dot(ws_ref[gi], vg, 1, 0) + bs_ref[gi]
            dus_g = d_ref[:, cols]
            dge_u.append(dus_g * s)
            ds = dus_g * ge[:, cols]
            dbs_ref[gi] += jnp.sum(ds, axis=1, keepdims=True)
            dws_ref[gi] += _dot(ds, vg, 1, 1)
            dvln[:, cols] = _dot(wst_ref[gi], ds, 1, 0)
        dv = dvln[...]
        dg_ref[...] += _sum0(dv * xh)
        db_ref[...] += _sum0(dv)
        dxh = dv * g_ref[...]
        dv0 = rstd * (dxh - jnp.mean(dxh, axis=-1, keepdims=True) - xh * jnp.mean(dxh * xh, axis=-1, keepdims=True))
        dpre = jnp.concatenate(dge_u + [dv0], axis=1) * _gelu_grad(pre)
        dbi_ref[...] += _sum0(dpre)
        dpre_ref[...] = dpre.astype(dpre_ref.dtype)

    full = lambda shape: pl.BlockSpec(shape, lambda i: (0,) * len(shape))
    mats = (G, GMLP_CHUNK, GMLP_CHUNK)
    return pl.pallas_call(
        body, name=name, grid=(L // GMLP_CHUNK,),
        in_specs=[pl.BlockSpec((GMLP_CHUNK, W2), lambda i: (i, 0)), pl.BlockSpec((GMLP_CHUNK, W), lambda i: (i, 0)),
                  full((1, W2)), full((1, W)), full((1, W)), full(mats), full(mats), full((G, GMLP_CHUNK, 1))],
        out_specs=[pl.BlockSpec((GMLP_CHUNK, W2), lambda i: (i, 0)), full((1, W2)), full((1, W)), full((1, W)), full(mats),
                   full((G, GMLP_CHUNK, 1))],
        out_shape=[SDS((L, W2), MMT), SDS((1, W2), F32), SDS((1, W), F32), SDS((1, W), F32), SDS(mats, F32),
                   SDS((G, GMLP_CHUNK, 1), F32)],
        scratch_shapes=[pltpu.VMEM((GMLP_CHUNK, W), F32)], compiler_params=_cparams("arbitrary"))(
            p0, dus, b_in, ln_g, ln_b, w_s, w_st, b_s)


def _loss_head(h, target, tm, name):
    L, D = h.shape

    def body(h_ref, t_ref, l_ref, d_ref):
        @pl.when(pl.program_id(0) == 0)
        def _():
            l_ref[...] = jnp.zeros_like(l_ref)

        e = h_ref[...] - t_ref[...]
        l_ref[...] += 0.5 * jnp.sum(jnp.mean(e * e, axis=-1, keepdims=True), axis=0, keepdims=True)
        d_ref[...] = e * (1.0 / D)

    row = pl.BlockSpec((tm, D), lambda i: (i, 0))
    return pl.pallas_call(body, name=name, grid=(L // tm,), in_specs=[row, row],
                          out_specs=[pl.BlockSpec((1, 1), lambda i: (0, 0)), row],
                          out_shape=[SDS((1, 1), F32), SDS((L, D), F32)], compiler_params=_cparams("arbitrary"))(h, target)


def _ada_fwd(cond, ada_w, ada_b, name):
    NL, D, n = ada_w.shape
    tn = _tile(n, 768)

    def body(c_ref, w_ref, b_ref, o_ref):
        cv = c_ref[...]
        o_ref[...] = _dot(cv * _sig(cv), w_ref[...], 1, 0) + b_ref[...]

    return pl.pallas_call(
        body, name=name, grid=(NL, n // tn),
        in_specs=[pl.BlockSpec((2 * SUBLANES, D), lambda i, j: (0, 0)), pl.BlockSpec((None, D, tn), lambda i, j: (i, 0, j)),
                  pl.BlockSpec((None, 1, tn), lambda i, j: (i, 0, j))],
        out_specs=pl.BlockSpec((None, 2 * SUBLANES, tn), lambda i, j: (i, 0, j)), out_shape=SDS((NL, 2 * SUBLANES, n), F32),
        compiler_params=_cparams("parallel", "parallel"))(cond, ada_w, ada_b)


def _ada_bwd(cond, ada_w, dm_lat, dm_ctx, name):
    NL, D, n = ada_w.shape
    tn = _tile(n, 768)

    def body(c_ref, w_ref, dl_ref, dc_ref, dw_ref, ds_ref):
        @pl.when((pl.program_id(0) == 0) & (pl.program_id(1) == 0))
        def _():
            ds_ref[...] = jnp.zeros_like(ds_ref)

        cv = c_ref[...]
        row = lax.broadcasted_iota(jnp.int32, (SUBLANES, tn), 0)
        ctx_rows = jnp.where(row == 0, _sum0(dc_ref[...]), 0.0)
        dm = jnp.concatenate([dl_ref[...], ctx_rows], axis=0)
        dw_ref[...] = _dot(cv * _sig(cv), dm, 0, 0)
        ds_ref[...] += _dot(dm, w_ref[...], 1, 1)

    dspec = pl.BlockSpec((None, SUBLANES, tn), lambda i, j: (i, 0, j))
    return pl.pallas_call(
        body, name=name, grid=(NL, n // tn),
        in_specs=[pl.BlockSpec((2 * SUBLANES, D), lambda i, j: (0, 0)), pl.BlockSpec((None, D, tn), lambda i, j: (i, 0, j)),
                  dspec, dspec],
        out_specs=[pl.BlockSpec((None, D, tn), lambda i, j: (i, 0, j)), pl.BlockSpec((2 * SUBLANES, D), lambda i, j: (0, 0))],
        out_shape=[SDS((NL, D, n), F32), SDS((2 * SUBLANES, D), F32)],
        compiler_params=_cparams("arbitrary", "arbitrary"))(cond, ada_w, dm_lat, dm_ctx)


def _adam_math(w, g, m, v):
    m = ADAM_B1 * m + (1.0 - ADAM_B1) * g
    v = ADAM_B2 * v + (1.0 - ADAM_B2) * jnp.square(g)
    m_hat = m / (1.0 - ADAM_B1 ** ADAM_STEP)
    v_hat = v / (1.0 - ADAM_B2 ** ADAM_STEP)
    return -ADAM_LR * (m_hat / (jnp.sqrt(v_hat) + ADAM_EPS) + ADAM_WD * w), m, v


def _row_tile(rows, cols, elems, mult=SUBLANES):
    want = max(mult, elems // cols)
    best = mult if rows % mult == 0 else rows
    for d in range(mult, min(rows, want) + 1, mult):
        if rows % d == 0:
            best = d
    return best


def _adamw(w, m, v, parts, name):
    R, C = w.shape
    tr = _row_tile(R, C, 128 * 1024)
    npart = len(parts)

    def body(*refs):
        w_ref, m_ref, v_ref = refs[:3]
        g_ref, d_ref, nm_ref, nv_ref = refs[3 + npart:]
        g = refs[3][...]
        for p_ref in refs[4:3 + npart]:
            g = g + p_ref[...]
        d, nm, nv = _adam_math(w_ref[...], g, m_ref[...], v_ref[...])
        g_ref[...], d_ref[...], nm_ref[...], nv_ref[...] = g, d, nm, nv

    blk = pl.BlockSpec((tr, C), lambda i: (i, 0))
    return pl.pallas_call(body, name=name, grid=(R // tr,), in_specs=[blk] * (3 + npart), out_specs=[blk] * 4,
                          out_shape=[SDS((R, C), F32)] * 4, compiler_params=_cparams("parallel"))(w, m, v, *parts)


def _adamw_layer(w, m, v, layer, parts, prev, name):
    _, R, C = w.shape
    tr = _row_tile(R, C, 128 * 1024, 2 * SUBLANES)
    npart = len(parts)
    nprev = 0 if prev is None else 4

    def body(*refs):
        w_ref, m_ref, v_ref = refs[:3]
        g_ref, d_ref, nm_ref, nv_ref = refs[3 + npart + nprev:]
        g = refs[3][...].astype(F32)
        for p_ref in refs[4:3 + npart]:
            g = g + p_ref[...].astype(F32)
        d, nm, nv = _adam_math(w_ref[...], g, m_ref[...], v_ref[...])
        g_ref[...], d_ref[...], nm_ref[...], nv_ref[...] = g, d, nm, nv

    stacked = pl.BlockSpec((None, tr, C), lambda i: (layer, i, 0))
    flat = pl.BlockSpec((tr, C), lambda i: (i, 0))
    return pl.pallas_call(
        body, name=name, grid=(R // tr,),
        in_specs=[stacked] * 3 + [flat] * npart + [pl.BlockSpec(memory_space=pl.ANY)] * nprev, out_specs=[stacked] * 4,
        out_shape=[SDS(w.shape, F32)] * 4, input_output_aliases={3 + npart + k: k for k in range(nprev)},
        compiler_params=_cparams("parallel"))(w, m, v, *parts, *(prev or ()))


def _sum_slots(x, name, after=None, out_dtype=F32):
    S, R, C = x.shape
    tr = _row_tile(R, C, 128 * 1024, SUBLANES * 4 // jnp.dtype(x.dtype).itemsize)
    extra = [] if after is None else [after]

    def body(x_ref, *rest):
        o_ref = rest[-1]
        acc = x_ref[0].astype(F32)
        for s in range(1, S):
            acc = acc + x_ref[s].astype(F32)
        o_ref[...] = acc.astype(o_ref.dtype)

    return pl.pallas_call(
        body, name=name, grid=(R // tr,),
        in_specs=[pl.BlockSpec((S, tr, C), lambda i: (0, i, 0))] + [pl.BlockSpec(memory_space=pl.ANY)] * len(extra),
        out_specs=pl.BlockSpec((tr, C), lambda i: (i, 0)), out_shape=SDS((R, C), out_dtype),
        compiler_params=_cparams("parallel"))(x, *extra)


def _my_place():
    return lax.axis_index("x"), lax.axis_index("y"), lax.axis_index("c")


def _other_chips(x, y):
    return [(1 - x, y), (x, 1 - y), (1 - x, 1 - y)]


def _all_gather(v, name):
    R, C = v.shape

    def body(v_ref, o_ref, send_sems, recv_sems, local_sem):
        x, y, c = _my_place()
        me = 4 * x + 2 * y + c
        mine = pltpu.make_async_copy(v_ref, o_ref.at[me], local_sem)
        mine.start()
        copies = []
        for flip in range(1, N_DEV):
            fx, fy, fc = (flip >> 2) & 1, (flip >> 1) & 1, flip & 1
            peer = ((x + fx) % 2, (y + fy) % 2, (c + fc) % 2)
            cp = pltpu.make_async_remote_copy(src_ref=v_ref, dst_ref=o_ref.at[me], send_sem=send_sems.at[flip - 1],
                                              recv_sem=recv_sems.at[flip - 1], device_id=peer, device_id_type=MESH)
            cp.start()
            copies.append(cp)
        for cp in copies:
            cp.wait()
        mine.wait()

    return pl.pallas_call(
        body, name=name, in_specs=[pl.BlockSpec(memory_space=pl.ANY)], out_specs=pl.BlockSpec(memory_space=pl.ANY),
        out_shape=SDS((N_DEV, R, C), v.dtype),
        scratch_shapes=[pltpu.SemaphoreType.DMA((N_DEV - 1,)), pltpu.SemaphoreType.DMA((N_DEV - 1,)), pltpu.SemaphoreType.DMA],
        )(v)


def _shard_window(ref, axis, j, size):
    idx = [slice(None)] * len(ref.shape)
    idx[axis] = pl.ds(pl.multiple_of(j * size, SUBLANES), size)
    return ref.at[tuple(idx)]


def _gather_plan(axis):
    def plan(srcs, lands):
        x, y, c = _my_place()
        shard, whole = srcs[0], lands[0]
        half = shard.shape[0] // 2
        size = shard.shape[axis]

        def window(chip, which):
            if axis == 1:
                return whole.at[pl.ds(pl.multiple_of(which * half, SUBLANES), half), pl.ds(pl.multiple_of(chip * size, LANES), size)]
            return whole.at[pl.ds(pl.multiple_of(chip * size + which * half, SUBLANES), half), :]

        j = 2 * x + y
        local = [(shard, _shard_window(whole, axis, j, size))]
        mine = shard.at[pl.ds(pl.multiple_of(c * half, SUBLANES), half), :]
        remote = [(mine, window(j, c), (px, py, c)) for px, py in _other_chips(x, y)]
        forward = [(window(2 * px + py, c), window(2 * px + py, c), (x, y, 1 - c)) for px, py in _other_chips(x, y)]
        return local, remote, forward
    return plan


def _scatter_plan(axis):
    def plan(srcs, lands):
        x, y, c = _my_place()
        j = 2 * x + y
        size = srcs[0].shape[axis] // N_CHIPS
        local = [(_shard_window(srcs[0], axis, j, size), lands[0].at[j])]
        remote = [(_shard_window(srcs[0], axis, 2 * px + py, size), lands[0].at[j], (px, py, c)) for px, py in _other_chips(x, y)]
        return local, remote, []
    return plan


def _all_gather_plan(srcs, lands):
    x, y, c = _my_place()
    dst = lands[0].at[4 * x + 2 * y + c]
    remote = []
    for flip in range(1, N_DEV):
        fx, fy, fc = (flip >> 2) & 1, (flip >> 1) & 1, flip & 1
        remote.append((srcs[0], dst, ((x + fx) % 2, (y + fy) % 2, (c + fc) % 2)))
    return [(srcs[0], dst)], remote, []


def _same_core_peers():
    x, y, c = _my_place()
    return [(px, py, c) for px, py in _other_chips(x, y)]


def _same_core_peers_and_sibling():
    x, y, c = _my_place()
    return _same_core_peers() + [(x, y, 1 - c)]


def _all_peers():
    x, y, c = _my_place()
    return [((x + (f >> 2 & 1)) % 2, (y + (f >> 1 & 1)) % 2, (c + (f & 1)) % 2) for f in range(1, N_DEV)]


def _sequencer_exchange(name, collective_id, exchanges, peers_fn):
    hbm = pltpu.MemorySpace.HBM
    src_refs = [[jax.new_ref(s, memory_space=hbm) for s in e[0]] for e in exchanges]
    land_refs = [[jax.empty_ref(s, memory_space=hbm) for s in e[1]] for e in exchanges]
    first = [sum(e[3] for e in exchanges[:i]) for i in range(len(exchanges))]
    ncopy = sum(e[3] for e in exchanges)
    dma = pltpu.SemaphoreType.DMA

    @pl.kernel(mesh=plsc.ScalarSubcoreMesh(axis_name="sequencer", num_cores=N_SEQUENCERS), name=name,
               scratch_types=(dma((ncopy,)), dma((ncopy,)), dma((ncopy,)), dma((ncopy,)), dma),
               compiler_params=pltpu.CompilerParams(collective_id=collective_id))
    def launch(send_sems, recv_sems, onward_send_sems, onward_recv_sems, local_sem):
        me = lax.axis_index("sequencer")
        peers = peers_fn()
        barrier = pltpu.get_barrier_semaphore()
        for peer in peers:
            pl.semaphore_signal(barrier, inc=1, device_id=peer, device_id_type=MESH)
        pl.semaphore_wait(barrier, len(peers))
        plans = [e[2](src_refs[i], land_refs[i]) for i, e in enumerate(exchanges)]
        nbytes = lambda ref: math.prod(ref.shape) * jnp.dtype(ref.dtype).itemsize
        jobs = [(nbytes(dst), "local", (i, k)) for i, p in enumerate(plans) for k, (_, dst) in enumerate(p[0])]
        jobs += [(nbytes(src) * (2 if p[2] else 1), "remote", (i, k)) for i, p in enumerate(plans) for k, (src, _, _) in enumerate(p[1])]
        load, owner = [0] * N_SEQUENCERS, {}
        for size, kind, key in sorted(jobs, key=lambda job: -job[0]):
            owner[kind, key] = load.index(min(load))
            load[owner[kind, key]] += size
        for s in range(N_SEQUENCERS):
            @pl.when(me == s)
            def _(s=s):
                for i, (local, _, _) in enumerate(plans):
                    for k, (src, dst) in enumerate(local):
                        if owner["local", (i, k)] == s:
                            cp = pltpu.make_async_copy(src, dst, local_sem)
                            cp.start()
                            cp.wait()
                copies, onward = [], []
                for i, (_, remote, forward) in enumerate(plans):
                    assert len(remote) == exchanges[i][3] and len(forward) in (0, len(remote))
                    for k, (src, dst, peer) in enumerate(remote):
                        if owner["remote", (i, k)] == s:
                            cp = pltpu.make_async_remote_copy(src_ref=src, dst_ref=dst, send_sem=send_sems.at[first[i] + k],
                                                              recv_sem=recv_sems.at[first[i] + k], device_id=peer,
                                                              device_id_type=MESH)
                            cp.start()
                            copies.append(cp)
                            if forward:
                                src, dst, peer = forward[k]
                                onward.append(pltpu.make_async_remote_copy(
                                    src_ref=src, dst_ref=dst, send_sem=onward_send_sems.at[first[i] + k],
                                    recv_sem=onward_recv_sems.at[first[i] + k], device_id=peer, device_id_type=MESH))
                            else:
                                onward.append(None)
                for cp, on in zip(copies, onward):
                    cp.wait_recv()
                    if on is not None:
                        on.start()
                for cp, on in zip(copies, onward):
                    cp.wait_send()
                    if on is not None:
                        on.wait()

    launch()
    return [[r[...] for r in refs] for refs in land_refs]


def _swap_with_sibling(parts, name):
    nt = len(parts)

    def body(*refs):
        ins, outs = refs[:nt], refs[nt:2 * nt]
        send_sems, recv_sems = refs[2 * nt:]
        x, y, c = _my_place()
        copies = []
        for t in range(nt):
            cp = pltpu.make_async_remote_copy(src_ref=ins[t], dst_ref=outs[t], send_sem=send_sems.at[t], recv_sem=recv_sems.at[t],
                                              device_id=(x, y, 1 - c), device_id_type=MESH)
            cp.start()
            copies.append(cp)
        for cp in copies:
            cp.wait()

    any_spec = pl.BlockSpec(memory_space=pl.ANY)
    return pl.pallas_call(
        body, name=name, in_specs=[any_spec] * nt, out_specs=[any_spec] * nt, out_shape=[SDS(p.shape, p.dtype) for p in parts],
        scratch_shapes=[pltpu.SemaphoreType.DMA((nt,)), pltpu.SemaphoreType.DMA((nt,))],
        )(*parts)


PACK_COLS = 1024


def _pack(arrays):
    flat = jnp.concatenate([a.reshape(-1) for a in arrays])
    pad = (-flat.shape[0]) % (2 * SUBLANES * PACK_COLS)
    return jnp.pad(flat, (0, pad)).reshape(-1, PACK_COLS)


def _unpack(packed, shapes):
    flat, out, pos = packed.reshape(-1), [], 0
    for shape in shapes:
        n = math.prod(shape)
        out.append(flat[pos:pos + n].reshape(shape))
        pos += n
    return out


def _unshard_last(stacked):
    moved = jnp.moveaxis(stacked, 0, -2)
    return moved.reshape(moved.shape[:-2] + (moved.shape[-2] * moved.shape[-1],))


def _my_block_last(full, j):
    s = full.shape[-1] // N_CHIPS
    return lax.dynamic_index_in_dim(full.reshape(full.shape[:-1] + (N_CHIPS, s)), j, axis=full.ndim - 1, keepdims=False)


def _rope_tables(L):
    rows = L // GRID_W
    row = jnp.repeat(jnp.arange(rows), GRID_W).astype(F32)
    col = jnp.tile(jnp.arange(GRID_W), rows).astype(F32)
    axis_dim = HEAD_DIM // 2
    inv_freq = ROPE_BASE ** (-jnp.arange(0, axis_dim, 2, dtype=F32) / axis_dim)
    ang_r, ang_c = row[:, None] * inv_freq[None, :], col[:, None] * inv_freq[None, :]
    ang = jnp.concatenate([ang_r, ang_r, ang_c, ang_c] * 2, axis=-1)
    return jnp.cos(ang), jnp.sin(ang)


SMALL_SHARDED = ("norm_g", "ffn_conv_w", "cm_b_in", "cm_dw_w", "cm_dw_b", "cm_ln_g", "cm_ln_b", "cm_b_out", "gm_b_in", "gm_ln_g",
                 "gm_ln_b")
SMALL_REPLICATED = ("c_ctx", "ada_b", "ffn_conv_b", "attn_sink", "gm_w_s", "gm_b_s")
BIG = ("ffn_w_up", "ffn_w_down", "cm_w_in", "cm_w_out", "attn_w_qkv", "attn_w_o", "gm_w_in", "gm_w_out")
BIG_AXIS = {"ffn_w_up": 2, "ffn_w_down": 1, "cm_w_in": 2, "cm_w_out": 1, "attn_w_qkv": 2, "attn_w_o": 1, "gm_w_in": 2, "gm_w_out": 1}
WEIGHTS = ("c_ctx", "ada_w", "ada_b", "norm_g", "ffn_w_up", "ffn_conv_w", "ffn_conv_b", "ffn_w_down", "cm_w_in", "cm_b_in",
           "cm_dw_w", "cm_dw_b", "cm_ln_g", "cm_ln_b", "cm_w_out", "cm_b_out", "attn_w_qkv", "attn_sink", "attn_w_o", "gm_w_in",
           "gm_b_in", "gm_ln_g", "gm_ln_b", "gm_w_s", "gm_b_s", "gm_w_out")


def _step(x, c, ctx, target, W, M, V):
    L, D = x.shape[1], x.shape[2]
    C = ctx.shape[1]
    T = L + C
    NL = W["ada_w"].shape[0]
    tm = 256 if C % 256 == 0 else 128
    nl = L // tm
    xi, yi, ci = _my_place()
    chip = 2 * xi + yi
    dev = 4 * xi + 2 * yi + ci
    segs2, segs1 = [(0, L), (L, C)], [(0, L)]
    vec = lambda a: a.reshape(1, -1)

    layer_sets = [[("cm_w_in", 0), ("cm_w_out", 0), ("ffn_w_up", 0), ("ffn_w_down", 0)],
                  [("attn_w_qkv", 0), ("attn_w_o", 0), ("ffn_w_up", 1), ("ffn_w_down", 1)],
                  [("gm_w_in", 0), ("gm_w_out", 0), ("ffn_w_up", 2), ("ffn_w_down", 2)],
                  [("cm_w_in", 1), ("cm_w_out", 1), ("ffn_w_up", 3), ("ffn_w_down", 3)]]
    arrived = {}

    def fetch(keys, zero, sequencer_id):
        exchanges = []
        for n, i in keys:
            shard = (W[n][i] + zero).astype(MMT)
            whole = list(shard.shape)
            whole[BIG_AXIS[n] - 1] *= N_CHIPS
            exchanges.append(([shard], [SDS(tuple(whole), MMT)], _gather_plan(BIG_AXIS[n] - 1), N_CHIPS - 1))
        lands = _sequencer_exchange(f"fetch_weights_{keys[0][0]}_{keys[0][1]}", sequencer_id, exchanges,
                                    _same_core_peers_and_sibling)
        for key, land in zip(keys, lands):
            arrived[key] = land[0]

    def big(n, i, after=None):
        return arrived[(n, i)]

    small_shapes = [W[n].shape for n in SMALL_SHARDED]
    ag1 = _all_gather(_pack([c.reshape(-1)] + [W[n] for n in SMALL_SHARDED]), "gather_small")
    parts = [_unpack(ag1[2 * s], [(D,)] + small_shapes) for s in range(N_CHIPS)]
    c_rows = jnp.stack([_unpack(ag1[d], [(D,)])[0] for d in range(N_DEV)])
    P = {n: _unshard_last(jnp.stack([parts[s][1 + i] for s in range(N_CHIPS)])) for i, n in enumerate(SMALL_SHARDED)}
    for n in SMALL_REPLICATED:
        P[n] = W[n]

    cond = jnp.concatenate([c_rows, W["c_ctx"][None, :], jnp.zeros((2 * SUBLANES - N_DEV - 1, D), F32)], axis=0)
    ncol = W["ada_w"].shape[2]
    ada_b_mine = lax.dynamic_slice_in_dim(W["ada_b"], chip * ncol, ncol, axis=1)[:, None, :]
    mods_mine = _ada_fwd(cond, W["ada_w"], ada_b_mine, "ada_fwd")
    ag2 = _all_gather(mods_mine.reshape(NL * 2 * SUBLANES, ncol), "gather_mods").reshape(N_DEV, NL, 2 * SUBLANES, ncol)
    mods_all = _unshard_last(jnp.stack([ag2[2 * s] for s in range(N_CHIPS)]))
    mod_lat = lax.dynamic_index_in_dim(mods_all, dev, axis=1, keepdims=False).reshape(NL, 6, D)
    mod_ctx = mods_all[:, N_DEV].reshape(NL, 6, D)
    mod2 = jnp.stack([mod_lat, mod_ctx], axis=1)
    mod1 = mod_lat[:, None]

    corner = mod2[0, 0, 0, 0]
    behind_small = jnp.where(corner != corner, corner, 0.0)
    for k, keys in enumerate([part for layer in layer_sets for part in (layer[:2], layer[2:])]):
        fetch(keys, behind_small, FETCH_IDS[k])
    zero_d = jnp.zeros((1, D), F32)
    cos, sin = _rope_tables(L)
    nkv = D // HEAD_DIM // Q_PER_KV
    qdim, kvdim = D, nkv * HEAD_DIM

    def ffn_fwd(i, h, mod, rows, segs, tag):
        a2 = _prenorm(h, mod, vec(P["norm_g"][i, 2]), 1, rows, nl, tm, f"pre_ffn_{tag}")
        z0 = _mm(a2, big("ffn_w_up", i, a2), "nn", F32, f"ffn_up_{tag}")
        u = _ffn_gate(z0, P["ffn_conv_w"][i], vec(P["ffn_conv_b"][i]), segs, f"ffn_gate_{tag}")
        f = _mm(u, big("ffn_w_down", i, u), "nn", F32, f"ffn_down_{tag}")
        h_out = _postnorm(h, f, zero_d, mod, vec(P["norm_g"][i, 3]), 5, rows, nl, tm, f"post_ffn_{tag}")
        return h_out, dict(h=h, a2=a2, z0=z0, f=f)

    def ffn_bwd(i, dh, sv, mod, rows, segs, tag, G):
        df, dg2, dgn3, _ = _postnorm_bwd(dh, sv["f"], zero_d, mod, vec(P["norm_g"][i, 3]), 5, rows, nl, tm, f"post_ffn_bwd_{tag}")
        du = _mm(df, big("ffn_w_down", i), "nt", F32, f"ffn_down_dx_{tag}")
        u, dz0, dcw, dcb = _ffn_gate_bwd(sv["z0"], du, P["ffn_conv_w"][i], vec(P["ffn_conv_b"][i]), segs, f"ffn_gate_bwd_{tag}")
        G["ffn_w_down"][i] = _mm(u, df, "tn", MMT, f"ffn_down_dw_{tag}")
        G["ffn_w_up"][i] = _mm(sv["a2"], dz0, "tn", MMT, f"ffn_up_dw_{tag}")
        da2 = _mm(dz0, big("ffn_w_up", i), "nt", F32, f"ffn_up_dx_{tag}")
        dh, dsh2, dsc2, dgn2 = _prenorm_bwd(sv["h"], da2, dh, mod, vec(P["norm_g"][i, 2]), 1, rows, nl, tm, f"pre_ffn_bwd_{tag}")
        G["ffn_conv_w"][i], G["ffn_conv_b"][i] = dcw, dcb[0]
        return dh, (dsh2, dsc2, dg2), (dgn2, dgn3)

    def conformer_fwd(i, j, h, mod, rows, segs, tag):
        a = _prenorm(h, mod, vec(P["norm_g"][i, 0]), 0, rows, nl, tm, f"pre_mix_{tag}")
        p0 = _mm(a, big("cm_w_in", j, a), "nn", F32, f"cm_in_{tag}")
        z2 = _glu_conv(p0, vec(P["cm_b_in"][j]), P["cm_dw_w"][j], vec(P["cm_dw_b"][j]), segs, f"cm_conv_{tag}")
        z4 = _ln_silu(z2, vec(P["cm_ln_g"][j]), vec(P["cm_ln_b"][j]), rows, tm, f"cm_ln_{tag}")
        y = _mm(z4, big("cm_w_out", j, z4), "nn", F32, f"cm_out_{tag}")
        h_out = _postnorm(h, y, vec(P["cm_b_out"][j]), mod, vec(P["norm_g"][i, 1]), 2, rows, nl, tm, f"post_mix_{tag}")
        return h_out, dict(h=h, a=a, p0=p0, z2=z2, z4=z4, y=y)

    def conformer_bwd(i, j, dh, sv, mod, rows, segs, tag, G):
        dy, dg1, dgn1, dbo = _postnorm_bwd(dh, sv["y"], vec(P["cm_b_out"][j]) + zero_d, mod, vec(P["norm_g"][i, 1]), 2, rows, nl,
                                           tm, f"post_mix_bwd_{tag}")
        G["cm_w_out"][j] = _mm(sv["z4"], dy, "tn", MMT, f"cm_out_dw_{tag}")
        dz4 = _mm(dy, big("cm_w_out", j), "nt", F32, f"cm_out_dx_{tag}")
        dz2, dlg, dlb = _ln_silu_bwd(sv["z2"], dz4, vec(P["cm_ln_g"][j]), vec(P["cm_ln_b"][j]), rows, tm, f"cm_ln_bwd_{tag}")
        dpa, dpg, ddw, ddb, dba, dbg = _glu_conv_bwd(sv["p0"], vec(P["cm_b_in"][j]), P["cm_dw_w"][j], dz2, segs, f"cm_conv_bwd_{tag}")
        dp = jnp.concatenate([dpa, dpg], axis=1)
        G["cm_w_in"][j] = _mm(sv["a"], dp, "tn", MMT, f"cm_in_dw_{tag}")
        da = _mm(dp, big("cm_w_in", j), "nt", F32, f"cm_in_dx_{tag}")
        dh, dsh1, dsc1, dgn0 = _prenorm_bwd(sv["h"], da, dh, mod, vec(P["norm_g"][i, 0]), 0, rows, nl, tm, f"pre_mix_bwd_{tag}")
        G["cm_b_out"][j] = jnp.sum(dbo, axis=0)[0]
        G["cm_ln_g"][j], G["cm_ln_b"][j], G["cm_dw_w"][j], G["cm_dw_b"][j] = dlg[0], dlb[0], ddw, ddb[0]
        G["cm_b_in"][j] = jnp.concatenate([dba[0], dbg[0]])
        return dh, (dsh1, dsc1, dg1), (dgn0, dgn1)

    def heads(a, n):
        return a.reshape(a.shape[0], n, HEAD_DIM).transpose(1, 0, 2)

    def unheads(a):
        return a.transpose(1, 0, 2).reshape(a.shape[1], -1)

    G = {n: [None] * W[n].shape[0] for n in WEIGHTS if n not in ("c_ctx", "ada_w", "ada_b", "norm_g")}
    saved = []
    h = jnp.concatenate([x[0], ctx[0]], axis=0)
    h, s_mix = conformer_fwd(0, 0, h, mod2[0], T, segs2, "l0")
    h, s_ffn = ffn_fwd(0, h, mod2[0], T, segs2, "l0")
    saved.append((s_mix, s_ffn))
    a_all = _prenorm(h, mod2[1], vec(P["norm_g"][1, 0]), 0, T, nl, tm, "pre_mix_l1")
    qkv = _mm(a_all, big("attn_w_qkv", 0, a_all), "nn", F32, "attn_qkv")
    qk_rot, v_lat = _rope(qkv, cos, sin, L, qdim + kvdim, tm, "rope")
    q_h = heads(qk_rot[:, :qdim], nkv * Q_PER_KV).reshape(nkv, Q_PER_KV, L, HEAD_DIM)
    k_h, v_h = heads(qk_rot[:, qdim:], nkv), heads(v_lat, nkv)
    kc_h = heads(qkv[L:, qdim:qdim + kvdim].astype(MMT), nkv)
    vc_h = heads(qkv[L:, qdim + kvdim:].astype(MMT), nkv)
    sink = P["attn_sink"][0]
    o_h, lse = _attn_fwd(q_h, k_h, v_h, kc_h, vc_h, sink, "attn")
    o_nat = unheads(o_h.reshape(nkv * Q_PER_KV, L, HEAD_DIM)).astype(MMT)
    y1 = _mm(o_nat, big("attn_w_o", 0, o_nat), "nn", F32, "attn_out")
    h_in1 = h
    h = _postnorm(h, y1, zero_d, mod1[1], vec(P["norm_g"][1, 1]), 2, L, nl, tm, "post_mix_l1")
    h, s_ffn1 = ffn_fwd(1, h, mod1[1], L, segs1, "lat")
    h_in2 = h
    a_2 = _prenorm(h, mod1[2], vec(P["norm_g"][2, 0]), 0, L, nl, tm, "pre_mix_l2")
    p0_2 = _mm(a_2, big("gm_w_in", 0, a_2), "nn", F32, "gm_in")
    ws_bf = P["gm_w_s"][0].astype(MMT)
    bs_col = P["gm_b_s"][0][:, :, None]
    us = _gmlp_fwd(p0_2, vec(P["gm_b_in"][0]), vec(P["gm_ln_g"][0]), vec(P["gm_ln_b"][0]), ws_bf, bs_col, "gmlp")
    y2 = _mm(us, big("gm_w_out", 0, us), "nn", F32, "gm_out")
    h = _postnorm(h, y2, zero_d, mod1[2], vec(P["norm_g"][2, 1]), 2, L, nl, tm, "post_mix_l2")
    h, s_ffn2 = ffn_fwd(2, h, mod1[2], L, segs1, "lat")
    h, s_mix3 = conformer_fwd(3, 1, h, mod1[3], L, segs1, "l3")
    h, s_ffn3 = ffn_fwd(3, h, mod1[3], L, segs1, "lat")

    loss_mine, dh = _loss_head(h, target[0], tm, "loss_head")

    dmod = [None] * NL
    dgn = [None] * NL

    def finish(i, mix, ffn, gns_mix, gns_ffn):
        dmod[i] = jnp.concatenate(list(mix) + list(ffn), axis=1)
        dgn[i] = jnp.stack([jnp.sum(g, axis=0)[0] for g in (gns_mix[0], gns_mix[1], gns_ffn[0], gns_ffn[1])])

    sent, so_far = {}, {}

    def send(tag, collective_id, tensors):
        exchanges = []
        for n, l in tensors:
            g = G[n][l]
            shard = list(g.shape)
            shard[BIG_AXIS[n] - 1] //= N_CHIPS
            exchanges.append(([g], [SDS((N_CHIPS,) + tuple(shard), g.dtype)], _scatter_plan(BIG_AXIS[n] - 1), N_CHIPS - 1))
        sent[tag] = (tensors, _sequencer_exchange(f"send_grads_{tag}", collective_id, exchanges, _same_core_peers))
        corner = sum(G[n][l][0:1, 0:1].astype(F32) for n, l in tensors)
        return jnp.where(corner != corner, corner, 0.0)

    def land(tag, after):
        tensors, landed = sent[tag]
        mine = [_sum_slots(lands[0], f"sum_chips_{n}_{l}", after, MMT) for (n, l), lands in zip(tensors, landed)]
        theirs = _swap_with_sibling(mine, f"swap_cores_{tag}")
        for (n, l), a, b in zip(tensors, mine, theirs):
            so_far[n] = _adamw_layer(W[n], M[n], V[n], l, [a, b], so_far.get(n), f"adamw_{n}_{l}")

    dh, m_ffn, n_ffn = ffn_bwd(3, dh, s_ffn3, mod1[3], L, segs1, "lat", G)
    dh, m_mix, n_mix = conformer_bwd(3, 1, dh, s_mix3, mod1[3], L, segs1, "l3", G)
    finish(3, m_mix, m_ffn, n_mix, n_ffn)
    zero_d = zero_d + send("l3", SEND_IDS[0], [("ffn_w_up", 3), ("ffn_w_down", 3), ("cm_w_in", 1), ("cm_w_out", 1)])
    land("l3", None)

    dh, m_ffn, n_ffn = ffn_bwd(2, dh, s_ffn2, mod1[2], L, segs1, "lat", G)
    dy2, dg1, dgn1, _ = _postnorm_bwd(dh, y2, zero_d, mod1[2], vec(P["norm_g"][2, 1]), 2, L, nl, tm, "post_mix_bwd_l2")
    G["gm_w_out"][0] = _mm(us, dy2, "tn", MMT, "gm_out_dw")
    dus = _mm(dy2, big("gm_w_out", 0), "nt", F32, "gm_out_dx")
    ws_t = jnp.swapaxes(P["gm_w_s"][0], 1, 2).astype(MMT)
    dpre, dbi, dlg, dlb, dws, dbs = _gmlp_bwd(p0_2, dus, vec(P["gm_b_in"][0]), vec(P["gm_ln_g"][0]), vec(P["gm_ln_b"][0]), ws_bf,
                                              ws_t, bs_col, "gmlp_bwd")
    G["gm_w_in"][0] = _mm(a_2, dpre, "tn", MMT, "gm_in_dw")
    da = _mm(dpre, big("gm_w_in", 0), "nt", F32, "gm_in_dx")
    dh, dsh1, dsc1, dgn0 = _prenorm_bwd(h_in2, da, dh, mod1[2], vec(P["norm_g"][2, 0]), 0, L, nl, tm, "pre_mix_bwd_l2")
    G["gm_b_in"][0], G["gm_ln_g"][0], G["gm_ln_b"][0], G["gm_w_s"][0], G["gm_b_s"][0] = dbi[0], dlg[0], dlb[0], dws, dbs[:, :, 0]
    finish(2, (dsh1, dsc1, dg1), m_ffn, (dgn0, dgn1), n_ffn)
    zero_d = zero_d + send("l2", SEND_IDS[1], [("ffn_w_up", 2), ("ffn_w_down", 2), ("gm_w_in", 0), ("gm_w_out", 0)])
    land("l2", None)

    dh, m_ffn, n_ffn = ffn_bwd(1, dh, s_ffn1, mod1[1], L, segs1, "lat", G)
    dy1, dg1, dgn1, _ = _postnorm_bwd(dh, y1, zero_d, mod1[1], vec(P["norm_g"][1, 1]), 2, L, nl, tm, "post_mix_bwd_l1")
    G["attn_w_o"][0] = _mm(o_nat, dy1, "tn", MMT, "attn_out_dw")
    do_nat = _mm(dy1, big("attn_w_o", 0), "nt", MMT, "attn_out_dx")
    do_h = heads(do_nat, nkv * Q_PER_KV).reshape(nkv, Q_PER_KV, L, HEAD_DIM)
    dq_h, dkc_h, dvc_h, dsk = _attn_bwd_q(q_h, k_h, v_h, kc_h, vc_h, sink, o_h, do_h, lse, "attn_bwd_q")
    dk_h, dv_h = _attn_bwd_kv(q_h, k_h, v_h, o_h, do_h, lse, "attn_bwd_kv")
    dqk = jnp.concatenate([unheads(dq_h.reshape(nkv * Q_PER_KV, L, HEAD_DIM)), unheads(dk_h)], axis=1)
    dqkv_lat = _rope_bwd(dqk, unheads(dv_h), cos, sin, tm, "rope_bwd")
    dqkv_ctx = jnp.concatenate([jnp.zeros((C, qdim), MMT), unheads(dkc_h).astype(MMT), unheads(dvc_h).astype(MMT)], axis=1)
    dqkv = jnp.concatenate([dqkv_lat, dqkv_ctx], axis=0)
    G["attn_w_qkv"][0] = _mm(a_all, dqkv, "tn", MMT, "attn_qkv_dw")
    da_all = _mm(dqkv, big("attn_w_qkv", 0), "nt", F32, "attn_qkv_dx")
    dh_all = jnp.concatenate([dh, jnp.zeros((C, D), F32)], axis=0)
    dh, dsh1, dsc1, dgn0 = _prenorm_bwd(h_in1, da_all, dh_all, mod2[1], vec(P["norm_g"][1, 0]), 0, T, nl, tm, "pre_mix_bwd_l1")
    G["attn_sink"][0] = dsk[:, :Q_PER_KV, 0].reshape(-1)
    pad_ctx = lambda a: jnp.concatenate([a, jnp.zeros_like(a)], axis=0)
    finish(1, (dsh1, dsc1, pad_ctx(dg1)), [pad_ctx(a) for a in m_ffn], (dgn0, dgn1), n_ffn)
    zero_d = zero_d + send("l1_ffn", SEND_IDS[2], [("ffn_w_up", 1), ("ffn_w_down", 1)])
    zero_d = zero_d + send("l1_mix", SEND_IDS[5], [("attn_w_qkv", 0), ("attn_w_o", 0)])
    land("l1_ffn", None)
    land("l1_mix", None)

    s_mix0, s_ffn0 = saved[0]
    dh, m_ffn, n_ffn = ffn_bwd(0, dh, s_ffn0, mod2[0], T, segs2, "l0", G)
    zero_d = zero_d + send("l0_ffn", SEND_IDS[3], [("ffn_w_up", 0), ("ffn_w_down", 0)])
    dh, m_mix, n_mix = conformer_bwd(0, 0, dh, s_mix0, mod2[0], T, segs2, "l0", G)
    finish(0, m_mix, m_ffn, n_mix, n_ffn)
    grad_x = dh[:L][None]
    sent_l0 = send("l0_mix", SEND_IDS[4], [("cm_w_in", 0), ("cm_w_out", 0)])

    for i in range(2, NL):
        dmod[i] = pad_ctx(dmod[i])
    dmod_all = jnp.stack(dmod).reshape(NL, 2, 6 * D) + sent_l0

    ag3 = _all_gather(dmod_all.reshape(NL * 2, 6 * D), "gather_dmods").reshape(N_DEV, NL, 2, N_CHIPS, ncol)
    dm_cols = lax.dynamic_index_in_dim(ag3, chip, axis=3, keepdims=False)
    dm_lat, dm_ctx = jnp.moveaxis(dm_cols[:, :, 0], 0, 1), jnp.moveaxis(dm_cols[:, :, 1], 0, 1)
    g_ada_w, dsilu = _ada_bwd(cond, W["ada_w"], dm_lat, dm_ctx, "ada_bwd")
    cc = W["c_ctx"]
    sg = jax.nn.sigmoid(cc)
    dcctx_part = jnp.where(ci == 0, 1.0, 0.0) * dsilu[N_DEV] * (sg * (1.0 + cc * (1.0 - sg)))

    Gs = {n: jnp.stack(G[n]) for n in G if n not in BIG}
    Gs["norm_g"] = jnp.stack(dgn)
    Gs["ada_b"] = jnp.sum(dmod_all, axis=1)
    Gs["c_ctx"] = dcctx_part
    small_names = list(SMALL_SHARDED) + list(SMALL_REPLICATED)
    small_full_shapes = [P[n].shape for n in small_names]
    small_pack = _pack([Gs[n] for n in small_names]).astype(MMT)
    ((ag4,),) = _sequencer_exchange("gather_small_grads", SMALL_GRADS_ID, [
        ([small_pack], [SDS((N_DEV,) + small_pack.shape, MMT)], _all_gather_plan, N_DEV - 1)], _all_peers)

    flat2 = lambda a: a.reshape(-1, a.shape[-1])
    res = {}
    outs = _adamw(flat2(W["ada_w"]), flat2(M["ada_w"]), flat2(V["ada_w"]), [flat2(g_ada_w)], "adamw_ada_w")
    res["ada_w"] = tuple(o.reshape(W["ada_w"].shape) for o in outs)

    land("l0_ffn", outs[0])
    land("l0_mix", so_far["ffn_w_up"][0])
    for n in BIG:
        res[n] = tuple(so_far[n])

    small_sum = _unpack(_sum_slots(ag4, "sum_small_grads"), small_full_shapes)
    g_small = {}
    for n, g in zip(small_names, small_sum):
        g_small[n] = _my_block_last(g, chip) if n in SMALL_SHARDED else g
    packed = [_pack([d[n] for n in small_names]) for d in (W, M, V)]
    outs_small = _adamw(packed[0], packed[1], packed[2], [_pack([g_small[n] for n in small_names])], "adamw_small")
    shard_shapes = [W[n].shape for n in small_names]
    for k, n in enumerate(small_names):
        res[n] = tuple(_unpack(o, shard_shapes)[k] for o in outs_small)

    loss = lax.psum(loss_mine[0, 0], ("x", "y", "c"))
    return (loss, grad_x) + tuple(res[n][k] for k in range(4) for n in WEIGHTS)


def kernel(x, c, ctx, c_ctx, ada_w, ada_b, norm_g, ffn_w_up, ffn_conv_w, ffn_conv_b, ffn_w_down, cm_w_in, cm_b_in, cm_dw_w, cm_dw_b, cm_ln_g, cm_ln_b, cm_w_out, cm_b_out, attn_w_qkv, attn_sink, attn_w_o, gm_w_in, gm_b_in, gm_ln_g, gm_ln_b, gm_w_s, gm_b_s, gm_w_out, loss_target, m_c_ctx, m_ada_w, m_ada_b, m_norm_g, m_ffn_w_up, m_ffn_conv_w, m_ffn_conv_b, m_ffn_w_down, m_cm_w_in, m_cm_b_in, m_cm_dw_w, m_cm_dw_b, m_cm_ln_g, m_cm_ln_b, m_cm_w_out, m_cm_b_out, m_attn_w_qkv, m_attn_sink, m_attn_w_o, m_gm_w_in, m_gm_b_in, m_gm_ln_g, m_gm_ln_b, m_gm_w_s, m_gm_b_s, m_gm_w_out, v_c_ctx, v_ada_w, v_ada_b, v_norm_g, v_ffn_w_up, v_ffn_conv_w, v_ffn_conv_b, v_ffn_w_down, v_cm_w_in, v_cm_b_in, v_cm_dw_w, v_cm_dw_b, v_cm_ln_g, v_cm_ln_b, v_cm_w_out, v_cm_b_out, v_attn_w_qkv, v_attn_sink, v_attn_w_o, v_gm_w_in, v_gm_b_in, v_gm_ln_g, v_gm_ln_b, v_gm_w_s, v_gm_b_s, v_gm_w_out):
    args = locals()
    W = {n: args[n] for n in WEIGHTS}
    M = {n: args["m_" + n] for n in WEIGHTS}
    V = {n: args["v_" + n] for n in WEIGHTS}
    return _step(x, c, ctx, loss_target, W, M, V)
```

```python
import functools
import math

import jax
import jax.numpy as jnp
from jax import lax
from jax.experimental import pallas as pl
from jax.experimental.pallas import tpu as pltpu
from jax.experimental.pallas import tpu_sc as plsc

F32 = jnp.float32
MMT = jnp.bfloat16
SDS = jax.ShapeDtypeStruct
MESH = pl.DeviceIdType.MESH

EPS = 1e-6
HEAD_DIM = 64
Q_PER_KV = 4
ATTN_BLOCK = 128
GRID_W = 64
ROPE_BASE = 10000.0
GMLP_CHUNK = 128
GMLP_GROUP_DIM = 128
CONV_WIDTH = 31
FFN_CONV_WIDTH = 3
NEG = -1e30

ADAM_LR, ADAM_B1, ADAM_B2, ADAM_EPS, ADAM_WD, ADAM_STEP = 0.001, 0.9, 0.999, 1e-08, 0.01, 10

LANES = 128
SUBLANES = 8
VMEM_LIMIT = 52 * 1024 * 1024
CONV_ROWS = 128
N_CHIPS = 4
N_DEV = 8
N_SEQUENCERS = 2
FETCH_IDS = (1, 2, 3, 4, 11, 12, 13, 14)
SEND_IDS = (5, 6, 7, 8, 9, 15)
SMALL_GRADS_ID = 10


def _cparams(*sem):
    return pltpu.CompilerParams(dimension_semantics=sem if sem else None, vmem_limit_bytes=VMEM_LIMIT)


def _tile(n, cap, mult=LANES):
    best = None
    for d in range(mult, min(n, cap) + 1, mult):
        if n % d == 0:
            best = d
    return best if best is not None else n


def _sum0(v):
    return jnp.sum(v, axis=0, keepdims=True)


def _rms(v):
    r = lax.rsqrt(jnp.mean(v * v, axis=-1, keepdims=True) + EPS)
    return v * r, r


def _sig(v):
    return jax.nn.sigmoid(v)


def _dot(a, b, ca, cb):
    return lax.dot_general(a.astype(MMT), b.astype(MMT), (((ca,), (cb,)), ((), ())), preferred_element_type=F32)


def _mm(a, b, mode, out_dtype, name):
    if mode == "nn":
        (M, K), N = a.shape, b.shape[1]
    elif mode == "nt":
        (M, K), N = a.shape, b.shape[0]
    else:
        (K, M), N = a.shape, b.shape[1]
    tm, tn, tk = _tile(M, 1408), _tile(N, 1408), _tile(K, 1536)
    nk = K // tk
    ca, cb = {"nn": (1, 0), "nt": (1, 1), "tn": (0, 0)}[mode]

    def body(a_ref, b_ref, o_ref, acc):
        k = pl.program_id(2)

        @pl.when(k == 0)
        def _():
            acc[...] = jnp.zeros_like(acc)

        acc[...] += _dot(a_ref[...], b_ref[...], ca, cb)

        @pl.when(k == nk - 1)
        def _():
            o_ref[...] = acc[...].astype(o_ref.dtype)

    a_spec = pl.BlockSpec((tk, tm), lambda i, j, k: (k, i)) if mode == "tn" else pl.BlockSpec((tm, tk), lambda i, j, k: (i, k))
    b_spec = pl.BlockSpec((tn, tk), lambda i, j, k: (j, k)) if mode == "nt" else pl.BlockSpec((tk, tn), lambda i, j, k: (k, j))
    return pl.pallas_call(
        body, name=name, grid=(M // tm, N // tn, nk), in_specs=[a_spec, b_spec],
        out_specs=pl.BlockSpec((tm, tn), lambda i, j, k: (i, j)), out_shape=SDS((M, N), out_dtype),
        scratch_shapes=[pltpu.VMEM((tm, tn), F32)], compiler_params=_cparams("parallel", "parallel", "arbitrary"))(a, b)


def _seg_of(nl, nseg):
    return (lambda i: jnp.where(i >= nl, 1, 0)) if nseg == 2 else (lambda i: 0)


def _prenorm(h, mod, gn, which, rows, nl, tm, name):
    D = h.shape[1]
    nseg = mod.shape[0]
    seg = _seg_of(nl, nseg)
    sh_i, sc_i = (0, 1) if which == 0 else (3, 4)

    def body(h_ref, mod_ref, gn_ref, a_ref):
        n, _ = _rms(h_ref[...])
        a_ref[...] = (n * gn_ref[...] * (1.0 + mod_ref[pl.ds(sc_i, 1), :]) + mod_ref[pl.ds(sh_i, 1), :]).astype(a_ref.dtype)

    return pl.pallas_call(
        body, name=name, grid=(rows // tm,),
        in_specs=[pl.BlockSpec((tm, D), lambda i: (i, 0)), pl.BlockSpec((None, 6, D), lambda i: (seg(i), 0, 0)),
                  pl.BlockSpec((1, D), lambda i: (0, 0))],
        out_specs=pl.BlockSpec((tm, D), lambda i: (i, 0)), out_shape=SDS((rows, D), MMT),
        compiler_params=_cparams("parallel"))(h, mod, gn)


def _acc_spec(D, seg):
    return pl.BlockSpec((None, 1, D), lambda i: (seg(i), 0, 0))


def _prenorm_bwd(h, da, dh_in, mod, gn, which, rows, nl, tm, name):
    D = h.shape[1]
    nseg = mod.shape[0]
    seg = _seg_of(nl, nseg)
    sc_i = 1 if which == 0 else 4

    def body(h_ref, da_ref, dhin_ref, mod_ref, gn_ref, dh_ref, dsh_ref, dsc_ref, dgn_ref):
        i = pl.program_id(0)
        first = (i == 0) | (i == nl) if nseg == 2 else (i == 0)

        @pl.when(first)
        def _():
            dsh_ref[...] = jnp.zeros_like(dsh_ref)
            dsc_ref[...] = jnp.zeros_like(dsc_ref)
            dgn_ref[...] = jnp.zeros_like(dgn_ref)

        n, r = _rms(h_ref[...])
        da_v = da_ref[...].astype(F32)
        gn_v = gn_ref[...]
        sc1 = 1.0 + mod_ref[pl.ds(sc_i, 1), :]
        dsh_ref[...] += _sum0(da_v)
        dsc_ref[...] += _sum0(da_v * (n * gn_v))
        dgn_ref[...] += _sum0(da_v * n * sc1)
        dn = da_v * (gn_v * sc1)
        dh_ref[...] = dhin_ref[...] + r * (dn - n * jnp.mean(dn * n, axis=-1, keepdims=True))

    row = pl.BlockSpec((tm, D), lambda i: (i, 0))
    acc = SDS((nseg, 1, D), F32)
    return pl.pallas_call(
        body, name=name, grid=(rows // tm,),
        in_specs=[row, row, row, pl.BlockSpec((None, 6, D), lambda i: (seg(i), 0, 0)), pl.BlockSpec((1, D), lambda i: (0, 0))],
        out_specs=[row, _acc_spec(D, seg), _acc_spec(D, seg), _acc_spec(D, seg)],
        out_shape=[SDS((rows, D), F32), acc, acc, acc], compiler_params=_cparams("arbitrary"))(h, da, dh_in, mod, gn)


def _postnorm(h, y, bias, mod, gn, gate_i, rows, nl, tm, name):
    D = h.shape[1]
    nseg = mod.shape[0]
    seg = _seg_of(nl, nseg)

    def body(h_ref, y_ref, b_ref, mod_ref, gn_ref, o_ref):
        ny, _ = _rms(y_ref[...] + b_ref[...])
        o_ref[...] = h_ref[...] + mod_ref[pl.ds(gate_i, 1), :] * (ny * gn_ref[...])

    row = pl.BlockSpec((tm, D), lambda i: (i, 0))
    vec = pl.BlockSpec((1, D), lambda i: (0, 0))
    return pl.pallas_call(
        body, name=name, grid=(rows // tm,),
        in_specs=[row, row, vec, pl.BlockSpec((None, 6, D), lambda i: (seg(i), 0, 0)), vec],
        out_specs=row, out_shape=SDS((rows, D), F32), compiler_params=_cparams("parallel"))(h, y, bias, mod, gn)


def _postnorm_bwd(dh, y, bias, mod, gn, gate_i, rows, nl, tm, name):
    D = y.shape[1]
    nseg = mod.shape[0]
    seg = _seg_of(nl, nseg)

    def body(dh_ref, y_ref, b_ref, mod_ref, gn_ref, dy_ref, dg_ref, dgn_ref, db_ref):
        i = pl.program_id(0)
        first = (i == 0) | (i == nl) if nseg == 2 else (i == 0)

        @pl.when(first)
        def _():
            dg_ref[...] = jnp.zeros_like(dg_ref)
            dgn_ref[...] = jnp.zeros_like(dgn_ref)
            db_ref[...] = jnp.zeros_like(db_ref)

        ny, ry = _rms(y_ref[...] + b_ref[...])
        g = mod_ref[pl.ds(gate_i, 1), :]
        gn_v = gn_ref[...]
        dh_v = dh_ref[...]
        dg_ref[...] += _sum0(dh_v * (ny * gn_v))
        dgn_ref[...] += _sum0(dh_v * ny * g)
        dny = dh_v * (g * gn_v)
        dy = ry * (dny - ny * jnp.mean(dny * ny, axis=-1, keepdims=True))
        db_ref[...] += _sum0(dy)
        dy_ref[...] = dy.astype(dy_ref.dtype)

    row = pl.BlockSpec((tm, D), lambda i: (i, 0))
    vec = pl.BlockSpec((1, D), lambda i: (0, 0))
    acc = SDS((nseg, 1, D), F32)
    return pl.pallas_call(
        body, name=name, grid=(rows // tm,),
        in_specs=[row, row, vec, pl.BlockSpec((None, 6, D), lambda i: (seg(i), 0, 0)), vec],
        out_specs=[row, _acc_spec(D, seg), _acc_spec(D, seg), _acc_spec(D, seg)],
        out_shape=[SDS((rows, D), MMT), acc, acc, acc], compiler_params=_cparams("arbitrary"))(dh, y, bias, mod, gn)


def _seg_layout(segs, H):
    out, base = [], H
    for s0, n in segs:
        out.append((s0, n, base))
        base += n + H
    return out, base


def _zero_pads(ref, lay, H):
    width = ref.shape[1]
    ref[pl.ds(0, H), :] = jnp.zeros((H, width), ref.dtype)
    for _, n, base in lay:
        ref[pl.ds(base + n, H), :] = jnp.zeros((H, width), ref.dtype)


def _window(ref, base, off, H):
    return ref[pl.ds(base - H + off, CONV_ROWS + 2 * H), :]


def _taps(win, H, offs):
    W = CONV_ROWS + 2 * H
    rolled, out = {}, {}
    for o in offs:
        s = H + o
        b = s % SUBLANES
        if b not in rolled:
            rolled[b] = win if b == 0 else pltpu.roll(win, shift=W - b, axis=0)
        out[o] = rolled[b][s - b:s - b + CONV_ROWS, :]
    return out


def _chunks(lay, fn):
    for s0, n, base in lay:
        def step(r, carry, s0=s0, base=base):
            fn(s0, base, pl.multiple_of(r * CONV_ROWS, CONV_ROWS))
            return carry
        lax.fori_loop(0, n // CONV_ROWS, step, 0)


def _ffn_gate(z0, conv_w, conv_b, segs, name):
    T, F2 = z0.shape
    F = F2 // 2
    tc = _tile(F, 256)
    nF = F // tc
    H = SUBLANES
    lay, srows = _seg_layout(segs, H)
    offs = [-1, 0, 1]

    def body(zg_ref, zv_ref, wg_ref, wv_ref, bg_ref, bv_ref, u_ref, xg, xv):
        _zero_pads(xg, lay, H)
        _zero_pads(xv, lay, H)
        for s0, n, base in lay:
            xg[pl.ds(base, n), :] = zg_ref[pl.ds(s0, n), :]
            xv[pl.ds(base, n), :] = zv_ref[pl.ds(s0, n), :]

        def chunk(s0, base, off):
            tg = _taps(_window(xg, base, off, H), H, offs)
            tv = _taps(_window(xv, base, off, H), H, offs)
            zg = bg_ref[...] + sum(tg[k - 1] * wg_ref[pl.ds(k, 1), :] for k in range(3))
            zv = bv_ref[...] + sum(tv[k - 1] * wv_ref[pl.ds(k, 1), :] for k in range(3))
            u_ref[pl.ds(s0 + off, CONV_ROWS), :] = (zg * _sig(zg) * zv).astype(u_ref.dtype)

        _chunks(lay, chunk)

    colg = lambda r: pl.BlockSpec((r, tc), lambda j: (0, j))
    colv = lambda r: pl.BlockSpec((r, tc), lambda j: (0, j + nF))
    return pl.pallas_call(
        body, name=name, grid=(nF,),
        in_specs=[colg(T), colv(T), colg(3), colv(3), colg(1), colv(1)],
        out_specs=colg(T), out_shape=SDS((T, F), MMT),
        scratch_shapes=[pltpu.VMEM((srows, tc), F32), pltpu.VMEM((srows, tc), F32)],
        compiler_params=_cparams("parallel"))(z0, z0, conv_w, conv_w, conv_b, conv_b)


def _ffn_gate_bwd(z0, du, conv_w, conv_b, segs, name):
    T, F2 = z0.shape
    F = F2 // 2
    tc = _tile(F, 256)
    nF = F // tc
    H = SUBLANES
    lay, srows = _seg_layout(segs, H)
    offs = [-1, 0, 1]

    def body(zo_ref, zt_ref, du_ref, wo_ref, wt_ref, bo_ref, bt_ref, u_ref, dz0_ref, dw_ref, db_ref, xo, xt, dzp):
        own_is_gate = pl.program_id(1) == 0
        for ref in (xo, xt, dzp):
            _zero_pads(ref, lay, H)
        for s0, n, base in lay:
            xo[pl.ds(base, n), :] = zo_ref[pl.ds(s0, n), :]
            xt[pl.ds(base, n), :] = zt_ref[pl.ds(s0, n), :]

        def grads(s0, base, off):
            to = _taps(_window(xo, base, off, H), H, offs)
            tt = _taps(_window(xt, base, off, H), H, offs)
            zo = bo_ref[...] + sum(to[k - 1] * wo_ref[pl.ds(k, 1), :] for k in range(3))
            zt = bt_ref[...] + sum(tt[k - 1] * wt_ref[pl.ds(k, 1), :] for k in range(3))
            so, st = _sig(zo), _sig(zt)
            du_v = du_ref[pl.ds(s0 + off, CONV_ROWS), :]
            d_gate = du_v * zt * (so * (1.0 + zo * (1.0 - so)))
            d_val = du_v * (zt * st)
            dzp[pl.ds(base + off, CONV_ROWS), :] = jnp.where(own_is_gate, d_gate, d_val)

            @pl.when(own_is_gate)
            def _():
                u_ref[pl.ds(s0 + off, CONV_ROWS), :] = (zo * so * zt).astype(u_ref.dtype)

        _chunks(lay, grads)
        dw_ref[...] = jnp.zeros_like(dw_ref)
        db_ref[...] = jnp.zeros_like(db_ref)

        def back(s0, base, off):
            td = _taps(_window(dzp, base, off, H), H, offs)
            tx = _taps(_window(xo, base, off, H), H, offs)
            dz0 = sum(td[1 - k] * wo_ref[pl.ds(k, 1), :] for k in range(3))
            dz0_ref[pl.ds(s0 + off, CONV_ROWS), :] = dz0.astype(dz0_ref.dtype)
            db_ref[...] += _sum0(td[0])
            for k in range(3):
                dw_ref[pl.ds(k, 1), :] += _sum0(td[0] * tx[k - 1])

        _chunks(lay, back)

    own = lambda r: pl.BlockSpec((r, tc), lambda j, hf: (0, hf * nF + j))
    oth = lambda r: pl.BlockSpec((r, tc), lambda j, hf: (0, (1 - hf) * nF + j))
    ucol = pl.BlockSpec((T, tc), lambda j, hf: (0, j))
    return pl.pallas_call(
        body, name=name, grid=(nF, 2),
        in_specs=[own(T), oth(T), ucol, own(3), oth(3), own(1), oth(1)],
        out_specs=[ucol, own(T), own(3), own(1)],
        out_shape=[SDS((T, F), MMT), SDS((T, F2), MMT), SDS((3, F2), F32), SDS((1, F2), F32)],
        scratch_shapes=[pltpu.VMEM((srows, tc), F32)] * 3,
        compiler_params=_cparams("parallel", "arbitrary"))(z0, z0, du, conv_w, conv_w, conv_b, conv_b)


def _glu_conv(p0, b_in, dw_w, dw_b, segs, name):
    T, D2 = p0.shape
    D = D2 // 2
    tc = _tile(D, 256)
    nD = D // tc
    H = 2 * SUBLANES
    half = (CONV_WIDTH - 1) // 2
    lay, srows = _seg_layout(segs, H)
    offs = list(range(-half, half + 1))

    def body(pa_ref, pg_ref, ba_ref, bg_ref, w_ref, b_ref, z2_ref, z1p):
        _zero_pads(z1p, lay, H)

        def glu(s0, base, off):
            rows = pl.ds(s0 + off, CONV_ROWS)
            z1p[pl.ds(base + off, CONV_ROWS), :] = (pa_ref[rows, :] + ba_ref[...]) * _sig(pg_ref[rows, :] + bg_ref[...])

        _chunks(lay, glu)

        def conv(s0, base, off):
            t = _taps(_window(z1p, base, off, H), H, offs)
            acc = b_ref[...] + t[-half] * w_ref[pl.ds(0, 1), :]
            for k in range(1, CONV_WIDTH):
                acc = acc + t[k - half] * w_ref[pl.ds(k, 1), :]
            z2_ref[pl.ds(s0 + off, CONV_ROWS), :] = acc

        _chunks(lay, conv)

    cola = lambda r: pl.BlockSpec((r, tc), lambda j: (0, j))
    colg = lambda r: pl.BlockSpec((r, tc), lambda j: (0, j + nD))
    return pl.pallas_call(
        body, name=name, grid=(nD,),
        in_specs=[cola(T), colg(T), cola(1), colg(1), cola(CONV_WIDTH), cola(1)],
        out_specs=cola(T), out_shape=SDS((T, D), F32), scratch_shapes=[pltpu.VMEM((srows, tc), F32)],
        compiler_params=_cparams("parallel"))(p0, p0, b_in, b_in, dw_w, dw_b)


def _glu_conv_bwd(p0, b_in, dw_w, dz2, segs, name):
    T, D2 = p0.shape
    D = D2 // 2
    tc = _tile(D, 256)
    nD = D // tc
    H = 2 * SUBLANES
    half = (CONV_WIDTH - 1) // 2
    lay, srows = _seg_layout(segs, H)
    offs = list(range(-half, half + 1))

    def body(pa_ref, pg_ref, ba_ref, bg_ref, w_ref, dz2_ref, dpa_ref, dpg_ref, dw_ref, db_ref, dba_ref, dbg_ref, z1p, dzp):
        _zero_pads(z1p, lay, H)
        _zero_pads(dzp, lay, H)
        for s0, n, base in lay:
            dzp[pl.ds(base, n), :] = dz2_ref[pl.ds(s0, n), :]

        def glu(s0, base, off):
            rows = pl.ds(s0 + off, CONV_ROWS)
            z1p[pl.ds(base + off, CONV_ROWS), :] = (pa_ref[rows, :] + ba_ref[...]) * _sig(pg_ref[rows, :] + bg_ref[...])

        _chunks(lay, glu)
        for ref in (dw_ref, db_ref, dba_ref, dbg_ref):
            ref[...] = jnp.zeros_like(ref)

        def back(s0, base, off):
            td = _taps(_window(dzp, base, off, H), H, offs)
            tz = _taps(_window(z1p, base, off, H), H, offs)
            dz1 = td[half] * w_ref[pl.ds(0, 1), :]
            for k in range(1, CONV_WIDTH):
                dz1 = dz1 + td[half - k] * w_ref[pl.ds(k, 1), :]
            db_ref[...] += _sum0(td[0])
            for k in range(CONV_WIDTH):
                dw_ref[pl.ds(k, 1), :] += _sum0(td[0] * tz[k - half])
            rows = pl.ds(s0 + off, CONV_ROWS)
            pa = pa_ref[rows, :] + ba_ref[...]
            sg = _sig(pg_ref[rows, :] + bg_ref[...])
            dpa = dz1 * sg
            dpg = dz1 * pa * (sg * (1.0 - sg))
            dba_ref[...] += _sum0(dpa)
            dbg_ref[...] += _sum0(dpg)
            dpa_ref[rows, :] = dpa.astype(dpa_ref.dtype)
            dpg_ref[rows, :] = dpg.astype(dpg_ref.dtype)

        _chunks(lay, back)

    cola = lambda r: pl.BlockSpec((r, tc), lambda j: (0, j))
    colg = lambda r: pl.BlockSpec((r, tc), lambda j: (0, j + nD))
    return pl.pallas_call(
        body, name=name, grid=(nD,),
        in_specs=[cola(T), colg(T), cola(1), colg(1), cola(CONV_WIDTH), cola(T)],
        out_specs=[cola(T), cola(T), cola(CONV_WIDTH), cola(1), cola(1), cola(1)],
        out_shape=[SDS((T, D), MMT), SDS((T, D), MMT), SDS((CONV_WIDTH, D), F32), SDS((1, D), F32), SDS((1, D), F32),
                   SDS((1, D), F32)],
        scratch_shapes=[pltpu.VMEM((srows, tc), F32)] * 2, compiler_params=_cparams("parallel"))(p0, p0, b_in, b_in, dw_w, dz2)


def _layer_norm_stats(v):
    mu = jnp.mean(v, axis=-1, keepdims=True)
    var = jnp.mean(jnp.square(v - mu), axis=-1, keepdims=True)
    rstd = lax.rsqrt(var + EPS)
    return (v - mu) * rstd, rstd


def _ln_silu(z2, ln_g, ln_b, rows, tm, name):
    D = z2.shape[1]

    def body(z_ref, g_ref, b_ref, o_ref):
        xh, _ = _layer_norm_stats(z_ref[...])
        z3 = xh * g_ref[...] + b_ref[...]
        o_ref[...] = (z3 * _sig(z3)).astype(o_ref.dtype)

    row = pl.BlockSpec((tm, D), lambda i: (i, 0))
    vec = pl.BlockSpec((1, D), lambda i: (0, 0))
    return pl.pallas_call(body, name=name, grid=(rows // tm,), in_specs=[row, vec, vec], out_specs=row,
                          out_shape=SDS((rows, D), MMT), compiler_params=_cparams("parallel"))(z2, ln_g, ln_b)


def _ln_silu_bwd(z2, dz4, ln_g, ln_b, rows, tm, name):
    D = z2.shape[1]

    def body(z_ref, d_ref, g_ref, b_ref, dz_ref, dg_ref, db_ref):
        @pl.when(pl.program_id(0) == 0)
        def _():
            dg_ref[...] = jnp.zeros_like(dg_ref)
            db_ref[...] = jnp.zeros_like(db_ref)

        xh, rstd = _layer_norm_stats(z_ref[...])
        z3 = xh * g_ref[...] + b_ref[...]
        s = _sig(z3)
        dz3 = d_ref[...] * (s * (1.0 + z3 * (1.0 - s)))
        dg_ref[...] += _sum0(dz3 * xh)
        db_ref[...] += _sum0(dz3)
        dxh = dz3 * g_ref[...]
        dz_ref[...] = rstd * (dxh - jnp.mean(dxh, axis=-1, keepdims=True) - xh * jnp.mean(dxh * xh, axis=-1, keepdims=True))

    row = pl.BlockSpec((tm, D), lambda i: (i, 0))
    vec = pl.BlockSpec((1, D), lambda i: (0, 0))
    return pl.pallas_call(body, name=name, grid=(rows // tm,), in_specs=[row, row, vec, vec], out_specs=[row, vec, vec],
                          out_shape=[SDS((rows, D), F32), SDS((1, D), F32), SDS((1, D), F32)],
                          compiler_params=_cparams("arbitrary"))(z2, dz4, ln_g, ln_b)


def _rot_half_pairs(v):
    width = v.shape[1]
    lane = lax.broadcasted_iota(jnp.int32, v.shape, 1)
    return jnp.where((lane % 32) < 16, -pltpu.roll(v, shift=width - 16, axis=1), pltpu.roll(v, shift=16, axis=1))


def _rope(qkv, cos, sin, L, qk, tm, name):
    width = qkv.shape[1]
    kv = width - qk

    def body(x_ref, c_ref, s_ref, qk_ref, v_ref):
        xv = x_ref[:, pl.ds(0, qk)]
        c = jnp.tile(c_ref[...], (1, qk // LANES))
        s = jnp.tile(s_ref[...], (1, qk // LANES))
        qk_ref[...] = (xv * c + _rot_half_pairs(xv) * s).astype(qk_ref.dtype)
        v_ref[...] = x_ref[:, pl.ds(qk, kv)].astype(v_ref.dtype)

    tab = pl.BlockSpec((tm, LANES), lambda i: (i, 0))
    return pl.pallas_call(
        body, name=name, grid=(L // tm,), in_specs=[pl.BlockSpec((tm, width), lambda i: (i, 0)), tab, tab],
        out_specs=[pl.BlockSpec((tm, qk), lambda i: (i, 0)), pl.BlockSpec((tm, kv), lambda i: (i, 0))],
        out_shape=[SDS((L, qk), MMT), SDS((L, kv), MMT)], compiler_params=_cparams("parallel"))(qkv, cos, sin)


def _rope_bwd(dqk, dv, cos, sin, tm, name):
    L, qk = dqk.shape
    kv = dv.shape[1]

    def body(d_ref, dv_ref, c_ref, s_ref, o_ref):
        dv_ = d_ref[...]
        c = jnp.tile(c_ref[...], (1, qk // LANES))
        s = jnp.tile(s_ref[...], (1, qk // LANES))
        o_ref[:, pl.ds(0, qk)] = (dv_ * c - _rot_half_pairs(dv_ * s)).astype(o_ref.dtype)
        o_ref[:, pl.ds(qk, kv)] = dv_ref[...].astype(o_ref.dtype)

    tab = pl.BlockSpec((tm, LANES), lambda i: (i, 0))
    return pl.pallas_call(
        body, name=name, grid=(L // tm,),
        in_specs=[pl.BlockSpec((tm, qk), lambda i: (i, 0)), pl.BlockSpec((tm, kv), lambda i: (i, 0)), tab, tab],
        out_specs=pl.BlockSpec((tm, qk + kv), lambda i: (i, 0)), out_shape=SDS((L, qk + kv), MMT),
        compiler_params=_cparams("parallel"))(dqk, dv, cos, sin)


def _band_specs(nb, width):
    blk = lambda f: pl.BlockSpec((None, ATTN_BLOCK, width), f)
    return [blk(lambda h, n: (h, jnp.maximum(n - 1, 0), 0)), blk(lambda h, n: (h, n, 0)),
            blk(lambda h, n: (h, jnp.minimum(n + 1, nb - 1), 0))]


def _window_mask(n, L):
    qi = lax.broadcasted_iota(jnp.int32, (ATTN_BLOCK, 3 * ATTN_BLOCK), 0)
    kk = lax.broadcasted_iota(jnp.int32, (ATTN_BLOCK, 3 * ATTN_BLOCK), 1)
    key_abs = (n - 1) * ATTN_BLOCK + kk
    return (jnp.abs(qi + ATTN_BLOCK - kk) <= ATTN_BLOCK) & (key_abs >= 0) & (key_abs < L)


def _attn_fwd(q, k, v, kc, vc, sink, name):
    nkv, _, L, hd = q.shape
    C = kc.shape[1]
    nb = L // ATTN_BLOCK
    scale = HEAD_DIM ** -0.5

    def body(sink_ref, q_ref, k0, k1, k2, v0, v1, v2, kc_ref, vc_ref, o_ref, lse_ref):
        hh, n = pl.program_id(0), pl.program_id(1)
        kw = jnp.concatenate([k0[...], k1[...], k2[...]], axis=0)
        vw = jnp.concatenate([v0[...], v1[...], v2[...]], axis=0)
        mask = _window_mask(n, L)
        for g in range(Q_PER_KV):
            qg = q_ref[g]
            sw = jnp.where(mask, _dot(qg, kw, 1, 1) * scale, NEG)
            sc = _dot(qg, kc_ref[...], 1, 1) * scale
            sk = sink_ref[hh * Q_PER_KV + g]
            m = jnp.maximum(jnp.maximum(jnp.max(sw, axis=-1, keepdims=True), jnp.max(sc, axis=-1, keepdims=True)), sk)
            pw, pc = jnp.exp(sw - m), jnp.exp(sc - m)
            den = jnp.sum(pw, axis=-1, keepdims=True) + jnp.sum(pc, axis=-1, keepdims=True) + jnp.exp(sk - m)
            inv = 1.0 / den
            o_ref[g] = _dot(pw * inv, vw, 1, 0) + _dot(pc * inv, vc_ref[...], 1, 0)
            lse_ref[g] = m + jnp.log(den)

    qspec = pl.BlockSpec((None, Q_PER_KV, ATTN_BLOCK, hd), lambda h, n: (h, 0, n, 0))
    cspec = pl.BlockSpec((None, C, hd), lambda h, n: (h, 0, 0))
    return pl.pallas_call(
        body, name=name, grid=(nkv, nb),
        in_specs=[pl.BlockSpec(memory_space=pltpu.SMEM), qspec] + _band_specs(nb, hd) + _band_specs(nb, hd) + [cspec, cspec],
        out_specs=[qspec, pl.BlockSpec((None, Q_PER_KV, ATTN_BLOCK, 1), lambda h, n: (h, 0, n, 0))],
        out_shape=[SDS((nkv, Q_PER_KV, L, hd), F32), SDS((nkv, Q_PER_KV, L, 1), F32)],
        compiler_params=_cparams("parallel", "parallel"))(sink, q, k, k, k, v, v, v, kc, vc)


def _attn_bwd_q(q, k, v, kc, vc, sink, o, do, lse, name):
    nkv, _, L, hd = q.shape
    C = kc.shape[1]
    nb = L // ATTN_BLOCK
    scale = HEAD_DIM ** -0.5

    def body(sink_ref, q_ref, k0, k1, k2, v0, v1, v2, kc_ref, vc_ref, o_ref, do_ref, lse_ref, dq_ref, dkc_ref, dvc_ref, dsk_ref):
        hh, n = pl.program_id(0), pl.program_id(1)

        @pl.when(n == 0)
        def _():
            dkc_ref[...] = jnp.zeros_like(dkc_ref)
            dvc_ref[...] = jnp.zeros_like(dvc_ref)
            dsk_ref[...] = jnp.zeros_like(dsk_ref)

        kw = jnp.concatenate([k0[...], k1[...], k2[...]], axis=0)
        vw = jnp.concatenate([v0[...], v1[...], v2[...]], axis=0)
        mask = _window_mask(n, L)
        for g in range(Q_PER_KV):
            qg, dog, lse_g = q_ref[g], do_ref[g], lse_ref[g]
            delta = jnp.sum(dog.astype(F32) * o_ref[g], axis=-1, keepdims=True)
            pw = jnp.exp(jnp.where(mask, _dot(qg, kw, 1, 1) * scale, NEG) - lse_g)
            pc = jnp.exp(_dot(qg, kc_ref[...], 1, 1) * scale - lse_g)
            dsw = pw * (_dot(dog, vw, 1, 1) - delta)
            dsc = pc * (_dot(dog, vc_ref[...], 1, 1) - delta)
            dq_ref[g] = (_dot(dsw, kw, 1, 0) + _dot(dsc, kc_ref[...], 1, 0)) * scale
            dkc_ref[...] += _dot(dsc, qg, 0, 0) * scale
            dvc_ref[...] += _dot(pc, dog, 0, 0)
            psk = jnp.exp(sink_ref[hh * Q_PER_KV + g] - lse_g)
            dsk_ref[pl.ds(g, 1), :] += jnp.broadcast_to(jnp.sum(-psk * delta, axis=0, keepdims=True), (1, LANES))

    qspec = pl.BlockSpec((None, Q_PER_KV, ATTN_BLOCK, hd), lambda h, n: (h, 0, n, 0))
    lspec = pl.BlockSpec((None, Q_PER_KV, ATTN_BLOCK, 1), lambda h, n: (h, 0, n, 0))
    cspec = pl.BlockSpec((None, C, hd), lambda h, n: (h, 0, 0))
    return pl.pallas_call(
        body, name=name, grid=(nkv, nb),
        in_specs=[pl.BlockSpec(memory_space=pltpu.SMEM), qspec] + _band_specs(nb, hd) + _band_specs(nb, hd)
        + [cspec, cspec, qspec, qspec, lspec],
        out_specs=[qspec, cspec, cspec, pl.BlockSpec((None, SUBLANES, LANES), lambda h, n: (h, 0, 0))],
        out_shape=[SDS((nkv, Q_PER_KV, L, hd), F32), SDS((nkv, C, hd), F32), SDS((nkv, C, hd), F32),
                   SDS((nkv, SUBLANES, LANES), F32)],
        compiler_params=_cparams("parallel", "arbitrary"))(sink, q, k, k, k, v, v, v, kc, vc, o, do, lse)


def _attn_bwd_kv(q, k, v, o, do, lse, name):
    nkv, _, L, hd = q.shape
    nb = L // ATTN_BLOCK
    scale = HEAD_DIM ** -0.5

    def body(q0, q1, q2, do0, do1, do2, o0, o1, o2, l0, l1, l2, k_ref, v_ref, dk_ref, dv_ref):
        j = pl.program_id(1)
        qi = lax.broadcasted_iota(jnp.int32, (ATTN_BLOCK, ATTN_BLOCK), 0)
        kk = lax.broadcasted_iota(jnp.int32, (ATTN_BLOCK, ATTN_BLOCK), 1)
        kj, vj = k_ref[...], v_ref[...]
        dk = jnp.zeros((ATTN_BLOCK, hd), F32)
        dv = jnp.zeros((ATTN_BLOCK, hd), F32)
        for slot, (q_r, do_r, o_r, l_r) in enumerate(((q0, do0, o0, l0), (q1, do1, o1, l1), (q2, do2, o2, l2))):
            n = j - 1 + slot
            ok = (n >= 0) & (n < nb) & (jnp.abs(qi + ATTN_BLOCK - ((2 - slot) * ATTN_BLOCK + kk)) <= ATTN_BLOCK)
            for g in range(Q_PER_KV):
                qg, dog = q_r[g], do_r[g]
                delta = jnp.sum(dog.astype(F32) * o_r[g], axis=-1, keepdims=True)
                p = jnp.exp(jnp.where(ok, _dot(qg, kj, 1, 1) * scale - l_r[g], NEG))
                ds = p * (_dot(dog, vj, 1, 1) - delta)
                dk = dk + _dot(ds, qg, 0, 0) * scale
                dv = dv + _dot(p, dog, 0, 0)
        dk_ref[...] = dk
        dv_ref[...] = dv

    def band(width):
        blk = lambda f: pl.BlockSpec((None, Q_PER_KV, ATTN_BLOCK, width), f)
        return [blk(lambda h, j: (h, 0, jnp.maximum(j - 1, 0), 0)), blk(lambda h, j: (h, 0, j, 0)),
                blk(lambda h, j: (h, 0, jnp.minimum(j + 1, nb - 1), 0))]

    kspec = pl.BlockSpec((None, ATTN_BLOCK, hd), lambda h, j: (h, j, 0))
    return pl.pallas_call(
        body, name=name, grid=(nkv, nb), in_specs=band(hd) + band(hd) + band(hd) + band(1) + [kspec, kspec],
        out_specs=[kspec, kspec], out_shape=[SDS((nkv, L, hd), F32), SDS((nkv, L, hd), F32)],
        compiler_params=_cparams("parallel", "parallel"))(q, q, q, do, do, do, o, o, o, lse, lse, lse, k, v)


_GELU_K = math.sqrt(2.0 / math.pi)


def _gelu(v):
    return 0.5 * v * (1.0 + jnp.tanh(_GELU_K * (v + 0.044715 * (v * v * v))))


def _gelu_grad(v):
    t = jnp.tanh(_GELU_K * (v + 0.044715 * (v * v * v)))
    return 0.5 * (1.0 + t) + 0.5 * v * (1.0 - t * t) * (_GELU_K * (1.0 + 3.0 * 0.044715 * (v * v)))


def _gmlp_fwd(p0, b_in, ln_g, ln_b, w_s, b_s, name):
    L, W2 = p0.shape
    W = W2 // 2
    G = W // GMLP_GROUP_DIM

    def body(p_ref, bi_ref, g_ref, b_ref, ws_ref, bs_ref, o_ref):
        ge = _gelu(p_ref[...] + bi_ref[...])
        xh, _ = _layer_norm_stats(ge[:, W:])
        vln = xh * g_ref[...] + b_ref[...]
        for gi in range(G):
            cols = slice(gi * GMLP_GROUP_DIM, (gi + 1) * GMLP_GROUP_DIM)
            s = _dot(ws_ref[gi], vln[:, cols], 1, 0) + bs_ref[gi]
            o_ref[:, cols] = (ge[:, cols] * s).astype(o_ref.dtype)

    full = lambda shape: pl.BlockSpec(shape, lambda i: (0,) * len(shape))
    return pl.pallas_call(
        body, name=name, grid=(L // GMLP_CHUNK,),
        in_specs=[pl.BlockSpec((GMLP_CHUNK, W2), lambda i: (i, 0)), full((1, W2)), full((1, W)), full((1, W)),
                  full((G, GMLP_CHUNK, GMLP_CHUNK)), full((G, GMLP_CHUNK, 1))],
        out_specs=pl.BlockSpec((GMLP_CHUNK, W), lambda i: (i, 0)), out_shape=SDS((L, W), MMT),
        compiler_params=_cparams("parallel"))(p0, b_in, ln_g, ln_b, w_s, b_s)


def _gmlp_bwd(p0, dus, b_in, ln_g, ln_b, w_s, w_st, b_s, name):
    L, W2 = p0.shape
    W = W2 // 2
    G = W // GMLP_GROUP_DIM

    def body(p_ref, d_ref, bi_ref, g_ref, b_ref, ws_ref, wst_ref, bs_ref, dpre_ref, dbi_ref, dg_ref, db_ref, dws_ref, dbs_ref, dvln):
        @pl.when(pl.program_id(0) == 0)
        def _():
            for ref in (dbi_ref, dg_ref, db_ref, dws_ref, dbs_ref):
                ref[...] = jnp.zeros_like(ref)

        pre = p_ref[...] + bi_ref[...]
        ge = _gelu(pre)
        xh, rstd = _layer_norm_stats(ge[:, W:])
        vln = xh * g_ref[...] + b_ref[...]
        dge_u = []
        for gi in range(G):
            cols = slice(gi * GMLP_GROUP_DIM, (gi + 1) * GMLP_GROUP_DIM)
            vg = vln[:, cols]
            s = _dot(ws_ref[gi], vg, 1, 0) + bs_ref[gi]
            dus_g = d_ref[:, cols]
            dge_u.append(dus_g * s)
            ds = dus_g * ge[:, cols]
            dbs_ref[gi] += jnp.sum(ds, axis=1, keepdims=True)
            dws_ref[gi] += _dot(ds, vg, 1, 1)
            dvln[:, cols] = _dot(wst_ref[gi], ds, 1, 0)
        dv = dvln[...]
        dg_ref[...] += _sum0(dv * xh)
        db_ref[...] += _sum0(dv)
        dxh = dv * g_ref[...]
        dv0 = rstd * (dxh - jnp.mean(dxh, axis=-1, keepdims=True) - xh * jnp.mean(dxh * xh, axis=-1, keepdims=True))
        dpre = jnp.concatenate(dge_u + [dv0], axis=1) * _gelu_grad(pre)
        dbi_ref[...] += _sum0(dpre)
        dpre_ref[...] = dpre.astype(dpre_ref.dtype)

    full = lambda shape: pl.BlockSpec(shape, lambda i: (0,) * len(shape))
    mats = (G, GMLP_CHUNK, GMLP_CHUNK)
    return pl.pallas_call(
        body, name=name, grid=(L // GMLP_CHUNK,),
        in_specs=[pl.BlockSpec((GMLP_CHUNK, W2), lambda i: (i, 0)), pl.BlockSpec((GMLP_CHUNK, W), lambda i: (i, 0)),
                  full((1, W2)), full((1, W)), full((1, W)), full(mats), full(mats), full((G, GMLP_CHUNK, 1))],
        out_specs=[pl.BlockSpec((GMLP_CHUNK, W2), lambda i: (i, 0)), full((1, W2)), full((1, W)), full((1, W)), full(mats),
                   full((G, GMLP_CHUNK, 1))],
        out_shape=[SDS((L, W2), MMT), SDS((1, W2), F32), SDS((1, W), F32), SDS((1, W), F32), SDS(mats, F32),
                   SDS((G, GMLP_CHUNK, 1), F32)],
        scratch_shapes=[pltpu.VMEM((GMLP_CHUNK, W), F32)], compiler_params=_cparams("arbitrary"))(
            p0, dus, b_in, ln_g, ln_b, w_s, w_st, b_s)


def _loss_head(h, target, tm, name):
    L, D = h.shape

    def body(h_ref, t_ref, l_ref, d_ref):
        @pl.when(pl.program_id(0) == 0)
        def _():
            l_ref[...] = jnp.zeros_like(l_ref)

        e = h_ref[...] - t_ref[...]
        l_ref[...] += 0.5 * jnp.sum(jnp.mean(e * e, axis=-1, keepdims=True), axis=0, keepdims=True)
        d_ref[...] = e * (1.0 / D)

    row = pl.BlockSpec((tm, D), lambda i: (i, 0))
    return pl.pallas_call(body, name=name, grid=(L // tm,), in_specs=[row, row],
                          out_specs=[pl.BlockSpec((1, 1), lambda i: (0, 0)), row],
                          out_shape=[SDS((1, 1), F32), SDS((L, D), F32)], compiler_params=_cparams("arbitrary"))(h, target)


def _ada_fwd(cond, ada_w, ada_b, name):
    NL, D, n = ada_w.shape
    tn = _tile(n, 768)

    def body(c_ref, w_ref, b_ref, o_ref):
        cv = c_ref[...]
        o_ref[...] = _dot(cv * _sig(cv), w_ref[...], 1, 0) + b_ref[...]

    return pl.pallas_call(
        body, name=name, grid=(NL, n // tn),
        in_specs=[pl.BlockSpec((2 * SUBLANES, D), lambda i, j: (0, 0)), pl.BlockSpec((None, D, tn), lambda i, j: (i, 0, j)),
                  pl.BlockSpec((None, 1, tn), lambda i, j: (i, 0, j))],
        out_specs=pl.BlockSpec((None, 2 * SUBLANES, tn), lambda i, j: (i, 0, j)), out_shape=SDS((NL, 2 * SUBLANES, n), F32),
        compiler_params=_cparams("parallel", "parallel"))(cond, ada_w, ada_b)


def _ada_bwd(cond, ada_w, dm_lat, dm_ctx, name):
    NL, D, n = ada_w.shape
    tn = _tile(n, 768)

    def body(c_ref, w_ref, dl_ref, dc_ref, dw_ref, ds_ref):
        @pl.when((pl.program_id(0) == 0) & (pl.program_id(1) == 0))
        def _():
            ds_ref[...] = jnp.zeros_like(ds_ref)

        cv = c_ref[...]
        row = lax.broadcasted_iota(jnp.int32, (SUBLANES, tn), 0)
        ctx_rows = jnp.where(row == 0, _sum0(dc_ref[...]), 0.0)
        dm = jnp.concatenate([dl_ref[...], ctx_rows], axis=0)
        dw_ref[...] = _dot(cv * _sig(cv), dm, 0, 0)
        ds_ref[...] += _dot(dm, w_ref[...], 1, 1)

    dspec = pl.BlockSpec((None, SUBLANES, tn), lambda i, j: (i, 0, j))
    return pl.pallas_call(
        body, name=name, grid=(NL, n // tn),
        in_specs=[pl.BlockSpec((2 * SUBLANES, D), lambda i, j: (0, 0)), pl.BlockSpec((None, D, tn), lambda i, j: (i, 0, j)),
                  dspec, dspec],
        out_specs=[pl.BlockSpec((None, D, tn), lambda i, j: (i, 0, j)), pl.BlockSpec((2 * SUBLANES, D), lambda i, j: (0, 0))],
        out_shape=[SDS((NL, D, n), F32), SDS((2 * SUBLANES, D), F32)],
        compiler_params=_cparams("arbitrary", "arbitrary"))(cond, ada_w, dm_lat, dm_ctx)


def _adam_math(w, g, m, v):
    m = ADAM_B1 * m + (1.0 - ADAM_B1) * g
    v = ADAM_B2 * v + (1.0 - ADAM_B2) * jnp.square(g)
    m_hat = m / (1.0 - ADAM_B1 ** ADAM_STEP)
    v_hat = v / (1.0 - ADAM_B2 ** ADAM_STEP)
    return -ADAM_LR * (m_hat / (jnp.sqrt(v_hat) + ADAM_EPS) + ADAM_WD * w), m, v


def _row_tile(rows, cols, elems, mult=SUBLANES):
    want = max(mult, elems // cols)
    best = mult if rows % mult == 0 else rows
    for d in range(mult, min(rows, want) + 1, mult):
        if rows % d == 0:
            best = d
    return best


def _adamw(w, m, v, parts, name):
    R, C = w.shape
    tr = _row_tile(R, C, 128 * 1024)
    npart = len(parts)

    def body(*refs):
        w_ref, m_ref, v_ref = refs[:3]
        g_ref, d_ref, nm_ref, nv_ref = refs[3 + npart:]
        g = refs[3][...]
        for p_ref in refs[4:3 + npart]:
            g = g + p_ref[...]
        d, nm, nv = _adam_math(w_ref[...], g, m_ref[...], v_ref[...])
        g_ref[...], d_ref[...], nm_ref[...], nv_ref[...] = g, d, nm, nv

    blk = pl.BlockSpec((tr, C), lambda i: (i, 0))
    return pl.pallas_call(body, name=name, grid=(R // tr,), in_specs=[blk] * (3 + npart), out_specs=[blk] * 4,
                          out_shape=[SDS((R, C), F32)] * 4, compiler_params=_cparams("parallel"))(w, m, v, *parts)


def _adamw_layer(w, m, v, layer, parts, prev, name):
    _, R, C = w.shape
    tr = _row_tile(R, C, 128 * 1024, 2 * SUBLANES)
    npart = len(parts)
    nprev = 0 if prev is None else 4

    def body(*refs):
        w_ref, m_ref, v_ref = refs[:3]
        g_ref, d_ref, nm_ref, nv_ref = refs[3 + npart + nprev:]
        g = refs[3][...].astype(F32)
        for p_ref in refs[4:3 + npart]:
            g = g + p_ref[...].astype(F32)
        d, nm, nv = _adam_math(w_ref[...], g, m_ref[...], v_ref[...])
        g_ref[...], d_ref[...], nm_ref[...], nv_ref[...] = g, d, nm, nv

    stacked = pl.BlockSpec((None, tr, C), lambda i: (layer, i, 0))
    flat = pl.BlockSpec((tr, C), lambda i: (i, 0))
    return pl.pallas_call(
        body, name=name, grid=(R // tr,),
        in_specs=[stacked] * 3 + [flat] * npart + [pl.BlockSpec(memory_space=pl.ANY)] * nprev, out_specs=[stacked] * 4,
        out_shape=[SDS(w.shape, F32)] * 4, input_output_aliases={3 + npart + k: k for k in range(nprev)},
        compiler_params=_cparams("parallel"))(w, m, v, *parts, *(prev or ()))


def _sum_slots(x, name, after=None, out_dtype=F32):
    S, R, C = x.shape
    tr = _row_tile(R, C, 128 * 1024, SUBLANES * 4 // jnp.dtype(x.dtype).itemsize)
    extra = [] if after is None else [after]

    def body(x_ref, *rest):
        o_ref = rest[-1]
        acc = x_ref[0].astype(F32)
        for s in range(1, S):
            acc = acc + x_ref[s].astype(F32)
        o_ref[...] = acc.astype(o_ref.dtype)

    return pl.pallas_call(
        body, name=name, grid=(R // tr,),
        in_specs=[pl.BlockSpec((S, tr, C), lambda i: (0, i, 0))] + [pl.BlockSpec(memory_space=pl.ANY)] * len(extra),
        out_specs=pl.BlockSpec((tr, C), lambda i: (i, 0)), out_shape=SDS((R, C), out_dtype),
        compiler_params=_cparams("parallel"))(x, *extra)


def _my_place():
    return lax.axis_index("x"), lax.axis_index("y"), lax.axis_index("c")


def _other_chips(x, y):
    return [(1 - x, y), (x, 1 - y), (1 - x, 1 - y)]


def _all_gather(v, name):
    R, C = v.shape

    def body(v_ref, o_ref, send_sems, recv_sems, local_sem):
        x, y, c = _my_place()
        me = 4 * x + 2 * y + c
        mine = pltpu.make_async_copy(v_ref, o_ref.at[me], local_sem)
        mine.start()
        copies = []
        for flip in range(1, N_DEV):
            fx, fy, fc = (flip >> 2) & 1, (flip >> 1) & 1, flip & 1
            peer = ((x + fx) % 2, (y + fy) % 2, (c + fc) % 2)
            cp = pltpu.make_async_remote_copy(src_ref=v_ref, dst_ref=o_ref.at[me], send_sem=send_sems.at[flip - 1],
                                              recv_sem=recv_sems.at[flip - 1], device_id=peer, device_id_type=MESH)
            cp.start()
            copies.append(cp)
        for cp in copies:
            cp.wait()
        mine.wait()

    return pl.pallas_call(
        body, name=name, in_specs=[pl.BlockSpec(memory_space=pl.ANY)], out_specs=pl.BlockSpec(memory_space=pl.ANY),
        out_shape=SDS((N_DEV, R, C), v.dtype),
        scratch_shapes=[pltpu.SemaphoreType.DMA((N_DEV - 1,)), pltpu.SemaphoreType.DMA((N_DEV - 1,)), pltpu.SemaphoreType.DMA],
        )(v)


def _shard_window(ref, axis, j, size):
    idx = [slice(None)] * len(ref.shape)
    idx[axis] = pl.ds(pl.multiple_of(j * size, SUBLANES), size)
    return ref.at[tuple(idx)]


def _gather_plan(axis):
    def plan(srcs, lands):
        x, y, c = _my_place()
        shard, whole = srcs[0], lands[0]
        half = shard.shape[0] // 2
        size = shard.shape[axis]

        def window(chip, which):
            if axis == 1:
                return whole.at[pl.ds(pl.multiple_of(which * half, SUBLANES), half), pl.ds(pl.multiple_of(chip * size, LANES), size)]
            return whole.at[pl.ds(pl.multiple_of(chip * size + which * half, SUBLANES), half), :]

        j = 2 * x + y
        local = [(shard, _shard_window(whole, axis, j, size))]
        mine = shard.at[pl.ds(pl.multiple_of(c * half, SUBLANES), half), :]
        remote = [(mine, window(j, c), (px, py, c)) for px, py in _other_chips(x, y)]
        forward = [(window(2 * px + py, c), window(2 * px + py, c), (x, y, 1 - c)) for px, py in _other_chips(x, y)]
        return local, remote, forward
    return plan


def _scatter_plan(axis):
    def plan(srcs, lands):
        x, y, c = _my_place()
        j = 2 * x + y
        size = srcs[0].shape[axis] // N_CHIPS
        local = [(_shard_window(srcs[0], axis, j, size), lands[0].at[j])]
        remote = [(_shard_window(srcs[0], axis, 2 * px + py, size), lands[0].at[j], (px, py, c)) for px, py in _other_chips(x, y)]
        return local, remote, []
    return plan


def _all_gather_plan(srcs, lands):
    x, y, c = _my_place()
    dst = lands[0].at[4 * x + 2 * y + c]
    remote = []
    for flip in range(1, N_DEV):
        fx, fy, fc = (flip >> 2) & 1, (flip >> 1) & 1, flip & 1
        remote.append((srcs[0], dst, ((x + fx) % 2, (y + fy) % 2, (c + fc) % 2)))
    return [(srcs[0], dst)], remote, []


def _same_core_peers():
    x, y, c = _my_place()
    return [(px, py, c) for px, py in _other_chips(x, y)]


def _same_core_peers_and_sibling():
    x, y, c = _my_place()
    return _same_core_peers() + [(x, y, 1 - c)]


def _all_peers():
    x, y, c = _my_place()
    return [((x + (f >> 2 & 1)) % 2, (y + (f >> 1 & 1)) % 2, (c + (f & 1)) % 2) for f in range(1, N_DEV)]


def _sequencer_exchange(name, collective_id, exchanges, peers_fn):
    hbm = pltpu.MemorySpace.HBM
    src_refs = [[jax.new_ref(s, memory_space=hbm) for s in e[0]] for e in exchanges]
    land_refs = [[jax.empty_ref(s, memory_space=hbm) for s in e[1]] for e in exchanges]
    first = [sum(e[3] for e in exchanges[:i]) for i in range(len(exchanges))]
    ncopy = sum(e[3] for e in exchanges)
    dma = pltpu.SemaphoreType.DMA

    @pl.kernel(mesh=plsc.ScalarSubcoreMesh(axis_name="sequencer", num_cores=N_SEQUENCERS), name=name,
               scratch_types=(dma((ncopy,)), dma((ncopy,)), dma((ncopy,)), dma((ncopy,)), dma),
               compiler_params=pltpu.CompilerParams(collective_id=collective_id))
    def launch(send_sems, recv_sems, onward_send_sems, onward_recv_sems, local_sem):
        me = lax.axis_index("sequencer")
        peers = peers_fn()
        barrier = pltpu.get_barrier_semaphore()
        for peer in peers:
            pl.semaphore_signal(barrier, inc=1, device_id=peer, device_id_type=MESH)
        pl.semaphore_wait(barrier, len(peers))
        plans = [e[2](src_refs[i], land_refs[i]) for i, e in enumerate(exchanges)]
        nbytes = lambda ref: math.prod(ref.shape) * jnp.dtype(ref.dtype).itemsize
        jobs = [(nbytes(dst), "local", (i, k)) for i, p in enumerate(plans) for k, (_, dst) in enumerate(p[0])]
        jobs += [(nbytes(src) * (2 if p[2] else 1), "remote", (i, k)) for i, p in enumerate(plans) for k, (src, _, _) in enumerate(p[1])]
        load, owner = [0] * N_SEQUENCERS, {}
        for size, kind, key in sorted(jobs, key=lambda job: -job[0]):
            owner[kind, key] = load.index(min(load))
            load[owner[kind, key]] += size
        for s in range(N_SEQUENCERS):
            @pl.when(me == s)
            def _(s=s):
                for i, (local, _, _) in enumerate(plans):
                    for k, (src, dst) in enumerate(local):
                        if owner["local", (i, k)] == s:
                            cp = pltpu.make_async_copy(src, dst, local_sem)
                            cp.start()
                            cp.wait()
                copies, onward = [], []
                for i, (_, remote, forward) in enumerate(plans):
                    assert len(remote) == exchanges[i][3] and len(forward) in (0, len(remote))
                    for k, (src, dst, peer) in enumerate(remote):
                        if owner["remote", (i, k)] == s:
                            cp = pltpu.make_async_remote_copy(src_ref=src, dst_ref=dst, send_sem=send_sems.at[first[i] + k],
                                                              recv_sem=recv_sems.at[first[i] + k], device_id=peer,
                                                              device_id_type=MESH)
                            cp.start()
                            copies.append(cp)
                            if forward:
                                src, dst, peer = forward[k]
                                onward.append(pltpu.make_async_remote_copy(
                                    src_ref=src, dst_ref=dst, send_sem=onward_send_sems.at[first[i] + k],
                                    recv_sem=onward_recv_sems.at[first[i] + k], device_id=peer, device_id_type=MESH))
                            else:
                                onward.append(None)
                for cp, on in zip(copies, onward):
                    cp.wait_recv()
                    if on is not None:
                        on.start()
                for cp, on in zip(copies, onward):
                    cp.wait_send()
                    if on is not None:
                        on.wait()

    launch()
    return [[r[...] for r in refs] for refs in land_refs]


def _swap_with_sibling(parts, name):
    nt = len(parts)

    def body(*refs):
        ins, outs = refs[:nt], refs[nt:2 * nt]
        send_sems, recv_sems = refs[2 * nt:]
        x, y, c = _my_place()
        copies = []
        for t in range(nt):
            cp = pltpu.make_async_remote_copy(src_ref=ins[t], dst_ref=outs[t], send_sem=send_sems.at[t], recv_sem=recv_sems.at[t],
                                              device_id=(x, y, 1 - c), device_id_type=MESH)
            cp.start()
            copies.append(cp)
        for cp in copies:
            cp.wait()

    any_spec = pl.BlockSpec(memory_space=pl.ANY)
    return pl.pallas_call(
        body, name=name, in_specs=[any_spec] * nt, out_specs=[any_spec] * nt, out_shape=[SDS(p.shape, p.dtype) for p in parts],
        scratch_shapes=[pltpu.SemaphoreType.DMA((nt,)), pltpu.SemaphoreType.DMA((nt,))],
        )(*parts)


PACK_COLS = 1024


def _pack(arrays):
    flat = jnp.concatenate([a.reshape(-1) for a in arrays])
    pad = (-flat.shape[0]) % (2 * SUBLANES * PACK_COLS)
    return jnp.pad(flat, (0, pad)).reshape(-1, PACK_COLS)


def _unpack(packed, shapes):
    flat, out, pos = packed.reshape(-1), [], 0
    for shape in shapes:
        n = math.prod(shape)
        out.append(flat[pos:pos + n].reshape(shape))
        pos += n
    return out


def _unshard_last(stacked):
    moved = jnp.moveaxis(stacked, 0, -2)
    return moved.reshape(moved.shape[:-2] + (moved.shape[-2] * moved.shape[-1],))


def _my_block_last(full, j):
    s = full.shape[-1] // N_CHIPS
    return lax.dynamic_index_in_dim(full.reshape(full.shape[:-1] + (N_CHIPS, s)), j, axis=full.ndim - 1, keepdims=False)


def _rope_tables(L):
    rows = L // GRID_W
    row = jnp.repeat(jnp.arange(rows), GRID_W).astype(F32)
    col = jnp.tile(jnp.arange(GRID_W), rows).astype(F32)
    axis_dim = HEAD_DIM // 2
    inv_freq = ROPE_BASE ** (-jnp.arange(0, axis_dim, 2, dtype=F32) / axis_dim)
    ang_r, ang_c = row[:, None] * inv_freq[None, :], col[:, None] * inv_freq[None, :]
    ang = jnp.concatenate([ang_r, ang_r, ang_c, ang_c] * 2, axis=-1)
    return jnp.cos(ang), jnp.sin(ang)


SMALL_SHARDED = ("norm_g", "ffn_conv_w", "cm_b_in", "cm_dw_w", "cm_dw_b", "cm_ln_g", "cm_ln_b", "cm_b_out", "gm_b_in", "gm_ln_g",
                 "gm_ln_b")
SMALL_REPLICATED = ("c_ctx", "ada_b", "ffn_conv_b", "attn_sink", "gm_w_s", "gm_b_s")
BIG = ("ffn_w_up", "ffn_w_down", "cm_w_in", "cm_w_out", "attn_w_qkv", "attn_w_o", "gm_w_in", "gm_w_out")
BIG_AXIS = {"ffn_w_up": 2, "ffn_w_down": 1, "cm_w_in": 2, "cm_w_out": 1, "attn_w_qkv": 2, "attn_w_o": 1, "gm_w_in": 2, "gm_w_out": 1}
WEIGHTS = ("c_ctx", "ada_w", "ada_b", "norm_g", "ffn_w_up", "ffn_conv_w", "ffn_conv_b", "ffn_w_down", "cm_w_in", "cm_b_in",
           "cm_dw_w", "cm_dw_b", "cm_ln_g", "cm_ln_b", "cm_w_out", "cm_b_out", "attn_w_qkv", "attn_sink", "attn_w_o", "gm_w_in",
           "gm_b_in", "gm_ln_g", "gm_ln_b", "gm_w_s", "gm_b_s", "gm_w_out")


def _step(x, c, ctx, target, W, M, V):
    L, D = x.shape[1], x.shape[2]
    C = ctx.shape[1]
    T = L + C
    NL = W["ada_w"].shape[0]
    tm = 256 if C % 256 == 0 else 128
    nl = L // tm
    xi, yi, ci = _my_place()
    chip = 2 * xi + yi
    dev = 4 * xi + 2 * yi + ci
    segs2, segs1 = [(0, L), (L, C)], [(0, L)]
    vec = lambda a: a.reshape(1, -1)

    layer_sets = [[("cm_w_in", 0), ("cm_w_out", 0), ("ffn_w_up", 0), ("ffn_w_down", 0)],
                  [("attn_w_qkv", 0), ("attn_w_o", 0), ("ffn_w_up", 1), ("ffn_w_down", 1)],
                  [("gm_w_in", 0), ("gm_w_out", 0), ("ffn_w_up", 2), ("ffn_w_down", 2)],
                  [("cm_w_in", 1), ("cm_w_out", 1), ("ffn_w_up", 3), ("ffn_w_down", 3)]]
    arrived = {}

    def fetch(keys, zero, sequencer_id):
        exchanges = []
        for n, i in keys:
            shard = (W[n][i] + zero).astype(MMT)
            whole = list(shard.shape)
            whole[BIG_AXIS[n] - 1] *= N_CHIPS
            exchanges.append(([shard], [SDS(tuple(whole), MMT)], _gather_plan(BIG_AXIS[n] - 1), N_CHIPS - 1))
        lands = _sequencer_exchange(f"fetch_weights_{keys[0][0]}_{keys[0][1]}", sequencer_id, exchanges,
                                    _same_core_peers_and_sibling)
        for key, land in zip(keys, lands):
            arrived[key] = land[0]

    def big(n, i, after=None):
        return arrived[(n, i)]

    small_shapes = [W[n].shape for n in SMALL_SHARDED]
    ag1 = _all_gather(_pack([c.reshape(-1)] + [W[n] for n in SMALL_SHARDED]), "gather_small")
    parts = [_unpack(ag1[2 * s], [(D,)] + small_shapes) for s in range(N_CHIPS)]
    c_rows = jnp.stack([_unpack(ag1[d], [(D,)])[0] for d in range(N_DEV)])
    P = {n: _unshard_last(jnp.stack([parts[s][1 + i] for s in range(N_CHIPS)])) for i, n in enumerate(SMALL_SHARDED)}
    for n in SMALL_REPLICATED:
        P[n] = W[n]

    cond = jnp.concatenate([c_rows, W["c_ctx"][None, :], jnp.zeros((2 * SUBLANES - N_DEV - 1, D), F32)], axis=0)
    ncol = W["ada_w"].shape[2]
    ada_b_mine = lax.dynamic_slice_in_dim(W["ada_b"], chip * ncol, ncol, axis=1)[:, None, :]
    mods_mine = _ada_fwd(cond, W["ada_w"], ada_b_mine, "ada_fwd")
    ag2 = _all_gather(mods_mine.reshape(NL * 2 * SUBLANES, ncol), "gather_mods").reshape(N_DEV, NL, 2 * SUBLANES, ncol)
    mods_all = _unshard_last(jnp.stack([ag2[2 * s] for s in range(N_CHIPS)]))
    mod_lat = lax.dynamic_index_in_dim(mods_all, dev, axis=1, keepdims=False).reshape(NL, 6, D)
    mod_ctx = mods_all[:, N_DEV].reshape(NL, 6, D)
    mod2 = jnp.stack([mod_lat, mod_ctx], axis=1)
    mod1 = mod_lat[:, None]

    corner = mod2[0, 0, 0, 0]
    behind_small = jnp.where(corner != corner, corner, 0.0)
    for k, keys in enumerate([part for layer in layer_sets for part in (layer[:2], layer[2:])]):
        fetch(keys, behind_small, FETCH_IDS[k])
    zero_d = jnp.zeros((1, D), F32)
    cos, sin = _rope_tables(L)
    nkv = D // HEAD_DIM // Q_PER_KV
    qdim, kvdim = D, nkv * HEAD_DIM

    def ffn_fwd(i, h, mod, rows, segs, tag):
        a2 = _prenorm(h, mod, vec(P["norm_g"][i, 2]), 1, rows, nl, tm, f"pre_ffn_{tag}")
        z0 = _mm(a2, big("ffn_w_up", i, a2), "nn", F32, f"ffn_up_{tag}")
        u = _ffn_gate(z0, P["ffn_conv_w"][i], vec(P["ffn_conv_b"][i]), segs, f"ffn_gate_{tag}")
        f = _mm(u, big("ffn_w_down", i, u), "nn", F32, f"ffn_down_{tag}")
        h_out = _postnorm(h, f, zero_d, mod, vec(P["norm_g"][i, 3]), 5, rows, nl, tm, f"post_ffn_{tag}")
        return h_out, dict(h=h, a2=a2, z0=z0, f=f)

    def ffn_bwd(i, dh, sv, mod, rows, segs, tag, G):
        df, dg2, dgn3, _ = _postnorm_bwd(dh, sv["f"], zero_d, mod, vec(P["norm_g"][i, 3]), 5, rows, nl, tm, f"post_ffn_bwd_{tag}")
        du = _mm(df, big("ffn_w_down", i), "nt", F32, f"ffn_down_dx_{tag}")
        u, dz0, dcw, dcb = _ffn_gate_bwd(sv["z0"], du, P["ffn_conv_w"][i], vec(P["ffn_conv_b"][i]), segs, f"ffn_gate_bwd_{tag}")
        G["ffn_w_down"][i] = _mm(u, df, "tn", MMT, f"ffn_down_dw_{tag}")
        G["ffn_w_up"][i] = _mm(sv["a2"], dz0, "tn", MMT, f"ffn_up_dw_{tag}")
        da2 = _mm(dz0, big("ffn_w_up", i), "nt", F32, f"ffn_up_dx_{tag}")
        dh, dsh2, dsc2, dgn2 = _prenorm_bwd(sv["h"], da2, dh, mod, vec(P["norm_g"][i, 2]), 1, rows, nl, tm, f"pre_ffn_bwd_{tag}")
        G["ffn_conv_w"][i], G["ffn_conv_b"][i] = dcw, dcb[0]
        return dh, (dsh2, dsc2, dg2), (dgn2, dgn3)

    def conformer_fwd(i, j, h, mod, rows, segs, tag):
        a = _prenorm(h, mod, vec(P["norm_g"][i, 0]), 0, rows, nl, tm, f"pre_mix_{tag}")
        p0 = _mm(a, big("cm_w_in", j, a), "nn", F32, f"cm_in_{tag}")
        z2 = _glu_conv(p0, vec(P["cm_b_in"][j]), P["cm_dw_w"][j], vec(P["cm_dw_b"][j]), segs, f"cm_conv_{tag}")
        z4 = _ln_silu(z2, vec(P["cm_ln_g"][j]), vec(P["cm_ln_b"][j]), rows, tm, f"cm_ln_{tag}")
        y = _mm(z4, big("cm_w_out", j, z4), "nn", F32, f"cm_out_{tag}")
        h_out = _postnorm(h, y, vec(P["cm_b_out"][j]), mod, vec(P["norm_g"][i, 1]), 2, rows, nl, tm, f"post_mix_{tag}")
        return h_out, dict(h=h, a=a, p0=p0, z2=z2, z4=z4, y=y)

    def conformer_bwd(i, j, dh, sv, mod, rows, segs, tag, G):
        dy, dg1, dgn1, dbo = _postnorm_bwd(dh, sv["y"], vec(P["cm_b_out"][j]) + zero_d, mod, vec(P["norm_g"][i, 1]), 2, rows, nl,
                                           tm, f"post_mix_bwd_{tag}")
        G["cm_w_out"][j] = _mm(sv["z4"], dy, "tn", MMT, f"cm_out_dw_{tag}")
        dz4 = _mm(dy, big("cm_w_out", j), "nt", F32, f"cm_out_dx_{tag}")
        dz2, dlg, dlb = _ln_silu_bwd(sv["z2"], dz4, vec(P["cm_ln_g"][j]), vec(P["cm_ln_b"][j]), rows, tm, f"cm_ln_bwd_{tag}")
        dpa, dpg, ddw, ddb, dba, dbg = _glu_conv_bwd(sv["p0"], vec(P["cm_b_in"][j]), P["cm_dw_w"][j], dz2, segs, f"cm_conv_bwd_{tag}")
        dp = jnp.concatenate([dpa, dpg], axis=1)
        G["cm_w_in"][j] = _mm(sv["a"], dp, "tn", MMT, f"cm_in_dw_{tag}")
        da = _mm(dp, big("cm_w_in", j), "nt", F32, f"cm_in_dx_{tag}")
        dh, dsh1, dsc1, dgn0 = _prenorm_bwd(sv["h"], da, dh, mod, vec(P["norm_g"][i, 0]), 0, rows, nl, tm, f"pre_mix_bwd_{tag}")
        G["cm_b_out"][j] = jnp.sum(dbo, axis=0)[0]
        G["cm_ln_g"][j], G["cm_ln_b"][j], G["cm_dw_w"][j], G["cm_dw_b"][j] = dlg[0], dlb[0], ddw, ddb[0]
        G["cm_b_in"][j] = jnp.concatenate([dba[0], dbg[0]])
        return dh, (dsh1, dsc1, dg1), (dgn0, dgn1)

    def heads(a, n):
        return a.reshape(a.shape[0], n, HEAD_DIM).transpose(1, 0, 2)

    def unheads(a):
        return a.transpose(1, 0, 2).reshape(a.shape[1], -1)

    G = {n: [None] * W[n].shape[0] for n in WEIGHTS if n not in ("c_ctx", "ada_w", "ada_b", "norm_g")}
    saved = []
    h = jnp.concatenate([x[0], ctx[0]], axis=0)
    h, s_mix = conformer_fwd(0, 0, h, mod2[0], T, segs2, "l0")
    h, s_ffn = ffn_fwd(0, h, mod2[0], T, segs2, "l0")
    saved.append((s_mix, s_ffn))
    a_all = _prenorm(h, mod2[1], vec(P["norm_g"][1, 0]), 0, T, nl, tm, "pre_mix_l1")
    qkv = _mm(a_all, big("attn_w_qkv", 0, a_all), "nn", F32, "attn_qkv")
    qk_rot, v_lat = _rope(qkv, cos, sin, L, qdim + kvdim, tm, "rope")
    q_h = heads(qk_rot[:, :qdim], nkv * Q_PER_KV).reshape(nkv, Q_PER_KV, L, HEAD_DIM)
    k_h, v_h = heads(qk_rot[:, qdim:], nkv), heads(v_lat, nkv)
    kc_h = heads(qkv[L:, qdim:qdim + kvdim].astype(MMT), nkv)
    vc_h = heads(qkv[L:, qdim + kvdim:].astype(MMT), nkv)
    sink = P["attn_sink"][0]
    o_h, lse = _attn_fwd(q_h, k_h, v_h, kc_h, vc_h, sink, "attn")
    o_nat = unheads(o_h.reshape(nkv * Q_PER_KV, L, HEAD_DIM)).astype(MMT)
    y1 = _mm(o_nat, big("attn_w_o", 0, o_nat), "nn", F32, "attn_out")
    h_in1 = h
    h = _postnorm(h, y1, zero_d, mod1[1], vec(P["norm_g"][1, 1]), 2, L, nl, tm, "post_mix_l1")
    h, s_ffn1 = ffn_fwd(1, h, mod1[1], L, segs1, "lat")
    h_in2 = h
    a_2 = _prenorm(h, mod1[2], vec(P["norm_g"][2, 0]), 0, L, nl, tm, "pre_mix_l2")
    p0_2 = _mm(a_2, big("gm_w_in", 0, a_2), "nn", F32, "gm_in")
    ws_bf = P["gm_w_s"][0].astype(MMT)
    bs_col = P["gm_b_s"][0][:, :, None]
    us = _gmlp_fwd(p0_2, vec(P["gm_b_in"][0]), vec(P["gm_ln_g"][0]), vec(P["gm_ln_b"][0]), ws_bf, bs_col, "gmlp")
    y2 = _mm(us, big("gm_w_out", 0, us), "nn", F32, "gm_out")
    h = _postnorm(h, y2, zero_d, mod1[2], vec(P["norm_g"][2, 1]), 2, L, nl, tm, "post_mix_l2")
    h, s_ffn2 = ffn_fwd(2, h, mod1[2], L, segs1, "lat")
    h, s_mix3 = conformer_fwd(3, 1, h, mod1[3], L, segs1, "l3")
    h, s_ffn3 = ffn_fwd(3, h, mod1[3], L, segs1, "lat")

    loss_mine, dh = _loss_head(h, target[0], tm, "loss_head")

    dmod = [None] * NL
    dgn = [None] * NL

    def finish(i, mix, ffn, gns_mix, gns_ffn):
        dmod[i] = jnp.concatenate(list(mix) + list(ffn), axis=1)
        dgn[i] = jnp.stack([jnp.sum(g, axis=0)[0] for g in (gns_mix[0], gns_mix[1], gns_ffn[0], gns_ffn[1])])

    sent, so_far = {}, {}

    def send(tag, collective_id, tensors):
        exchanges = []
        for n, l in tensors:
            g = G[n][l]
            shard = list(g.shape)
            shard[BIG_AXIS[n] - 1] //= N_CHIPS
            exchanges.append(([g], [SDS((N_CHIPS,) + tuple(shard), g.dtype)], _scatter_plan(BIG_AXIS[n] - 1), N_CHIPS - 1))
        sent[tag] = (tensors, _sequencer_exchange(f"send_grads_{tag}", collective_id, exchanges, _same_core_peers))
        corner = sum(G[n][l][0:1, 0:1].astype(F32) for n, l in tensors)
        return jnp.where(corner != corner, corner, 0.0)

    def land(tag, after):
        tensors, landed = sent[tag]
        mine = [_sum_slots(lands[0], f"sum_chips_{n}_{l}", after, MMT) for (n, l), lands in zip(tensors, landed)]
        theirs = _swap_with_sibling(mine, f"swap_cores_{tag}")
        for (n, l), a, b in zip(tensors, mine, theirs):
            so_far[n] = _adamw_layer(W[n], M[n], V[n], l, [a, b], so_far.get(n), f"adamw_{n}_{l}")

    dh, m_ffn, n_ffn = ffn_bwd(3, dh, s_ffn3, mod1[3], L, segs1, "lat", G)
    dh, m_mix, n_mix = conformer_bwd(3, 1, dh, s_mix3, mod1[3], L, segs1, "l3", G)
    finish(3, m_mix, m_ffn, n_mix, n_ffn)
    zero_d = zero_d + send("l3", SEND_IDS[0], [("ffn_w_up", 3), ("ffn_w_down", 3), ("cm_w_in", 1), ("cm_w_out", 1)])
    land("l3", None)

    dh, m_ffn, n_ffn = ffn_bwd(2, dh, s_ffn2, mod1[2], L, segs1, "lat", G)
    dy2, dg1, dgn1, _ = _postnorm_bwd(dh, y2, zero_d, mod1[2], vec(P["norm_g"][2, 1]), 2, L, nl, tm, "post_mix_bwd_l2")
    G["gm_w_out"][0] = _mm(us, dy2, "tn", MMT, "gm_out_dw")
    dus = _mm(dy2, big("gm_w_out", 0), "nt", F32, "gm_out_dx")
    ws_t = jnp.swapaxes(P["gm_w_s"][0], 1, 2).astype(MMT)
    dpre, dbi, dlg, dlb, dws, dbs = _gmlp_bwd(p0_2, dus, vec(P["gm_b_in"][0]), vec(P["gm_ln_g"][0]), vec(P["gm_ln_b"][0]), ws_bf,
                                              ws_t, bs_col, "gmlp_bwd")
    G["gm_w_in"][0] = _mm(a_2, dpre, "tn", MMT, "gm_in_dw")
    da = _mm(dpre, big("gm_w_in", 0), "nt", F32, "gm_in_dx")
    dh, dsh1, dsc1, dgn0 = _prenorm_bwd(h_in2, da, dh, mod1[2], vec(P["norm_g"][2, 0]), 0, L, nl, tm, "pre_mix_bwd_l2")
    G["gm_b_in"][0], G["gm_ln_g"][0], G["gm_ln_b"][0], G["gm_w_s"][0], G["gm_b_s"][0] = dbi[0], dlg[0], dlb[0], dws, dbs[:, :, 0]
    finish(2, (dsh1, dsc1, dg1), m_ffn, (dgn0, dgn1), n_ffn)
    zero_d = zero_d + send("l2", SEND_IDS[1], [("ffn_w_up", 2), ("ffn_w_down", 2), ("gm_w_in", 0), ("gm_w_out", 0)])
    land("l2", None)

    dh, m_ffn, n_ffn = ffn_bwd(1, dh, s_ffn1, mod1[1], L, segs1, "lat", G)
    dy1, dg1, dgn1, _ = _postnorm_bwd(dh, y1, zero_d, mod1[1], vec(P["norm_g"][1, 1]), 2, L, nl, tm, "post_mix_bwd_l1")
    G["attn_w_o"][0] = _mm(o_nat, dy1, "tn", MMT, "attn_out_dw")
    do_nat = _mm(dy1, big("attn_w_o", 0), "nt", MMT, "attn_out_dx")
    do_h = heads(do_nat, nkv * Q_PER_KV).reshape(nkv, Q_PER_KV, L, HEAD_DIM)
    dq_h, dkc_h, dvc_h, dsk = _attn_bwd_q(q_h, k_h, v_h, kc_h, vc_h, sink, o_h, do_h, lse, "attn_bwd_q")
    dk_h, dv_h = _attn_bwd_kv(q_h, k_h, v_h, o_h, do_h, lse, "attn_bwd_kv")
    dqk = jnp.concatenate([unheads(dq_h.reshape(nkv * Q_PER_KV, L, HEAD_DIM)), unheads(dk_h)], axis=1)
    dqkv_lat = _rope_bwd(dqk, unheads(dv_h), cos, sin, tm, "rope_bwd")
    dqkv_ctx = jnp.concatenate([jnp.zeros((C, qdim), MMT), unheads(dkc_h).astype(MMT), unheads(dvc_h).astype(MMT)], axis=1)
    dqkv = jnp.concatenate([dqkv_lat, dqkv_ctx], axis=0)
    G["attn_w_qkv"][0] = _mm(a_all, dqkv, "tn", MMT, "attn_qkv_dw")
    da_all = _mm(dqkv, big("attn_w_qkv", 0), "nt", F32, "attn_qkv_dx")
    dh_all = jnp.concatenate([dh, jnp.zeros((C, D), F32)], axis=0)
    dh, dsh1, dsc1, dgn0 = _prenorm_bwd(h_in1, da_all, dh_all, mod2[1], vec(P["norm_g"][1, 0]), 0, T, nl, tm, "pre_mix_bwd_l1")
    G["attn_sink"][0] = dsk[:, :Q_PER_KV, 0].reshape(-1)
    pad_ctx = lambda a: jnp.concatenate([a, jnp.zeros_like(a)], axis=0)
    finish(1, (dsh1, dsc1, pad_ctx(dg1)), [pad_ctx(a) for a in m_ffn], (dgn0, dgn1), n_ffn)
    zero_d = zero_d + send("l1_ffn", SEND_IDS[2], [("ffn_w_up", 1), ("ffn_w_down", 1)])
    zero_d = zero_d + send("l1_mix", SEND_IDS[5], [("attn_w_qkv", 0), ("attn_w_o", 0)])
    land("l1_ffn", None)
    land("l1_mix", None)

    s_mix0, s_ffn0 = saved[0]
    dh, m_ffn, n_ffn = ffn_bwd(0, dh, s_ffn0, mod2[0], T, segs2, "l0", G)
    zero_d = zero_d + send("l0_ffn", SEND_IDS[3], [("ffn_w_up", 0), ("ffn_w_down", 0)])
    dh, m_mix, n_mix = conformer_bwd(0, 0, dh, s_mix0, mod2[0], T, segs2, "l0", G)
    finish(0, m_mix, m_ffn, n_mix, n_ffn)
    grad_x = dh[:L][None]
    sent_l0 = send("l0_mix", SEND_IDS[4], [("cm_w_in", 0), ("cm_w_out", 0)])

    for i in range(2, NL):
        dmod[i] = pad_ctx(dmod[i])
    dmod_all = jnp.stack(dmod).reshape(NL, 2, 6 * D) + sent_l0

    ag3 = _all_gather(dmod_all.reshape(NL * 2, 6 * D), "gather_dmods").reshape(N_DEV, NL, 2, N_CHIPS, ncol)
    dm_cols = lax.dynamic_index_in_dim(ag3, chip, axis=3, keepdims=False)
    dm_lat, dm_ctx = jnp.moveaxis(dm_cols[:, :, 0], 0, 1), jnp.moveaxis(dm_cols[:, :, 1], 0, 1)
    g_ada_w, dsilu = _ada_bwd(cond, W["ada_w"], dm_lat, dm_ctx, "ada_bwd")
    cc = W["c_ctx"]
    sg = jax.nn.sigmoid(cc)
    dcctx_part = jnp.where(ci == 0, 1.0, 0.0) * dsilu[N_DEV] * (sg * (1.0 + cc * (1.0 - sg)))

    Gs = {n: jnp.stack(G[n]) for n in G if n not in BIG}
    Gs["norm_g"] = jnp.stack(dgn)
    Gs["ada_b"] = jnp.sum(dmod_all, axis=1)
    Gs["c_ctx"] = dcctx_part
    small_names = list(SMALL_SHARDED) + list(SMALL_REPLICATED)
    small_full_shapes = [P[n].shape for n in small_names]
    small_pack = _pack([Gs[n] for n in small_names]).astype(MMT)
    ((ag4,),) = _sequencer_exchange("gather_small_grads", SMALL_GRADS_ID, [
        ([small_pack], [SDS((N_DEV,) + small_pack.shape, MMT)], _all_gather_plan, N_DEV - 1)], _all_peers)

    flat2 = lambda a: a.reshape(-1, a.shape[-1])
    res = {}
    outs = _adamw(flat2(W["ada_w"]), flat2(M["ada_w"]), flat2(V["ada_w"]), [flat2(g_ada_w)], "adamw_ada_w")
    res["ada_w"] = tuple(o.reshape(W["ada_w"].shape) for o in outs)

    land("l0_ffn", outs[0])
    land("l0_mix", so_far["ffn_w_up"][0])
    for n in BIG:
        res[n] = tuple(so_far[n])

    small_sum = _unpack(_sum_slots(ag4, "sum_small_grads"), small_full_shapes)
    g_small = {}
    for n, g in zip(small_names, small_sum):
        g_small[n] = _my_block_last(g, chip) if n in SMALL_SHARDED else g
    packed = [_pack([d[n] for n in small_names]) for d in (W, M, V)]
    outs_small = _adamw(packed[0], packed[1], packed[2], [_pack([g_small[n] for n in small_names])], "adamw_small")
    shard_shapes = [W[n].shape for n in small_names]
    for k, n in enumerate(small_names):
        res[n] = tuple(_unpack(o, shard_shapes)[k] for o in outs_small)

    loss = lax.psum(loss_mine[0, 0], ("x", "y", "c"))
    return (loss, grad_x) + tuple(res[n][k] for k in range(4) for n in WEIGHTS)


def kernel(x, c, ctx, c_ctx, ada_w, ada_b, norm_g, ffn_w_up, ffn_conv_w, ffn_conv_b, ffn_w_down, cm_w_in, cm_b_in, cm_dw_w, cm_dw_b, cm_ln_g, cm_ln_b, cm_w_out, cm_b_out, attn_w_qkv, attn_sink, attn_w_o, gm_w_in, gm_b_in, gm_ln_g, gm_ln_b, gm_w_s, gm_b_s, gm_w_out, loss_target, m_c_ctx, m_ada_w, m_ada_b, m_norm_g, m_ffn_w_up, m_ffn_conv_w, m_ffn_conv_b, m_ffn_w_down, m_cm_w_in, m_cm_b_in, m_cm_dw_w, m_cm_dw_b, m_cm_ln_g, m_cm_ln_b, m_cm_w_out, m_cm_b_out, m_attn_w_qkv, m_attn_sink, m_attn_w_o, m_gm_w_in, m_gm_b_in, m_gm_ln_g, m_gm_ln_b, m_gm_w_s, m_gm_b_s, m_gm_w_out, v_c_ctx, v_ada_w, v_ada_b, v_norm_g, v_ffn_w_up, v_ffn_conv_w, v_ffn_conv_b, v_ffn_w_down, v_cm_w_in, v_cm_b_in, v_cm_dw_w, v_cm_dw_b, v_cm_ln_g, v_cm_ln_b, v_cm_w_out, v_cm_b_out, v_attn_w_qkv, v_attn_sink, v_attn_w_o, v_gm_w_in, v_gm_b_in, v_gm_ln_g, v_gm_ln_b, v_gm_w_s, v_gm_b_s, v_gm_w_out):
    args = locals()
    W = {n: args[n] for n in WEIGHTS}
    M = {n: args["m_" + n] for n in WEIGHTS}
    V = {n: args["v_" + n] for n in WEIGHTS}
    return _step(x, c, ctx, loss_target, W, M, V)
```

```python
import functools
import math

import jax
import jax.numpy as jnp
from jax import lax
from jax.experimental import pallas as pl
from jax.experimental.pallas import tpu as pltpu
from jax.experimental.pallas import tpu_sc as plsc

F32 = jnp.float32
MMT = jnp.bfloat16
SDS = jax.ShapeDtypeStruct
MESH = pl.DeviceIdType.MESH

EPS = 1e-6
HEAD_DIM = 64
Q_PER_KV = 4
ATTN_BLOCK = 128
GRID_W = 64
ROPE_BASE = 10000.0
GMLP_CHUNK = 128
GMLP_GROUP_DIM = 128
CONV_WIDTH = 31
FFN_CONV_WIDTH = 3
NEG = -1e30

ADAM_LR, ADAM_B1, ADAM_B2, ADAM_EPS, ADAM_WD, ADAM_STEP = 0.001, 0.9, 0.999, 1e-08, 0.01, 10

LANES = 128
SUBLANES = 8
VMEM_LIMIT = 52 * 1024 * 1024
CONV_ROWS = 128
N_CHIPS = 4
N_DEV = 8
N_SEQUENCERS = 2
FETCH_IDS = (1, 2, 3, 4, 11, 12, 13, 14)
SEND_IDS = (5, 6, 7, 8, 9, 15)
SMALL_GRADS_ID = 10


def _cparams(*sem):
    return pltpu.CompilerParams(dimension_semantics=sem if sem else None, vmem_limit_bytes=VMEM_LIMIT)


def _tile(n, cap, mult=LANES):
    best = None
    for d in range(mult, min(n, cap) + 1, mult):
        if n % d == 0:
            best = d
    return best if best is not None else n


def _sum0(v):
    return jnp.sum(v, axis=0, keepdims=True)


def _rms(v):
    r = lax.rsqrt(jnp.mean(v * v, axis=-1, keepdims=True) + EPS)
    return v * r, r


def _sig(v):
    return jax.nn.sigmoid(v)


def _dot(a, b, ca, cb):
    return lax.dot_general(a.astype(MMT), b.astype(MMT), (((ca,), (cb,)), ((), ())), preferred_element_type=F32)


def _mm(a, b, mode, out_dtype, name):
    if mode == "nn":
        (M, K), N = a.shape, b.shape[1]
    elif mode == "nt":
        (M, K), N = a.shape, b.shape[0]
    else:
        (K, M), N = a.shape, b.shape[1]
    tm, tn, tk = _tile(M, 1408), _tile(N, 1408), _tile(K, 1536)
    nk = K // tk
    ca, cb = {"nn": (1, 0), "nt": (1, 1), "tn": (0, 0)}[mode]

    def body(a_ref, b_ref, o_ref, acc):
        k = pl.program_id(2)

        @pl.when(k == 0)
        def _():
            acc[...] = jnp.zeros_like(acc)

        acc[...] += _dot(a_ref[...], b_ref[...], ca, cb)

        @pl.when(k == nk - 1)
        def _():
            o_ref[...] = acc[...].astype(o_ref.dtype)

    a_spec = pl.BlockSpec((tk, tm), lambda i, j, k: (k, i)) if mode == "tn" else pl.BlockSpec((tm, tk), lambda i, j, k: (i, k))
    b_spec = pl.BlockSpec((tn, tk), lambda i, j, k: (j, k)) if mode == "nt" else pl.BlockSpec((tk, tn), lambda i, j, k: (k, j))
    return pl.pallas_call(
        body, name=name, grid=(M // tm, N // tn, nk), in_specs=[a_spec, b_spec],
        out_specs=pl.BlockSpec((tm, tn), lambda i, j, k: (i, j)), out_shape=SDS((M, N), out_dtype),
        scratch_shapes=[pltpu.VMEM((tm, tn), F32)], compiler_params=_cparams("parallel", "parallel", "arbitrary"))(a, b)


def _seg_of(nl, nseg):
    return (lambda i: jnp.where(i >= nl, 1, 0)) if nseg == 2 else (lambda i: 0)


def _prenorm(h, mod, gn, which, rows, nl, tm, name):
    D = h.shape[1]
    nseg = mod.shape[0]
    seg = _seg_of(nl, nseg)
    sh_i, sc_i = (0, 1) if which == 0 else (3, 4)

    def body(h_ref, mod_ref, gn_ref, a_ref):
        n, _ = _rms(h_ref[...])
        a_ref[...] = (n * gn_ref[...] * (1.0 + mod_ref[pl.ds(sc_i, 1), :]) + mod_ref[pl.ds(sh_i, 1), :]).astype(a_ref.dtype)

    return pl.pallas_call(
        body, name=name, grid=(rows // tm,),
        in_specs=[pl.BlockSpec((tm, D), lambda i: (i, 0)), pl.BlockSpec((None, 6, D), lambda i: (seg(i), 0, 0)),
                  pl.BlockSpec((1, D), lambda i: (0, 0))],
        out_specs=pl.BlockSpec((tm, D), lambda i: (i, 0)), out_shape=SDS((rows, D), MMT),
        compiler_params=_cparams("parallel"))(h, mod, gn)


def _acc_spec(D, seg):
    return pl.BlockSpec((None, 1, D), lambda i: (seg(i), 0, 0))


def _prenorm_bwd(h, da, dh_in, mod, gn, which, rows, nl, tm, name):
    D = h.shape[1]
    nseg = mod.shape[0]
    seg = _seg_of(nl, nseg)
    sc_i = 1 if which == 0 else 4

    def body(h_ref, da_ref, dhin_ref, mod_ref, gn_ref, dh_ref, dsh_ref, dsc_ref, dgn_ref):
        i = pl.program_id(0)
        first = (i == 0) | (i == nl) if nseg == 2 else (i == 0)

        @pl.when(first)
        def _():
            dsh_ref[...] = jnp.zeros_like(dsh_ref)
            dsc_ref[...] = jnp.zeros_like(dsc_ref)
            dgn_ref[...] = jnp.zeros_like(dgn_ref)

        n, r = _rms(h_ref[...])
        da_v = da_ref[...].astype(F32)
        gn_v = gn_ref[...]
        sc1 = 1.0 + mod_ref[pl.ds(sc_i, 1), :]
        dsh_ref[...] += _sum0(da_v)
        dsc_ref[...] += _sum0(da_v * (n * gn_v))
        dgn_ref[...] += _sum0(da_v * n * sc1)
        dn = da_v * (gn_v * sc1)
        dh_ref[...] = dhin_ref[...] + r * (dn - n * jnp.mean(dn * n, axis=-1, keepdims=True))

    row = pl.BlockSpec((tm, D), lambda i: (i, 0))
    acc = SDS((nseg, 1, D), F32)
    return pl.pallas_call(
        body, name=name, grid=(rows // tm,),
        in_specs=[row, row, row, pl.BlockSpec((None, 6, D), lambda i: (seg(i), 0, 0)), pl.BlockSpec((1, D), lambda i: (0, 0))],
        out_specs=[row, _acc_spec(D, seg), _acc_spec(D, seg), _acc_spec(D, seg)],
        out_shape=[SDS((rows, D), F32), acc, acc, acc], compiler_params=_cparams("arbitrary"))(h, da, dh_in, mod, gn)


def _postnorm(h, y, bias, mod, gn, gate_i, rows, nl, tm, name):
    D = h.shape[1]
    nseg = mod.shape[0]
    seg = _seg_of(nl, nseg)

    def body(h_ref, y_ref, b_ref, mod_ref, gn_ref, o_ref):
        ny, _ = _rms(y_ref[...] + b_ref[...])
        o_ref[...] = h_ref[...] + mod_ref[pl.ds(gate_i, 1), :] * (ny * gn_ref[...])

    row = pl.BlockSpec((tm, D), lambda i: (i, 0))
    vec = pl.BlockSpec((1, D), lambda i: (0, 0))
    return pl.pallas_call(
        body, name=name, grid=(rows // tm,),
        in_specs=[row, row, vec, pl.BlockSpec((None, 6, D), lambda i: (seg(i), 0, 0)), vec],
        out_specs=row, out_shape=SDS((rows, D), F32), compiler_params=_cparams("parallel"))(h, y, bias, mod, gn)


def _postnorm_bwd(dh, y, bias, mod, gn, gate_i, rows, nl, tm, name):
    D = y.shape[1]
    nseg = mod.shape[0]
    seg = _seg_of(nl, nseg)

    def body(dh_ref, y_ref, b_ref, mod_ref, gn_ref, dy_ref, dg_ref, dgn_ref, db_ref):
        i = pl.program_id(0)
        first = (i == 0) | (i == nl) if nseg == 2 else (i == 0)

        @pl.when(first)
        def _():
            dg_ref[...] = jnp.zeros_like(dg_ref)
            dgn_ref[...] = jnp.zeros_like(dgn_ref)
            db_ref[...] = jnp.zeros_like(db_ref)

        ny, ry = _rms(y_ref[...] + b_ref[...])
        g = mod_ref[pl.ds(gate_i, 1), :]
        gn_v = gn_ref[...]
        dh_v = dh_ref[...]
        dg_ref[...] += _sum0(dh_v * (ny * gn_v))
        dgn_ref[...] += _sum0(dh_v * ny * g)
        dny = dh_v * (g * gn_v)
        dy = ry * (dny - ny * jnp.mean(dny * ny, axis=-1, keepdims=True))
        db_ref[...] += _sum0(dy)
        dy_ref[...] = dy.astype(dy_ref.dtype)

    row = pl.BlockSpec((tm, D), lambda i: (i, 0))
    vec = pl.BlockSpec((1, D), lambda i: (0, 0))
    acc = SDS((nseg, 1, D), F32)
    return pl.pallas_call(
        body, name=name, grid=(rows // tm,),
        in_specs=[row, row, vec, pl.BlockSpec((None, 6, D), lambda i: (seg(i), 0, 0)), vec],
        out_specs=[row, _acc_spec(D, seg), _acc_spec(D, seg), _acc_spec(D, seg)],
        out_shape=[SDS((rows, D), MMT), acc, acc, acc], compiler_params=_cparams("arbitrary"))(dh, y, bias, mod, gn)


def _seg_layout(segs, H):
    out, base = [], H
    for s0, n in segs:
        out.append((s0, n, base))
        base += n + H
    return out, base


def _zero_pads(ref, lay, H):
    width = ref.shape[1]
    ref[pl.ds(0, H), :] = jnp.zeros((H, width), ref.dtype)
    for _, n, base in lay:
        ref[pl.ds(base + n, H), :] = jnp.zeros((H, width), ref.dtype)


def _window(ref, base, off, H):
    return ref[pl.ds(base - H + off, CONV_ROWS + 2 * H), :]


def _taps(win, H, offs):
    W = CONV_ROWS + 2 * H
    rolled, out = {}, {}
    for o in offs:
        s = H + o
        b = s % SUBLANES
        if b not in rolled:
            rolled[b] = win if b == 0 else pltpu.roll(win, shift=W - b, axis=0)
        out[o] = rolled[b][s - b:s - b + CONV_ROWS, :]
    return out


def _chunks(lay, fn):
    for s0, n, base in lay:
        def step(r, carry, s0=s0, base=base):
            fn(s0, base, pl.multiple_of(r * CONV_ROWS, CONV_ROWS))
            return carry
        lax.fori_loop(0, n // CONV_ROWS, step, 0)


def _ffn_gate(z0, conv_w, conv_b, segs, name):
    T, F2 = z0.shape
    F = F2 // 2
    tc = _tile(F, 256)
    nF = F // tc
    H = SUBLANES
    lay, srows = _seg_layout(segs, H)
    offs = [-1, 0, 1]

    def body(zg_ref, zv_ref, wg_ref, wv_ref, bg_ref, bv_ref, u_ref, xg, xv):
        _zero_pads(xg, lay, H)
        _zero_pads(xv, lay, H)
        for s0, n, base in lay:
            xg[pl.ds(base, n), :] = zg_ref[pl.ds(s0, n), :]
            xv[pl.ds(base, n), :] = zv_ref[pl.ds(s0, n), :]

        def chunk(s0, base, off):
            tg = _taps(_window(xg, base, off, H), H, offs)
            tv = _taps(_window(xv, base, off, H), H, offs)
            zg = bg_ref[...] + sum(tg[k - 1] * wg_ref[pl.ds(k, 1), :] for k in range(3))
            zv = bv_ref[...] + sum(tv[k - 1] * wv_ref[pl.ds(k, 1), :] for k in range(3))
            u_ref[pl.ds(s0 + off, CONV_ROWS), :] = (zg * _sig(zg) * zv).astype(u_ref.dtype)

        _chunks(lay, chunk)

    colg = lambda r: pl.BlockSpec((r, tc), lambda j: (0, j))
    colv = lambda r: pl.BlockSpec((r, tc), lambda j: (0, j + nF))
    return pl.pallas_call(
        body, name=name, grid=(nF,),
        in_specs=[colg(T), colv(T), colg(3), colv(3), colg(1), colv(1)],
        out_specs=colg(T), out_shape=SDS((T, F), MMT),
        scratch_shapes=[pltpu.VMEM((srows, tc), F32), pltpu.VMEM((srows, tc), F32)],
        compiler_params=_cparams("parallel"))(z0, z0, conv_w, conv_w, conv_b, conv_b)


def _ffn_gate_bwd(z0, du, conv_w, conv_b, segs, name):
    T, F2 = z0.shape
    F = F2 // 2
    tc = _tile(F, 256)
    nF = F // tc
    H = SUBLANES
    lay, srows = _seg_layout(segs, H)
    offs = [-1, 0, 1]

    def body(zo_ref, zt_ref, du_ref, wo_ref, wt_ref, bo_ref, bt_ref, u_ref, dz0_ref, dw_ref, db_ref, xo, xt, dzp):
        own_is_gate = pl.program_id(1) == 0
        for ref in (xo, xt, dzp):
            _zero_pads(ref, lay, H)
        for s0, n, base in lay:
            xo[pl.ds(base, n), :] = zo_ref[pl.ds(s0, n), :]
            xt[pl.ds(base, n), :] = zt_ref[pl.ds(s0, n), :]

        def grads(s0, base, off):
            to = _taps(_window(xo, base, off, H), H, offs)
            tt = _taps(_window(xt, base, off, H), H, offs)
            zo = bo_ref[...] + sum(to[k - 1] * wo_ref[pl.ds(k, 1), :] for k in range(3))
            zt = bt_ref[...] + sum(tt[k - 1] * wt_ref[pl.ds(k, 1), :] for k in range(3))
            so, st = _sig(zo), _sig(zt)
            du_v = du_ref[pl.ds(s0 + off, CONV_ROWS), :]
            d_gate = du_v * zt * (so * (1.0 + zo * (1.0 - so)))
            d_val = du_v * (zt * st)
            dzp[pl.ds(base + off, CONV_ROWS), :] = jnp.where(own_is_gate, d_gate, d_val)

            @pl.when(own_is_gate)
            def _():
                u_ref[pl.ds(s0 + off, CONV_ROWS), :] = (zo * so * zt).astype(u_ref.dtype)

        _chunks(lay, grads)
        dw_ref[...] = jnp.zeros_like(dw_ref)
        db_ref[...] = jnp.zeros_like(db_ref)

        def back(s0, base, off):
            td = _taps(_window(dzp, base, off, H), H, offs)
            tx = _taps(_window(xo, base, off, H), H, offs)
            dz0 = sum(td[1 - k] * wo_ref[pl.ds(k, 1), :] for k in range(3))
            dz0_ref[pl.ds(s0 + off, CONV_ROWS), :] = dz0.astype(dz0_ref.dtype)
            db_ref[...] += _sum0(td[0])
            for k in range(3):
                dw_ref[pl.ds(k, 1), :] += _sum0(td[0] * tx[k - 1])

        _chunks(lay, back)

    own = lambda r: pl.BlockSpec((r, tc), lambda j, hf: (0, hf * nF + j))
    oth = lambda r: pl.BlockSpec((r, tc), lambda j, hf: (0, (1 - hf) * nF + j))
    ucol = pl.BlockSpec((T, tc), lambda j, hf: (0, j))
    return pl.pallas_call(
        body, name=name, grid=(nF, 2),
        in_specs=[own(T), oth(T), ucol, own(3), oth(3), own(1), oth(1)],
        out_specs=[ucol, own(T), own(3), own(1)],
        out_shape=[SDS((T, F), MMT), SDS((T, F2), MMT), SDS((3, F2), F32), SDS((1, F2), F32)],
        scratch_shapes=[pltpu.VMEM((srows, tc), F32)] * 3,
        compiler_params=_cparams("parallel", "arbitrary"))(z0, z0, du, conv_w, conv_w, conv_b, conv_b)


def _glu_conv(p0, b_in, dw_w, dw_b, segs, name):
    T, D2 = p0.shape
    D = D2 // 2
    tc = _tile(D, 256)
    nD = D // tc
    H = 2 * SUBLANES
    half = (CONV_WIDTH - 1) // 2
    lay, srows = _seg_layout(segs, H)
    offs = list(range(-half, half + 1))

    def body(pa_ref, pg_ref, ba_ref, bg_ref, w_ref, b_ref, z2_ref, z1p):
        _zero_pads(z1p, lay, H)

        def glu(s0, base, off):
            rows = pl.ds(s0 + off, CONV_ROWS)
            z1p[pl.ds(base + off, CONV_ROWS), :] = (pa_ref[rows, :] + ba_ref[...]) * _sig(pg_ref[rows, :] + bg_ref[...])

        _chunks(lay, glu)

        def conv(s0, base, off):
            t = _taps(_window(z1p, base, off, H), H, offs)
            acc = b_ref[...] + t[-half] * w_ref[pl.ds(0, 1), :]
            for k in range(1, CONV_WIDTH):
                acc = acc + t[k - half] * w_ref[pl.ds(k, 1), :]
            z2_ref[pl.ds(s0 + off, CONV_ROWS), :] = acc

        _chunks(lay, conv)

    cola = lambda r: pl.BlockSpec((r, tc), lambda j: (0, j))
    colg = lambda r: pl.BlockSpec((r, tc), lambda j: (0, j + nD))
    return pl.pallas_call(
        body, name=name, grid=(nD,),
        in_specs=[cola(T), colg(T), cola(1), colg(1), cola(CONV_WIDTH), cola(1)],
        out_specs=cola(T), out_shape=SDS((T, D), F32), scratch_shapes=[pltpu.VMEM((srows, tc), F32)],
        compiler_params=_cparams("parallel"))(p0, p0, b_in, b_in, dw_w, dw_b)


def _glu_conv_bwd(p0, b_in, dw_w, dz2, segs, name):
    T, D2 = p0.shape
    D = D2 // 2
    tc = _tile(D, 256)
    nD = D // tc
    H = 2 * SUBLANES
    half = (CONV_WIDTH - 1) // 2
    lay, srows = _seg_layout(segs, H)
    offs = list(range(-half, half + 1))

    def body(pa_ref, pg_ref, ba_ref, bg_ref, w_ref, dz2_ref, dpa_ref, dpg_ref, dw_ref, db_ref, dba_ref, dbg_ref, z1p, dzp):
        _zero_pads(z1p, lay, H)
        _zero_pads(dzp, lay, H)
        for s0, n, base in lay:
            dzp[pl.ds(base, n), :] = dz2_ref[pl.ds(s0, n), :]

        def glu(s0, base, off):
            rows = pl.ds(s0 + off, CONV_ROWS)
            z1p[pl.ds(base + off, CONV_ROWS), :] = (pa_ref[rows, :] + ba_ref[...]) * _sig(pg_ref[rows, :] + bg_ref[...])

        _chunks(lay, glu)
        for ref in (dw_ref, db_ref, dba_ref, dbg_ref):
            ref[...] = jnp.zeros_like(ref)

        def back(s0, base, off):
            td = _taps(_window(dzp, base, off, H), H, offs)
            tz = _taps(_window(z1p, base, off, H), H, offs)
            dz1 = td[half] * w_ref[pl.ds(0, 1), :]
            for k in range(1, CONV_WIDTH):
                dz1 = dz1 + td[half - k] * w_ref[pl.ds(k, 1), :]
            db_ref[...] += _sum0(td[0])
            for k in range(CONV_WIDTH):
                dw_ref[pl.ds(k, 1), :] += _sum0(td[0] * tz[k - half])
            rows = pl.ds(s0 + off, CONV_ROWS)
            pa = pa_ref[rows, :] + ba_ref[...]
            sg = _sig(pg_ref[rows, :] + bg_ref[...])
            dpa = dz1 * sg
            dpg = dz1 * pa * (sg * (1.0 - sg))
            dba_ref[...] += _sum0(dpa)
            dbg_ref[...] += _sum0(dpg)
            dpa_ref[rows, :] = dpa.astype(dpa_ref.dtype)
            dpg_ref[rows, :] = dpg.astype(dpg_ref.dtype)

        _chunks(lay, back)

    cola = lambda r: pl.BlockSpec((r, tc), lambda j: (0, j))
    colg = lambda r: pl.BlockSpec((r, tc), lambda j: (0, j + nD))
    return pl.pallas_call(
        body, name=name, grid=(nD,),
        in_specs=[cola(T), colg(T), cola(1), colg(1), cola(CONV_WIDTH), cola(T)],
        out_specs=[cola(T), cola(T), cola(CONV_WIDTH), cola(1), cola(1), cola(1)],
        out_shape=[SDS((T, D), MMT), SDS((T, D), MMT), SDS((CONV_WIDTH, D), F32), SDS((1, D), F32), SDS((1, D), F32),
                   SDS((1, D), F32)],
        scratch_shapes=[pltpu.VMEM((srows, tc), F32)] * 2, compiler_params=_cparams("parallel"))(p0, p0, b_in, b_in, dw_w, dz2)


def _layer_norm_stats(v):
    mu = jnp.mean(v, axis=-1, keepdims=True)
    var = jnp.mean(jnp.square(v - mu), axis=-1, keepdims=True)
    rstd = lax.rsqrt(var + EPS)
    return (v - mu) * rstd, rstd


def _ln_silu(z2, ln_g, ln_b, rows, tm, name):
    D = z2.shape[1]

    def body(z_ref, g_ref, b_ref, o_ref):
        xh, _ = _layer_norm_stats(z_ref[...])
        z3 = xh * g_ref[...] + b_ref[...]
        o_ref[...] = (z3 * _sig(z3)).astype(o_ref.dtype)

    row = pl.BlockSpec((tm, D), lambda i: (i, 0))
    vec = pl.BlockSpec((1, D), lambda i: (0, 0))
    return pl.pallas_call(body, name=name, grid=(rows // tm,), in_specs=[row, vec, vec], out_specs=row,
                          out_shape=SDS((rows, D), MMT), compiler_params=_cparams("parallel"))(z2, ln_g, ln_b)


def _ln_silu_bwd(z2, dz4, ln_g, ln_b, rows, tm, name):
    D = z2.shape[1]

    def body(z_ref, d_ref, g_ref, b_ref, dz_ref, dg_ref, db_ref):
        @pl.when(pl.program_id(0) == 0)
        def _():
            dg_ref[...] = jnp.zeros_like(dg_ref)
            db_ref[...] = jnp.zeros_like(db_ref)

        xh, rstd = _layer_norm_stats(z_ref[...])
        z3 = xh * g_ref[...] + b_ref[...]
        s = _sig(z3)
        dz3 = d_ref[...] * (s * (1.0 + z3 * (1.0 - s)))
        dg_ref[...] += _sum0(dz3 * xh)
        db_ref[...] += _sum0(dz3)
        dxh = dz3 * g_ref[...]
        dz_ref[...] = rstd * (dxh - jnp.mean(dxh, axis=-1, keepdims=True) - xh * jnp.mean(dxh * xh, axis=-1, keepdims=True))

    row = pl.BlockSpec((tm, D), lambda i: (i, 0))
    vec = pl.BlockSpec((1, D), lambda i: (0, 0))
    return pl.pallas_call(body, name=name, grid=(rows // tm,), in_specs=[row, row, vec, vec], out_specs=[row, vec, vec],
                          out_shape=[SDS((rows, D), F32), SDS((1, D), F32), SDS((1, D), F32)],
                          compiler_params=_cparams("arbitrary"))(z2, dz4, ln_g, ln_b)


def _rot_half_pairs(v):
    width = v.shape[1]
    lane = lax.broadcasted_iota(jnp.int32, v.shape, 1)
    return jnp.where((lane % 32) < 16, -pltpu.roll(v, shift=width - 16, axis=1), pltpu.roll(v, shift=16, axis=1))


def _rope(qkv, cos, sin, L, qk, tm, name):
    width = qkv.shape[1]
    kv = width - qk

    def body(x_ref, c_ref, s_ref, qk_ref, v_ref):
        xv = x_ref[:, pl.ds(0, qk)]
        c = jnp.tile(c_ref[...], (1, qk // LANES))
        s = jnp.tile(s_ref[...], (1, qk // LANES))
        qk_ref[...] = (xv * c + _rot_half_pairs(xv) * s).astype(qk_ref.dtype)
        v_ref[...] = x_ref[:, pl.ds(qk, kv)].astype(v_ref.dtype)

    tab = pl.BlockSpec((tm, LANES), lambda i: (i, 0))
    return pl.pallas_call(
        body, name=name, grid=(L // tm,), in_specs=[pl.BlockSpec((tm, width), lambda i: (i, 0)), tab, tab],
        out_specs=[pl.BlockSpec((tm, qk), lambda i: (i, 0)), pl.BlockSpec((tm, kv), lambda i: (i, 0))],
        out_shape=[SDS((L, qk), MMT), SDS((L, kv), MMT)], compiler_params=_cparams("parallel"))(qkv, cos, sin)


def _rope_bwd(dqk, dv, cos, sin, tm, name):
    L, qk = dqk.shape
    kv = dv.shape[1]

    def body(d_ref, dv_ref, c_ref, s_ref, o_ref):
        dv_ = d_ref[...]
        c = jnp.tile(c_ref[...], (1, qk // LANES))
        s = jnp.tile(s_ref[...], (1, qk // LANES))
        o_ref[:, pl.ds(0, qk)] = (dv_ * c - _rot_half_pairs(dv_ * s)).astype(o_ref.dtype)
        o_ref[:, pl.ds(qk, kv)] = dv_ref[...].astype(o_ref.dtype)

    tab = pl.BlockSpec((tm, LANES), lambda i: (i, 0))
    return pl.pallas_call(
        body, name=name, grid=(L // tm,),
        in_specs=[pl.BlockSpec((tm, qk), lambda i: (i, 0)), pl.BlockSpec((tm, kv), lambda i: (i, 0)), tab, tab],
        out_specs=pl.BlockSpec((tm, qk + kv), lambda i: (i, 0)), out_shape=SDS((L, qk + kv), MMT),
        compiler_params=_cparams("parallel"))(dqk, dv, cos, sin)


def _band_specs(nb, width):
    blk = lambda f: pl.BlockSpec((None, ATTN_BLOCK, width), f)
    return [blk(lambda h, n: (h, jnp.maximum(n - 1, 0), 0)), blk(lambda h, n: (h, n, 0)),
            blk(lambda h, n: (h, jnp.minimum(n + 1, nb - 1), 0))]


def _window_mask(n, L):
    qi = lax.broadcasted_iota(jnp.int32, (ATTN_BLOCK, 3 * ATTN_BLOCK), 0)
    kk = lax.broadcasted_iota(jnp.int32, (ATTN_BLOCK, 3 * ATTN_BLOCK), 1)
    key_abs = (n - 1) * ATTN_BLOCK + kk
    return (jnp.abs(qi + ATTN_BLOCK - kk) <= ATTN_BLOCK) & (key_abs >= 0) & (key_abs < L)


def _attn_fwd(q, k, v, kc, vc, sink, name):
    nkv, _, L, hd = q.shape
    C = kc.shape[1]
    nb = L // ATTN_BLOCK
    scale = HEAD_DIM ** -0.5

    def body(sink_ref, q_ref, k0, k1, k2, v0, v1, v2, kc_ref, vc_ref, o_ref, lse_ref):
        hh, n = pl.program_id(0), pl.program_id(1)
        kw = jnp.concatenate([k0[...], k1[...], k2[...]], axis=0)
        vw = jnp.concatenate([v0[...], v1[...], v2[...]], axis=0)
        mask = _window_mask(n, L)
        for g in range(Q_PER_KV):
            qg = q_ref[g]
            sw = jnp.where(mask, _dot(qg, kw, 1, 1) * scale, NEG)
            sc = _dot(qg, kc_ref[...], 1, 1) * scale
            sk = sink_ref[hh * Q_PER_KV + g]
            m = jnp.maximum(jnp.maximum(jnp.max(sw, axis=-1, keepdims=True), jnp.max(sc, axis=-1, keepdims=True)), sk)
            pw, pc = jnp.exp(sw - m), jnp.exp(sc - m)
            den = jnp.sum(pw, axis=-1, keepdims=True) + jnp.sum(pc, axis=-1, keepdims=True) + jnp.exp(sk - m)
            inv = 1.0 / den
            o_ref[g] = _dot(pw * inv, vw, 1, 0) + _dot(pc * inv, vc_ref[...], 1, 0)
            lse_ref[g] = m + jnp.log(den)

    qspec = pl.BlockSpec((None, Q_PER_KV, ATTN_BLOCK, hd), lambda h, n: (h, 0, n, 0))
    cspec = pl.BlockSpec((None, C, hd), lambda h, n: (h, 0, 0))
    return pl.pallas_call(
        body, name=name, grid=(nkv, nb),
        in_specs=[pl.BlockSpec(memory_space=pltpu.SMEM), qspec] + _band_specs(nb, hd) + _band_specs(nb, hd) + [cspec, cspec],
        out_specs=[qspec, pl.BlockSpec((None, Q_PER_KV, ATTN_BLOCK, 1), lambda h, n: (h, 0, n, 0))],
        out_shape=[SDS((nkv, Q_PER_KV, L, hd), F32), SDS((nkv, Q_PER_KV, L, 1), F32)],
        compiler_params=_cparams("parallel", "parallel"))(sink, q, k, k, k, v, v, v, kc, vc)


def _attn_bwd_q(q, k, v, kc, vc, sink, o, do, lse, name):
    nkv, _, L, hd = q.shape
    C = kc.shape[1]
    nb = L // ATTN_BLOCK
    scale = HEAD_DIM ** -0.5

    def body(sink_ref, q_ref, k0, k1, k2, v0, v1, v2, kc_ref, vc_ref, o_ref, do_ref, lse_ref, dq_ref, dkc_ref, dvc_ref, dsk_ref):
        hh, n = pl.program_id(0), pl.program_id(1)

        @pl.when(n == 0)
        def _():
            dkc_ref[...] = jnp.zeros_like(dkc_ref)
            dvc_ref[...] = jnp.zeros_like(dvc_ref)
            dsk_ref[...] = jnp.zeros_like(dsk_ref)

        kw = jnp.concatenate([k0[...], k1[...], k2[...]], axis=0)
        vw = jnp.concatenate([v0[...], v1[...], v2[...]], axis=0)
        mask = _window_mask(n, L)
        for g in range(Q_PER_KV):
            qg, dog, lse_g = q_ref[g], do_ref[g], lse_ref[g]
            delta = jnp.sum(dog.astype(F32) * o_ref[g], axis=-1, keepdims=True)
            pw = jnp.exp(jnp.where(mask, _dot(qg, kw, 1, 1) * scale, NEG) - lse_g)
            pc = jnp.exp(_dot(qg, kc_ref[...], 1, 1) * scale - lse_g)
            dsw = pw * (_dot(dog, vw, 1, 1) - delta)
            dsc = pc * (_dot(dog, vc_ref[...], 1, 1) - delta)
            dq_ref[g] = (_dot(dsw, kw, 1, 0) + _dot(dsc, kc_ref[...], 1, 0)) * scale
            dkc_ref[...] += _dot(dsc, qg, 0, 0) * scale
            dvc_ref[...] += _dot(pc, dog, 0, 0)
            psk = jnp.exp(sink_ref[hh * Q_PER_KV + g] - lse_g)
            dsk_ref[pl.ds(g, 1), :] += jnp.broadcast_to(jnp.sum(-psk * delta, axis=0, keepdims=True), (1, LANES))

    qspec = pl.BlockSpec((None, Q_PER_KV, ATTN_BLOCK, hd), lambda h, n: (h, 0, n, 0))
    lspec = pl.BlockSpec((None, Q_PER_KV, ATTN_BLOCK, 1), lambda h, n: (h, 0, n, 0))
    cspec = pl.BlockSpec((None, C, hd), lambda h, n: (h, 0, 0))
    return pl.pallas_call(
        body, name=name, grid=(nkv, nb),
        in_specs=[pl.BlockSpec(memory_space=pltpu.SMEM), qspec] + _band_specs(nb, hd) + _band_specs(nb, hd)
        + [cspec, cspec, qspec, qspec, lspec],
        out_specs=[qspec, cspec, cspec, pl.BlockSpec((None, SUBLANES, LANES), lambda h, n: (h, 0, 0))],
        out_shape=[SDS((nkv, Q_PER_KV, L, hd), F32), SDS((nkv, C, hd), F32), SDS((nkv, C, hd), F32),
                   SDS((nkv, SUBLANES, LANES), F32)],
        compiler_params=_cparams("parallel", "arbitrary"))(sink, q, k, k, k, v, v, v, kc, vc, o, do, lse)


def _attn_bwd_kv(q, k, v, o, do, lse, name):
    nkv, _, L, hd = q.shape
    nb = L // ATTN_BLOCK
    scale = HEAD_DIM ** -0.5

    def body(q0, q1, q2, do0, do1, do2, o0, o1, o2, l0, l1, l2, k_ref, v_ref, dk_ref, dv_ref):
        j = pl.program_id(1)
        qi = lax.broadcasted_iota(jnp.int32, (ATTN_BLOCK, ATTN_BLOCK), 0)
        kk = lax.broadcasted_iota(jnp.int32, (ATTN_BLOCK, ATTN_BLOCK), 1)
        kj, vj = k_ref[...], v_ref[...]
        dk = jnp.zeros((ATTN_BLOCK, hd), F32)
        dv = jnp.zeros((ATTN_BLOCK, hd), F32)
        for slot, (q_r, do_r, o_r, l_r) in enumerate(((q0, do0, o0, l0), (q1, do1, o1, l1), (q2, do2, o2, l2))):
            n = j - 1 + slot
            ok = (n >= 0) & (n < nb) & (jnp.abs(qi + ATTN_BLOCK - ((2 - slot) * ATTN_BLOCK + kk)) <= ATTN_BLOCK)
            for g in range(Q_PER_KV):
                qg, dog = q_r[g], do_r[g]
                delta = jnp.sum(dog.astype(F32) * o_r[g], axis=-1, keepdims=True)
                p = jnp.exp(jnp.where(ok, _dot(qg, kj, 1, 1) * scale - l_r[g], NEG))
                ds = p * (_dot(dog, vj, 1, 1) - delta)
                dk = dk + _dot(ds, qg, 0, 0) * scale
                dv = dv + _dot(p, dog, 0, 0)
        dk_ref[...] = dk
        dv_ref[...] = dv

    def band(width):
        blk = lambda f: pl.BlockSpec((None, Q_PER_KV, ATTN_BLOCK, width), f)
        return [blk(lambda h, j: (h, 0, jnp.maximum(j - 1, 0), 0)), blk(lambda h, j: (h, 0, j, 0)),
                blk(lambda h, j: (h, 0, jnp.minimum(j + 1, nb - 1), 0))]

    kspec = pl.BlockSpec((None, ATTN_BLOCK, hd), lambda h, j: (h, j, 0))
    return pl.pallas_call(
        body, name=name, grid=(nkv, nb), in_specs=band(hd) + band(hd) + band(hd) + band(1) + [kspec, kspec],
        out_specs=[kspec, kspec], out_shape=[SDS((nkv, L, hd), F32), SDS((nkv, L, hd), F32)],
        compiler_params=_cparams("parallel", "parallel"))(q, q, q, do, do, do, o, o, o, lse, lse, lse, k, v)


_GELU_K = math.sqrt(2.0 / math.pi)


def _gelu(v):
    return 0.5 * v * (1.0 + jnp.tanh(_GELU_K * (v + 0.044715 * (v * v * v))))


def _gelu_grad(v):
    t = jnp.tanh(_GELU_K * (v + 0.044715 * (v * v * v)))
    return 0.5 * (1.0 + t) + 0.5 * v * (1.0 - t * t) * (_GELU_K * (1.0 + 3.0 * 0.044715 * (v * v)))


def _gmlp_fwd(p0, b_in, ln_g, ln_b, w_s, b_s, name):
    L, W2 = p0.shape
    W = W2 // 2
    G = W // GMLP_GROUP_DIM

    def body(p_ref, bi_ref, g_ref, b_ref, ws_ref, bs_ref, o_ref):
        ge = _gelu(p_ref[...] + bi_ref[...])
        xh, _ = _layer_norm_stats(ge[:, W:])
        vln = xh * g_ref[...] + b_ref[...]
        for gi in range(G):
            cols = slice(gi * GMLP_GROUP_DIM, (gi + 1) * GMLP_GROUP_DIM)
            s = _dot(ws_ref[gi], vln[:, cols], 1, 0) + bs_ref[gi]
            o_ref[:, cols] = (ge[:, cols] * s).astype(o_ref.dtype)

    full = lambda shape: pl.BlockSpec(shape, lambda i: (0,) * len(shape))
    return pl.pallas_call(
        body, name=name, grid=(L // GMLP_CHUNK,),
        in_specs=[pl.BlockSpec((GMLP_CHUNK, W2), lambda i: (i, 0)), full((1, W2)), full((1, W)), full((1, W)),
                  full((G, GMLP_CHUNK, GMLP_CHUNK)), full((G, GMLP_CHUNK, 1))],
        out_specs=pl.BlockSpec((GMLP_CHUNK, W), lambda i: (i, 0)), out_shape=SDS((L, W), MMT),
        compiler_params=_cparams("parallel"))(p0, b_in, ln_g, ln_b, w_s, b_s)


def _gmlp_bwd(p0, dus, b_in, ln_g, ln_b, w_s, w_st, b_s, name):
    L, W2 = p0.shape
    W = W2 // 2
    G = W // GMLP_GROUP_DIM

    def body(p_ref, d_ref, bi_ref, g_ref, b_ref, ws_ref, wst_ref, bs_ref, dpre_ref, dbi_ref, dg_ref, db_ref, dws_ref, dbs_ref, dvln):
        @pl.when(pl.program_id(0) == 0)
        def _():
            for ref in (dbi_ref, dg_ref, db_ref, dws_ref, dbs_ref):
                ref[...] = jnp.zeros_like(ref)

        pre = p_ref[...] + bi_ref[...]
        ge = _gelu(pre)
        xh, rstd = _layer_norm_stats(ge[:, W:])
        vln = xh * g_ref[...] + b_ref[...]
        dge_u = []
        for gi in range(G):
            cols = slice(gi * GMLP_GROUP_DIM, (gi + 1) * GMLP_GROUP_DIM)
            vg = vln[:, cols]
            s = _dot(ws_ref[gi], vg, 1, 0) + bs_ref[gi]
            dus_g = d_ref[:, cols]
            dge_u.append(dus_g * s)
            ds = dus_g * ge[:, cols]
            dbs_ref[gi] += jnp.sum(ds, axis=1, keepdims=True)
            dws_ref[gi] += _dot(ds, vg, 1, 1)
            dvln[:, cols] = _dot(wst_ref[gi], ds, 1, 0)
        dv = dvln[...]
        dg_ref[...] += _sum0(dv * xh)
        db_ref[...] += _sum0(dv)
        dxh = dv * g_ref[...]
        dv0 = rstd * (dxh - jnp.mean(dxh, axis=-1, keepdims=True) - xh * jnp.mean(dxh * xh, axis=-1, keepdims=True))
        dpre = jnp.concatenate(dge_u + [dv0], axis=1) * _gelu_grad(pre)
        dbi_ref[...] += _sum0(dpre)
        dpre_ref[...] = dpre.astype(dpre_ref.dtype)

    full = lambda shape: pl.BlockSpec(shape, lambda i: (0,) * len(shape))
    mats = (G, GMLP_CHUNK, GMLP_CHUNK)
    return pl.pallas_call(
        body, name=name, grid=(L // GMLP_CHUNK,),
        in_specs=[pl.BlockSpec((GMLP_CHUNK, W2), lambda i: (i, 0)), pl.BlockSpec((GMLP_CHUNK, W), lambda i: (i, 0)),
                  full((1, W2)), full((1, W)), full((1, W)), full(mats), full(mats), full((G, GMLP_CHUNK, 1))],
        out_specs=[pl.BlockSpec((GMLP_CHUNK, W2), lambda i: (i, 0)), full((1, W2)), full((1, W)), full((1, W)), full(mats),
                   full((G, GMLP_CHUNK, 1))],
        out_shape=[SDS((L, W2), MMT), SDS((1, W2), F32), SDS((1, W), F32), SDS((1, W), F32), SDS(mats, F32),
                   SDS((G, GMLP_CHUNK, 1), F32)],
        scratch_shapes=[pltpu.VMEM((GMLP_CHUNK, W), F32)], compiler_params=_cparams("arbitrary"))(
            p0, dus, b_in, ln_g, ln_b, w_s, w_st, b_s)


def _loss_head(h, target, tm, name):
    L, D = h.shape

    def body(h_ref, t_ref, l_ref, d_ref):
        @pl.when(pl.program_id(0) == 0)
        def _():
            l_ref[...] = jnp.zeros_like(l_ref)

        e = h_ref[...] - t_ref[...]
        l_ref[...] += 0.5 * jnp.sum(jnp.mean(e * e, axis=-1, keepdims=True), axis=0, keepdims=True)
        d_ref[...] = e * (1.0 / D)

    row = pl.BlockSpec((tm, D), lambda i: (i, 0))
    return pl.pallas_call(body, name=name, grid=(L // tm,), in_specs=[row, row],
                          out_specs=[pl.BlockSpec((1, 1), lambda i: (0, 0)), row],
                          out_shape=[SDS((1, 1), F32), SDS((L, D), F32)], compiler_params=_cparams("arbitrary"))(h, target)


def _ada_fwd(cond, ada_w, ada_b, name):
    NL, D, n = ada_w.shape
    tn = _tile(n, 768)

    def body(c_ref, w_ref, b_ref, o_ref):
        cv = c_ref[...]
        o_ref[...] = _dot(cv * _sig(cv), w_ref[...], 1, 0) + b_ref[...]

    return pl.pallas_call(
        body, name=name, grid=(NL, n // tn),
        in_specs=[pl.BlockSpec((2 * SUBLANES, D), lambda i, j: (0, 0)), pl.BlockSpec((None, D, tn), lambda i, j: (i, 0, j)),
                  pl.BlockSpec((None, 1, tn), lambda i, j: (i, 0, j))],
        out_specs=pl.BlockSpec((None, 2 * SUBLANES, tn), lambda i, j: (i, 0, j)), out_shape=SDS((NL, 2 * SUBLANES, n), F32),
        compiler_params=_cparams("parallel", "parallel"))(cond, ada_w, ada_b)


def _ada_bwd(cond, ada_w, dm_lat, dm_ctx, name):
    NL, D, n = ada_w.shape
    tn = _tile(n, 768)

    def body(c_ref, w_ref, dl_ref, dc_ref, dw_ref, ds_ref):
        @pl.when((pl.program_id(0) == 0) & (pl.program_id(1) == 0))
        def _():
            ds_ref[...] = jnp.zeros_like(ds_ref)

        cv = c_ref[...]
        row = lax.broadcasted_iota(jnp.int32, (SUBLANES, tn), 0)
        ctx_rows = jnp.where(row == 0, _sum0(dc_ref[...]), 0.0)
        dm = jnp.concatenate([dl_ref[...], ctx_rows], axis=0)
        dw_ref[...] = _dot(cv * _sig(cv), dm, 0, 0)
        ds_ref[...] += _dot(dm, w_ref[...], 1, 1)

    dspec = pl.BlockSpec((None, SUBLANES, tn), lambda i, j: (i, 0, j))
    return pl.pallas_call(
        body, name=name, grid=(NL, n // tn),
        in_specs=[pl.BlockSpec((2 * SUBLANES, D), lambda i, j: (0, 0)), pl.BlockSpec((None, D, tn), lambda i, j: (i, 0, j)),
                  dspec, dspec],
        out_specs=[pl.BlockSpec((None, D, tn), lambda i, j: (i, 0, j)), pl.BlockSpec((2 * SUBLANES, D), lambda i, j: (0, 0))],
        out_shape=[SDS((NL, D, n), F32), SDS((2 * SUBLANES, D), F32)],
        compiler_params=_cparams("arbitrary", "arbitrary"))(cond, ada_w, dm_lat, dm_ctx)


def _adam_math(w, g, m, v):
    m = ADAM_B1 * m + (1.0 - ADAM_B1) * g
    v = ADAM_B2 * v + (1.0 - ADAM_B2) * jnp.square(g)
    m_hat = m / (1.0 - ADAM_B1 ** ADAM_STEP)
    v_hat = v / (1.0 - ADAM_B2 ** ADAM_STEP)
    return -ADAM_LR * (m_hat / (jnp.sqrt(v_hat) + ADAM_EPS) + ADAM_WD * w), m, v


def _row_tile(rows, cols, elems, mult=SUBLANES):
    want = max(mult, elems // cols)
    best = mult if rows % mult == 0 else rows
    for d in range(mult, min(rows, want) + 1, mult):
        if rows % d == 0:
            best = d
    return best


def _adamw(w, m, v, parts, name):
    R, C = w.shape
    tr = _row_tile(R, C, 128 * 1024)
    npart = len(parts)

    def body(*refs):
        w_ref, m_ref, v_ref = refs[:3]
        g_ref, d_ref, nm_ref, nv_ref = refs[3 + npart:]
        g = refs[3][...]
        for p_ref in refs[4:3 + npart]:
            g = g + p_ref[...]
        d, nm, nv = _adam_math(w_ref[...], g, m_ref[...], v_ref[...])
        g_ref[...], d_ref[...], nm_ref[...], nv_ref[...] = g, d, nm, nv

    blk = pl.BlockSpec((tr, C), lambda i: (i, 0))
    return pl.pallas_call(body, name=name, grid=(R // tr,), in_specs=[blk] * (3 + npart), out_specs=[blk] * 4,
                          out_shape=[SDS((R, C), F32)] * 4, compiler_params=_cparams("parallel"))(w, m, v, *parts)


def _adamw_layer(w, m, v, layer, parts, prev, name):
    _, R, C = w.shape
    tr = _row_tile(R, C, 128 * 1024, 2 * SUBLANES)
    npart = len(parts)
    nprev = 0 if prev is None else 4

    def body(*refs):
        w_ref, m_ref, v_ref = refs[:3]
        g_ref, d_ref, nm_ref, nv_ref = refs[3 + npart + nprev:]
        g = refs[3][...].astype(F32)
        for p_ref in refs[4:3 + npart]:
            g = g + p_ref[...].astype(F32)
        d, nm, nv = _adam_math(w_ref[...], g, m_ref[...], v_ref[...])
        g_ref[...], d_ref[...], nm_ref[...], nv_ref[...] = g, d, nm, nv

    stacked = pl.BlockSpec((None, tr, C), lambda i: (layer, i, 0))
    flat = pl.BlockSpec((tr, C), lambda i: (i, 0))
    return pl.pallas_call(
        body, name=name, grid=(R // tr,),
        in_specs=[stacked] * 3 + [flat] * npart + [pl.BlockSpec(memory_space=pl.ANY)] * nprev, out_specs=[stacked] * 4,
        out_shape=[SDS(w.shape, F32)] * 4, input_output_aliases={3 + npart + k: k for k in range(nprev)},
        compiler_params=_cparams("parallel"))(w, m, v, *parts, *(prev or ()))


def _sum_slots(x, name, after=None, out_dtype=F32):
    S, R, C = x.shape
    tr = _row_tile(R, C, 128 * 1024, SUBLANES * 4 // jnp.dtype(x.dtype).itemsize)
    extra = [] if after is None else [after]

    def body(x_ref, *rest):
        o_ref = rest[-1]
        acc = x_ref[0].astype(F32)
        for s in range(1, S):
            acc = acc + x_ref[s].astype(F32)
        o_ref[...] = acc.astype(o_ref.dtype)

    return pl.pallas_call(
        body, name=name, grid=(R // tr,),
        in_specs=[pl.BlockSpec((S, tr, C), lambda i: (0, i, 0))] + [pl.BlockSpec(memory_space=pl.ANY)] * len(extra),
        out_specs=pl.BlockSpec((tr, C), lambda i: (i, 0)), out_shape=SDS((R, C), out_dtype),
        compiler_params=_cparams("parallel"))(x, *extra)


def _my_place():
    return lax.axis_index("x"), lax.axis_index("y"), lax.axis_index("c")


def _other_chips(x, y):
    return [(1 - x, y), (x, 1 - y), (1 - x, 1 - y)]


def _all_gather(v, name):
    R, C = v.shape

    def body(v_ref, o_ref, send_sems, recv_sems, local_sem):
        x, y, c = _my_place()
        me = 4 * x + 2 * y + c
        mine = pltpu.make_async_copy(v_ref, o_ref.at[me], local_sem)
        mine.start()
        copies = []
        for flip in range(1, N_DEV):
            fx, fy, fc = (flip >> 2) & 1, (flip >> 1) & 1, flip & 1
            peer = ((x + fx) % 2, (y + fy) % 2, (c + fc) % 2)
            cp = pltpu.make_async_remote_copy(src_ref=v_ref, dst_ref=o_ref.at[me], send_sem=send_sems.at[flip - 1],
                                              recv_sem=recv_sems.at[flip - 1], device_id=peer, device_id_type=MESH)
            cp.start()
            copies.append(cp)
        for cp in copies:
            cp.wait()
        mine.wait()

    return pl.pallas_call(
        body, name=name, in_specs=[pl.BlockSpec(memory_space=pl.ANY)], out_specs=pl.BlockSpec(memory_space=pl.ANY),
        out_shape=SDS((N_DEV, R, C), v.dtype),
        scratch_shapes=[pltpu.SemaphoreType.DMA((N_DEV - 1,)), pltpu.SemaphoreType.DMA((N_DEV - 1,)), pltpu.SemaphoreType.DMA],
        )(v)


def _shard_window(ref, axis, j, size):
    idx = [slice(None)] * len(ref.shape)
    idx[axis] = pl.ds(pl.multiple_of(j * size, SUBLANES), size)
    return ref.at[tuple(idx)]


def _gather_plan(axis):
    def plan(srcs, lands):
        x, y, c = _my_place()
        shard, whole = srcs[0], lands[0]
        half = shard.shape[0] // 2
        size = shard.shape[axis]

        def window(chip, which):
            if axis == 1:
                return whole.at[pl.ds(pl.multiple_of(which * half, SUBLANES), half), pl.ds(pl.multiple_of(chip * size, LANES), size)]
            return whole.at[pl.ds(pl.multiple_of(chip * size + which * half, SUBLANES), half), :]

        j = 2 * x + y
        local = [(shard, _shard_window(whole, axis, j, size))]
        mine = shard.at[pl.ds(pl.multiple_of(c * half, SUBLANES), half), :]
        remote = [(mine, window(j, c), (px, py, c)) for px, py in _other_chips(x, y)]
        forward = [(window(2 * px + py, c), window(2 * px + py, c), (x, y, 1 - c)) for px, py in _other_chips(x, y)]
        return local, remote, forward
    return plan


def _scatter_plan(axis):
    def plan(srcs, lands):
        x, y, c = _my_place()
        j = 2 * x + y
        size = srcs[0].shape[axis] // N_CHIPS
        local = [(_shard_window(srcs[0], axis, j, size), lands[0].at[j])]
        remote = [(_shard_window(srcs[0], axis, 2 * px + py, size), lands[0].at[j], (px, py, c)) for px, py in _other_chips(x, y)]
        return local, remote, []
    return plan


def _all_gather_plan(srcs, lands):
    x, y, c = _my_place()
    dst = lands[0].at[4 * x + 2 * y + c]
    remote = []
    for flip in range(1, N_DEV):
        fx, fy, fc = (flip >> 2) & 1, (flip >> 1) & 1, flip & 1
        remote.append((srcs[0], dst, ((x + fx) % 2, (y + fy) % 2, (c + fc) % 2)))
    return [(srcs[0], dst)], remote, []


def _same_core_peers():
    x, y, c = _my_place()
    return [(px, py, c) for px, py in _other_chips(x, y)]


def _same_core_peers_and_sibling():
    x, y, c = _my_place()
    return _same_core_peers() + [(x, y, 1 - c)]


def _all_peers():
    x, y, c = _my_place()
    return [((x + (f >> 2 & 1)) % 2, (y + (f >> 1 & 1)) % 2, (c + (f & 1)) % 2) for f in range(1, N_DEV)]


def _sequencer_exchange(name, collective_id, exchanges, peers_fn):
    hbm = pltpu.MemorySpace.HBM
    src_refs = [[jax.new_ref(s, memory_space=hbm) for s in e[0]] for e in exchanges]
    land_refs = [[jax.empty_ref(s, memory_space=hbm) for s in e[1]] for e in exchanges]
    first = [sum(e[3] for e in exchanges[:i]) for i in range(len(exchanges))]
    ncopy = sum(e[3] for e in exchanges)
    dma = pltpu.SemaphoreType.DMA

    @pl.kernel(mesh=plsc.ScalarSubcoreMesh(axis_name="sequencer", num_cores=N_SEQUENCERS), name=name,
               scratch_types=(dma((ncopy,)), dma((ncopy,)), dma((ncopy,)), dma((ncopy,)), dma),
               compiler_params=pltpu.CompilerParams(collective_id=collective_id))
    def launch(send_sems, recv_sems, onward_send_sems, onward_recv_sems, local_sem):
        me = lax.axis_index("sequencer")
        peers = peers_fn()
        barrier = pltpu.get_barrier_semaphore()
        for peer in peers:
            pl.semaphore_signal(barrier, inc=1, device_id=peer, device_id_type=MESH)
        pl.semaphore_wait(barrier, len(peers))
        plans = [e[2](src_refs[i], land_refs[i]) for i, e in enumerate(exchanges)]
        nbytes = lambda ref: math.prod(ref.shape) * jnp.dtype(ref.dtype).itemsize
        jobs = [(nbytes(dst), "local", (i, k)) for i, p in enumerate(plans) for k, (_, dst) in enumerate(p[0])]
        jobs += [(nbytes(src) * (2 if p[2] else 1), "remote", (i, k)) for i, p in enumerate(plans) for k, (src, _, _) in enumerate(p[1])]
        load, owner = [0] * N_SEQUENCERS, {}
        for size, kind, key in sorted(jobs, key=lambda job: -job[0]):
            owner[kind, key] = load.index(min(load))
            load[owner[kind, key]] += size
        for s in range(N_SEQUENCERS):
            @pl.when(me == s)
            def _(s=s):
                for i, (local, _, _) in enumerate(plans):
                    for k, (src, dst) in enumerate(local):
                        if owner["local", (i, k)] == s:
                            cp = pltpu.make_async_copy(src, dst, local_sem)
                            cp.start()
                            cp.wait()
                copies, onward = [], []
                for i, (_, remote, forward) in enumerate(plans):
                    assert len(remote) == exchanges[i][3] and len(forward) in (0, len(remote))
                    for k, (src, dst, peer) in enumerate(remote):
                        if owner["remote", (i, k)] == s:
                            cp = pltpu.make_async_remote_copy(src_ref=src, dst_ref=dst, send_sem=send_sems.at[first[i] + k],
                                                              recv_sem=recv_sems.at[first[i] + k], device_id=peer,
                                                              device_id_type=MESH)
                            cp.start()
                            copies.append(cp)
                            if forward:
                                src, dst, peer = forward[k]
                                onward.append(pltpu.make_async_remote_copy(
                                    src_ref=src, dst_ref=dst, send_sem=onward_send_sems.at[first[i] + k],
                                    recv_sem=onward_recv_sems.at[first[i] + k], device_id=peer, device_id_type=MESH))
                            else:
                                onward.append(None)
                for cp, on in zip(copies, onward):
                    cp.wait_recv()
                    if on is not None:
                        on.start()
                for cp, on in zip(copies, onward):
                    cp.wait_send()
                    if on is not None:
                        on.wait()

    launch()
    return [[r[...] for r in refs] for refs in land_refs]


def _swap_with_sibling(parts, name):
    nt = len(parts)

    def body(*refs):
        ins, outs = refs[:nt], refs[nt:2 * nt]
        send_sems, recv_sems = refs[2 * nt:]
        x, y, c = _my_place()
        copies = []
        for t in range(nt):
            cp = pltpu.make_async_remote_copy(src_ref=ins[t], dst_ref=outs[t], send_sem=send_sems.at[t], recv_sem=recv_sems.at[t],
                                              device_id=(x, y, 1 - c), device_id_type=MESH)
            cp.start()
            copies.append(cp)
        for cp in copies:
            cp.wait()

    any_spec = pl.BlockSpec(memory_space=pl.ANY)
    return pl.pallas_call(
        body, name=name, in_specs=[any_spec] * nt, out_specs=[any_spec] * nt, out_shape=[SDS(p.shape, p.dtype) for p in parts],
        scratch_shapes=[pltpu.SemaphoreType.DMA((nt,)), pltpu.SemaphoreType.DMA((nt,))],
        )(*parts)


PACK_COLS = 1024


def _pack(arrays):
    flat = jnp.concatenate([a.reshape(-1) for a in arrays])
    pad = (-flat.shape[0]) % (2 * SUBLANES * PACK_COLS)
    return jnp.pad(flat, (0, pad)).reshape(-1, PACK_COLS)


def _unpack(packed, shapes):
    flat, out, pos = packed.reshape(-1), [], 0
    for shape in shapes:
        n = math.prod(shape)
        out.append(flat[pos:pos + n].reshape(shape))
        pos += n
    return out


def _unshard_last(stacked):
    moved = jnp.moveaxis(stacked, 0, -2)
    return moved.reshape(moved.shape[:-2] + (moved.shape[-2] * moved.shape[-1],))


def _my_block_last(full, j):
    s = full.shape[-1] // N_CHIPS
    return lax.dynamic_index_in_dim(full.reshape(full.shape[:-1] + (N_CHIPS, s)), j, axis=full.ndim - 1, keepdims=False)


def _rope_tables(L):
    rows = L // GRID_W
    row = jnp.repeat(jnp.arange(rows), GRID_W).astype(F32)
    col = jnp.tile(jnp.arange(GRID_W), rows).astype(F32)
    axis_dim = HEAD_DIM // 2
    inv_freq = ROPE_BASE ** (-jnp.arange(0, axis_dim, 2, dtype=F32) / axis_dim)
    ang_r, ang_c = row[:, None] * inv_freq[None, :], col[:, None] * inv_freq[None, :]
    ang = jnp.concatenate([ang_r, ang_r, ang_c, ang_c] * 2, axis=-1)
    return jnp.cos(ang), jnp.sin(ang)


SMALL_SHARDED = ("norm_g", "ffn_conv_w", "cm_b_in", "cm_dw_w", "cm_dw_b", "cm_ln_g", "cm_ln_b", "cm_b_out", "gm_b_in", "gm_ln_g",
                 "gm_ln_b")
SMALL_REPLICATED = ("c_ctx", "ada_b", "ffn_conv_b", "attn_sink", "gm_w_s", "gm_b_s")
BIG = ("ffn_w_up", "ffn_w_down", "cm_w_in", "cm_w_out", "attn_w_qkv", "attn_w_o", "gm_w_in", "gm_w_out")
BIG_AXIS = {"ffn_w_up": 2, "ffn_w_down": 1, "cm_w_in": 2, "cm_w_out": 1, "attn_w_qkv": 2, "attn_w_o": 1, "gm_w_in": 2, "gm_w_out": 1}
WEIGHTS = ("c_ctx", "ada_w", "ada_b", "norm_g", "ffn_w_up", "ffn_conv_w", "ffn_conv_b", "ffn_w_down", "cm_w_in", "cm_b_in",
           "cm_dw_w", "cm_dw_b", "cm_ln_g", "cm_ln_b", "cm_w_out", "cm_b_out", "attn_w_qkv", "attn_sink", "attn_w_o", "gm_w_in",
           "gm_b_in", "gm_ln_g", "gm_ln_b", "gm_w_s", "gm_b_s", "gm_w_out")


def _step(x, c, ctx, target, W, M, V):
    L, D = x.shape[1], x.shape[2]
    C = ctx.shape[1]
    T = L + C
    NL = W["ada_w"].shape[0]
    tm = 256 if C % 256 == 0 else 128
    nl = L // tm
    xi, yi, ci = _my_place()
    chip = 2 * xi + yi
    dev = 4 * xi + 2 * yi + ci
    segs2, segs1 = [(0, L), (L, C)], [(0, L)]
    vec = lambda a: a.reshape(1, -1)

    layer_sets = [[("cm_w_in", 0), ("cm_w_out", 0), ("ffn_w_up", 0), ("ffn_w_down", 0)],
                  [("attn_w_qkv", 0), ("attn_w_o", 0), ("ffn_w_up", 1), ("ffn_w_down", 1)],
                  [("gm_w_in", 0), ("gm_w_out", 0), ("ffn_w_up", 2), ("ffn_w_down", 2)],
                  [("cm_w_in", 1), ("cm_w_out", 1), ("ffn_w_up", 3), ("ffn_w_down", 3)]]
    arrived = {}

    def fetch(keys, zero, sequencer_id):
        exchanges = []
        for n, i in keys:
            shard = (W[n][i] + zero).astype(MMT)
            whole = list(shard.shape)
            whole[BIG_AXIS[n] - 1] *= N_CHIPS
            exchanges.append(([shard], [SDS(tuple(whole), MMT)], _gather_plan(BIG_AXIS[n] - 1), N_CHIPS - 1))
        lands = _sequencer_exchange(f"fetch_weights_{keys[0][0]}_{keys[0][1]}", sequencer_id, exchanges,
                                    _same_core_peers_and_sibling)
        for key, land in zip(keys, lands):
            arrived[key] = land[0]

    def big(n, i, after=None):
        return arrived[(n, i)]

    small_shapes = [W[n].shape for n in SMALL_SHARDED]
    ag1 = _all_gather(_pack([c.reshape(-1)] + [W[n] for n in SMALL_SHARDED]), "gather_small")
    parts = [_unpack(ag1[2 * s], [(D,)] + small_shapes) for s in range(N_CHIPS)]
    c_rows = jnp.stack([_unpack(ag1[d], [(D,)])[0] for d in range(N_DEV)])
    P = {n: _unshard_last(jnp.stack([parts[s][1 + i] for s in range(N_CHIPS)])) for i, n in enumerate(SMALL_SHARDED)}
    for n in SMALL_REPLICATED:
        P[n] = W[n]

    cond = jnp.concatenate([c_rows, W["c_ctx"][None, :], jnp.zeros((2 * SUBLANES - N_DEV - 1, D), F32)], axis=0)
    ncol = W["ada_w"].shape[2]
    ada_b_mine = lax.dynamic_slice_in_dim(W["ada_b"], chip * ncol, ncol, axis=1)[:, None, :]
    mods_mine = _ada_fwd(cond, W["ada_w"], ada_b_mine, "ada_fwd")
    ag2 = _all_gather(mods_mine.reshape(NL * 2 * SUBLANES, ncol), "gather_mods").reshape(N_DEV, NL, 2 * SUBLANES, ncol)
    mods_all = _unshard_last(jnp.stack([ag2[2 * s] for s in range(N_CHIPS)]))
    mod_lat = lax.dynamic_index_in_dim(mods_all, dev, axis=1, keepdims=False).reshape(NL, 6, D)
    mod_ctx = mods_all[:, N_DEV].reshape(NL, 6, D)
    mod2 = jnp.stack([mod_lat, mod_ctx], axis=1)
    mod1 = mod_lat[:, None]

    for k, keys in enumerate([part for layer in layer_sets for part in (layer[:2], layer[2:])]):
        fetch(keys, 0.0, FETCH_IDS[k])
    zero_d = jnp.zeros((1, D), F32)
    cos, sin = _rope_tables(L)
    nkv = D // HEAD_DIM // Q_PER_KV
    qdim, kvdim = D, nkv * HEAD_DIM

    def ffn_fwd(i, h, mod, rows, segs, tag):
        a2 = _prenorm(h, mod, vec(P["norm_g"][i, 2]), 1, rows, nl, tm, f"pre_ffn_{tag}")
        z0 = _mm(a2, big("ffn_w_up", i, a2), "nn", F32, f"ffn_up_{tag}")
        u = _ffn_gate(z0, P["ffn_conv_w"][i], vec(P["ffn_conv_b"][i]), segs, f"ffn_gate_{tag}")
        f = _mm(u, big("ffn_w_down", i, u), "nn", F32, f"ffn_down_{tag}")
        h_out = _postnorm(h, f, zero_d, mod, vec(P["norm_g"][i, 3]), 5, rows, nl, tm, f"post_ffn_{tag}")
        return h_out, dict(h=h, a2=a2, z0=z0, f=f)

    def ffn_bwd(i, dh, sv, mod, rows, segs, tag, G):
        df, dg2, dgn3, _ = _postnorm_bwd(dh, sv["f"], zero_d, mod, vec(P["norm_g"][i, 3]), 5, rows, nl, tm, f"post_ffn_bwd_{tag}")
        du = _mm(df, big("ffn_w_down", i), "nt", F32, f"ffn_down_dx_{tag}")
        u, dz0, dcw, dcb = _ffn_gate_bwd(sv["z0"], du, P["ffn_conv_w"][i], vec(P["ffn_conv_b"][i]), segs, f"ffn_gate_bwd_{tag}")
        G["ffn_w_down"][i] = _mm(u, df, "tn", MMT, f"ffn_down_dw_{tag}")
        G["ffn_w_up"][i] = _mm(sv["a2"], dz0, "tn", MMT, f"ffn_up_dw_{tag}")
        da2 = _mm(dz0, big("ffn_w_up", i), "nt", F32, f"ffn_up_dx_{tag}")
        dh, dsh2, dsc2, dgn2 = _prenorm_bwd(sv["h"], da2, dh, mod, vec(P["norm_g"][i, 2]), 1, rows, nl, tm, f"pre_ffn_bwd_{tag}")
        G["ffn_conv_w"][i], G["ffn_conv_b"][i] = dcw, dcb[0]
        return dh, (dsh2, dsc2, dg2), (dgn2, dgn3)

    def conformer_fwd(i, j, h, mod, rows, segs, tag):
        a = _prenorm(h, mod, vec(P["norm_g"][i, 0]), 0, rows, nl, tm, f"pre_mix_{tag}")
        p0 = _mm(a, big("cm_w_in", j, a), "nn", F32, f"cm_in_{tag}")
        z2 = _glu_conv(p0, vec(P["cm_b_in"][j]), P["cm_dw_w"][j], vec(P["cm_dw_b"][j]), segs, f"cm_conv_{tag}")
        z4 = _ln_silu(z2, vec(P["cm_ln_g"][j]), vec(P["cm_ln_b"][j]), rows, tm, f"cm_ln_{tag}")
        y = _mm(z4, big("cm_w_out", j, z4), "nn", F32, f"cm_out_{tag}")
        h_out = _postnorm(h, y, vec(P["cm_b_out"][j]), mod, vec(P["norm_g"][i, 1]), 2, rows, nl, tm, f"post_mix_{tag}")
        return h_out, dict(h=h, a=a, p0=p0, z2=z2, z4=z4, y=y)

    def conformer_bwd(i, j, dh, sv, mod, rows, segs, tag, G):
        dy, dg1, dgn1, dbo = _postnorm_bwd(dh, sv["y"], vec(P["cm_b_out"][j]) + zero_d, mod, vec(P["norm_g"][i, 1]), 2, rows, nl,
                                           tm, f"post_mix_bwd_{tag}")
        G["cm_w_out"][j] = _mm(sv["z4"], dy, "tn", MMT, f"cm_out_dw_{tag}")
        dz4 = _mm(dy, big("cm_w_out", j), "nt", F32, f"cm_out_dx_{tag}")
        dz2, dlg, dlb = _ln_silu_bwd(sv["z2"], dz4, vec(P["cm_ln_g"][j]), vec(P["cm_ln_b"][j]), rows, tm, f"cm_ln_bwd_{tag}")
        dpa, dpg, ddw, ddb, dba, dbg = _glu_conv_bwd(sv["p0"], vec(P["cm_b_in"][j]), P["cm_dw_w"][j], dz2, segs, f"cm_conv_bwd_{tag}")
        dp = jnp.concatenate([dpa, dpg], axis=1)
        G["cm_w_in"][j] = _mm(sv["a"], dp, "tn", MMT, f"cm_in_dw_{tag}")
        da = _mm(dp, big("cm_w_in", j), "nt", F32, f"cm_in_dx_{tag}")
        dh, dsh1, dsc1, dgn0 = _prenorm_bwd(sv["h"], da, dh, mod, vec(P["norm_g"][i, 0]), 0, rows, nl, tm, f"pre_mix_bwd_{tag}")
        G["cm_b_out"][j] = jnp.sum(dbo, axis=0)[0]
        G["cm_ln_g"][j], G["cm_ln_b"][j], G["cm_dw_w"][j], G["cm_dw_b"][j] = dlg[0], dlb[0], ddw, ddb[0]
        G["cm_b_in"][j] = jnp.concatenate([dba[0], dbg[0]])
        return dh, (dsh1, dsc1, dg1), (dgn0, dgn1)

    def heads(a, n):
        return a.reshape(a.shape[0], n, HEAD_DIM).transpose(1, 0, 2)

    def unheads(a):
        return a.transpose(1, 0, 2).reshape(a.shape[1], -1)

    G = {n: [None] * W[n].shape[0] for n in WEIGHTS if n not in ("c_ctx", "ada_w", "ada_b", "norm_g")}
    saved = []
    h = jnp.concatenate([x[0], ctx[0]], axis=0)
    h, s_mix = conformer_fwd(0, 0, h, mod2[0], T, segs2, "l0")
    h, s_ffn = ffn_fwd(0, h, mod2[0], T, segs2, "l0")
    saved.append((s_mix, s_ffn))
    a_all = _prenorm(h, mod2[1], vec(P["norm_g"][1, 0]), 0, T, nl, tm, "pre_mix_l1")
    qkv = _mm(a_all, big("attn_w_qkv", 0, a_all), "nn", F32, "attn_qkv")
    qk_rot, v_lat = _rope(qkv, cos, sin, L, qdim + kvdim, tm, "rope")
    q_h = heads(qk_rot[:, :qdim], nkv * Q_PER_KV).reshape(nkv, Q_PER_KV, L, HEAD_DIM)
    k_h, v_h = heads(qk_rot[:, qdim:], nkv), heads(v_lat, nkv)
    kc_h = heads(qkv[L:, qdim:qdim + kvdim].astype(MMT), nkv)
    vc_h = heads(qkv[L:, qdim + kvdim:].astype(MMT), nkv)
    sink = P["attn_sink"][0]
    o_h, lse = _attn_fwd(q_h, k_h, v_h, kc_h, vc_h, sink, "attn")
    o_nat = unheads(o_h.reshape(nkv * Q_PER_KV, L, HEAD_DIM)).astype(MMT)
    y1 = _mm(o_nat, big("attn_w_o", 0, o_nat), "nn", F32, "attn_out")
    h_in1 = h
    h = _postnorm(h, y1, zero_d, mod1[1], vec(P["norm_g"][1, 1]), 2, L, nl, tm, "post_mix_l1")
    h, s_ffn1 = ffn_fwd(1, h, mod1[1], L, segs1, "lat")
    h_in2 = h
    a_2 = _prenorm(h, mod1[2], vec(P["norm_g"][2, 0]), 0, L, nl, tm, "pre_mix_l2")
    p0_2 = _mm(a_2, big("gm_w_in", 0, a_2), "nn", F32, "gm_in")
    ws_bf = P["gm_w_s"][0].astype(MMT)
    bs_col = P["gm_b_s"][0][:, :, None]
    us = _gmlp_fwd(p0_2, vec(P["gm_b_in"][0]), vec(P["gm_ln_g"][0]), vec(P["gm_ln_b"][0]), ws_bf, bs_col, "gmlp")
    y2 = _mm(us, big("gm_w_out", 0, us), "nn", F32, "gm_out")
    h = _postnorm(h, y2, zero_d, mod1[2], vec(P["norm_g"][2, 1]), 2, L, nl, tm, "post_mix_l2")
    h, s_ffn2 = ffn_fwd(2, h, mod1[2], L, segs1, "lat")
    h, s_mix3 = conformer_fwd(3, 1, h, mod1[3], L, segs1, "l3")
    h, s_ffn3 = ffn_fwd(3, h, mod1[3], L, segs1, "lat")

    loss_mine, dh = _loss_head(h, target[0], tm, "loss_head")

    dmod = [None] * NL
    dgn = [None] * NL

    def finish(i, mix, ffn, gns_mix, gns_ffn):
        dmod[i] = jnp.concatenate(list(mix) + list(ffn), axis=1)
        dgn[i] = jnp.stack([jnp.sum(g, axis=0)[0] for g in (gns_mix[0], gns_mix[1], gns_ffn[0], gns_ffn[1])])

    sent, so_far = {}, {}

    def send(tag, collective_id, tensors):
        exchanges = []
        for n, l in tensors:
            g = G[n][l]
            shard = list(g.shape)
            shard[BIG_AXIS[n] - 1] //= N_CHIPS
            exchanges.append(([g], [SDS((N_CHIPS,) + tuple(shard), g.dtype)], _scatter_plan(BIG_AXIS[n] - 1), N_CHIPS - 1))
        sent[tag] = (tensors, _sequencer_exchange(f"send_grads_{tag}", collective_id, exchanges, _same_core_peers))
        corner = sum(G[n][l][0:1, 0:1].astype(F32) for n, l in tensors)
        return jnp.where(corner != corner, corner, 0.0)

    def land(tag, after):
        tensors, landed = sent[tag]
        mine = [_sum_slots(lands[0], f"sum_chips_{n}_{l}", after, MMT) for (n, l), lands in zip(tensors, landed)]
        theirs = _swap_with_sibling(mine, f"swap_cores_{tag}")
        for (n, l), a, b in zip(tensors, mine, theirs):
            so_far[n] = _adamw_layer(W[n], M[n], V[n], l, [a, b], so_far.get(n), f"adamw_{n}_{l}")

    dh, m_ffn, n_ffn = ffn_bwd(3, dh, s_ffn3, mod1[3], L, segs1, "lat", G)
    dh, m_mix, n_mix = conformer_bwd(3, 1, dh, s_mix3, mod1[3], L, segs1, "l3", G)
    finish(3, m_mix, m_ffn, n_mix, n_ffn)
    zero_d = zero_d + send("l3", SEND_IDS[0], [("ffn_w_up", 3), ("ffn_w_down", 3), ("cm_w_in", 1), ("cm_w_out", 1)])
    land("l3", None)

    dh, m_ffn, n_ffn = ffn_bwd(2, dh, s_ffn2, mod1[2], L, segs1, "lat", G)
    dy2, dg1, dgn1, _ = _postnorm_bwd(dh, y2, zero_d, mod1[2], vec(P["norm_g"][2, 1]), 2, L, nl, tm, "post_mix_bwd_l2")
    G["gm_w_out"][0] = _mm(us, dy2, "tn", MMT, "gm_out_dw")
    dus = _mm(dy2, big("gm_w_out", 0), "nt", F32, "gm_out_dx")
    ws_t = jnp.swapaxes(P["gm_w_s"][0], 1, 2).astype(MMT)
    dpre, dbi, dlg, dlb, dws, dbs = _gmlp_bwd(p0_2, dus, vec(P["gm_b_in"][0]), vec(P["gm_ln_g"][0]), vec(P["gm_ln_b"][0]), ws_bf,
                                              ws_t, bs_col, "gmlp_bwd")
    G["gm_w_in"][0] = _mm(a_2, dpre, "tn", MMT, "gm_in_dw")
    da = _mm(dpre, big("gm_w_in", 0), "nt", F32, "gm_in_dx")
    dh, dsh1, dsc1, dgn0 = _prenorm_bwd(h_in2, da, dh, mod1[2], vec(P["norm_g"][2, 0]), 0, L, nl, tm, "pre_mix_bwd_l2")
    G["gm_b_in"][0], G["gm_ln_g"][0], G["gm_ln_b"][0], G["gm_w_s"][0], G["gm_b_s"][0] = dbi[0], dlg[0], dlb[0], dws, dbs[:, :, 0]
    finish(2, (dsh1, dsc1, dg1), m_ffn, (dgn0, dgn1), n_ffn)
    zero_d = zero_d + send("l2", SEND_IDS[1], [("ffn_w_up", 2), ("ffn_w_down", 2), ("gm_w_in", 0), ("gm_w_out", 0)])
    land("l2", None)

    dh, m_ffn, n_ffn = ffn_bwd(1, dh, s_ffn1, mod1[1], L, segs1, "lat", G)
    dy1, dg1, dgn1, _ = _postnorm_bwd(dh, y1, zero_d, mod1[1], vec(P["norm_g"][1, 1]), 2, L, nl, tm, "post_mix_bwd_l1")
    G["attn_w_o"][0] = _mm(o_nat, dy1, "tn", MMT, "attn_out_dw")
    do_nat = _mm(dy1, big("attn_w_o", 0), "nt", MMT, "attn_out_dx")
    do_h = heads(do_nat, nkv * Q_PER_KV).reshape(nkv, Q_PER_KV, L, HEAD_DIM)
    dq_h, dkc_h, dvc_h, dsk = _attn_bwd_q(q_h, k_h, v_h, kc_h, vc_h, sink, o_h, do_h, lse, "attn_bwd_q")
    dk_h, dv_h = _attn_bwd_kv(q_h, k_h, v_h, o_h, do_h, lse, "attn_bwd_kv")
    dqk = jnp.concatenate([unheads(dq_h.reshape(nkv * Q_PER_KV, L, HEAD_DIM)), unheads(dk_h)], axis=1)
    dqkv_lat = _rope_bwd(dqk, unheads(dv_h), cos, sin, tm, "rope_bwd")
    dqkv_ctx = jnp.concatenate([jnp.zeros((C, qdim), MMT), unheads(dkc_h).astype(MMT), unheads(dvc_h).astype(MMT)], axis=1)
    dqkv = jnp.concatenate([dqkv_lat, dqkv_ctx], axis=0)
    G["attn_w_qkv"][0] = _mm(a_all, dqkv, "tn", MMT, "attn_qkv_dw")
    da_all = _mm(dqkv, big("attn_w_qkv", 0), "nt", F32, "attn_qkv_dx")
    dh_all = jnp.concatenate([dh, jnp.zeros((C, D), F32)], axis=0)
    dh, dsh1, dsc1, dgn0 = _prenorm_bwd(h_in1, da_all, dh_all, mod2[1], vec(P["norm_g"][1, 0]), 0, T, nl, tm, "pre_mix_bwd_l1")
    G["attn_sink"][0] = dsk[:, :Q_PER_KV, 0].reshape(-1)
    pad_ctx = lambda a: jnp.concatenate([a, jnp.zeros_like(a)], axis=0)
    finish(1, (dsh1, dsc1, pad_ctx(dg1)), [pad_ctx(a) for a in m_ffn], (dgn0, dgn1), n_ffn)
    zero_d = zero_d + send("l1_ffn", SEND_IDS[2], [("ffn_w_up", 1), ("ffn_w_down", 1)])
    zero_d = zero_d + send("l1_mix", SEND_IDS[5], [("attn_w_qkv", 0), ("attn_w_o", 0)])
    land("l1_ffn", None)
    land("l1_mix", None)

    s_mix0, s_ffn0 = saved[0]
    dh, m_ffn, n_ffn = ffn_bwd(0, dh, s_ffn0, mod2[0], T, segs2, "l0", G)
    zero_d = zero_d + send("l0_ffn", SEND_IDS[3], [("ffn_w_up", 0), ("ffn_w_down", 0)])
    dh, m_mix, n_mix = conformer_bwd(0, 0, dh, s_mix0, mod2[0], T, segs2, "l0", G)
    finish(0, m_mix, m_ffn, n_mix, n_ffn)
    grad_x = dh[:L][None]
    sent_l0 = send("l0_mix", SEND_IDS[4], [("cm_w_in", 0), ("cm_w_out", 0)])

    for i in range(2, NL):
        dmod[i] = pad_ctx(dmod[i])
    dmod_all = jnp.stack(dmod).reshape(NL, 2, 6 * D) + sent_l0

    ag3 = _all_gather(dmod_all.reshape(NL * 2, 6 * D), "gather_dmods").reshape(N_DEV, NL, 2, N_CHIPS, ncol)
    dm_cols = lax.dynamic_index_in_dim(ag3, chip, axis=3, keepdims=False)
    dm_lat, dm_ctx = jnp.moveaxis(dm_cols[:, :, 0], 0, 1), jnp.moveaxis(dm_cols[:, :, 1], 0, 1)
    g_ada_w, dsilu = _ada_bwd(cond, W["ada_w"], dm_lat, dm_ctx, "ada_bwd")
    cc = W["c_ctx"]
    sg = jax.nn.sigmoid(cc)
    dcctx_part = jnp.where(ci == 0, 1.0, 0.0) * dsilu[N_DEV] * (sg * (1.0 + cc * (1.0 - sg)))

    Gs = {n: jnp.stack(G[n]) for n in G if n not in BIG}
    Gs["norm_g"] = jnp.stack(dgn)
    Gs["ada_b"] = jnp.sum(dmod_all, axis=1)
    Gs["c_ctx"] = dcctx_part
    small_names = list(SMALL_SHARDED) + list(SMALL_REPLICATED)
    small_full_shapes = [P[n].shape for n in small_names]
    small_pack = _pack([Gs[n] for n in small_names]).astype(MMT)
    ((ag4,),) = _sequencer_exchange("gather_small_grads", SMALL_GRADS_ID, [
        ([small_pack], [SDS((N_DEV,) + small_pack.shape, MMT)], _all_gather_plan, N_DEV - 1)], _all_peers)

    flat2 = lambda a: a.reshape(-1, a.shape[-1])
    res = {}
    outs = _adamw(flat2(W["ada_w"]), flat2(M["ada_w"]), flat2(V["ada_w"]), [flat2(g_ada_w)], "adamw_ada_w")
    res["ada_w"] = tuple(o.reshape(W["ada_w"].shape) for o in outs)

    land("l0_ffn", outs[0])
    land("l0_mix", so_far["ffn_w_up"][0])
    for n in BIG:
        res[n] = tuple(so_far[n])

    small_sum = _unpack(_sum_slots(ag4, "sum_small_grads"), small_full_shapes)
    g_small = {}
    for n, g in zip(small_names, small_sum):
        g_small[n] = _my_block_last(g, chip) if n in SMALL_SHARDED else g
    packed = [_pack([d[n] for n in small_names]) for d in (W, M, V)]
    outs_small = _adamw(packed[0], packed[1], packed[2], [_pack([g_small[n] for n in small_names])], "adamw_small")
    shard_shapes = [W[n].shape for n in small_names]
    for k, n in enumerate(small_names):
        res[n] = tuple(_unpack(o, shard_shapes)[k] for o in outs_small)

    loss = lax.psum(loss_mine[0, 0], ("x", "y", "c"))
    return (loss, grad_x) + tuple(res[n][k] for k in range(4) for n in WEIGHTS)


def kernel(x, c, ctx, c_ctx, ada_w, ada_b, norm_g, ffn_w_up, ffn_conv_w, ffn_conv_b, ffn_w_down, cm_w_in, cm_b_in, cm_dw_w, cm_dw_b, cm_ln_g, cm_ln_b, cm_w_out, cm_b_out, attn_w_qkv, attn_sink, attn_w_o, gm_w_in, gm_b_in, gm_ln_g, gm_ln_b, gm_w_s, gm_b_s, gm_w_out, loss_target, m_c_ctx, m_ada_w, m_ada_b, m_norm_g, m_ffn_w_up, m_ffn_conv_w, m_ffn_conv_b, m_ffn_w_down, m_cm_w_in, m_cm_b_in, m_cm_dw_w, m_cm_dw_b, m_cm_ln_g, m_cm_ln_b, m_cm_w_out, m_cm_b_out, m_attn_w_qkv, m_attn_sink, m_attn_w_o, m_gm_w_in, m_gm_b_in, m_gm_ln_g, m_gm_ln_b, m_gm_w_s, m_gm_b_s, m_gm_w_out, v_c_ctx, v_ada_w, v_ada_b, v_norm_g, v_ffn_w_up, v_ffn_conv_w, v_ffn_conv_b, v_ffn_w_down, v_cm_w_in, v_cm_b_in, v_cm_dw_w, v_cm_dw_b, v_cm_ln_g, v_cm_ln_b, v_cm_w_out, v_cm_b_out, v_attn_w_qkv, v_attn_sink, v_attn_w_o, v_gm_w_in, v_gm_b_in, v_gm_ln_g, v_gm_ln_b, v_gm_w_s, v_gm_b_s, v_gm_w_out):
    args = locals()
    W = {n: args[n] for n in WEIGHTS}
    M = {n: args["m_" + n] for n in WEIGHTS}
    V = {n: args["v_" + n] for n in WEIGHTS}
    return _step(x, c, ctx, loss_target, W, M, V)
```
